```python
import jax, jax.numpy as jnp
from jax import lax
import numpy as np

D_MODEL = 1024
BATCH = 32
SEQ = 2048
DEPTH = 1

CHUNK = 64
Q_BLOCK = 128
SB_HEADS = 8
SB_HEAD_DIM = 64
SB_WIDTH = SB_HEADS * SB_HEAD_DIM
RW_HEADS = 8
RW_HEAD_DIM = 64
RW_WIDTH = RW_HEADS * RW_HEAD_DIM
W_LORA = 64
A_LORA = 64
G_LORA = 128
GN_EPS = RW_HEAD_DIM * 1e-5
N_BRANCH = 2
SB_COLS = 3 * SB_WIDTH
RW_COLS = 3 * RW_WIDTH + W_LORA + A_LORA + G_LORA
GATE_COLS = N_BRANCH * D_MODEL
IN_COLS = SB_COLS + RW_COLS + GATE_COLS
D_FF = ((8 * D_MODEL + 3 * 256 - 1) // (3 * 256)) * 256
RMS_EPS = 1e-6

kernel_name = "hybrid_stickbreak_rwkv7_swiglu_sandwich"


def rms_norm(x, g):
    xf = x.astype(jnp.float32)
    y = xf * lax.rsqrt(jnp.mean(xf * xf, axis=-1, keepdims=True) + RMS_EPS)
    return (y * g).astype(x.dtype)


def token_shift(p):
    return jnp.pad(p, ((0, 0), (1, 0), (0, 0)))[:, :-1]


def stick_breaking_attention(q, k, v):
    B, S, H, Dh = q.shape
    scale = Dh ** -0.5
    outs = []
    for i in range(S // Q_BLOCK):
        q0 = i * Q_BLOCK
        kv_len = q0 + Q_BLOCK
        qb = q[:, q0:kv_len]
        kb = k[:, :kv_len]
        vb = v[:, :kv_len]
        z = jnp.einsum('bqhd,bkhd->bhqk', qb, kb).astype(jnp.float32) * scale
        t_idx = q0 + jnp.arange(Q_BLOCK)[:, None]
        s_idx = jnp.arange(kv_len)[None, :]
        strict = s_idx < t_idx
        log_fail = jnp.where(strict, jax.nn.log_sigmoid(-z), 0.0)
        after = lax.cumsum(log_fail, axis=3, reverse=True) - log_fail
        w = jnp.where(strict, jnp.exp(jax.nn.log_sigmoid(z) + after), 0.0)
        outs.append(jnp.einsum('bhqk,bkhd->bqhd', w.astype(vb.dtype), vb))
    return jnp.concatenate(outs, axis=1)


def wkv7_scan(r, decay, k, v, kk, a):
    B, S, H, N = r.shape

    def to_chunks(t):
        return t.astype(jnp.float32).transpose(1, 0, 2, 3).reshape(S // CHUNK, CHUNK, B, H, N)

    def step(state, inp):
        r_t, w_t, k_t, v_t, kk_t, a_t = inp
        sa = jnp.einsum('bhvk,bhk->bhv', state, -kk_t)
        state = (state * w_t[:, :, None, :]
                 + sa[..., None] * (kk_t * a_t)[:, :, None, :]
                 + v_t[..., None] * k_t[:, :, None, :])
        return state, jnp.einsum('bhvk,bhk->bhv', state, r_t)

    def chunk_step(state, chunk_inp):
        return lax.scan(step, state, chunk_inp)

    state0 = jnp.zeros((B, H, N, N), jnp.float32)
    _, y = lax.scan(chunk_step, state0, tuple(to_chunks(t) for t in (r, decay, k, v, kk, a)))
    return y.reshape(S, B, H, N).transpose(1, 0, 2, 3)


def rwkv7_time_mix(p, mu, w0, w_up, a0, a_up, g_up, k_k, k_a, r_k, lnx_w, lnx_b):
    B, S, _ = p.shape
    H, N = RW_HEADS, RW_HEAD_DIM
    p = p + (token_shift(p) - p) * mu
    o1, o2, o3 = RW_WIDTH, 2 * RW_WIDTH, 3 * RW_WIDTH
    r, k, v = p[..., :o1], p[..., o1:o2], p[..., o2:o3]
    xw = p[..., o3:o3 + W_LORA]
    xa = p[..., o3 + W_LORA:o3 + W_LORA + A_LORA]
    xg = p[..., o3 + W_LORA + A_LORA:]
    w_raw = (w0 + jnp.tanh(xw) @ w_up).astype(jnp.float32)
    decay = jnp.exp(-jnp.exp(-jax.nn.softplus(-w_raw) - 0.5))
    a = jax.nn.sigmoid((a0 + xa @ a_up).astype(jnp.float32))
    g = jax.nn.sigmoid(xg) @ g_up
    kk = (k * k_k).astype(jnp.float32).reshape(B, S, H, N)
    kk = kk / jnp.maximum(jnp.linalg.norm(kk, axis=-1, keepdims=True), 1e-12)
    k = k.astype(jnp.float32) * (1.0 + (a - 1.0) * k_a)
    heads = lambda t: t.astype(jnp.float32).reshape(B, S, H, N)
    r_h, k_h, v_h = heads(r), heads(k), heads(v)
    y = wkv7_scan(r_h, heads(decay), k_h, v_h, kk, heads(a))
    mean = jnp.mean(y, axis=-1, keepdims=True)
    var = jnp.mean(jnp.square(y - mean), axis=-1, keepdims=True)
    y = ((y - mean) * lax.rsqrt(var + GN_EPS)).reshape(B, S, RW_WIDTH) * lnx_w + lnx_b
    bonus = (jnp.sum(r_h * k_h * r_k, axis=-1, keepdims=True) * v_h).reshape(B, S, RW_WIDTH)
    return ((y + bonus) * g).astype(p.dtype)


def _fwd_setup_inputs(seed: int = 0) -> dict:
    key = jax.random.key(seed)
    ks = jax.random.split(key, 24)
    L, D = DEPTH, D_MODEL
    nrm = lambda k_, shape, s: jax.random.normal(k_, shape, jnp.float32) * s
    return {
        "x": nrm(ks[0], (BATCH, SEQ, D), 1.0),
        "norm_mix_pre": 1.0 + nrm(ks[1], (L, D), 0.1),
        "w_in": nrm(ks[2], (L, D, IN_COLS), D ** -0.5),
        "b_gate": nrm(ks[3], (L, GATE_COLS), 0.1),
        "mu_rw": jax.random.uniform(ks[4], (L, RW_COLS), jnp.float32),
        "w0": jax.random.uniform(ks[5], (L, RW_WIDTH), jnp.float32, -6.0, 0.0),
        "w_up": nrm(ks[6], (L, W_LORA, RW_WIDTH), 0.1),
        "a0": nrm(ks[7], (L, RW_WIDTH), 0.1),
        "a_up": nrm(ks[8], (L, A_LORA, RW_WIDTH), 0.5 * A_LORA ** -0.5),
        "g_up": nrm(ks[9], (L, G_LORA, RW_WIDTH), G_LORA ** -0.5),
        "k_k": 0.85 + nrm(ks[10], (L, RW_WIDTH), 0.05),
        "k_a": 1.0 + nrm(ks[11], (L, RW_WIDTH), 0.05),
        "r_k": nrm(ks[12], (L, RW_HEADS, RW_HEAD_DIM), 0.1),
        "lnx_w": 1.0 + nrm(ks[13], (L, RW_WIDTH), 0.1),
        "lnx_b": nrm(ks[14], (L, RW_WIDTH), 0.02),
        "w_sb_out": nrm(ks[15], (L, SB_WIDTH, D), SB_WIDTH ** -0.5),
        "w_rw_out": nrm(ks[16], (L, RW_WIDTH, D), RW_WIDTH ** -0.5),
        "w_o": nrm(ks[17], (L, D, D), D ** -0.5),
        "norm_mix_post": 1.0 + nrm(ks[18], (L, D), 0.1),
        "norm_ffn_pre": 1.0 + nrm(ks[19], (L, D), 0.1),
        "w_ffn_gate": nrm(ks[20], (L, D, D_FF), D ** -0.5),
        "w_ffn_up": nrm(ks[21], (L, D, D_FF), D ** -0.5),
        "w_ffn_down": nrm(ks[22], (L, D_FF, D), D_FF ** -0.5),
        "norm_ffn_post": 1.0 + nrm(ks[23], (L, D), 0.1),
    }


def _fwd_reference(x, norm_mix_pre, w_in, b_gate, mu_rw, w0, w_up, a0, a_up, g_up, k_k, k_a, r_k,
              lnx_w, lnx_b, w_sb_out, w_rw_out, w_o, norm_mix_post, norm_ffn_pre,
              w_ffn_gate, w_ffn_up, w_ffn_down, norm_ffn_post):
    B, S, D = x.shape
    for l in range(DEPTH):
        h = rms_norm(x, norm_mix_pre[l])
        proj = h @ w_in[l]
        p_sb = proj[..., :SB_COLS]
        p_rw = proj[..., SB_COLS:SB_COLS + RW_COLS]
        gates = jax.nn.sigmoid(proj[..., SB_COLS + RW_COLS:] + b_gate[l])
        q = p_sb[..., :SB_WIDTH].reshape(B, S, SB_HEADS, SB_HEAD_DIM)
        k = p_sb[..., SB_WIDTH:2 * SB_WIDTH].reshape(B, S, SB_HEADS, SB_HEAD_DIM)
        v = p_sb[..., 2 * SB_WIDTH:].reshape(B, S, SB_HEADS, SB_HEAD_DIM)
        o_sb = stick_breaking_attention(q, k, v).reshape(B, S, SB_WIDTH)
        o_rw = rwkv7_time_mix(p_rw, mu_rw[l], w0[l], w_up[l], a0[l], a_up[l], g_up[l],
                              k_k[l], k_a[l], r_k[l], lnx_w[l], lnx_b[l])
        merged = (gates[..., :D] * (o_sb @ w_sb_out[l])
                  + gates[..., D:] * (o_rw @ w_rw_out[l]))
        x = x + rms_norm(merged @ w_o[l], norm_mix_post[l])
        h = rms_norm(x, norm_ffn_pre[l])
        f = (jax.nn.silu(h @ w_ffn_gate[l]) * (h @ w_ffn_up[l])) @ w_ffn_down[l]
        x = x + rms_norm(f, norm_ffn_post[l])
    return x


import jax as _jax
import jax.numpy as _jnp

TWIN_FORMAT = 'train_step'
FWD_PARAMS = ['x', 'norm_mix_pre', 'w_in', 'b_gate', 'mu_rw', 'w0', 'w_up', 'a0', 'a_up', 'g_up', 'k_k', 'k_a', 'r_k', 'lnx_w', 'lnx_b', 'w_sb_out', 'w_rw_out', 'w_o', 'norm_mix_post', 'norm_ffn_pre', 'w_ffn_gate', 'w_ffn_up', 'w_ffn_down', 'norm_ffn_post']
TWIN_WEIGHTS = ['norm_mix_pre', 'w_in', 'b_gate', 'mu_rw', 'w0', 'w_up', 'a0', 'a_up', 'g_up', 'k_k', 'k_a', 'r_k', 'lnx_w', 'lnx_b', 'w_sb_out', 'w_rw_out', 'w_o', 'norm_mix_post', 'norm_ffn_pre', 'w_ffn_gate', 'w_ffn_up', 'w_ffn_down', 'norm_ffn_post']
TWIN_DIFF_INPUT = 'x'
TWIN_INPUTS = ['x', 'norm_mix_pre', 'w_in', 'b_gate', 'mu_rw', 'w0', 'w_up', 'a0', 'a_up', 'g_up', 'k_k', 'k_a', 'r_k', 'lnx_w', 'lnx_b', 'w_sb_out', 'w_rw_out', 'w_o', 'norm_mix_post', 'norm_ffn_pre', 'w_ffn_gate', 'w_ffn_up', 'w_ffn_down', 'norm_ffn_post', 'loss_target', 'm_norm_mix_pre', 'm_w_in', 'm_b_gate', 'm_mu_rw', 'm_w0', 'm_w_up', 'm_a0', 'm_a_up', 'm_g_up', 'm_k_k', 'm_k_a', 'm_r_k', 'm_lnx_w', 'm_lnx_b', 'm_w_sb_out', 'm_w_rw_out', 'm_w_o', 'm_norm_mix_post', 'm_norm_ffn_pre', 'm_w_ffn_gate', 'm_w_ffn_up', 'm_w_ffn_down', 'm_norm_ffn_post', 'v_norm_mix_pre', 'v_w_in', 'v_b_gate', 'v_mu_rw', 'v_w0', 'v_w_up', 'v_a0', 'v_a_up', 'v_g_up', 'v_k_k', 'v_k_a', 'v_r_k', 'v_lnx_w', 'v_lnx_b', 'v_w_sb_out', 'v_w_rw_out', 'v_w_o', 'v_norm_mix_post', 'v_norm_ffn_pre', 'v_w_ffn_gate', 'v_w_ffn_up', 'v_w_ffn_down', 'v_norm_ffn_post']
TWIN_OUTPUTS = ['loss', 'grad_x', 'grad_norm_mix_pre', 'grad_w_in', 'grad_b_gate', 'grad_mu_rw', 'grad_w0', 'grad_w_up', 'grad_a0', 'grad_a_up', 'grad_g_up', 'grad_k_k', 'grad_k_a', 'grad_r_k', 'grad_lnx_w', 'grad_lnx_b', 'grad_w_sb_out', 'grad_w_rw_out', 'grad_w_o', 'grad_norm_mix_post', 'grad_norm_ffn_pre', 'grad_w_ffn_gate', 'grad_w_ffn_up', 'grad_w_ffn_down', 'grad_norm_ffn_post', 'delta_norm_mix_pre', 'delta_w_in', 'delta_b_gate', 'delta_mu_rw', 'delta_w0', 'delta_w_up', 'delta_a0', 'delta_a_up', 'delta_g_up', 'delta_k_k', 'delta_k_a', 'delta_r_k', 'delta_lnx_w', 'delta_lnx_b', 'delta_w_sb_out', 'delta_w_rw_out', 'delta_w_o', 'delta_norm_mix_post', 'delta_norm_ffn_pre', 'delta_w_ffn_gate', 'delta_w_ffn_up', 'delta_w_ffn_down', 'delta_norm_ffn_post', 'new_m_norm_mix_pre', 'new_m_w_in', 'new_m_b_gate', 'new_m_mu_rw', 'new_m_w0', 'new_m_w_up', 'new_m_a0', 'new_m_a_up', 'new_m_g_up', 'new_m_k_k', 'new_m_k_a', 'new_m_r_k', 'new_m_lnx_w', 'new_m_lnx_b', 'new_m_w_sb_out', 'new_m_w_rw_out', 'new_m_w_o', 'new_m_norm_mix_post', 'new_m_norm_ffn_pre', 'new_m_w_ffn_gate', 'new_m_w_ffn_up', 'new_m_w_ffn_down', 'new_m_norm_ffn_post', 'new_v_norm_mix_pre', 'new_v_w_in', 'new_v_b_gate', 'new_v_mu_rw', 'new_v_w0', 'new_v_w_up', 'new_v_a0', 'new_v_a_up', 'new_v_g_up', 'new_v_k_k', 'new_v_k_a', 'new_v_r_k', 'new_v_lnx_w', 'new_v_lnx_b', 'new_v_w_sb_out', 'new_v_w_rw_out', 'new_v_w_o', 'new_v_norm_mix_post', 'new_v_norm_ffn_pre', 'new_v_w_ffn_gate', 'new_v_w_ffn_up', 'new_v_w_ffn_down', 'new_v_norm_ffn_post']
TWIN_LEAF_KINDS = {'loss': 'loss', 'grad_x': 'grad_x', 'grad_norm_mix_pre': 'grad_w', 'grad_w_in': 'grad_w', 'grad_b_gate': 'grad_w', 'grad_mu_rw': 'grad_w', 'grad_w0': 'grad_w', 'grad_w_up': 'grad_w', 'grad_a0': 'grad_w', 'grad_a_up': 'grad_w', 'grad_g_up': 'grad_w', 'grad_k_k': 'grad_w', 'grad_k_a': 'grad_w', 'grad_r_k': 'grad_w', 'grad_lnx_w': 'grad_w', 'grad_lnx_b': 'grad_w', 'grad_w_sb_out': 'grad_w', 'grad_w_rw_out': 'grad_w', 'grad_w_o': 'grad_w', 'grad_norm_mix_post': 'grad_w', 'grad_norm_ffn_pre': 'grad_w', 'grad_w_ffn_gate': 'grad_w', 'grad_w_ffn_up': 'grad_w', 'grad_w_ffn_down': 'grad_w', 'grad_norm_ffn_post': 'grad_w', 'delta_norm_mix_pre': 'delta_w', 'delta_w_in': 'delta_w', 'delta_b_gate': 'delta_w', 'delta_mu_rw': 'delta_w', 'delta_w0': 'delta_w', 'delta_w_up': 'delta_w', 'delta_a0': 'delta_w', 'delta_a_up': 'delta_w', 'delta_g_up': 'delta_w', 'delta_k_k': 'delta_w', 'delta_k_a': 'delta_w', 'delta_r_k': 'delta_w', 'delta_lnx_w': 'delta_w', 'delta_lnx_b': 'delta_w', 'delta_w_sb_out': 'delta_w', 'delta_w_rw_out': 'delta_w', 'delta_w_o': 'delta_w', 'delta_norm_mix_post': 'delta_w', 'delta_norm_ffn_pre': 'delta_w', 'delta_w_ffn_gate': 'delta_w', 'delta_w_ffn_up': 'delta_w', 'delta_w_ffn_down': 'delta_w', 'delta_norm_ffn_post': 'delta_w', 'new_m_norm_mix_pre': 'new_m', 'new_m_w_in': 'new_m', 'new_m_b_gate': 'new_m', 'new_m_mu_rw': 'new_m', 'new_m_w0': 'new_m', 'new_m_w_up': 'new_m', 'new_m_a0': 'new_m', 'new_m_a_up': 'new_m', 'new_m_g_up': 'new_m', 'new_m_k_k': 'new_m', 'new_m_k_a': 'new_m', 'new_m_r_k': 'new_m', 'new_m_lnx_w': 'new_m', 'new_m_lnx_b': 'new_m', 'new_m_w_sb_out': 'new_m', 'new_m_w_rw_out': 'new_m', 'new_m_w_o': 'new_m', 'new_m_norm_mix_post': 'new_m', 'new_m_norm_ffn_pre': 'new_m', 'new_m_w_ffn_gate': 'new_m', 'new_m_w_ffn_up': 'new_m', 'new_m_w_ffn_down': 'new_m', 'new_m_norm_ffn_post': 'new_m', 'new_v_norm_mix_pre': 'new_v', 'new_v_w_in': 'new_v', 'new_v_b_gate': 'new_v', 'new_v_mu_rw': 'new_v', 'new_v_w0': 'new_v', 'new_v_w_up': 'new_v', 'new_v_a0': 'new_v', 'new_v_a_up': 'new_v', 'new_v_g_up': 'new_v', 'new_v_k_k': 'new_v', 'new_v_k_a': 'new_v', 'new_v_r_k': 'new_v', 'new_v_lnx_w': 'new_v', 'new_v_lnx_b': 'new_v', 'new_v_w_sb_out': 'new_v', 'new_v_w_rw_out': 'new_v', 'new_v_w_o': 'new_v', 'new_v_norm_mix_post': 'new_v', 'new_v_norm_ffn_pre': 'new_v', 'new_v_w_ffn_gate': 'new_v', 'new_v_w_ffn_up': 'new_v', 'new_v_w_ffn_down': 'new_v', 'new_v_norm_ffn_post': 'new_v'}


def _forward(args):
    return _fwd_reference(*[args[k] for k in FWD_PARAMS])


def _output_shape():
    out = _jax.eval_shape(lambda: _forward(_fwd_setup_inputs(0)))
    return out.shape, out.dtype

N_MICROBATCH = 1
ADAM_LR = 0.001
ADAM_B1 = 0.9
ADAM_B2 = 0.999
ADAM_EPS = 1e-08
ADAM_WD = 0.01
ADAM_STEP = 10
PER_EXAMPLE_BATCH_AXIS = {'x': 0, 'loss_target': 0}
SHARED_INPUTS = []
_WEIGHT_DTYPES = {'norm_mix_pre': _jnp.float32, 'w_in': _jnp.float32, 'b_gate': _jnp.float32, 'mu_rw': _jnp.float32, 'w0': _jnp.float32, 'w_up': _jnp.float32, 'a0': _jnp.float32, 'a_up': _jnp.float32, 'g_up': _jnp.float32, 'k_k': _jnp.float32, 'k_a': _jnp.float32, 'r_k': _jnp.float32, 'lnx_w': _jnp.float32, 'lnx_b': _jnp.float32, 'w_sb_out': _jnp.float32, 'w_rw_out': _jnp.float32, 'w_o': _jnp.float32, 'norm_mix_post': _jnp.float32, 'norm_ffn_pre': _jnp.float32, 'w_ffn_gate': _jnp.float32, 'w_ffn_up': _jnp.float32, 'w_ffn_down': _jnp.float32, 'norm_ffn_post': _jnp.float32}
MOMENT_SCALE = {'norm_mix_pre': 1.086082e+00, 'w_in': 4.399282e-01, 'b_gate': 2.370318e-01, 'mu_rw': 9.098342e-01, 'w0': 2.693706e-01, 'w_up': 2.952686e-02, 'a0': 2.458792e-01, 'a_up': 2.306466e-01, 'g_up': 7.098141e-01, 'k_k': 1.156772e+00, 'k_a': 7.249566e-01, 'r_k': 1.354983e+00, 'lnx_w': 8.335079e-01, 'lnx_b': 1.757669e+00, 'w_sb_out': 5.566847e-01, 'w_rw_out': 4.940325e-01, 'w_o': 8.239255e-01, 'norm_mix_post': 6.442560e+01, 'norm_ffn_pre': 8.016224e-01, 'w_ffn_gate': 2.767568e-01, 'w_ffn_up': 4.379604e-01, 'w_ffn_down': 7.341088e-01, 'norm_ffn_post': 6.392080e+01}


def _to_microbatches(a, axis):
    t = _jnp.moveaxis(a, axis, 0)
    t = t.reshape((N_MICROBATCH, t.shape[0] // N_MICROBATCH) + t.shape[1:])
    return _jnp.moveaxis(t, 1, axis + 1)


def setup_inputs(seed: int = 0) -> dict:
    inp = _fwd_setup_inputs(seed)
    key = _jax.random.fold_in(_jax.random.key(seed), 7919)
    shape, _ = _output_shape()
    out = dict(inp)
    out["loss_target"] = _jax.random.normal(_jax.random.fold_in(key, 0), shape, _jnp.float32)
    for i, name in enumerate(TWIN_WEIGHTS):
        w = inp[name].astype(_jnp.float32)
        if MOMENT_SCALE is None:
            s = _jnp.sqrt(_jnp.mean(_jnp.square(w)) + 1e-30)
        else:
            s = MOMENT_SCALE[name]
        km, kv = _jax.random.split(_jax.random.fold_in(key, i + 1))
        out[name] = w
        out["m_" + name] = s * _jax.random.normal(km, w.shape, _jnp.float32)
        out["v_" + name] = (s * s) * _jax.random.uniform(kv, w.shape, _jnp.float32, 0.5, 1.5)
    if N_MICROBATCH > 1:
        for name, axis in PER_EXAMPLE_BATCH_AXIS.items():
            out[name] = _to_microbatches(out[name], axis)
    return {'x': out['x'], 'norm_mix_pre': out['norm_mix_pre'], 'w_in': out['w_in'], 'b_gate': out['b_gate'], 'mu_rw': out['mu_rw'], 'w0': out['w0'], 'w_up': out['w_up'], 'a0': out['a0'], 'a_up': out['a_up'], 'g_up': out['g_up'], 'k_k': out['k_k'], 'k_a': out['k_a'], 'r_k': out['r_k'], 'lnx_w': out['lnx_w'], 'lnx_b': out['lnx_b'], 'w_sb_out': out['w_sb_out'], 'w_rw_out': out['w_rw_out'], 'w_o': out['w_o'], 'norm_mix_post': out['norm_mix_post'], 'norm_ffn_pre': out['norm_ffn_pre'], 'w_ffn_gate': out['w_ffn_gate'], 'w_ffn_up': out['w_ffn_up'], 'w_ffn_down': out['w_ffn_down'], 'norm_ffn_post': out['norm_ffn_post'], 'loss_target': out['loss_target'], 'm_norm_mix_pre': out['m_norm_mix_pre'], 'm_w_in': out['m_w_in'], 'm_b_gate': out['m_b_gate'], 'm_mu_rw': out['m_mu_rw'], 'm_w0': out['m_w0'], 'm_w_up': out['m_w_up'], 'm_a0': out['m_a0'], 'm_a_up': out['m_a_up'], 'm_g_up': out['m_g_up'], 'm_k_k': out['m_k_k'], 'm_k_a': out['m_k_a'], 'm_r_k': out['m_r_k'], 'm_lnx_w': out['m_lnx_w'], 'm_lnx_b': out['m_lnx_b'], 'm_w_sb_out': out['m_w_sb_out'], 'm_w_rw_out': out['m_w_rw_out'], 'm_w_o': out['m_w_o'], 'm_norm_mix_post': out['m_norm_mix_post'], 'm_norm_ffn_pre': out['m_norm_ffn_pre'], 'm_w_ffn_gate': out['m_w_ffn_gate'], 'm_w_ffn_up': out['m_w_ffn_up'], 'm_w_ffn_down': out['m_w_ffn_down'], 'm_norm_ffn_post': out['m_norm_ffn_post'], 'v_norm_mix_pre': out['v_norm_mix_pre'], 'v_w_in': out['v_w_in'], 'v_b_gate': out['v_b_gate'], 'v_mu_rw': out['v_mu_rw'], 'v_w0': out['v_w0'], 'v_w_up': out['v_w_up'], 'v_a0': out['v_a0'], 'v_a_up': out['v_a_up'], 'v_g_up': out['v_g_up'], 'v_k_k': out['v_k_k'], 'v_k_a': out['v_k_a'], 'v_r_k': out['v_r_k'], 'v_lnx_w': out['v_lnx_w'], 'v_lnx_b': out['v_lnx_b'], 'v_w_sb_out': out['v_w_sb_out'], 'v_w_rw_out': out['v_w_rw_out'], 'v_w_o': out['v_w_o'], 'v_norm_mix_post': out['v_norm_mix_post'], 'v_norm_ffn_pre': out['v_norm_ffn_pre'], 'v_w_ffn_gate': out['v_w_ffn_gate'], 'v_w_ffn_up': out['v_w_ffn_up'], 'v_w_ffn_down': out['v_w_ffn_down'], 'v_norm_ffn_post': out['v_norm_ffn_post']}


def _loss(weights, diff, rest, loss_target):
    with _jax.named_scope("forward"):
        args = {**rest, TWIN_DIFF_INPUT: diff, **{k: w.astype(_WEIGHT_DTYPES[k]) for k, w in weights.items()}}
        y = _forward(args)
    with _jax.named_scope("loss_head"):
        err = _jnp.square(y.astype(_jnp.float32) - loss_target)
        return 0.5 * _jnp.sum(_jnp.mean(err, axis=-1)) if err.ndim else 0.5 * err


def _adamw(w, g, m, v):
    m = ADAM_B1 * m + (1.0 - ADAM_B1) * g
    v = ADAM_B2 * v + (1.0 - ADAM_B2) * _jnp.square(g)
    m_hat = m / (1.0 - ADAM_B1 ** ADAM_STEP)
    v_hat = v / (1.0 - ADAM_B2 ** ADAM_STEP)
    delta = -ADAM_LR * (m_hat / (_jnp.sqrt(v_hat) + ADAM_EPS) + ADAM_WD * w)
    return delta, m, v


def reference(x, norm_mix_pre, w_in, b_gate, mu_rw, w0, w_up, a0, a_up, g_up, k_k, k_a, r_k, lnx_w, lnx_b, w_sb_out, w_rw_out, w_o, norm_mix_post, norm_ffn_pre, w_ffn_gate, w_ffn_up, w_ffn_down, norm_ffn_post, loss_target, m_norm_mix_pre, m_w_in, m_b_gate, m_mu_rw, m_w0, m_w_up, m_a0, m_a_up, m_g_up, m_k_k, m_k_a, m_r_k, m_lnx_w, m_lnx_b, m_w_sb_out, m_w_rw_out, m_w_o, m_norm_mix_post, m_norm_ffn_pre, m_w_ffn_gate, m_w_ffn_up, m_w_ffn_down, m_norm_ffn_post, v_norm_mix_pre, v_w_in, v_b_gate, v_mu_rw, v_w0, v_w_up, v_a0, v_a_up, v_g_up, v_k_k, v_k_a, v_r_k, v_lnx_w, v_lnx_b, v_w_sb_out, v_w_rw_out, v_w_o, v_norm_mix_post, v_norm_ffn_pre, v_w_ffn_gate, v_w_ffn_up, v_w_ffn_down, v_norm_ffn_post):
    given = dict(x=x, norm_mix_pre=norm_mix_pre, w_in=w_in, b_gate=b_gate, mu_rw=mu_rw, w0=w0, w_up=w_up, a0=a0, a_up=a_up, g_up=g_up, k_k=k_k, k_a=k_a, r_k=r_k, lnx_w=lnx_w, lnx_b=lnx_b, w_sb_out=w_sb_out, w_rw_out=w_rw_out, w_o=w_o, norm_mix_post=norm_mix_post, norm_ffn_pre=norm_ffn_pre, w_ffn_gate=w_ffn_gate, w_ffn_up=w_ffn_up, w_ffn_down=w_ffn_down, norm_ffn_post=norm_ffn_post, loss_target=loss_target, m_norm_mix_pre=m_norm_mix_pre, m_w_in=m_w_in, m_b_gate=m_b_gate, m_mu_rw=m_mu_rw, m_w0=m_w0, m_w_up=m_w_up, m_a0=m_a0, m_a_up=m_a_up, m_g_up=m_g_up, m_k_k=m_k_k, m_k_a=m_k_a, m_r_k=m_r_k, m_lnx_w=m_lnx_w, m_lnx_b=m_lnx_b, m_w_sb_out=m_w_sb_out, m_w_rw_out=m_w_rw_out, m_w_o=m_w_o, m_norm_mix_post=m_norm_mix_post, m_norm_ffn_pre=m_norm_ffn_pre, m_w_ffn_gate=m_w_ffn_gate, m_w_ffn_up=m_w_ffn_up, m_w_ffn_down=m_w_ffn_down, m_norm_ffn_post=m_norm_ffn_post, v_norm_mix_pre=v_norm_mix_pre, v_w_in=v_w_in, v_b_gate=v_b_gate, v_mu_rw=v_mu_rw, v_w0=v_w0, v_w_up=v_w_up, v_a0=v_a0, v_a_up=v_a_up, v_g_up=v_g_up, v_k_k=v_k_k, v_k_a=v_k_a, v_r_k=v_r_k, v_lnx_w=v_lnx_w, v_lnx_b=v_lnx_b, v_w_sb_out=v_w_sb_out, v_w_rw_out=v_w_rw_out, v_w_o=v_w_o, v_norm_mix_post=v_norm_mix_post, v_norm_ffn_pre=v_norm_ffn_pre, v_w_ffn_gate=v_w_ffn_gate, v_w_ffn_up=v_w_ffn_up, v_w_ffn_down=v_w_ffn_down, v_norm_ffn_post=v_norm_ffn_post)
    weights = {n: given[n] for n in TWIN_WEIGHTS}
    shared = {n: given[n] for n in SHARED_INPUTS}
    per_example = {n: given[n] for n in ['x']}
    grad_fn = _jax.value_and_grad(_loss, argnums=(0, 1))

    def one_microbatch(ex, loss_target):
        ex = dict(ex)
        diff = ex.pop(TWIN_DIFF_INPUT)
        return grad_fn(weights, diff, {**shared, **ex}, loss_target)

    if N_MICROBATCH == 1:
        loss, (grad_w, grad_x) = one_microbatch(per_example, given["loss_target"])
    else:
        def body(carry, xs):
            loss_sum, grad_sum = carry
            l_k, (gw_k, gx_k) = one_microbatch(xs[0], xs[1])
            with _jax.named_scope("update"):
                return (loss_sum + l_k, _jax.tree.map(_jnp.add, grad_sum, gw_k)), gx_k

        init = (_jnp.zeros((), _jnp.float32), _jax.tree.map(_jnp.zeros_like, weights))
        (loss, grad_w), grad_x = _jax.lax.scan(body, init, (per_example, given["loss_target"]))
    with _jax.named_scope("update"):
        delta_w, new_m, new_v = {}, {}, {}
        for n in TWIN_WEIGHTS:
            delta_w[n], new_m[n], new_v[n] = _adamw(weights[n], grad_w[n], given["m_" + n], given["v_" + n])
    return (loss, grad_x, *[grad_w[n] for n in TWIN_WEIGHTS], *[delta_w[n] for n in TWIN_WEIGHTS],
            *[new_m[n] for n in TWIN_WEIGHTS], *[new_v[n] for n in TWIN_WEIGHTS])
```

```python
import functools

import jax
import jax.numpy as jnp
from jax import lax
from jax.experimental import pallas as pl
from jax.experimental.pallas import tpu as pltpu

F32 = jnp.float32
BF16 = jnp.bfloat16
HI = lax.Precision.HIGHEST
MESH = pl.DeviceIdType.MESH

D_MODEL = 1024
SB_HEADS = 8
HEAD_DIM = 64
SB_WIDTH = SB_HEADS * HEAD_DIM
RW_WIDTH = 512
LORA_COLS = 256
SB_COLS = 3 * SB_WIDTH
RW_COLS = 3 * RW_WIDTH + LORA_COLS
GATE_COLS = 2 * D_MODEL
D_FF = 2816
RMS_EPS = 1e-6
GN_EPS = HEAD_DIM * 1e-5
WKV_CHUNK = 64
ATTN_BLOCK = 128
LANES = 128
SUBLANES = 8
N_CHIPS = 4
N_DEV = 8

ADAM_LR = 0.001
ADAM_B1 = 0.9
ADAM_B2 = 0.999
ADAM_EPS = 1e-08
ADAM_WD = 0.01
ADAM_STEP = 10

VMEM_LIMIT = 48 * 1024 * 1024


def _params(sem=None, **kw):
    if sem is not None:
        kw["dimension_semantics"] = sem
    return pltpu.CompilerParams(vmem_limit_bytes=VMEM_LIMIT, **kw)


def _div_tile(dim, pref, mult=LANES):
    if dim <= pref:
        return dim
    t = pref - pref % mult
    while t >= mult:
        if dim % t == 0:
            return t
        t -= mult
    return dim


def _dot(a, b, dims):
    return lax.dot_general(a, b, (dims, ((), ())), preferred_element_type=F32)


def _mm(name, a, b, *, ta=False, tb=False, acc=None, out_dtype=F32):
    if ta:
        K, M = a.shape
    else:
        M, K = a.shape
    N = b.shape[0] if tb else b.shape[1]
    tm, tn, tk = _div_tile(M, 512), _div_tile(N, 1408), _div_tile(K, 1408)
    nk = K // tk
    dims = ((0,) if ta else (1,), (1,) if tb else (0,))
    has_acc = acc is not None

    def body(*refs):
        a_ref, b_ref = refs[0], refs[1]
        o_ref, scr = refs[-2], refs[-1]
        k = pl.program_id(2)
        part = _dot(a_ref[...].astype(BF16), b_ref[...].astype(BF16), dims)

        @pl.when(k == 0)
        def _():
            scr[...] = part + refs[2][...] if has_acc else part

        @pl.when(k > 0)
        def _():
            scr[...] += part

        @pl.when(k == nk - 1)
        def _():
            o_ref[...] = scr[...].astype(o_ref.dtype)

    a_spec = pl.BlockSpec((tk, tm), lambda i, j, k: (k, i)) if ta else pl.BlockSpec((tm, tk), lambda i, j, k: (i, k))
    b_spec = pl.BlockSpec((tn, tk), lambda i, j, k: (j, k)) if tb else pl.BlockSpec((tk, tn), lambda i, j, k: (k, j))
    o_spec = pl.BlockSpec((tm, tn), lambda i, j, k: (i, j))
    return pl.pallas_call(
        body, name=name,
        grid=(M // tm, N // tn, nk),
        in_specs=[a_spec, b_spec] + ([o_spec] if has_acc else []),
        out_specs=o_spec,
        out_shape=jax.ShapeDtypeStruct((M, N), out_dtype),
        scratch_shapes=[pltpu.VMEM((tm, tn), F32)],
        compiler_params=_params(("parallel", "parallel", "arbitrary")),
    )(*([a, b] + ([acc] if has_acc else [])))


def _row_spec(tile, width, colblk):
    return pl.BlockSpec((tile, width), lambda i: (i, colblk))


def _full_spec(shape):
    return pl.BlockSpec(shape, lambda i: (0,) * len(shape))


def _rowwise(name, fn, rows, params, outs, tile=256):
    T = rows[0][0].shape[0]
    tile = min(tile, T)
    n_r, n_p = len(rows), len(params)

    def body(*refs):
        r = [x[...].astype(F32) for x in refs[:n_r]]
        p = [x[...].astype(F32) for x in refs[n_r:n_r + n_p]]
        for o_ref, val in zip(refs[n_r + n_p:], fn(*r, *p)):
            o_ref[...] = val.astype(o_ref.dtype)

    return pl.pallas_call(
        body, name=name,
        grid=(T // tile,),
        in_specs=[_row_spec(tile, w, cb) for _, w, cb in rows] + [_full_spec(p.shape) for p in params],
        out_specs=[_row_spec(tile, w, 0) for w, _ in outs],
        out_shape=[jax.ShapeDtypeStruct((T, w), dt) for w, dt in outs],
        compiler_params=_params(("parallel",)),
    )(*([a for a, _, _ in rows] + list(params)))


def _rowwise_vjp(name, fn, rows, params, cts, need_rows, need_params, add_to=None, tile=256):
    add_to = add_to or {}
    T = rows[0][0].shape[0]
    tile = min(tile, T)
    n_r, n_p = len(rows), len(params)
    ct_flat = [c for group in cts for c in group]
    ct_sizes = [len(group) for group in cts]
    add_idx = sorted(add_to)
    row_out = [i for i in range(n_r) if need_rows[i]]
    par_out = [i for i in range(n_p) if need_params[i]]
    n_ct, n_add = len(ct_flat), len(add_idx)

    def body(*refs):
        pos = 0
        r = [x[...].astype(F32) for x in refs[pos:pos + n_r]]
        pos += n_r
        p = [x[...].astype(F32) for x in refs[pos:pos + n_p]]
        pos += n_p
        ct_vals = [x[...].astype(F32) for x in refs[pos:pos + n_ct]]
        pos += n_ct
        adds = {i: x[...] for i, x in zip(add_idx, refs[pos:pos + n_add])}
        pos += n_add
        drow_refs = refs[pos:pos + len(row_out)]
        pos += len(row_out)
        dpar_refs = refs[pos:pos + len(par_out)]
        ct_in, q = [], 0
        for n in ct_sizes:
            ct_in.append(functools.reduce(lambda u, v: u + v, ct_vals[q:q + n]))
            q += n
        _, vjp = jax.vjp(fn, *r, *p)
        grads = vjp(tuple(ct_in))
        for ref, i in zip(drow_refs, row_out):
            g = grads[i]
            ref[...] = g + adds[i] if i in adds else g

        @pl.when(pl.program_id(0) == 0)
        def _():
            for ref in dpar_refs:
                ref[...] = jnp.zeros_like(ref)

        for ref, i in zip(dpar_refs, par_out):
            ref[...] += grads[n_r + i]

    ct_widths = [c.shape[1] for c in ct_flat]
    in_specs = ([_row_spec(tile, w, cb) for _, w, cb in rows] + [_full_spec(p.shape) for p in params]
                + [_row_spec(tile, w, 0) for w in ct_widths] + [_row_spec(tile, rows[i][1], 0) for i in add_idx])
    out_specs = [_row_spec(tile, rows[i][1], 0) for i in row_out] + [_full_spec(params[i].shape) for i in par_out]
    out_shape = ([jax.ShapeDtypeStruct((T, rows[i][1]), F32) for i in row_out]
                 + [jax.ShapeDtypeStruct(params[i].shape, F32) for i in par_out])
    res = pl.pallas_call(
        body, name=name,
        grid=(T // tile,),
        in_specs=in_specs, out_specs=out_specs, out_shape=out_shape,
        compiler_params=_params(("arbitrary",)),
    )(*([a for a, _, _ in rows] + list(params) + ct_flat + [add_to[i] for i in add_idx]))
    return res[:len(row_out)], res[len(row_out):]


def _sigmoid(x):
    return 0.5 * (jnp.tanh(0.5 * x) + 1.0)


def _softplus(x):
    return jnp.maximum(x, 0.0) + jnp.log(1.0 + jnp.exp(-jnp.abs(x)))


def _rms(x, g):
    return x * lax.rsqrt(jnp.mean(x * x, axis=-1, keepdims=True) + RMS_EPS) * g


def _segsum_impl(x):
    n = x.shape[-1]
    r = lax.shift_right_logical(lax.broadcasted_iota(jnp.int32, (n, n), 0), 6)
    c = lax.shift_right_logical(lax.broadcasted_iota(jnp.int32, (n, n), 1), 6)
    bd = (r == c).astype(BF16)
    hi = x.astype(BF16)
    rest = x - hi.astype(F32)
    mid = rest.astype(BF16)
    lo = (rest - mid.astype(F32)).astype(BF16)
    nn = ((1,), (0,))
    return _dot(hi, bd, nn) + _dot(mid, bd, nn) + _dot(lo, bd, nn)


@jax.custom_vjp
def _segsum(x):
    return _segsum_impl(x)


_segsum.defvjp(lambda x: (_segsum_impl(x), None), lambda _, g: (_segsum_impl(g),))


@jax.custom_vjp
def _mmb(a, w):
    return _dot(a.astype(BF16), w.astype(BF16), ((1,), (0,)))


def _mmb_fwd(a, w):
    return _mmb(a, w), (a, w)


def _mmb_bwd(res, g):
    a, w = res
    gb = g.astype(BF16)
    return _dot(gb, w.astype(BF16), ((1,), (1,))), _dot(a.astype(BF16), gb, ((0,), (0,)))


_mmb.defvjp(_mmb_fwd, _mmb_bwd)


def _f_norm(x, g):
    return (_rms(x, g),)


def _f_post1(x, u, g2, g3):
    x1 = x + _rms(u, g2)
    return x1, _rms(x1, g3)


def _f_swiglu(ag, au):
    return (ag * _sigmoid(ag) * au,)


def _f_merge(pg1, pg2, m1, m2, b1, b2):
    return (_sigmoid(pg1 + b1) * m1 + _sigmoid(pg2 + b2) * m2,)


def _f_out(x1, f, g4):
    return (x1 + _rms(f, g4),)


def _f_rwpre(pr, pk, pv, pz, qr, qk, qv, qz, mur, muk, muv, muz, w0, wup, a0, aup, gup, k_k, k_a):
    r = pr + (qr - pr) * mur
    k = pk + (qk - pk) * muk
    v = pv + (qv - pv) * muv
    z = pz + (qz - pz) * muz
    w_raw = w0 + _mmb(jnp.tanh(z), wup)
    lw = -jnp.exp(-_softplus(-w_raw) - 0.5)
    a = _sigmoid(a0 + _mmb(z, aup))
    g = _mmb(_sigmoid(z), gup)
    kk = k * k_k
    kap = kk * lax.rsqrt(jnp.maximum(_segsum(kk * kk), 1e-24))
    k2 = k * (1.0 + (a - 1.0) * k_a)
    return r, lw, k2, v, kap, a, g


def _f_rwpost(y, r, k2, v, g, lnx_w, lnx_b, r_k):
    inv = 1.0 / HEAD_DIM
    yc = y - _segsum(y) * inv
    var = _segsum(yc * yc) * inv
    yn = yc * lax.rsqrt(var + GN_EPS) * lnx_w + lnx_b
    bonus = _segsum(r * k2 * r_k) * v
    return ((yn + bonus) * g,)


def _loss_head(x1, f, target, g4, tile=256):
    T, D = x1.shape
    tile = min(tile, T)

    def body(x1_ref, f_ref, t_ref, g_ref, loss_ref, dx1_ref, df_ref, dg_ref):
        (y,), vjp = jax.vjp(_f_out, x1_ref[...], f_ref[...], g_ref[...])
        err = y - t_ref[...]
        dx1, df, dg = vjp((err * (1.0 / D),))
        dx1_ref[...] = dx1
        df_ref[...] = df

        @pl.when(pl.program_id(0) == 0)
        def _():
            loss_ref[...] = jnp.zeros_like(loss_ref)
            dg_ref[...] = jnp.zeros_like(dg_ref)

        part = jnp.sum(jnp.sum(err * err, axis=1, keepdims=True), axis=0, keepdims=True) * (0.5 / D)
        loss_ref[...] += jnp.broadcast_to(part, loss_ref.shape)
        dg_ref[...] += dg

    row = pl.BlockSpec((tile, D), lambda i: (i, 0))
    return pl.pallas_call(
        body, name="loss_head",
        grid=(T // tile,),
        in_specs=[row, row, row, _full_spec(g4.shape)],
        out_specs=[_full_spec((SUBLANES, LANES)), row, row, _full_spec(g4.shape)],
        out_shape=[jax.ShapeDtypeStruct((SUBLANES, LANES), F32), jax.ShapeDtypeStruct((T, D), F32),
                   jax.ShapeDtypeStruct((T, D), F32), jax.ShapeDtypeStruct(g4.shape, F32)],
        compiler_params=_params(("arbitrary",)),
    )(x1, f, target, g4)


def _nn(a, b):
    return _dot(a, b, ((1,), (0,)))


def _nt(a, b):
    return _dot(a, b, ((1,), (1,)))


def _tn(a, b):
    return _dot(a, b, ((0,), (0,)))


def _split_dot(x, u):
    hi = x.astype(BF16)
    lo = (x - hi.astype(F32)).astype(BF16)
    return _nn(hi, u) + _nn(lo, u)


def _block_iota():
    row = lax.broadcasted_iota(jnp.int32, (ATTN_BLOCK, ATTN_BLOCK), 0)
    col = lax.broadcasted_iota(jnp.int32, (ATTN_BLOCK, ATTN_BLOCK), 1)
    return row, col


def _head_masks():
    lane = lax.broadcasted_iota(jnp.int32, (1, LANES), 1)
    return [((lane >= h * HEAD_DIM) & (lane < (h + 1) * HEAD_DIM)).astype(F32) for h in range(LANES // HEAD_DIM)]


def _sb_weights(qh, kb, c_fail, u_gt, strict, scale):
    z = _nt(qh, kb) * scale
    L = jnp.minimum(-z, 0.0) - jnp.log(1.0 + jnp.exp(-jnp.abs(z)))
    Lm = L if strict is None else jnp.where(strict, L, 0.0)
    A = jnp.exp(z + L + c_fail + _split_dot(Lm, u_gt))
    if strict is not None:
        A = jnp.where(strict, A, 0.0)
    return z, L, Lm, A


def _attn_specs(S, nq):
    q_spec = pl.BlockSpec((ATTN_BLOCK, LANES), lambda b, p, i: (b * nq + i, p))
    k_spec = pl.BlockSpec((S, LANES), lambda b, p, i: (b, SB_WIDTH // LANES + p))
    v_spec = pl.BlockSpec((S, LANES), lambda b, p, i: (b, 2 * SB_WIDTH // LANES + p))
    blk = pl.BlockSpec((ATTN_BLOCK, LANES), lambda b, p, i: (b * nq + i, p))
    seq = pl.BlockSpec((S, LANES), lambda b, p, i: (b, p))
    return q_spec, k_spec, v_spec, blk, seq


def _attn_fwd(proj, B, S):
    nq = S // ATTN_BLOCK
    scale = HEAD_DIM ** -0.5

    def body(q_ref, k_ref, v_ref, o_ref):
        i = pl.program_id(2)
        masks = _head_masks()
        row, col = _block_iota()
        u_gt = (row > col).astype(BF16)
        strict = col < row
        q = q_ref[...]
        qhs = [(q * m).astype(BF16) for m in masks]

        def block(J, carry, strict_mask):
            acc, cs = carry
            r0 = pl.multiple_of(J * ATTN_BLOCK, ATTN_BLOCK)
            kb = k_ref[pl.ds(r0, ATTN_BLOCK), :].astype(BF16)
            vb = v_ref[pl.ds(r0, ATTN_BLOCK), :]
            new_cs = []
            for h, m in enumerate(masks):
                _, _, Lm, A = _sb_weights(qhs[h], kb, cs[h], u_gt, strict_mask, scale)
                acc = acc + _nn(A.astype(BF16), (vb * m).astype(BF16))
                new_cs.append(cs[h] + jnp.sum(Lm, axis=1, keepdims=True))
            return acc, tuple(new_cs)

        zero_c = tuple(jnp.zeros((ATTN_BLOCK, 1), F32) for _ in masks)
        carry = block(i, (jnp.zeros((ATTN_BLOCK, LANES), F32), zero_c), strict)
        carry = lax.fori_loop(0, i, lambda j, c: block(i - 1 - j, c, None), carry)
        o_ref[...] = carry[0]

    q_spec, k_spec, v_spec, blk, _ = _attn_specs(S, nq)
    return pl.pallas_call(
        body, name="sb_attn_fwd",
        grid=(B, SB_WIDTH // LANES, nq),
        in_specs=[q_spec, k_spec, v_spec],
        out_specs=blk,
        out_shape=jax.ShapeDtypeStruct((B * S, SB_WIDTH), F32),
        compiler_params=_params(("parallel", "parallel", "arbitrary")),
    )(proj, proj, proj)


def _attn_bwd(proj, o, do, B, S):
    nq = S // ATTN_BLOCK
    scale = HEAD_DIM ** -0.5

    def body(q_ref, k_ref, v_ref, o_ref, do_ref, dq_ref, dk_ref, dv_ref):
        i = pl.program_id(2)

        @pl.when(i == 0)
        def _():
            dk_ref[...] = jnp.zeros_like(dk_ref)
            dv_ref[...] = jnp.zeros_like(dv_ref)

        masks = _head_masks()
        row, col = _block_iota()
        u_gt = (row > col).astype(BF16)
        u_ge = (row >= col).astype(BF16)
        strict = col < row
        q = q_ref[...]
        do_b = do_ref[...].astype(BF16)
        od = o_ref[...] * do_b.astype(F32)
        qhs = [(q * m).astype(BF16) for m in masks]
        dohs = [(do_b * m.astype(BF16)) for m in masks]
        totals = [jnp.sum(od * m, axis=1, keepdims=True) for m in masks]

        def block(J, carry, strict_mask):
            dq, c_fail, c_p = carry
            r0 = pl.multiple_of(J * ATTN_BLOCK, ATTN_BLOCK)
            kb32 = k_ref[pl.ds(r0, ATTN_BLOCK), :]
            kb = kb32.astype(BF16)
            vb = v_ref[pl.ds(r0, ATTN_BLOCK), :].astype(BF16)
            dk_blk = jnp.zeros((ATTN_BLOCK, LANES), F32)
            dv_blk = jnp.zeros((ATTN_BLOCK, LANES), F32)
            new_fail, new_p = [], []
            for h, m in enumerate(masks):
                z, L, Lm, A = _sb_weights(qhs[h], kb, c_fail[h], u_gt, strict_mask, scale)
                Ab = A.astype(BF16)
                P = Ab.astype(F32) * _nt(dohs[h], vb)
                after = c_p[h] + _split_dot(P, u_ge)
                sig = jnp.exp(z + L)
                dz = (P * (1.0 - sig) - sig * (totals[h] - after)) * scale
                if strict_mask is not None:
                    dz = jnp.where(strict_mask, dz, 0.0)
                dzb = dz.astype(BF16)
                dv_blk = dv_blk + _tn(Ab, dohs[h])
                dk_blk = dk_blk + _tn(dzb, qhs[h])
                dq = dq + _nn(dzb, (kb32 * m).astype(BF16))
                new_fail.append(c_fail[h] + jnp.sum(Lm, axis=1, keepdims=True))
                new_p.append(c_p[h] + jnp.sum(P, axis=1, keepdims=True))
            dk_ref[pl.ds(r0, ATTN_BLOCK), :] += dk_blk
            dv_ref[pl.ds(r0, ATTN_BLOCK), :] += dv_blk
            return dq, tuple(new_fail), tuple(new_p)

        zc = tuple(jnp.zeros((ATTN_BLOCK, 1), F32) for _ in masks)
        carry = block(i, (jnp.zeros((ATTN_BLOCK, LANES), F32), zc, zc), strict)
        carry = lax.fori_loop(0, i, lambda j, c: block(i - 1 - j, c, None), carry)
        dq_ref[...] = carry[0]

    q_spec, k_spec, v_spec, blk, seq = _attn_specs(S, nq)
    return pl.pallas_call(
        body, name="sb_attn_bwd",
        grid=(B, SB_WIDTH // LANES, nq),
        in_specs=[q_spec, k_spec, v_spec, blk, blk],
        out_specs=[blk, seq, seq],
        out_shape=[jax.ShapeDtypeStruct((B * S, SB_WIDTH), F32)] * 3,
        compiler_params=_params(("parallel", "parallel", "arbitrary")),
    )(proj, proj, proj, o, do)


def _bnn(a, b):
    return jnp.einsum('gmk,gkn->gmn', a, b, precision=HI, preferred_element_type=F32)


def _bnt(a, b):
    return jnp.einsum('gmk,gnk->gmn', a, b, precision=HI, preferred_element_type=F32)


def _btn(a, b):
    return jnp.einsum('gkm,gkn->gmn', a, b, precision=HI, preferred_element_type=F32)


def _wkv_chunk(S0, r, lw, k, v, kap, a):
    G, C, N = r.shape
    row = lax.broadcasted_iota(jnp.int32, (C, C), 0)
    col = lax.broadcasted_iota(jnp.int32, (C, C), 1)
    incl = (col <= row).astype(F32)
    strict = (col < row).astype(F32)
    cum = _bnn(jnp.broadcast_to(incl, (G, C, C)), lw)
    e_pos = jnp.exp(cum)
    e_neg = jnp.exp(-cum)
    al = -kap * jnp.exp(cum - lw)
    be = kap * a * e_neg
    kt = k * e_neg
    rt = r * e_pos
    m_ab = _bnt(al, be) * strict
    m_ak = _bnt(al, kt) * strict
    m_rb = _bnt(rt, be) * incl
    m_rk = _bnt(rt, kt) * incl
    sa = _bnt(al, S0) + _bnn(m_ak, v)
    p = m_ab
    steps = max(1, (C - 1).bit_length())
    for j in range(steps):
        sa = sa + _bnn(p, sa)
        if j + 1 < steps:
            p = _bnn(p, p)
    y = _bnt(rt, S0) + _bnn(m_rb, sa) + _bnn(m_rk, v)
    S1 = (S0 + _btn(sa, be) + _btn(v, kt)) * e_pos[:, C - 1:C, :]
    return y, S1


def _split_heads(x):
    return jnp.stack([x[:, h * HEAD_DIM:(h + 1) * HEAD_DIM] for h in range(x.shape[1] // HEAD_DIM)], axis=0)


def _merge_heads(x):
    return jnp.concatenate([x[h] for h in range(x.shape[0])], axis=1)


def _wkv_fwd(r, lw, k, v, kap, a, B, S):
    C, H, N = WKV_CHUNK, RW_WIDTH // HEAD_DIM, HEAD_DIM
    nc = S // C

    def body(r_ref, lw_ref, k_ref, v_ref, kap_ref, a_ref, y_ref, st_ref, s_scr):
        @pl.when(pl.program_id(1) == 0)
        def _():
            s_scr[...] = jnp.zeros_like(s_scr)

        S0 = s_scr[...]
        st_ref[0, 0] = S0
        args = [_split_heads(ref[...]) for ref in (r_ref, lw_ref, k_ref, v_ref, kap_ref, a_ref)]
        y, S1 = _wkv_chunk(S0, *args)
        s_scr[...] = S1
        y_ref[...] = _merge_heads(y)

    row_spec = pl.BlockSpec((C, RW_WIDTH), lambda b, c: (b * nc + c, 0))
    return pl.pallas_call(
        body, name="wkv_fwd",
        grid=(B, nc),
        in_specs=[row_spec] * 6,
        out_specs=[row_spec, pl.BlockSpec((1, 1, H, N, N), lambda b, c: (b, c, 0, 0, 0))],
        out_shape=[jax.ShapeDtypeStruct((B * S, RW_WIDTH), F32), jax.ShapeDtypeStruct((B, nc, H, N, N), F32)],
        scratch_shapes=[pltpu.VMEM((H, N, N), F32)],
        compiler_params=_params(("arbitrary", "arbitrary")),
    )(r, lw, k, v, kap, a)


def _wkv_bwd(r, lw, k, v, kap, a, states, dy, B, S):
    C, H, N = WKV_CHUNK, RW_WIDTH // HEAD_DIM, HEAD_DIM
    nc = S // C

    def body(r_ref, lw_ref, k_ref, v_ref, kap_ref, a_ref, st_ref, dy_ref,
             dr_ref, dlw_ref, dk_ref, dv_ref, dkap_ref, da_ref, ds_scr):
        @pl.when(pl.program_id(1) == 0)
        def _():
            ds_scr[...] = jnp.zeros_like(ds_scr)

        args = [_split_heads(ref[...]) for ref in (r_ref, lw_ref, k_ref, v_ref, kap_ref, a_ref)]
        _, vjp = jax.vjp(_wkv_chunk, st_ref[0, 0], *args)
        g = vjp((_split_heads(dy_ref[...]), ds_scr[...]))
        ds_scr[...] = g[0]
        for ref, gv in zip((dr_ref, dlw_ref, dk_ref, dv_ref, dkap_ref, da_ref), g[1:]):
            ref[...] = _merge_heads(gv)

    row_spec = pl.BlockSpec((C, RW_WIDTH), lambda b, c: (b * nc + (nc - 1 - c), 0))
    st_spec = pl.BlockSpec((1, 1, H, N, N), lambda b, c: (b, nc - 1 - c, 0, 0, 0))
    return pl.pallas_call(
        body, name="wkv_bwd",
        grid=(B, nc),
        in_specs=[row_spec] * 6 + [st_spec, row_spec],
        out_specs=[row_spec] * 6,
        out_shape=[jax.ShapeDtypeStruct((B * S, RW_WIDTH), F32)] * 6,
        scratch_shapes=[pltpu.VMEM((H, N, N), F32)],
        compiler_params=_params(("arbitrary", "arbitrary")),
    )(r, lw, k, v, kap, a, states, dy)


HBM = pl.BlockSpec(memory_space=pl.ANY)


def _place():
    return lax.axis_index("x"), lax.axis_index("y"), lax.axis_index("c")


def _other_chips(x, y):
    return [(1 - x, y), (x, 1 - y), (1 - x, 1 - y)]


def _all_gather_chips(shards):
    n = len(shards)

    def body(*refs):
        ins, outs = refs[:n], refs[n:2 * n]
        send, recv, local = refs[2 * n:]
        x, y, c = _place()
        me = 2 * x + y
        copies = []
        for w in range(n):
            cp = pltpu.make_async_copy(ins[w], outs[w].at[me], local.at[w])
            cp.start()
            copies.append(cp)
        sends = []
        for w in range(n):
            for j, (px, py) in enumerate(_other_chips(x, y)):
                s = 3 * w + j
                rd = pltpu.make_async_remote_copy(
                    src_ref=ins[w], dst_ref=outs[w].at[me], send_sem=send.at[s], recv_sem=recv.at[s],
                    device_id=(px, py, c), device_id_type=MESH)
                rd.start()
                sends.append(rd)
        for w in range(n):
            for j, (px, py) in enumerate(_other_chips(x, y)):
                s = 3 * w + j
                pltpu.make_async_remote_copy(
                    src_ref=ins[w], dst_ref=outs[w].at[2 * px + py], send_sem=send.at[s], recv_sem=recv.at[s],
                    device_id=(px, py, c), device_id_type=MESH).wait_recv()
        for rd in sends:
            rd.wait_send()
        for cp in copies:
            cp.wait()

    return pl.pallas_call(
        body, name="gather_weights",
        in_specs=[HBM] * n, out_specs=[HBM] * n,
        out_shape=[jax.ShapeDtypeStruct((N_CHIPS,) + s.shape, s.dtype) for s in shards],
        scratch_shapes=[pltpu.SemaphoreType.DMA((3 * n,)), pltpu.SemaphoreType.DMA((3 * n,)),
                        pltpu.SemaphoreType.DMA((n,))],
        compiler_params=pltpu.CompilerParams(has_side_effects=True),
    )(*shards)


def _pair_split(grads):
    n = len(grads)

    def body(*refs):
        ins, mine, theirs = refs[:n], refs[n:2 * n], refs[2 * n:3 * n]
        send, recv, local = refs[3 * n:]
        x, y, c = _place()
        sib = (x, y, 1 - c)
        cps, rds = [], []
        for w in range(n):
            cp = pltpu.make_async_copy(ins[w].at[:, c], mine[w], local.at[w])
            cp.start()
            cps.append(cp)
            rd = pltpu.make_async_remote_copy(
                src_ref=ins[w].at[:, 1 - c], dst_ref=theirs[w], send_sem=send.at[w], recv_sem=recv.at[w],
                device_id=sib, device_id_type=MESH)
            rd.start()
            rds.append(rd)
        for rd in rds:
            rd.wait_recv()
        for rd in rds:
            rd.wait_send()
        for cp in cps:
            cp.wait()

    half = [jax.ShapeDtypeStruct((g.shape[0],) + g.shape[2:], g.dtype) for g in grads]
    res = pl.pallas_call(
        body, name="grad_pair_split",
        in_specs=[HBM] * n, out_specs=[HBM] * (2 * n),
        out_shape=half + half,
        scratch_shapes=[pltpu.SemaphoreType.DMA((n,)), pltpu.SemaphoreType.DMA((n,)), pltpu.SemaphoreType.DMA((n,))],
        compiler_params=pltpu.CompilerParams(has_side_effects=True),
    )(*grads)
    return res[:n], res[n:]


def _chip_scatter(parts):
    n = len(parts)

    def body(*refs):
        ins, outs = refs[:n], refs[n:2 * n]
        send, recv, local = refs[2 * n:]
        x, y, c = _place()
        me = 2 * x + y
        cps, rds = [], []
        for w in range(n):
            cp = pltpu.make_async_copy(ins[w].at[me], outs[w].at[me], local.at[w])
            cp.start()
            cps.append(cp)
            for j, (px, py) in enumerate(_other_chips(x, y)):
                s = 3 * w + j
                rd = pltpu.make_async_remote_copy(
                    src_ref=ins[w].at[2 * px + py], dst_ref=outs[w].at[me], send_sem=send.at[s], recv_sem=recv.at[s],
                    device_id=(px, py, c), device_id_type=MESH)
                rd.start()
                rds.append(rd)
        for w in range(n):
            for j, (px, py) in enumerate(_other_chips(x, y)):
                s = 3 * w + j
                pltpu.make_async_remote_copy(
                    src_ref=ins[w].at[me], dst_ref=outs[w].at[2 * px + py], send_sem=send.at[s], recv_sem=recv.at[s],
                    device_id=(px, py, c), device_id_type=MESH).wait_recv()
        for rd in rds:
            rd.wait_send()
        for cp in cps:
            cp.wait()

    return pl.pallas_call(
        body, name="grad_chip_scatter",
        in_specs=[HBM] * n, out_specs=[HBM] * n,
        out_shape=[jax.ShapeDtypeStruct(p.shape, p.dtype) for p in parts],
        scratch_shapes=[pltpu.SemaphoreType.DMA((3 * n,)), pltpu.SemaphoreType.DMA((3 * n,)),
                        pltpu.SemaphoreType.DMA((n,))],
        compiler_params=pltpu.CompilerParams(has_side_effects=True),
    )(*parts)


def _pair_join(halves):
    n = len(halves)

    def body(*refs):
        ins, outs = refs[:n], refs[n:2 * n]
        send, recv, local = refs[2 * n:]
        x, y, c = _place()
        sib = (x, y, 1 - c)
        cps, rds = [], []
        for w in range(n):
            cp = pltpu.make_async_copy(ins[w], outs[w].at[c], local.at[w])
            cp.start()
            cps.append(cp)
            rd = pltpu.make_async_remote_copy(
                src_ref=ins[w], dst_ref=outs[w].at[c], send_sem=send.at[w], recv_sem=recv.at[w],
                device_id=sib, device_id_type=MESH)
            rd.start()
            rds.append(rd)
        for w in range(n):
            pltpu.make_async_remote_copy(
                src_ref=ins[w], dst_ref=outs[w].at[1 - c], send_sem=send.at[w], recv_sem=recv.at[w],
                device_id=sib, device_id_type=MESH).wait_recv()
        for rd in rds:
            rd.wait_send()
        for cp in cps:
            cp.wait()

    return pl.pallas_call(
        body, name="grad_pair_join",
        in_specs=[HBM] * n, out_specs=[HBM] * n,
        out_shape=[jax.ShapeDtypeStruct((2,) + h.shape, h.dtype) for h in halves],
        scratch_shapes=[pltpu.SemaphoreType.DMA((n,)), pltpu.SemaphoreType.DMA((n,)), pltpu.SemaphoreType.DMA((n,))],
        compiler_params=pltpu.CompilerParams(has_side_effects=True),
    )(*halves)


def _all_reduce_small(packed):
    R = packed.shape[0]

    def body(x_ref, o_ref, buf, send, recv):
        x, y, c = _place()
        me = 4 * x + 2 * y + c
        buf[me] = x_ref[...]
        rds = []
        for rel in range(1, N_DEV):
            fx, fy, fc = (rel >> 2) & 1, (rel >> 1) & 1, rel & 1
            peer = (1 - x if fx else x, 1 - y if fy else y, 1 - c if fc else c)
            rd = pltpu.make_async_remote_copy(
                src_ref=x_ref, dst_ref=buf.at[me], send_sem=send.at[rel - 1], recv_sem=recv.at[rel - 1],
                device_id=peer, device_id_type=MESH)
            rd.start()
            rds.append((rd, peer))
        for rel in range(1, N_DEV):
            rd, (px, py, pc) = rds[rel - 1]
            pltpu.make_async_remote_copy(
                src_ref=x_ref, dst_ref=buf.at[4 * px + 2 * py + pc], send_sem=send.at[rel - 1], recv_sem=recv.at[rel - 1],
                device_id=(px, py, pc), device_id_type=MESH).wait_recv()
        for rd, _ in rds:
            rd.wait_send()
        total = buf[0]
        for d in range(1, N_DEV):
            total = total + buf[d]
        o_ref[...] = total

    return pl.pallas_call(
        body, name="all_reduce_small",
        in_specs=[pl.BlockSpec(memory_space=pltpu.VMEM)],
        out_specs=pl.BlockSpec(memory_space=pltpu.VMEM),
        out_shape=jax.ShapeDtypeStruct(packed.shape, F32),
        scratch_shapes=[pltpu.VMEM((N_DEV, R, LANES), F32), pltpu.SemaphoreType.DMA((N_DEV - 1,)),
                        pltpu.SemaphoreType.DMA((N_DEV - 1,))],
        compiler_params=pltpu.CompilerParams(has_side_effects=True),
    )(packed)


def _add_arrays(name, arrays):
    shape = arrays[0].shape
    C = shape[-1]
    flat = [a.reshape(-1, C) for a in arrays]
    R = flat[0].shape[0]
    tile = _div_tile(R, 256, SUBLANES)
    n = len(flat)

    def body(*refs):
        total = refs[0][...]
        for ref in refs[1:n]:
            total = total + ref[...]
        refs[n][...] = total

    spec = pl.BlockSpec((tile, C), lambda i: (i, 0))
    out = pl.pallas_call(
        body, name=name,
        grid=(R // tile,),
        in_specs=[spec] * n, out_specs=spec,
        out_shape=jax.ShapeDtypeStruct((R, C), F32),
        compiler_params=_params(("parallel",)),
    )(*flat)
    return out.reshape(shape)


def _add_slots(name, stacked):
    return _add_arrays(name, [stacked[i] for i in range(stacked.shape[0])])


def _adamw(name, w, g, m, v):
    R, C = w.shape
    tile = _div_tile(R, 256, SUBLANES)
    c1 = 1.0 / (1.0 - ADAM_B1 ** ADAM_STEP)
    c2 = 1.0 / (1.0 - ADAM_B2 ** ADAM_STEP)

    def body(w_ref, g_ref, m_ref, v_ref, d_ref, nm_ref, nv_ref):
        g_ = g_ref[...]
        nm = ADAM_B1 * m_ref[...] + (1.0 - ADAM_B1) * g_
        nv = ADAM_B2 * v_ref[...] + (1.0 - ADAM_B2) * (g_ * g_)
        d_ref[...] = -ADAM_LR * ((nm * c1) / (jnp.sqrt(nv * c2) + ADAM_EPS) + ADAM_WD * w_ref[...])
        nm_ref[...] = nm
        nv_ref[...] = nv

    spec = pl.BlockSpec((tile, C), lambda i: (i, 0))
    return pl.pallas_call(
        body, name=name,
        grid=(R // tile,),
        in_specs=[spec] * 4, out_specs=[spec] * 3,
        out_shape=[jax.ShapeDtypeStruct((R, C), F32)] * 3,
        compiler_params=_params(("parallel",)),
    )(w, g, m, v)


def _cols_to_shards(full):
    K, N = full.shape
    return full.reshape(K, N_CHIPS, N // N_CHIPS).transpose(1, 0, 2)


def _shards_to_cols(sh):
    return sh.transpose(1, 0, 2).reshape(sh.shape[1], -1)


def _rows_to_shards(full):
    return full.reshape(N_CHIPS, full.shape[0] // N_CHIPS, full.shape[1])


def _shift_tokens(p, B, S):
    p3 = p.reshape(B, S, p.shape[1])
    return jnp.pad(p3, ((0, 0), (1, 0), (0, 0)))[:, :-1].reshape(p.shape)


def _unshift_tokens(d, B, S):
    d3 = d.reshape(B, S, d.shape[1])
    return jnp.pad(d3, ((0, 0), (0, 1), (0, 0)))[:, 1:].reshape(d.shape)


SMALL = ["norm_mix_pre", "b_gate", "mu_rw", "w0", "a0", "k_k", "k_a", "r_k", "lnx_w", "lnx_b",
         "norm_mix_post", "norm_ffn_pre", "norm_ffn_post"]
BIG = ["w_in", "w_up", "a_up", "g_up", "w_sb_out", "w_rw_out", "w_o", "w_ffn_gate", "w_ffn_up", "w_ffn_down"]
ROW_SHARDED = ("w_o", "w_ffn_down")
ORDER = ["norm_mix_pre", "w_in", "b_gate", "mu_rw", "w0", "w_up", "a0", "a_up", "g_up", "k_k", "k_a", "r_k",
         "lnx_w", "lnx_b", "w_sb_out", "w_rw_out", "w_o", "norm_mix_post", "norm_ffn_pre", "w_ffn_gate",
         "w_ffn_up", "w_ffn_down", "norm_ffn_post"]


def _pack_small(vals, extra_rows=0):
    rows = jnp.concatenate([vals[n].reshape(-1, LANES) for n in SMALL], axis=0)
    pad = (-(rows.shape[0] + extra_rows)) % SUBLANES + extra_rows
    return jnp.pad(rows, ((0, pad), (0, 0)))


def _unpack_small(packed, shapes):
    out, r = {}, 0
    for n in SMALL:
        size = 1
        for s in shapes[n]:
            size *= s
        out[n] = packed[r:r + size // LANES].reshape(shapes[n])
        r += size // LANES
    return out


def kernel(x, norm_mix_pre, w_in, b_gate, mu_rw, w0, w_up, a0, a_up, g_up, k_k, k_a, r_k, lnx_w, lnx_b, w_sb_out, w_rw_out, w_o, norm_mix_post, norm_ffn_pre, w_ffn_gate, w_ffn_up, w_ffn_down, norm_ffn_post, loss_target, m_norm_mix_pre, m_w_in, m_b_gate, m_mu_rw, m_w0, m_w_up, m_a0, m_a_up, m_g_up, m_k_k, m_k_a, m_r_k, m_lnx_w, m_lnx_b, m_w_sb_out, m_w_rw_out, m_w_o, m_norm_mix_post, m_norm_ffn_pre, m_w_ffn_gate, m_w_ffn_up, m_w_ffn_down, m_norm_ffn_post, v_norm_mix_pre, v_w_in, v_b_gate, v_mu_rw, v_w0, v_w_up, v_a0, v_a_up, v_g_up, v_k_k, v_k_a, v_r_k, v_lnx_w, v_lnx_b, v_w_sb_out, v_w_rw_out, v_w_o, v_norm_mix_post, v_norm_ffn_pre, v_w_ffn_gate, v_w_ffn_up, v_w_ffn_down, v_norm_ffn_post):
    W = dict(norm_mix_pre=norm_mix_pre, w_in=w_in, b_gate=b_gate, mu_rw=mu_rw, w0=w0, w_up=w_up, a0=a0, a_up=a_up,
             g_up=g_up, k_k=k_k, k_a=k_a, r_k=r_k, lnx_w=lnx_w, lnx_b=lnx_b, w_sb_out=w_sb_out, w_rw_out=w_rw_out,
             w_o=w_o, norm_mix_post=norm_mix_post, norm_ffn_pre=norm_ffn_pre, w_ffn_gate=w_ffn_gate,
             w_ffn_up=w_ffn_up, w_ffn_down=w_ffn_down, norm_ffn_post=norm_ffn_post)
    Mo = dict(norm_mix_pre=m_norm_mix_pre, w_in=m_w_in, b_gate=m_b_gate, mu_rw=m_mu_rw, w0=m_w0, w_up=m_w_up, a0=m_a0,
              a_up=m_a_up, g_up=m_g_up, k_k=m_k_k, k_a=m_k_a, r_k=m_r_k, lnx_w=m_lnx_w, lnx_b=m_lnx_b,
              w_sb_out=m_w_sb_out, w_rw_out=m_w_rw_out, w_o=m_w_o, norm_mix_post=m_norm_mix_post,
              norm_ffn_pre=m_norm_ffn_pre, w_ffn_gate=m_w_ffn_gate, w_ffn_up=m_w_ffn_up, w_ffn_down=m_w_ffn_down,
              norm_ffn_post=m_norm_ffn_post)
    Vo = dict(norm_mix_pre=v_norm_mix_pre, w_in=v_w_in, b_gate=v_b_gate, mu_rw=v_mu_rw, w0=v_w0, w_up=v_w_up, a0=v_a0,
              a_up=v_a_up, g_up=v_g_up, k_k=v_k_k, k_a=v_k_a, r_k=v_r_k, lnx_w=v_lnx_w, lnx_b=v_lnx_b,
              w_sb_out=v_w_sb_out, w_rw_out=v_w_rw_out, w_o=v_w_o, norm_mix_post=v_norm_mix_post,
              norm_ffn_pre=v_norm_ffn_pre, w_ffn_gate=v_w_ffn_gate, w_ffn_up=v_w_ffn_up, w_ffn_down=v_w_ffn_down,
              norm_ffn_post=v_norm_ffn_post)
    shapes = {n: W[n].shape for n in ORDER}
    B, S, D = x.shape
    T = B * S
    x2 = x.reshape(T, D)
    tgt = loss_target.reshape(T, D)
    vec = {n: W[n].reshape(1, -1) for n in SMALL}

    gathered = _all_gather_chips([W[n][0].astype(BF16) for n in BIG])
    full = {}
    for n, gth in zip(BIG, gathered):
        full[n] = gth.reshape(-1, gth.shape[2]) if n in ROW_SHARDED else _shards_to_cols(gth)
    w_sb, w_rw, w_gt = full["w_in"][:, :SB_COLS], full["w_in"][:, SB_COLS:SB_COLS + RW_COLS], full["w_in"][:, SB_COLS + RW_COLS:]
    lora_rows = {"w_up": 0, "a_up": 64, "g_up": 128}
    lora = {n: jnp.pad(full[n], ((r0, LORA_COLS - r0 - full[n].shape[0]), (0, 0))) for n, r0 in lora_rows.items()}
    mu = vec["mu_rw"]
    mu_parts = [mu[:, :512], mu[:, 512:1024], mu[:, 1024:1536], mu[:, 1536:]]
    b1, b2 = vec["b_gate"][:, :D], vec["b_gate"][:, D:]

    (h1,) = _rowwise("norm_mix_pre", _f_norm, [(x2, D, 0)], [vec["norm_mix_pre"]], [(D, BF16)])
    p_sb = _mm("proj_sb", h1, w_sb)
    p_rw = _mm("proj_rw", h1, w_rw)
    p_gt = _mm("proj_gate", h1, w_gt)
    o_sb = _attn_fwd(p_sb, B, S)
    p_prev = _shift_tokens(p_rw, B, S)
    pre_rows = [(p_rw, 512, 0), (p_rw, 512, 1), (p_rw, 512, 2), (p_rw, LORA_COLS, 6),
                (p_prev, 512, 0), (p_prev, 512, 1), (p_prev, 512, 2), (p_prev, LORA_COLS, 6)]
    pre_params = mu_parts + [vec["w0"], lora["w_up"], vec["a0"], lora["a_up"], lora["g_up"], vec["k_k"], vec["k_a"]]
    r_, lw_, k2_, v_, kap_, a_, g_ = _rowwise("rw_pre", _f_rwpre, pre_rows, pre_params, [(512, F32)] * 7, tile=128)
    y_wkv, states = _wkv_fwd(r_, lw_, k2_, v_, kap_, a_, B, S)
    post_rows = [(y_wkv, 512, 0), (r_, 512, 0), (k2_, 512, 0), (v_, 512, 0), (g_, 512, 0)]
    post_params = [vec["lnx_w"], vec["lnx_b"], vec["r_k"]]
    (o_rw,) = _rowwise("rw_post", _f_rwpost, post_rows, post_params, [(512, F32)])
    m1 = _mm("mix_sb_out", o_sb, full["w_sb_out"])
    m2 = _mm("mix_rw_out", o_rw, full["w_rw_out"])
    merge_rows = [(p_gt, D, 0), (p_gt, D, 1), (m1, D, 0), (m2, D, 0)]
    (merged,) = _rowwise("merge", _f_merge, merge_rows, [b1, b2], [(D, F32)])
    u = _mm("mix_out", merged, full["w_o"])
    post1_params = [vec["norm_mix_post"], vec["norm_ffn_pre"]]
    x1, h2 = _rowwise("post_mix", _f_post1, [(x2, D, 0), (u, D, 0)], post1_params, [(D, F32), (D, BF16)])
    ag = _mm("ffn_gate", h2, full["w_ffn_gate"])
    au = _mm("ffn_up", h2, full["w_ffn_up"])
    (sw,) = _rowwise("swiglu", _f_swiglu, [(ag, D_FF, 0), (au, D_FF, 0)], [], [(D_FF, BF16)])
    f = _mm("ffn_down", sw, full["w_ffn_down"])
    loss_part, dx1, df, dg4 = _loss_head(x1, f, tgt, vec["norm_ffn_post"])

    gbig, gsmall = {}, {"norm_ffn_post": dg4}
    dsw = _mm("d_swiglu_out", df, full["w_ffn_down"], tb=True)
    gbig["w_ffn_down"] = _mm("g_ffn_down", sw, df, ta=True)
    (dag, dau), _ = _rowwise_vjp("swiglu_bwd", _f_swiglu, [(ag, D_FF, 0), (au, D_FF, 0)], [], [[dsw]], [True, True], [])
    dh2 = _mm("d_h2_gate", dag, full["w_ffn_gate"], tb=True)
    dh2 = _mm("d_h2_up", dau, full["w_ffn_up"], tb=True, acc=dh2)
    gbig["w_ffn_gate"] = _mm("g_ffn_gate", h2, dag, ta=True)
    gbig["w_ffn_up"] = _mm("g_ffn_up", h2, dau, ta=True)
    (dx_res, du), (dg2, dg3) = _rowwise_vjp("post_mix_bwd", _f_post1, [(x2, D, 0), (u, D, 0)], post1_params,
                                            [[dx1], [dh2]], [True, True], [True, True])
    gsmall["norm_mix_post"], gsmall["norm_ffn_pre"] = dg2, dg3
    dmerged = _mm("d_merged", du, full["w_o"], tb=True)
    gbig["w_o"] = _mm("g_w_o", merged, du, ta=True)
    (dpg1, dpg2, dm1, dm2), (db1, db2) = _rowwise_vjp("merge_bwd", _f_merge, merge_rows, [b1, b2], [[dmerged]],
                                                      [True] * 4, [True, True])
    gsmall["b_gate"] = jnp.concatenate([db1, db2], axis=1)
    do_sb = _mm("d_o_sb", dm1, full["w_sb_out"], tb=True)
    do_rw = _mm("d_o_rw", dm2, full["w_rw_out"], tb=True)
    gbig["w_sb_out"] = _mm("g_sb_out", o_sb, dm1, ta=True)
    gbig["w_rw_out"] = _mm("g_rw_out", o_rw, dm2, ta=True)
    (dy_wkv, dr_a, dk2_a, dv_a, dg_), (dlnx_w, dlnx_b, dr_k) = _rowwise_vjp(
        "rw_post_bwd", _f_rwpost, post_rows, post_params, [[do_rw]], [True] * 5, [True] * 3)
    gsmall["lnx_w"], gsmall["lnx_b"], gsmall["r_k"] = dlnx_w, dlnx_b, dr_k
    dr_b, dlw, dk2_b, dv_b, dkap, da = _wkv_bwd(r_, lw_, k2_, v_, kap_, a_, states, dy_wkv, B, S)
    pre_cts = [[dr_a, dr_b], [dlw], [dk2_a, dk2_b], [dv_a, dv_b], [dkap], [da], [dg_]]
    dpre_rows, dpre_params = _rowwise_vjp("rw_pre_bwd", _f_rwpre, pre_rows, pre_params, pre_cts,
                                          [True] * 8, [True] * 11, tile=128)
    dp_rw = jnp.concatenate(dpre_rows[:4], axis=1) + _unshift_tokens(jnp.concatenate(dpre_rows[4:], axis=1), B, S)
    gsmall["mu_rw"] = jnp.concatenate(dpre_params[:4], axis=1)
    gsmall["w0"], gsmall["a0"], gsmall["k_k"], gsmall["k_a"] = dpre_params[4], dpre_params[6], dpre_params[9], dpre_params[10]
    glora = {"w_up": dpre_params[5][0:64], "a_up": dpre_params[7][64:128], "g_up": dpre_params[8][128:256]}
    dq, dk, dv = _attn_bwd(p_sb, o_sb, do_sb, B, S)
    dh1 = _mm("d_h1_q", dq, w_sb[:, :512], tb=True)
    dh1 = _mm("d_h1_k", dk, w_sb[:, 512:1024], tb=True, acc=dh1)
    dh1 = _mm("d_h1_v", dv, w_sb[:, 1024:], tb=True, acc=dh1)
    dh1 = _mm("d_h1_rw", dp_rw, w_rw, tb=True, acc=dh1)
    dh1 = _mm("d_h1_g1", dpg1, w_gt[:, :D], tb=True, acc=dh1)
    dh1 = _mm("d_h1_g2", dpg2, w_gt[:, D:], tb=True, acc=dh1)
    gbig["w_in"] = jnp.concatenate(
        [_mm("g_in_" + tag, h1, d, ta=True)
         for tag, d in (("q", dq), ("k", dk), ("v", dv), ("rw", dp_rw), ("g1", dpg1), ("g2", dpg2))], axis=1)
    (grad_x2,), (dg1,) = _rowwise_vjp("norm_mix_pre_bwd", _f_norm, [(x2, D, 0)], [vec["norm_mix_pre"]], [[dh1]],
                                      [True], [True], add_to={0: dx_res})
    gsmall["norm_mix_pre"] = dg1
    gbig.update(glora)

    split = []
    for n in BIG:
        g = _rows_to_shards(gbig[n]) if n in ROW_SHARDED else _cols_to_shards(gbig[n])
        split.append(g.reshape(N_CHIPS, 2, g.shape[1] // 2, g.shape[2]))
    mine, theirs = _pair_split(split)
    chip_sums = [_add_arrays("pair_sum_" + n, [a, b]) for n, a, b in zip(BIG, mine, theirs)]
    landed = _chip_scatter(chip_sums)
    halves = [_add_slots("chip_sum_" + n, s) for n, s in zip(BIG, landed)]
    joined = _pair_join(halves)
    grads = {n: j.reshape(W[n].shape[1:]) for n, j in zip(BIG, joined)}

    small_local = _pack_small({n: gsmall[n] for n in SMALL}, extra_rows=1)
    loss_row = small_local.shape[0] - 1
    small_local = small_local.at[loss_row].set(loss_part[0])
    small_sum = _all_reduce_small(small_local)
    loss = small_sum[loss_row, 0]

    delta, new_m, new_v = {}, {}, {}
    for n in BIG:
        d_, m_, v2_ = _adamw("adamw_" + n, W[n][0], grads[n], Mo[n][0], Vo[n][0])
        delta[n], new_m[n], new_v[n] = d_[None], m_[None], v2_[None]
        grads[n] = grads[n][None]
    pk = lambda src: _pack_small({n: src[n] for n in SMALL}, extra_rows=1)
    d_s, m_s, v_s = _adamw("adamw_small", pk(W), small_sum.at[loss_row].set(0.0), pk(Mo), pk(Vo))
    for dst, packed in ((grads, small_sum), (delta, d_s), (new_m, m_s), (new_v, v_s)):
        dst.update(_unpack_small(packed, shapes))

    return (loss, grad_x2.reshape(B, S, D), *[grads[n] for n in ORDER], *[delta[n] for n in ORDER],
            *[new_m[n] for n in ORDER], *[new_v[n] for n in ORDER])
```

```python
import functools

import jax
import jax.numpy as jnp
from jax import lax
from jax.experimental import pallas as pl
from jax.experimental.pallas import tpu as pltpu

F32 = jnp.float32
BF16 = jnp.bfloat16
MESH = pl.DeviceIdType.MESH

D_MODEL = 1024
SB_HEADS = 8
HEAD_DIM = 64
SB_WIDTH = SB_HEADS * HEAD_DIM
RW_WIDTH = 512
LORA_COLS = 256
SB_COLS = 3 * SB_WIDTH
RW_COLS = 3 * RW_WIDTH + LORA_COLS
GATE_COLS = 2 * D_MODEL
D_FF = 2816
RMS_EPS = 1e-6
GN_EPS = HEAD_DIM * 1e-5
WKV_CHUNK = 64
ATTN_QUERIES = 512
ATTN_KEYS = 128
LANES = 128
SUBLANES = 8
N_CHIPS = 4
N_DEV = 8

ADAM_LR = 0.001
ADAM_B1 = 0.9
ADAM_B2 = 0.999
ADAM_EPS = 1e-08
ADAM_WD = 0.01
ADAM_STEP = 10

VMEM_LIMIT = 48 * 1024 * 1024


def _params(sem=None, **kw):
    if sem is not None:
        kw["dimension_semantics"] = sem
    return pltpu.CompilerParams(vmem_limit_bytes=VMEM_LIMIT, **kw)


def _div_tile(dim, pref, mult=LANES):
    if dim <= pref:
        return dim
    t = pref - pref % mult
    while t >= mult:
        if dim % t == 0:
            return t
        t -= mult
    return dim


def _dot(a, b, dims):
    return lax.dot_general(a, b, (dims, ((), ())), preferred_element_type=F32)


def _mm(name, a, b, *, ta=False, tb=False, acc=None, out_dtype=F32):
    if ta:
        K, M = a.shape
    else:
        M, K = a.shape
    N = b.shape[0] if tb else b.shape[1]
    tm, tn, tk = _div_tile(M, 512), _div_tile(N, 1408), _div_tile(K, 1408)
    nk = K // tk
    dims = ((0,) if ta else (1,), (1,) if tb else (0,))
    has_acc = acc is not None

    def body(*refs):
        a_ref, b_ref = refs[0], refs[1]
        o_ref, scr = refs[-2], refs[-1]
        k = pl.program_id(2)
        part = _dot(a_ref[...].astype(BF16), b_ref[...].astype(BF16), dims)

        @pl.when(k == 0)
        def _():
            scr[...] = part + refs[2][...] if has_acc else part

        @pl.when(k > 0)
        def _():
            scr[...] += part

        @pl.when(k == nk - 1)
        def _():
            o_ref[...] = scr[...].astype(o_ref.dtype)

    a_spec = pl.BlockSpec((tk, tm), lambda i, j, k: (k, i)) if ta else pl.BlockSpec((tm, tk), lambda i, j, k: (i, k))
    b_spec = pl.BlockSpec((tn, tk), lambda i, j, k: (j, k)) if tb else pl.BlockSpec((tk, tn), lambda i, j, k: (k, j))
    o_spec = pl.BlockSpec((tm, tn), lambda i, j, k: (i, j))
    return pl.pallas_call(
        body, name=name,
        grid=(M // tm, N // tn, nk),
        in_specs=[a_spec, b_spec] + ([o_spec] if has_acc else []),
        out_specs=o_spec,
        out_shape=jax.ShapeDtypeStruct((M, N), out_dtype),
        scratch_shapes=[pltpu.VMEM((tm, tn), F32)],
        compiler_params=_params(("parallel", "parallel", "arbitrary")),
    )(*([a, b] + ([acc] if has_acc else [])))


def _row_spec(tile, width, colblk):
    return pl.BlockSpec((tile, width), lambda i: (i, colblk))


def _full_spec(shape):
    return pl.BlockSpec(shape, lambda i: (0,) * len(shape))


def _rowwise(name, fn, rows, params, outs, tile=256):
    T = rows[0][0].shape[0]
    tile = min(tile, T)
    n_r, n_p = len(rows), len(params)

    def body(*refs):
        r = [x[...].astype(F32) for x in refs[:n_r]]
        p = [x[...].astype(F32) for x in refs[n_r:n_r + n_p]]
        for o_ref, val in zip(refs[n_r + n_p:], fn(*r, *p)):
            o_ref[...] = val.astype(o_ref.dtype)

    return pl.pallas_call(
        body, name=name,
        grid=(T // tile,),
        in_specs=[_row_spec(tile, w, cb) for _, w, cb in rows] + [_full_spec(p.shape) for p in params],
        out_specs=[_row_spec(tile, w, 0) for w, _ in outs],
        out_shape=[jax.ShapeDtypeStruct((T, w), dt) for w, dt in outs],
        compiler_params=_params(("parallel",)),
    )(*([a for a, _, _ in rows] + list(params)))


def _rowwise_vjp(name, fn, rows, params, cts, need_rows, need_params, add_to=None, tile=256):
    add_to = add_to or {}
    T = rows[0][0].shape[0]
    tile = min(tile, T)
    n_r, n_p = len(rows), len(params)
    ct_flat = [c for group in cts for c in group]
    ct_sizes = [len(group) for group in cts]
    add_idx = sorted(add_to)
    row_out = [i for i in range(n_r) if need_rows[i]]
    par_out = [i for i in range(n_p) if need_params[i]]
    n_ct, n_add = len(ct_flat), len(add_idx)

    def body(*refs):
        pos = 0
        r = [x[...].astype(F32) for x in refs[pos:pos + n_r]]
        pos += n_r
        p = [x[...].astype(F32) for x in refs[pos:pos + n_p]]
        pos += n_p
        ct_vals = [x[...].astype(F32) for x in refs[pos:pos + n_ct]]
        pos += n_ct
        adds = {i: x[...] for i, x in zip(add_idx, refs[pos:pos + n_add])}
        pos += n_add
        drow_refs = refs[pos:pos + len(row_out)]
        pos += len(row_out)
        dpar_refs = refs[pos:pos + len(par_out)]
        ct_in, q = [], 0
        for n in ct_sizes:
            ct_in.append(functools.reduce(lambda u, v: u + v, ct_vals[q:q + n]))
            q += n
        _, vjp = jax.vjp(fn, *r, *p)
        grads = vjp(tuple(ct_in))
        for ref, i in zip(drow_refs, row_out):
            g = grads[i]
            ref[...] = g + adds[i] if i in adds else g

        @pl.when(pl.program_id(0) == 0)
        def _():
            for ref in dpar_refs:
                ref[...] = jnp.zeros_like(ref)

        for ref, i in zip(dpar_refs, par_out):
            ref[...] += grads[n_r + i]

    ct_widths = [c.shape[1] for c in ct_flat]
    in_specs = ([_row_spec(tile, w, cb) for _, w, cb in rows] + [_full_spec(p.shape) for p in params]
                + [_row_spec(tile, w, 0) for w in ct_widths] + [_row_spec(tile, rows[i][1], 0) for i in add_idx])
    out_specs = [_row_spec(tile, rows[i][1], 0) for i in row_out] + [_full_spec(params[i].shape) for i in par_out]
    out_shape = ([jax.ShapeDtypeStruct((T, rows[i][1]), F32) for i in row_out]
                 + [jax.ShapeDtypeStruct(params[i].shape, F32) for i in par_out])
    res = pl.pallas_call(
        body, name=name,
        grid=(T // tile,),
        in_specs=in_specs, out_specs=out_specs, out_shape=out_shape,
        compiler_params=_params(("arbitrary",)),
    )(*([a for a, _, _ in rows] + list(params) + ct_flat + [add_to[i] for i in add_idx]))
    return res[:len(row_out)], res[len(row_out):]


def _sigmoid(x):
    return 0.5 * (jnp.tanh(0.5 * x) + 1.0)


def _softplus(x):
    return jnp.maximum(x, 0.0) + jnp.log(1.0 + jnp.exp(-jnp.abs(x)))


def _rms(x, g):
    return x * lax.rsqrt(jnp.mean(x * x, axis=-1, keepdims=True) + RMS_EPS) * g


def _segsum_impl(x):
    n = x.shape[-1]
    r = lax.shift_right_logical(lax.broadcasted_iota(jnp.int32, (n, n), 0), 6)
    c = lax.shift_right_logical(lax.broadcasted_iota(jnp.int32, (n, n), 1), 6)
    bd = (r == c).astype(BF16)
    hi = x.astype(BF16)
    rest = x - hi.astype(F32)
    mid = rest.astype(BF16)
    lo = (rest - mid.astype(F32)).astype(BF16)
    nn = ((1,), (0,))
    return _dot(hi, bd, nn) + _dot(mid, bd, nn) + _dot(lo, bd, nn)


@jax.custom_vjp
def _segsum(x):
    return _segsum_impl(x)


_segsum.defvjp(lambda x: (_segsum_impl(x), None), lambda _, g: (_segsum_impl(g),))


@jax.custom_vjp
def _mmb(a, w):
    return _dot(a.astype(BF16), w.astype(BF16), ((1,), (0,)))


def _mmb_fwd(a, w):
    return _mmb(a, w), (a, w)


def _mmb_bwd(res, g):
    a, w = res
    gb = g.astype(BF16)
    return _dot(gb, w.astype(BF16), ((1,), (1,))), _dot(a.astype(BF16), gb, ((0,), (0,)))


_mmb.defvjp(_mmb_fwd, _mmb_bwd)


def _f_norm(x, g):
    return (_rms(x, g),)


def _f_post1(x, u, g2, g3):
    x1 = x + _rms(u, g2)
    return x1, _rms(x1, g3)


def _f_swiglu(ag, au):
    return (ag * _sigmoid(ag) * au,)


def _f_merge(pg1, pg2, m1, m2, b1, b2):
    return (_sigmoid(pg1 + b1) * m1 + _sigmoid(pg2 + b2) * m2,)


def _f_out(x1, f, g4):
    return (x1 + _rms(f, g4),)


def _f_rwpre(pr, pk, pv, pz, qr, qk, qv, qz, mur, muk, muv, muz, w0, wup, a0, aup, gup, k_k, k_a):
    r = pr + (qr - pr) * mur
    k = pk + (qk - pk) * muk
    v = pv + (qv - pv) * muv
    z = pz + (qz - pz) * muz
    w_raw = w0 + _mmb(jnp.tanh(z), wup)
    lw = -jnp.exp(-_softplus(-w_raw) - 0.5)
    a = _sigmoid(a0 + _mmb(z, aup))
    g = _mmb(_sigmoid(z), gup)
    kk = k * k_k
    kap = kk * lax.rsqrt(jnp.maximum(_segsum(kk * kk), 1e-24))
    k2 = k * (1.0 + (a - 1.0) * k_a)
    return r, lw, k2, v, kap, a, g


def _f_rwpost(y, r, k2, v, g, lnx_w, lnx_b, r_k):
    inv = 1.0 / HEAD_DIM
    yc = y - _segsum(y) * inv
    var = _segsum(yc * yc) * inv
    yn = yc * lax.rsqrt(var + GN_EPS) * lnx_w + lnx_b
    bonus = _segsum(r * k2 * r_k) * v
    return ((yn + bonus) * g,)


def _loss_head(x1, f, target, g4, tile=256):
    T, D = x1.shape
    tile = min(tile, T)

    def body(x1_ref, f_ref, t_ref, g_ref, loss_ref, dx1_ref, df_ref, dg_ref):
        (y,), vjp = jax.vjp(_f_out, x1_ref[...], f_ref[...], g_ref[...])
        err = y - t_ref[...]
        dx1, df, dg = vjp((err * (1.0 / D),))
        dx1_ref[...] = dx1
        df_ref[...] = df

        @pl.when(pl.program_id(0) == 0)
        def _():
            loss_ref[...] = jnp.zeros_like(loss_ref)
            dg_ref[...] = jnp.zeros_like(dg_ref)

        part = jnp.sum(jnp.sum(err * err, axis=1, keepdims=True), axis=0, keepdims=True) * (0.5 / D)
        loss_ref[...] += jnp.broadcast_to(part, loss_ref.shape)
        dg_ref[...] += dg

    row = pl.BlockSpec((tile, D), lambda i: (i, 0))
    return pl.pallas_call(
        body, name="loss_head",
        grid=(T // tile,),
        in_specs=[row, row, row, _full_spec(g4.shape)],
        out_specs=[_full_spec((SUBLANES, LANES)), row, row, _full_spec(g4.shape)],
        out_shape=[jax.ShapeDtypeStruct((SUBLANES, LANES), F32), jax.ShapeDtypeStruct((T, D), F32),
                   jax.ShapeDtypeStruct((T, D), F32), jax.ShapeDtypeStruct(g4.shape, F32)],
        compiler_params=_params(("arbitrary",)),
    )(x1, f, target, g4)


def _nn(a, b):
    return _dot(a, b, ((1,), (0,)))


def _nt(a, b):
    return _dot(a, b, ((1,), (1,)))


def _tn(a, b):
    return _dot(a, b, ((0,), (0,)))


def _split_dot(x, u):
    hi = x.astype(BF16)
    lo = (x - hi.astype(F32)).astype(BF16)
    return _nn(hi, u) + _nn(lo, u)


def _head_masks():
    lane = lax.broadcasted_iota(jnp.int32, (1, LANES), 1)
    return [((lane >= h * HEAD_DIM) & (lane < (h + 1) * HEAD_DIM)).astype(F32) for h in range(LANES // HEAD_DIM)]


def _key_tri(op):
    row = lax.broadcasted_iota(jnp.int32, (ATTN_KEYS, ATTN_KEYS), 0)
    col = lax.broadcasted_iota(jnp.int32, (ATTN_KEYS, ATTN_KEYS), 1)
    return op(row, col).astype(BF16)


def _causal(qb, sub):
    row = lax.broadcasted_iota(jnp.int32, (qb, ATTN_KEYS), 0)
    col = lax.broadcasted_iota(jnp.int32, (qb, ATTN_KEYS), 1)
    return col + sub * ATTN_KEYS < row


def _sb_weights(qh, kb, c_fail, u_gt, strict, scale):
    z = _nt(qh, kb) * scale
    L = jnp.minimum(-z, 0.0) - jnp.log(1.0 + jnp.exp(-jnp.abs(z)))
    Lm = L if strict is None else jnp.where(strict, L, 0.0)
    A = jnp.exp(z + L + c_fail + _split_dot(Lm, u_gt))
    if strict is not None:
        A = jnp.where(strict, A, 0.0)
    return z, L, Lm, A


def _attn_specs(S, qb):
    nq = S // qb
    q_spec = pl.BlockSpec((qb, LANES), lambda b, p, i: (b * nq + i, p))
    k_spec = pl.BlockSpec((S, LANES), lambda b, p, i: (b, SB_WIDTH // LANES + p))
    v_spec = pl.BlockSpec((S, LANES), lambda b, p, i: (b, 2 * SB_WIDTH // LANES + p))
    seq = pl.BlockSpec((S, LANES), lambda b, p, i: (b, p))
    return q_spec, k_spec, v_spec, q_spec, seq


def _key_walk(i, qb, block, carry):
    per = qb // ATTN_KEYS
    for sub in reversed(range(per)):
        carry = block(i * per + sub, carry, _causal(qb, sub))
    return lax.fori_loop(0, i * per, lambda j, c: block(i * per - 1 - j, c, None), carry)


def _attn_fwd(proj, B, S):
    qb = min(ATTN_QUERIES, S)
    scale = HEAD_DIM ** -0.5

    def body(q_ref, k_ref, v_ref, o_ref):
        i = pl.program_id(2)
        masks = _head_masks()
        u_gt = _key_tri(lambda r, c: r > c)
        q = q_ref[...]
        qhs = [(q * m).astype(BF16) for m in masks]

        def block(J, carry, strict_mask):
            acc, cs = carry
            r0 = pl.multiple_of(J * ATTN_KEYS, ATTN_KEYS)
            kb = k_ref[pl.ds(r0, ATTN_KEYS), :].astype(BF16)
            vb = v_ref[pl.ds(r0, ATTN_KEYS), :]
            new_cs = []
            for h, m in enumerate(masks):
                _, _, Lm, A = _sb_weights(qhs[h], kb, cs[h], u_gt, strict_mask, scale)
                acc = acc + _nn(A.astype(BF16), (vb * m).astype(BF16))
                new_cs.append(cs[h] + jnp.sum(Lm, axis=1, keepdims=True))
            return acc, tuple(new_cs)

        zero_c = tuple(jnp.zeros((qb, 1), F32) for _ in masks)
        carry = _key_walk(i, qb, block, (jnp.zeros((qb, LANES), F32), zero_c))
        o_ref[...] = carry[0]

    q_spec, k_spec, v_spec, blk, _ = _attn_specs(S, qb)
    return pl.pallas_call(
        body, name="sb_attn_fwd",
        grid=(B, SB_WIDTH // LANES, S // qb),
        in_specs=[q_spec, k_spec, v_spec],
        out_specs=blk,
        out_shape=jax.ShapeDtypeStruct((B * S, SB_WIDTH), F32),
        compiler_params=_params(("parallel", "parallel", "arbitrary")),
    )(proj, proj, proj)


def _attn_bwd(proj, o, do, B, S):
    qb = min(ATTN_QUERIES, S)
    scale = HEAD_DIM ** -0.5

    def body(q_ref, k_ref, v_ref, o_ref, do_ref, dq_ref, dk_ref, dv_ref):
        i = pl.program_id(2)

        @pl.when(i == 0)
        def _():
            dk_ref[...] = jnp.zeros_like(dk_ref)
            dv_ref[...] = jnp.zeros_like(dv_ref)

        masks = _head_masks()
        u_gt = _key_tri(lambda r, c: r > c)
        u_ge = _key_tri(lambda r, c: r >= c)
        q = q_ref[...]
        do_b = do_ref[...].astype(BF16)
        od = o_ref[...] * do_b.astype(F32)
        qhs = [(q * m).astype(BF16) for m in masks]
        dohs = [(do_b * m.astype(BF16)) for m in masks]
        totals = [jnp.sum(od * m, axis=1, keepdims=True) for m in masks]

        def block(J, carry, strict_mask):
            dq, c_fail, c_p = carry
            r0 = pl.multiple_of(J * ATTN_KEYS, ATTN_KEYS)
            kb32 = k_ref[pl.ds(r0, ATTN_KEYS), :]
            kb = kb32.astype(BF16)
            vb = v_ref[pl.ds(r0, ATTN_KEYS), :].astype(BF16)
            dk_blk = jnp.zeros((ATTN_KEYS, LANES), F32)
            dv_blk = jnp.zeros((ATTN_KEYS, LANES), F32)
            new_fail, new_p = [], []
            for h, m in enumerate(masks):
                z, L, Lm, A = _sb_weights(qhs[h], kb, c_fail[h], u_gt, strict_mask, scale)
                Ab = A.astype(BF16)
                P = Ab.astype(F32) * _nt(dohs[h], vb)
                after = c_p[h] + _split_dot(P, u_ge)
                sig = jnp.exp(z + L)
                dz = (P * (1.0 - sig) - sig * (totals[h] - after)) * scale
                if strict_mask is not None:
                    dz = jnp.where(strict_mask, dz, 0.0)
                dzb = dz.astype(BF16)
                dv_blk = dv_blk + _tn(Ab, dohs[h])
                dk_blk = dk_blk + _tn(dzb, qhs[h])
                dq = dq + _nn(dzb, (kb32 * m).astype(BF16))
                new_fail.append(c_fail[h] + jnp.sum(Lm, axis=1, keepdims=True))
                new_p.append(c_p[h] + jnp.sum(P, axis=1, keepdims=True))
            dk_ref[pl.ds(r0, ATTN_KEYS), :] += dk_blk
            dv_ref[pl.ds(r0, ATTN_KEYS), :] += dv_blk
            return dq, tuple(new_fail), tuple(new_p)

        zc = tuple(jnp.zeros((qb, 1), F32) for _ in masks)
        carry = _key_walk(i, qb, block, (jnp.zeros((qb, LANES), F32), zc, zc))
        dq_ref[...] = carry[0]

    q_spec, k_spec, v_spec, blk, seq = _attn_specs(S, qb)
    return pl.pallas_call(
        body, name="sb_attn_bwd",
        grid=(B, SB_WIDTH // LANES, S // qb),
        in_specs=[q_spec, k_spec, v_spec, blk, blk],
        out_specs=[blk, seq, seq],
        out_shape=[jax.ShapeDtypeStruct((B * S, SB_WIDTH), F32)] * 3,
        compiler_params=_params(("parallel", "parallel", "arbitrary")),
    )(proj, proj, proj, o, do)


_BATCHED = {"nn": "gmk,gkn->gmn", "nt": "gmk,gnk->gmn", "tn": "gkm,gkn->gmn"}


def _bdot_raw(a, b, kind, passes):
    e = functools.partial(jnp.einsum, _BATCHED[kind], preferred_element_type=F32)
    ah, bh = a.astype(BF16), b.astype(BF16)
    if passes == 1:
        return e(ah, bh)
    al, bl = (a - ah.astype(F32)).astype(BF16), (b - bh.astype(F32)).astype(BF16)
    return e(ah, bh) + e(ah, bl) + e(al, bh)


@functools.partial(jax.custom_vjp, nondiff_argnums=(2, 3))
def _bdot(a, b, kind, passes):
    return _bdot_raw(a, b, kind, passes)


def _bdot_fwd(a, b, kind, passes):
    return _bdot_raw(a, b, kind, passes), (a, b)


def _bdot_bwd(kind, passes, res, g):
    a, b = res
    if kind == "nn":
        return _bdot_raw(g, b, "nt", passes), _bdot_raw(a, g, "tn", passes)
    if kind == "nt":
        return _bdot_raw(g, b, "nn", passes), _bdot_raw(g, a, "tn", passes)
    return _bdot_raw(b, g, "nt", passes), _bdot_raw(a, g, "nn", passes)


_bdot.defvjp(_bdot_fwd, _bdot_bwd)


def _wkv_chunk(S0, r, lw, k, v, kap, a):
    G, C, N = r.shape
    row = lax.broadcasted_iota(jnp.int32, (C, C), 0)
    col = lax.broadcasted_iota(jnp.int32, (C, C), 1)
    incl = (col <= row).astype(F32)
    strict = (col < row).astype(F32)
    cum = _bdot(jnp.broadcast_to(incl, (G, C, C)), lw, "nn", 3)
    e_pos = jnp.exp(cum)
    e_neg = jnp.exp(-cum)
    al = -kap * jnp.exp(cum - lw)
    be = kap * a * e_neg
    kt = k * e_neg
    rt = r * e_pos
    m_ab = _bdot(al, be, "nt", 3) * strict
    m_ak = _bdot(al, kt, "nt", 3) * strict
    m_rb = _bdot(rt, be, "nt", 3) * incl
    m_rk = _bdot(rt, kt, "nt", 3) * incl
    sa = _bdot(al, S0, "nt", 3) + _bdot(m_ak, v, "nn", 3)
    p = m_ab
    steps = max(1, (C - 1).bit_length())
    for j in range(steps):
        sa = sa + _bdot(p, sa, "nn", 1)
        if j + 1 < steps:
            p = _bdot(p, p, "nn", 1)
    y = _bdot(rt, S0, "nt", 3) + _bdot(m_rb, sa, "nn", 3) + _bdot(m_rk, v, "nn", 3)
    S1 = (S0 + _bdot(sa, be, "tn", 1) + _bdot(v, kt, "tn", 1)) * e_pos[:, C - 1:C, :]
    return y, S1


def _split_heads(x):
    return jnp.stack([x[:, h * HEAD_DIM:(h + 1) * HEAD_DIM] for h in range(x.shape[1] // HEAD_DIM)], axis=0)


def _merge_heads(x):
    return jnp.concatenate([x[h] for h in range(x.shape[0])], axis=1)


def _wkv_fwd(r, lw, k, v, kap, a, B, S):
    C, H, N = WKV_CHUNK, RW_WIDTH // HEAD_DIM, HEAD_DIM
    nc = S // C

    def body(r_ref, lw_ref, k_ref, v_ref, kap_ref, a_ref, y_ref, st_ref, s_scr):
        @pl.when(pl.program_id(1) == 0)
        def _():
            s_scr[...] = jnp.zeros_like(s_scr)

        S0 = s_scr[...]
        st_ref[0, 0] = S0
        args = [_split_heads(ref[...]) for ref in (r_ref, lw_ref, k_ref, v_ref, kap_ref, a_ref)]
        y, S1 = _wkv_chunk(S0, *args)
        s_scr[...] = S1
        y_ref[...] = _merge_heads(y)

    row_spec = pl.BlockSpec((C, RW_WIDTH), lambda b, c: (b * nc + c, 0))
    return pl.pallas_call(
        body, name="wkv_fwd",
        grid=(B, nc),
        in_specs=[row_spec] * 6,
        out_specs=[row_spec, pl.BlockSpec((1, 1, H, N, N), lambda b, c: (b, c, 0, 0, 0))],
        out_shape=[jax.ShapeDtypeStruct((B * S, RW_WIDTH), F32), jax.ShapeDtypeStruct((B, nc, H, N, N), F32)],
        scratch_shapes=[pltpu.VMEM((H, N, N), F32)],
        compiler_params=_params(("arbitrary", "arbitrary")),
    )(r, lw, k, v, kap, a)


def _wkv_bwd(r, lw, k, v, kap, a, states, dy, B, S):
    C, H, N = WKV_CHUNK, RW_WIDTH // HEAD_DIM, HEAD_DIM
    nc = S // C

    def body(r_ref, lw_ref, k_ref, v_ref, kap_ref, a_ref, st_ref, dy_ref,
             dr_ref, dlw_ref, dk_ref, dv_ref, dkap_ref, da_ref, ds_scr):
        @pl.when(pl.program_id(1) == 0)
        def _():
            ds_scr[...] = jnp.zeros_like(ds_scr)

        args = [_split_heads(ref[...]) for ref in (r_ref, lw_ref, k_ref, v_ref, kap_ref, a_ref)]
        _, vjp = jax.vjp(_wkv_chunk, st_ref[0, 0], *args)
        g = vjp((_split_heads(dy_ref[...]), ds_scr[...]))
        ds_scr[...] = g[0]
        for ref, gv in zip((dr_ref, dlw_ref, dk_ref, dv_ref, dkap_ref, da_ref), g[1:]):
            ref[...] = _merge_heads(gv)

    row_spec = pl.BlockSpec((C, RW_WIDTH), lambda b, c: (b * nc + (nc - 1 - c), 0))
    st_spec = pl.BlockSpec((1, 1, H, N, N), lambda b, c: (b, nc - 1 - c, 0, 0, 0))
    return pl.pallas_call(
        body, name="wkv_bwd",
        grid=(B, nc),
        in_specs=[row_spec] * 6 + [st_spec, row_spec],
        out_specs=[row_spec] * 6,
        out_shape=[jax.ShapeDtypeStruct((B * S, RW_WIDTH), F32)] * 6,
        scratch_shapes=[pltpu.VMEM((H, N, N), F32)],
        compiler_params=_params(("arbitrary", "arbitrary")),
    )(r, lw, k, v, kap, a, states, dy)


HBM = pl.BlockSpec(memory_space=pl.ANY)


def _place():
    return lax.axis_index("x"), lax.axis_index("y"), lax.axis_index("c")


def _other_chips(x, y):
    return [(1 - x, y), (x, 1 - y), (1 - x, 1 - y)]


def _all_gather_chips(shards):
    n = len(shards)

    def body(*refs):
        ins, outs = refs[:n], refs[n:2 * n]
        send, recv, local = refs[2 * n:]
        x, y, c = _place()
        me = 2 * x + y
        copies = []
        for w in range(n):
            cp = pltpu.make_async_copy(ins[w], outs[w].at[me], local.at[w])
            cp.start()
            copies.append(cp)
        sends = []
        for w in range(n):
            for j, (px, py) in enumerate(_other_chips(x, y)):
                s = 3 * w + j
                rd = pltpu.make_async_remote_copy(
                    src_ref=ins[w], dst_ref=outs[w].at[me], send_sem=send.at[s], recv_sem=recv.at[s],
                    device_id=(px, py, c), device_id_type=MESH)
                rd.start()
                sends.append(rd)
        for w in range(n):
            for j, (px, py) in enumerate(_other_chips(x, y)):
                s = 3 * w + j
                pltpu.make_async_remote_copy(
                    src_ref=ins[w], dst_ref=outs[w].at[2 * px + py], send_sem=send.at[s], recv_sem=recv.at[s],
                    device_id=(px, py, c), device_id_type=MESH).wait_recv()
        for rd in sends:
            rd.wait_send()
        for cp in copies:
            cp.wait()

    return pl.pallas_call(
        body, name="gather_weights",
        in_specs=[HBM] * n, out_specs=[HBM] * n,
        out_shape=[jax.ShapeDtypeStruct((N_CHIPS,) + s.shape, s.dtype) for s in shards],
        scratch_shapes=[pltpu.SemaphoreType.DMA((3 * n,)), pltpu.SemaphoreType.DMA((3 * n,)),
                        pltpu.SemaphoreType.DMA((n,))],
        compiler_params=pltpu.CompilerParams(has_side_effects=True),
    )(*shards)


def _pair_split(grads):
    n = len(grads)

    def body(*refs):
        ins, theirs = refs[:n], refs[n:2 * n]
        send, recv = refs[2 * n:]
        x, y, c = _place()
        sib = (x, y, 1 - c)
        rds = []
        for w in range(n):
            rd = pltpu.make_async_remote_copy(
                src_ref=ins[w].at[:, 1 - c], dst_ref=theirs[w], send_sem=send.at[w], recv_sem=recv.at[w],
                device_id=sib, device_id_type=MESH)
            rd.start()
            rds.append(rd)
        for rd in rds:
            rd.wait_recv()
        for rd in rds:
            rd.wait_send()

    return pl.pallas_call(
        body, name="grad_pair_split",
        in_specs=[HBM] * n, out_specs=[HBM] * n,
        out_shape=[jax.ShapeDtypeStruct((g.shape[0],) + g.shape[2:], g.dtype) for g in grads],
        scratch_shapes=[pltpu.SemaphoreType.DMA((n,)), pltpu.SemaphoreType.DMA((n,))],
        compiler_params=pltpu.CompilerParams(has_side_effects=True),
    )(*grads)


def _chip_scatter(parts):
    n = len(parts)

    def body(*refs):
        ins, outs = refs[:n], refs[n:2 * n]
        send, recv, local = refs[2 * n:]
        x, y, c = _place()
        me = 2 * x + y
        cps, rds = [], []
        for w in range(n):
            cp = pltpu.make_async_copy(ins[w].at[me], outs[w].at[me], local.at[w])
            cp.start()
            cps.append(cp)
            for j, (px, py) in enumerate(_other_chips(x, y)):
                s = 3 * w + j
                rd = pltpu.make_async_remote_copy(
                    src_ref=ins[w].at[2 * px + py], dst_ref=outs[w].at[me], send_sem=send.at[s], recv_sem=recv.at[s],
                    device_id=(px, py, c), device_id_type=MESH)
                rd.start()
                rds.append(rd)
        for w in range(n):
            for j, (px, py) in enumerate(_other_chips(x, y)):
                s = 3 * w + j
                pltpu.make_async_remote_copy(
                    src_ref=ins[w].at[me], dst_ref=outs[w].at[2 * px + py], send_sem=send.at[s], recv_sem=recv.at[s],
                    device_id=(px, py, c), device_id_type=MESH).wait_recv()
        for rd in rds:
            rd.wait_send()
        for cp in cps:
            cp.wait()

    return pl.pallas_call(
        body, name="grad_chip_scatter",
        in_specs=[HBM] * n, out_specs=[HBM] * n,
        out_shape=[jax.ShapeDtypeStruct(p.shape, p.dtype) for p in parts],
        scratch_shapes=[pltpu.SemaphoreType.DMA((3 * n,)), pltpu.SemaphoreType.DMA((3 * n,)),
                        pltpu.SemaphoreType.DMA((n,))],
        compiler_params=pltpu.CompilerParams(has_side_effects=True),
    )(*parts)


def _pair_join(bufs):
    n = len(bufs)

    def body(*refs):
        ins, outs = refs[:n], refs[n:2 * n]
        send, recv = refs[2 * n:]
        x, y, c = _place()
        sib = (x, y, 1 - c)
        rds = []
        for w in range(n):
            rd = pltpu.make_async_remote_copy(
                src_ref=ins[w].at[c], dst_ref=outs[w].at[c], send_sem=send.at[w], recv_sem=recv.at[w],
                device_id=sib, device_id_type=MESH)
            rd.start()
            rds.append(rd)
        for w in range(n):
            pltpu.make_async_remote_copy(
                src_ref=ins[w].at[c], dst_ref=outs[w].at[1 - c], send_sem=send.at[w], recv_sem=recv.at[w],
                device_id=sib, device_id_type=MESH).wait_recv()
        for rd in rds:
            rd.wait_send()

    return pl.pallas_call(
        body, name="grad_pair_join",
        in_specs=[HBM] * n, out_specs=[HBM] * n,
        out_shape=[jax.ShapeDtypeStruct(b.shape, b.dtype) for b in bufs],
        input_output_aliases={w: w for w in range(n)},
        scratch_shapes=[pltpu.SemaphoreType.DMA((n,)), pltpu.SemaphoreType.DMA((n,))],
        compiler_params=pltpu.CompilerParams(has_side_effects=True),
    )(*bufs)


def _all_reduce_small(packed):
    R = packed.shape[0]

    def body(x_ref, o_ref, buf, send, recv):
        x, y, c = _place()
        me = 4 * x + 2 * y + c
        buf[me] = x_ref[...]
        rds = []
        for rel in range(1, N_DEV):
            fx, fy, fc = (rel >> 2) & 1, (rel >> 1) & 1, rel & 1
            peer = (1 - x if fx else x, 1 - y if fy else y, 1 - c if fc else c)
            rd = pltpu.make_async_remote_copy(
                src_ref=x_ref, dst_ref=buf.at[me], send_sem=send.at[rel - 1], recv_sem=recv.at[rel - 1],
                device_id=peer, device_id_type=MESH)
            rd.start()
            rds.append((rd, peer))
        for rel in range(1, N_DEV):
            rd, (px, py, pc) = rds[rel - 1]
            pltpu.make_async_remote_copy(
                src_ref=x_ref, dst_ref=buf.at[4 * px + 2 * py + pc], send_sem=send.at[rel - 1], recv_sem=recv.at[rel - 1],
                device_id=(px, py, pc), device_id_type=MESH).wait_recv()
        for rd, _ in rds:
            rd.wait_send()
        total = buf[0]
        for d in range(1, N_DEV):
            total = total + buf[d]
        o_ref[...] = total

    return pl.pallas_call(
        body, name="all_reduce_small",
        in_specs=[pl.BlockSpec(memory_space=pltpu.VMEM)],
        out_specs=pl.BlockSpec(memory_space=pltpu.VMEM),
        out_shape=jax.ShapeDtypeStruct(packed.shape, F32),
        scratch_shapes=[pltpu.VMEM((N_DEV, R, LANES), F32), pltpu.SemaphoreType.DMA((N_DEV - 1,)),
                        pltpu.SemaphoreType.DMA((N_DEV - 1,))],
        compiler_params=pltpu.CompilerParams(has_side_effects=True),
    )(packed)


def _pair_sum(name, split, theirs, core):
    n_chip, _, Rh, C = split.shape
    tile = _div_tile(Rh, 256, SUBLANES)
    nt = Rh // tile

    def body(core_ref, a_ref, b_ref, o_ref):
        o_ref[...] = a_ref[...] + b_ref[...]

    return pl.pallas_call(
        body, name=name,
        grid_spec=pltpu.PrefetchScalarGridSpec(
            num_scalar_prefetch=1,
            grid=(n_chip, nt),
            in_specs=[pl.BlockSpec((None, None, tile, C), lambda j, i, core_ref: (j, core_ref[0], i, 0)),
                      pl.BlockSpec((None, tile, C), lambda j, i, core_ref: (j, i, 0))],
            out_specs=pl.BlockSpec((None, tile, C), lambda j, i, core_ref: (j, i, 0)),
        ),
        out_shape=jax.ShapeDtypeStruct((n_chip, Rh, C), F32),
        compiler_params=_params(("parallel", "parallel")),
    )(core, split, theirs)


def _chip_sum(name, landed, core):
    n_chip, Rh, C = landed.shape
    tile = _div_tile(Rh, 256, SUBLANES)

    def body(core_ref, *refs):
        total = refs[0][...]
        for ref in refs[1:n_chip]:
            total = total + ref[...]
        refs[n_chip][...] = total

    slot = lambda j: pl.BlockSpec((None, tile, C), lambda i, core_ref: (j, i, 0))
    return pl.pallas_call(
        body, name=name,
        grid_spec=pltpu.PrefetchScalarGridSpec(
            num_scalar_prefetch=1,
            grid=(Rh // tile,),
            in_specs=[slot(j) for j in range(n_chip)],
            out_specs=pl.BlockSpec((None, tile, C), lambda i, core_ref: (core_ref[0], i, 0)),
        ),
        out_shape=jax.ShapeDtypeStruct((2, Rh, C), F32),
        compiler_params=_params(("parallel",)),
    )(core, *([landed] * n_chip))


def _adamw(name, w, g, m, v):
    R, C = w.shape
    tile = _div_tile(R, 256, SUBLANES)
    c1 = 1.0 / (1.0 - ADAM_B1 ** ADAM_STEP)
    c2 = 1.0 / (1.0 - ADAM_B2 ** ADAM_STEP)

    def body(w_ref, g_ref, m_ref, v_ref, d_ref, nm_ref, nv_ref):
        g_ = g_ref[...]
        nm = ADAM_B1 * m_ref[...] + (1.0 - ADAM_B1) * g_
        nv = ADAM_B2 * v_ref[...] + (1.0 - ADAM_B2) * (g_ * g_)
        d_ref[...] = -ADAM_LR * ((nm * c1) / (jnp.sqrt(nv * c2) + ADAM_EPS) + ADAM_WD * w_ref[...])
        nm_ref[...] = nm
        nv_ref[...] = nv

    spec = pl.BlockSpec((tile, C), lambda i: (i, 0))
    return pl.pallas_call(
        body, name=name,
        grid=(R // tile,),
        in_specs=[spec] * 4, out_specs=[spec] * 3,
        out_shape=[jax.ShapeDtypeStruct((R, C), F32)] * 3,
        compiler_params=_params(("parallel",)),
    )(w, g, m, v)


def _cols_to_shards(full):
    K, N = full.shape
    return full.reshape(K, N_CHIPS, N // N_CHIPS).transpose(1, 0, 2)


def _shards_to_cols(sh):
    return sh.transpose(1, 0, 2).reshape(sh.shape[1], -1)


def _rows_to_shards(full):
    return full.reshape(N_CHIPS, full.shape[0] // N_CHIPS, full.shape[1])


def _shift_tokens(p, B, S):
    p3 = p.reshape(B, S, p.shape[1])
    return jnp.pad(p3, ((0, 0), (1, 0), (0, 0)))[:, :-1].reshape(p.shape)


def _unshift_tokens(d, B, S):
    d3 = d.reshape(B, S, d.shape[1])
    return jnp.pad(d3, ((0, 0), (0, 1), (0, 0)))[:, 1:].reshape(d.shape)


SMALL = ["norm_mix_pre", "b_gate", "mu_rw", "w0", "a0", "k_k", "k_a", "r_k", "lnx_w", "lnx_b",
         "norm_mix_post", "norm_ffn_pre", "norm_ffn_post"]
BIG = ["w_in", "w_up", "a_up", "g_up", "w_sb_out", "w_rw_out", "w_o", "w_ffn_gate", "w_ffn_up", "w_ffn_down"]
ROW_SHARDED = ("w_o", "w_ffn_down")
ORDER = ["norm_mix_pre", "w_in", "b_gate", "mu_rw", "w0", "w_up", "a0", "a_up", "g_up", "k_k", "k_a", "r_k",
         "lnx_w", "lnx_b", "w_sb_out", "w_rw_out", "w_o", "norm_mix_post", "norm_ffn_pre", "w_ffn_gate",
         "w_ffn_up", "w_ffn_down", "norm_ffn_post"]


def _pack_small(vals, extra_rows=0):
    rows = jnp.concatenate([vals[n].reshape(-1, LANES) for n in SMALL], axis=0)
    pad = (-(rows.shape[0] + extra_rows)) % SUBLANES + extra_rows
    return jnp.pad(rows, ((0, pad), (0, 0)))


def _unpack_small(packed, shapes):
    out, r = {}, 0
    for n in SMALL:
        size = 1
        for s in shapes[n]:
            size *= s
        out[n] = packed[r:r + size // LANES].reshape(shapes[n])
        r += size // LANES
    return out


def kernel(x, norm_mix_pre, w_in, b_gate, mu_rw, w0, w_up, a0, a_up, g_up, k_k, k_a, r_k, lnx_w, lnx_b, w_sb_out, w_rw_out, w_o, norm_mix_post, norm_ffn_pre, w_ffn_gate, w_ffn_up, w_ffn_down, norm_ffn_post, loss_target, m_norm_mix_pre, m_w_in, m_b_gate, m_mu_rw, m_w0, m_w_up, m_a0, m_a_up, m_g_up, m_k_k, m_k_a, m_r_k, m_lnx_w, m_lnx_b, m_w_sb_out, m_w_rw_out, m_w_o, m_norm_mix_post, m_norm_ffn_pre, m_w_ffn_gate, m_w_ffn_up, m_w_ffn_down, m_norm_ffn_post, v_norm_mix_pre, v_w_in, v_b_gate, v_mu_rw, v_w0, v_w_up, v_a0, v_a_up, v_g_up, v_k_k, v_k_a, v_r_k, v_lnx_w, v_lnx_b, v_w_sb_out, v_w_rw_out, v_w_o, v_norm_mix_post, v_norm_ffn_pre, v_w_ffn_gate, v_w_ffn_up, v_w_ffn_down, v_norm_ffn_post):
    W = dict(norm_mix_pre=norm_mix_pre, w_in=w_in, b_gate=b_gate, mu_rw=mu_rw, w0=w0, w_up=w_up, a0=a0, a_up=a_up,
             g_up=g_up, k_k=k_k, k_a=k_a, r_k=r_k, lnx_w=lnx_w, lnx_b=lnx_b, w_sb_out=w_sb_out, w_rw_out=w_rw_out,
             w_o=w_o, norm_mix_post=norm_mix_post, norm_ffn_pre=norm_ffn_pre, w_ffn_gate=w_ffn_gate,
             w_ffn_up=w_ffn_up, w_ffn_down=w_ffn_down, norm_ffn_post=norm_ffn_post)
    Mo = dict(norm_mix_pre=m_norm_mix_pre, w_in=m_w_in, b_gate=m_b_gate, mu_rw=m_mu_rw, w0=m_w0, w_up=m_w_up, a0=m_a0,
              a_up=m_a_up, g_up=m_g_up, k_k=m_k_k, k_a=m_k_a, r_k=m_r_k, lnx_w=m_lnx_w, lnx_b=m_lnx_b,
              w_sb_out=m_w_sb_out, w_rw_out=m_w_rw_out, w_o=m_w_o, norm_mix_post=m_norm_mix_post,
              norm_ffn_pre=m_norm_ffn_pre, w_ffn_gate=m_w_ffn_gate, w_ffn_up=m_w_ffn_up, w_ffn_down=m_w_ffn_down,
              norm_ffn_post=m_norm_ffn_post)
    Vo = dict(norm_mix_pre=v_norm_mix_pre, w_in=v_w_in, b_gate=v_b_gate, mu_rw=v_mu_rw, w0=v_w0, w_up=v_w_up, a0=v_a0,
              a_up=v_a_up, g_up=v_g_up, k_k=v_k_k, k_a=v_k_a, r_k=v_r_k, lnx_w=v_lnx_w, lnx_b=v_lnx_b,
              w_sb_out=v_w_sb_out, w_rw_out=v_w_rw_out, w_o=v_w_o, norm_mix_post=v_norm_mix_post,
              norm_ffn_pre=v_norm_ffn_pre, w_ffn_gate=v_w_ffn_gate, w_ffn_up=v_w_ffn_up, w_ffn_down=v_w_ffn_down,
              norm_ffn_post=v_norm_ffn_post)
    shapes = {n: W[n].shape for n in ORDER}
    B, S, D = x.shape
    T = B * S
    x2 = x.reshape(T, D)
    tgt = loss_target.reshape(T, D)
    vec = {n: W[n].reshape(1, -1) for n in SMALL}

    gathered = _all_gather_chips([W[n][0].astype(BF16) for n in BIG])
    full = {}
    for n, gth in zip(BIG, gathered):
        full[n] = gth.reshape(-1, gth.shape[2]) if n in ROW_SHARDED else _shards_to_cols(gth)
    w_sb, w_rw, w_gt = full["w_in"][:, :SB_COLS], full["w_in"][:, SB_COLS:SB_COLS + RW_COLS], full["w_in"][:, SB_COLS + RW_COLS:]
    lora_rows = {"w_up": 0, "a_up": 64, "g_up": 128}
    lora = {n: jnp.pad(full[n], ((r0, LORA_COLS - r0 - full[n].shape[0]), (0, 0))) for n, r0 in lora_rows.items()}
    mu = vec["mu_rw"]
    mu_parts = [mu[:, :512], mu[:, 512:1024], mu[:, 1024:1536], mu[:, 1536:]]
    b1, b2 = vec["b_gate"][:, :D], vec["b_gate"][:, D:]

    (h1,) = _rowwise("norm_mix_pre", _f_norm, [(x2, D, 0)], [vec["norm_mix_pre"]], [(D, BF16)])
    p_sb = _mm("proj_sb", h1, w_sb)
    p_rw = _mm("proj_rw", h1, w_rw)
    p_gt = _mm("proj_gate", h1, w_gt)
    o_sb = _attn_fwd(p_sb, B, S)
    p_prev = _shift_tokens(p_rw, B, S)
    pre_rows = [(p_rw, 512, 0), (p_rw, 512, 1), (p_rw, 512, 2), (p_rw, LORA_COLS, 6),
                (p_prev, 512, 0), (p_prev, 512, 1), (p_prev, 512, 2), (p_prev, LORA_COLS, 6)]
    pre_params = mu_parts + [vec["w0"], lora["w_up"], vec["a0"], lora["a_up"], lora["g_up"], vec["k_k"], vec["k_a"]]
    r_, lw_, k2_, v_, kap_, a_, g_ = _rowwise("rw_pre", _f_rwpre, pre_rows, pre_params, [(512, F32)] * 7, tile=128)
    y_wkv, states = _wkv_fwd(r_, lw_, k2_, v_, kap_, a_, B, S)
    post_rows = [(y_wkv, 512, 0), (r_, 512, 0), (k2_, 512, 0), (v_, 512, 0), (g_, 512, 0)]
    post_params = [vec["lnx_w"], vec["lnx_b"], vec["r_k"]]
    (o_rw,) = _rowwise("rw_post", _f_rwpost, post_rows, post_params, [(512, F32)])
    m1 = _mm("mix_sb_out", o_sb, full["w_sb_out"])
    m2 = _mm("mix_rw_out", o_rw, full["w_rw_out"])
    merge_rows = [(p_gt, D, 0), (p_gt, D, 1), (m1, D, 0), (m2, D, 0)]
    (merged,) = _rowwise("merge", _f_merge, merge_rows, [b1, b2], [(D, F32)])
    u = _mm("mix_out", merged, full["w_o"])
    post1_params = [vec["norm_mix_post"], vec["norm_ffn_pre"]]
    x1, h2 = _rowwise("post_mix", _f_post1, [(x2, D, 0), (u, D, 0)], post1_params, [(D, F32), (D, BF16)])
    ag = _mm("ffn_gate", h2, full["w_ffn_gate"])
    au = _mm("ffn_up", h2, full["w_ffn_up"])
    (sw,) = _rowwise("swiglu", _f_swiglu, [(ag, D_FF, 0), (au, D_FF, 0)], [], [(D_FF, BF16)])
    f = _mm("ffn_down", sw, full["w_ffn_down"])
    loss_part, dx1, df, dg4 = _loss_head(x1, f, tgt, vec["norm_ffn_post"])

    gbig, gsmall = {}, {"norm_ffn_post": dg4}
    dsw = _mm("d_swiglu_out", df, full["w_ffn_down"], tb=True)
    gbig["w_ffn_down"] = _mm("g_ffn_down", sw, df, ta=True)
    (dag, dau), _ = _rowwise_vjp("swiglu_bwd", _f_swiglu, [(ag, D_FF, 0), (au, D_FF, 0)], [], [[dsw]], [True, True], [])
    dh2 = _mm("d_h2_gate", dag, full["w_ffn_gate"], tb=True)
    dh2 = _mm("d_h2_up", dau, full["w_ffn_up"], tb=True, acc=dh2)
    gbig["w_ffn_gate"] = _mm("g_ffn_gate", h2, dag, ta=True)
    gbig["w_ffn_up"] = _mm("g_ffn_up", h2, dau, ta=True)
    (dx_res, du), (dg2, dg3) = _rowwise_vjp("post_mix_bwd", _f_post1, [(x2, D, 0), (u, D, 0)], post1_params,
                                            [[dx1], [dh2]], [True, True], [True, True])
    gsmall["norm_mix_post"], gsmall["norm_ffn_pre"] = dg2, dg3
    dmerged = _mm("d_merged", du, full["w_o"], tb=True)
    gbig["w_o"] = _mm("g_w_o", merged, du, ta=True)
    (dpg1, dpg2, dm1, dm2), (db1, db2) = _rowwise_vjp("merge_bwd", _f_merge, merge_rows, [b1, b2], [[dmerged]],
                                                      [True] * 4, [True, True])
    gsmall["b_gate"] = jnp.concatenate([db1, db2], axis=1)
    do_sb = _mm("d_o_sb", dm1, full["w_sb_out"], tb=True)
    do_rw = _mm("d_o_rw", dm2, full["w_rw_out"], tb=True)
    gbig["w_sb_out"] = _mm("g_sb_out", o_sb, dm1, ta=True)
    gbig["w_rw_out"] = _mm("g_rw_out", o_rw, dm2, ta=True)
    (dy_wkv, dr_a, dk2_a, dv_a, dg_), (dlnx_w, dlnx_b, dr_k) = _rowwise_vjp(
        "rw_post_bwd", _f_rwpost, post_rows, post_params, [[do_rw]], [True] * 5, [True] * 3)
    gsmall["lnx_w"], gsmall["lnx_b"], gsmall["r_k"] = dlnx_w, dlnx_b, dr_k
    dr_b, dlw, dk2_b, dv_b, dkap, da = _wkv_bwd(r_, lw_, k2_, v_, kap_, a_, states, dy_wkv, B, S)
    pre_cts = [[dr_a, dr_b], [dlw], [dk2_a, dk2_b], [dv_a, dv_b], [dkap], [da], [dg_]]
    dpre_rows, dpre_params = _rowwise_vjp("rw_pre_bwd", _f_rwpre, pre_rows, pre_params, pre_cts,
                                          [True] * 8, [True] * 11, tile=128)
    dp_rw = jnp.concatenate(dpre_rows[:4], axis=1) + _unshift_tokens(jnp.concatenate(dpre_rows[4:], axis=1), B, S)
    gsmall["mu_rw"] = jnp.concatenate(dpre_params[:4], axis=1)
    gsmall["w0"], gsmall["a0"], gsmall["k_k"], gsmall["k_a"] = dpre_params[4], dpre_params[6], dpre_params[9], dpre_params[10]
    glora = {"w_up": dpre_params[5][0:64], "a_up": dpre_params[7][64:128], "g_up": dpre_params[8][128:256]}
    dq, dk, dv = _attn_bwd(p_sb, o_sb, do_sb, B, S)
    dh1 = _mm("d_h1_q", dq, w_sb[:, :512], tb=True)
    dh1 = _mm("d_h1_k", dk, w_sb[:, 512:1024], tb=True, acc=dh1)
    dh1 = _mm("d_h1_v", dv, w_sb[:, 1024:], tb=True, acc=dh1)
    dh1 = _mm("d_h1_rw", dp_rw, w_rw, tb=True, acc=dh1)
    dh1 = _mm("d_h1_g1", dpg1, w_gt[:, :D], tb=True, acc=dh1)
    dh1 = _mm("d_h1_g2", dpg2, w_gt[:, D:], tb=True, acc=dh1)
    gbig["w_in"] = jnp.concatenate(
        [_mm("g_in_" + tag, h1, d, ta=True)
         for tag, d in (("q", dq), ("k", dk), ("v", dv), ("rw", dp_rw), ("g1", dpg1), ("g2", dpg2))], axis=1)
    (grad_x2,), (dg1,) = _rowwise_vjp("norm_mix_pre_bwd", _f_norm, [(x2, D, 0)], [vec["norm_mix_pre"]], [[dh1]],
                                      [True], [True], add_to={0: dx_res})
    gsmall["norm_mix_pre"] = dg1
    gbig.update(glora)

    split = []
    for n in BIG:
        g = _rows_to_shards(gbig[n]) if n in ROW_SHARDED else _cols_to_shards(gbig[n])
        split.append(g.reshape(N_CHIPS, 2, g.shape[1] // 2, g.shape[2]))
    core = lax.axis_index("c").astype(jnp.int32).reshape(1)
    theirs = _pair_split(split)
    chip_sums = [_pair_sum("pair_sum_" + n, a, b, core) for n, a, b in zip(BIG, split, theirs)]
    landed = _chip_scatter(chip_sums)
    joined = _pair_join([_chip_sum("chip_sum_" + n, s, core) for n, s in zip(BIG, landed)])
    grads = {n: j.reshape(W[n].shape[1:]) for n, j in zip(BIG, joined)}

    small_local = _pack_small({n: gsmall[n] for n in SMALL}, extra_rows=1)
    loss_row = small_local.shape[0] - 1
    small_local = small_local.at[loss_row].set(loss_part[0])
    small_sum = _all_reduce_small(small_local)
    loss = small_sum[loss_row, 0]

    delta, new_m, new_v = {}, {}, {}
    for n in BIG:
        d_, m_, v2_ = _adamw("adamw_" + n, W[n][0], grads[n], Mo[n][0], Vo[n][0])
        delta[n], new_m[n], new_v[n] = d_[None], m_[None], v2_[None]
        grads[n] = grads[n][None]
    pk = lambda src: _pack_small({n: src[n] for n in SMALL}, extra_rows=1)
    d_s, m_s, v_s = _adamw("adamw_small", pk(W), small_sum.at[loss_row].set(0.0), pk(Mo), pk(Vo))
    for dst, packed in ((grads, small_sum), (delta, d_s), (new_m, m_s), (new_v, v_s)):
        dst.update(_unpack_small(packed, shapes))

    return (loss, grad_x2.reshape(B, S, D), *[grads[n] for n in ORDER], *[delta[n] for n in ORDER],
            *[new_m[n] for n in ORDER], *[new_v[n] for n in ORDER])
```

```python
import functools

import jax
import jax.numpy as jnp
from jax import lax
from jax.experimental import pallas as pl
from jax.experimental.pallas import tpu as pltpu

F32 = jnp.float32
BF16 = jnp.bfloat16
MESH = pl.DeviceIdType.MESH

D_MODEL = 1024
SB_HEADS = 8
HEAD_DIM = 64
SB_WIDTH = SB_HEADS * HEAD_DIM
RW_WIDTH = 512
LORA_COLS = 256
SB_COLS = 3 * SB_WIDTH
RW_COLS = 3 * RW_WIDTH + LORA_COLS
GATE_COLS = 2 * D_MODEL
D_FF = 2816
RMS_EPS = 1e-6
GN_EPS = HEAD_DIM * 1e-5
WKV_CHUNK = 64
ATTN_QUERIES = 512
ATTN_KEYS = 128
LANES = 128
SUBLANES = 8
N_CHIPS = 4
N_DEV = 8

ADAM_LR = 0.001
ADAM_B1 = 0.9
ADAM_B2 = 0.999
ADAM_EPS = 1e-08
ADAM_WD = 0.01
ADAM_STEP = 10

VMEM_LIMIT = 48 * 1024 * 1024


def _params(sem=None, **kw):
    if sem is not None:
        kw["dimension_semantics"] = sem
    return pltpu.CompilerParams(vmem_limit_bytes=VMEM_LIMIT, **kw)


def _div_tile(dim, pref, mult=LANES):
    if dim <= pref:
        return dim
    t = pref - pref % mult
    while t >= mult:
        if dim % t == 0:
            return t
        t -= mult
    return dim


def _dot(a, b, dims):
    return lax.dot_general(a, b, (dims, ((), ())), preferred_element_type=F32)


def _mm(name, a, b, *, ta=False, tb=False, acc=None, out_dtype=F32):
    if ta:
        K, M = a.shape
    else:
        M, K = a.shape
    N = b.shape[0] if tb else b.shape[1]
    tm, tn, tk = _div_tile(M, 1408), _div_tile(N, 1408), _div_tile(K, 512)
    nk = K // tk
    dims = ((0,) if ta else (1,), (1,) if tb else (0,))
    has_acc = acc is not None

    def body(*refs):
        a_ref, b_ref = refs[0], refs[1]
        o_ref, scr = refs[-2], refs[-1]
        k = pl.program_id(2)
        part = _dot(a_ref[...].astype(BF16), b_ref[...].astype(BF16), dims)

        @pl.when(k == 0)
        def _():
            scr[...] = part + refs[2][...] if has_acc else part

        @pl.when(k > 0)
        def _():
            scr[...] += part

        @pl.when(k == nk - 1)
        def _():
            o_ref[...] = scr[...].astype(o_ref.dtype)

    a_spec = pl.BlockSpec((tk, tm), lambda i, j, k: (k, i)) if ta else pl.BlockSpec((tm, tk), lambda i, j, k: (i, k))
    b_spec = pl.BlockSpec((tn, tk), lambda i, j, k: (j, k)) if tb else pl.BlockSpec((tk, tn), lambda i, j, k: (k, j))
    o_spec = pl.BlockSpec((tm, tn), lambda i, j, k: (i, j))
    return pl.pallas_call(
        body, name=name,
        grid=(M // tm, N // tn, nk),
        in_specs=[a_spec, b_spec] + ([o_spec] if has_acc else []),
        out_specs=o_spec,
        out_shape=jax.ShapeDtypeStruct((M, N), out_dtype),
        scratch_shapes=[pltpu.VMEM((tm, tn), F32)],
        compiler_params=_params(("parallel", "parallel", "arbitrary")),
    )(*([a, b] + ([acc] if has_acc else [])))


def _row_spec(tile, width, colblk):
    return pl.BlockSpec((tile, width), lambda i: (i, colblk))


def _full_spec(shape):
    return pl.BlockSpec(shape, lambda i: (0,) * len(shape))


def _rowwise(name, fn, rows, params, outs, tile=256):
    T = rows[0][0].shape[0]
    tile = min(tile, T)
    n_r, n_p = len(rows), len(params)

    def body(*refs):
        r = [x[...].astype(F32) for x in refs[:n_r]]
        p = [x[...].astype(F32) for x in refs[n_r:n_r + n_p]]
        for o_ref, val in zip(refs[n_r + n_p:], fn(*r, *p)):
            o_ref[...] = val.astype(o_ref.dtype)

    return pl.pallas_call(
        body, name=name,
        grid=(T // tile,),
        in_specs=[_row_spec(tile, w, cb) for _, w, cb in rows] + [_full_spec(p.shape) for p in params],
        out_specs=[_row_spec(tile, w, 0) for w, _ in outs],
        out_shape=[jax.ShapeDtypeStruct((T, w), dt) for w, dt in outs],
        compiler_params=_params(("parallel",)),
    )(*([a for a, _, _ in rows] + list(params)))


def _rowwise_vjp(name, fn, rows, params, cts, need_rows, need_params, add_to=None, tile=256, bf16_rows=()):
    add_to = add_to or {}
    T = rows[0][0].shape[0]
    tile = min(tile, T)
    n_r, n_p = len(rows), len(params)
    ct_flat = [c for group in cts for c in group]
    ct_sizes = [len(group) for group in cts]
    add_idx = sorted(add_to)
    row_out = [i for i in range(n_r) if need_rows[i]]
    par_out = [i for i in range(n_p) if need_params[i]]
    n_ct, n_add = len(ct_flat), len(add_idx)

    def body(*refs):
        pos = 0
        r = [x[...].astype(F32) for x in refs[pos:pos + n_r]]
        pos += n_r
        p = [x[...].astype(F32) for x in refs[pos:pos + n_p]]
        pos += n_p
        ct_vals = [x[...].astype(F32) for x in refs[pos:pos + n_ct]]
        pos += n_ct
        adds = {i: x[...] for i, x in zip(add_idx, refs[pos:pos + n_add])}
        pos += n_add
        drow_refs = refs[pos:pos + len(row_out)]
        pos += len(row_out)
        dpar_refs = refs[pos:pos + len(par_out)]
        ct_in, q = [], 0
        for n in ct_sizes:
            ct_in.append(functools.reduce(lambda u, v: u + v, ct_vals[q:q + n]))
            q += n
        _, vjp = jax.vjp(fn, *r, *p)
        grads = vjp(tuple(ct_in))
        for ref, i in zip(drow_refs, row_out):
            g = grads[i]
            ref[...] = (g + adds[i] if i in adds else g).astype(ref.dtype)

        @pl.when(pl.program_id(0) == 0)
        def _():
            for ref in dpar_refs:
                ref[...] = jnp.zeros_like(ref)

        for ref, i in zip(dpar_refs, par_out):
            ref[...] += grads[n_r + i]

    ct_widths = [c.shape[1] for c in ct_flat]
    in_specs = ([_row_spec(tile, w, cb) for _, w, cb in rows] + [_full_spec(p.shape) for p in params]
                + [_row_spec(tile, w, 0) for w in ct_widths] + [_row_spec(tile, rows[i][1], 0) for i in add_idx])
    out_specs = [_row_spec(tile, rows[i][1], 0) for i in row_out] + [_full_spec(params[i].shape) for i in par_out]
    out_shape = ([jax.ShapeDtypeStruct((T, rows[i][1]), BF16 if i in bf16_rows else F32) for i in row_out]
                 + [jax.ShapeDtypeStruct(params[i].shape, F32) for i in par_out])
    res = pl.pallas_call(
        body, name=name,
        grid=(T // tile,),
        in_specs=in_specs, out_specs=out_specs, out_shape=out_shape,
        compiler_params=_params(("arbitrary",)),
    )(*([a for a, _, _ in rows] + list(params) + ct_flat + [add_to[i] for i in add_idx]))
    return res[:len(row_out)], res[len(row_out):]


def _sigmoid(x):
    return 0.5 * (jnp.tanh(0.5 * x) + 1.0)


def _softplus(x):
    return jnp.maximum(x, 0.0) + jnp.log(1.0 + jnp.exp(-jnp.abs(x)))


def _rms(x, g):
    return x * lax.rsqrt(jnp.mean(x * x, axis=-1, keepdims=True) + RMS_EPS) * g


def _segsum_impl(x):
    n = x.shape[-1]
    r = lax.shift_right_logical(lax.broadcasted_iota(jnp.int32, (n, n), 0), 6)
    c = lax.shift_right_logical(lax.broadcasted_iota(jnp.int32, (n, n), 1), 6)
    bd = (r == c).astype(BF16)
    hi = x.astype(BF16)
    rest = x - hi.astype(F32)
    mid = rest.astype(BF16)
    lo = (rest - mid.astype(F32)).astype(BF16)
    nn = ((1,), (0,))
    return _dot(hi, bd, nn) + _dot(mid, bd, nn) + _dot(lo, bd, nn)


@jax.custom_vjp
def _segsum(x):
    return _segsum_impl(x)


_segsum.defvjp(lambda x: (_segsum_impl(x), None), lambda _, g: (_segsum_impl(g),))


@jax.custom_vjp
def _mmb(a, w):
    return _dot(a.astype(BF16), w.astype(BF16), ((1,), (0,)))


def _mmb_fwd(a, w):
    return _mmb(a, w), (a, w)


def _mmb_bwd(res, g):
    a, w = res
    gb = g.astype(BF16)
    return _dot(gb, w.astype(BF16), ((1,), (1,))), _dot(a.astype(BF16), gb, ((0,), (0,)))


_mmb.defvjp(_mmb_fwd, _mmb_bwd)


def _f_norm(x, g):
    return (_rms(x, g),)


def _f_post1(x, u, g2, g3):
    x1 = x + _rms(u, g2)
    return x1, _rms(x1, g3)


def _f_swiglu(ag, au):
    return (ag * _sigmoid(ag) * au,)


def _f_merge(pg1, pg2, m1, m2, b1, b2):
    return (_sigmoid(pg1 + b1) * m1 + _sigmoid(pg2 + b2) * m2,)


def _f_out(x1, f, g4):
    return (x1 + _rms(f, g4),)


def _f_rwpre(pr, pk, pv, pz, qr, qk, qv, qz, mur, muk, muv, muz, w0, wup, a0, aup, gup, k_k, k_a):
    r = pr + (qr - pr) * mur
    k = pk + (qk - pk) * muk
    v = pv + (qv - pv) * muv
    z = pz + (qz - pz) * muz
    w_raw = w0 + _mmb(jnp.tanh(z), wup)
    lw = -jnp.exp(-_softplus(-w_raw) - 0.5)
    a = _sigmoid(a0 + _mmb(z, aup))
    g = _mmb(_sigmoid(z), gup)
    kk = k * k_k
    kap = kk * lax.rsqrt(jnp.maximum(_segsum(kk * kk), 1e-24))
    k2 = k * (1.0 + (a - 1.0) * k_a)
    return r, lw, k2, v, kap, a, g


def _f_rwpost(y, r, k2, v, g, lnx_w, lnx_b, r_k):
    inv = 1.0 / HEAD_DIM
    yc = y - _segsum(y) * inv
    var = _segsum(yc * yc) * inv
    yn = yc * lax.rsqrt(var + GN_EPS) * lnx_w + lnx_b
    bonus = _segsum(r * k2 * r_k) * v
    return ((yn + bonus) * g,)


def _loss_head(x1, f, target, g4, tile=256):
    T, D = x1.shape
    tile = min(tile, T)

    def body(x1_ref, f_ref, t_ref, g_ref, loss_ref, dx1_ref, df_ref, dg_ref):
        (y,), vjp = jax.vjp(_f_out, x1_ref[...], f_ref[...], g_ref[...])
        err = y - t_ref[...]
        dx1, df, dg = vjp((err * (1.0 / D),))
        dx1_ref[...] = dx1
        df_ref[...] = df.astype(df_ref.dtype)

        @pl.when(pl.program_id(0) == 0)
        def _():
            loss_ref[...] = jnp.zeros_like(loss_ref)
            dg_ref[...] = jnp.zeros_like(dg_ref)

        part = jnp.sum(jnp.sum(err * err, axis=1, keepdims=True), axis=0, keepdims=True) * (0.5 / D)
        loss_ref[...] += jnp.broadcast_to(part, loss_ref.shape)
        dg_ref[...] += dg

    row = pl.BlockSpec((tile, D), lambda i: (i, 0))
    return pl.pallas_call(
        body, name="loss_head",
        grid=(T // tile,),
        in_specs=[row, row, row, _full_spec(g4.shape)],
        out_specs=[_full_spec((SUBLANES, LANES)), row, row, _full_spec(g4.shape)],
        out_shape=[jax.ShapeDtypeStruct((SUBLANES, LANES), F32), jax.ShapeDtypeStruct((T, D), F32),
                   jax.ShapeDtypeStruct((T, D), BF16), jax.ShapeDtypeStruct(g4.shape, F32)],
        compiler_params=_params(("arbitrary",)),
    )(x1, f, target, g4)


def _nn(a, b):
    return _dot(a, b, ((1,), (0,)))


def _nt(a, b):
    return _dot(a, b, ((1,), (1,)))


def _tn(a, b):
    return _dot(a, b, ((0,), (0,)))


def _split_dot(x, u):
    hi = x.astype(BF16)
    lo = (x - hi.astype(F32)).astype(BF16)
    return _nn(hi, u) + _nn(lo, u)


def _head_masks():
    lane = lax.broadcasted_iota(jnp.int32, (1, LANES), 1)
    return [((lane >= h * HEAD_DIM) & (lane < (h + 1) * HEAD_DIM)).astype(F32) for h in range(LANES // HEAD_DIM)]


def _key_tri(op):
    row = lax.broadcasted_iota(jnp.int32, (ATTN_KEYS, ATTN_KEYS), 0)
    col = lax.broadcasted_iota(jnp.int32, (ATTN_KEYS, ATTN_KEYS), 1)
    return op(row, col).astype(BF16)


def _causal(qb, sub):
    row = lax.broadcasted_iota(jnp.int32, (qb, ATTN_KEYS), 0)
    col = lax.broadcasted_iota(jnp.int32, (qb, ATTN_KEYS), 1)
    return col + sub * ATTN_KEYS < row


def _sb_weights(qh, kb, c_fail, u_gt, strict, scale):
    z = _nt(qh, kb) * scale
    L = jnp.minimum(-z, 0.0) - jnp.log(1.0 + jnp.exp(-jnp.abs(z)))
    Lm = L if strict is None else jnp.where(strict, L, 0.0)
    A = jnp.exp(z + L + c_fail + _split_dot(Lm, u_gt))
    if strict is not None:
        A = jnp.where(strict, A, 0.0)
    return z, L, Lm, A


def _attn_specs(S, qb):
    nq = S // qb
    q_spec = pl.BlockSpec((qb, LANES), lambda b, p, i: (b * nq + i, p))
    k_spec = pl.BlockSpec((S, LANES), lambda b, p, i: (b, SB_WIDTH // LANES + p))
    v_spec = pl.BlockSpec((S, LANES), lambda b, p, i: (b, 2 * SB_WIDTH // LANES + p))
    seq = pl.BlockSpec((S, LANES), lambda b, p, i: (b, p))
    return q_spec, k_spec, v_spec, q_spec, seq


def _key_walk(i, qb, block, carry):
    per = qb // ATTN_KEYS
    for sub in reversed(range(per)):
        carry = block(i * per + sub, carry, _causal(qb, sub))
    return lax.fori_loop(0, i * per, lambda j, c: block(i * per - 1 - j, c, None), carry)


def _attn_fwd(proj, B, S):
    qb = min(ATTN_QUERIES, S)
    scale = HEAD_DIM ** -0.5

    def body(q_ref, k_ref, v_ref, o_ref):
        i = pl.program_id(2)
        masks = _head_masks()
        u_gt = _key_tri(lambda r, c: r > c)
        q = q_ref[...]
        qhs = [(q * m).astype(BF16) for m in masks]

        def block(J, carry, strict_mask):
            acc, cs = carry
            r0 = pl.multiple_of(J * ATTN_KEYS, ATTN_KEYS)
            kb = k_ref[pl.ds(r0, ATTN_KEYS), :].astype(BF16)
            vb = v_ref[pl.ds(r0, ATTN_KEYS), :]
            new_cs = []
            for h, m in enumerate(masks):
                _, _, Lm, A = _sb_weights(qhs[h], kb, cs[h], u_gt, strict_mask, scale)
                acc = acc + _nn(A.astype(BF16), (vb * m).astype(BF16))
                new_cs.append(cs[h] + jnp.sum(Lm, axis=1, keepdims=True))
            return acc, tuple(new_cs)

        zero_c = tuple(jnp.zeros((qb, 1), F32) for _ in masks)
        carry = _key_walk(i, qb, block, (jnp.zeros((qb, LANES), F32), zero_c))
        o_ref[...] = carry[0]

    q_spec, k_spec, v_spec, blk, _ = _attn_specs(S, qb)
    return pl.pallas_call(
        body, name="sb_attn_fwd",
        grid=(B, SB_WIDTH // LANES, S // qb),
        in_specs=[q_spec, k_spec, v_spec],
        out_specs=blk,
        out_shape=jax.ShapeDtypeStruct((B * S, SB_WIDTH), F32),
        compiler_params=_params(("parallel", "parallel", "arbitrary")),
    )(proj, proj, proj)


def _attn_bwd(proj, o, do, B, S):
    qb = min(ATTN_QUERIES, S)
    nq = S // qb
    scale = HEAD_DIM ** -0.5

    def body(q_ref, k_ref, v_ref, o_ref, do_ref, dq_ref, dk_out, dv_out, dk_ref, dv_ref):
        i = pl.program_id(2)

        @pl.when(i == 0)
        def _():
            dk_ref[...] = jnp.zeros_like(dk_ref)
            dv_ref[...] = jnp.zeros_like(dv_ref)

        masks = _head_masks()
        u_gt = _key_tri(lambda r, c: r > c)
        u_ge = _key_tri(lambda r, c: r >= c)
        q = q_ref[...]
        do_b = do_ref[...].astype(BF16)
        od = o_ref[...] * do_b.astype(F32)
        qhs = [(q * m).astype(BF16) for m in masks]
        dohs = [(do_b * m.astype(BF16)) for m in masks]
        totals = [jnp.sum(od * m, axis=1, keepdims=True) for m in masks]

        def block(J, carry, strict_mask):
            dq, c_fail, c_p = carry
            r0 = pl.multiple_of(J * ATTN_KEYS, ATTN_KEYS)
            kb32 = k_ref[pl.ds(r0, ATTN_KEYS), :]
            kb = kb32.astype(BF16)
            vb = v_ref[pl.ds(r0, ATTN_KEYS), :].astype(BF16)
            dk_blk = jnp.zeros((ATTN_KEYS, LANES), F32)
            dv_blk = jnp.zeros((ATTN_KEYS, LANES), F32)
            new_fail, new_p = [], []
            for h, m in enumerate(masks):
                z, L, Lm, A = _sb_weights(qhs[h], kb, c_fail[h], u_gt, strict_mask, scale)
                Ab = A.astype(BF16)
                P = Ab.astype(F32) * _nt(dohs[h], vb)
                after = c_p[h] + _split_dot(P, u_ge)
                sig = jnp.exp(z + L)
                dz = (P * (1.0 - sig) - sig * (totals[h] - after)) * scale
                if strict_mask is not None:
                    dz = jnp.where(strict_mask, dz, 0.0)
                dzb = dz.astype(BF16)
                dv_blk = dv_blk + _tn(Ab, dohs[h])
                dk_blk = dk_blk + _tn(dzb, qhs[h])
                dq = dq + _nn(dzb, (kb32 * m).astype(BF16))
                new_fail.append(c_fail[h] + jnp.sum(Lm, axis=1, keepdims=True))
                new_p.append(c_p[h] + jnp.sum(P, axis=1, keepdims=True))
            dk_ref[pl.ds(r0, ATTN_KEYS), :] += dk_blk
            dv_ref[pl.ds(r0, ATTN_KEYS), :] += dv_blk
            return dq, tuple(new_fail), tuple(new_p)

        zc = tuple(jnp.zeros((qb, 1), F32) for _ in masks)
        carry = _key_walk(i, qb, block, (jnp.zeros((qb, LANES), F32), zc, zc))
        dq_ref[...] = carry[0].astype(dq_ref.dtype)

        @pl.when(i == nq - 1)
        def _():
            dk_out[...] = dk_ref[...].astype(dk_out.dtype)
            dv_out[...] = dv_ref[...].astype(dv_out.dtype)

    q_spec, k_spec, v_spec, blk, seq = _attn_specs(S, qb)
    return pl.pallas_call(
        body, name="sb_attn_bwd",
        grid=(B, SB_WIDTH // LANES, nq),
        in_specs=[q_spec, k_spec, v_spec, blk, blk],
        out_specs=[blk, seq, seq],
        out_shape=[jax.ShapeDtypeStruct((B * S, SB_WIDTH), BF16)] * 3,
        scratch_shapes=[pltpu.VMEM((S, LANES), F32), pltpu.VMEM((S, LANES), F32)],
        compiler_params=_params(("parallel", "parallel", "arbitrary")),
    )(proj, proj, proj, o, do)


_BATCHED = {"nn": "gmk,gkn->gmn", "nt": "gmk,gnk->gmn", "tn": "gkm,gkn->gmn"}


def _bdot_raw(a, b, kind, passes):
    e = functools.partial(jnp.einsum, _BATCHED[kind], preferred_element_type=F32)
    ah, bh = a.astype(BF16), b.astype(BF16)
    if passes == 1:
        return e(ah, bh)
    al, bl = (a - ah.astype(F32)).astype(BF16), (b - bh.astype(F32)).astype(BF16)
    return e(ah, bh) + e(ah, bl) + e(al, bh)


@functools.partial(jax.custom_vjp, nondiff_argnums=(2, 3))
def _bdot(a, b, kind, passes):
    return _bdot_raw(a, b, kind, passes)


def _bdot_fwd(a, b, kind, passes):
    return _bdot_raw(a, b, kind, passes), (a, b)


def _bdot_bwd(kind, passes, res, g):
    a, b = res
    if kind == "nn":
        return _bdot_raw(g, b, "nt", passes), _bdot_raw(a, g, "tn", passes)
    if kind == "nt":
        return _bdot_raw(g, b, "nn", passes), _bdot_raw(g, a, "tn", passes)
    return _bdot_raw(b, g, "nt", passes), _bdot_raw(a, g, "nn", passes)


_bdot.defvjp(_bdot_fwd, _bdot_bwd)


def _wkv_chunk(S0, r, lw, k, v, kap, a):
    G, C, N = r.shape
    row = lax.broadcasted_iota(jnp.int32, (C, C), 0)
    col = lax.broadcasted_iota(jnp.int32, (C, C), 1)
    incl = (col <= row).astype(F32)
    strict = (col < row).astype(F32)
    cum = _bdot(jnp.broadcast_to(incl, (G, C, C)), lw, "nn", 3)
    e_pos = jnp.exp(cum)
    e_neg = jnp.exp(-cum)
    al = -kap * jnp.exp(cum - lw)
    be = kap * a * e_neg
    kt = k * e_neg
    rt = r * e_pos
    m_ab = _bdot(al, be, "nt", 3) * strict
    m_ak = _bdot(al, kt, "nt", 3) * strict
    m_rb = _bdot(rt, be, "nt", 3) * incl
    m_rk = _bdot(rt, kt, "nt", 3) * incl
    sa = _bdot(al, S0, "nt", 3) + _bdot(m_ak, v, "nn", 3)
    p = m_ab
    steps = max(1, (C - 1).bit_length())
    for j in range(steps):
        sa = sa + _bdot(p, sa, "nn", 1)
        if j + 1 < steps:
            p = _bdot(p, p, "nn", 1)
    y = _bdot(rt, S0, "nt", 3) + _bdot(m_rb, sa, "nn", 3) + _bdot(m_rk, v, "nn", 3)
    S1 = (S0 + _bdot(sa, be, "tn", 1) + _bdot(v, kt, "tn", 1)) * e_pos[:, C - 1:C, :]
    return y, S1


def _split_heads(x):
    return jnp.stack([x[:, h * HEAD_DIM:(h + 1) * HEAD_DIM] for h in range(x.shape[1] // HEAD_DIM)], axis=0)


def _merge_heads(x):
    return jnp.concatenate([x[h] for h in range(x.shape[0])], axis=1)


def _wkv_fwd(r, lw, k, v, kap, a, B, S):
    C, H, N = WKV_CHUNK, RW_WIDTH // HEAD_DIM, HEAD_DIM
    nc = S // C

    def body(r_ref, lw_ref, k_ref, v_ref, kap_ref, a_ref, y_ref, st_ref, s_scr):
        @pl.when(pl.program_id(1) == 0)
        def _():
            s_scr[...] = jnp.zeros_like(s_scr)

        S0 = s_scr[...]
        st_ref[0, 0] = S0
        args = [_split_heads(ref[...]) for ref in (r_ref, lw_ref, k_ref, v_ref, kap_ref, a_ref)]
        y, S1 = _wkv_chunk(S0, *args)
        s_scr[...] = S1
        y_ref[...] = _merge_heads(y)

    row_spec = pl.BlockSpec((C, RW_WIDTH), lambda b, c: (b * nc + c, 0))
    return pl.pallas_call(
        body, name="wkv_fwd",
        grid=(B, nc),
        in_specs=[row_spec] * 6,
        out_specs=[row_spec, pl.BlockSpec((1, 1, H, N, N), lambda b, c: (b, c, 0, 0, 0))],
        out_shape=[jax.ShapeDtypeStruct((B * S, RW_WIDTH), F32), jax.ShapeDtypeStruct((B, nc, H, N, N), F32)],
        scratch_shapes=[pltpu.VMEM((H, N, N), F32)],
        compiler_params=_params(("arbitrary", "arbitrary")),
    )(r, lw, k, v, kap, a)


def _wkv_bwd(r, lw, k, v, kap, a, states, dy, B, S):
    C, H, N = WKV_CHUNK, RW_WIDTH // HEAD_DIM, HEAD_DIM
    nc = S // C

    def body(r_ref, lw_ref, k_ref, v_ref, kap_ref, a_ref, st_ref, dy_ref,
             dr_ref, dlw_ref, dk_ref, dv_ref, dkap_ref, da_ref, ds_scr):
        @pl.when(pl.program_id(1) == 0)
        def _():
            ds_scr[...] = jnp.zeros_like(ds_scr)

        args = [_split_heads(ref[...]) for ref in (r_ref, lw_ref, k_ref, v_ref, kap_ref, a_ref)]
        _, vjp = jax.vjp(_wkv_chunk, st_ref[0, 0], *args)
        g = vjp((_split_heads(dy_ref[...]), ds_scr[...]))
        ds_scr[...] = g[0]
        for ref, gv in zip((dr_ref, dlw_ref, dk_ref, dv_ref, dkap_ref, da_ref), g[1:]):
            ref[...] = _merge_heads(gv)

    row_spec = pl.BlockSpec((C, RW_WIDTH), lambda b, c: (b * nc + (nc - 1 - c), 0))
    st_spec = pl.BlockSpec((1, 1, H, N, N), lambda b, c: (b, nc - 1 - c, 0, 0, 0))
    return pl.pallas_call(
        body, name="wkv_bwd",
        grid=(B, nc),
        in_specs=[row_spec] * 6 + [st_spec, row_spec],
        out_specs=[row_spec] * 6,
        out_shape=[jax.ShapeDtypeStruct((B * S, RW_WIDTH), F32)] * 6,
        scratch_shapes=[pltpu.VMEM((H, N, N), F32)],
        compiler_params=_params(("arbitrary", "arbitrary")),
    )(r, lw, k, v, kap, a, states, dy)


HBM = pl.BlockSpec(memory_space=pl.ANY)


def _place():
    return lax.axis_index("x"), lax.axis_index("y"), lax.axis_index("c")


def _other_chips(x, y):
    return [(1 - x, y), (x, 1 - y), (1 - x, 1 - y)]


def _all_gather_chips(shards):
    n = len(shards)

    def body(*refs):
        ins, outs = refs[:n], refs[n:2 * n]
        ici_send, ici_recv, d2d_send, d2d_recv, local = refs[2 * n:]
        x, y, c = _place()
        me = 2 * x + y
        sib = (x, y, 1 - c)
        chips = _other_chips(x, y)
        started, copies = [], []
        for w in range(n):
            cp = pltpu.make_async_copy(ins[w].at[c], outs[w].at[me, c], local.at[w])
            cp.start()
            copies.append(cp)
            for j, (px, py) in enumerate(chips):
                rd = pltpu.make_async_remote_copy(
                    src_ref=ins[w].at[c], dst_ref=outs[w].at[me, c], send_sem=ici_send.at[3 * w + j],
                    recv_sem=ici_recv.at[3 * w + j], device_id=(px, py, c), device_id_type=MESH)
                rd.start()
                started.append(rd)
            rd = pltpu.make_async_remote_copy(
                src_ref=ins[w].at[c], dst_ref=outs[w].at[me, c], send_sem=d2d_send.at[4 * w + 3],
                recv_sem=d2d_recv.at[4 * w + 3], device_id=sib, device_id_type=MESH)
            rd.start()
            started.append(rd)
        for w in range(n):
            for j, (px, py) in enumerate(chips):
                src = 2 * px + py
                pltpu.make_async_remote_copy(
                    src_ref=ins[w].at[c], dst_ref=outs[w].at[src, c], send_sem=ici_send.at[3 * w + j],
                    recv_sem=ici_recv.at[3 * w + j], device_id=(px, py, c), device_id_type=MESH).wait_recv()
                rd = pltpu.make_async_remote_copy(
                    src_ref=outs[w].at[src, c], dst_ref=outs[w].at[src, c], send_sem=d2d_send.at[4 * w + j],
                    recv_sem=d2d_recv.at[4 * w + j], device_id=sib, device_id_type=MESH)
                rd.start()
                started.append(rd)
        for w in range(n):
            for j, (px, py) in enumerate(chips):
                pltpu.make_async_remote_copy(
                    src_ref=ins[w].at[c], dst_ref=outs[w].at[2 * px + py, 1 - c], send_sem=d2d_send.at[4 * w + j],
                    recv_sem=d2d_recv.at[4 * w + j], device_id=sib, device_id_type=MESH).wait_recv()
            pltpu.make_async_remote_copy(
                src_ref=ins[w].at[c], dst_ref=outs[w].at[me, 1 - c], send_sem=d2d_send.at[4 * w + 3],
                recv_sem=d2d_recv.at[4 * w + 3], device_id=sib, device_id_type=MESH).wait_recv()
        for rd in started:
            rd.wait_send()
        for cp in copies:
            cp.wait()

    return pl.pallas_call(
        body, name="gather_weights",
        in_specs=[HBM] * n, out_specs=[HBM] * n,
        out_shape=[jax.ShapeDtypeStruct((N_CHIPS,) + s.shape, s.dtype) for s in shards],
        scratch_shapes=[pltpu.SemaphoreType.DMA((3 * n,)), pltpu.SemaphoreType.DMA((3 * n,)),
                        pltpu.SemaphoreType.DMA((4 * n,)), pltpu.SemaphoreType.DMA((4 * n,)),
                        pltpu.SemaphoreType.DMA((n,))],
        compiler_params=pltpu.CompilerParams(has_side_effects=True),
    )(*shards)


def _pair_split(grads):
    n = len(grads)

    def body(*refs):
        ins, theirs = refs[:n], refs[n:2 * n]
        send, recv = refs[2 * n:]
        x, y, c = _place()
        sib = (x, y, 1 - c)
        rds = []
        for w in range(n):
            rd = pltpu.make_async_remote_copy(
                src_ref=ins[w].at[:, 1 - c], dst_ref=theirs[w], send_sem=send.at[w], recv_sem=recv.at[w],
                device_id=sib, device_id_type=MESH)
            rd.start()
            rds.append(rd)
        for rd in rds:
            rd.wait_recv()
        for rd in rds:
            rd.wait_send()

    return pl.pallas_call(
        body, name="grad_pair_split",
        in_specs=[HBM] * n, out_specs=[HBM] * n,
        out_shape=[jax.ShapeDtypeStruct((g.shape[0],) + g.shape[2:], g.dtype) for g in grads],
        scratch_shapes=[pltpu.SemaphoreType.DMA((n,)), pltpu.SemaphoreType.DMA((n,))],
        compiler_params=pltpu.CompilerParams(has_side_effects=True),
    )(*grads)


def _chip_scatter(parts):
    n = len(parts)

    def body(*refs):
        ins, outs = refs[:n], refs[n:2 * n]
        send, recv = refs[2 * n:]
        x, y, c = _place()
        me = 2 * x + y
        rds = []
        for w in range(n):
            for j, (px, py) in enumerate(_other_chips(x, y)):
                s = 3 * w + j
                rd = pltpu.make_async_remote_copy(
                    src_ref=ins[w].at[2 * px + py], dst_ref=outs[w].at[j], send_sem=send.at[s], recv_sem=recv.at[s],
                    device_id=(px, py, c), device_id_type=MESH)
                rd.start()
                rds.append(rd)
        for w in range(n):
            for j, (px, py) in enumerate(_other_chips(x, y)):
                s = 3 * w + j
                pltpu.make_async_remote_copy(
                    src_ref=ins[w].at[me], dst_ref=outs[w].at[j], send_sem=send.at[s], recv_sem=recv.at[s],
                    device_id=(px, py, c), device_id_type=MESH).wait_recv()
        for rd in rds:
            rd.wait_send()

    return pl.pallas_call(
        body, name="grad_chip_scatter",
        in_specs=[HBM] * n, out_specs=[HBM] * n,
        out_shape=[jax.ShapeDtypeStruct((N_CHIPS - 1,) + p.shape[1:], p.dtype) for p in parts],
        scratch_shapes=[pltpu.SemaphoreType.DMA((3 * n,)), pltpu.SemaphoreType.DMA((3 * n,))],
        compiler_params=pltpu.CompilerParams(has_side_effects=True),
    )(*parts)


def _pair_join(bufs):
    n = len(bufs)

    def body(*refs):
        ins, outs = refs[:n], refs[n:2 * n]
        send, recv = refs[2 * n:]
        x, y, c = _place()
        sib = (x, y, 1 - c)
        rds = []
        for w in range(n):
            rd = pltpu.make_async_remote_copy(
                src_ref=ins[w].at[c], dst_ref=outs[w].at[c], send_sem=send.at[w], recv_sem=recv.at[w],
                device_id=sib, device_id_type=MESH)
            rd.start()
            rds.append(rd)
        for w in range(n):
            pltpu.make_async_remote_copy(
                src_ref=ins[w].at[c], dst_ref=outs[w].at[1 - c], send_sem=send.at[w], recv_sem=recv.at[w],
                device_id=sib, device_id_type=MESH).wait_recv()
        for rd in rds:
            rd.wait_send()

    return pl.pallas_call(
        body, name="grad_pair_join",
        in_specs=[HBM] * n, out_specs=[HBM] * n,
        out_shape=[jax.ShapeDtypeStruct(b.shape, b.dtype) for b in bufs],
        input_output_aliases={w: w for w in range(n)},
        scratch_shapes=[pltpu.SemaphoreType.DMA((n,)), pltpu.SemaphoreType.DMA((n,))],
        compiler_params=pltpu.CompilerParams(has_side_effects=True),
    )(*bufs)


def _all_reduce_small(packed):
    R = packed.shape[0]

    def body(x_ref, o_ref, buf, send, recv):
        x, y, c = _place()
        me = 4 * x + 2 * y + c
        buf[me] = x_ref[...]
        rds = []
        for rel in range(1, N_DEV):
            fx, fy, fc = (rel >> 2) & 1, (rel >> 1) & 1, rel & 1
            peer = (1 - x if fx else x, 1 - y if fy else y, 1 - c if fc else c)
            rd = pltpu.make_async_remote_copy(
                src_ref=x_ref, dst_ref=buf.at[me], send_sem=send.at[rel - 1], recv_sem=recv.at[rel - 1],
                device_id=peer, device_id_type=MESH)
            rd.start()
            rds.append((rd, peer))
        for rel in range(1, N_DEV):
            rd, (px, py, pc) = rds[rel - 1]
            pltpu.make_async_remote_copy(
                src_ref=x_ref, dst_ref=buf.at[4 * px + 2 * py + pc], send_sem=send.at[rel - 1], recv_sem=recv.at[rel - 1],
                device_id=(px, py, pc), device_id_type=MESH).wait_recv()
        for rd, _ in rds:
            rd.wait_send()
        total = buf[0]
        for d in range(1, N_DEV):
            total = total + buf[d]
        o_ref[...] = total

    return pl.pallas_call(
        body, name="all_reduce_small",
        in_specs=[pl.BlockSpec(memory_space=pltpu.VMEM)],
        out_specs=pl.BlockSpec(memory_space=pltpu.VMEM),
        out_shape=jax.ShapeDtypeStruct(packed.shape, F32),
        scratch_shapes=[pltpu.VMEM((N_DEV, R, LANES), F32), pltpu.SemaphoreType.DMA((N_DEV - 1,)),
                        pltpu.SemaphoreType.DMA((N_DEV - 1,))],
        compiler_params=pltpu.CompilerParams(has_side_effects=True),
    )(packed)


def _pair_sum(name, split, theirs, core):
    n_chip, _, Rh, C = split.shape
    tile = _div_tile(Rh, 256, 2 * SUBLANES)
    nt = Rh // tile

    def body(core_ref, a_ref, b_ref, o_ref):
        o_ref[...] = (a_ref[...] + b_ref[...]).astype(o_ref.dtype)

    return pl.pallas_call(
        body, name=name,
        grid_spec=pltpu.PrefetchScalarGridSpec(
            num_scalar_prefetch=1,
            grid=(n_chip, nt),
            in_specs=[pl.BlockSpec((None, None, tile, C), lambda j, i, core_ref: (j, core_ref[0], i, 0)),
                      pl.BlockSpec((None, tile, C), lambda j, i, core_ref: (j, i, 0))],
            out_specs=pl.BlockSpec((None, tile, C), lambda j, i, core_ref: (j, i, 0)),
        ),
        out_shape=jax.ShapeDtypeStruct((n_chip, Rh, C), BF16),
        compiler_params=_params(("parallel", "parallel")),
    )(core, split, theirs)


def _chip_sum(name, own, landed, core):
    n_in, Rh, C = landed.shape
    tile = _div_tile(Rh, 256, 2 * SUBLANES)

    def body(core_ref, *refs):
        total = refs[0][...].astype(F32)
        for ref in refs[1:n_in + 1]:
            total = total + ref[...].astype(F32)
        refs[n_in + 1][...] = total

    slot = lambda j: pl.BlockSpec((None, tile, C), lambda i, core_ref: (j, i, 0))
    return pl.pallas_call(
        body, name=name,
        grid_spec=pltpu.PrefetchScalarGridSpec(
            num_scalar_prefetch=1,
            grid=(Rh // tile,),
            in_specs=[pl.BlockSpec((None, tile, C), lambda i, core_ref: (core_ref[1], i, 0))]
                     + [slot(j) for j in range(n_in)],
            out_specs=pl.BlockSpec((None, tile, C), lambda i, core_ref: (core_ref[0], i, 0)),
        ),
        out_shape=jax.ShapeDtypeStruct((2, Rh, C), F32),
        compiler_params=_params(("parallel",)),
    )(core, own, *([landed] * n_in))


def _adamw(name, w, g, m, v):
    R, C = w.shape
    tile = _div_tile(R, 256, SUBLANES)
    c1 = 1.0 / (1.0 - ADAM_B1 ** ADAM_STEP)
    c2 = 1.0 / (1.0 - ADAM_B2 ** ADAM_STEP)

    def body(w_ref, g_ref, m_ref, v_ref, d_ref, nm_ref, nv_ref):
        g_ = g_ref[...]
        nm = ADAM_B1 * m_ref[...] + (1.0 - ADAM_B1) * g_
        nv = ADAM_B2 * v_ref[...] + (1.0 - ADAM_B2) * (g_ * g_)
        d_ref[...] = -ADAM_LR * ((nm * c1) / (jnp.sqrt(nv * c2) + ADAM_EPS) + ADAM_WD * w_ref[...])
        nm_ref[...] = nm
        nv_ref[...] = nv

    spec = pl.BlockSpec((tile, C), lambda i: (i, 0))
    return pl.pallas_call(
        body, name=name,
        grid=(R // tile,),
        in_specs=[spec] * 4, out_specs=[spec] * 3,
        out_shape=[jax.ShapeDtypeStruct((R, C), F32)] * 3,
        compiler_params=_params(("parallel",)),
    )(w, g, m, v)


def _cols_to_shards(full):
    K, N = full.shape
    return full.reshape(K, N_CHIPS, N // N_CHIPS).transpose(1, 0, 2)


def _shards_to_cols(sh):
    return sh.transpose(1, 0, 2).reshape(sh.shape[1], -1)


def _rows_to_shards(full):
    return full.reshape(N_CHIPS, full.shape[0] // N_CHIPS, full.shape[1])


def _shift_tokens(p, B, S):
    p3 = p.reshape(B, S, p.shape[1])
    return jnp.pad(p3, ((0, 0), (1, 0), (0, 0)))[:, :-1].reshape(p.shape)


def _unshift_tokens(d, B, S):
    d3 = d.reshape(B, S, d.shape[1])
    return jnp.pad(d3, ((0, 0), (0, 1), (0, 0)))[:, 1:].reshape(d.shape)


SMALL = ["norm_mix_pre", "b_gate", "mu_rw", "w0", "a0", "k_k", "k_a", "r_k", "lnx_w", "lnx_b",
         "norm_mix_post", "norm_ffn_pre", "norm_ffn_post"]
BIG = ["w_in", "w_up", "a_up", "g_up", "w_sb_out", "w_rw_out", "w_o", "w_ffn_gate", "w_ffn_up", "w_ffn_down"]
ROW_SHARDED = ("w_o", "w_ffn_down")
ORDER = ["norm_mix_pre", "w_in", "b_gate", "mu_rw", "w0", "w_up", "a0", "a_up", "g_up", "k_k", "k_a", "r_k",
         "lnx_w", "lnx_b", "w_sb_out", "w_rw_out", "w_o", "norm_mix_post", "norm_ffn_pre", "w_ffn_gate",
         "w_ffn_up", "w_ffn_down", "norm_ffn_post"]


def _pack_small(vals, extra_rows=0):
    rows = jnp.concatenate([vals[n].reshape(-1, LANES) for n in SMALL], axis=0)
    pad = (-(rows.shape[0] + extra_rows)) % SUBLANES + extra_rows
    return jnp.pad(rows, ((0, pad), (0, 0)))


def _unpack_small(packed, shapes):
    out, r = {}, 0
    for n in SMALL:
        size = 1
        for s in shapes[n]:
            size *= s
        out[n] = packed[r:r + size // LANES].reshape(shapes[n])
        r += size // LANES
    return out


def kernel(x, norm_mix_pre, w_in, b_gate, mu_rw, w0, w_up, a0, a_up, g_up, k_k, k_a, r_k, lnx_w, lnx_b, w_sb_out, w_rw_out, w_o, norm_mix_post, norm_ffn_pre, w_ffn_gate, w_ffn_up, w_ffn_down, norm_ffn_post, loss_target, m_norm_mix_pre, m_w_in, m_b_gate, m_mu_rw, m_w0, m_w_up, m_a0, m_a_up, m_g_up, m_k_k, m_k_a, m_r_k, m_lnx_w, m_lnx_b, m_w_sb_out, m_w_rw_out, m_w_o, m_norm_mix_post, m_norm_ffn_pre, m_w_ffn_gate, m_w_ffn_up, m_w_ffn_down, m_norm_ffn_post, v_norm_mix_pre, v_w_in, v_b_gate, v_mu_rw, v_w0, v_w_up, v_a0, v_a_up, v_g_up, v_k_k, v_k_a, v_r_k, v_lnx_w, v_lnx_b, v_w_sb_out, v_w_rw_out, v_w_o, v_norm_mix_post, v_norm_ffn_pre, v_w_ffn_gate, v_w_ffn_up, v_w_ffn_down, v_norm_ffn_post):
    W = dict(norm_mix_pre=norm_mix_pre, w_in=w_in, b_gate=b_gate, mu_rw=mu_rw, w0=w0, w_up=w_up, a0=a0, a_up=a_up,
             g_up=g_up, k_k=k_k, k_a=k_a, r_k=r_k, lnx_w=lnx_w, lnx_b=lnx_b, w_sb_out=w_sb_out, w_rw_out=w_rw_out,
             w_o=w_o, norm_mix_post=norm_mix_post, norm_ffn_pre=norm_ffn_pre, w_ffn_gate=w_ffn_gate,
             w_ffn_up=w_ffn_up, w_ffn_down=w_ffn_down, norm_ffn_post=norm_ffn_post)
    Mo = dict(norm_mix_pre=m_norm_mix_pre, w_in=m_w_in, b_gate=m_b_gate, mu_rw=m_mu_rw, w0=m_w0, w_up=m_w_up, a0=m_a0,
              a_up=m_a_up, g_up=m_g_up, k_k=m_k_k, k_a=m_k_a, r_k=m_r_k, lnx_w=m_lnx_w, lnx_b=m_lnx_b,
              w_sb_out=m_w_sb_out, w_rw_out=m_w_rw_out, w_o=m_w_o, norm_mix_post=m_norm_mix_post,
              norm_ffn_pre=m_norm_ffn_pre, w_ffn_gate=m_w_ffn_gate, w_ffn_up=m_w_ffn_up, w_ffn_down=m_w_ffn_down,
              norm_ffn_post=m_norm_ffn_post)
    Vo = dict(norm_mix_pre=v_norm_mix_pre, w_in=v_w_in, b_gate=v_b_gate, mu_rw=v_mu_rw, w0=v_w0, w_up=v_w_up, a0=v_a0,
              a_up=v_a_up, g_up=v_g_up, k_k=v_k_k, k_a=v_k_a, r_k=v_r_k, lnx_w=v_lnx_w, lnx_b=v_lnx_b,
              w_sb_out=v_w_sb_out, w_rw_out=v_w_rw_out, w_o=v_w_o, norm_mix_post=v_norm_mix_post,
              norm_ffn_pre=v_norm_ffn_pre, w_ffn_gate=v_w_ffn_gate, w_ffn_up=v_w_ffn_up, w_ffn_down=v_w_ffn_down,
              norm_ffn_post=v_norm_ffn_post)
    shapes = {n: W[n].shape for n in ORDER}
    B, S, D = x.shape
    T = B * S
    x2 = x.reshape(T, D)
    tgt = loss_target.reshape(T, D)
    vec = {n: W[n].reshape(1, -1) for n in SMALL}

    halved = [W[n][0].astype(BF16).reshape(2, W[n].shape[1] // 2, W[n].shape[2]) for n in BIG]
    full = {}
    for n, gth in zip(BIG, _all_gather_chips(halved)):
        gth = gth.reshape((N_CHIPS,) + W[n].shape[1:])
        full[n] = gth.reshape(-1, gth.shape[2]) if n in ROW_SHARDED else _shards_to_cols(gth)
    w_sb, w_rw, w_gt = full["w_in"][:, :SB_COLS], full["w_in"][:, SB_COLS:SB_COLS + RW_COLS], full["w_in"][:, SB_COLS + RW_COLS:]
    lora_rows = {"w_up": 0, "a_up": 64, "g_up": 128}
    lora = {n: jnp.pad(full[n], ((r0, LORA_COLS - r0 - full[n].shape[0]), (0, 0))) for n, r0 in lora_rows.items()}
    mu = vec["mu_rw"]
    mu_parts = [mu[:, :512], mu[:, 512:1024], mu[:, 1024:1536], mu[:, 1536:]]
    b1, b2 = vec["b_gate"][:, :D], vec["b_gate"][:, D:]

    (h1,) = _rowwise("norm_mix_pre", _f_norm, [(x2, D, 0)], [vec["norm_mix_pre"]], [(D, BF16)])
    p_sb = _mm("proj_sb", h1, w_sb)
    p_rw = _mm("proj_rw", h1, w_rw)
    p_gt = _mm("proj_gate", h1, w_gt)
    o_sb = _attn_fwd(p_sb, B, S)
    p_prev = _shift_tokens(p_rw, B, S)
    pre_rows = [(p_rw, 512, 0), (p_rw, 512, 1), (p_rw, 512, 2), (p_rw, LORA_COLS, 6),
                (p_prev, 512, 0), (p_prev, 512, 1), (p_prev, 512, 2), (p_prev, LORA_COLS, 6)]
    pre_params = mu_parts + [vec["w0"], lora["w_up"], vec["a0"], lora["a_up"], lora["g_up"], vec["k_k"], vec["k_a"]]
    r_, lw_, k2_, v_, kap_, a_, g_ = _rowwise("rw_pre", _f_rwpre, pre_rows, pre_params, [(512, F32)] * 7, tile=128)
    y_wkv, states = _wkv_fwd(r_, lw_, k2_, v_, kap_, a_, B, S)
    post_rows = [(y_wkv, 512, 0), (r_, 512, 0), (k2_, 512, 0), (v_, 512, 0), (g_, 512, 0)]
    post_params = [vec["lnx_w"], vec["lnx_b"], vec["r_k"]]
    (o_rw,) = _rowwise("rw_post", _f_rwpost, post_rows, post_params, [(512, BF16)])
    m1 = _mm("mix_sb_out", o_sb, full["w_sb_out"])
    m2 = _mm("mix_rw_out", o_rw, full["w_rw_out"])
    merge_rows = [(p_gt, D, 0), (p_gt, D, 1), (m1, D, 0), (m2, D, 0)]
    (merged,) = _rowwise("merge", _f_merge, merge_rows, [b1, b2], [(D, BF16)])
    u = _mm("mix_out", merged, full["w_o"])
    post1_params = [vec["norm_mix_post"], vec["norm_ffn_pre"]]
    x1, h2 = _rowwise("post_mix", _f_post1, [(x2, D, 0), (u, D, 0)], post1_params, [(D, F32), (D, BF16)])
    ag = _mm("ffn_gate", h2, full["w_ffn_gate"])
    au = _mm("ffn_up", h2, full["w_ffn_up"])
    (sw,) = _rowwise("swiglu", _f_swiglu, [(ag, D_FF, 0), (au, D_FF, 0)], [], [(D_FF, BF16)])
    f = _mm("ffn_down", sw, full["w_ffn_down"])
    loss_part, dx1, df, dg4 = _loss_head(x1, f, tgt, vec["norm_ffn_post"])

    gbig, gsmall = {}, {"norm_ffn_post": dg4}
    dsw = _mm("d_swiglu_out", df, full["w_ffn_down"], tb=True)
    gbig["w_ffn_down"] = _mm("g_ffn_down", sw, df, ta=True)
    (dag, dau), _ = _rowwise_vjp("swiglu_bwd", _f_swiglu, [(ag, D_FF, 0), (au, D_FF, 0)], [], [[dsw]], [True, True], [],
                                 bf16_rows=(0, 1))
    dh2 = _mm("d_h2_gate", dag, full["w_ffn_gate"], tb=True)
    dh2 = _mm("d_h2_up", dau, full["w_ffn_up"], tb=True, acc=dh2)
    gbig["w_ffn_gate"] = _mm("g_ffn_gate", h2, dag, ta=True)
    gbig["w_ffn_up"] = _mm("g_ffn_up", h2, dau, ta=True)
    (dx_res, du), (dg2, dg3) = _rowwise_vjp("post_mix_bwd", _f_post1, [(x2, D, 0), (u, D, 0)], post1_params,
                                            [[dx1], [dh2]], [True, True], [True, True], bf16_rows=(1,))
    gsmall["norm_mix_post"], gsmall["norm_ffn_pre"] = dg2, dg3
    dmerged = _mm("d_merged", du, full["w_o"], tb=True)
    gbig["w_o"] = _mm("g_w_o", merged, du, ta=True)
    (dpg1, dpg2, dm1, dm2), (db1, db2) = _rowwise_vjp("merge_bwd", _f_merge, merge_rows, [b1, b2], [[dmerged]],
                                                      [True] * 4, [True, True], bf16_rows=(0, 1, 2, 3))
    gsmall["b_gate"] = jnp.concatenate([db1, db2], axis=1)
    do_sb = _mm("d_o_sb", dm1, full["w_sb_out"], tb=True)
    do_rw = _mm("d_o_rw", dm2, full["w_rw_out"], tb=True)
    gbig["w_sb_out"] = _mm("g_sb_out", o_sb, dm1, ta=True)
    gbig["w_rw_out"] = _mm("g_rw_out", o_rw, dm2, ta=True)
    (dy_wkv, dr_a, dk2_a, dv_a, dg_), (dlnx_w, dlnx_b, dr_k) = _rowwise_vjp(
        "rw_post_bwd", _f_rwpost, post_rows, post_params, [[do_rw]], [True] * 5, [True] * 3)
    gsmall["lnx_w"], gsmall["lnx_b"], gsmall["r_k"] = dlnx_w, dlnx_b, dr_k
    dr_b, dlw, dk2_b, dv_b, dkap, da = _wkv_bwd(r_, lw_, k2_, v_, kap_, a_, states, dy_wkv, B, S)
    pre_cts = [[dr_a, dr_b], [dlw], [dk2_a, dk2_b], [dv_a, dv_b], [dkap], [da], [dg_]]
    dpre_rows, dpre_params = _rowwise_vjp("rw_pre_bwd", _f_rwpre, pre_rows, pre_params, pre_cts,
                                          [True] * 8, [True] * 11, tile=128)
    dp_rw = jnp.concatenate(dpre_rows[:4], axis=1) + _unshift_tokens(jnp.concatenate(dpre_rows[4:], axis=1), B, S)
    dp_rw = dp_rw.astype(BF16)
    gsmall["mu_rw"] = jnp.concatenate(dpre_params[:4], axis=1)
    gsmall["w0"], gsmall["a0"], gsmall["k_k"], gsmall["k_a"] = dpre_params[4], dpre_params[6], dpre_params[9], dpre_params[10]
    glora = {"w_up": dpre_params[5][0:64], "a_up": dpre_params[7][64:128], "g_up": dpre_params[8][128:256]}
    dq, dk, dv = _attn_bwd(p_sb, o_sb, do_sb, B, S)
    dh1 = _mm("d_h1_q", dq, w_sb[:, :512], tb=True)
    dh1 = _mm("d_h1_k", dk, w_sb[:, 512:1024], tb=True, acc=dh1)
    dh1 = _mm("d_h1_v", dv, w_sb[:, 1024:], tb=True, acc=dh1)
    dh1 = _mm("d_h1_rw", dp_rw, w_rw, tb=True, acc=dh1)
    dh1 = _mm("d_h1_g1", dpg1, w_gt[:, :D], tb=True, acc=dh1)
    dh1 = _mm("d_h1_g2", dpg2, w_gt[:, D:], tb=True, acc=dh1)
    gbig["w_in"] = jnp.concatenate(
        [_mm("g_in_" + tag, h1, d, ta=True)
         for tag, d in (("q", dq), ("k", dk), ("v", dv), ("rw", dp_rw), ("g1", dpg1), ("g2", dpg2))], axis=1)
    (grad_x2,), (dg1,) = _rowwise_vjp("norm_mix_pre_bwd", _f_norm, [(x2, D, 0)], [vec["norm_mix_pre"]], [[dh1]],
                                      [True], [True], add_to={0: dx_res})
    gsmall["norm_mix_pre"] = dg1
    gbig.update(glora)

    split = []
    for n in BIG:
        g = _rows_to_shards(gbig[n]) if n in ROW_SHARDED else _cols_to_shards(gbig[n])
        split.append(g.reshape(N_CHIPS, 2, g.shape[1] // 2, g.shape[2]))
    core = jnp.stack([lax.axis_index("c"), 2 * lax.axis_index("x") + lax.axis_index("y")]).astype(jnp.int32)
    theirs = _pair_split(split)
    chip_sums = [_pair_sum("pair_sum_" + n, a, b, core) for n, a, b in zip(BIG, split, theirs)]
    landed = _chip_scatter(chip_sums)
    joined = _pair_join([_chip_sum("chip_sum_" + n, own, got, core) for n, own, got in zip(BIG, chip_sums, landed)])
    grads = {n: j.reshape(W[n].shape[1:]) for n, j in zip(BIG, joined)}

    small_local = _pack_small({n: gsmall[n] for n in SMALL}, extra_rows=1)
    loss_row = small_local.shape[0] - 1
    small_local = small_local.at[loss_row].set(loss_part[0])
    small_sum = _all_reduce_small(small_local)
    loss = small_sum[loss_row, 0]

    delta, new_m, new_v = {}, {}, {}
    for n in BIG:
        d_, m_, v2_ = _adamw("adamw_" + n, W[n][0], grads[n], Mo[n][0], Vo[n][0])
        delta[n], new_m[n], new_v[n] = d_[None], m_[None], v2_[None]
        grads[n] = grads[n][None]
    pk = lambda src: _pack_small({n: src[n] for n in SMALL}, extra_rows=1)
    d_s, m_s, v_s = _adamw("adamw_small", pk(W), small_sum.at[loss_row].set(0.0), pk(Mo), pk(Vo))
    for dst, packed in ((grads, small_sum), (delta, d_s), (new_m, m_s), (new_v, v_s)):
        dst.update(_unpack_small(packed, shapes))

    return (loss, grad_x2.reshape(B, S, D), *[grads[n] for n in ORDER], *[delta[n] for n in ORDER],
            *[new_m[n] for n in ORDER], *[new_v[n] for n in ORDER])
```

```python
import functools

import jax
import jax.numpy as jnp
from jax import lax
from jax.experimental import pallas as pl
from jax.experimental.pallas import tpu as pltpu

F32 = jnp.float32
BF16 = jnp.bfloat16
MESH = pl.DeviceIdType.MESH

D_MODEL = 1024
SB_HEADS = 8
HEAD_DIM = 64
SB_WIDTH = SB_HEADS * HEAD_DIM
RW_WIDTH = 512
LORA_COLS = 256
SB_COLS = 3 * SB_WIDTH
RW_COLS = 3 * RW_WIDTH + LORA_COLS
GATE_COLS = 2 * D_MODEL
D_FF = 2816
RMS_EPS = 1e-6
GN_EPS = HEAD_DIM * 1e-5
WKV_CHUNK = 64
ATTN_QUERIES = 512
ATTN_KEYS = 128
LANES = 128
SUBLANES = 8
N_CHIPS = 4
N_DEV = 8

ADAM_LR = 0.001
ADAM_B1 = 0.9
ADAM_B2 = 0.999
ADAM_EPS = 1e-08
ADAM_WD = 0.01
ADAM_STEP = 10

VMEM_LIMIT = 48 * 1024 * 1024


def _params(sem=None, **kw):
    if sem is not None:
        kw["dimension_semantics"] = sem
    return pltpu.CompilerParams(vmem_limit_bytes=VMEM_LIMIT, **kw)


def _div_tile(dim, pref, mult=LANES):
    if dim <= pref:
        return dim
    t = pref - pref % mult
    while t >= mult:
        if dim % t == 0:
            return t
        t -= mult
    return dim


def _dot(a, b, dims):
    return lax.dot_general(a, b, (dims, ((), ())), preferred_element_type=F32)


def _mm(name, a, b, *, ta=False, tb=False, acc=None, out_dtype=F32):
    if ta:
        K, M = a.shape
    else:
        M, K = a.shape
    N = b.shape[0] if tb else b.shape[1]
    if ta:
        tm, tn, tk = _div_tile(M, 1408), _div_tile(N, 1408), _div_tile(K, 512)
    else:
        tm, tn, tk = _div_tile(M, 512), _div_tile(N, 1408), _div_tile(K, 1408)
    nk = K // tk
    dims = ((0,) if ta else (1,), (1,) if tb else (0,))
    has_acc = acc is not None

    def body(*refs):
        a_ref, b_ref = refs[0], refs[1]
        o_ref, scr = refs[-2], refs[-1]
        k = pl.program_id(2)
        part = _dot(a_ref[...].astype(BF16), b_ref[...].astype(BF16), dims)

        @pl.when(k == 0)
        def _():
            scr[...] = part + refs[2][...] if has_acc else part

        @pl.when(k > 0)
        def _():
            scr[...] += part

        @pl.when(k == nk - 1)
        def _():
            o_ref[...] = scr[...].astype(o_ref.dtype)

    a_spec = pl.BlockSpec((tk, tm), lambda i, j, k: (k, i)) if ta else pl.BlockSpec((tm, tk), lambda i, j, k: (i, k))
    b_spec = pl.BlockSpec((tn, tk), lambda i, j, k: (j, k)) if tb else pl.BlockSpec((tk, tn), lambda i, j, k: (k, j))
    o_spec = pl.BlockSpec((tm, tn), lambda i, j, k: (i, j))
    return pl.pallas_call(
        body, name=name,
        grid=(M // tm, N // tn, nk),
        in_specs=[a_spec, b_spec] + ([o_spec] if has_acc else []),
        out_specs=o_spec,
        out_shape=jax.ShapeDtypeStruct((M, N), out_dtype),
        scratch_shapes=[pltpu.VMEM((tm, tn), F32)],
        compiler_params=_params(("parallel", "parallel", "arbitrary")),
    )(*([a, b] + ([acc] if has_acc else [])))


def _row_spec(tile, width, colblk):
    return pl.BlockSpec((tile, width), lambda i: (i, colblk))


def _full_spec(shape):
    return pl.BlockSpec(shape, lambda i: (0,) * len(shape))


def _rowwise(name, fn, rows, params, outs, tile=256):
    T = rows[0][0].shape[0]
    tile = min(tile, T)
    n_r, n_p = len(rows), len(params)

    def body(*refs):
        r = [x[...].astype(F32) for x in refs[:n_r]]
        p = [x[...].astype(F32) for x in refs[n_r:n_r + n_p]]
        for o_ref, val in zip(refs[n_r + n_p:], fn(*r, *p)):
            o_ref[...] = val.astype(o_ref.dtype)

    return pl.pallas_call(
        body, name=name,
        grid=(T // tile,),
        in_specs=[_row_spec(tile, w, cb) for _, w, cb in rows] + [_full_spec(p.shape) for p in params],
        out_specs=[_row_spec(tile, w, 0) for w, _ in outs],
        out_shape=[jax.ShapeDtypeStruct((T, w), dt) for w, dt in outs],
        compiler_params=_params(("parallel",)),
    )(*([a for a, _, _ in rows] + list(params)))


def _rowwise_vjp(name, fn, rows, params, cts, need_rows, need_params, add_to=None, tile=256, bf16_rows=()):
    add_to = add_to or {}
    T = rows[0][0].shape[0]
    tile = min(tile, T)
    n_r, n_p = len(rows), len(params)
    ct_flat = [c for group in cts for c in group]
    ct_sizes = [len(group) for group in cts]
    add_idx = sorted(add_to)
    row_out = [i for i in range(n_r) if need_rows[i]]
    par_out = [i for i in range(n_p) if need_params[i]]
    n_ct, n_add = len(ct_flat), len(add_idx)

    def body(*refs):
        pos = 0
        r = [x[...].astype(F32) for x in refs[pos:pos + n_r]]
        pos += n_r
        p = [x[...].astype(F32) for x in refs[pos:pos + n_p]]
        pos += n_p
        ct_vals = [x[...].astype(F32) for x in refs[pos:pos + n_ct]]
        pos += n_ct
        adds = {i: x[...] for i, x in zip(add_idx, refs[pos:pos + n_add])}
        pos += n_add
        drow_refs = refs[pos:pos + len(row_out)]
        pos += len(row_out)
        dpar_refs = refs[pos:pos + len(par_out)]
        ct_in, q = [], 0
        for n in ct_sizes:
            ct_in.append(functools.reduce(lambda u, v: u + v, ct_vals[q:q + n]))
            q += n
        _, vjp = jax.vjp(fn, *r, *p)
        grads = vjp(tuple(ct_in))
        for ref, i in zip(drow_refs, row_out):
            g = grads[i]
            ref[...] = (g + adds[i] if i in adds else g).astype(ref.dtype)

        @pl.when(pl.program_id(0) == 0)
        def _():
            for ref in dpar_refs:
                ref[...] = jnp.zeros_like(ref)

        for ref, i in zip(dpar_refs, par_out):
            ref[...] += grads[n_r + i]

    ct_widths = [c.shape[1] for c in ct_flat]
    in_specs = ([_row_spec(tile, w, cb) for _, w, cb in rows] + [_full_spec(p.shape) for p in params]
                + [_row_spec(tile, w, 0) for w in ct_widths] + [_row_spec(tile, rows[i][1], 0) for i in add_idx])
    out_specs = [_row_spec(tile, rows[i][1], 0) for i in row_out] + [_full_spec(params[i].shape) for i in par_out]
    out_shape = ([jax.ShapeDtypeStruct((T, rows[i][1]), BF16 if i in bf16_rows else F32) for i in row_out]
                 + [jax.ShapeDtypeStruct(params[i].shape, F32) for i in par_out])
    res = pl.pallas_call(
        body, name=name,
        grid=(T // tile,),
        in_specs=in_specs, out_specs=out_specs, out_shape=out_shape,
        compiler_params=_params(("arbitrary",)),
    )(*([a for a, _, _ in rows] + list(params) + ct_flat + [add_to[i] for i in add_idx]))
    return res[:len(row_out)], res[len(row_out):]


def _sigmoid(x):
    return 0.5 * (jnp.tanh(0.5 * x) + 1.0)


def _softplus(x):
    return jnp.maximum(x, 0.0) + jnp.log(1.0 + jnp.exp(-jnp.abs(x)))


def _rms(x, g):
    return x * lax.rsqrt(jnp.mean(x * x, axis=-1, keepdims=True) + RMS_EPS) * g


def _segsum_impl(x):
    n = x.shape[-1]
    r = lax.shift_right_logical(lax.broadcasted_iota(jnp.int32, (n, n), 0), 6)
    c = lax.shift_right_logical(lax.broadcasted_iota(jnp.int32, (n, n), 1), 6)
    bd = (r == c).astype(BF16)
    hi = x.astype(BF16)
    rest = x - hi.astype(F32)
    mid = rest.astype(BF16)
    lo = (rest - mid.astype(F32)).astype(BF16)
    nn = ((1,), (0,))
    return _dot(hi, bd, nn) + _dot(mid, bd, nn) + _dot(lo, bd, nn)


@jax.custom_vjp
def _segsum(x):
    return _segsum_impl(x)


_segsum.defvjp(lambda x: (_segsum_impl(x), None), lambda _, g: (_segsum_impl(g),))


@jax.custom_vjp
def _mmb(a, w):
    return _dot(a.astype(BF16), w.astype(BF16), ((1,), (0,)))


def _mmb_fwd(a, w):
    return _mmb(a, w), (a, w)


def _mmb_bwd(res, g):
    a, w = res
    gb = g.astype(BF16)
    return _dot(gb, w.astype(BF16), ((1,), (1,))), _dot(a.astype(BF16), gb, ((0,), (0,)))


_mmb.defvjp(_mmb_fwd, _mmb_bwd)


def _f_norm(x, g):
    return (_rms(x, g),)


def _f_post1(x, u, g2, g3):
    x1 = x + _rms(u, g2)
    return x1, _rms(x1, g3)


def _f_swiglu(ag, au):
    return (ag * _sigmoid(ag) * au,)


def _f_merge(pg1, pg2, m1, m2, b1, b2):
    return (_sigmoid(pg1 + b1) * m1 + _sigmoid(pg2 + b2) * m2,)


def _f_out(x1, f, g4):
    return (x1 + _rms(f, g4),)


def _f_rwpre(pr, pk, pv, pz, qr, qk, qv, qz, mur, muk, muv, muz, w0, wup, a0, aup, gup, k_k, k_a):
    r = pr + (qr - pr) * mur
    k = pk + (qk - pk) * muk
    v = pv + (qv - pv) * muv
    z = pz + (qz - pz) * muz
    w_raw = w0 + _mmb(jnp.tanh(z), wup)
    lw = -jnp.exp(-_softplus(-w_raw) - 0.5)
    a = _sigmoid(a0 + _mmb(z, aup))
    g = _mmb(_sigmoid(z), gup)
    kk = k * k_k
    kap = kk * lax.rsqrt(jnp.maximum(_segsum(kk * kk), 1e-24))
    k2 = k * (1.0 + (a - 1.0) * k_a)
    return r, lw, k2, v, kap, a, g


def _f_rwpost(y, r, k2, v, g, lnx_w, lnx_b, r_k):
    inv = 1.0 / HEAD_DIM
    yc = y - _segsum(y) * inv
    var = _segsum(yc * yc) * inv
    yn = yc * lax.rsqrt(var + GN_EPS) * lnx_w + lnx_b
    bonus = _segsum(r * k2 * r_k) * v
    return ((yn + bonus) * g,)


RW_GROUPS = (0, 512, 1024, 1536, RW_COLS)


def _column_groups(p):
    return [p[:, a:b] for a, b in zip(RW_GROUPS[:-1], RW_GROUPS[1:])]


def _previous_tokens(p, halo, first_of_sequence):
    rows = lax.broadcasted_iota(jnp.int32, (p.shape[0], 1), 0)
    before = jnp.where(first_of_sequence, 0.0, halo[SUBLANES - 1:SUBLANES, :])
    return jnp.where(rows == 0, before, pltpu.roll(p, 1, axis=0))


def _halo_spec(tile, order):
    per = tile // SUBLANES
    return pl.BlockSpec((SUBLANES, RW_COLS), lambda i: (jnp.maximum(order(i) * per - 1, 0), 0))


def _rw_pre(p_rw, params, S, tile=128):
    T = p_rw.shape[0]
    tile = min(tile, T)
    assert S % tile == 0
    n_p = len(params)

    def body(*refs):
        p_ref, halo_ref = refs[0], refs[1]
        par = [x[...].astype(F32) for x in refs[2:2 + n_p]]
        p = p_ref[...]
        first = lax.rem(pl.program_id(0) * tile, S) == 0
        prev = _previous_tokens(p, halo_ref[...], first)
        for o_ref, val in zip(refs[2 + n_p:], _f_rwpre(*_column_groups(p), *_column_groups(prev), *par)):
            o_ref[...] = val

    out_spec = pl.BlockSpec((tile, RW_WIDTH), lambda i: (i, 0))
    return pl.pallas_call(
        body, name="rw_pre",
        grid=(T // tile,),
        in_specs=[pl.BlockSpec((tile, RW_COLS), lambda i: (i, 0)), _halo_spec(tile, lambda i: i)]
                 + [_full_spec(q.shape) for q in params],
        out_specs=[out_spec] * 7,
        out_shape=[jax.ShapeDtypeStruct((T, RW_WIDTH), F32)] * 7,
        compiler_params=_params(("parallel",)),
    )(p_rw, p_rw, *params)


def _rw_pre_bwd(p_rw, params, cts, S, tile=128):
    T = p_rw.shape[0]
    tile = min(tile, T)
    assert S % tile == 0
    nt = T // tile
    n_p = len(params)
    ct_flat = [c for group in cts for c in group]
    ct_sizes = [len(group) for group in cts]
    n_ct = len(ct_flat)

    def body(*refs):
        p_ref, halo_ref = refs[0], refs[1]
        par = [x[...].astype(F32) for x in refs[2:2 + n_p]]
        ct_vals = [x[...] for x in refs[2 + n_p:2 + n_p + n_ct]]
        dp_ref = refs[2 + n_p + n_ct]
        dpar_refs = refs[3 + n_p + n_ct:3 + 2 * n_p + n_ct]
        carry = refs[-1]
        step = pl.program_id(0)

        @pl.when(step == 0)
        def _():
            carry[...] = jnp.zeros_like(carry)
            for ref in dpar_refs:
                ref[...] = jnp.zeros_like(ref)

        ct_in, q = [], 0
        for n in ct_sizes:
            ct_in.append(functools.reduce(lambda u, v: u + v, ct_vals[q:q + n]))
            q += n
        p = p_ref[...]
        first = lax.rem((nt - 1 - step) * tile, S) == 0
        prev = _previous_tokens(p, halo_ref[...], first)
        _, vjp = jax.vjp(_f_rwpre, *_column_groups(p), *_column_groups(prev), *par)
        grads = vjp(tuple(ct_in))
        d_here = jnp.concatenate(grads[0:4], axis=1)
        d_prev = jnp.concatenate(grads[4:8], axis=1)
        rows = lax.broadcasted_iota(jnp.int32, (tile, 1), 0)
        from_next = jnp.where(rows == tile - 1, carry[0:1, :], pltpu.roll(d_prev, tile - 1, axis=0))
        dp_ref[...] = (d_here + from_next).astype(dp_ref.dtype)
        carry[...] = jnp.broadcast_to(jnp.where(first, 0.0, d_prev[0:1, :]), carry.shape)
        for ref, g in zip(dpar_refs, grads[8:]):
            ref[...] += g

    back = lambda i: nt - 1 - i
    row = lambda w: pl.BlockSpec((tile, w), lambda i: (back(i), 0))
    res = pl.pallas_call(
        body, name="rw_pre_bwd",
        grid=(nt,),
        in_specs=[row(RW_COLS), _halo_spec(tile, back)] + [_full_spec(q.shape) for q in params]
                 + [row(RW_WIDTH)] * n_ct,
        out_specs=[row(RW_COLS)] + [_full_spec(q.shape) for q in params],
        out_shape=[jax.ShapeDtypeStruct((T, RW_COLS), BF16)] + [jax.ShapeDtypeStruct(q.shape, F32) for q in params],
        scratch_shapes=[pltpu.VMEM((SUBLANES, RW_COLS), F32)],
        compiler_params=_params(("arbitrary",)),
    )(p_rw, p_rw, *params, *ct_flat)
    return res[0], res[1:]


def _loss_head(x1, f, target, g4, tile=256):
    T, D = x1.shape
    tile = min(tile, T)

    def body(x1_ref, f_ref, t_ref, g_ref, loss_ref, dx1_ref, df_ref, dg_ref):
        (y,), vjp = jax.vjp(_f_out, x1_ref[...], f_ref[...], g_ref[...])
        err = y - t_ref[...]
        dx1, df, dg = vjp((err * (1.0 / D),))
        dx1_ref[...] = dx1
        df_ref[...] = df.astype(df_ref.dtype)

        @pl.when(pl.program_id(0) == 0)
        def _():
            loss_ref[...] = jnp.zeros_like(loss_ref)
            dg_ref[...] = jnp.zeros_like(dg_ref)

        part = jnp.sum(jnp.sum(err * err, axis=1, keepdims=True), axis=0, keepdims=True) * (0.5 / D)
        loss_ref[...] += jnp.broadcast_to(part, loss_ref.shape)
        dg_ref[...] += dg

    row = pl.BlockSpec((tile, D), lambda i: (i, 0))
    return pl.pallas_call(
        body, name="loss_head",
        grid=(T // tile,),
        in_specs=[row, row, row, _full_spec(g4.shape)],
        out_specs=[_full_spec((SUBLANES, LANES)), row, row, _full_spec(g4.shape)],
        out_shape=[jax.ShapeDtypeStruct((SUBLANES, LANES), F32), jax.ShapeDtypeStruct((T, D), F32),
                   jax.ShapeDtypeStruct((T, D), BF16), jax.ShapeDtypeStruct(g4.shape, F32)],
        compiler_params=_params(("arbitrary",)),
    )(x1, f, target, g4)


def _nn(a, b):
    return _dot(a, b, ((1,), (0,)))


def _nt(a, b):
    return _dot(a, b, ((1,), (1,)))


def _tn(a, b):
    return _dot(a, b, ((0,), (0,)))


def _split_dot(x, u):
    hi = x.astype(BF16)
    lo = (x - hi.astype(F32)).astype(BF16)
    return _nn(hi, u) + _nn(lo, u)


def _head_masks():
    lane = lax.broadcasted_iota(jnp.int32, (1, LANES), 1)
    return [((lane >= h * HEAD_DIM) & (lane < (h + 1) * HEAD_DIM)).astype(F32) for h in range(LANES // HEAD_DIM)]


def _key_tri(op):
    row = lax.broadcasted_iota(jnp.int32, (ATTN_KEYS, ATTN_KEYS), 0)
    col = lax.broadcasted_iota(jnp.int32, (ATTN_KEYS, ATTN_KEYS), 1)
    return op(row, col).astype(BF16)


def _causal(qb, sub):
    row = lax.broadcasted_iota(jnp.int32, (qb, ATTN_KEYS), 0)
    col = lax.broadcasted_iota(jnp.int32, (qb, ATTN_KEYS), 1)
    return col + sub * ATTN_KEYS < row


def _sb_weights(qh, kb, c_fail, u_gt, strict, scale):
    z = _nt(qh, kb) * scale
    L = jnp.minimum(-z, 0.0) - jnp.log(1.0 + jnp.exp(-jnp.abs(z)))
    Lm = L if strict is None else jnp.where(strict, L, 0.0)
    A = jnp.exp(z + L + c_fail + _split_dot(Lm, u_gt))
    if strict is not None:
        A = jnp.where(strict, A, 0.0)
    return z, L, Lm, A


def _attn_specs(S, qb):
    nq = S // qb
    q_spec = pl.BlockSpec((qb, LANES), lambda b, p, i: (b * nq + i, p))
    k_spec = pl.BlockSpec((S, LANES), lambda b, p, i: (b, SB_WIDTH // LANES + p))
    v_spec = pl.BlockSpec((S, LANES), lambda b, p, i: (b, 2 * SB_WIDTH // LANES + p))
    seq = pl.BlockSpec((S, LANES), lambda b, p, i: (b, p))
    return q_spec, k_spec, v_spec, q_spec, seq


def _key_walk(i, qb, block, carry):
    per = qb // ATTN_KEYS
    for sub in reversed(range(per)):
        carry = block(i * per + sub, carry, _causal(qb, sub))
    return lax.fori_loop(0, i * per, lambda j, c: block(i * per - 1 - j, c, None), carry)


def _attn_fwd(proj, B, S):
    qb = min(ATTN_QUERIES, S)
    scale = HEAD_DIM ** -0.5

    def body(q_ref, k_ref, v_ref, o_ref):
        i = pl.program_id(2)
        masks = _head_masks()
        u_gt = _key_tri(lambda r, c: r > c)
        q = q_ref[...]
        qhs = [(q * m).astype(BF16) for m in masks]

        def block(J, carry, strict_mask):
            acc, cs = carry
            r0 = pl.multiple_of(J * ATTN_KEYS, ATTN_KEYS)
            kb = k_ref[pl.ds(r0, ATTN_KEYS), :].astype(BF16)
            vb = v_ref[pl.ds(r0, ATTN_KEYS), :]
            new_cs = []
            for h, m in enumerate(masks):
                _, _, Lm, A = _sb_weights(qhs[h], kb, cs[h], u_gt, strict_mask, scale)
                acc = acc + _nn(A.astype(BF16), (vb * m).astype(BF16))
                new_cs.append(cs[h] + jnp.sum(Lm, axis=1, keepdims=True))
            return acc, tuple(new_cs)

        zero_c = tuple(jnp.zeros((qb, 1), F32) for _ in masks)
        carry = _key_walk(i, qb, block, (jnp.zeros((qb, LANES), F32), zero_c))
        o_ref[...] = carry[0]

    q_spec, k_spec, v_spec, blk, _ = _attn_specs(S, qb)
    return pl.pallas_call(
        body, name="sb_attn_fwd",
        grid=(B, SB_WIDTH // LANES, S // qb),
        in_specs=[q_spec, k_spec, v_spec],
        out_specs=blk,
        out_shape=jax.ShapeDtypeStruct((B * S, SB_WIDTH), F32),
        compiler_params=_params(("parallel", "parallel", "arbitrary")),
    )(proj, proj, proj)


def _attn_bwd(proj, o, do, B, S):
    qb = min(ATTN_QUERIES, S)
    nq = S // qb
    scale = HEAD_DIM ** -0.5

    def body(q_ref, k_ref, v_ref, o_ref, do_ref, dq_ref, dk_out, dv_out, dk_ref, dv_ref):
        i = pl.program_id(2)

        @pl.when(i == 0)
        def _():
            dk_ref[...] = jnp.zeros_like(dk_ref)
            dv_ref[...] = jnp.zeros_like(dv_ref)

        masks = _head_masks()
        u_gt = _key_tri(lambda r, c: r > c)
        u_ge = _key_tri(lambda r, c: r >= c)
        q = q_ref[...]
        do_b = do_ref[...].astype(BF16)
        od = o_ref[...] * do_b.astype(F32)
        qhs = [(q * m).astype(BF16) for m in masks]
        dohs = [(do_b * m.astype(BF16)) for m in masks]
        totals = [jnp.sum(od * m, axis=1, keepdims=True) for m in masks]

        def block(J, carry, strict_mask):
            dq, c_fail, c_p = carry
            r0 = pl.multiple_of(J * ATTN_KEYS, ATTN_KEYS)
            kb32 = k_ref[pl.ds(r0, ATTN_KEYS), :]
            kb = kb32.astype(BF16)
            vb = v_ref[pl.ds(r0, ATTN_KEYS), :].astype(BF16)
            dk_blk = jnp.zeros((ATTN_KEYS, LANES), F32)
            dv_blk = jnp.zeros((ATTN_KEYS, LANES), F32)
            new_fail, new_p = [], []
            for h, m in enumerate(masks):
                z, L, Lm, A = _sb_weights(qhs[h], kb, c_fail[h], u_gt, strict_mask, scale)
                Ab = A.astype(BF16)
                P = Ab.astype(F32) * _nt(dohs[h], vb)
                after = c_p[h] + _split_dot(P, u_ge)
                sig = jnp.exp(z + L)
                dz = (P * (1.0 - sig) - sig * (totals[h] - after)) * scale
                if strict_mask is not None:
                    dz = jnp.where(strict_mask, dz, 0.0)
                dzb = dz.astype(BF16)
                dv_blk = dv_blk + _tn(Ab, dohs[h])
                dk_blk = dk_blk + _tn(dzb, qhs[h])
                dq = dq + _nn(dzb, (kb32 * m).astype(BF16))
                new_fail.append(c_fail[h] + jnp.sum(Lm, axis=1, keepdims=True))
                new_p.append(c_p[h] + jnp.sum(P, axis=1, keepdims=True))
            dk_ref[pl.ds(r0, ATTN_KEYS), :] += dk_blk
            dv_ref[pl.ds(r0, ATTN_KEYS), :] += dv_blk
            return dq, tuple(new_fail), tuple(new_p)

        zc = tuple(jnp.zeros((qb, 1), F32) for _ in masks)
        carry = _key_walk(i, qb, block, (jnp.zeros((qb, LANES), F32), zc, zc))
        dq_ref[...] = carry[0].astype(dq_ref.dtype)

        @pl.when(i == nq - 1)
        def _():
            dk_out[...] = dk_ref[...].astype(dk_out.dtype)
            dv_out[...] = dv_ref[...].astype(dv_out.dtype)

    q_spec, k_spec, v_spec, blk, seq = _attn_specs(S, qb)
    return pl.pallas_call(
        body, name="sb_attn_bwd",
        grid=(B, SB_WIDTH // LANES, nq),
        in_specs=[q_spec, k_spec, v_spec, blk, blk],
        out_specs=[blk, seq, seq],
        out_shape=[jax.ShapeDtypeStruct((B * S, SB_WIDTH), BF16)] * 3,
        scratch_shapes=[pltpu.VMEM((S, LANES), F32), pltpu.VMEM((S, LANES), F32)],
        compiler_params=_params(("parallel", "parallel", "arbitrary")),
    )(proj, proj, proj, o, do)


_BATCHED = {"nn": "gmk,gkn->gmn", "nt": "gmk,gnk->gmn", "tn": "gkm,gkn->gmn"}


def _bdot_raw(a, b, kind, passes):
    e = functools.partial(jnp.einsum, _BATCHED[kind], preferred_element_type=F32)
    ah, bh = a.astype(BF16), b.astype(BF16)
    if passes == 1:
        return e(ah, bh)
    al, bl = (a - ah.astype(F32)).astype(BF16), (b - bh.astype(F32)).astype(BF16)
    return e(ah, bh) + e(ah, bl) + e(al, bh)


@functools.partial(jax.custom_vjp, nondiff_argnums=(2, 3))
def _bdot(a, b, kind, passes):
    return _bdot_raw(a, b, kind, passes)


def _bdot_fwd(a, b, kind, passes):
    return _bdot_raw(a, b, kind, passes), (a, b)


def _bdot_bwd(kind, passes, res, g):
    a, b = res
    if kind == "nn":
        return _bdot_raw(g, b, "nt", passes), _bdot_raw(a, g, "tn", passes)
    if kind == "nt":
        return _bdot_raw(g, b, "nn", passes), _bdot_raw(g, a, "tn", passes)
    return _bdot_raw(b, g, "nt", passes), _bdot_raw(a, g, "nn", passes)


_bdot.defvjp(_bdot_fwd, _bdot_bwd)


def _wkv_chunk(S0, r, lw, k, v, kap, a):
    G, C, N = r.shape
    row = lax.broadcasted_iota(jnp.int32, (C, C), 0)
    col = lax.broadcasted_iota(jnp.int32, (C, C), 1)
    incl = (col <= row).astype(F32)
    strict = (col < row).astype(F32)
    cum = _bdot(jnp.broadcast_to(incl, (G, C, C)), lw, "nn", 3)
    e_pos = jnp.exp(cum)
    e_neg = jnp.exp(-cum)
    al = -kap * jnp.exp(cum - lw)
    be = kap * a * e_neg
    kt = k * e_neg
    rt = r * e_pos
    m_ab = _bdot(al, be, "nt", 3) * strict
    m_ak = _bdot(al, kt, "nt", 3) * strict
    m_rb = _bdot(rt, be, "nt", 3) * incl
    m_rk = _bdot(rt, kt, "nt", 3) * incl
    sa = _bdot(al, S0, "nt", 3) + _bdot(m_ak, v, "nn", 3)
    p = m_ab
    steps = max(1, (C - 1).bit_length())
    for j in range(steps):
        sa = sa + _bdot(p, sa, "nn", 1)
        if j + 1 < steps:
            p = _bdot(p, p, "nn", 1)
    y = _bdot(rt, S0, "nt", 3) + _bdot(m_rb, sa, "nn", 3) + _bdot(m_rk, v, "nn", 3)
    S1 = (S0 + _bdot(sa, be, "tn", 1) + _bdot(v, kt, "tn", 1)) * e_pos[:, C - 1:C, :]
    return y, S1


def _split_heads(x):
    return jnp.stack([x[:, h * HEAD_DIM:(h + 1) * HEAD_DIM] for h in range(x.shape[1] // HEAD_DIM)], axis=0)


def _merge_heads(x):
    return jnp.concatenate([x[h] for h in range(x.shape[0])], axis=1)


def _wkv_fwd(r, lw, k, v, kap, a, B, S):
    C, H, N = WKV_CHUNK, RW_WIDTH // HEAD_DIM, HEAD_DIM
    nc = S // C

    def body(r_ref, lw_ref, k_ref, v_ref, kap_ref, a_ref, y_ref, st_ref, s_scr):
        @pl.when(pl.program_id(1) == 0)
        def _():
            s_scr[...] = jnp.zeros_like(s_scr)

        S0 = s_scr[...]
        st_ref[0, 0] = S0
        args = [_split_heads(ref[...]) for ref in (r_ref, lw_ref, k_ref, v_ref, kap_ref, a_ref)]
        y, S1 = _wkv_chunk(S0, *args)
        s_scr[...] = S1
        y_ref[...] = _merge_heads(y)

    row_spec = pl.BlockSpec((C, RW_WIDTH), lambda b, c: (b * nc + c, 0))
    return pl.pallas_call(
        body, name="wkv_fwd",
        grid=(B, nc),
        in_specs=[row_spec] * 6,
        out_specs=[row_spec, pl.BlockSpec((1, 1, H, N, N), lambda b, c: (b, c, 0, 0, 0))],
        out_shape=[jax.ShapeDtypeStruct((B * S, RW_WIDTH), F32), jax.ShapeDtypeStruct((B, nc, H, N, N), F32)],
        scratch_shapes=[pltpu.VMEM((H, N, N), F32)],
        compiler_params=_params(("arbitrary", "arbitrary")),
    )(r, lw, k, v, kap, a)


def _wkv_bwd(r, lw, k, v, kap, a, states, dy, B, S):
    C, H, N = WKV_CHUNK, RW_WIDTH // HEAD_DIM, HEAD_DIM
    nc = S // C

    def body(r_ref, lw_ref, k_ref, v_ref, kap_ref, a_ref, st_ref, dy_ref,
             dr_ref, dlw_ref, dk_ref, dv_ref, dkap_ref, da_ref, ds_scr):
        @pl.when(pl.program_id(1) == 0)
        def _():
            ds_scr[...] = jnp.zeros_like(ds_scr)

        args = [_split_heads(ref[...]) for ref in (r_ref, lw_ref, k_ref, v_ref, kap_ref, a_ref)]
        _, vjp = jax.vjp(_wkv_chunk, st_ref[0, 0], *args)
        g = vjp((_split_heads(dy_ref[...]), ds_scr[...]))
        ds_scr[...] = g[0]
        for ref, gv in zip((dr_ref, dlw_ref, dk_ref, dv_ref, dkap_ref, da_ref), g[1:]):
            ref[...] = _merge_heads(gv)

    row_spec = pl.BlockSpec((C, RW_WIDTH), lambda b, c: (b * nc + (nc - 1 - c), 0))
    st_spec = pl.BlockSpec((1, 1, H, N, N), lambda b, c: (b, nc - 1 - c, 0, 0, 0))
    return pl.pallas_call(
        body, name="wkv_bwd",
        grid=(B, nc),
        in_specs=[row_spec] * 6 + [st_spec, row_spec],
        out_specs=[row_spec] * 6,
        out_shape=[jax.ShapeDtypeStruct((B * S, RW_WIDTH), F32)] * 6,
        scratch_shapes=[pltpu.VMEM((H, N, N), F32)],
        compiler_params=_params(("arbitrary", "arbitrary")),
    )(r, lw, k, v, kap, a, states, dy)


HBM = pl.BlockSpec(memory_space=pl.ANY)


def _place():
    return lax.axis_index("x"), lax.axis_index("y"), lax.axis_index("c")


def _other_chips(x, y):
    return [(1 - x, y), (x, 1 - y), (1 - x, 1 - y)]


def _all_gather_chips(shards):
    n = len(shards)

    def body(*refs):
        ins, outs = refs[:n], refs[n:2 * n]
        ici_send, ici_recv, d2d_send, d2d_recv, local = refs[2 * n:]
        x, y, c = _place()
        me = 2 * x + y
        sib = (x, y, 1 - c)
        chips = _other_chips(x, y)
        started, copies = [], []
        for w in range(n):
            cp = pltpu.make_async_copy(ins[w].at[c], outs[w].at[me, c], local.at[w])
            cp.start()
            copies.append(cp)
            for j, (px, py) in enumerate(chips):
                rd = pltpu.make_async_remote_copy(
                    src_ref=ins[w].at[c], dst_ref=outs[w].at[me, c], send_sem=ici_send.at[3 * w + j],
                    recv_sem=ici_recv.at[3 * w + j], device_id=(px, py, c), device_id_type=MESH)
                rd.start()
                started.append(rd)
            rd = pltpu.make_async_remote_copy(
                src_ref=ins[w].at[c], dst_ref=outs[w].at[me, c], send_sem=d2d_send.at[4 * w + 3],
                recv_sem=d2d_recv.at[4 * w + 3], device_id=sib, device_id_type=MESH)
            rd.start()
            started.append(rd)
        for w in range(n):
            for j, (px, py) in enumerate(chips):
                src = 2 * px + py
                pltpu.make_async_remote_copy(
                    src_ref=ins[w].at[c], dst_ref=outs[w].at[src, c], send_sem=ici_send.at[3 * w + j],
                    recv_sem=ici_recv.at[3 * w + j], device_id=(px, py, c), device_id_type=MESH).wait_recv()
                rd = pltpu.make_async_remote_copy(
                    src_ref=outs[w].at[src, c], dst_ref=outs[w].at[src, c], send_sem=d2d_send.at[4 * w + j],
                    recv_sem=d2d_recv.at[4 * w + j], device_id=sib, device_id_type=MESH)
                rd.start()
                started.append(rd)
        for w in range(n):
            for j, (px, py) in enumerate(chips):
                pltpu.make_async_remote_copy(
                    src_ref=ins[w].at[c], dst_ref=outs[w].at[2 * px + py, 1 - c], send_sem=d2d_send.at[4 * w + j],
                    recv_sem=d2d_recv.at[4 * w + j], device_id=sib, device_id_type=MESH).wait_recv()
            pltpu.make_async_remote_copy(
                src_ref=ins[w].at[c], dst_ref=outs[w].at[me, 1 - c], send_sem=d2d_send.at[4 * w + 3],
                recv_sem=d2d_recv.at[4 * w + 3], device_id=sib, device_id_type=MESH).wait_recv()
        for rd in started:
            rd.wait_send()
        for cp in copies:
            cp.wait()

    return pl.pallas_call(
        body, name="gather_weights",
        in_specs=[HBM] * n, out_specs=[HBM] * n,
        out_shape=[jax.ShapeDtypeStruct((N_CHIPS,) + s.shape, s.dtype) for s in shards],
        scratch_shapes=[pltpu.SemaphoreType.DMA((3 * n,)), pltpu.SemaphoreType.DMA((3 * n,)),
                        pltpu.SemaphoreType.DMA((4 * n,)), pltpu.SemaphoreType.DMA((4 * n,)),
                        pltpu.SemaphoreType.DMA((n,))],
        compiler_params=pltpu.CompilerParams(has_side_effects=True),
    )(*shards)


def _pair_split(grads):
    n = len(grads)

    def body(*refs):
        ins, theirs = refs[:n], refs[n:2 * n]
        send, recv = refs[2 * n:]
        x, y, c = _place()
        sib = (x, y, 1 - c)
        rds = []
        for w in range(n):
            rd = pltpu.make_async_remote_copy(
                src_ref=ins[w].at[:, 1 - c], dst_ref=theirs[w], send_sem=send.at[w], recv_sem=recv.at[w],
                device_id=sib, device_id_type=MESH)
            rd.start()
            rds.append(rd)
        for rd in rds:
            rd.wait_recv()
        for rd in rds:
            rd.wait_send()

    return pl.pallas_call(
        body, name="grad_pair_split",
        in_specs=[HBM] * n, out_specs=[HBM] * n,
        out_shape=[jax.ShapeDtypeStruct((g.shape[0],) + g.shape[2:], g.dtype) for g in grads],
        scratch_shapes=[pltpu.SemaphoreType.DMA((n,)), pltpu.SemaphoreType.DMA((n,))],
        compiler_params=pltpu.CompilerParams(has_side_effects=True),
    )(*grads)


def _chip_scatter(parts):
    n = len(parts)

    def body(*refs):
        ins, outs = refs[:n], refs[n:2 * n]
        send, recv = refs[2 * n:]
        x, y, c = _place()
        me = 2 * x + y
        rds = []
        for w in range(n):
            for j, (px, py) in enumerate(_other_chips(x, y)):
                s = 3 * w + j
                rd = pltpu.make_async_remote_copy(
                    src_ref=ins[w].at[2 * px + py], dst_ref=outs[w].at[j], send_sem=send.at[s], recv_sem=recv.at[s],
                    device_id=(px, py, c), device_id_type=MESH)
                rd.start()
                rds.append(rd)
        for w in range(n):
            for j, (px, py) in enumerate(_other_chips(x, y)):
                s = 3 * w + j
                pltpu.make_async_remote_copy(
                    src_ref=ins[w].at[me], dst_ref=outs[w].at[j], send_sem=send.at[s], recv_sem=recv.at[s],
                    device_id=(px, py, c), device_id_type=MESH).wait_recv()
        for rd in rds:
            rd.wait_send()

    return pl.pallas_call(
        body, name="grad_chip_scatter",
        in_specs=[HBM] * n, out_specs=[HBM] * n,
        out_shape=[jax.ShapeDtypeStruct((N_CHIPS - 1,) + p.shape[1:], p.dtype) for p in parts],
        scratch_shapes=[pltpu.SemaphoreType.DMA((3 * n,)), pltpu.SemaphoreType.DMA((3 * n,))],
        compiler_params=pltpu.CompilerParams(has_side_effects=True),
    )(*parts)


def _pair_join(bufs):
    n = len(bufs)

    def body(*refs):
        ins, outs = refs[:n], refs[n:2 * n]
        send, recv = refs[2 * n:]
        x, y, c = _place()
        sib = (x, y, 1 - c)
        rds = []
        for w in range(n):
            rd = pltpu.make_async_remote_copy(
                src_ref=ins[w].at[c], dst_ref=outs[w].at[c], send_sem=send.at[w], recv_sem=recv.at[w],
                device_id=sib, device_id_type=MESH)
            rd.start()
            rds.append(rd)
        for w in range(n):
            pltpu.make_async_remote_copy(
                src_ref=ins[w].at[c], dst_ref=outs[w].at[1 - c], send_sem=send.at[w], recv_sem=recv.at[w],
                device_id=sib, device_id_type=MESH).wait_recv()
        for rd in rds:
            rd.wait_send()

    return pl.pallas_call(
        body, name="grad_pair_join",
        in_specs=[HBM] * n, out_specs=[HBM] * n,
        out_shape=[jax.ShapeDtypeStruct(b.shape, b.dtype) for b in bufs],
        input_output_aliases={w: w for w in range(n)},
        scratch_shapes=[pltpu.SemaphoreType.DMA((n,)), pltpu.SemaphoreType.DMA((n,))],
        compiler_params=pltpu.CompilerParams(has_side_effects=True),
    )(*bufs)


def _all_reduce_small(packed):
    R = packed.shape[0]

    def body(x_ref, o_ref, buf, send, recv):
        x, y, c = _place()
        me = 4 * x + 2 * y + c
        buf[me] = x_ref[...]
        rds = []
        for rel in range(1, N_DEV):
            fx, fy, fc = (rel >> 2) & 1, (rel >> 1) & 1, rel & 1
            peer = (1 - x if fx else x, 1 - y if fy else y, 1 - c if fc else c)
            rd = pltpu.make_async_remote_copy(
                src_ref=x_ref, dst_ref=buf.at[me], send_sem=send.at[rel - 1], recv_sem=recv.at[rel - 1],
                device_id=peer, device_id_type=MESH)
            rd.start()
            rds.append((rd, peer))
        for rel in range(1, N_DEV):
            rd, (px, py, pc) = rds[rel - 1]
            pltpu.make_async_remote_copy(
                src_ref=x_ref, dst_ref=buf.at[4 * px + 2 * py + pc], send_sem=send.at[rel - 1], recv_sem=recv.at[rel - 1],
                device_id=(px, py, pc), device_id_type=MESH).wait_recv()
        for rd, _ in rds:
            rd.wait_send()
        total = buf[0]
        for d in range(1, N_DEV):
            total = total + buf[d]
        o_ref[...] = total

    return pl.pallas_call(
        body, name="all_reduce_small",
        in_specs=[pl.BlockSpec(memory_space=pltpu.VMEM)],
        out_specs=pl.BlockSpec(memory_space=pltpu.VMEM),
        out_shape=jax.ShapeDtypeStruct(packed.shape, F32),
        scratch_shapes=[pltpu.VMEM((N_DEV, R, LANES), F32), pltpu.SemaphoreType.DMA((N_DEV - 1,)),
                        pltpu.SemaphoreType.DMA((N_DEV - 1,))],
        compiler_params=pltpu.CompilerParams(has_side_effects=True),
    )(packed)


def _pair_sum(name, split, theirs, core):
    n_chip, _, Rh, C = split.shape
    tile = _div_tile(Rh, 256, 2 * SUBLANES)
    nt = Rh // tile

    def body(core_ref, a_ref, b_ref, o_ref):
        o_ref[...] = (a_ref[...] + b_ref[...]).astype(o_ref.dtype)

    return pl.pallas_call(
        body, name=name,
        grid_spec=pltpu.PrefetchScalarGridSpec(
            num_scalar_prefetch=1,
            grid=(n_chip, nt),
            in_specs=[pl.BlockSpec((None, None, tile, C), lambda j, i, core_ref: (j, core_ref[0], i, 0)),
                      pl.BlockSpec((None, tile, C), lambda j, i, core_ref: (j, i, 0))],
            out_specs=pl.BlockSpec((None, tile, C), lambda j, i, core_ref: (j, i, 0)),
        ),
        out_shape=jax.ShapeDtypeStruct((n_chip, Rh, C), BF16),
        compiler_params=_params(("parallel", "parallel")),
    )(core, split, theirs)


def _chip_sum(name, own, landed, core):
    n_in, Rh, C = landed.shape
    tile = _div_tile(Rh, 256, 2 * SUBLANES)

    def body(core_ref, *refs):
        total = refs[0][...].astype(F32)
        for ref in refs[1:n_in + 1]:
            total = total + ref[...].astype(F32)
        refs[n_in + 1][...] = total

    slot = lambda j: pl.BlockSpec((None, tile, C), lambda i, core_ref: (j, i, 0))
    return pl.pallas_call(
        body, name=name,
        grid_spec=pltpu.PrefetchScalarGridSpec(
            num_scalar_prefetch=1,
            grid=(Rh // tile,),
            in_specs=[pl.BlockSpec((None, tile, C), lambda i, core_ref: (core_ref[1], i, 0))]
                     + [slot(j) for j in range(n_in)],
            out_specs=pl.BlockSpec((None, tile, C), lambda i, core_ref: (core_ref[0], i, 0)),
        ),
        out_shape=jax.ShapeDtypeStruct((2, Rh, C), F32),
        compiler_params=_params(("parallel",)),
    )(core, own, *([landed] * n_in))


def _adamw(name, w, g, m, v):
    R, C = w.shape
    tile = _div_tile(R, 256, SUBLANES)
    c1 = 1.0 / (1.0 - ADAM_B1 ** ADAM_STEP)
    c2 = 1.0 / (1.0 - ADAM_B2 ** ADAM_STEP)

    def body(w_ref, g_ref, m_ref, v_ref, d_ref, nm_ref, nv_ref):
        g_ = g_ref[...]
        nm = ADAM_B1 * m_ref[...] + (1.0 - ADAM_B1) * g_
        nv = ADAM_B2 * v_ref[...] + (1.0 - ADAM_B2) * (g_ * g_)
        d_ref[...] = -ADAM_LR * ((nm * c1) / (jnp.sqrt(nv * c2) + ADAM_EPS) + ADAM_WD * w_ref[...])
        nm_ref[...] = nm
        nv_ref[...] = nv

    spec = pl.BlockSpec((tile, C), lambda i: (i, 0))
    return pl.pallas_call(
        body, name=name,
        grid=(R // tile,),
        in_specs=[spec] * 4, out_specs=[spec] * 3,
        out_shape=[jax.ShapeDtypeStruct((R, C), F32)] * 3,
        compiler_params=_params(("parallel",)),
    )(w, g, m, v)


def _cols_to_shards(full):
    K, N = full.shape
    return full.reshape(K, N_CHIPS, N // N_CHIPS).transpose(1, 0, 2)


def _shards_to_cols(sh):
    return sh.transpose(1, 0, 2).reshape(sh.shape[1], -1)


def _rows_to_shards(full):
    return full.reshape(N_CHIPS, full.shape[0] // N_CHIPS, full.shape[1])


SMALL = ["norm_mix_pre", "b_gate", "mu_rw", "w0", "a0", "k_k", "k_a", "r_k", "lnx_w", "lnx_b",
         "norm_mix_post", "norm_ffn_pre", "norm_ffn_post"]
BIG = ["w_in", "w_up", "a_up", "g_up", "w_sb_out", "w_rw_out", "w_o", "w_ffn_gate", "w_ffn_up", "w_ffn_down"]
ROW_SHARDED = ("w_o", "w_ffn_down")
ORDER = ["norm_mix_pre", "w_in", "b_gate", "mu_rw", "w0", "w_up", "a0", "a_up", "g_up", "k_k", "k_a", "r_k",
         "lnx_w", "lnx_b", "w_sb_out", "w_rw_out", "w_o", "norm_mix_post", "norm_ffn_pre", "w_ffn_gate",
         "w_ffn_up", "w_ffn_down", "norm_ffn_post"]


def _pack_small(vals, extra_rows=0):
    rows = jnp.concatenate([vals[n].reshape(-1, LANES) for n in SMALL], axis=0)
    pad = (-(rows.shape[0] + extra_rows)) % SUBLANES + extra_rows
    return jnp.pad(rows, ((0, pad), (0, 0)))


def _unpack_small(packed, shapes):
    out, r = {}, 0
    for n in SMALL:
        size = 1
        for s in shapes[n]:
            size *= s
        out[n] = packed[r:r + size // LANES].reshape(shapes[n])
        r += size // LANES
    return out


def kernel(x, norm_mix_pre, w_in, b_gate, mu_rw, w0, w_up, a0, a_up, g_up, k_k, k_a, r_k, lnx_w, lnx_b, w_sb_out, w_rw_out, w_o, norm_mix_post, norm_ffn_pre, w_ffn_gate, w_ffn_up, w_ffn_down, norm_ffn_post, loss_target, m_norm_mix_pre, m_w_in, m_b_gate, m_mu_rw, m_w0, m_w_up, m_a0, m_a_up, m_g_up, m_k_k, m_k_a, m_r_k, m_lnx_w, m_lnx_b, m_w_sb_out, m_w_rw_out, m_w_o, m_norm_mix_post, m_norm_ffn_pre, m_w_ffn_gate, m_w_ffn_up, m_w_ffn_down, m_norm_ffn_post, v_norm_mix_pre, v_w_in, v_b_gate, v_mu_rw, v_w0, v_w_up, v_a0, v_a_up, v_g_up, v_k_k, v_k_a, v_r_k, v_lnx_w, v_lnx_b, v_w_sb_out, v_w_rw_out, v_w_o, v_norm_mix_post, v_norm_ffn_pre, v_w_ffn_gate, v_w_ffn_up, v_w_ffn_down, v_norm_ffn_post):
    W = dict(norm_mix_pre=norm_mix_pre, w_in=w_in, b_gate=b_gate, mu_rw=mu_rw, w0=w0, w_up=w_up, a0=a0, a_up=a_up,
             g_up=g_up, k_k=k_k, k_a=k_a, r_k=r_k, lnx_w=lnx_w, lnx_b=lnx_b, w_sb_out=w_sb_out, w_rw_out=w_rw_out,
             w_o=w_o, norm_mix_post=norm_mix_post, norm_ffn_pre=norm_ffn_pre, w_ffn_gate=w_ffn_gate,
             w_ffn_up=w_ffn_up, w_ffn_down=w_ffn_down, norm_ffn_post=norm_ffn_post)
    Mo = dict(norm_mix_pre=m_norm_mix_pre, w_in=m_w_in, b_gate=m_b_gate, mu_rw=m_mu_rw, w0=m_w0, w_up=m_w_up, a0=m_a0,
              a_up=m_a_up, g_up=m_g_up, k_k=m_k_k, k_a=m_k_a, r_k=m_r_k, lnx_w=m_lnx_w, lnx_b=m_lnx_b,
              w_sb_out=m_w_sb_out, w_rw_out=m_w_rw_out, w_o=m_w_o, norm_mix_post=m_norm_mix_post,
              norm_ffn_pre=m_norm_ffn_pre, w_ffn_gate=m_w_ffn_gate, w_ffn_up=m_w_ffn_up, w_ffn_down=m_w_ffn_down,
              norm_ffn_post=m_norm_ffn_post)
    Vo = dict(norm_mix_pre=v_norm_mix_pre, w_in=v_w_in, b_gate=v_b_gate, mu_rw=v_mu_rw, w0=v_w0, w_up=v_w_up, a0=v_a0,
              a_up=v_a_up, g_up=v_g_up, k_k=v_k_k, k_a=v_k_a, r_k=v_r_k, lnx_w=v_lnx_w, lnx_b=v_lnx_b,
              w_sb_out=v_w_sb_out, w_rw_out=v_w_rw_out, w_o=v_w_o, norm_mix_post=v_norm_mix_post,
              norm_ffn_pre=v_norm_ffn_pre, w_ffn_gate=v_w_ffn_gate, w_ffn_up=v_w_ffn_up, w_ffn_down=v_w_ffn_down,
              norm_ffn_post=v_norm_ffn_post)
    shapes = {n: W[n].shape for n in ORDER}
    B, S, D = x.shape
    T = B * S
    x2 = x.reshape(T, D)
    tgt = loss_target.reshape(T, D)
    vec = {n: W[n].reshape(1, -1) for n in SMALL}

    halved = [W[n][0].astype(BF16).reshape(2, W[n].shape[1] // 2, W[n].shape[2]) for n in BIG]
    full = {}
    for n, gth in zip(BIG, _all_gather_chips(halved)):
        gth = gth.reshape((N_CHIPS,) + W[n].shape[1:])
        full[n] = gth.reshape(-1, gth.shape[2]) if n in ROW_SHARDED else _shards_to_cols(gth)
    w_sb, w_rw, w_gt = full["w_in"][:, :SB_COLS], full["w_in"][:, SB_COLS:SB_COLS + RW_COLS], full["w_in"][:, SB_COLS + RW_COLS:]
    lora_rows = {"w_up": 0, "a_up": 64, "g_up": 128}
    lora = {n: jnp.pad(full[n], ((r0, LORA_COLS - r0 - full[n].shape[0]), (0, 0))) for n, r0 in lora_rows.items()}
    mu = vec["mu_rw"]
    mu_parts = [mu[:, :512], mu[:, 512:1024], mu[:, 1024:1536], mu[:, 1536:]]
    b1, b2 = vec["b_gate"][:, :D], vec["b_gate"][:, D:]

    (h1,) = _rowwise("norm_mix_pre", _f_norm, [(x2, D, 0)], [vec["norm_mix_pre"]], [(D, BF16)])
    p_sb = _mm("proj_sb", h1, w_sb)
    p_rw = _mm("proj_rw", h1, w_rw)
    p_gt = _mm("proj_gate", h1, w_gt)
    o_sb = _attn_fwd(p_sb, B, S)
    pre_params = mu_parts + [vec["w0"], lora["w_up"], vec["a0"], lora["a_up"], lora["g_up"], vec["k_k"], vec["k_a"]]
    r_, lw_, k2_, v_, kap_, a_, g_ = _rw_pre(p_rw, pre_params, S)
    y_wkv, states = _wkv_fwd(r_, lw_, k2_, v_, kap_, a_, B, S)
    post_rows = [(y_wkv, 512, 0), (r_, 512, 0), (k2_, 512, 0), (v_, 512, 0), (g_, 512, 0)]
    post_params = [vec["lnx_w"], vec["lnx_b"], vec["r_k"]]
    (o_rw,) = _rowwise("rw_post", _f_rwpost, post_rows, post_params, [(512, BF16)])
    m1 = _mm("mix_sb_out", o_sb, full["w_sb_out"])
    m2 = _mm("mix_rw_out", o_rw, full["w_rw_out"])
    merge_rows = [(p_gt, D, 0), (p_gt, D, 1), (m1, D, 0), (m2, D, 0)]
    (merged,) = _rowwise("merge", _f_merge, merge_rows, [b1, b2], [(D, BF16)])
    u = _mm("mix_out", merged, full["w_o"])
    post1_params = [vec["norm_mix_post"], vec["norm_ffn_pre"]]
    x1, h2 = _rowwise("post_mix", _f_post1, [(x2, D, 0), (u, D, 0)], post1_params, [(D, F32), (D, BF16)])
    ag = _mm("ffn_gate", h2, full["w_ffn_gate"], out_dtype=BF16)
    au = _mm("ffn_up", h2, full["w_ffn_up"], out_dtype=BF16)
    (sw,) = _rowwise("swiglu", _f_swiglu, [(ag, D_FF, 0), (au, D_FF, 0)], [], [(D_FF, BF16)])
    f = _mm("ffn_down", sw, full["w_ffn_down"])
    loss_part, dx1, df, dg4 = _loss_head(x1, f, tgt, vec["norm_ffn_post"])

    gbig, gsmall = {}, {"norm_ffn_post": dg4}
    dsw = _mm("d_swiglu_out", df, full["w_ffn_down"], tb=True, out_dtype=BF16)
    gbig["w_ffn_down"] = _mm("g_ffn_down", sw, df, ta=True)
    (dag, dau), _ = _rowwise_vjp("swiglu_bwd", _f_swiglu, [(ag, D_FF, 0), (au, D_FF, 0)], [], [[dsw]], [True, True], [],
                                 bf16_rows=(0, 1))
    dh2 = _mm("d_h2_gate", dag, full["w_ffn_gate"], tb=True)
    dh2 = _mm("d_h2_up", dau, full["w_ffn_up"], tb=True, acc=dh2)
    gbig["w_ffn_gate"] = _mm("g_ffn_gate", h2, dag, ta=True)
    gbig["w_ffn_up"] = _mm("g_ffn_up", h2, dau, ta=True)
    (dx_res, du), (dg2, dg3) = _rowwise_vjp("post_mix_bwd", _f_post1, [(x2, D, 0), (u, D, 0)], post1_params,
                                            [[dx1], [dh2]], [True, True], [True, True], bf16_rows=(1,))
    gsmall["norm_mix_post"], gsmall["norm_ffn_pre"] = dg2, dg3
    dmerged = _mm("d_merged", du, full["w_o"], tb=True)
    gbig["w_o"] = _mm("g_w_o", merged, du, ta=True)
    (dpg1, dpg2, dm1, dm2), (db1, db2) = _rowwise_vjp("merge_bwd", _f_merge, merge_rows, [b1, b2], [[dmerged]],
                                                      [True] * 4, [True, True], bf16_rows=(0, 1, 2, 3))
    gsmall["b_gate"] = jnp.concatenate([db1, db2], axis=1)
    do_sb = _mm("d_o_sb", dm1, full["w_sb_out"], tb=True)
    do_rw = _mm("d_o_rw", dm2, full["w_rw_out"], tb=True)
    gbig["w_sb_out"] = _mm("g_sb_out", o_sb, dm1, ta=True)
    gbig["w_rw_out"] = _mm("g_rw_out", o_rw, dm2, ta=True)
    (dy_wkv, dr_a, dk2_a, dv_a, dg_), (dlnx_w, dlnx_b, dr_k) = _rowwise_vjp(
        "rw_post_bwd", _f_rwpost, post_rows, post_params, [[do_rw]], [True] * 5, [True] * 3)
    gsmall["lnx_w"], gsmall["lnx_b"], gsmall["r_k"] = dlnx_w, dlnx_b, dr_k
    dr_b, dlw, dk2_b, dv_b, dkap, da = _wkv_bwd(r_, lw_, k2_, v_, kap_, a_, states, dy_wkv, B, S)
    pre_cts = [[dr_a, dr_b], [dlw], [dk2_a, dk2_b], [dv_a, dv_b], [dkap], [da], [dg_]]
    dp_rw, dpre_params = _rw_pre_bwd(p_rw, pre_params, pre_cts, S)
    gsmall["mu_rw"] = jnp.concatenate(dpre_params[:4], axis=1)
    gsmall["w0"], gsmall["a0"], gsmall["k_k"], gsmall["k_a"] = dpre_params[4], dpre_params[6], dpre_params[9], dpre_params[10]
    glora = {"w_up": dpre_params[5][0:64], "a_up": dpre_params[7][64:128], "g_up": dpre_params[8][128:256]}
    dq, dk, dv = _attn_bwd(p_sb, o_sb, do_sb, B, S)
    dh1 = _mm("d_h1_q", dq, w_sb[:, :512], tb=True)
    dh1 = _mm("d_h1_k", dk, w_sb[:, 512:1024], tb=True, acc=dh1)
    dh1 = _mm("d_h1_v", dv, w_sb[:, 1024:], tb=True, acc=dh1)
    dh1 = _mm("d_h1_rw", dp_rw, w_rw, tb=True, acc=dh1)
    dh1 = _mm("d_h1_g1", dpg1, w_gt[:, :D], tb=True, acc=dh1)
    dh1 = _mm("d_h1_g2", dpg2, w_gt[:, D:], tb=True, acc=dh1)
    gbig["w_in"] = jnp.concatenate(
        [_mm("g_in_" + tag, h1, d, ta=True)
         for tag, d in (("q", dq), ("k", dk), ("v", dv), ("rw", dp_rw), ("g1", dpg1), ("g2", dpg2))], axis=1)
    (grad_x2,), (dg1,) = _rowwise_vjp("norm_mix_pre_bwd", _f_norm, [(x2, D, 0)], [vec["norm_mix_pre"]], [[dh1]],
                                      [True], [True], add_to={0: dx_res})
    gsmall["norm_mix_pre"] = dg1
    gbig.update(glora)

    split = []
    for n in BIG:
        g = _rows_to_shards(gbig[n]) if n in ROW_SHARDED else _cols_to_shards(gbig[n])
        split.append(g.reshape(N_CHIPS, 2, g.shape[1] // 2, g.shape[2]))
    core = jnp.stack([lax.axis_index("c"), 2 * lax.axis_index("x") + lax.axis_index("y")]).astype(jnp.int32)
    theirs = _pair_split(split)
    chip_sums = [_pair_sum("pair_sum_" + n, a, b, core) for n, a, b in zip(BIG, split, theirs)]
    landed = _chip_scatter(chip_sums)
    joined = _pair_join([_chip_sum("chip_sum_" + n, own, got, core) for n, own, got in zip(BIG, chip_sums, landed)])
    grads = {n: j.reshape(W[n].shape[1:]) for n, j in zip(BIG, joined)}

    small_local = _pack_small({n: gsmall[n] for n in SMALL}, extra_rows=1)
    loss_row = small_local.shape[0] - 1
    small_local = small_local.at[loss_row].set(loss_part[0])
    small_sum = _all_reduce_small(small_local)
    loss = small_sum[loss_row, 0]

    delta, new_m, new_v = {}, {}, {}
    for n in BIG:
        d_, m_, v2_ = _adamw("adamw_" + n, W[n][0], grads[n], Mo[n][0], Vo[n][0])
        delta[n], new_m[n], new_v[n] = d_[None], m_[None], v2_[None]
        grads[n] = grads[n][None]
    pk = lambda src: _pack_small({n: src[n] for n in SMALL}, extra_rows=1)
    d_s, m_s, v_s = _adamw("adamw_small", pk(W), small_sum.at[loss_row].set(0.0), pk(Mo), pk(Vo))
    for dst, packed in ((grads, small_sum), (delta, d_s), (new_m, m_s), (new_v, v_s)):
        dst.update(_unpack_small(packed, shapes))

    return (loss, grad_x2.reshape(B, S, D), *[grads[n] for n in ORDER], *[delta[n] for n in ORDER],
            *[new_m[n] for n in ORDER], *[new_v[n] for n in ORDER])
```

```python
import functools

import jax
import jax.numpy as jnp
from jax import lax
from jax.experimental import pallas as pl
from jax.experimental.pallas import tpu as pltpu

F32 = jnp.float32
BF16 = jnp.bfloat16
MESH = pl.DeviceIdType.MESH

D_MODEL = 1024
SB_HEADS = 8
HEAD_DIM = 64
SB_WIDTH = SB_HEADS * HEAD_DIM
RW_WIDTH = 512
LORA_COLS = 256
SB_COLS = 3 * SB_WIDTH
RW_COLS = 3 * RW_WIDTH + LORA_COLS
GATE_COLS = 2 * D_MODEL
D_FF = 2816
RMS_EPS = 1e-6
GN_EPS = HEAD_DIM * 1e-5
WKV_CHUNK = 64
ATTN_QUERIES = 512
ATTN_KEYS = 128
LANES = 128
SUBLANES = 8
N_CHIPS = 4
N_DEV = 8

ADAM_LR = 0.001
ADAM_B1 = 0.9
ADAM_B2 = 0.999
ADAM_EPS = 1e-08
ADAM_WD = 0.01
ADAM_STEP = 10

VMEM_LIMIT = 48 * 1024 * 1024


def _params(sem=None, **kw):
    if sem is not None:
        kw["dimension_semantics"] = sem
    return pltpu.CompilerParams(vmem_limit_bytes=VMEM_LIMIT, **kw)


def _div_tile(dim, pref, mult=LANES):
    if dim <= pref:
        return dim
    t = pref - pref % mult
    while t >= mult:
        if dim % t == 0:
            return t
        t -= mult
    return dim


def _dot(a, b, dims):
    return lax.dot_general(a, b, (dims, ((), ())), preferred_element_type=F32)


def _mm(name, a, b, *, ta=False, tb=False, acc=None, out_dtype=F32):
    if ta:
        K, M = a.shape
    else:
        M, K = a.shape
    N = b.shape[0] if tb else b.shape[1]
    if ta:
        tm, tn, tk = _div_tile(M, 1408), _div_tile(N, 1408), _div_tile(K, 512)
    else:
        tm, tn, tk = _div_tile(M, 512), _div_tile(N, 1408), _div_tile(K, 1408)
    nk = K // tk
    dims = ((0,) if ta else (1,), (1,) if tb else (0,))
    has_acc = acc is not None

    def body(*refs):
        a_ref, b_ref = refs[0], refs[1]
        o_ref, scr = refs[-2], refs[-1]
        k = pl.program_id(2)
        part = _dot(a_ref[...].astype(BF16), b_ref[...].astype(BF16), dims)

        @pl.when(k == 0)
        def _():
            scr[...] = part + refs[2][...] if has_acc else part

        @pl.when(k > 0)
        def _():
            scr[...] += part

        @pl.when(k == nk - 1)
        def _():
            o_ref[...] = scr[...].astype(o_ref.dtype)

    a_spec = pl.BlockSpec((tk, tm), lambda i, j, k: (k, i)) if ta else pl.BlockSpec((tm, tk), lambda i, j, k: (i, k))
    b_spec = pl.BlockSpec((tn, tk), lambda i, j, k: (j, k)) if tb else pl.BlockSpec((tk, tn), lambda i, j, k: (k, j))
    o_spec = pl.BlockSpec((tm, tn), lambda i, j, k: (i, j))
    return pl.pallas_call(
        body, name=name,
        grid=(M // tm, N // tn, nk),
        in_specs=[a_spec, b_spec] + ([o_spec] if has_acc else []),
        out_specs=o_spec,
        out_shape=jax.ShapeDtypeStruct((M, N), out_dtype),
        scratch_shapes=[pltpu.VMEM((tm, tn), F32)],
        compiler_params=_params(("parallel", "parallel", "arbitrary")),
    )(*([a, b] + ([acc] if has_acc else [])))


def _row_spec(tile, width, colblk):
    return pl.BlockSpec((tile, width), lambda i: (i, colblk))


def _full_spec(shape):
    return pl.BlockSpec(shape, lambda i: (0,) * len(shape))


def _rowwise(name, fn, rows, params, outs, tile=256):
    T = rows[0][0].shape[0]
    tile = min(tile, T)
    n_r, n_p = len(rows), len(params)

    def body(*refs):
        r = [x[...].astype(F32) for x in refs[:n_r]]
        p = [x[...].astype(F32) for x in refs[n_r:n_r + n_p]]
        for o_ref, val in zip(refs[n_r + n_p:], fn(*r, *p)):
            o_ref[...] = val.astype(o_ref.dtype)

    return pl.pallas_call(
        body, name=name,
        grid=(T // tile,),
        in_specs=[_row_spec(tile, w, cb) for _, w, cb in rows] + [_full_spec(p.shape) for p in params],
        out_specs=[_row_spec(tile, w, 0) for w, _ in outs],
        out_shape=[jax.ShapeDtypeStruct((T, w), dt) for w, dt in outs],
        compiler_params=_params(("parallel",)),
    )(*([a for a, _, _ in rows] + list(params)))


def _rowwise_vjp(name, fn, rows, params, cts, need_rows, need_params, add_to=None, tile=256, bf16_rows=()):
    add_to = add_to or {}
    T = rows[0][0].shape[0]
    tile = min(tile, T)
    n_r, n_p = len(rows), len(params)
    ct_flat = [c for group in cts for c in group]
    ct_sizes = [len(group) for group in cts]
    add_idx = sorted(add_to)
    row_out = [i for i in range(n_r) if need_rows[i]]
    par_out = [i for i in range(n_p) if need_params[i]]
    n_ct, n_add = len(ct_flat), len(add_idx)

    def body(*refs):
        pos = 0
        r = [x[...].astype(F32) for x in refs[pos:pos + n_r]]
        pos += n_r
        p = [x[...].astype(F32) for x in refs[pos:pos + n_p]]
        pos += n_p
        ct_vals = [x[...].astype(F32) for x in refs[pos:pos + n_ct]]
        pos += n_ct
        adds = {i: x[...] for i, x in zip(add_idx, refs[pos:pos + n_add])}
        pos += n_add
        drow_refs = refs[pos:pos + len(row_out)]
        pos += len(row_out)
        dpar_refs = refs[pos:pos + len(par_out)]
        ct_in, q = [], 0
        for n in ct_sizes:
            ct_in.append(functools.reduce(lambda u, v: u + v, ct_vals[q:q + n]))
            q += n
        _, vjp = jax.vjp(fn, *r, *p)
        grads = vjp(tuple(ct_in))
        for ref, i in zip(drow_refs, row_out):
            g = grads[i]
            ref[...] = (g + adds[i] if i in adds else g).astype(ref.dtype)

        @pl.when(pl.program_id(0) == 0)
        def _():
            for ref in dpar_refs:
                ref[...] = jnp.zeros_like(ref)

        for ref, i in zip(dpar_refs, par_out):
            ref[...] += grads[n_r + i]

    ct_widths = [c.shape[1] for c in ct_flat]
    in_specs = ([_row_spec(tile, w, cb) for _, w, cb in rows] + [_full_spec(p.shape) for p in params]
                + [_row_spec(tile, w, 0) for w in ct_widths] + [_row_spec(tile, rows[i][1], 0) for i in add_idx])
    out_specs = [_row_spec(tile, rows[i][1], 0) for i in row_out] + [_full_spec(params[i].shape) for i in par_out]
    out_shape = ([jax.ShapeDtypeStruct((T, rows[i][1]), BF16 if i in bf16_rows else F32) for i in row_out]
                 + [jax.ShapeDtypeStruct(params[i].shape, F32) for i in par_out])
    res = pl.pallas_call(
        body, name=name,
        grid=(T // tile,),
        in_specs=in_specs, out_specs=out_specs, out_shape=out_shape,
        compiler_params=_params(("arbitrary",)),
    )(*([a for a, _, _ in rows] + list(params) + ct_flat + [add_to[i] for i in add_idx]))
    return res[:len(row_out)], res[len(row_out):]


def _sigmoid(x):
    return 0.5 * (jnp.tanh(0.5 * x) + 1.0)


def _softplus(x):
    return jnp.maximum(x, 0.0) + jnp.log(1.0 + jnp.exp(-jnp.abs(x)))


def _rms(x, g):
    return x * lax.rsqrt(jnp.mean(x * x, axis=-1, keepdims=True) + RMS_EPS) * g


def _segsum_impl(x):
    n = x.shape[-1]
    r = lax.shift_right_logical(lax.broadcasted_iota(jnp.int32, (n, n), 0), 6)
    c = lax.shift_right_logical(lax.broadcasted_iota(jnp.int32, (n, n), 1), 6)
    bd = (r == c).astype(BF16)
    hi = x.astype(BF16)
    rest = x - hi.astype(F32)
    mid = rest.astype(BF16)
    lo = (rest - mid.astype(F32)).astype(BF16)
    nn = ((1,), (0,))
    return _dot(hi, bd, nn) + _dot(mid, bd, nn) + _dot(lo, bd, nn)


@jax.custom_vjp
def _segsum(x):
    return _segsum_impl(x)


_segsum.defvjp(lambda x: (_segsum_impl(x), None), lambda _, g: (_segsum_impl(g),))


@jax.custom_vjp
def _mmb(a, w):
    return _dot(a.astype(BF16), w.astype(BF16), ((1,), (0,)))


def _mmb_fwd(a, w):
    return _mmb(a, w), (a, w)


def _mmb_bwd(res, g):
    a, w = res
    gb = g.astype(BF16)
    return _dot(gb, w.astype(BF16), ((1,), (1,))), _dot(a.astype(BF16), gb, ((0,), (0,)))


_mmb.defvjp(_mmb_fwd, _mmb_bwd)


def _f_norm(x, g):
    return (_rms(x, g),)


def _f_post1(x, u, g2, g3):
    x1 = x + _rms(u, g2)
    return x1, _rms(x1, g3)


def _f_swiglu(ag, au):
    return (ag * _sigmoid(ag) * au,)


def _f_merge(pg1, pg2, m1, m2, b1, b2):
    return (_sigmoid(pg1 + b1) * m1 + _sigmoid(pg2 + b2) * m2,)


def _f_out(x1, f, g4):
    return (x1 + _rms(f, g4),)


def _f_rwpre(pr, pk, pv, pz, qr, qk, qv, qz, mur, muk, muv, muz, w0, wup, a0, aup, gup, k_k, k_a):
    r = pr + (qr - pr) * mur
    k = pk + (qk - pk) * muk
    v = pv + (qv - pv) * muv
    z = pz + (qz - pz) * muz
    w_raw = w0 + _mmb(jnp.tanh(z), wup)
    lw = -jnp.exp(-_softplus(-w_raw) - 0.5)
    a = _sigmoid(a0 + _mmb(z, aup))
    g = _mmb(_sigmoid(z), gup)
    kk = k * k_k
    kap = kk * lax.rsqrt(jnp.maximum(_segsum(kk * kk), 1e-24))
    k2 = k * (1.0 + (a - 1.0) * k_a)
    return r, lw, k2, v, kap, a, g


def _f_rwpost(y, r, k2, v, g, lnx_w, lnx_b, r_k):
    inv = 1.0 / HEAD_DIM
    yc = y - _segsum(y) * inv
    var = _segsum(yc * yc) * inv
    yn = yc * lax.rsqrt(var + GN_EPS) * lnx_w + lnx_b
    bonus = _segsum(r * k2 * r_k) * v
    return ((yn + bonus) * g,)


RW_GROUPS = (0, 512, 1024, 1536, RW_COLS)


def _column_groups(p):
    return [p[:, a:b] for a, b in zip(RW_GROUPS[:-1], RW_GROUPS[1:])]


def _previous_tokens(p, halo, first_of_sequence):
    rows = lax.broadcasted_iota(jnp.int32, (p.shape[0], 1), 0)
    before = jnp.where(first_of_sequence, 0.0, halo[SUBLANES - 1:SUBLANES, :])
    return jnp.where(rows == 0, before, pltpu.roll(p, 1, axis=0))


def _halo_spec(tile, order):
    per = tile // SUBLANES
    return pl.BlockSpec((SUBLANES, RW_COLS), lambda i: (jnp.maximum(order(i) * per - 1, 0), 0))


def _rw_pre(p_rw, params, S, tile=128):
    T = p_rw.shape[0]
    tile = min(tile, T)
    assert S % tile == 0
    n_p = len(params)

    def body(*refs):
        p_ref, halo_ref = refs[0], refs[1]
        par = [x[...].astype(F32) for x in refs[2:2 + n_p]]
        p = p_ref[...]
        first = lax.rem(pl.program_id(0) * tile, S) == 0
        prev = _previous_tokens(p, halo_ref[...], first)
        for o_ref, val in zip(refs[2 + n_p:], _f_rwpre(*_column_groups(p), *_column_groups(prev), *par)):
            o_ref[...] = val

    out_spec = pl.BlockSpec((tile, RW_WIDTH), lambda i: (i, 0))
    return pl.pallas_call(
        body, name="rw_pre",
        grid=(T // tile,),
        in_specs=[pl.BlockSpec((tile, RW_COLS), lambda i: (i, 0)), _halo_spec(tile, lambda i: i)]
                 + [_full_spec(q.shape) for q in params],
        out_specs=[out_spec] * 7,
        out_shape=[jax.ShapeDtypeStruct((T, RW_WIDTH), F32)] * 7,
        compiler_params=_params(("parallel",)),
    )(p_rw, p_rw, *params)


def _rw_pre_bwd(p_rw, params, cts, S, tile=128):
    T = p_rw.shape[0]
    tile = min(tile, T)
    assert S % tile == 0
    nt = T // tile
    n_p = len(params)
    ct_flat = [c for group in cts for c in group]
    ct_sizes = [len(group) for group in cts]
    n_ct = len(ct_flat)

    def body(*refs):
        p_ref, halo_ref = refs[0], refs[1]
        par = [x[...].astype(F32) for x in refs[2:2 + n_p]]
        ct_vals = [x[...] for x in refs[2 + n_p:2 + n_p + n_ct]]
        dp_ref = refs[2 + n_p + n_ct]
        dpar_refs = refs[3 + n_p + n_ct:3 + 2 * n_p + n_ct]
        carry = refs[-1]
        step = pl.program_id(0)

        @pl.when(step == 0)
        def _():
            carry[...] = jnp.zeros_like(carry)
            for ref in dpar_refs:
                ref[...] = jnp.zeros_like(ref)

        ct_in, q = [], 0
        for n in ct_sizes:
            ct_in.append(functools.reduce(lambda u, v: u + v, ct_vals[q:q + n]))
            q += n
        p = p_ref[...]
        first = lax.rem((nt - 1 - step) * tile, S) == 0
        prev = _previous_tokens(p, halo_ref[...], first)
        _, vjp = jax.vjp(_f_rwpre, *_column_groups(p), *_column_groups(prev), *par)
        grads = vjp(tuple(ct_in))
        d_here = jnp.concatenate(grads[0:4], axis=1)
        d_prev = jnp.concatenate(grads[4:8], axis=1)
        rows = lax.broadcasted_iota(jnp.int32, (tile, 1), 0)
        from_next = jnp.where(rows == tile - 1, carry[0:1, :], pltpu.roll(d_prev, tile - 1, axis=0))
        dp_ref[...] = (d_here + from_next).astype(dp_ref.dtype)
        carry[...] = jnp.broadcast_to(jnp.where(first, 0.0, d_prev[0:1, :]), carry.shape)
        for ref, g in zip(dpar_refs, grads[8:]):
            ref[...] += g

    back = lambda i: nt - 1 - i
    row = lambda w: pl.BlockSpec((tile, w), lambda i: (back(i), 0))
    res = pl.pallas_call(
        body, name="rw_pre_bwd",
        grid=(nt,),
        in_specs=[row(RW_COLS), _halo_spec(tile, back)] + [_full_spec(q.shape) for q in params]
                 + [row(RW_WIDTH)] * n_ct,
        out_specs=[row(RW_COLS)] + [_full_spec(q.shape) for q in params],
        out_shape=[jax.ShapeDtypeStruct((T, RW_COLS), BF16)] + [jax.ShapeDtypeStruct(q.shape, F32) for q in params],
        scratch_shapes=[pltpu.VMEM((SUBLANES, RW_COLS), F32)],
        compiler_params=_params(("arbitrary",)),
    )(p_rw, p_rw, *params, *ct_flat)
    return res[0], res[1:]


def _loss_head(x1, f, target, g4, tile=256):
    T, D = x1.shape
    tile = min(tile, T)

    def body(x1_ref, f_ref, t_ref, g_ref, loss_ref, dx1_ref, df_ref, dg_ref):
        (y,), vjp = jax.vjp(_f_out, x1_ref[...], f_ref[...], g_ref[...])
        err = y - t_ref[...]
        dx1, df, dg = vjp((err * (1.0 / D),))
        dx1_ref[...] = dx1
        df_ref[...] = df.astype(df_ref.dtype)

        @pl.when(pl.program_id(0) == 0)
        def _():
            loss_ref[...] = jnp.zeros_like(loss_ref)
            dg_ref[...] = jnp.zeros_like(dg_ref)

        part = jnp.sum(jnp.sum(err * err, axis=1, keepdims=True), axis=0, keepdims=True) * (0.5 / D)
        loss_ref[...] += jnp.broadcast_to(part, loss_ref.shape)
        dg_ref[...] += dg

    row = pl.BlockSpec((tile, D), lambda i: (i, 0))
    return pl.pallas_call(
        body, name="loss_head",
        grid=(T // tile,),
        in_specs=[row, row, row, _full_spec(g4.shape)],
        out_specs=[_full_spec((SUBLANES, LANES)), row, row, _full_spec(g4.shape)],
        out_shape=[jax.ShapeDtypeStruct((SUBLANES, LANES), F32), jax.ShapeDtypeStruct((T, D), F32),
                   jax.ShapeDtypeStruct((T, D), BF16), jax.ShapeDtypeStruct(g4.shape, F32)],
        compiler_params=_params(("arbitrary",)),
    )(x1, f, target, g4)


def _nn(a, b):
    return _dot(a, b, ((1,), (0,)))


def _nt(a, b):
    return _dot(a, b, ((1,), (1,)))


def _tn(a, b):
    return _dot(a, b, ((0,), (0,)))


def _split_dot(x, u):
    hi = x.astype(BF16)
    lo = (x - hi.astype(F32)).astype(BF16)
    return _nn(hi, u) + _nn(lo, u)


def _head_masks():
    lane = lax.broadcasted_iota(jnp.int32, (1, LANES), 1)
    return [((lane >= h * HEAD_DIM) & (lane < (h + 1) * HEAD_DIM)).astype(F32) for h in range(LANES // HEAD_DIM)]


def _key_tri(op):
    row = lax.broadcasted_iota(jnp.int32, (ATTN_KEYS, ATTN_KEYS), 0)
    col = lax.broadcasted_iota(jnp.int32, (ATTN_KEYS, ATTN_KEYS), 1)
    return op(row, col).astype(BF16)


def _causal(qb, sub):
    row = lax.broadcasted_iota(jnp.int32, (qb, ATTN_KEYS), 0)
    col = lax.broadcasted_iota(jnp.int32, (qb, ATTN_KEYS), 1)
    return col + sub * ATTN_KEYS < row


def _sb_weights(qhs, kb, c_fails, u_gt, strict, scale):
    zs = [_nt(qh, kb) * scale for qh in qhs]
    Ls = [jnp.minimum(-z, 0.0) - jnp.log(1.0 + jnp.exp(-jnp.abs(z))) for z in zs]
    Lms = Ls if strict is None else [jnp.where(strict, L, 0.0) for L in Ls]
    cums = [_split_dot(Lm, u_gt) for Lm in Lms]
    As = [jnp.exp(z + L + c + cum) for z, L, c, cum in zip(zs, Ls, c_fails, cums)]
    if strict is not None:
        As = [jnp.where(strict, A, 0.0) for A in As]
    return zs, Ls, Lms, As


def _attn_specs(S, qb):
    nq = S // qb
    q_spec = pl.BlockSpec((qb, LANES), lambda b, p, i: (b * nq + i, p))
    k_spec = pl.BlockSpec((S, LANES), lambda b, p, i: (b, SB_WIDTH // LANES + p))
    v_spec = pl.BlockSpec((S, LANES), lambda b, p, i: (b, 2 * SB_WIDTH // LANES + p))
    seq = pl.BlockSpec((S, LANES), lambda b, p, i: (b, p))
    return q_spec, k_spec, v_spec, q_spec, seq


def _key_walk(i, qb, block, carry):
    per = qb // ATTN_KEYS
    for sub in reversed(range(per)):
        carry = block(i * per + sub, carry, _causal(qb, sub))
    return lax.fori_loop(0, i * per, lambda j, c: block(i * per - 1 - j, c, None), carry)


def _attn_fwd(proj, B, S):
    qb = min(ATTN_QUERIES, S)
    scale = HEAD_DIM ** -0.5

    def body(q_ref, k_ref, v_ref, o_ref):
        i = pl.program_id(2)
        masks = _head_masks()
        u_gt = _key_tri(lambda r, c: r > c)
        q = q_ref[...]
        qhs = [(q * m).astype(BF16) for m in masks]

        def block(J, carry, strict_mask):
            acc, cs = carry
            r0 = pl.multiple_of(J * ATTN_KEYS, ATTN_KEYS)
            kb = k_ref[pl.ds(r0, ATTN_KEYS), :].astype(BF16)
            vb = v_ref[pl.ds(r0, ATTN_KEYS), :]
            _, _, Lms, As = _sb_weights(qhs, kb, cs, u_gt, strict_mask, scale)
            for A, m in zip(As, masks):
                acc = acc + _nn(A.astype(BF16), (vb * m).astype(BF16))
            return acc, tuple(c + jnp.sum(Lm, axis=1, keepdims=True) for c, Lm in zip(cs, Lms))

        zero_c = tuple(jnp.zeros((qb, 1), F32) for _ in masks)
        carry = _key_walk(i, qb, block, (jnp.zeros((qb, LANES), F32), zero_c))
        o_ref[...] = carry[0]

    q_spec, k_spec, v_spec, blk, _ = _attn_specs(S, qb)
    return pl.pallas_call(
        body, name="sb_attn_fwd",
        grid=(B, SB_WIDTH // LANES, S // qb),
        in_specs=[q_spec, k_spec, v_spec],
        out_specs=blk,
        out_shape=jax.ShapeDtypeStruct((B * S, SB_WIDTH), F32),
        compiler_params=_params(("parallel", "parallel", "arbitrary")),
    )(proj, proj, proj)


def _attn_bwd(proj, o, do, B, S):
    qb = min(ATTN_QUERIES, S)
    nq = S // qb
    scale = HEAD_DIM ** -0.5

    def body(q_ref, k_ref, v_ref, o_ref, do_ref, dq_ref, dk_out, dv_out, dk_ref, dv_ref):
        i = pl.program_id(2)

        @pl.when(i == 0)
        def _():
            dk_ref[...] = jnp.zeros_like(dk_ref)
            dv_ref[...] = jnp.zeros_like(dv_ref)

        masks = _head_masks()
        u_gt = _key_tri(lambda r, c: r > c)
        u_ge = _key_tri(lambda r, c: r >= c)
        q = q_ref[...]
        do_b = do_ref[...].astype(BF16)
        od = o_ref[...] * do_b.astype(F32)
        qhs = [(q * m).astype(BF16) for m in masks]
        dohs = [(do_b * m.astype(BF16)) for m in masks]
        totals = [jnp.sum(od * m, axis=1, keepdims=True) for m in masks]

        def block(J, carry, strict_mask):
            dq, c_fail, c_p = carry
            r0 = pl.multiple_of(J * ATTN_KEYS, ATTN_KEYS)
            kb32 = k_ref[pl.ds(r0, ATTN_KEYS), :]
            kb = kb32.astype(BF16)
            vb = v_ref[pl.ds(r0, ATTN_KEYS), :].astype(BF16)
            heads = range(len(masks))
            zs, Ls, Lms, As = _sb_weights(qhs, kb, c_fail, u_gt, strict_mask, scale)
            Abs = [A.astype(BF16) for A in As]
            Ps = [Abs[h].astype(F32) * _nt(dohs[h], vb) for h in heads]
            afters = [c_p[h] + _split_dot(Ps[h], u_ge) for h in heads]
            sigs = [jnp.exp(zs[h] + Ls[h]) for h in heads]
            dzs = [(Ps[h] * (1.0 - sigs[h]) - sigs[h] * (totals[h] - afters[h])) * scale for h in heads]
            if strict_mask is not None:
                dzs = [jnp.where(strict_mask, dz, 0.0) for dz in dzs]
            dzbs = [dz.astype(BF16) for dz in dzs]
            dv_blk = functools.reduce(lambda u, v: u + v, [_tn(Abs[h], dohs[h]) for h in heads])
            dk_blk = functools.reduce(lambda u, v: u + v, [_tn(dzbs[h], qhs[h]) for h in heads])
            for h, m in enumerate(masks):
                dq = dq + _nn(dzbs[h], (kb32 * m).astype(BF16))
            dk_ref[pl.ds(r0, ATTN_KEYS), :] += dk_blk
            dv_ref[pl.ds(r0, ATTN_KEYS), :] += dv_blk
            new_fail = tuple(c_fail[h] + jnp.sum(Lms[h], axis=1, keepdims=True) for h in heads)
            new_p = tuple(c_p[h] + jnp.sum(Ps[h], axis=1, keepdims=True) for h in heads)
            return dq, new_fail, new_p

        zc = tuple(jnp.zeros((qb, 1), F32) for _ in masks)
        carry = _key_walk(i, qb, block, (jnp.zeros((qb, LANES), F32), zc, zc))
        dq_ref[...] = carry[0].astype(dq_ref.dtype)

        @pl.when(i == nq - 1)
        def _():
            dk_out[...] = dk_ref[...].astype(dk_out.dtype)
            dv_out[...] = dv_ref[...].astype(dv_out.dtype)

    q_spec, k_spec, v_spec, blk, seq = _attn_specs(S, qb)
    return pl.pallas_call(
        body, name="sb_attn_bwd",
        grid=(B, SB_WIDTH // LANES, nq),
        in_specs=[q_spec, k_spec, v_spec, blk, blk],
        out_specs=[blk, seq, seq],
        out_shape=[jax.ShapeDtypeStruct((B * S, SB_WIDTH), BF16)] * 3,
        scratch_shapes=[pltpu.VMEM((S, LANES), F32), pltpu.VMEM((S, LANES), F32)],
        compiler_params=_params(("parallel", "parallel", "arbitrary")),
    )(proj, proj, proj, o, do)


_BATCHED = {"nn": "gmk,gkn->gmn", "nt": "gmk,gnk->gmn", "tn": "gkm,gkn->gmn"}


def _bdot_raw(a, b, kind, passes):
    e = functools.partial(jnp.einsum, _BATCHED[kind], preferred_element_type=F32)
    ah, bh = a.astype(BF16), b.astype(BF16)
    if passes == 1:
        return e(ah, bh)
    al, bl = (a - ah.astype(F32)).astype(BF16), (b - bh.astype(F32)).astype(BF16)
    return e(ah, bh) + e(ah, bl) + e(al, bh)


@functools.partial(jax.custom_vjp, nondiff_argnums=(2, 3))
def _bdot(a, b, kind, passes):
    return _bdot_raw(a, b, kind, passes)


def _bdot_fwd(a, b, kind, passes):
    return _bdot_raw(a, b, kind, passes), (a, b)


def _bdot_bwd(kind, passes, res, g):
    a, b = res
    if kind == "nn":
        return _bdot_raw(g, b, "nt", passes), _bdot_raw(a, g, "tn", passes)
    if kind == "nt":
        return _bdot_raw(g, b, "nn", passes), _bdot_raw(g, a, "tn", passes)
    return _bdot_raw(b, g, "nt", passes), _bdot_raw(a, g, "nn", passes)


_bdot.defvjp(_bdot_fwd, _bdot_bwd)


def _wkv_chunk(S0, r, lw, k, v, kap, a):
    G, C, N = r.shape
    row = lax.broadcasted_iota(jnp.int32, (C, C), 0)
    col = lax.broadcasted_iota(jnp.int32, (C, C), 1)
    incl = (col <= row).astype(F32)
    strict = (col < row).astype(F32)
    cum = _bdot(jnp.broadcast_to(incl, (G, C, C)), lw, "nn", 3)
    e_pos = jnp.exp(cum)
    e_neg = jnp.exp(-cum)
    al = -kap * jnp.exp(cum - lw)
    be = kap * a * e_neg
    kt = k * e_neg
    rt = r * e_pos
    bk = jnp.concatenate([be, kt], axis=1)
    mask = jnp.concatenate([jnp.concatenate([strict, strict], axis=1), jnp.concatenate([incl, incl], axis=1)], axis=0)
    m_all = _bdot(jnp.concatenate([al, rt], axis=1), bk, "nt", 3) * mask
    m_ab, m_ak = m_all[:, :C, :C], m_all[:, :C, C:]
    m_rb, m_rk = m_all[:, C:, :C], m_all[:, C:, C:]
    S0t = jnp.swapaxes(S0, 1, 2)
    sa = _bdot(jnp.concatenate([al, m_ak], axis=2), jnp.concatenate([S0t, v], axis=1), "nn", 3)
    p = m_ab
    steps = max(1, (C - 1).bit_length())
    for j in range(steps):
        sa = sa + _bdot(p, sa, "nn", 1)
        if j + 1 < steps:
            p = _bdot(p, p, "nn", 1)
    y = _bdot(jnp.concatenate([rt, m_rb, m_rk], axis=2), jnp.concatenate([S0t, sa, v], axis=1), "nn", 3)
    S1 = (S0 + _bdot(jnp.concatenate([sa, v], axis=1), bk, "tn", 1)) * e_pos[:, C - 1:C, :]
    return y, S1


def _split_heads(x):
    return jnp.stack([x[:, h * HEAD_DIM:(h + 1) * HEAD_DIM] for h in range(x.shape[1] // HEAD_DIM)], axis=0)


def _merge_heads(x):
    return jnp.concatenate([x[h] for h in range(x.shape[0])], axis=1)


def _wkv_fwd(r, lw, k, v, kap, a, B, S):
    C, H, N = WKV_CHUNK, RW_WIDTH // HEAD_DIM, HEAD_DIM
    nc = S // C

    def body(r_ref, lw_ref, k_ref, v_ref, kap_ref, a_ref, y_ref, st_ref, s_scr):
        @pl.when(pl.program_id(1) == 0)
        def _():
            s_scr[...] = jnp.zeros_like(s_scr)

        S0 = s_scr[...]
        st_ref[0, 0] = S0
        args = [_split_heads(ref[...]) for ref in (r_ref, lw_ref, k_ref, v_ref, kap_ref, a_ref)]
        y, S1 = _wkv_chunk(S0, *args)
        s_scr[...] = S1
        y_ref[...] = _merge_heads(y)

    row_spec = pl.BlockSpec((C, RW_WIDTH), lambda b, c: (b * nc + c, 0))
    return pl.pallas_call(
        body, name="wkv_fwd",
        grid=(B, nc),
        in_specs=[row_spec] * 6,
        out_specs=[row_spec, pl.BlockSpec((1, 1, H, N, N), lambda b, c: (b, c, 0, 0, 0))],
        out_shape=[jax.ShapeDtypeStruct((B * S, RW_WIDTH), F32), jax.ShapeDtypeStruct((B, nc, H, N, N), F32)],
        scratch_shapes=[pltpu.VMEM((H, N, N), F32)],
        compiler_params=_params(("arbitrary", "arbitrary")),
    )(r, lw, k, v, kap, a)


def _wkv_bwd(r, lw, k, v, kap, a, states, dy, B, S):
    C, H, N = WKV_CHUNK, RW_WIDTH // HEAD_DIM, HEAD_DIM
    nc = S // C

    def body(r_ref, lw_ref, k_ref, v_ref, kap_ref, a_ref, st_ref, dy_ref,
             dr_ref, dlw_ref, dk_ref, dv_ref, dkap_ref, da_ref, ds_scr):
        @pl.when(pl.program_id(1) == 0)
        def _():
            ds_scr[...] = jnp.zeros_like(ds_scr)

        args = [_split_heads(ref[...]) for ref in (r_ref, lw_ref, k_ref, v_ref, kap_ref, a_ref)]
        _, vjp = jax.vjp(_wkv_chunk, st_ref[0, 0], *args)
        g = vjp((_split_heads(dy_ref[...]), ds_scr[...]))
        ds_scr[...] = g[0]
        for ref, gv in zip((dr_ref, dlw_ref, dk_ref, dv_ref, dkap_ref, da_ref), g[1:]):
            ref[...] = _merge_heads(gv)

    row_spec = pl.BlockSpec((C, RW_WIDTH), lambda b, c: (b * nc + (nc - 1 - c), 0))
    st_spec = pl.BlockSpec((1, 1, H, N, N), lambda b, c: (b, nc - 1 - c, 0, 0, 0))
    return pl.pallas_call(
        body, name="wkv_bwd",
        grid=(B, nc),
        in_specs=[row_spec] * 6 + [st_spec, row_spec],
        out_specs=[row_spec] * 6,
        out_shape=[jax.ShapeDtypeStruct((B * S, RW_WIDTH), F32)] * 6,
        scratch_shapes=[pltpu.VMEM((H, N, N), F32)],
        compiler_params=_params(("arbitrary", "arbitrary")),
    )(r, lw, k, v, kap, a, states, dy)


HBM = pl.BlockSpec(memory_space=pl.ANY)


def _place():
    return lax.axis_index("x"), lax.axis_index("y"), lax.axis_index("c")


def _other_chips(x, y):
    return [(1 - x, y), (x, 1 - y), (1 - x, 1 - y)]


def _all_gather_chips(shards):
    n = len(shards)

    def body(*refs):
        ins, outs = refs[:n], refs[n:2 * n]
        ici_send, ici_recv, d2d_send, d2d_recv, local = refs[2 * n:]
        x, y, c = _place()
        me = 2 * x + y
        sib = (x, y, 1 - c)
        chips = _other_chips(x, y)
        started, copies = [], []
        for w in range(n):
            cp = pltpu.make_async_copy(ins[w].at[c], outs[w].at[me, c], local.at[w])
            cp.start()
            copies.append(cp)
            for j, (px, py) in enumerate(chips):
                rd = pltpu.make_async_remote_copy(
                    src_ref=ins[w].at[c], dst_ref=outs[w].at[me, c], send_sem=ici_send.at[3 * w + j],
                    recv_sem=ici_recv.at[3 * w + j], device_id=(px, py, c), device_id_type=MESH)
                rd.start()
                started.append(rd)
            rd = pltpu.make_async_remote_copy(
                src_ref=ins[w].at[c], dst_ref=outs[w].at[me, c], send_sem=d2d_send.at[4 * w + 3],
                recv_sem=d2d_recv.at[4 * w + 3], device_id=sib, device_id_type=MESH)
            rd.start()
            started.append(rd)
        for w in range(n):
            for j, (px, py) in enumerate(chips):
                src = 2 * px + py
                pltpu.make_async_remote_copy(
                    src_ref=ins[w].at[c], dst_ref=outs[w].at[src, c], send_sem=ici_send.at[3 * w + j],
                    recv_sem=ici_recv.at[3 * w + j], device_id=(px, py, c), device_id_type=MESH).wait_recv()
                rd = pltpu.make_async_remote_copy(
                    src_ref=outs[w].at[src, c], dst_ref=outs[w].at[src, c], send_sem=d2d_send.at[4 * w + j],
                    recv_sem=d2d_recv.at[4 * w + j], device_id=sib, device_id_type=MESH)
                rd.start()
                started.append(rd)
        for w in range(n):
            for j, (px, py) in enumerate(chips):
                pltpu.make_async_remote_copy(
                    src_ref=ins[w].at[c], dst_ref=outs[w].at[2 * px + py, 1 - c], send_sem=d2d_send.at[4 * w + j],
                    recv_sem=d2d_recv.at[4 * w + j], device_id=sib, device_id_type=MESH).wait_recv()
            pltpu.make_async_remote_copy(
                src_ref=ins[w].at[c], dst_ref=outs[w].at[me, 1 - c], send_sem=d2d_send.at[4 * w + 3],
                recv_sem=d2d_recv.at[4 * w + 3], device_id=sib, device_id_type=MESH).wait_recv()
        for rd in started:
            rd.wait_send()
        for cp in copies:
            cp.wait()

    return pl.pallas_call(
        body, name="gather_weights",
        in_specs=[HBM] * n, out_specs=[HBM] * n,
        out_shape=[jax.ShapeDtypeStruct((N_CHIPS,) + s.shape, s.dtype) for s in shards],
        scratch_shapes=[pltpu.SemaphoreType.DMA((3 * n,)), pltpu.SemaphoreType.DMA((3 * n,)),
                        pltpu.SemaphoreType.DMA((4 * n,)), pltpu.SemaphoreType.DMA((4 * n,)),
                        pltpu.SemaphoreType.DMA((n,))],
        compiler_params=pltpu.CompilerParams(has_side_effects=True),
    )(*shards)


def _pair_split(grads):
    n = len(grads)

    def body(*refs):
        ins, theirs = refs[:n], refs[n:2 * n]
        send, recv = refs[2 * n:]
        x, y, c = _place()
        sib = (x, y, 1 - c)
        rds = []
        for w in range(n):
            rd = pltpu.make_async_remote_copy(
                src_ref=ins[w].at[:, 1 - c], dst_ref=theirs[w], send_sem=send.at[w], recv_sem=recv.at[w],
                device_id=sib, device_id_type=MESH)
            rd.start()
            rds.append(rd)
        for rd in rds:
            rd.wait_recv()
        for rd in rds:
            rd.wait_send()

    return pl.pallas_call(
        body, name="grad_pair_split",
        in_specs=[HBM] * n, out_specs=[HBM] * n,
        out_shape=[jax.ShapeDtypeStruct((g.shape[0],) + g.shape[2:], g.dtype) for g in grads],
        scratch_shapes=[pltpu.SemaphoreType.DMA((n,)), pltpu.SemaphoreType.DMA((n,))],
        compiler_params=pltpu.CompilerParams(has_side_effects=True),
    )(*grads)


def _chip_scatter(parts):
    n = len(parts)

    def body(*refs):
        ins, outs = refs[:n], refs[n:2 * n]
        send, recv = refs[2 * n:]
        x, y, c = _place()
        me = 2 * x + y
        rds = []
        for w in range(n):
            for j, (px, py) in enumerate(_other_chips(x, y)):
                s = 3 * w + j
                rd = pltpu.make_async_remote_copy(
                    src_ref=ins[w].at[2 * px + py], dst_ref=outs[w].at[j], send_sem=send.at[s], recv_sem=recv.at[s],
                    device_id=(px, py, c), device_id_type=MESH)
                rd.start()
                rds.append(rd)
        for w in range(n):
            for j, (px, py) in enumerate(_other_chips(x, y)):
                s = 3 * w + j
                pltpu.make_async_remote_copy(
                    src_ref=ins[w].at[me], dst_ref=outs[w].at[j], send_sem=send.at[s], recv_sem=recv.at[s],
                    device_id=(px, py, c), device_id_type=MESH).wait_recv()
        for rd in rds:
            rd.wait_send()

    return pl.pallas_call(
        body, name="grad_chip_scatter",
        in_specs=[HBM] * n, out_specs=[HBM] * n,
        out_shape=[jax.ShapeDtypeStruct((N_CHIPS - 1,) + p.shape[1:], p.dtype) for p in parts],
        scratch_shapes=[pltpu.SemaphoreType.DMA((3 * n,)), pltpu.SemaphoreType.DMA((3 * n,))],
        compiler_params=pltpu.CompilerParams(has_side_effects=True),
    )(*parts)


def _pair_join(bufs):
    n = len(bufs)

    def body(*refs):
        ins, outs = refs[:n], refs[n:2 * n]
        send, recv = refs[2 * n:]
        x, y, c = _place()
        sib = (x, y, 1 - c)
        rds = []
        for w in range(n):
            rd = pltpu.make_async_remote_copy(
                src_ref=ins[w].at[c], dst_ref=outs[w].at[c], send_sem=send.at[w], recv_sem=recv.at[w],
                device_id=sib, device_id_type=MESH)
            rd.start()
            rds.append(rd)
        for w in range(n):
            pltpu.make_async_remote_copy(
                src_ref=ins[w].at[c], dst_ref=outs[w].at[1 - c], send_sem=send.at[w], recv_sem=recv.at[w],
                device_id=sib, device_id_type=MESH).wait_recv()
        for rd in rds:
            rd.wait_send()

    return pl.pallas_call(
        body, name="grad_pair_join",
        in_specs=[HBM] * n, out_specs=[HBM] * n,
        out_shape=[jax.ShapeDtypeStruct(b.shape, b.dtype) for b in bufs],
        input_output_aliases={w: w for w in range(n)},
        scratch_shapes=[pltpu.SemaphoreType.DMA((n,)), pltpu.SemaphoreType.DMA((n,))],
        compiler_params=pltpu.CompilerParams(has_side_effects=True),
    )(*bufs)


def _all_reduce_small(packed):
    R = packed.shape[0]

    def body(x_ref, o_ref, buf, send, recv):
        x, y, c = _place()
        me = 4 * x + 2 * y + c
        buf[me] = x_ref[...]
        rds = []
        for rel in range(1, N_DEV):
            fx, fy, fc = (rel >> 2) & 1, (rel >> 1) & 1, rel & 1
            peer = (1 - x if fx else x, 1 - y if fy else y, 1 - c if fc else c)
            rd = pltpu.make_async_remote_copy(
                src_ref=x_ref, dst_ref=buf.at[me], send_sem=send.at[rel - 1], recv_sem=recv.at[rel - 1],
                device_id=peer, device_id_type=MESH)
            rd.start()
            rds.append((rd, peer))
        for rel in range(1, N_DEV):
            rd, (px, py, pc) = rds[rel - 1]
            pltpu.make_async_remote_copy(
                src_ref=x_ref, dst_ref=buf.at[4 * px + 2 * py + pc], send_sem=send.at[rel - 1], recv_sem=recv.at[rel - 1],
                device_id=(px, py, pc), device_id_type=MESH).wait_recv()
        for rd, _ in rds:
            rd.wait_send()
        total = buf[0]
        for d in range(1, N_DEV):
            total = total + buf[d]
        o_ref[...] = total

    return pl.pallas_call(
        body, name="all_reduce_small",
        in_specs=[pl.BlockSpec(memory_space=pltpu.VMEM)],
        out_specs=pl.BlockSpec(memory_space=pltpu.VMEM),
        out_shape=jax.ShapeDtypeStruct(packed.shape, F32),
        scratch_shapes=[pltpu.VMEM((N_DEV, R, LANES), F32), pltpu.SemaphoreType.DMA((N_DEV - 1,)),
                        pltpu.SemaphoreType.DMA((N_DEV - 1,))],
        compiler_params=pltpu.CompilerParams(has_side_effects=True),
    )(packed)


def _pair_sum(name, split, theirs, core):
    n_chip, _, Rh, C = split.shape
    tile = _div_tile(Rh, 256, 2 * SUBLANES)
    nt = Rh // tile

    def body(core_ref, a_ref, b_ref, o_ref):
        o_ref[...] = (a_ref[...] + b_ref[...]).astype(o_ref.dtype)

    return pl.pallas_call(
        body, name=name,
        grid_spec=pltpu.PrefetchScalarGridSpec(
            num_scalar_prefetch=1,
            grid=(n_chip, nt),
            in_specs=[pl.BlockSpec((None, None, tile, C), lambda j, i, core_ref: (j, core_ref[0], i, 0)),
                      pl.BlockSpec((None, tile, C), lambda j, i, core_ref: (j, i, 0))],
            out_specs=pl.BlockSpec((None, tile, C), lambda j, i, core_ref: (j, i, 0)),
        ),
        out_shape=jax.ShapeDtypeStruct((n_chip, Rh, C), BF16),
        compiler_params=_params(("parallel", "parallel")),
    )(core, split, theirs)


def _chip_sum(name, own, landed, core):
    n_in, Rh, C = landed.shape
    tile = _div_tile(Rh, 256, 2 * SUBLANES)

    def body(core_ref, *refs):
        total = refs[0][...].astype(F32)
        for ref in refs[1:n_in + 1]:
            total = total + ref[...].astype(F32)
        refs[n_in + 1][...] = total

    slot = lambda j: pl.BlockSpec((None, tile, C), lambda i, core_ref: (j, i, 0))
    return pl.pallas_call(
        body, name=name,
        grid_spec=pltpu.PrefetchScalarGridSpec(
            num_scalar_prefetch=1,
            grid=(Rh // tile,),
            in_specs=[pl.BlockSpec((None, tile, C), lambda i, core_ref: (core_ref[1], i, 0))]
                     + [slot(j) for j in range(n_in)],
            out_specs=pl.BlockSpec((None, tile, C), lambda i, core_ref: (core_ref[0], i, 0)),
        ),
        out_shape=jax.ShapeDtypeStruct((2, Rh, C), F32),
        compiler_params=_params(("parallel",)),
    )(core, own, *([landed] * n_in))


def _adamw(name, w, g, m, v):
    R, C = w.shape
    tile = _div_tile(R, 256, SUBLANES)
    c1 = 1.0 / (1.0 - ADAM_B1 ** ADAM_STEP)
    c2 = 1.0 / (1.0 - ADAM_B2 ** ADAM_STEP)

    def body(w_ref, g_ref, m_ref, v_ref, d_ref, nm_ref, nv_ref):
        g_ = g_ref[...]
        nm = ADAM_B1 * m_ref[...] + (1.0 - ADAM_B1) * g_
        nv = ADAM_B2 * v_ref[...] + (1.0 - ADAM_B2) * (g_ * g_)
        d_ref[...] = -ADAM_LR * ((nm * c1) / (jnp.sqrt(nv * c2) + ADAM_EPS) + ADAM_WD * w_ref[...])
        nm_ref[...] = nm
        nv_ref[...] = nv

    spec = pl.BlockSpec((tile, C), lambda i: (i, 0))
    return pl.pallas_call(
        body, name=name,
        grid=(R // tile,),
        in_specs=[spec] * 4, out_specs=[spec] * 3,
        out_shape=[jax.ShapeDtypeStruct((R, C), F32)] * 3,
        compiler_params=_params(("parallel",)),
    )(w, g, m, v)


def _cols_to_shards(full):
    K, N = full.shape
    return full.reshape(K, N_CHIPS, N // N_CHIPS).transpose(1, 0, 2)


def _shards_to_cols(sh):
    return sh.transpose(1, 0, 2).reshape(sh.shape[1], -1)


def _rows_to_shards(full):
    return full.reshape(N_CHIPS, full.shape[0] // N_CHIPS, full.shape[1])


SMALL = ["norm_mix_pre", "b_gate", "mu_rw", "w0", "a0", "k_k", "k_a", "r_k", "lnx_w", "lnx_b",
         "norm_mix_post", "norm_ffn_pre", "norm_ffn_post"]
BIG = ["w_in", "w_up", "a_up", "g_up", "w_sb_out", "w_rw_out", "w_o", "w_ffn_gate", "w_ffn_up", "w_ffn_down"]
ROW_SHARDED = ("w_o", "w_ffn_down")
ORDER = ["norm_mix_pre", "w_in", "b_gate", "mu_rw", "w0", "w_up", "a0", "a_up", "g_up", "k_k", "k_a", "r_k",
         "lnx_w", "lnx_b", "w_sb_out", "w_rw_out", "w_o", "norm_mix_post", "norm_ffn_pre", "w_ffn_gate",
         "w_ffn_up", "w_ffn_down", "norm_ffn_post"]


def _pack_small(vals, extra_rows=0):
    rows = jnp.concatenate([vals[n].reshape(-1, LANES) for n in SMALL], axis=0)
    pad = (-(rows.shape[0] + extra_rows)) % SUBLANES + extra_rows
    return jnp.pad(rows, ((0, pad), (0, 0)))


def _unpack_small(packed, shapes):
    out, r = {}, 0
    for n in SMALL:
        size = 1
        for s in shapes[n]:
            size *= s
        out[n] = packed[r:r + size // LANES].reshape(shapes[n])
        r += size // LANES
    return out


def kernel(x, norm_mix_pre, w_in, b_gate, mu_rw, w0, w_up, a0, a_up, g_up, k_k, k_a, r_k, lnx_w, lnx_b, w_sb_out, w_rw_out, w_o, norm_mix_post, norm_ffn_pre, w_ffn_gate, w_ffn_up, w_ffn_down, norm_ffn_post, loss_target, m_norm_mix_pre, m_w_in, m_b_gate, m_mu_rw, m_w0, m_w_up, m_a0, m_a_up, m_g_up, m_k_k, m_k_a, m_r_k, m_lnx_w, m_lnx_b, m_w_sb_out, m_w_rw_out, m_w_o, m_norm_mix_post, m_norm_ffn_pre, m_w_ffn_gate, m_w_ffn_up, m_w_ffn_down, m_norm_ffn_post, v_norm_mix_pre, v_w_in, v_b_gate, v_mu_rw, v_w0, v_w_up, v_a0, v_a_up, v_g_up, v_k_k, v_k_a, v_r_k, v_lnx_w, v_lnx_b, v_w_sb_out, v_w_rw_out, v_w_o, v_norm_mix_post, v_norm_ffn_pre, v_w_ffn_gate, v_w_ffn_up, v_w_ffn_down, v_norm_ffn_post):
    W = dict(norm_mix_pre=norm_mix_pre, w_in=w_in, b_gate=b_gate, mu_rw=mu_rw, w0=w0, w_up=w_up, a0=a0, a_up=a_up,
             g_up=g_up, k_k=k_k, k_a=k_a, r_k=r_k, lnx_w=lnx_w, lnx_b=lnx_b, w_sb_out=w_sb_out, w_rw_out=w_rw_out,
             w_o=w_o, norm_mix_post=norm_mix_post, norm_ffn_pre=norm_ffn_pre, w_ffn_gate=w_ffn_gate,
             w_ffn_up=w_ffn_up, w_ffn_down=w_ffn_down, norm_ffn_post=norm_ffn_post)
    Mo = dict(norm_mix_pre=m_norm_mix_pre, w_in=m_w_in, b_gate=m_b_gate, mu_rw=m_mu_rw, w0=m_w0, w_up=m_w_up, a0=m_a0,
              a_up=m_a_up, g_up=m_g_up, k_k=m_k_k, k_a=m_k_a, r_k=m_r_k, lnx_w=m_lnx_w, lnx_b=m_lnx_b,
              w_sb_out=m_w_sb_out, w_rw_out=m_w_rw_out, w_o=m_w_o, norm_mix_post=m_norm_mix_post,
              norm_ffn_pre=m_norm_ffn_pre, w_ffn_gate=m_w_ffn_gate, w_ffn_up=m_w_ffn_up, w_ffn_down=m_w_ffn_down,
              norm_ffn_post=m_norm_ffn_post)
    Vo = dict(norm_mix_pre=v_norm_mix_pre, w_in=v_w_in, b_gate=v_b_gate, mu_rw=v_mu_rw, w0=v_w0, w_up=v_w_up, a0=v_a0,
              a_up=v_a_up, g_up=v_g_up, k_k=v_k_k, k_a=v_k_a, r_k=v_r_k, lnx_w=v_lnx_w, lnx_b=v_lnx_b,
              w_sb_out=v_w_sb_out, w_rw_out=v_w_rw_out, w_o=v_w_o, norm_mix_post=v_norm_mix_post,
              norm_ffn_pre=v_norm_ffn_pre, w_ffn_gate=v_w_ffn_gate, w_ffn_up=v_w_ffn_up, w_ffn_down=v_w_ffn_down,
              norm_ffn_post=v_norm_ffn_post)
    shapes = {n: W[n].shape for n in ORDER}
    B, S, D = x.shape
    T = B * S
    x2 = x.reshape(T, D)
    tgt = loss_target.reshape(T, D)
    vec = {n: W[n].reshape(1, -1) for n in SMALL}

    halved = [W[n][0].astype(BF16).reshape(2, W[n].shape[1] // 2, W[n].shape[2]) for n in BIG]
    full = {}
    for n, gth in zip(BIG, _all_gather_chips(halved)):
        gth = gth.reshape((N_CHIPS,) + W[n].shape[1:])
        full[n] = gth.reshape(-1, gth.shape[2]) if n in ROW_SHARDED else _shards_to_cols(gth)
    w_sb, w_rw, w_gt = full["w_in"][:, :SB_COLS], full["w_in"][:, SB_COLS:SB_COLS + RW_COLS], full["w_in"][:, SB_COLS + RW_COLS:]
    lora_rows = {"w_up": 0, "a_up": 64, "g_up": 128}
    lora = {n: jnp.pad(full[n], ((r0, LORA_COLS - r0 - full[n].shape[0]), (0, 0))) for n, r0 in lora_rows.items()}
    mu = vec["mu_rw"]
    mu_parts = [mu[:, :512], mu[:, 512:1024], mu[:, 1024:1536], mu[:, 1536:]]
    b1, b2 = vec["b_gate"][:, :D], vec["b_gate"][:, D:]

    (h1,) = _rowwise("norm_mix_pre", _f_norm, [(x2, D, 0)], [vec["norm_mix_pre"]], [(D, BF16)])
    p_sb = _mm("proj_sb", h1, w_sb)
    p_rw = _mm("proj_rw", h1, w_rw)
    p_gt = _mm("proj_gate", h1, w_gt)
    o_sb = _attn_fwd(p_sb, B, S)
    pre_params = mu_parts + [vec["w0"], lora["w_up"], vec["a0"], lora["a_up"], lora["g_up"], vec["k_k"], vec["k_a"]]
    r_, lw_, k2_, v_, kap_, a_, g_ = _rw_pre(p_rw, pre_params, S)
    y_wkv, states = _wkv_fwd(r_, lw_, k2_, v_, kap_, a_, B, S)
    post_rows = [(y_wkv, 512, 0), (r_, 512, 0), (k2_, 512, 0), (v_, 512, 0), (g_, 512, 0)]
    post_params = [vec["lnx_w"], vec["lnx_b"], vec["r_k"]]
    (o_rw,) = _rowwise("rw_post", _f_rwpost, post_rows, post_params, [(512, BF16)])
    m1 = _mm("mix_sb_out", o_sb, full["w_sb_out"])
    m2 = _mm("mix_rw_out", o_rw, full["w_rw_out"])
    merge_rows = [(p_gt, D, 0), (p_gt, D, 1), (m1, D, 0), (m2, D, 0)]
    (merged,) = _rowwise("merge", _f_merge, merge_rows, [b1, b2], [(D, BF16)])
    u = _mm("mix_out", merged, full["w_o"])
    post1_params = [vec["norm_mix_post"], vec["norm_ffn_pre"]]
    x1, h2 = _rowwise("post_mix", _f_post1, [(x2, D, 0), (u, D, 0)], post1_params, [(D, F32), (D, BF16)])
    ag = _mm("ffn_gate", h2, full["w_ffn_gate"], out_dtype=BF16)
    au = _mm("ffn_up", h2, full["w_ffn_up"], out_dtype=BF16)
    (sw,) = _rowwise("swiglu", _f_swiglu, [(ag, D_FF, 0), (au, D_FF, 0)], [], [(D_FF, BF16)])
    f = _mm("ffn_down", sw, full["w_ffn_down"])
    loss_part, dx1, df, dg4 = _loss_head(x1, f, tgt, vec["norm_ffn_post"])

    gbig, gsmall = {}, {"norm_ffn_post": dg4}
    dsw = _mm("d_swiglu_out", df, full["w_ffn_down"], tb=True, out_dtype=BF16)
    gbig["w_ffn_down"] = _mm("g_ffn_down", sw, df, ta=True)
    (dag, dau), _ = _rowwise_vjp("swiglu_bwd", _f_swiglu, [(ag, D_FF, 0), (au, D_FF, 0)], [], [[dsw]], [True, True], [],
                                 bf16_rows=(0, 1))
    dh2 = _mm("d_h2_gate", dag, full["w_ffn_gate"], tb=True)
    dh2 = _mm("d_h2_up", dau, full["w_ffn_up"], tb=True, acc=dh2)
    gbig["w_ffn_gate"] = _mm("g_ffn_gate", h2, dag, ta=True)
    gbig["w_ffn_up"] = _mm("g_ffn_up", h2, dau, ta=True)
    (dx_res, du), (dg2, dg3) = _rowwise_vjp("post_mix_bwd", _f_post1, [(x2, D, 0), (u, D, 0)], post1_params,
                                            [[dx1], [dh2]], [True, True], [True, True], bf16_rows=(1,))
    gsmall["norm_mix_post"], gsmall["norm_ffn_pre"] = dg2, dg3
    dmerged = _mm("d_merged", du, full["w_o"], tb=True)
    gbig["w_o"] = _mm("g_w_o", merged, du, ta=True)
    (dpg1, dpg2, dm1, dm2), (db1, db2) = _rowwise_vjp("merge_bwd", _f_merge, merge_rows, [b1, b2], [[dmerged]],
                                                      [True] * 4, [True, True], bf16_rows=(0, 1, 2, 3))
    gsmall["b_gate"] = jnp.concatenate([db1, db2], axis=1)
    do_sb = _mm("d_o_sb", dm1, full["w_sb_out"], tb=True)
    do_rw = _mm("d_o_rw", dm2, full["w_rw_out"], tb=True)
    gbig["w_sb_out"] = _mm("g_sb_out", o_sb, dm1, ta=True)
    gbig["w_rw_out"] = _mm("g_rw_out", o_rw, dm2, ta=True)
    (dy_wkv, dr_a, dk2_a, dv_a, dg_), (dlnx_w, dlnx_b, dr_k) = _rowwise_vjp(
        "rw_post_bwd", _f_rwpost, post_rows, post_params, [[do_rw]], [True] * 5, [True] * 3)
    gsmall["lnx_w"], gsmall["lnx_b"], gsmall["r_k"] = dlnx_w, dlnx_b, dr_k
    dr_b, dlw, dk2_b, dv_b, dkap, da = _wkv_bwd(r_, lw_, k2_, v_, kap_, a_, states, dy_wkv, B, S)
    pre_cts = [[dr_a, dr_b], [dlw], [dk2_a, dk2_b], [dv_a, dv_b], [dkap], [da], [dg_]]
    dp_rw, dpre_params = _rw_pre_bwd(p_rw, pre_params, pre_cts, S)
    gsmall["mu_rw"] = jnp.concatenate(dpre_params[:4], axis=1)
    gsmall["w0"], gsmall["a0"], gsmall["k_k"], gsmall["k_a"] = dpre_params[4], dpre_params[6], dpre_params[9], dpre_params[10]
    glora = {"w_up": dpre_params[5][0:64], "a_up": dpre_params[7][64:128], "g_up": dpre_params[8][128:256]}
    dq, dk, dv = _attn_bwd(p_sb, o_sb, do_sb, B, S)
    dh1 = _mm("d_h1_q", dq, w_sb[:, :512], tb=True)
    dh1 = _mm("d_h1_k", dk, w_sb[:, 512:1024], tb=True, acc=dh1)
    dh1 = _mm("d_h1_v", dv, w_sb[:, 1024:], tb=True, acc=dh1)
    dh1 = _mm("d_h1_rw", dp_rw, w_rw, tb=True, acc=dh1)
    dh1 = _mm("d_h1_g1", dpg1, w_gt[:, :D], tb=True, acc=dh1)
    dh1 = _mm("d_h1_g2", dpg2, w_gt[:, D:], tb=True, acc=dh1)
    gbig["w_in"] = jnp.concatenate(
        [_mm("g_in_" + tag, h1, d, ta=True)
         for tag, d in (("q", dq), ("k", dk), ("v", dv), ("rw", dp_rw), ("g1", dpg1), ("g2", dpg2))], axis=1)
    (grad_x2,), (dg1,) = _rowwise_vjp("norm_mix_pre_bwd", _f_norm, [(x2, D, 0)], [vec["norm_mix_pre"]], [[dh1]],
                                      [True], [True], add_to={0: dx_res})
    gsmall["norm_mix_pre"] = dg1
    gbig.update(glora)

    split = []
    for n in BIG:
        g = _rows_to_shards(gbig[n]) if n in ROW_SHARDED else _cols_to_shards(gbig[n])
        split.append(g.reshape(N_CHIPS, 2, g.shape[1] // 2, g.shape[2]))
    core = jnp.stack([lax.axis_index("c"), 2 * lax.axis_index("x") + lax.axis_index("y")]).astype(jnp.int32)
    theirs = _pair_split(split)
    chip_sums = [_pair_sum("pair_sum_" + n, a, b, core) for n, a, b in zip(BIG, split, theirs)]
    landed = _chip_scatter(chip_sums)
    joined = _pair_join([_chip_sum("chip_sum_" + n, own, got, core) for n, own, got in zip(BIG, chip_sums, landed)])
    grads = {n: j.reshape(W[n].shape[1:]) for n, j in zip(BIG, joined)}

    small_local = _pack_small({n: gsmall[n] for n in SMALL}, extra_rows=1)
    loss_row = small_local.shape[0] - 1
    small_local = small_local.at[loss_row].set(loss_part[0])
    small_sum = _all_reduce_small(small_local)
    loss = small_sum[loss_row, 0]

    delta, new_m, new_v = {}, {}, {}
    for n in BIG:
        d_, m_, v2_ = _adamw("adamw_" + n, W[n][0], grads[n], Mo[n][0], Vo[n][0])
        delta[n], new_m[n], new_v[n] = d_[None], m_[None], v2_[None]
        grads[n] = grads[n][None]
    pk = lambda src: _pack_small({n: src[n] for n in SMALL}, extra_rows=1)
    d_s, m_s, v_s = _adamw("adamw_small", pk(W), small_sum.at[loss_row].set(0.0), pk(Mo), pk(Vo))
    for dst, packed in ((grads, small_sum), (delta, d_s), (new_m, m_s), (new_v, v_s)):
        dst.update(_unpack_small(packed, shapes))

    return (loss, grad_x2.reshape(B, S, D), *[grads[n] for n in ORDER], *[delta[n] for n in ORDER],
            *[new_m[n] for n in ORDER], *[new_v[n] for n in ORDER])
```

```python
import functools

import jax
import jax.numpy as jnp
from jax import lax
from jax.experimental import pallas as pl
from jax.experimental.pallas import tpu as pltpu

F32 = jnp.float32
BF16 = jnp.bfloat16
MESH = pl.DeviceIdType.MESH

D_MODEL = 1024
SB_HEADS = 8
HEAD_DIM = 64
SB_WIDTH = SB_HEADS * HEAD_DIM
RW_WIDTH = 512
LORA_COLS = 256
SB_COLS = 3 * SB_WIDTH
RW_COLS = 3 * RW_WIDTH + LORA_COLS
GATE_COLS = 2 * D_MODEL
D_FF = 2816
RMS_EPS = 1e-6
GN_EPS = HEAD_DIM * 1e-5
WKV_CHUNK = 64
WKV_SEQS = 2
ATTN_QUERIES = 512
ATTN_KEYS = 128
LANES = 128
SUBLANES = 8
N_CHIPS = 4
N_DEV = 8

ADAM_LR = 0.001
ADAM_B1 = 0.9
ADAM_B2 = 0.999
ADAM_EPS = 1e-08
ADAM_WD = 0.01
ADAM_STEP = 10

VMEM_LIMIT = 48 * 1024 * 1024


def _params(sem=None, **kw):
    if sem is not None:
        kw["dimension_semantics"] = sem
    return pltpu.CompilerParams(vmem_limit_bytes=VMEM_LIMIT, **kw)


def _div_tile(dim, pref, mult=LANES):
    if dim <= pref:
        return dim
    t = pref - pref % mult
    while t >= mult:
        if dim % t == 0:
            return t
        t -= mult
    return dim


def _dot(a, b, dims):
    return lax.dot_general(a, b, (dims, ((), ())), preferred_element_type=F32)


def _mm(name, a, b, *, ta=False, tb=False, acc=None, out_dtype=F32):
    if ta:
        K, M = a.shape
    else:
        M, K = a.shape
    N = b.shape[0] if tb else b.shape[1]
    if ta:
        tm, tn, tk = _div_tile(M, 1408), _div_tile(N, 1408), _div_tile(K, 512)
    else:
        tm, tn, tk = _div_tile(M, 512), _div_tile(N, 1408), _div_tile(K, 1408)
    nk = K // tk
    dims = ((0,) if ta else (1,), (1,) if tb else (0,))
    has_acc = acc is not None

    def body(*refs):
        a_ref, b_ref = refs[0], refs[1]
        o_ref, scr = refs[-2], refs[-1]
        k = pl.program_id(2)
        part = _dot(a_ref[...].astype(BF16), b_ref[...].astype(BF16), dims)

        @pl.when(k == 0)
        def _():
            scr[...] = part + refs[2][...] if has_acc else part

        @pl.when(k > 0)
        def _():
            scr[...] += part

        @pl.when(k == nk - 1)
        def _():
            o_ref[...] = scr[...].astype(o_ref.dtype)

    a_spec = pl.BlockSpec((tk, tm), lambda i, j, k: (k, i)) if ta else pl.BlockSpec((tm, tk), lambda i, j, k: (i, k))
    b_spec = pl.BlockSpec((tn, tk), lambda i, j, k: (j, k)) if tb else pl.BlockSpec((tk, tn), lambda i, j, k: (k, j))
    o_spec = pl.BlockSpec((tm, tn), lambda i, j, k: (i, j))
    return pl.pallas_call(
        body, name=name,
        grid=(M // tm, N // tn, nk),
        in_specs=[a_spec, b_spec] + ([o_spec] if has_acc else []),
        out_specs=o_spec,
        out_shape=jax.ShapeDtypeStruct((M, N), out_dtype),
        scratch_shapes=[pltpu.VMEM((tm, tn), F32)],
        compiler_params=_params(("parallel", "parallel", "arbitrary")),
    )(*([a, b] + ([acc] if has_acc else [])))


def _row_spec(tile, width, colblk):
    return pl.BlockSpec((tile, width), lambda i: (i, colblk))


def _full_spec(shape):
    return pl.BlockSpec(shape, lambda i: (0,) * len(shape))


def _rowwise(name, fn, rows, params, outs, tile=256):
    T = rows[0][0].shape[0]
    tile = min(tile, T)
    n_r, n_p = len(rows), len(params)

    def body(*refs):
        r = [x[...].astype(F32) for x in refs[:n_r]]
        p = [x[...].astype(F32) for x in refs[n_r:n_r + n_p]]
        for o_ref, val in zip(refs[n_r + n_p:], fn(*r, *p)):
            o_ref[...] = val.astype(o_ref.dtype)

    return pl.pallas_call(
        body, name=name,
        grid=(T // tile,),
        in_specs=[_row_spec(tile, w, cb) for _, w, cb in rows] + [_full_spec(p.shape) for p in params],
        out_specs=[_row_spec(tile, w, 0) for w, _ in outs],
        out_shape=[jax.ShapeDtypeStruct((T, w), dt) for w, dt in outs],
        compiler_params=_params(("parallel",)),
    )(*([a for a, _, _ in rows] + list(params)))


def _rowwise_vjp(name, fn, rows, params, cts, need_rows, need_params, add_to=None, tile=256, bf16_rows=()):
    add_to = add_to or {}
    T = rows[0][0].shape[0]
    tile = min(tile, T)
    n_r, n_p = len(rows), len(params)
    ct_flat = [c for group in cts for c in group]
    ct_sizes = [len(group) for group in cts]
    add_idx = sorted(add_to)
    row_out = [i for i in range(n_r) if need_rows[i]]
    par_out = [i for i in range(n_p) if need_params[i]]
    n_ct, n_add = len(ct_flat), len(add_idx)

    def body(*refs):
        pos = 0
        r = [x[...].astype(F32) for x in refs[pos:pos + n_r]]
        pos += n_r
        p = [x[...].astype(F32) for x in refs[pos:pos + n_p]]
        pos += n_p
        ct_vals = [x[...].astype(F32) for x in refs[pos:pos + n_ct]]
        pos += n_ct
        adds = {i: x[...] for i, x in zip(add_idx, refs[pos:pos + n_add])}
        pos += n_add
        drow_refs = refs[pos:pos + len(row_out)]
        pos += len(row_out)
        dpar_refs = refs[pos:pos + len(par_out)]
        ct_in, q = [], 0
        for n in ct_sizes:
            ct_in.append(functools.reduce(lambda u, v: u + v, ct_vals[q:q + n]))
            q += n
        _, vjp = jax.vjp(fn, *r, *p)
        grads = vjp(tuple(ct_in))
        for ref, i in zip(drow_refs, row_out):
            g = grads[i]
            ref[...] = (g + adds[i] if i in adds else g).astype(ref.dtype)

        @pl.when(pl.program_id(0) == 0)
        def _():
            for ref in dpar_refs:
                ref[...] = jnp.zeros_like(ref)

        for ref, i in zip(dpar_refs, par_out):
            ref[...] += grads[n_r + i]

    ct_widths = [c.shape[1] for c in ct_flat]
    in_specs = ([_row_spec(tile, w, cb) for _, w, cb in rows] + [_full_spec(p.shape) for p in params]
                + [_row_spec(tile, w, 0) for w in ct_widths] + [_row_spec(tile, rows[i][1], 0) for i in add_idx])
    out_specs = [_row_spec(tile, rows[i][1], 0) for i in row_out] + [_full_spec(params[i].shape) for i in par_out]
    out_shape = ([jax.ShapeDtypeStruct((T, rows[i][1]), BF16 if i in bf16_rows else F32) for i in row_out]
                 + [jax.ShapeDtypeStruct(params[i].shape, F32) for i in par_out])
    res = pl.pallas_call(
        body, name=name,
        grid=(T // tile,),
        in_specs=in_specs, out_specs=out_specs, out_shape=out_shape,
        compiler_params=_params(("arbitrary",)),
    )(*([a for a, _, _ in rows] + list(params) + ct_flat + [add_to[i] for i in add_idx]))
    return res[:len(row_out)], res[len(row_out):]


def _sigmoid(x):
    return 0.5 * (jnp.tanh(0.5 * x) + 1.0)


def _softplus(x):
    return jnp.maximum(x, 0.0) + jnp.log(1.0 + jnp.exp(-jnp.abs(x)))


def _rms(x, g):
    return x * lax.rsqrt(jnp.mean(x * x, axis=-1, keepdims=True) + RMS_EPS) * g


def _segsum_impl(x):
    n = x.shape[-1]
    r = lax.shift_right_logical(lax.broadcasted_iota(jnp.int32, (n, n), 0), 6)
    c = lax.shift_right_logical(lax.broadcasted_iota(jnp.int32, (n, n), 1), 6)
    bd = (r == c).astype(BF16)
    hi = x.astype(BF16)
    rest = x - hi.astype(F32)
    mid = rest.astype(BF16)
    lo = (rest - mid.astype(F32)).astype(BF16)
    nn = ((1,), (0,))
    return _dot(hi, bd, nn) + _dot(mid, bd, nn) + _dot(lo, bd, nn)


@jax.custom_vjp
def _segsum(x):
    return _segsum_impl(x)


_segsum.defvjp(lambda x: (_segsum_impl(x), None), lambda _, g: (_segsum_impl(g),))


@jax.custom_vjp
def _mmb(a, w):
    return _dot(a.astype(BF16), w.astype(BF16), ((1,), (0,)))


def _mmb_fwd(a, w):
    return _mmb(a, w), (a, w)


def _mmb_bwd(res, g):
    a, w = res
    gb = g.astype(BF16)
    return _dot(gb, w.astype(BF16), ((1,), (1,))), _dot(a.astype(BF16), gb, ((0,), (0,)))


_mmb.defvjp(_mmb_fwd, _mmb_bwd)


def _f_norm(x, g):
    return (_rms(x, g),)


def _f_post1(x, u, g2, g3):
    x1 = x + _rms(u, g2)
    return x1, _rms(x1, g3)


def _f_swiglu(ag, au):
    return (ag * _sigmoid(ag) * au,)


def _f_merge(pg1, pg2, m1, m2, b1, b2):
    return (_sigmoid(pg1 + b1) * m1 + _sigmoid(pg2 + b2) * m2,)


def _f_out(x1, f, g4):
    return (x1 + _rms(f, g4),)


def _f_rwpre(pr, pk, pv, pz, qr, qk, qv, qz, mur, muk, muv, muz, w0, wup, a0, aup, gup, k_k, k_a):
    r = pr + (qr - pr) * mur
    k = pk + (qk - pk) * muk
    v = pv + (qv - pv) * muv
    z = pz + (qz - pz) * muz
    w_raw = w0 + _mmb(jnp.tanh(z), wup)
    lw = -jnp.exp(-_softplus(-w_raw) - 0.5)
    a = _sigmoid(a0 + _mmb(z, aup))
    g = _mmb(_sigmoid(z), gup)
    kk = k * k_k
    kap = kk * lax.rsqrt(jnp.maximum(_segsum(kk * kk), 1e-24))
    k2 = k * (1.0 + (a - 1.0) * k_a)
    return r, lw, k2, v, kap, a, g


def _f_rwpost(y, r, k2, v, g, lnx_w, lnx_b, r_k):
    inv = 1.0 / HEAD_DIM
    yc = y - _segsum(y) * inv
    var = _segsum(yc * yc) * inv
    yn = yc * lax.rsqrt(var + GN_EPS) * lnx_w + lnx_b
    bonus = _segsum(r * k2 * r_k) * v
    return ((yn + bonus) * g,)


RW_GROUPS = (0, 512, 1024, 1536, RW_COLS)


def _column_groups(p):
    return [p[:, a:b] for a, b in zip(RW_GROUPS[:-1], RW_GROUPS[1:])]


def _previous_tokens(p, halo, first_of_sequence):
    rows = lax.broadcasted_iota(jnp.int32, (p.shape[0], 1), 0)
    before = jnp.where(first_of_sequence, 0.0, halo[SUBLANES - 1:SUBLANES, :])
    return jnp.where(rows == 0, before, pltpu.roll(p, 1, axis=0))


def _halo_spec(tile, order):
    per = tile // SUBLANES
    return pl.BlockSpec((SUBLANES, RW_COLS), lambda i: (jnp.maximum(order(i) * per - 1, 0), 0))


def _rw_pre(p_rw, params, S, tile=128):
    T = p_rw.shape[0]
    tile = min(tile, T)
    assert S % tile == 0
    n_p = len(params)

    def body(*refs):
        p_ref, halo_ref = refs[0], refs[1]
        par = [x[...].astype(F32) for x in refs[2:2 + n_p]]
        p = p_ref[...]
        first = lax.rem(pl.program_id(0) * tile, S) == 0
        prev = _previous_tokens(p, halo_ref[...], first)
        for o_ref, val in zip(refs[2 + n_p:], _f_rwpre(*_column_groups(p), *_column_groups(prev), *par)):
            o_ref[...] = val

    out_spec = pl.BlockSpec((tile, RW_WIDTH), lambda i: (i, 0))
    return pl.pallas_call(
        body, name="rw_pre",
        grid=(T // tile,),
        in_specs=[pl.BlockSpec((tile, RW_COLS), lambda i: (i, 0)), _halo_spec(tile, lambda i: i)]
                 + [_full_spec(q.shape) for q in params],
        out_specs=[out_spec] * 7,
        out_shape=[jax.ShapeDtypeStruct((T, RW_WIDTH), F32)] * 7,
        compiler_params=_params(("parallel",)),
    )(p_rw, p_rw, *params)


def _rw_pre_bwd(p_rw, params, cts, S, tile=128):
    T = p_rw.shape[0]
    tile = min(tile, T)
    assert S % tile == 0
    nt = T // tile
    n_p = len(params)
    ct_flat = [c for group in cts for c in group]
    ct_sizes = [len(group) for group in cts]
    n_ct = len(ct_flat)

    def body(*refs):
        p_ref, halo_ref = refs[0], refs[1]
        par = [x[...].astype(F32) for x in refs[2:2 + n_p]]
        ct_vals = [x[...] for x in refs[2 + n_p:2 + n_p + n_ct]]
        dp_ref = refs[2 + n_p + n_ct]
        dpar_refs = refs[3 + n_p + n_ct:3 + 2 * n_p + n_ct]
        carry = refs[-1]
        step = pl.program_id(0)

        @pl.when(step == 0)
        def _():
            carry[...] = jnp.zeros_like(carry)
            for ref in dpar_refs:
                ref[...] = jnp.zeros_like(ref)

        ct_in, q = [], 0
        for n in ct_sizes:
            ct_in.append(functools.reduce(lambda u, v: u + v, ct_vals[q:q + n]))
            q += n
        p = p_ref[...]
        first = lax.rem((nt - 1 - step) * tile, S) == 0
        prev = _previous_tokens(p, halo_ref[...], first)
        _, vjp = jax.vjp(_f_rwpre, *_column_groups(p), *_column_groups(prev), *par)
        grads = vjp(tuple(ct_in))
        d_here = jnp.concatenate(grads[0:4], axis=1)
        d_prev = jnp.concatenate(grads[4:8], axis=1)
        rows = lax.broadcasted_iota(jnp.int32, (tile, 1), 0)
        from_next = jnp.where(rows == tile - 1, carry[0:1, :], pltpu.roll(d_prev, tile - 1, axis=0))
        dp_ref[...] = (d_here + from_next).astype(dp_ref.dtype)
        carry[...] = jnp.broadcast_to(jnp.where(first, 0.0, d_prev[0:1, :]), carry.shape)
        for ref, g in zip(dpar_refs, grads[8:]):
            ref[...] += g

    back = lambda i: nt - 1 - i
    row = lambda w: pl.BlockSpec((tile, w), lambda i: (back(i), 0))
    res = pl.pallas_call(
        body, name="rw_pre_bwd",
        grid=(nt,),
        in_specs=[row(RW_COLS), _halo_spec(tile, back)] + [_full_spec(q.shape) for q in params]
                 + [row(RW_WIDTH)] * n_ct,
        out_specs=[row(RW_COLS)] + [_full_spec(q.shape) for q in params],
        out_shape=[jax.ShapeDtypeStruct((T, RW_COLS), BF16)] + [jax.ShapeDtypeStruct(q.shape, F32) for q in params],
        scratch_shapes=[pltpu.VMEM((SUBLANES, RW_COLS), F32)],
        compiler_params=_params(("arbitrary",)),
    )(p_rw, p_rw, *params, *ct_flat)
    return res[0], res[1:]


def _loss_head(x1, f, target, g4, tile=256):
    T, D = x1.shape
    tile = min(tile, T)

    def body(x1_ref, f_ref, t_ref, g_ref, loss_ref, dx1_ref, df_ref, dg_ref):
        (y,), vjp = jax.vjp(_f_out, x1_ref[...], f_ref[...], g_ref[...])
        err = y - t_ref[...]
        dx1, df, dg = vjp((err * (1.0 / D),))
        dx1_ref[...] = dx1
        df_ref[...] = df.astype(df_ref.dtype)

        @pl.when(pl.program_id(0) == 0)
        def _():
            loss_ref[...] = jnp.zeros_like(loss_ref)
            dg_ref[...] = jnp.zeros_like(dg_ref)

        part = jnp.sum(jnp.sum(err * err, axis=1, keepdims=True), axis=0, keepdims=True) * (0.5 / D)
        loss_ref[...] += jnp.broadcast_to(part, loss_ref.shape)
        dg_ref[...] += dg

    row = pl.BlockSpec((tile, D), lambda i: (i, 0))
    return pl.pallas_call(
        body, name="loss_head",
        grid=(T // tile,),
        in_specs=[row, row, row, _full_spec(g4.shape)],
        out_specs=[_full_spec((SUBLANES, LANES)), row, row, _full_spec(g4.shape)],
        out_shape=[jax.ShapeDtypeStruct((SUBLANES, LANES), F32), jax.ShapeDtypeStruct((T, D), F32),
                   jax.ShapeDtypeStruct((T, D), BF16), jax.ShapeDtypeStruct(g4.shape, F32)],
        compiler_params=_params(("arbitrary",)),
    )(x1, f, target, g4)


def _nn(a, b):
    return _dot(a, b, ((1,), (0,)))


def _nt(a, b):
    return _dot(a, b, ((1,), (1,)))


def _tn(a, b):
    return _dot(a, b, ((0,), (0,)))


def _split_dot(x, u2):
    hi = x.astype(BF16)
    lo = (x - hi.astype(F32)).astype(BF16)
    return _nn(jnp.concatenate([hi, lo], axis=1), u2)


def _by_head(x, masks):
    return jnp.concatenate([(x * m).astype(BF16) for m in masks], axis=0)


def _fold_heads(x2, masks):
    R = x2.shape[0] // len(masks)
    return functools.reduce(lambda u, v: u + v, [x2[h * R:(h + 1) * R] * m for h, m in enumerate(masks)])


def _head_masks():
    lane = lax.broadcasted_iota(jnp.int32, (1, LANES), 1)
    return [((lane >= h * HEAD_DIM) & (lane < (h + 1) * HEAD_DIM)).astype(F32) for h in range(LANES // HEAD_DIM)]


def _key_tri(op):
    row = lax.broadcasted_iota(jnp.int32, (ATTN_KEYS, ATTN_KEYS), 0)
    col = lax.broadcasted_iota(jnp.int32, (ATTN_KEYS, ATTN_KEYS), 1)
    u = op(row, col).astype(BF16)
    return jnp.concatenate([u, u], axis=0)


def _causal(qb, sub):
    row = lax.broadcasted_iota(jnp.int32, (qb, ATTN_KEYS), 0)
    col = lax.broadcasted_iota(jnp.int32, (qb, ATTN_KEYS), 1)
    return col + sub * ATTN_KEYS < row


def _sb_weights(qb16, kbh, c_fails, u_gt, strict, scale):
    z_all = _nt(qb16, kbh) * scale
    zs = [z_all[:, h * ATTN_KEYS:(h + 1) * ATTN_KEYS] for h in range(len(c_fails))]
    Ls = [jnp.minimum(-z, 0.0) - jnp.log(1.0 + jnp.exp(-jnp.abs(z))) for z in zs]
    Lms = Ls if strict is None else [jnp.where(strict, L, 0.0) for L in Ls]
    cums = [_split_dot(Lm, u_gt) for Lm in Lms]
    As = [jnp.exp(z + L + c + cum) for z, L, c, cum in zip(zs, Ls, c_fails, cums)]
    if strict is not None:
        As = [jnp.where(strict, A, 0.0) for A in As]
    return zs, Ls, Lms, As


def _attn_specs(S, qb):
    nq = S // qb
    q_spec = pl.BlockSpec((qb, LANES), lambda b, p, i: (b * nq + i, p))
    k_spec = pl.BlockSpec((S, LANES), lambda b, p, i: (b, SB_WIDTH // LANES + p))
    v_spec = pl.BlockSpec((S, LANES), lambda b, p, i: (b, 2 * SB_WIDTH // LANES + p))
    seq = pl.BlockSpec((S, LANES), lambda b, p, i: (b, p))
    return q_spec, k_spec, v_spec, q_spec, seq


def _key_walk(i, qb, block, carry):
    per = qb // ATTN_KEYS
    for sub in reversed(range(per)):
        carry = block(i * per + sub, carry, _causal(qb, sub))
    return lax.fori_loop(0, i * per, lambda j, c: block(i * per - 1 - j, c, None), carry)


def _attn_fwd(proj, B, S):
    qb = min(ATTN_QUERIES, S)
    scale = HEAD_DIM ** -0.5

    def body(q_ref, k_ref, v_ref, o_ref):
        i = pl.program_id(2)
        masks = _head_masks()
        u_gt = _key_tri(lambda r, c: r > c)
        q16 = q_ref[...].astype(BF16)

        def block(J, carry, strict_mask):
            acc, cs = carry
            r0 = pl.multiple_of(J * ATTN_KEYS, ATTN_KEYS)
            kbh = _by_head(k_ref[pl.ds(r0, ATTN_KEYS), :], masks)
            vbh = _by_head(v_ref[pl.ds(r0, ATTN_KEYS), :], masks)
            _, _, Lms, As = _sb_weights(q16, kbh, cs, u_gt, strict_mask, scale)
            acc = acc + _nn(jnp.concatenate([A.astype(BF16) for A in As], axis=1), vbh)
            return acc, tuple(c + jnp.sum(Lm, axis=1, keepdims=True) for c, Lm in zip(cs, Lms))

        zero_c = tuple(jnp.zeros((qb, 1), F32) for _ in masks)
        carry = _key_walk(i, qb, block, (jnp.zeros((qb, LANES), F32), zero_c))
        o_ref[...] = carry[0]

    q_spec, k_spec, v_spec, blk, _ = _attn_specs(S, qb)
    return pl.pallas_call(
        body, name="sb_attn_fwd",
        grid=(B, SB_WIDTH // LANES, S // qb),
        in_specs=[q_spec, k_spec, v_spec],
        out_specs=blk,
        out_shape=jax.ShapeDtypeStruct((B * S, SB_WIDTH), F32),
        compiler_params=_params(("parallel", "parallel", "arbitrary")),
    )(proj, proj, proj)


def _attn_bwd(proj, o, do, B, S):
    qb = min(ATTN_QUERIES, S)
    nq = S // qb
    scale = HEAD_DIM ** -0.5

    def body(q_ref, k_ref, v_ref, o_ref, do_ref, dq_ref, dk_out, dv_out, dk_ref, dv_ref):
        i = pl.program_id(2)

        @pl.when(i == 0)
        def _():
            dk_ref[...] = jnp.zeros_like(dk_ref)
            dv_ref[...] = jnp.zeros_like(dv_ref)

        masks = _head_masks()
        u_gt = _key_tri(lambda r, c: r > c)
        u_ge = _key_tri(lambda r, c: r >= c)
        q16 = q_ref[...].astype(BF16)
        do_b = do_ref[...].astype(BF16)
        od = o_ref[...] * do_b.astype(F32)
        totals = [jnp.sum(od * m, axis=1, keepdims=True) for m in masks]

        def block(J, carry, strict_mask):
            dq, c_fail, c_p = carry
            r0 = pl.multiple_of(J * ATTN_KEYS, ATTN_KEYS)
            kbh = _by_head(k_ref[pl.ds(r0, ATTN_KEYS), :], masks)
            vbh = _by_head(v_ref[pl.ds(r0, ATTN_KEYS), :], masks)
            heads = range(len(masks))
            zs, Ls, Lms, As = _sb_weights(q16, kbh, c_fail, u_gt, strict_mask, scale)
            Abs = [A.astype(BF16) for A in As]
            dA_all = _nt(do_b, vbh)
            Ps = [Abs[h].astype(F32) * dA_all[:, h * ATTN_KEYS:(h + 1) * ATTN_KEYS] for h in heads]
            afters = [c_p[h] + _split_dot(Ps[h], u_ge) for h in heads]
            sigs = [jnp.exp(zs[h] + Ls[h]) for h in heads]
            dzs = [(Ps[h] * (1.0 - sigs[h]) - sigs[h] * (totals[h] - afters[h])) * scale for h in heads]
            if strict_mask is not None:
                dzs = [jnp.where(strict_mask, dz, 0.0) for dz in dzs]
            dz_all = jnp.concatenate([dz.astype(BF16) for dz in dzs], axis=1)
            dv_blk = _fold_heads(_tn(jnp.concatenate(Abs, axis=1), do_b), masks)
            dk_blk = _fold_heads(_tn(dz_all, q16), masks)
            dq = dq + _nn(dz_all, kbh)
            dk_ref[pl.ds(r0, ATTN_KEYS), :] += dk_blk
            dv_ref[pl.ds(r0, ATTN_KEYS), :] += dv_blk
            new_fail = tuple(c_fail[h] + jnp.sum(Lms[h], axis=1, keepdims=True) for h in heads)
            new_p = tuple(c_p[h] + jnp.sum(Ps[h], axis=1, keepdims=True) for h in heads)
            return dq, new_fail, new_p

        zc = tuple(jnp.zeros((qb, 1), F32) for _ in masks)
        carry = _key_walk(i, qb, block, (jnp.zeros((qb, LANES), F32), zc, zc))
        dq_ref[...] = carry[0].astype(dq_ref.dtype)

        @pl.when(i == nq - 1)
        def _():
            dk_out[...] = dk_ref[...].astype(dk_out.dtype)
            dv_out[...] = dv_ref[...].astype(dv_out.dtype)

    q_spec, k_spec, v_spec, blk, seq = _attn_specs(S, qb)
    return pl.pallas_call(
        body, name="sb_attn_bwd",
        grid=(B, SB_WIDTH // LANES, nq),
        in_specs=[q_spec, k_spec, v_spec, blk, blk],
        out_specs=[blk, seq, seq],
        out_shape=[jax.ShapeDtypeStruct((B * S, SB_WIDTH), BF16)] * 3,
        scratch_shapes=[pltpu.VMEM((S, LANES), F32), pltpu.VMEM((S, LANES), F32)],
        compiler_params=_params(("parallel", "parallel", "arbitrary")),
    )(proj, proj, proj, o, do)


_BATCHED = {"nn": "gmk,gkn->gmn", "nt": "gmk,gnk->gmn", "tn": "gkm,gkn->gmn"}


def _bdot_raw(a, b, kind, passes):
    e = functools.partial(jnp.einsum, _BATCHED[kind], preferred_element_type=F32)
    ah, bh = a.astype(BF16), b.astype(BF16)
    if passes == 1:
        return e(ah, bh)
    al, bl = (a - ah.astype(F32)).astype(BF16), (b - bh.astype(F32)).astype(BF16)
    return e(ah, bh) + e(ah, bl) + e(al, bh)


@functools.partial(jax.custom_vjp, nondiff_argnums=(2, 3))
def _bdot(a, b, kind, passes):
    return _bdot_raw(a, b, kind, passes)


def _bdot_fwd(a, b, kind, passes):
    return _bdot_raw(a, b, kind, passes), (a, b)


def _bdot_bwd(kind, passes, res, g):
    a, b = res
    if kind == "nn":
        return _bdot_raw(g, b, "nt", passes), _bdot_raw(a, g, "tn", passes)
    if kind == "nt":
        return _bdot_raw(g, b, "nn", passes), _bdot_raw(g, a, "tn", passes)
    return _bdot_raw(b, g, "nt", passes), _bdot_raw(a, g, "nn", passes)


_bdot.defvjp(_bdot_fwd, _bdot_bwd)


def _wkv_chunk(S0, r, lw, k, v, kap, a):
    G, C, N = r.shape
    row = lax.broadcasted_iota(jnp.int32, (C, C), 0)
    col = lax.broadcasted_iota(jnp.int32, (C, C), 1)
    incl = (col <= row).astype(F32)
    strict = (col < row).astype(F32)
    cum = _bdot(jnp.broadcast_to(incl, (G, C, C)), lw, "nn", 3)
    e_pos = jnp.exp(cum)
    e_neg = jnp.exp(-cum)
    al = -kap * jnp.exp(cum - lw)
    be = kap * a * e_neg
    kt = k * e_neg
    rt = r * e_pos
    bk = jnp.concatenate([be, kt], axis=1)
    mask = jnp.concatenate([jnp.concatenate([strict, strict], axis=1), jnp.concatenate([incl, incl], axis=1)], axis=0)
    m_all = _bdot(jnp.concatenate([al, rt], axis=1), bk, "nt", 3) * mask
    m_ab, m_ak = m_all[:, :C, :C], m_all[:, :C, C:]
    m_rb, m_rk = m_all[:, C:, :C], m_all[:, C:, C:]
    S0t = jnp.swapaxes(S0, 1, 2)
    sa = _bdot(jnp.concatenate([al, m_ak], axis=2), jnp.concatenate([S0t, v], axis=1), "nn", 3)
    p = m_ab
    steps = max(1, (C - 1).bit_length())
    for j in range(steps):
        sa = sa + _bdot(p, sa, "nn", 1)
        if j + 1 < steps:
            p = _bdot(p, p, "nn", 1)
    y = _bdot(jnp.concatenate([rt, m_rb, m_rk], axis=2), jnp.concatenate([S0t, sa, v], axis=1), "nn", 3)
    S1 = (S0 + _bdot(jnp.concatenate([sa, v], axis=1), bk, "tn", 1)) * e_pos[:, C - 1:C, :]
    return y, S1


def _split_heads(x):
    return jnp.stack([x[:, h * HEAD_DIM:(h + 1) * HEAD_DIM] for h in range(x.shape[1] // HEAD_DIM)], axis=0)


def _merge_heads(x):
    return jnp.concatenate([x[h] for h in range(x.shape[0])], axis=1)


def _seq_heads(ref):
    return jnp.concatenate([_split_heads(ref[s]) for s in range(ref.shape[0])], axis=0)


def _store_seq_heads(ref, x):
    heads = x.shape[0] // ref.shape[0]
    for s in range(ref.shape[0]):
        ref[s] = _merge_heads(x[s * heads:(s + 1) * heads])


def _wkv_fwd(r, lw, k, v, kap, a, B, S):
    C, H, N = WKV_CHUNK, RW_WIDTH // HEAD_DIM, HEAD_DIM
    nc = S // C
    Q = min(WKV_SEQS, B)

    def body(r_ref, lw_ref, k_ref, v_ref, kap_ref, a_ref, y_ref, st_ref, s_scr):
        @pl.when(pl.program_id(1) == 0)
        def _():
            s_scr[...] = jnp.zeros_like(s_scr)

        S0 = s_scr[...]
        for s in range(Q):
            st_ref[s, 0] = S0[s * H:(s + 1) * H]
        args = [_seq_heads(ref) for ref in (r_ref, lw_ref, k_ref, v_ref, kap_ref, a_ref)]
        y, S1 = _wkv_chunk(S0, *args)
        s_scr[...] = S1
        _store_seq_heads(y_ref, y)

    row_spec = pl.BlockSpec((Q, C, RW_WIDTH), lambda b, c: (b, c, 0))
    seqs = lambda t: t.reshape(B, S, RW_WIDTH)
    y, states = pl.pallas_call(
        body, name="wkv_fwd",
        grid=(B // Q, nc),
        in_specs=[row_spec] * 6,
        out_specs=[row_spec, pl.BlockSpec((Q, 1, H, N, N), lambda b, c: (b, c, 0, 0, 0))],
        out_shape=[jax.ShapeDtypeStruct((B, S, RW_WIDTH), F32), jax.ShapeDtypeStruct((B, nc, H, N, N), F32)],
        scratch_shapes=[pltpu.VMEM((Q * H, N, N), F32)],
        compiler_params=_params(("arbitrary", "arbitrary")),
    )(*map(seqs, (r, lw, k, v, kap, a)))
    return y.reshape(B * S, RW_WIDTH), states


def _wkv_bwd(r, lw, k, v, kap, a, states, dy, B, S):
    C, H, N = WKV_CHUNK, RW_WIDTH // HEAD_DIM, HEAD_DIM
    nc = S // C
    Q = min(WKV_SEQS, B)

    def body(r_ref, lw_ref, k_ref, v_ref, kap_ref, a_ref, st_ref, dy_ref,
             dr_ref, dlw_ref, dk_ref, dv_ref, dkap_ref, da_ref, ds_scr):
        @pl.when(pl.program_id(1) == 0)
        def _():
            ds_scr[...] = jnp.zeros_like(ds_scr)

        args = [_seq_heads(ref) for ref in (r_ref, lw_ref, k_ref, v_ref, kap_ref, a_ref)]
        S0 = jnp.concatenate([st_ref[s, 0] for s in range(Q)], axis=0)
        _, vjp = jax.vjp(_wkv_chunk, S0, *args)
        g = vjp((_seq_heads(dy_ref), ds_scr[...]))
        ds_scr[...] = g[0]
        for ref, gv in zip((dr_ref, dlw_ref, dk_ref, dv_ref, dkap_ref, da_ref), g[1:]):
            _store_seq_heads(ref, gv)

    row_spec = pl.BlockSpec((Q, C, RW_WIDTH), lambda b, c: (b, nc - 1 - c, 0))
    st_spec = pl.BlockSpec((Q, 1, H, N, N), lambda b, c: (b, nc - 1 - c, 0, 0, 0))
    seqs = lambda t: t.reshape(B, S, RW_WIDTH)
    res = pl.pallas_call(
        body, name="wkv_bwd",
        grid=(B // Q, nc),
        in_specs=[row_spec] * 6 + [st_spec, row_spec],
        out_specs=[row_spec] * 6,
        out_shape=[jax.ShapeDtypeStruct((B, S, RW_WIDTH), F32)] * 6,
        scratch_shapes=[pltpu.VMEM((Q * H, N, N), F32)],
        compiler_params=_params(("arbitrary", "arbitrary")),
    )(*map(seqs, (r, lw, k, v, kap, a)), states, seqs(dy))
    return [t.reshape(B * S, RW_WIDTH) for t in res]


HBM = pl.BlockSpec(memory_space=pl.ANY)


def _place():
    return lax.axis_index("x"), lax.axis_index("y"), lax.axis_index("c")


def _other_chips(x, y):
    return [(1 - x, y), (x, 1 - y), (1 - x, 1 - y)]


def _all_gather_chips(shards):
    n = len(shards)

    def body(*refs):
        ins, outs = refs[:n], refs[n:2 * n]
        ici_send, ici_recv, d2d_send, d2d_recv, local = refs[2 * n:]
        x, y, c = _place()
        me = 2 * x + y
        sib = (x, y, 1 - c)
        chips = _other_chips(x, y)
        started, copies = [], []
        for w in range(n):
            cp = pltpu.make_async_copy(ins[w].at[c], outs[w].at[me, c], local.at[w])
            cp.start()
            copies.append(cp)
            for j, (px, py) in enumerate(chips):
                rd = pltpu.make_async_remote_copy(
                    src_ref=ins[w].at[c], dst_ref=outs[w].at[me, c], send_sem=ici_send.at[3 * w + j],
                    recv_sem=ici_recv.at[3 * w + j], device_id=(px, py, c), device_id_type=MESH)
                rd.start()
                started.append(rd)
            rd = pltpu.make_async_remote_copy(
                src_ref=ins[w].at[c], dst_ref=outs[w].at[me, c], send_sem=d2d_send.at[4 * w + 3],
                recv_sem=d2d_recv.at[4 * w + 3], device_id=sib, device_id_type=MESH)
            rd.start()
            started.append(rd)
        for w in range(n):
            for j, (px, py) in enumerate(chips):
                src = 2 * px + py
                pltpu.make_async_remote_copy(
                    src_ref=ins[w].at[c], dst_ref=outs[w].at[src, c], send_sem=ici_send.at[3 * w + j],
                    recv_sem=ici_recv.at[3 * w + j], device_id=(px, py, c), device_id_type=MESH).wait_recv()
                rd = pltpu.make_async_remote_copy(
                    src_ref=outs[w].at[src, c], dst_ref=outs[w].at[src, c], send_sem=d2d_send.at[4 * w + j],
                    recv_sem=d2d_recv.at[4 * w + j], device_id=sib, device_id_type=MESH)
                rd.start()
                started.append(rd)
        for w in range(n):
            for j, (px, py) in enumerate(chips):
                pltpu.make_async_remote_copy(
                    src_ref=ins[w].at[c], dst_ref=outs[w].at[2 * px + py, 1 - c], send_sem=d2d_send.at[4 * w + j],
                    recv_sem=d2d_recv.at[4 * w + j], device_id=sib, device_id_type=MESH).wait_recv()
            pltpu.make_async_remote_copy(
                src_ref=ins[w].at[c], dst_ref=outs[w].at[me, 1 - c], send_sem=d2d_send.at[4 * w + 3],
                recv_sem=d2d_recv.at[4 * w + 3], device_id=sib, device_id_type=MESH).wait_recv()
        for rd in started:
            rd.wait_send()
        for cp in copies:
            cp.wait()

    return pl.pallas_call(
        body, name="gather_weights",
        in_specs=[HBM] * n, out_specs=[HBM] * n,
        out_shape=[jax.ShapeDtypeStruct((N_CHIPS,) + s.shape, s.dtype) for s in shards],
        scratch_shapes=[pltpu.SemaphoreType.DMA((3 * n,)), pltpu.SemaphoreType.DMA((3 * n,)),
                        pltpu.SemaphoreType.DMA((4 * n,)), pltpu.SemaphoreType.DMA((4 * n,)),
                        pltpu.SemaphoreType.DMA((n,))],
        compiler_params=pltpu.CompilerParams(has_side_effects=True),
    )(*shards)


def _pair_split(grads):
    n = len(grads)

    def body(*refs):
        ins, theirs = refs[:n], refs[n:2 * n]
        send, recv = refs[2 * n:]
        x, y, c = _place()
        sib = (x, y, 1 - c)
        rds = []
        for w in range(n):
            rd = pltpu.make_async_remote_copy(
                src_ref=ins[w].at[:, 1 - c], dst_ref=theirs[w], send_sem=send.at[w], recv_sem=recv.at[w],
                device_id=sib, device_id_type=MESH)
            rd.start()
            rds.append(rd)
        for rd in rds:
            rd.wait_recv()
        for rd in rds:
            rd.wait_send()

    return pl.pallas_call(
        body, name="grad_pair_split",
        in_specs=[HBM] * n, out_specs=[HBM] * n,
        out_shape=[jax.ShapeDtypeStruct((g.shape[0],) + g.shape[2:], g.dtype) for g in grads],
        scratch_shapes=[pltpu.SemaphoreType.DMA((n,)), pltpu.SemaphoreType.DMA((n,))],
        compiler_params=pltpu.CompilerParams(has_side_effects=True),
    )(*grads)


def _chip_scatter(parts):
    n = len(parts)

    def body(*refs):
        ins, outs = refs[:n], refs[n:2 * n]
        send, recv = refs[2 * n:]
        x, y, c = _place()
        me = 2 * x + y
        rds = []
        for w in range(n):
            for j, (px, py) in enumerate(_other_chips(x, y)):
                s = 3 * w + j
                rd = pltpu.make_async_remote_copy(
                    src_ref=ins[w].at[2 * px + py], dst_ref=outs[w].at[j], send_sem=send.at[s], recv_sem=recv.at[s],
                    device_id=(px, py, c), device_id_type=MESH)
                rd.start()
                rds.append(rd)
        for w in range(n):
            for j, (px, py) in enumerate(_other_chips(x, y)):
                s = 3 * w + j
                pltpu.make_async_remote_copy(
                    src_ref=ins[w].at[me], dst_ref=outs[w].at[j], send_sem=send.at[s], recv_sem=recv.at[s],
                    device_id=(px, py, c), device_id_type=MESH).wait_recv()
        for rd in rds:
            rd.wait_send()

    return pl.pallas_call(
        body, name="grad_chip_scatter",
        in_specs=[HBM] * n, out_specs=[HBM] * n,
        out_shape=[jax.ShapeDtypeStruct((N_CHIPS - 1,) + p.shape[1:], p.dtype) for p in parts],
        scratch_shapes=[pltpu.SemaphoreType.DMA((3 * n,)), pltpu.SemaphoreType.DMA((3 * n,))],
        compiler_params=pltpu.CompilerParams(has_side_effects=True),
    )(*parts)


def _pair_join(bufs):
    n = len(bufs)

    def body(*refs):
        ins, outs = refs[:n], refs[n:2 * n]
        send, recv = refs[2 * n:]
        x, y, c = _place()
        sib = (x, y, 1 - c)
        rds = []
        for w in range(n):
            rd = pltpu.make_async_remote_copy(
                src_ref=ins[w].at[c], dst_ref=outs[w].at[c], send_sem=send.at[w], recv_sem=recv.at[w],
                device_id=sib, device_id_type=MESH)
            rd.start()
            rds.append(rd)
        for w in range(n):
            pltpu.make_async_remote_copy(
                src_ref=ins[w].at[c], dst_ref=outs[w].at[1 - c], send_sem=send.at[w], recv_sem=recv.at[w],
                device_id=sib, device_id_type=MESH).wait_recv()
        for rd in rds:
            rd.wait_send()

    return pl.pallas_call(
        body, name="grad_pair_join",
        in_specs=[HBM] * n, out_specs=[HBM] * n,
        out_shape=[jax.ShapeDtypeStruct(b.shape, b.dtype) for b in bufs],
        input_output_aliases={w: w for w in range(n)},
        scratch_shapes=[pltpu.SemaphoreType.DMA((n,)), pltpu.SemaphoreType.DMA((n,))],
        compiler_params=pltpu.CompilerParams(has_side_effects=True),
    )(*bufs)


def _all_reduce_small(packed):
    R = packed.shape[0]

    def body(x_ref, o_ref, buf, send, recv):
        x, y, c = _place()
        me = 4 * x + 2 * y + c
        buf[me] = x_ref[...]
        rds = []
        for rel in range(1, N_DEV):
            fx, fy, fc = (rel >> 2) & 1, (rel >> 1) & 1, rel & 1
            peer = (1 - x if fx else x, 1 - y if fy else y, 1 - c if fc else c)
            rd = pltpu.make_async_remote_copy(
                src_ref=x_ref, dst_ref=buf.at[me], send_sem=send.at[rel - 1], recv_sem=recv.at[rel - 1],
                device_id=peer, device_id_type=MESH)
            rd.start()
            rds.append((rd, peer))
        for rel in range(1, N_DEV):
            rd, (px, py, pc) = rds[rel - 1]
            pltpu.make_async_remote_copy(
                src_ref=x_ref, dst_ref=buf.at[4 * px + 2 * py + pc], send_sem=send.at[rel - 1], recv_sem=recv.at[rel - 1],
                device_id=(px, py, pc), device_id_type=MESH).wait_recv()
        for rd, _ in rds:
            rd.wait_send()
        total = buf[0]
        for d in range(1, N_DEV):
            total = total + buf[d]
        o_ref[...] = total

    return pl.pallas_call(
        body, name="all_reduce_small",
        in_specs=[pl.BlockSpec(memory_space=pltpu.VMEM)],
        out_specs=pl.BlockSpec(memory_space=pltpu.VMEM),
        out_shape=jax.ShapeDtypeStruct(packed.shape, F32),
        scratch_shapes=[pltpu.VMEM((N_DEV, R, LANES), F32), pltpu.SemaphoreType.DMA((N_DEV - 1,)),
                        pltpu.SemaphoreType.DMA((N_DEV - 1,))],
        compiler_params=pltpu.CompilerParams(has_side_effects=True),
    )(packed)


def _pair_sum(name, split, theirs, core):
    n_chip, _, Rh, C = split.shape
    tile = _div_tile(Rh, 256, 2 * SUBLANES)
    nt = Rh // tile

    def body(core_ref, a_ref, b_ref, o_ref):
        o_ref[...] = (a_ref[...] + b_ref[...]).astype(o_ref.dtype)

    return pl.pallas_call(
        body, name=name,
        grid_spec=pltpu.PrefetchScalarGridSpec(
            num_scalar_prefetch=1,
            grid=(n_chip, nt),
            in_specs=[pl.BlockSpec((None, None, tile, C), lambda j, i, core_ref: (j, core_ref[0], i, 0)),
                      pl.BlockSpec((None, tile, C), lambda j, i, core_ref: (j, i, 0))],
            out_specs=pl.BlockSpec((None, tile, C), lambda j, i, core_ref: (j, i, 0)),
        ),
        out_shape=jax.ShapeDtypeStruct((n_chip, Rh, C), BF16),
        compiler_params=_params(("parallel", "parallel")),
    )(core, split, theirs)


def _chip_sum(name, own, landed, core):
    n_in, Rh, C = landed.shape
    tile = _div_tile(Rh, 256, 2 * SUBLANES)

    def body(core_ref, *refs):
        total = refs[0][...].astype(F32)
        for ref in refs[1:n_in + 1]:
            total = total + ref[...].astype(F32)
        refs[n_in + 1][...] = total

    slot = lambda j: pl.BlockSpec((None, tile, C), lambda i, core_ref: (j, i, 0))
    return pl.pallas_call(
        body, name=name,
        grid_spec=pltpu.PrefetchScalarGridSpec(
            num_scalar_prefetch=1,
            grid=(Rh // tile,),
            in_specs=[pl.BlockSpec((None, tile, C), lambda i, core_ref: (core_ref[1], i, 0))]
                     + [slot(j) for j in range(n_in)],
            out_specs=pl.BlockSpec((None, tile, C), lambda i, core_ref: (core_ref[0], i, 0)),
        ),
        out_shape=jax.ShapeDtypeStruct((2, Rh, C), F32),
        compiler_params=_params(("parallel",)),
    )(core, own, *([landed] * n_in))


def _adamw(name, w, g, m, v):
    R, C = w.shape
    tile = _div_tile(R, 256, SUBLANES)
    c1 = 1.0 / (1.0 - ADAM_B1 ** ADAM_STEP)
    c2 = 1.0 / (1.0 - ADAM_B2 ** ADAM_STEP)

    def body(w_ref, g_ref, m_ref, v_ref, d_ref, nm_ref, nv_ref):
        g_ = g_ref[...]
        nm = ADAM_B1 * m_ref[...] + (1.0 - ADAM_B1) * g_
        nv = ADAM_B2 * v_ref[...] + (1.0 - ADAM_B2) * (g_ * g_)
        d_ref[...] = -ADAM_LR * ((nm * c1) / (jnp.sqrt(nv * c2) + ADAM_EPS) + ADAM_WD * w_ref[...])
        nm_ref[...] = nm
        nv_ref[...] = nv

    spec = pl.BlockSpec((tile, C), lambda i: (i, 0))
    return pl.pallas_call(
        body, name=name,
        grid=(R // tile,),
        in_specs=[spec] * 4, out_specs=[spec] * 3,
        out_shape=[jax.ShapeDtypeStruct((R, C), F32)] * 3,
        compiler_params=_params(("parallel",)),
    )(w, g, m, v)


def _cols_to_shards(full):
    K, N = full.shape
    return full.reshape(K, N_CHIPS, N // N_CHIPS).transpose(1, 0, 2)


def _shards_to_cols(sh):
    return sh.transpose(1, 0, 2).reshape(sh.shape[1], -1)


def _rows_to_shards(full):
    return full.reshape(N_CHIPS, full.shape[0] // N_CHIPS, full.shape[1])


SMALL = ["norm_mix_pre", "b_gate", "mu_rw", "w0", "a0", "k_k", "k_a", "r_k", "lnx_w", "lnx_b",
         "norm_mix_post", "norm_ffn_pre", "norm_ffn_post"]
BIG = ["w_in", "w_up", "a_up", "g_up", "w_sb_out", "w_rw_out", "w_o", "w_ffn_gate", "w_ffn_up", "w_ffn_down"]
ROW_SHARDED = ("w_o", "w_ffn_down")
ORDER = ["norm_mix_pre", "w_in", "b_gate", "mu_rw", "w0", "w_up", "a0", "a_up", "g_up", "k_k", "k_a", "r_k",
         "lnx_w", "lnx_b", "w_sb_out", "w_rw_out", "w_o", "norm_mix_post", "norm_ffn_pre", "w_ffn_gate",
         "w_ffn_up", "w_ffn_down", "norm_ffn_post"]


def _pack_small(vals, extra_rows=0):
    rows = jnp.concatenate([vals[n].reshape(-1, LANES) for n in SMALL], axis=0)
    pad = (-(rows.shape[0] + extra_rows)) % SUBLANES + extra_rows
    return jnp.pad(rows, ((0, pad), (0, 0)))


def _unpack_small(packed, shapes):
    out, r = {}, 0
    for n in SMALL:
        size = 1
        for s in shapes[n]:
            size *= s
        out[n] = packed[r:r + size // LANES].reshape(shapes[n])
        r += size // LANES
    return out


def kernel(x, norm_mix_pre, w_in, b_gate, mu_rw, w0, w_up, a0, a_up, g_up, k_k, k_a, r_k, lnx_w, lnx_b, w_sb_out, w_rw_out, w_o, norm_mix_post, norm_ffn_pre, w_ffn_gate, w_ffn_up, w_ffn_down, norm_ffn_post, loss_target, m_norm_mix_pre, m_w_in, m_b_gate, m_mu_rw, m_w0, m_w_up, m_a0, m_a_up, m_g_up, m_k_k, m_k_a, m_r_k, m_lnx_w, m_lnx_b, m_w_sb_out, m_w_rw_out, m_w_o, m_norm_mix_post, m_norm_ffn_pre, m_w_ffn_gate, m_w_ffn_up, m_w_ffn_down, m_norm_ffn_post, v_norm_mix_pre, v_w_in, v_b_gate, v_mu_rw, v_w0, v_w_up, v_a0, v_a_up, v_g_up, v_k_k, v_k_a, v_r_k, v_lnx_w, v_lnx_b, v_w_sb_out, v_w_rw_out, v_w_o, v_norm_mix_post, v_norm_ffn_pre, v_w_ffn_gate, v_w_ffn_up, v_w_ffn_down, v_norm_ffn_post):
    W = dict(norm_mix_pre=norm_mix_pre, w_in=w_in, b_gate=b_gate, mu_rw=mu_rw, w0=w0, w_up=w_up, a0=a0, a_up=a_up,
             g_up=g_up, k_k=k_k, k_a=k_a, r_k=r_k, lnx_w=lnx_w, lnx_b=lnx_b, w_sb_out=w_sb_out, w_rw_out=w_rw_out,
             w_o=w_o, norm_mix_post=norm_mix_post, norm_ffn_pre=norm_ffn_pre, w_ffn_gate=w_ffn_gate,
             w_ffn_up=w_ffn_up, w_ffn_down=w_ffn_down, norm_ffn_post=norm_ffn_post)
    Mo = dict(norm_mix_pre=m_norm_mix_pre, w_in=m_w_in, b_gate=m_b_gate, mu_rw=m_mu_rw, w0=m_w0, w_up=m_w_up, a0=m_a0,
              a_up=m_a_up, g_up=m_g_up, k_k=m_k_k, k_a=m_k_a, r_k=m_r_k, lnx_w=m_lnx_w, lnx_b=m_lnx_b,
              w_sb_out=m_w_sb_out, w_rw_out=m_w_rw_out, w_o=m_w_o, norm_mix_post=m_norm_mix_post,
              norm_ffn_pre=m_norm_ffn_pre, w_ffn_gate=m_w_ffn_gate, w_ffn_up=m_w_ffn_up, w_ffn_down=m_w_ffn_down,
              norm_ffn_post=m_norm_ffn_post)
    Vo = dict(norm_mix_pre=v_norm_mix_pre, w_in=v_w_in, b_gate=v_b_gate, mu_rw=v_mu_rw, w0=v_w0, w_up=v_w_up, a0=v_a0,
              a_up=v_a_up, g_up=v_g_up, k_k=v_k_k, k_a=v_k_a, r_k=v_r_k, lnx_w=v_lnx_w, lnx_b=v_lnx_b,
              w_sb_out=v_w_sb_out, w_rw_out=v_w_rw_out, w_o=v_w_o, norm_mix_post=v_norm_mix_post,
              norm_ffn_pre=v_norm_ffn_pre, w_ffn_gate=v_w_ffn_gate, w_ffn_up=v_w_ffn_up, w_ffn_down=v_w_ffn_down,
              norm_ffn_post=v_norm_ffn_post)
    shapes = {n: W[n].shape for n in ORDER}
    B, S, D = x.shape
    T = B * S
    x2 = x.reshape(T, D)
    tgt = loss_target.reshape(T, D)
    vec = {n: W[n].reshape(1, -1) for n in SMALL}

    halved = [W[n][0].astype(BF16).reshape(2, W[n].shape[1] // 2, W[n].shape[2]) for n in BIG]
    full = {}
    for n, gth in zip(BIG, _all_gather_chips(halved)):
        gth = gth.reshape((N_CHIPS,) + W[n].shape[1:])
        full[n] = gth.reshape(-1, gth.shape[2]) if n in ROW_SHARDED else _shards_to_cols(gth)
    w_sb, w_rw, w_gt = full["w_in"][:, :SB_COLS], full["w_in"][:, SB_COLS:SB_COLS + RW_COLS], full["w_in"][:, SB_COLS + RW_COLS:]
    lora_rows = {"w_up": 0, "a_up": 64, "g_up": 128}
    lora = {n: jnp.pad(full[n], ((r0, LORA_COLS - r0 - full[n].shape[0]), (0, 0))) for n, r0 in lora_rows.items()}
    mu = vec["mu_rw"]
    mu_parts = [mu[:, :512], mu[:, 512:1024], mu[:, 1024:1536], mu[:, 1536:]]
    b1, b2 = vec["b_gate"][:, :D], vec["b_gate"][:, D:]

    (h1,) = _rowwise("norm_mix_pre", _f_norm, [(x2, D, 0)], [vec["norm_mix_pre"]], [(D, BF16)])
    p_sb = _mm("proj_sb", h1, w_sb)
    p_rw = _mm("proj_rw", h1, w_rw)
    p_gt = _mm("proj_gate", h1, w_gt)
    o_sb = _attn_fwd(p_sb, B, S)
    pre_params = mu_parts + [vec["w0"], lora["w_up"], vec["a0"], lora["a_up"], lora["g_up"], vec["k_k"], vec["k_a"]]
    r_, lw_, k2_, v_, kap_, a_, g_ = _rw_pre(p_rw, pre_params, S)
    y_wkv, states = _wkv_fwd(r_, lw_, k2_, v_, kap_, a_, B, S)
    post_rows = [(y_wkv, 512, 0), (r_, 512, 0), (k2_, 512, 0), (v_, 512, 0), (g_, 512, 0)]
    post_params = [vec["lnx_w"], vec["lnx_b"], vec["r_k"]]
    (o_rw,) = _rowwise("rw_post", _f_rwpost, post_rows, post_params, [(512, BF16)])
    m1 = _mm("mix_sb_out", o_sb, full["w_sb_out"])
    m2 = _mm("mix_rw_out", o_rw, full["w_rw_out"])
    merge_rows = [(p_gt, D, 0), (p_gt, D, 1), (m1, D, 0), (m2, D, 0)]
    (merged,) = _rowwise("merge", _f_merge, merge_rows, [b1, b2], [(D, BF16)])
    u = _mm("mix_out", merged, full["w_o"])
    post1_params = [vec["norm_mix_post"], vec["norm_ffn_pre"]]
    x1, h2 = _rowwise("post_mix", _f_post1, [(x2, D, 0), (u, D, 0)], post1_params, [(D, F32), (D, BF16)])
    ag = _mm("ffn_gate", h2, full["w_ffn_gate"], out_dtype=BF16)
    au = _mm("ffn_up", h2, full["w_ffn_up"], out_dtype=BF16)
    (sw,) = _rowwise("swiglu", _f_swiglu, [(ag, D_FF, 0), (au, D_FF, 0)], [], [(D_FF, BF16)])
    f = _mm("ffn_down", sw, full["w_ffn_down"])
    loss_part, dx1, df, dg4 = _loss_head(x1, f, tgt, vec["norm_ffn_post"])

    gbig, gsmall = {}, {"norm_ffn_post": dg4}
    dsw = _mm("d_swiglu_out", df, full["w_ffn_down"], tb=True, out_dtype=BF16)
    gbig["w_ffn_down"] = _mm("g_ffn_down", sw, df, ta=True)
    (dag, dau), _ = _rowwise_vjp("swiglu_bwd", _f_swiglu, [(ag, D_FF, 0), (au, D_FF, 0)], [], [[dsw]], [True, True], [],
                                 bf16_rows=(0, 1))
    dh2 = _mm("d_h2_gate", dag, full["w_ffn_gate"], tb=True)
    dh2 = _mm("d_h2_up", dau, full["w_ffn_up"], tb=True, acc=dh2)
    gbig["w_ffn_gate"] = _mm("g_ffn_gate", h2, dag, ta=True)
    gbig["w_ffn_up"] = _mm("g_ffn_up", h2, dau, ta=True)
    (dx_res, du), (dg2, dg3) = _rowwise_vjp("post_mix_bwd", _f_post1, [(x2, D, 0), (u, D, 0)], post1_params,
                                            [[dx1], [dh2]], [True, True], [True, True], bf16_rows=(1,))
    gsmall["norm_mix_post"], gsmall["norm_ffn_pre"] = dg2, dg3
    dmerged = _mm("d_merged", du, full["w_o"], tb=True)
    gbig["w_o"] = _mm("g_w_o", merged, du, ta=True)
    (dpg1, dpg2, dm1, dm2), (db1, db2) = _rowwise_vjp("merge_bwd", _f_merge, merge_rows, [b1, b2], [[dmerged]],
                                                      [True] * 4, [True, True], bf16_rows=(0, 1, 2, 3))
    gsmall["b_gate"] = jnp.concatenate([db1, db2], axis=1)
    do_sb = _mm("d_o_sb", dm1, full["w_sb_out"], tb=True)
    do_rw = _mm("d_o_rw", dm2, full["w_rw_out"], tb=True)
    gbig["w_sb_out"] = _mm("g_sb_out", o_sb, dm1, ta=True)
    gbig["w_rw_out"] = _mm("g_rw_out", o_rw, dm2, ta=True)
    (dy_wkv, dr_a, dk2_a, dv_a, dg_), (dlnx_w, dlnx_b, dr_k) = _rowwise_vjp(
        "rw_post_bwd", _f_rwpost, post_rows, post_params, [[do_rw]], [True] * 5, [True] * 3)
    gsmall["lnx_w"], gsmall["lnx_b"], gsmall["r_k"] = dlnx_w, dlnx_b, dr_k
    dr_b, dlw, dk2_b, dv_b, dkap, da = _wkv_bwd(r_, lw_, k2_, v_, kap_, a_, states, dy_wkv, B, S)
    pre_cts = [[dr_a, dr_b], [dlw], [dk2_a, dk2_b], [dv_a, dv_b], [dkap], [da], [dg_]]
    dp_rw, dpre_params = _rw_pre_bwd(p_rw, pre_params, pre_cts, S)
    gsmall["mu_rw"] = jnp.concatenate(dpre_params[:4], axis=1)
    gsmall["w0"], gsmall["a0"], gsmall["k_k"], gsmall["k_a"] = dpre_params[4], dpre_params[6], dpre_params[9], dpre_params[10]
    glora = {"w_up": dpre_params[5][0:64], "a_up": dpre_params[7][64:128], "g_up": dpre_params[8][128:256]}
    dq, dk, dv = _attn_bwd(p_sb, o_sb, do_sb, B, S)
    dh1 = _mm("d_h1_q", dq, w_sb[:, :512], tb=True)
    dh1 = _mm("d_h1_k", dk, w_sb[:, 512:1024], tb=True, acc=dh1)
    dh1 = _mm("d_h1_v", dv, w_sb[:, 1024:], tb=True, acc=dh1)
    dh1 = _mm("d_h1_rw", dp_rw, w_rw, tb=True, acc=dh1)
    dh1 = _mm("d_h1_g1", dpg1, w_gt[:, :D], tb=True, acc=dh1)
    dh1 = _mm("d_h1_g2", dpg2, w_gt[:, D:], tb=True, acc=dh1)
    gbig["w_in"] = jnp.concatenate(
        [_mm("g_in_" + tag, h1, d, ta=True)
         for tag, d in (("q", dq), ("k", dk), ("v", dv), ("rw", dp_rw), ("g1", dpg1), ("g2", dpg2))], axis=1)
    (grad_x2,), (dg1,) = _rowwise_vjp("norm_mix_pre_bwd", _f_norm, [(x2, D, 0)], [vec["norm_mix_pre"]], [[dh1]],
                                      [True], [True], add_to={0: dx_res})
    gsmall["norm_mix_pre"] = dg1
    gbig.update(glora)

    split = []
    for n in BIG:
        g = _rows_to_shards(gbig[n]) if n in ROW_SHARDED else _cols_to_shards(gbig[n])
        split.append(g.reshape(N_CHIPS, 2, g.shape[1] // 2, g.shape[2]))
    core = jnp.stack([lax.axis_index("c"), 2 * lax.axis_index("x") + lax.axis_index("y")]).astype(jnp.int32)
    theirs = _pair_split(split)
    chip_sums = [_pair_sum("pair_sum_" + n, a, b, core) for n, a, b in zip(BIG, split, theirs)]
    landed = _chip_scatter(chip_sums)
    joined = _pair_join([_chip_sum("chip_sum_" + n, own, got, core) for n, own, got in zip(BIG, chip_sums, landed)])
    grads = {n: j.reshape(W[n].shape[1:]) for n, j in zip(BIG, joined)}

    small_local = _pack_small({n: gsmall[n] for n in SMALL}, extra_rows=1)
    loss_row = small_local.shape[0] - 1
    small_local = small_local.at[loss_row].set(loss_part[0])
    small_sum = _all_reduce_small(small_local)
    loss = small_sum[loss_row, 0]

    delta, new_m, new_v = {}, {}, {}
    for n in BIG:
        d_, m_, v2_ = _adamw("adamw_" + n, W[n][0], grads[n], Mo[n][0], Vo[n][0])
        delta[n], new_m[n], new_v[n] = d_[None], m_[None], v2_[None]
        grads[n] = grads[n][None]
    pk = lambda src: _pack_small({n: src[n] for n in SMALL}, extra_rows=1)
    d_s, m_s, v_s = _adamw("adamw_small", pk(W), small_sum.at[loss_row].set(0.0), pk(Mo), pk(Vo))
    for dst, packed in ((grads, small_sum), (delta, d_s), (new_m, m_s), (new_v, v_s)):
        dst.update(_unpack_small(packed, shapes))

    return (loss, grad_x2.reshape(B, S, D), *[grads[n] for n in ORDER], *[delta[n] for n in ORDER],
            *[new_m[n] for n in ORDER], *[new_v[n] for n in ORDER])
```

```python
import functools

import jax
import jax.numpy as jnp
from jax import lax
from jax.experimental import pallas as pl
from jax.experimental.pallas import tpu as pltpu

F32 = jnp.float32
BF16 = jnp.bfloat16
MESH = pl.DeviceIdType.MESH

D_MODEL = 1024
SB_HEADS = 8
HEAD_DIM = 64
SB_WIDTH = SB_HEADS * HEAD_DIM
RW_WIDTH = 512
LORA_COLS = 256
SB_COLS = 3 * SB_WIDTH
RW_COLS = 3 * RW_WIDTH + LORA_COLS
GATE_COLS = 2 * D_MODEL
D_FF = 2816
RMS_EPS = 1e-6
GN_EPS = HEAD_DIM * 1e-5
WKV_CHUNK = 64
WKV_SEQS = 4
ATTN_QUERIES = 512
ATTN_KEYS = 128
LANES = 128
SUBLANES = 8
N_CHIPS = 4
N_DEV = 8

ADAM_LR = 0.001
ADAM_B1 = 0.9
ADAM_B2 = 0.999
ADAM_EPS = 1e-08
ADAM_WD = 0.01
ADAM_STEP = 10

VMEM_LIMIT = 48 * 1024 * 1024
WKV_BWD_VMEM = 58 * 1024 * 1024


def _params(sem=None, vmem=VMEM_LIMIT, **kw):
    if sem is not None:
        kw["dimension_semantics"] = sem
    return pltpu.CompilerParams(vmem_limit_bytes=vmem, **kw)


def _div_tile(dim, pref, mult=LANES):
    if dim <= pref:
        return dim
    t = pref - pref % mult
    while t >= mult:
        if dim % t == 0:
            return t
        t -= mult
    return dim


def _dot(a, b, dims):
    return lax.dot_general(a, b, (dims, ((), ())), preferred_element_type=F32)


def _mm(name, a, b, *, ta=False, tb=False, acc=None, out_dtype=F32):
    if ta:
        K, M = a.shape
    else:
        M, K = a.shape
    N = b.shape[0] if tb else b.shape[1]
    if ta:
        tm, tn, tk = _div_tile(M, 1408), _div_tile(N, 1408), _div_tile(K, 512)
    else:
        tm, tn, tk = _div_tile(M, 512), _div_tile(N, 1408), _div_tile(K, 1408)
    nk = K // tk
    dims = ((0,) if ta else (1,), (1,) if tb else (0,))
    has_acc = acc is not None

    def body(*refs):
        a_ref, b_ref = refs[0], refs[1]
        o_ref, scr = refs[-2], refs[-1]
        k = pl.program_id(2)
        part = _dot(a_ref[...].astype(BF16), b_ref[...].astype(BF16), dims)

        @pl.when(k == 0)
        def _():
            scr[...] = part + refs[2][...] if has_acc else part

        @pl.when(k > 0)
        def _():
            scr[...] += part

        @pl.when(k == nk - 1)
        def _():
            o_ref[...] = scr[...].astype(o_ref.dtype)

    a_spec = pl.BlockSpec((tk, tm), lambda i, j, k: (k, i)) if ta else pl.BlockSpec((tm, tk), lambda i, j, k: (i, k))
    b_spec = pl.BlockSpec((tn, tk), lambda i, j, k: (j, k)) if tb else pl.BlockSpec((tk, tn), lambda i, j, k: (k, j))
    o_spec = pl.BlockSpec((tm, tn), lambda i, j, k: (i, j))
    return pl.pallas_call(
        body, name=name,
        grid=(M // tm, N // tn, nk),
        in_specs=[a_spec, b_spec] + ([o_spec] if has_acc else []),
        out_specs=o_spec,
        out_shape=jax.ShapeDtypeStruct((M, N), out_dtype),
        scratch_shapes=[pltpu.VMEM((tm, tn), F32)],
        compiler_params=_params(("parallel", "parallel", "arbitrary")),
    )(*([a, b] + ([acc] if has_acc else [])))


def _row_spec(tile, width, colblk):
    return pl.BlockSpec((tile, width), lambda i: (i, colblk))


def _full_spec(shape):
    return pl.BlockSpec(shape, lambda i: (0,) * len(shape))


def _rowwise(name, fn, rows, params, outs, tile=256):
    T = rows[0][0].shape[0]
    tile = min(tile, T)
    n_r, n_p = len(rows), len(params)

    def body(*refs):
        r = [x[...].astype(F32) for x in refs[:n_r]]
        p = [x[...].astype(F32) for x in refs[n_r:n_r + n_p]]
        for o_ref, val in zip(refs[n_r + n_p:], fn(*r, *p)):
            o_ref[...] = val.astype(o_ref.dtype)

    return pl.pallas_call(
        body, name=name,
        grid=(T // tile,),
        in_specs=[_row_spec(tile, w, cb) for _, w, cb in rows] + [_full_spec(p.shape) for p in params],
        out_specs=[_row_spec(tile, w, 0) for w, _ in outs],
        out_shape=[jax.ShapeDtypeStruct((T, w), dt) for w, dt in outs],
        compiler_params=_params(("parallel",)),
    )(*([a for a, _, _ in rows] + list(params)))


def _rowwise_vjp(name, fn, rows, params, cts, need_rows, need_params, add_to=None, tile=256, bf16_rows=()):
    add_to = add_to or {}
    T = rows[0][0].shape[0]
    tile = min(tile, T)
    n_r, n_p = len(rows), len(params)
    ct_flat = [c for group in cts for c in group]
    ct_sizes = [len(group) for group in cts]
    add_idx = sorted(add_to)
    row_out = [i for i in range(n_r) if need_rows[i]]
    par_out = [i for i in range(n_p) if need_params[i]]
    n_ct, n_add = len(ct_flat), len(add_idx)

    def body(*refs):
        pos = 0
        r = [x[...].astype(F32) for x in refs[pos:pos + n_r]]
        pos += n_r
        p = [x[...].astype(F32) for x in refs[pos:pos + n_p]]
        pos += n_p
        ct_vals = [x[...].astype(F32) for x in refs[pos:pos + n_ct]]
        pos += n_ct
        adds = {i: x[...] for i, x in zip(add_idx, refs[pos:pos + n_add])}
        pos += n_add
        drow_refs = refs[pos:pos + len(row_out)]
        pos += len(row_out)
        dpar_refs = refs[pos:pos + len(par_out)]
        ct_in, q = [], 0
        for n in ct_sizes:
            ct_in.append(functools.reduce(lambda u, v: u + v, ct_vals[q:q + n]))
            q += n
        _, vjp = jax.vjp(fn, *r, *p)
        grads = vjp(tuple(ct_in))
        for ref, i in zip(drow_refs, row_out):
            g = grads[i]
            ref[...] = (g + adds[i] if i in adds else g).astype(ref.dtype)

        @pl.when(pl.program_id(0) == 0)
        def _():
            for ref in dpar_refs:
                ref[...] = jnp.zeros_like(ref)

        for ref, i in zip(dpar_refs, par_out):
            ref[...] += grads[n_r + i]

    ct_widths = [c.shape[1] for c in ct_flat]
    in_specs = ([_row_spec(tile, w, cb) for _, w, cb in rows] + [_full_spec(p.shape) for p in params]
                + [_row_spec(tile, w, 0) for w in ct_widths] + [_row_spec(tile, rows[i][1], 0) for i in add_idx])
    out_specs = [_row_spec(tile, rows[i][1], 0) for i in row_out] + [_full_spec(params[i].shape) for i in par_out]
    out_shape = ([jax.ShapeDtypeStruct((T, rows[i][1]), BF16 if i in bf16_rows else F32) for i in row_out]
                 + [jax.ShapeDtypeStruct(params[i].shape, F32) for i in par_out])
    res = pl.pallas_call(
        body, name=name,
        grid=(T // tile,),
        in_specs=in_specs, out_specs=out_specs, out_shape=out_shape,
        compiler_params=_params(("arbitrary",)),
    )(*([a for a, _, _ in rows] + list(params) + ct_flat + [add_to[i] for i in add_idx]))
    return res[:len(row_out)], res[len(row_out):]


def _sigmoid(x):
    return 0.5 * (jnp.tanh(0.5 * x) + 1.0)


def _softplus(x):
    return jnp.maximum(x, 0.0) + jnp.log(1.0 + jnp.exp(-jnp.abs(x)))


def _rms(x, g):
    return x * lax.rsqrt(jnp.mean(x * x, axis=-1, keepdims=True) + RMS_EPS) * g


def _segsum_impl(x):
    n = x.shape[-1]
    r = lax.shift_right_logical(lax.broadcasted_iota(jnp.int32, (n, n), 0), 6)
    c = lax.shift_right_logical(lax.broadcasted_iota(jnp.int32, (n, n), 1), 6)
    bd = (r == c).astype(BF16)
    hi = x.astype(BF16)
    rest = x - hi.astype(F32)
    mid = rest.astype(BF16)
    lo = (rest - mid.astype(F32)).astype(BF16)
    nn = ((1,), (0,))
    return _dot(hi, bd, nn) + _dot(mid, bd, nn) + _dot(lo, bd, nn)


@jax.custom_vjp
def _segsum(x):
    return _segsum_impl(x)


_segsum.defvjp(lambda x: (_segsum_impl(x), None), lambda _, g: (_segsum_impl(g),))


@jax.custom_vjp
def _mmb(a, w):
    return _dot(a.astype(BF16), w.astype(BF16), ((1,), (0,)))


def _mmb_fwd(a, w):
    return _mmb(a, w), (a, w)


def _mmb_bwd(res, g):
    a, w = res
    gb = g.astype(BF16)
    return _dot(gb, w.astype(BF16), ((1,), (1,))), _dot(a.astype(BF16), gb, ((0,), (0,)))


_mmb.defvjp(_mmb_fwd, _mmb_bwd)


def _f_norm(x, g):
    return (_rms(x, g),)


def _f_post1(x, u, g2, g3):
    x1 = x + _rms(u, g2)
    return x1, _rms(x1, g3)


def _f_swiglu(ag, au):
    return (ag * _sigmoid(ag) * au,)


def _f_merge(pg1, pg2, m1, m2, b1, b2):
    return (_sigmoid(pg1 + b1) * m1 + _sigmoid(pg2 + b2) * m2,)


def _f_out(x1, f, g4):
    return (x1 + _rms(f, g4),)


def _f_rwpre(pr, pk, pv, pz, qr, qk, qv, qz, mur, muk, muv, muz, w0, wup, a0, aup, gup, k_k, k_a):
    r = pr + (qr - pr) * mur
    k = pk + (qk - pk) * muk
    v = pv + (qv - pv) * muv
    z = pz + (qz - pz) * muz
    w_raw = w0 + _mmb(jnp.tanh(z), wup)
    lw = -jnp.exp(-_softplus(-w_raw) - 0.5)
    a = _sigmoid(a0 + _mmb(z, aup))
    g = _mmb(_sigmoid(z), gup)
    kk = k * k_k
    kap = kk * lax.rsqrt(jnp.maximum(_segsum(kk * kk), 1e-24))
    k2 = k * (1.0 + (a - 1.0) * k_a)
    return r, lw, k2, v, kap, a, g


def _f_rwpost(y, r, k2, v, g, lnx_w, lnx_b, r_k):
    inv = 1.0 / HEAD_DIM
    yc = y - _segsum(y) * inv
    var = _segsum(yc * yc) * inv
    yn = yc * lax.rsqrt(var + GN_EPS) * lnx_w + lnx_b
    bonus = _segsum(r * k2 * r_k) * v
    return ((yn + bonus) * g,)


RW_GROUPS = (0, 512, 1024, 1536, RW_COLS)


def _column_groups(p):
    return [p[:, a:b] for a, b in zip(RW_GROUPS[:-1], RW_GROUPS[1:])]


def _previous_tokens(p, halo, first_of_sequence):
    rows = lax.broadcasted_iota(jnp.int32, (p.shape[0], 1), 0)
    before = jnp.where(first_of_sequence, 0.0, halo[SUBLANES - 1:SUBLANES, :])
    return jnp.where(rows == 0, before, pltpu.roll(p, 1, axis=0))


def _halo_spec(tile, order):
    per = tile // SUBLANES
    return pl.BlockSpec((SUBLANES, RW_COLS), lambda i: (jnp.maximum(order(i) * per - 1, 0), 0))


def _rw_pre(p_rw, params, S, tile=128):
    T = p_rw.shape[0]
    tile = min(tile, T)
    assert S % tile == 0
    n_p = len(params)

    def body(*refs):
        p_ref, halo_ref = refs[0], refs[1]
        par = [x[...].astype(F32) for x in refs[2:2 + n_p]]
        p = p_ref[...]
        first = lax.rem(pl.program_id(0) * tile, S) == 0
        prev = _previous_tokens(p, halo_ref[...], first)
        for o_ref, val in zip(refs[2 + n_p:], _f_rwpre(*_column_groups(p), *_column_groups(prev), *par)):
            o_ref[...] = val

    out_spec = pl.BlockSpec((tile, RW_WIDTH), lambda i: (i, 0))
    return pl.pallas_call(
        body, name="rw_pre",
        grid=(T // tile,),
        in_specs=[pl.BlockSpec((tile, RW_COLS), lambda i: (i, 0)), _halo_spec(tile, lambda i: i)]
                 + [_full_spec(q.shape) for q in params],
        out_specs=[out_spec] * 7,
        out_shape=[jax.ShapeDtypeStruct((T, RW_WIDTH), F32)] * 7,
        compiler_params=_params(("parallel",)),
    )(p_rw, p_rw, *params)


def _rw_pre_bwd(p_rw, params, cts, S, tile=128):
    T = p_rw.shape[0]
    tile = min(tile, T)
    assert S % tile == 0
    nt = T // tile
    n_p = len(params)
    ct_flat = [c for group in cts for c in group]
    ct_sizes = [len(group) for group in cts]
    n_ct = len(ct_flat)

    def body(*refs):
        p_ref, halo_ref = refs[0], refs[1]
        par = [x[...].astype(F32) for x in refs[2:2 + n_p]]
        ct_vals = [x[...] for x in refs[2 + n_p:2 + n_p + n_ct]]
        dp_ref = refs[2 + n_p + n_ct]
        dpar_refs = refs[3 + n_p + n_ct:3 + 2 * n_p + n_ct]
        carry = refs[-1]
        step = pl.program_id(0)

        @pl.when(step == 0)
        def _():
            carry[...] = jnp.zeros_like(carry)
            for ref in dpar_refs:
                ref[...] = jnp.zeros_like(ref)

        ct_in, q = [], 0
        for n in ct_sizes:
            ct_in.append(functools.reduce(lambda u, v: u + v, ct_vals[q:q + n]))
            q += n
        p = p_ref[...]
        first = lax.rem((nt - 1 - step) * tile, S) == 0
        prev = _previous_tokens(p, halo_ref[...], first)
        _, vjp = jax.vjp(_f_rwpre, *_column_groups(p), *_column_groups(prev), *par)
        grads = vjp(tuple(ct_in))
        d_here = jnp.concatenate(grads[0:4], axis=1)
        d_prev = jnp.concatenate(grads[4:8], axis=1)
        rows = lax.broadcasted_iota(jnp.int32, (tile, 1), 0)
        from_next = jnp.where(rows == tile - 1, carry[0:1, :], pltpu.roll(d_prev, tile - 1, axis=0))
        dp_ref[...] = (d_here + from_next).astype(dp_ref.dtype)
        carry[...] = jnp.broadcast_to(jnp.where(first, 0.0, d_prev[0:1, :]), carry.shape)
        for ref, g in zip(dpar_refs, grads[8:]):
            ref[...] += g

    back = lambda i: nt - 1 - i
    row = lambda w: pl.BlockSpec((tile, w), lambda i: (back(i), 0))
    res = pl.pallas_call(
        body, name="rw_pre_bwd",
        grid=(nt,),
        in_specs=[row(RW_COLS), _halo_spec(tile, back)] + [_full_spec(q.shape) for q in params]
                 + [row(RW_WIDTH)] * n_ct,
        out_specs=[row(RW_COLS)] + [_full_spec(q.shape) for q in params],
        out_shape=[jax.ShapeDtypeStruct((T, RW_COLS), BF16)] + [jax.ShapeDtypeStruct(q.shape, F32) for q in params],
        scratch_shapes=[pltpu.VMEM((SUBLANES, RW_COLS), F32)],
        compiler_params=_params(("arbitrary",)),
    )(p_rw, p_rw, *params, *ct_flat)
    return res[0], res[1:]


def _loss_head(x1, f, target, g4, tile=256):
    T, D = x1.shape
    tile = min(tile, T)

    def body(x1_ref, f_ref, t_ref, g_ref, loss_ref, dx1_ref, df_ref, dg_ref):
        (y,), vjp = jax.vjp(_f_out, x1_ref[...], f_ref[...], g_ref[...])
        err = y - t_ref[...]
        dx1, df, dg = vjp((err * (1.0 / D),))
        dx1_ref[...] = dx1
        df_ref[...] = df.astype(df_ref.dtype)

        @pl.when(pl.program_id(0) == 0)
        def _():
            loss_ref[...] = jnp.zeros_like(loss_ref)
            dg_ref[...] = jnp.zeros_like(dg_ref)

        part = jnp.sum(jnp.sum(err * err, axis=1, keepdims=True), axis=0, keepdims=True) * (0.5 / D)
        loss_ref[...] += jnp.broadcast_to(part, loss_ref.shape)
        dg_ref[...] += dg

    row = pl.BlockSpec((tile, D), lambda i: (i, 0))
    return pl.pallas_call(
        body, name="loss_head",
        grid=(T // tile,),
        in_specs=[row, row, row, _full_spec(g4.shape)],
        out_specs=[_full_spec((SUBLANES, LANES)), row, row, _full_spec(g4.shape)],
        out_shape=[jax.ShapeDtypeStruct((SUBLANES, LANES), F32), jax.ShapeDtypeStruct((T, D), F32),
                   jax.ShapeDtypeStruct((T, D), BF16), jax.ShapeDtypeStruct(g4.shape, F32)],
        compiler_params=_params(("arbitrary",)),
    )(x1, f, target, g4)


def _nn(a, b):
    return _dot(a, b, ((1,), (0,)))


def _nt(a, b):
    return _dot(a, b, ((1,), (1,)))


def _tn(a, b):
    return _dot(a, b, ((0,), (0,)))


def _split_dot(x, u2):
    hi = x.astype(BF16)
    lo = (x - hi.astype(F32)).astype(BF16)
    return _nn(jnp.concatenate([hi, lo], axis=1), u2)


def _by_head(x, masks):
    return jnp.concatenate([(x * m).astype(BF16) for m in masks], axis=0)


def _fold_heads(x2, masks):
    R = x2.shape[0] // len(masks)
    return functools.reduce(lambda u, v: u + v, [x2[h * R:(h + 1) * R] * m for h, m in enumerate(masks)])


def _head_masks():
    lane = lax.broadcasted_iota(jnp.int32, (1, LANES), 1)
    return [((lane >= h * HEAD_DIM) & (lane < (h + 1) * HEAD_DIM)).astype(F32) for h in range(LANES // HEAD_DIM)]


def _key_tri(op):
    row = lax.broadcasted_iota(jnp.int32, (ATTN_KEYS, ATTN_KEYS), 0)
    col = lax.broadcasted_iota(jnp.int32, (ATTN_KEYS, ATTN_KEYS), 1)
    u = op(row, col).astype(BF16)
    return jnp.concatenate([u, u], axis=0)


def _causal(rows):
    row = lax.broadcasted_iota(jnp.int32, (rows, ATTN_KEYS), 0)
    col = lax.broadcasted_iota(jnp.int32, (rows, ATTN_KEYS), 1)
    return col < row


def _from_row(tree, r):
    return jax.tree.map(lambda x: x[r:], tree)


def _onto_rows(old, new, r):
    return jax.tree.map(lambda o, n: jnp.concatenate([o[:r], n], axis=0) if r else n, old, new)


def _sb_weights(qb16, kbh, c_fails, u_gt, strict, scale):
    z_all = _nt(qb16, kbh) * scale
    zs = [z_all[:, h * ATTN_KEYS:(h + 1) * ATTN_KEYS] for h in range(len(c_fails))]
    Ls = [jnp.minimum(-z, 0.0) - jnp.log(1.0 + jnp.exp(-jnp.abs(z))) for z in zs]
    Lms = Ls if strict is None else [jnp.where(strict, L, 0.0) for L in Ls]
    cums = [_split_dot(Lm, u_gt) for Lm in Lms]
    As = [jnp.exp(z + L + c + cum) for z, L, c, cum in zip(zs, Ls, c_fails, cums)]
    if strict is not None:
        As = [jnp.where(strict, A, 0.0) for A in As]
    return zs, Ls, Lms, As


def _attn_specs(S, qb):
    nq = S // qb
    q_spec = pl.BlockSpec((qb, LANES), lambda b, p, i: (b * nq + i, p))
    k_spec = pl.BlockSpec((S, LANES), lambda b, p, i: (b, SB_WIDTH // LANES + p))
    v_spec = pl.BlockSpec((S, LANES), lambda b, p, i: (b, 2 * SB_WIDTH // LANES + p))
    seq = pl.BlockSpec((S, LANES), lambda b, p, i: (b, p))
    return q_spec, k_spec, v_spec, q_spec, seq


def _key_walk(i, qb, block, carry):
    per = qb // ATTN_KEYS
    for sub in reversed(range(per)):
        carry = block(i * per + sub, carry, sub * ATTN_KEYS)
    return lax.fori_loop(0, i * per, lambda j, c: block(i * per - 1 - j, c, None), carry)


def _attn_fwd(proj, B, S):
    qb = min(ATTN_QUERIES, S)
    scale = HEAD_DIM ** -0.5

    def body(q_ref, k_ref, v_ref, o_ref):
        i = pl.program_id(2)
        masks = _head_masks()
        u_gt = _key_tri(lambda r, c: r > c)
        q16 = q_ref[...].astype(BF16)

        def block(J, carry, row0):
            r0 = pl.multiple_of(J * ATTN_KEYS, ATTN_KEYS)
            kbh = _by_head(k_ref[pl.ds(r0, ATTN_KEYS), :], masks)
            vbh = _by_head(v_ref[pl.ds(r0, ATTN_KEYS), :], masks)
            lo = row0 or 0
            strict = None if row0 is None else _causal(qb - lo)
            acc, cs = _from_row(carry, lo)
            _, _, Lms, As = _sb_weights(q16[lo:], kbh, cs, u_gt, strict, scale)
            acc = acc + _nn(jnp.concatenate([A.astype(BF16) for A in As], axis=1), vbh)
            cs = tuple(c + jnp.sum(Lm, axis=1, keepdims=True) for c, Lm in zip(cs, Lms))
            return _onto_rows(carry, (acc, cs), lo)

        zero_c = tuple(jnp.zeros((qb, 1), F32) for _ in masks)
        carry = _key_walk(i, qb, block, (jnp.zeros((qb, LANES), F32), zero_c))
        o_ref[...] = carry[0]

    q_spec, k_spec, v_spec, blk, _ = _attn_specs(S, qb)
    return pl.pallas_call(
        body, name="sb_attn_fwd",
        grid=(B, SB_WIDTH // LANES, S // qb),
        in_specs=[q_spec, k_spec, v_spec],
        out_specs=blk,
        out_shape=jax.ShapeDtypeStruct((B * S, SB_WIDTH), F32),
        compiler_params=_params(("parallel", "parallel", "arbitrary")),
    )(proj, proj, proj)


def _attn_bwd(proj, o, do, B, S):
    qb = min(ATTN_QUERIES, S)
    nq = S // qb
    scale = HEAD_DIM ** -0.5

    def body(q_ref, k_ref, v_ref, o_ref, do_ref, dq_ref, dk_out, dv_out, dk_ref, dv_ref):
        i = pl.program_id(2)

        @pl.when(i == 0)
        def _():
            dk_ref[...] = jnp.zeros_like(dk_ref)
            dv_ref[...] = jnp.zeros_like(dv_ref)

        masks = _head_masks()
        u_gt = _key_tri(lambda r, c: r > c)
        u_ge = _key_tri(lambda r, c: r >= c)
        heads = range(len(masks))
        q16 = q_ref[...].astype(BF16)
        do16 = do_ref[...].astype(BF16)
        od = o_ref[...] * do16.astype(F32)
        totals = tuple(jnp.sum(od * m, axis=1, keepdims=True) for m in masks)

        def block(J, carry, row0):
            r0 = pl.multiple_of(J * ATTN_KEYS, ATTN_KEYS)
            kbh = _by_head(k_ref[pl.ds(r0, ATTN_KEYS), :], masks)
            vbh = _by_head(v_ref[pl.ds(r0, ATTN_KEYS), :], masks)
            lo = row0 or 0
            strict = None if row0 is None else _causal(qb - lo)
            dq, c_fail, c_p = _from_row(carry, lo)
            tot = _from_row(totals, lo)
            zs, Ls, Lms, As = _sb_weights(q16[lo:], kbh, c_fail, u_gt, strict, scale)
            Abs = [A.astype(BF16) for A in As]
            dA_all = _nt(do16[lo:], vbh)
            Ps = [Abs[h].astype(F32) * dA_all[:, h * ATTN_KEYS:(h + 1) * ATTN_KEYS] for h in heads]
            afters = [c_p[h] + _split_dot(Ps[h], u_ge) for h in heads]
            sigs = [jnp.exp(zs[h] + Ls[h]) for h in heads]
            dzs = [(Ps[h] * (1.0 - sigs[h]) - sigs[h] * (tot[h] - afters[h])) * scale for h in heads]
            if strict is not None:
                dzs = [jnp.where(strict, dz, 0.0) for dz in dzs]
            dz_all = jnp.concatenate([dz.astype(BF16) for dz in dzs], axis=1)
            dv_ref[pl.ds(r0, ATTN_KEYS), :] += _fold_heads(_tn(jnp.concatenate(Abs, axis=1), do16[lo:]), masks)
            dk_ref[pl.ds(r0, ATTN_KEYS), :] += _fold_heads(_tn(dz_all, q16[lo:]), masks)
            dq = dq + _nn(dz_all, kbh)
            c_fail = tuple(c_fail[h] + jnp.sum(Lms[h], axis=1, keepdims=True) for h in heads)
            c_p = tuple(c_p[h] + jnp.sum(Ps[h], axis=1, keepdims=True) for h in heads)
            return _onto_rows(carry, (dq, c_fail, c_p), lo)

        zc = tuple(jnp.zeros((qb, 1), F32) for _ in masks)
        carry = _key_walk(i, qb, block, (jnp.zeros((qb, LANES), F32), zc, zc))
        dq_ref[...] = carry[0].astype(dq_ref.dtype)

        @pl.when(i == nq - 1)
        def _():
            dk_out[...] = dk_ref[...].astype(dk_out.dtype)
            dv_out[...] = dv_ref[...].astype(dv_out.dtype)

    q_spec, k_spec, v_spec, blk, seq = _attn_specs(S, qb)
    return pl.pallas_call(
        body, name="sb_attn_bwd",
        grid=(B, SB_WIDTH // LANES, nq),
        in_specs=[q_spec, k_spec, v_spec, blk, blk],
        out_specs=[blk, seq, seq],
        out_shape=[jax.ShapeDtypeStruct((B * S, SB_WIDTH), BF16)] * 3,
        scratch_shapes=[pltpu.VMEM((S, LANES), F32), pltpu.VMEM((S, LANES), F32)],
        compiler_params=_params(("parallel", "parallel", "arbitrary")),
    )(proj, proj, proj, o, do)


_BATCHED = {"nn": "gmk,gkn->gmn", "nt": "gmk,gnk->gmn", "tn": "gkm,gkn->gmn"}


def _bdot_raw(a, b, kind, passes):
    e = functools.partial(jnp.einsum, _BATCHED[kind], preferred_element_type=F32)
    ah, bh = a.astype(BF16), b.astype(BF16)
    if passes == 1:
        return e(ah, bh)
    al, bl = (a - ah.astype(F32)).astype(BF16), (b - bh.astype(F32)).astype(BF16)
    return e(ah, bh) + e(ah, bl) + e(al, bh)


@functools.partial(jax.custom_vjp, nondiff_argnums=(2, 3))
def _bdot(a, b, kind, passes):
    return _bdot_raw(a, b, kind, passes)


def _bdot_fwd(a, b, kind, passes):
    return _bdot_raw(a, b, kind, passes), (a, b)


def _bdot_bwd(kind, passes, res, g):
    a, b = res
    if kind == "nn":
        return _bdot_raw(g, b, "nt", passes), _bdot_raw(a, g, "tn", passes)
    if kind == "nt":
        return _bdot_raw(g, b, "nn", passes), _bdot_raw(g, a, "tn", passes)
    return _bdot_raw(b, g, "nt", passes), _bdot_raw(a, g, "nn", passes)


_bdot.defvjp(_bdot_fwd, _bdot_bwd)


def _wkv_chunk(S0, r, lw, k, v, kap, a):
    G, C, N = r.shape
    row = lax.broadcasted_iota(jnp.int32, (C, C), 0)
    col = lax.broadcasted_iota(jnp.int32, (C, C), 1)
    incl = (col <= row).astype(F32)
    strict = (col < row).astype(F32)
    cum = _bdot(jnp.broadcast_to(incl, (G, C, C)), lw, "nn", 3)
    e_pos = jnp.exp(cum)
    e_neg = jnp.exp(-cum)
    al = -kap * jnp.exp(cum - lw)
    be = kap * a * e_neg
    kt = k * e_neg
    rt = r * e_pos
    bk = jnp.concatenate([be, kt], axis=1)
    mask = jnp.concatenate([jnp.concatenate([strict, strict], axis=1), jnp.concatenate([incl, incl], axis=1)], axis=0)
    m_all = _bdot(jnp.concatenate([al, rt], axis=1), bk, "nt", 3) * mask
    m_ab, m_ak = m_all[:, :C, :C], m_all[:, :C, C:]
    m_rb, m_rk = m_all[:, C:, :C], m_all[:, C:, C:]
    S0t = jnp.swapaxes(S0, 1, 2)
    sa = _bdot(jnp.concatenate([al, m_ak], axis=2), jnp.concatenate([S0t, v], axis=1), "nn", 3)
    p = m_ab
    steps = max(1, (C - 1).bit_length())
    for j in range(steps):
        sa = sa + _bdot(p, sa, "nn", 1)
        if j + 1 < steps:
            p = _bdot(p, p, "nn", 1)
    y = _bdot(jnp.concatenate([rt, m_rb, m_rk], axis=2), jnp.concatenate([S0t, sa, v], axis=1), "nn", 3)
    S1 = (S0 + _bdot(jnp.concatenate([sa, v], axis=1), bk, "tn", 1)) * e_pos[:, C - 1:C, :]
    return y, S1


def _split_heads(x):
    return jnp.stack([x[:, h * HEAD_DIM:(h + 1) * HEAD_DIM] for h in range(x.shape[1] // HEAD_DIM)], axis=0)


def _merge_heads(x):
    return jnp.concatenate([x[h] for h in range(x.shape[0])], axis=1)


def _seq_heads(ref):
    return jnp.concatenate([_split_heads(ref[s]) for s in range(ref.shape[0])], axis=0)


def _store_seq_heads(ref, x):
    heads = x.shape[0] // ref.shape[0]
    for s in range(ref.shape[0]):
        ref[s] = _merge_heads(x[s * heads:(s + 1) * heads])


def _wkv_fwd(r, lw, k, v, kap, a, B, S):
    C, H, N = WKV_CHUNK, RW_WIDTH // HEAD_DIM, HEAD_DIM
    nc = S // C
    Q = min(WKV_SEQS, B)

    def body(r_ref, lw_ref, k_ref, v_ref, kap_ref, a_ref, y_ref, st_ref, s_scr):
        @pl.when(pl.program_id(1) == 0)
        def _():
            s_scr[...] = jnp.zeros_like(s_scr)

        S0 = s_scr[...]
        for s in range(Q):
            st_ref[s, 0] = S0[s * H:(s + 1) * H]
        args = [_seq_heads(ref) for ref in (r_ref, lw_ref, k_ref, v_ref, kap_ref, a_ref)]
        y, S1 = _wkv_chunk(S0, *args)
        s_scr[...] = S1
        _store_seq_heads(y_ref, y)

    row_spec = pl.BlockSpec((Q, C, RW_WIDTH), lambda b, c: (b, c, 0))
    seqs = lambda t: t.reshape(B, S, RW_WIDTH)
    y, states = pl.pallas_call(
        body, name="wkv_fwd",
        grid=(B // Q, nc),
        in_specs=[row_spec] * 6,
        out_specs=[row_spec, pl.BlockSpec((Q, 1, H, N, N), lambda b, c: (b, c, 0, 0, 0))],
        out_shape=[jax.ShapeDtypeStruct((B, S, RW_WIDTH), F32), jax.ShapeDtypeStruct((B, nc, H, N, N), F32)],
        scratch_shapes=[pltpu.VMEM((Q * H, N, N), F32)],
        compiler_params=_params(("arbitrary", "arbitrary")),
    )(*map(seqs, (r, lw, k, v, kap, a)))
    return y.reshape(B * S, RW_WIDTH), states


def _wkv_bwd(r, lw, k, v, kap, a, states, dy, B, S):
    C, H, N = WKV_CHUNK, RW_WIDTH // HEAD_DIM, HEAD_DIM
    nc = S // C
    Q = min(WKV_SEQS, B)

    def body(r_ref, lw_ref, k_ref, v_ref, kap_ref, a_ref, st_ref, dy_ref,
             dr_ref, dlw_ref, dk_ref, dv_ref, dkap_ref, da_ref, ds_scr):
        @pl.when(pl.program_id(1) == 0)
        def _():
            ds_scr[...] = jnp.zeros_like(ds_scr)

        args = [_seq_heads(ref) for ref in (r_ref, lw_ref, k_ref, v_ref, kap_ref, a_ref)]
        S0 = jnp.concatenate([st_ref[s, 0] for s in range(Q)], axis=0)
        _, vjp = jax.vjp(_wkv_chunk, S0, *args)
        g = vjp((_seq_heads(dy_ref), ds_scr[...]))
        ds_scr[...] = g[0]
        for ref, gv in zip((dr_ref, dlw_ref, dk_ref, dv_ref, dkap_ref, da_ref), g[1:]):
            _store_seq_heads(ref, gv)

    row_spec = pl.BlockSpec((Q, C, RW_WIDTH), lambda b, c: (b, nc - 1 - c, 0))
    st_spec = pl.BlockSpec((Q, 1, H, N, N), lambda b, c: (b, nc - 1 - c, 0, 0, 0))
    seqs = lambda t: t.reshape(B, S, RW_WIDTH)
    res = pl.pallas_call(
        body, name="wkv_bwd",
        grid=(B // Q, nc),
        in_specs=[row_spec] * 6 + [st_spec, row_spec],
        out_specs=[row_spec] * 6,
        out_shape=[jax.ShapeDtypeStruct((B, S, RW_WIDTH), F32)] * 6,
        scratch_shapes=[pltpu.VMEM((Q * H, N, N), F32)],
        compiler_params=_params(("arbitrary", "arbitrary"), vmem=WKV_BWD_VMEM),
    )(*map(seqs, (r, lw, k, v, kap, a)), states, seqs(dy))
    return [t.reshape(B * S, RW_WIDTH) for t in res]


HBM = pl.BlockSpec(memory_space=pl.ANY)


def _place():
    return lax.axis_index("x"), lax.axis_index("y"), lax.axis_index("c")


def _other_chips(x, y):
    return [(1 - x, y), (x, 1 - y), (1 - x, 1 - y)]


def _all_gather_chips(shards):
    n = len(shards)

    def body(*refs):
        ins, outs = refs[:n], refs[n:2 * n]
        ici_send, ici_recv, d2d_send, d2d_recv, local = refs[2 * n:]
        x, y, c = _place()
        me = 2 * x + y
        sib = (x, y, 1 - c)
        chips = _other_chips(x, y)
        started, copies = [], []
        for w in range(n):
            cp = pltpu.make_async_copy(ins[w].at[c], outs[w].at[me, c], local.at[w])
            cp.start()
            copies.append(cp)
            for j, (px, py) in enumerate(chips):
                rd = pltpu.make_async_remote_copy(
                    src_ref=ins[w].at[c], dst_ref=outs[w].at[me, c], send_sem=ici_send.at[3 * w + j],
                    recv_sem=ici_recv.at[3 * w + j], device_id=(px, py, c), device_id_type=MESH)
                rd.start()
                started.append(rd)
            rd = pltpu.make_async_remote_copy(
                src_ref=ins[w].at[c], dst_ref=outs[w].at[me, c], send_sem=d2d_send.at[4 * w + 3],
                recv_sem=d2d_recv.at[4 * w + 3], device_id=sib, device_id_type=MESH)
            rd.start()
            started.append(rd)
        for w in range(n):
            for j, (px, py) in enumerate(chips):
                src = 2 * px + py
                pltpu.make_async_remote_copy(
                    src_ref=ins[w].at[c], dst_ref=outs[w].at[src, c], send_sem=ici_send.at[3 * w + j],
                    recv_sem=ici_recv.at[3 * w + j], device_id=(px, py, c), device_id_type=MESH).wait_recv()
                rd = pltpu.make_async_remote_copy(
                    src_ref=outs[w].at[src, c], dst_ref=outs[w].at[src, c], send_sem=d2d_send.at[4 * w + j],
                    recv_sem=d2d_recv.at[4 * w + j], device_id=sib, device_id_type=MESH)
                rd.start()
                started.append(rd)
        for w in range(n):
            for j, (px, py) in enumerate(chips):
                pltpu.make_async_remote_copy(
                    src_ref=ins[w].at[c], dst_ref=outs[w].at[2 * px + py, 1 - c], send_sem=d2d_send.at[4 * w + j],
                    recv_sem=d2d_recv.at[4 * w + j], device_id=sib, device_id_type=MESH).wait_recv()
            pltpu.make_async_remote_copy(
                src_ref=ins[w].at[c], dst_ref=outs[w].at[me, 1 - c], send_sem=d2d_send.at[4 * w + 3],
                recv_sem=d2d_recv.at[4 * w + 3], device_id=sib, device_id_type=MESH).wait_recv()
        for rd in started:
            rd.wait_send()
        for cp in copies:
            cp.wait()

    return pl.pallas_call(
        body, name="gather_weights",
        in_specs=[HBM] * n, out_specs=[HBM] * n,
        out_shape=[jax.ShapeDtypeStruct((N_CHIPS,) + s.shape, s.dtype) for s in shards],
        scratch_shapes=[pltpu.SemaphoreType.DMA((3 * n,)), pltpu.SemaphoreType.DMA((3 * n,)),
                        pltpu.SemaphoreType.DMA((4 * n,)), pltpu.SemaphoreType.DMA((4 * n,)),
                        pltpu.SemaphoreType.DMA((n,))],
        compiler_params=pltpu.CompilerParams(has_side_effects=True),
    )(*shards)


def _pair_split(grads):
    n = len(grads)

    def body(*refs):
        ins, theirs = refs[:n], refs[n:2 * n]
        send, recv = refs[2 * n:]
        x, y, c = _place()
        sib = (x, y, 1 - c)
        rds = []
        for w in range(n):
            rd = pltpu.make_async_remote_copy(
                src_ref=ins[w].at[:, 1 - c], dst_ref=theirs[w], send_sem=send.at[w], recv_sem=recv.at[w],
                device_id=sib, device_id_type=MESH)
            rd.start()
            rds.append(rd)
        for rd in rds:
            rd.wait_recv()
        for rd in rds:
            rd.wait_send()

    return pl.pallas_call(
        body, name="grad_pair_split",
        in_specs=[HBM] * n, out_specs=[HBM] * n,
        out_shape=[jax.ShapeDtypeStruct((g.shape[0],) + g.shape[2:], g.dtype) for g in grads],
        scratch_shapes=[pltpu.SemaphoreType.DMA((n,)), pltpu.SemaphoreType.DMA((n,))],
        compiler_params=pltpu.CompilerParams(has_side_effects=True),
    )(*grads)


def _chip_scatter(parts):
    n = len(parts)

    def body(*refs):
        ins, outs = refs[:n], refs[n:2 * n]
        send, recv = refs[2 * n:]
        x, y, c = _place()
        me = 2 * x + y
        rds = []
        for w in range(n):
            for j, (px, py) in enumerate(_other_chips(x, y)):
                s = 3 * w + j
                rd = pltpu.make_async_remote_copy(
                    src_ref=ins[w].at[2 * px + py], dst_ref=outs[w].at[j], send_sem=send.at[s], recv_sem=recv.at[s],
                    device_id=(px, py, c), device_id_type=MESH)
                rd.start()
                rds.append(rd)
        for w in range(n):
            for j, (px, py) in enumerate(_other_chips(x, y)):
                s = 3 * w + j
                pltpu.make_async_remote_copy(
                    src_ref=ins[w].at[me], dst_ref=outs[w].at[j], send_sem=send.at[s], recv_sem=recv.at[s],
                    device_id=(px, py, c), device_id_type=MESH).wait_recv()
        for rd in rds:
            rd.wait_send()

    return pl.pallas_call(
        body, name="grad_chip_scatter",
        in_specs=[HBM] * n, out_specs=[HBM] * n,
        out_shape=[jax.ShapeDtypeStruct((N_CHIPS - 1,) + p.shape[1:], p.dtype) for p in parts],
        scratch_shapes=[pltpu.SemaphoreType.DMA((3 * n,)), pltpu.SemaphoreType.DMA((3 * n,))],
        compiler_params=pltpu.CompilerParams(has_side_effects=True),
    )(*parts)


def _pair_join(bufs):
    n = len(bufs)

    def body(*refs):
        ins, outs = refs[:n], refs[n:2 * n]
        send, recv = refs[2 * n:]
        x, y, c = _place()
        sib = (x, y, 1 - c)
        rds = []
        for w in range(n):
            rd = pltpu.make_async_remote_copy(
                src_ref=ins[w].at[c], dst_ref=outs[w].at[c], send_sem=send.at[w], recv_sem=recv.at[w],
                device_id=sib, device_id_type=MESH)
            rd.start()
            rds.append(rd)
        for w in range(n):
            pltpu.make_async_remote_copy(
                src_ref=ins[w].at[c], dst_ref=outs[w].at[1 - c], send_sem=send.at[w], recv_sem=recv.at[w],
                device_id=sib, device_id_type=MESH).wait_recv()
        for rd in rds:
            rd.wait_send()

    return pl.pallas_call(
        body, name="grad_pair_join",
        in_specs=[HBM] * n, out_specs=[HBM] * n,
        out_shape=[jax.ShapeDtypeStruct(b.shape, b.dtype) for b in bufs],
        input_output_aliases={w: w for w in range(n)},
        scratch_shapes=[pltpu.SemaphoreType.DMA((n,)), pltpu.SemaphoreType.DMA((n,))],
        compiler_params=pltpu.CompilerParams(has_side_effects=True),
    )(*bufs)


def _all_reduce_small(packed):
    R = packed.shape[0]

    def body(x_ref, o_ref, buf, send, recv):
        x, y, c = _place()
        me = 4 * x + 2 * y + c
        buf[me] = x_ref[...]
        rds = []
        for rel in range(1, N_DEV):
            fx, fy, fc = (rel >> 2) & 1, (rel >> 1) & 1, rel & 1
            peer = (1 - x if fx else x, 1 - y if fy else y, 1 - c if fc else c)
            rd = pltpu.make_async_remote_copy(
                src_ref=x_ref, dst_ref=buf.at[me], send_sem=send.at[rel - 1], recv_sem=recv.at[rel - 1],
                device_id=peer, device_id_type=MESH)
            rd.start()
            rds.append((rd, peer))
        for rel in range(1, N_DEV):
            rd, (px, py, pc) = rds[rel - 1]
            pltpu.make_async_remote_copy(
                src_ref=x_ref, dst_ref=buf.at[4 * px + 2 * py + pc], send_sem=send.at[rel - 1], recv_sem=recv.at[rel - 1],
                device_id=(px, py, pc), device_id_type=MESH).wait_recv()
        for rd, _ in rds:
            rd.wait_send()
        total = buf[0]
        for d in range(1, N_DEV):
            total = total + buf[d]
        o_ref[...] = total

    return pl.pallas_call(
        body, name="all_reduce_small",
        in_specs=[pl.BlockSpec(memory_space=pltpu.VMEM)],
        out_specs=pl.BlockSpec(memory_space=pltpu.VMEM),
        out_shape=jax.ShapeDtypeStruct(packed.shape, F32),
        scratch_shapes=[pltpu.VMEM((N_DEV, R, LANES), F32), pltpu.SemaphoreType.DMA((N_DEV - 1,)),
                        pltpu.SemaphoreType.DMA((N_DEV - 1,))],
        compiler_params=pltpu.CompilerParams(has_side_effects=True),
    )(packed)


def _pair_sum(name, split, theirs, core):
    n_chip, _, Rh, C = split.shape
    tile = _div_tile(Rh, 256, 2 * SUBLANES)
    nt = Rh // tile

    def body(core_ref, a_ref, b_ref, o_ref):
        o_ref[...] = (a_ref[...] + b_ref[...]).astype(o_ref.dtype)

    return pl.pallas_call(
        body, name=name,
        grid_spec=pltpu.PrefetchScalarGridSpec(
            num_scalar_prefetch=1,
            grid=(n_chip, nt),
            in_specs=[pl.BlockSpec((None, None, tile, C), lambda j, i, core_ref: (j, core_ref[0], i, 0)),
                      pl.BlockSpec((None, tile, C), lambda j, i, core_ref: (j, i, 0))],
            out_specs=pl.BlockSpec((None, tile, C), lambda j, i, core_ref: (j, i, 0)),
        ),
        out_shape=jax.ShapeDtypeStruct((n_chip, Rh, C), BF16),
        compiler_params=_params(("parallel", "parallel")),
    )(core, split, theirs)


def _chip_sum(name, own, landed, core):
    n_in, Rh, C = landed.shape
    tile = _div_tile(Rh, 256, 2 * SUBLANES)

    def body(core_ref, *refs):
        total = refs[0][...].astype(F32)
        for ref in refs[1:n_in + 1]:
            total = total + ref[...].astype(F32)
        refs[n_in + 1][...] = total

    slot = lambda j: pl.BlockSpec((None, tile, C), lambda i, core_ref: (j, i, 0))
    return pl.pallas_call(
        body, name=name,
        grid_spec=pltpu.PrefetchScalarGridSpec(
            num_scalar_prefetch=1,
            grid=(Rh // tile,),
            in_specs=[pl.BlockSpec((None, tile, C), lambda i, core_ref: (core_ref[1], i, 0))]
                     + [slot(j) for j in range(n_in)],
            out_specs=pl.BlockSpec((None, tile, C), lambda i, core_ref: (core_ref[0], i, 0)),
        ),
        out_shape=jax.ShapeDtypeStruct((2, Rh, C), F32),
        compiler_params=_params(("parallel",)),
    )(core, own, *([landed] * n_in))


def _adamw(name, w, g, m, v):
    R, C = w.shape
    tile = _div_tile(R, 256, SUBLANES)
    c1 = 1.0 / (1.0 - ADAM_B1 ** ADAM_STEP)
    c2 = 1.0 / (1.0 - ADAM_B2 ** ADAM_STEP)

    def body(w_ref, g_ref, m_ref, v_ref, d_ref, nm_ref, nv_ref):
        g_ = g_ref[...]
        nm = ADAM_B1 * m_ref[...] + (1.0 - ADAM_B1) * g_
        nv = ADAM_B2 * v_ref[...] + (1.0 - ADAM_B2) * (g_ * g_)
        d_ref[...] = -ADAM_LR * ((nm * c1) / (jnp.sqrt(nv * c2) + ADAM_EPS) + ADAM_WD * w_ref[...])
        nm_ref[...] = nm
        nv_ref[...] = nv

    spec = pl.BlockSpec((tile, C), lambda i: (i, 0))
    return pl.pallas_call(
        body, name=name,
        grid=(R // tile,),
        in_specs=[spec] * 4, out_specs=[spec] * 3,
        out_shape=[jax.ShapeDtypeStruct((R, C), F32)] * 3,
        compiler_params=_params(("parallel",)),
    )(w, g, m, v)


def _cols_to_shards(full):
    K, N = full.shape
    return full.reshape(K, N_CHIPS, N // N_CHIPS).transpose(1, 0, 2)


def _shards_to_cols(sh):
    return sh.transpose(1, 0, 2).reshape(sh.shape[1], -1)


def _rows_to_shards(full):
    return full.reshape(N_CHIPS, full.shape[0] // N_CHIPS, full.shape[1])


SMALL = ["norm_mix_pre", "b_gate", "mu_rw", "w0", "a0", "k_k", "k_a", "r_k", "lnx_w", "lnx_b",
         "norm_mix_post", "norm_ffn_pre", "norm_ffn_post"]
BIG = ["w_in", "w_up", "a_up", "g_up", "w_sb_out", "w_rw_out", "w_o", "w_ffn_gate", "w_ffn_up", "w_ffn_down"]
ROW_SHARDED = ("w_o", "w_ffn_down")
ORDER = ["norm_mix_pre", "w_in", "b_gate", "mu_rw", "w0", "w_up", "a0", "a_up", "g_up", "k_k", "k_a", "r_k",
         "lnx_w", "lnx_b", "w_sb_out", "w_rw_out", "w_o", "norm_mix_post", "norm_ffn_pre", "w_ffn_gate",
         "w_ffn_up", "w_ffn_down", "norm_ffn_post"]


def _pack_small(vals, extra_rows=0):
    rows = jnp.concatenate([vals[n].reshape(-1, LANES) for n in SMALL], axis=0)
    pad = (-(rows.shape[0] + extra_rows)) % SUBLANES + extra_rows
    return jnp.pad(rows, ((0, pad), (0, 0)))


def _unpack_small(packed, shapes):
    out, r = {}, 0
    for n in SMALL:
        size = 1
        for s in shapes[n]:
            size *= s
        out[n] = packed[r:r + size // LANES].reshape(shapes[n])
        r += size // LANES
    return out


def kernel(x, norm_mix_pre, w_in, b_gate, mu_rw, w0, w_up, a0, a_up, g_up, k_k, k_a, r_k, lnx_w, lnx_b, w_sb_out, w_rw_out, w_o, norm_mix_post, norm_ffn_pre, w_ffn_gate, w_ffn_up, w_ffn_down, norm_ffn_post, loss_target, m_norm_mix_pre, m_w_in, m_b_gate, m_mu_rw, m_w0, m_w_up, m_a0, m_a_up, m_g_up, m_k_k, m_k_a, m_r_k, m_lnx_w, m_lnx_b, m_w_sb_out, m_w_rw_out, m_w_o, m_norm_mix_post, m_norm_ffn_pre, m_w_ffn_gate, m_w_ffn_up, m_w_ffn_down, m_norm_ffn_post, v_norm_mix_pre, v_w_in, v_b_gate, v_mu_rw, v_w0, v_w_up, v_a0, v_a_up, v_g_up, v_k_k, v_k_a, v_r_k, v_lnx_w, v_lnx_b, v_w_sb_out, v_w_rw_out, v_w_o, v_norm_mix_post, v_norm_ffn_pre, v_w_ffn_gate, v_w_ffn_up, v_w_ffn_down, v_norm_ffn_post):
    W = dict(norm_mix_pre=norm_mix_pre, w_in=w_in, b_gate=b_gate, mu_rw=mu_rw, w0=w0, w_up=w_up, a0=a0, a_up=a_up,
             g_up=g_up, k_k=k_k, k_a=k_a, r_k=r_k, lnx_w=lnx_w, lnx_b=lnx_b, w_sb_out=w_sb_out, w_rw_out=w_rw_out,
             w_o=w_o, norm_mix_post=norm_mix_post, norm_ffn_pre=norm_ffn_pre, w_ffn_gate=w_ffn_gate,
             w_ffn_up=w_ffn_up, w_ffn_down=w_ffn_down, norm_ffn_post=norm_ffn_post)
    Mo = dict(norm_mix_pre=m_norm_mix_pre, w_in=m_w_in, b_gate=m_b_gate, mu_rw=m_mu_rw, w0=m_w0, w_up=m_w_up, a0=m_a0,
              a_up=m_a_up, g_up=m_g_up, k_k=m_k_k, k_a=m_k_a, r_k=m_r_k, lnx_w=m_lnx_w, lnx_b=m_lnx_b,
              w_sb_out=m_w_sb_out, w_rw_out=m_w_rw_out, w_o=m_w_o, norm_mix_post=m_norm_mix_post,
              norm_ffn_pre=m_norm_ffn_pre, w_ffn_gate=m_w_ffn_gate, w_ffn_up=m_w_ffn_up, w_ffn_down=m_w_ffn_down,
              norm_ffn_post=m_norm_ffn_post)
    Vo = dict(norm_mix_pre=v_norm_mix_pre, w_in=v_w_in, b_gate=v_b_gate, mu_rw=v_mu_rw, w0=v_w0, w_up=v_w_up, a0=v_a0,
              a_up=v_a_up, g_up=v_g_up, k_k=v_k_k, k_a=v_k_a, r_k=v_r_k, lnx_w=v_lnx_w, lnx_b=v_lnx_b,
              w_sb_out=v_w_sb_out, w_rw_out=v_w_rw_out, w_o=v_w_o, norm_mix_post=v_norm_mix_post,
              norm_ffn_pre=v_norm_ffn_pre, w_ffn_gate=v_w_ffn_gate, w_ffn_up=v_w_ffn_up, w_ffn_down=v_w_ffn_down,
              norm_ffn_post=v_norm_ffn_post)
    shapes = {n: W[n].shape for n in ORDER}
    B, S, D = x.shape
    T = B * S
    x2 = x.reshape(T, D)
    tgt = loss_target.reshape(T, D)
    vec = {n: W[n].reshape(1, -1) for n in SMALL}

    halved = [W[n][0].astype(BF16).reshape(2, W[n].shape[1] // 2, W[n].shape[2]) for n in BIG]
    full = {}
    for n, gth in zip(BIG, _all_gather_chips(halved)):
        gth = gth.reshape((N_CHIPS,) + W[n].shape[1:])
        full[n] = gth.reshape(-1, gth.shape[2]) if n in ROW_SHARDED else _shards_to_cols(gth)
    w_sb, w_rw, w_gt = full["w_in"][:, :SB_COLS], full["w_in"][:, SB_COLS:SB_COLS + RW_COLS], full["w_in"][:, SB_COLS + RW_COLS:]
    lora_rows = {"w_up": 0, "a_up": 64, "g_up": 128}
    lora = {n: jnp.pad(full[n], ((r0, LORA_COLS - r0 - full[n].shape[0]), (0, 0))) for n, r0 in lora_rows.items()}
    mu = vec["mu_rw"]
    mu_parts = [mu[:, :512], mu[:, 512:1024], mu[:, 1024:1536], mu[:, 1536:]]
    b1, b2 = vec["b_gate"][:, :D], vec["b_gate"][:, D:]

    (h1,) = _rowwise("norm_mix_pre", _f_norm, [(x2, D, 0)], [vec["norm_mix_pre"]], [(D, BF16)])
    p_sb = _mm("proj_sb", h1, w_sb)
    p_rw = _mm("proj_rw", h1, w_rw)
    p_gt = _mm("proj_gate", h1, w_gt)
    o_sb = _attn_fwd(p_sb, B, S)
    pre_params = mu_parts + [vec["w0"], lora["w_up"], vec["a0"], lora["a_up"], lora["g_up"], vec["k_k"], vec["k_a"]]
    r_, lw_, k2_, v_, kap_, a_, g_ = _rw_pre(p_rw, pre_params, S)
    y_wkv, states = _wkv_fwd(r_, lw_, k2_, v_, kap_, a_, B, S)
    post_rows = [(y_wkv, 512, 0), (r_, 512, 0), (k2_, 512, 0), (v_, 512, 0), (g_, 512, 0)]
    post_params = [vec["lnx_w"], vec["lnx_b"], vec["r_k"]]
    (o_rw,) = _rowwise("rw_post", _f_rwpost, post_rows, post_params, [(512, BF16)])
    m1 = _mm("mix_sb_out", o_sb, full["w_sb_out"])
    m2 = _mm("mix_rw_out", o_rw, full["w_rw_out"])
    merge_rows = [(p_gt, D, 0), (p_gt, D, 1), (m1, D, 0), (m2, D, 0)]
    (merged,) = _rowwise("merge", _f_merge, merge_rows, [b1, b2], [(D, BF16)])
    u = _mm("mix_out", merged, full["w_o"])
    post1_params = [vec["norm_mix_post"], vec["norm_ffn_pre"]]
    x1, h2 = _rowwise("post_mix", _f_post1, [(x2, D, 0), (u, D, 0)], post1_params, [(D, F32), (D, BF16)])
    ag = _mm("ffn_gate", h2, full["w_ffn_gate"], out_dtype=BF16)
    au = _mm("ffn_up", h2, full["w_ffn_up"], out_dtype=BF16)
    (sw,) = _rowwise("swiglu", _f_swiglu, [(ag, D_FF, 0), (au, D_FF, 0)], [], [(D_FF, BF16)])
    f = _mm("ffn_down", sw, full["w_ffn_down"])
    loss_part, dx1, df, dg4 = _loss_head(x1, f, tgt, vec["norm_ffn_post"])

    gbig, gsmall = {}, {"norm_ffn_post": dg4}
    dsw = _mm("d_swiglu_out", df, full["w_ffn_down"], tb=True, out_dtype=BF16)
    gbig["w_ffn_down"] = _mm("g_ffn_down", sw, df, ta=True)
    (dag, dau), _ = _rowwise_vjp("swiglu_bwd", _f_swiglu, [(ag, D_FF, 0), (au, D_FF, 0)], [], [[dsw]], [True, True], [],
                                 bf16_rows=(0, 1))
    dh2 = _mm("d_h2_gate", dag, full["w_ffn_gate"], tb=True)
    dh2 = _mm("d_h2_up", dau, full["w_ffn_up"], tb=True, acc=dh2)
    gbig["w_ffn_gate"] = _mm("g_ffn_gate", h2, dag, ta=True)
    gbig["w_ffn_up"] = _mm("g_ffn_up", h2, dau, ta=True)
    (dx_res, du), (dg2, dg3) = _rowwise_vjp("post_mix_bwd", _f_post1, [(x2, D, 0), (u, D, 0)], post1_params,
                                            [[dx1], [dh2]], [True, True], [True, True], bf16_rows=(1,))
    gsmall["norm_mix_post"], gsmall["norm_ffn_pre"] = dg2, dg3
    dmerged = _mm("d_merged", du, full["w_o"], tb=True)
    gbig["w_o"] = _mm("g_w_o", merged, du, ta=True)
    (dpg1, dpg2, dm1, dm2), (db1, db2) = _rowwise_vjp("merge_bwd", _f_merge, merge_rows, [b1, b2], [[dmerged]],
                                                      [True] * 4, [True, True], bf16_rows=(0, 1, 2, 3))
    gsmall["b_gate"] = jnp.concatenate([db1, db2], axis=1)
    do_sb = _mm("d_o_sb", dm1, full["w_sb_out"], tb=True)
    do_rw = _mm("d_o_rw", dm2, full["w_rw_out"], tb=True)
    gbig["w_sb_out"] = _mm("g_sb_out", o_sb, dm1, ta=True)
    gbig["w_rw_out"] = _mm("g_rw_out", o_rw, dm2, ta=True)
    (dy_wkv, dr_a, dk2_a, dv_a, dg_), (dlnx_w, dlnx_b, dr_k) = _rowwise_vjp(
        "rw_post_bwd", _f_rwpost, post_rows, post_params, [[do_rw]], [True] * 5, [True] * 3)
    gsmall["lnx_w"], gsmall["lnx_b"], gsmall["r_k"] = dlnx_w, dlnx_b, dr_k
    dr_b, dlw, dk2_b, dv_b, dkap, da = _wkv_bwd(r_, lw_, k2_, v_, kap_, a_, states, dy_wkv, B, S)
    pre_cts = [[dr_a, dr_b], [dlw], [dk2_a, dk2_b], [dv_a, dv_b], [dkap], [da], [dg_]]
    dp_rw, dpre_params = _rw_pre_bwd(p_rw, pre_params, pre_cts, S)
    gsmall["mu_rw"] = jnp.concatenate(dpre_params[:4], axis=1)
    gsmall["w0"], gsmall["a0"], gsmall["k_k"], gsmall["k_a"] = dpre_params[4], dpre_params[6], dpre_params[9], dpre_params[10]
    glora = {"w_up": dpre_params[5][0:64], "a_up": dpre_params[7][64:128], "g_up": dpre_params[8][128:256]}
    dq, dk, dv = _attn_bwd(p_sb, o_sb, do_sb, B, S)
    dh1 = _mm("d_h1_q", dq, w_sb[:, :512], tb=True)
    dh1 = _mm("d_h1_k", dk, w_sb[:, 512:1024], tb=True, acc=dh1)
    dh1 = _mm("d_h1_v", dv, w_sb[:, 1024:], tb=True, acc=dh1)
    dh1 = _mm("d_h1_rw", dp_rw, w_rw, tb=True, acc=dh1)
    dh1 = _mm("d_h1_g1", dpg1, w_gt[:, :D], tb=True, acc=dh1)
    dh1 = _mm("d_h1_g2", dpg2, w_gt[:, D:], tb=True, acc=dh1)
    gbig["w_in"] = jnp.concatenate(
        [_mm("g_in_" + tag, h1, d, ta=True)
         for tag, d in (("q", dq), ("k", dk), ("v", dv), ("rw", dp_rw), ("g1", dpg1), ("g2", dpg2))], axis=1)
    (grad_x2,), (dg1,) = _rowwise_vjp("norm_mix_pre_bwd", _f_norm, [(x2, D, 0)], [vec["norm_mix_pre"]], [[dh1]],
                                      [True], [True], add_to={0: dx_res})
    gsmall["norm_mix_pre"] = dg1
    gbig.update(glora)

    split = []
    for n in BIG:
        g = _rows_to_shards(gbig[n]) if n in ROW_SHARDED else _cols_to_shards(gbig[n])
        split.append(g.reshape(N_CHIPS, 2, g.shape[1] // 2, g.shape[2]))
    core = jnp.stack([lax.axis_index("c"), 2 * lax.axis_index("x") + lax.axis_index("y")]).astype(jnp.int32)
    theirs = _pair_split(split)
    chip_sums = [_pair_sum("pair_sum_" + n, a, b, core) for n, a, b in zip(BIG, split, theirs)]
    landed = _chip_scatter(chip_sums)
    joined = _pair_join([_chip_sum("chip_sum_" + n, own, got, core) for n, own, got in zip(BIG, chip_sums, landed)])
    grads = {n: j.reshape(W[n].shape[1:]) for n, j in zip(BIG, joined)}

    small_local = _pack_small({n: gsmall[n] for n in SMALL}, extra_rows=1)
    loss_row = small_local.shape[0] - 1
    small_local = small_local.at[loss_row].set(loss_part[0])
    small_sum = _all_reduce_small(small_local)
    loss = small_sum[loss_row, 0]

    delta, new_m, new_v = {}, {}, {}
    for n in BIG:
        d_, m_, v2_ = _adamw("adamw_" + n, W[n][0], grads[n], Mo[n][0], Vo[n][0])
        delta[n], new_m[n], new_v[n] = d_[None], m_[None], v2_[None]
        grads[n] = grads[n][None]
    pk = lambda src: _pack_small({n: src[n] for n in SMALL}, extra_rows=1)
    d_s, m_s, v_s = _adamw("adamw_small", pk(W), small_sum.at[loss_row].set(0.0), pk(Mo), pk(Vo))
    for dst, packed in ((grads, small_sum), (delta, d_s), (new_m, m_s), (new_v, v_s)):
        dst.update(_unpack_small(packed, shapes))

    return (loss, grad_x2.reshape(B, S, D), *[grads[n] for n in ORDER], *[delta[n] for n in ORDER],
            *[new_m[n] for n in ORDER], *[new_v[n] for n in ORDER])
```

```python
import functools

import jax
import jax.numpy as jnp
from jax import lax
from jax.experimental import pallas as pl
from jax.experimental.pallas import tpu as pltpu

F32 = jnp.float32
BF16 = jnp.bfloat16
MESH = pl.DeviceIdType.MESH

D_MODEL = 1024
SB_HEADS = 8
HEAD_DIM = 64
SB_WIDTH = SB_HEADS * HEAD_DIM
RW_WIDTH = 512
LORA_COLS = 256
SB_COLS = 3 * SB_WIDTH
RW_COLS = 3 * RW_WIDTH + LORA_COLS
GATE_COLS = 2 * D_MODEL
D_FF = 2816
RMS_EPS = 1e-6
GN_EPS = HEAD_DIM * 1e-5
WKV_CHUNK = 64
WKV_SEQS = 4
ATTN_QUERIES = 512
ATTN_KEYS = 128
LANES = 128
SUBLANES = 8
N_CHIPS = 4
N_DEV = 8

ADAM_LR = 0.001
ADAM_B1 = 0.9
ADAM_B2 = 0.999
ADAM_EPS = 1e-08
ADAM_WD = 0.01
ADAM_STEP = 10

VMEM_LIMIT = 48 * 1024 * 1024
WKV_BWD_VMEM = 58 * 1024 * 1024


def _params(sem=None, vmem=VMEM_LIMIT, **kw):
    if sem is not None:
        kw["dimension_semantics"] = sem
    return pltpu.CompilerParams(vmem_limit_bytes=vmem, **kw)


def _div_tile(dim, pref, mult=LANES):
    if dim <= pref:
        return dim
    t = pref - pref % mult
    while t >= mult:
        if dim % t == 0:
            return t
        t -= mult
    return dim


def _dot(a, b, dims):
    return lax.dot_general(a, b, (dims, ((), ())), preferred_element_type=F32)


def _mm(name, a, b, *, ta=False, tb=False, acc=None, out_dtype=F32):
    if ta:
        K, M = a.shape
    else:
        M, K = a.shape
    N = b.shape[0] if tb else b.shape[1]
    if ta:
        tm, tn, tk = _div_tile(M, 1408), _div_tile(N, 1408), _div_tile(K, 512)
    else:
        tm, tn, tk = _div_tile(M, 512), _div_tile(N, 1408), _div_tile(K, 1408)
    nk = K // tk
    dims = ((0,) if ta else (1,), (1,) if tb else (0,))
    has_acc = acc is not None

    def body(*refs):
        a_ref, b_ref = refs[0], refs[1]
        part = _dot(a_ref[...].astype(BF16), b_ref[...].astype(BF16), dims)
        if nk == 1:
            o_ref = refs[-1]
            o_ref[...] = (part + refs[2][...] if has_acc else part).astype(o_ref.dtype)
            return
        o_ref, scr = refs[-2], refs[-1]
        k = pl.program_id(2)

        @pl.when(k == 0)
        def _():
            scr[...] = part + refs[2][...] if has_acc else part

        @pl.when(k > 0)
        def _():
            scr[...] += part

        @pl.when(k == nk - 1)
        def _():
            o_ref[...] = scr[...].astype(o_ref.dtype)

    a_spec = pl.BlockSpec((tk, tm), lambda i, j, k: (k, i)) if ta else pl.BlockSpec((tm, tk), lambda i, j, k: (i, k))
    b_spec = pl.BlockSpec((tn, tk), lambda i, j, k: (j, k)) if tb else pl.BlockSpec((tk, tn), lambda i, j, k: (k, j))
    o_spec = pl.BlockSpec((tm, tn), lambda i, j, k: (i, j))
    return pl.pallas_call(
        body, name=name,
        grid=(M // tm, N // tn, nk),
        in_specs=[a_spec, b_spec] + ([o_spec] if has_acc else []),
        out_specs=o_spec,
        out_shape=jax.ShapeDtypeStruct((M, N), out_dtype),
        scratch_shapes=[pltpu.VMEM((tm, tn), F32)] if nk > 1 else [],
        compiler_params=_params(("parallel", "parallel", "arbitrary")),
    )(*([a, b] + ([acc] if has_acc else [])))


def _mm_fused(name, lhs, rhs, outs, *, tb=False, add=False, extras=(), epilogue=None):
    M, K = lhs[0].shape
    N = rhs[0].shape[0] if tb else rhs[0].shape[1]
    tm, tn, tk = _div_tile(M, 512), _div_tile(N, 1408), _div_tile(K, 1408)
    nk = K // tk
    n_l, n_e, n_o = len(lhs), len(extras), len(outs)
    n_acc = 1 if add else n_l
    dims = ((1,), (1,) if tb else (0,))

    def body(*refs):
        l_refs, r_refs = refs[:n_l], refs[n_l:2 * n_l]
        e_refs = refs[2 * n_l:2 * n_l + n_e]
        o_refs = refs[2 * n_l + n_e:2 * n_l + n_e + n_o]
        scr = refs[2 * n_l + n_e + n_o:]
        parts = [_dot(l[...].astype(BF16), r[...].astype(BF16), dims) for l, r in zip(l_refs, r_refs)]
        if add:
            parts = [functools.reduce(lambda u, v: u + v, parts)]

        def finish(vals):
            res = epilogue(vals, [e[...].astype(F32) for e in e_refs]) if epilogue else vals
            for ref, val in zip(o_refs, res):
                ref[...] = val.astype(ref.dtype)

        if nk == 1:
            finish(parts)
            return
        k = pl.program_id(2)

        @pl.when(k == 0)
        def _():
            for s, part in zip(scr, parts):
                s[...] = part

        @pl.when(k > 0)
        def _():
            for s, part in zip(scr, parts):
                s[...] += part

        @pl.when(k == nk - 1)
        def _():
            finish([s[...] for s in scr])

    a_spec = pl.BlockSpec((tm, tk), lambda i, j, k: (i, k))
    b_spec = pl.BlockSpec((tn, tk), lambda i, j, k: (j, k)) if tb else pl.BlockSpec((tk, tn), lambda i, j, k: (k, j))
    o_spec = pl.BlockSpec((tm, tn), lambda i, j, k: (i, j))
    return pl.pallas_call(
        body, name=name,
        grid=(M // tm, N // tn, nk),
        in_specs=[a_spec] * n_l + [b_spec] * n_l + [o_spec] * n_e,
        out_specs=[o_spec] * n_o,
        out_shape=[jax.ShapeDtypeStruct((M, N), dt) for dt in outs],
        scratch_shapes=[pltpu.VMEM((tm, tn), F32)] * (n_acc if nk > 1 else 0),
        compiler_params=_params(("parallel", "parallel", "arbitrary")),
    )(*lhs, *rhs, *extras)


def _row_spec(tile, width, colblk):
    return pl.BlockSpec((tile, width), lambda i: (i, colblk))


def _full_spec(shape):
    return pl.BlockSpec(shape, lambda i: (0,) * len(shape))


def _rowwise(name, fn, rows, params, outs, tile=256):
    T = rows[0][0].shape[0]
    tile = min(tile, T)
    n_r, n_p = len(rows), len(params)

    def body(*refs):
        r = [x[...].astype(F32) for x in refs[:n_r]]
        p = [x[...].astype(F32) for x in refs[n_r:n_r + n_p]]
        for o_ref, val in zip(refs[n_r + n_p:], fn(*r, *p)):
            o_ref[...] = val.astype(o_ref.dtype)

    return pl.pallas_call(
        body, name=name,
        grid=(T // tile,),
        in_specs=[_row_spec(tile, w, cb) for _, w, cb in rows] + [_full_spec(p.shape) for p in params],
        out_specs=[_row_spec(tile, w, 0) for w, _ in outs],
        out_shape=[jax.ShapeDtypeStruct((T, w), dt) for w, dt in outs],
        compiler_params=_params(("parallel",)),
    )(*([a for a, _, _ in rows] + list(params)))


def _rowwise_vjp(name, fn, rows, params, cts, need_rows, need_params, add_to=None, tile=256, bf16_rows=()):
    add_to = add_to or {}
    T = rows[0][0].shape[0]
    tile = min(tile, T)
    n_r, n_p = len(rows), len(params)
    ct_flat = [c for group in cts for c in group]
    ct_sizes = [len(group) for group in cts]
    add_idx = sorted(add_to)
    row_out = [i for i in range(n_r) if need_rows[i]]
    par_out = [i for i in range(n_p) if need_params[i]]
    n_ct, n_add = len(ct_flat), len(add_idx)

    def body(*refs):
        pos = 0
        r = [x[...].astype(F32) for x in refs[pos:pos + n_r]]
        pos += n_r
        p = [x[...].astype(F32) for x in refs[pos:pos + n_p]]
        pos += n_p
        ct_vals = [x[...].astype(F32) for x in refs[pos:pos + n_ct]]
        pos += n_ct
        adds = {i: x[...] for i, x in zip(add_idx, refs[pos:pos + n_add])}
        pos += n_add
        drow_refs = refs[pos:pos + len(row_out)]
        pos += len(row_out)
        dpar_refs = refs[pos:pos + len(par_out)]
        ct_in, q = [], 0
        for n in ct_sizes:
            ct_in.append(functools.reduce(lambda u, v: u + v, ct_vals[q:q + n]))
            q += n
        _, vjp = jax.vjp(fn, *r, *p)
        grads = vjp(tuple(ct_in))
        for ref, i in zip(drow_refs, row_out):
            g = grads[i]
            ref[...] = (g + adds[i] if i in adds else g).astype(ref.dtype)

        @pl.when(pl.program_id(0) == 0)
        def _():
            for ref in dpar_refs:
                ref[...] = jnp.zeros_like(ref)

        for ref, i in zip(dpar_refs, par_out):
            ref[...] += grads[n_r + i]

    ct_widths = [c.shape[1] for c in ct_flat]
    in_specs = ([_row_spec(tile, w, cb) for _, w, cb in rows] + [_full_spec(p.shape) for p in params]
                + [_row_spec(tile, w, 0) for w in ct_widths] + [_row_spec(tile, rows[i][1], 0) for i in add_idx])
    out_specs = [_row_spec(tile, rows[i][1], 0) for i in row_out] + [_full_spec(params[i].shape) for i in par_out]
    out_shape = ([jax.ShapeDtypeStruct((T, rows[i][1]), BF16 if i in bf16_rows else F32) for i in row_out]
                 + [jax.ShapeDtypeStruct(params[i].shape, F32) for i in par_out])
    res = pl.pallas_call(
        body, name=name,
        grid=(T // tile,),
        in_specs=in_specs, out_specs=out_specs, out_shape=out_shape,
        compiler_params=_params(("arbitrary",)),
    )(*([a for a, _, _ in rows] + list(params) + ct_flat + [add_to[i] for i in add_idx]))
    return res[:len(row_out)], res[len(row_out):]


def _sigmoid(x):
    return 0.5 * (jnp.tanh(0.5 * x) + 1.0)


def _softplus(x):
    return jnp.maximum(x, 0.0) + jnp.log(1.0 + jnp.exp(-jnp.abs(x)))


def _rms(x, g):
    return x * lax.rsqrt(jnp.mean(x * x, axis=-1, keepdims=True) + RMS_EPS) * g


def _segsum_impl(x):
    n = x.shape[-1]
    r = lax.shift_right_logical(lax.broadcasted_iota(jnp.int32, (n, n), 0), 6)
    c = lax.shift_right_logical(lax.broadcasted_iota(jnp.int32, (n, n), 1), 6)
    bd = (r == c).astype(BF16)
    hi = x.astype(BF16)
    rest = x - hi.astype(F32)
    mid = rest.astype(BF16)
    lo = (rest - mid.astype(F32)).astype(BF16)
    nn = ((1,), (0,))
    return _dot(hi, bd, nn) + _dot(mid, bd, nn) + _dot(lo, bd, nn)


@jax.custom_vjp
def _segsum(x):
    return _segsum_impl(x)


_segsum.defvjp(lambda x: (_segsum_impl(x), None), lambda _, g: (_segsum_impl(g),))


@jax.custom_vjp
def _mmb(a, w):
    return _dot(a.astype(BF16), w.astype(BF16), ((1,), (0,)))


def _mmb_fwd(a, w):
    return _mmb(a, w), (a, w)


def _mmb_bwd(res, g):
    a, w = res
    gb = g.astype(BF16)
    return _dot(gb, w.astype(BF16), ((1,), (1,))), _dot(a.astype(BF16), gb, ((0,), (0,)))


_mmb.defvjp(_mmb_fwd, _mmb_bwd)


def _f_norm(x, g):
    return (_rms(x, g),)


def _f_post1(x, u, g2, g3):
    x1 = x + _rms(u, g2)
    return x1, _rms(x1, g3)


def _f_swiglu(ag, au):
    return (ag * _sigmoid(ag) * au,)


def _f_merge(pg1, pg2, m1, m2, b1, b2):
    return (_sigmoid(pg1 + b1) * m1 + _sigmoid(pg2 + b2) * m2,)


def _f_out(x1, f, g4):
    return (x1 + _rms(f, g4),)


def _f_rwpre(pr, pk, pv, pz, qr, qk, qv, qz, mur, muk, muv, muz, w0, wup, a0, aup, gup, k_k, k_a):
    r = pr + (qr - pr) * mur
    k = pk + (qk - pk) * muk
    v = pv + (qv - pv) * muv
    z = pz + (qz - pz) * muz
    w_raw = w0 + _mmb(jnp.tanh(z), wup)
    lw = -jnp.exp(-_softplus(-w_raw) - 0.5)
    a = _sigmoid(a0 + _mmb(z, aup))
    g = _mmb(_sigmoid(z), gup)
    kk = k * k_k
    kap = kk * lax.rsqrt(jnp.maximum(_segsum(kk * kk), 1e-24))
    k2 = k * (1.0 + (a - 1.0) * k_a)
    return r, lw, k2, v, kap, a, g


def _f_rwpost(y, r, k2, v, g, lnx_w, lnx_b, r_k):
    inv = 1.0 / HEAD_DIM
    yc = y - _segsum(y) * inv
    var = _segsum(yc * yc) * inv
    yn = yc * lax.rsqrt(var + GN_EPS) * lnx_w + lnx_b
    bonus = _segsum(r * k2 * r_k) * v
    return ((yn + bonus) * g,)


RW_GROUPS = (0, 512, 1024, 1536, RW_COLS)


def _column_groups(p):
    return [p[:, a:b] for a, b in zip(RW_GROUPS[:-1], RW_GROUPS[1:])]


def _previous_tokens(p, halo, first_of_sequence):
    rows = lax.broadcasted_iota(jnp.int32, (p.shape[0], 1), 0)
    before = jnp.where(first_of_sequence, 0.0, halo[SUBLANES - 1:SUBLANES, :])
    return jnp.where(rows == 0, before, pltpu.roll(p, 1, axis=0))


def _halo_spec(tile, order):
    per = tile // SUBLANES
    return pl.BlockSpec((SUBLANES, RW_COLS), lambda i: (jnp.maximum(order(i) * per - 1, 0), 0))


def _rw_pre(p_rw, params, S, tile=128):
    T = p_rw.shape[0]
    tile = min(tile, T)
    assert S % tile == 0
    n_p = len(params)

    def body(*refs):
        p_ref, halo_ref = refs[0], refs[1]
        par = [x[...].astype(F32) for x in refs[2:2 + n_p]]
        p = p_ref[...]
        first = lax.rem(pl.program_id(0) * tile, S) == 0
        prev = _previous_tokens(p, halo_ref[...], first)
        for o_ref, val in zip(refs[2 + n_p:], _f_rwpre(*_column_groups(p), *_column_groups(prev), *par)):
            o_ref[...] = val

    out_spec = pl.BlockSpec((tile, RW_WIDTH), lambda i: (i, 0))
    return pl.pallas_call(
        body, name="rw_pre",
        grid=(T // tile,),
        in_specs=[pl.BlockSpec((tile, RW_COLS), lambda i: (i, 0)), _halo_spec(tile, lambda i: i)]
                 + [_full_spec(q.shape) for q in params],
        out_specs=[out_spec] * 7,
        out_shape=[jax.ShapeDtypeStruct((T, RW_WIDTH), F32)] * 7,
        compiler_params=_params(("parallel",)),
    )(p_rw, p_rw, *params)


def _rw_pre_bwd(p_rw, params, cts, S, tile=128):
    T = p_rw.shape[0]
    tile = min(tile, T)
    assert S % tile == 0
    nt = T // tile
    n_p = len(params)
    ct_flat = [c for group in cts for c in group]
    ct_sizes = [len(group) for group in cts]
    n_ct = len(ct_flat)

    def body(*refs):
        p_ref, halo_ref = refs[0], refs[1]
        par = [x[...].astype(F32) for x in refs[2:2 + n_p]]
        ct_vals = [x[...] for x in refs[2 + n_p:2 + n_p + n_ct]]
        dp_ref = refs[2 + n_p + n_ct]
        dpar_refs = refs[3 + n_p + n_ct:3 + 2 * n_p + n_ct]
        carry = refs[-1]
        step = pl.program_id(0)

        @pl.when(step == 0)
        def _():
            carry[...] = jnp.zeros_like(carry)
            for ref in dpar_refs:
                ref[...] = jnp.zeros_like(ref)

        ct_in, q = [], 0
        for n in ct_sizes:
            ct_in.append(functools.reduce(lambda u, v: u + v, ct_vals[q:q + n]))
            q += n
        p = p_ref[...]
        first = lax.rem((nt - 1 - step) * tile, S) == 0
        prev = _previous_tokens(p, halo_ref[...], first)
        _, vjp = jax.vjp(_f_rwpre, *_column_groups(p), *_column_groups(prev), *par)
        grads = vjp(tuple(ct_in))
        d_here = jnp.concatenate(grads[0:4], axis=1)
        d_prev = jnp.concatenate(grads[4:8], axis=1)
        rows = lax.broadcasted_iota(jnp.int32, (tile, 1), 0)
        from_next = jnp.where(rows == tile - 1, carry[0:1, :], pltpu.roll(d_prev, tile - 1, axis=0))
        dp_ref[...] = (d_here + from_next).astype(dp_ref.dtype)
        carry[...] = jnp.broadcast_to(jnp.where(first, 0.0, d_prev[0:1, :]), carry.shape)
        for ref, g in zip(dpar_refs, grads[8:]):
            ref[...] += g

    back = lambda i: nt - 1 - i
    row = lambda w: pl.BlockSpec((tile, w), lambda i: (back(i), 0))
    res = pl.pallas_call(
        body, name="rw_pre_bwd",
        grid=(nt,),
        in_specs=[row(RW_COLS), _halo_spec(tile, back)] + [_full_spec(q.shape) for q in params]
                 + [row(RW_WIDTH)] * n_ct,
        out_specs=[row(RW_COLS)] + [_full_spec(q.shape) for q in params],
        out_shape=[jax.ShapeDtypeStruct((T, RW_COLS), BF16)] + [jax.ShapeDtypeStruct(q.shape, F32) for q in params],
        scratch_shapes=[pltpu.VMEM((SUBLANES, RW_COLS), F32)],
        compiler_params=_params(("arbitrary",)),
    )(p_rw, p_rw, *params, *ct_flat)
    return res[0], res[1:]


def _loss_head(x1, f, target, g4, tile=256):
    T, D = x1.shape
    tile = min(tile, T)

    def body(x1_ref, f_ref, t_ref, g_ref, loss_ref, dx1_ref, df_ref, dg_ref):
        (y,), vjp = jax.vjp(_f_out, x1_ref[...], f_ref[...], g_ref[...])
        err = y - t_ref[...]
        dx1, df, dg = vjp((err * (1.0 / D),))
        dx1_ref[...] = dx1
        df_ref[...] = df.astype(df_ref.dtype)

        @pl.when(pl.program_id(0) == 0)
        def _():
            loss_ref[...] = jnp.zeros_like(loss_ref)
            dg_ref[...] = jnp.zeros_like(dg_ref)

        part = jnp.sum(jnp.sum(err * err, axis=1, keepdims=True), axis=0, keepdims=True) * (0.5 / D)
        loss_ref[...] += jnp.broadcast_to(part, loss_ref.shape)
        dg_ref[...] += dg

    row = pl.BlockSpec((tile, D), lambda i: (i, 0))
    return pl.pallas_call(
        body, name="loss_head",
        grid=(T // tile,),
        in_specs=[row, row, row, _full_spec(g4.shape)],
        out_specs=[_full_spec((SUBLANES, LANES)), row, row, _full_spec(g4.shape)],
        out_shape=[jax.ShapeDtypeStruct((SUBLANES, LANES), F32), jax.ShapeDtypeStruct((T, D), F32),
                   jax.ShapeDtypeStruct((T, D), BF16), jax.ShapeDtypeStruct(g4.shape, F32)],
        compiler_params=_params(("arbitrary",)),
    )(x1, f, target, g4)


def _nn(a, b):
    return _dot(a, b, ((1,), (0,)))


def _nt(a, b):
    return _dot(a, b, ((1,), (1,)))


def _tn(a, b):
    return _dot(a, b, ((0,), (0,)))


def _split_dot(x, u2):
    hi = x.astype(BF16)
    lo = (x - hi.astype(F32)).astype(BF16)
    return _nn(jnp.concatenate([hi, lo], axis=1), u2)


def _by_head(x, masks):
    return jnp.concatenate([(x * m).astype(BF16) for m in masks], axis=0)


def _fold_heads(x2, masks):
    R = x2.shape[0] // len(masks)
    return functools.reduce(lambda u, v: u + v, [x2[h * R:(h + 1) * R] * m for h, m in enumerate(masks)])


def _head_masks():
    lane = lax.broadcasted_iota(jnp.int32, (1, LANES), 1)
    return [((lane >= h * HEAD_DIM) & (lane < (h + 1) * HEAD_DIM)).astype(F32) for h in range(LANES // HEAD_DIM)]


def _key_tri(op):
    row = lax.broadcasted_iota(jnp.int32, (ATTN_KEYS, ATTN_KEYS), 0)
    col = lax.broadcasted_iota(jnp.int32, (ATTN_KEYS, ATTN_KEYS), 1)
    u = op(row, col).astype(BF16)
    return jnp.concatenate([u, u], axis=0)


def _causal(rows):
    row = lax.broadcasted_iota(jnp.int32, (rows, ATTN_KEYS), 0)
    col = lax.broadcasted_iota(jnp.int32, (rows, ATTN_KEYS), 1)
    return col < row


def _from_row(tree, r):
    return jax.tree.map(lambda x: x[r:], tree)


def _onto_rows(old, new, r):
    return jax.tree.map(lambda o, n: jnp.concatenate([o[:r], n], axis=0) if r else n, old, new)


def _sb_weights(qb16, kbh, c_fails, u_gt, strict, scale):
    z_all = _nt(qb16, kbh) * scale
    zs = [z_all[:, h * ATTN_KEYS:(h + 1) * ATTN_KEYS] for h in range(len(c_fails))]
    Ls = [jnp.minimum(-z, 0.0) - jnp.log(1.0 + jnp.exp(-jnp.abs(z))) for z in zs]
    Lms = Ls if strict is None else [jnp.where(strict, L, 0.0) for L in Ls]
    cums = [_split_dot(Lm, u_gt) for Lm in Lms]
    As = [jnp.exp(z + L + c + cum) for z, L, c, cum in zip(zs, Ls, c_fails, cums)]
    if strict is not None:
        As = [jnp.where(strict, A, 0.0) for A in As]
    return zs, Ls, Lms, As


def _attn_specs(S, qb):
    nq = S // qb
    q_spec = pl.BlockSpec((qb, LANES), lambda b, p, i: (b * nq + i, p))
    k_spec = pl.BlockSpec((S, LANES), lambda b, p, i: (b, SB_WIDTH // LANES + p))
    v_spec = pl.BlockSpec((S, LANES), lambda b, p, i: (b, 2 * SB_WIDTH // LANES + p))
    seq = pl.BlockSpec((S, LANES), lambda b, p, i: (b, p))
    return q_spec, k_spec, v_spec, q_spec, seq


def _key_walk(i, qb, block, carry):
    per = qb // ATTN_KEYS
    for sub in reversed(range(per)):
        carry = block(i * per + sub, carry, sub * ATTN_KEYS)
    return lax.fori_loop(0, i * per, lambda j, c: block(i * per - 1 - j, c, None), carry)


def _attn_fwd(proj, B, S):
    qb = min(ATTN_QUERIES, S)
    scale = HEAD_DIM ** -0.5

    def body(q_ref, k_ref, v_ref, o_ref):
        i = pl.program_id(2)
        masks = _head_masks()
        u_gt = _key_tri(lambda r, c: r > c)
        q16 = q_ref[...].astype(BF16)

        def block(J, carry, row0):
            r0 = pl.multiple_of(J * ATTN_KEYS, ATTN_KEYS)
            kbh = _by_head(k_ref[pl.ds(r0, ATTN_KEYS), :], masks)
            vbh = _by_head(v_ref[pl.ds(r0, ATTN_KEYS), :], masks)
            lo = row0 or 0
            strict = None if row0 is None else _causal(qb - lo)
            acc, cs = _from_row(carry, lo)
            _, _, Lms, As = _sb_weights(q16[lo:], kbh, cs, u_gt, strict, scale)
            acc = acc + _nn(jnp.concatenate([A.astype(BF16) for A in As], axis=1), vbh)
            cs = tuple(c + jnp.sum(Lm, axis=1, keepdims=True) for c, Lm in zip(cs, Lms))
            return _onto_rows(carry, (acc, cs), lo)

        zero_c = tuple(jnp.zeros((qb, 1), F32) for _ in masks)
        carry = _key_walk(i, qb, block, (jnp.zeros((qb, LANES), F32), zero_c))
        o_ref[...] = carry[0]

    q_spec, k_spec, v_spec, blk, _ = _attn_specs(S, qb)
    return pl.pallas_call(
        body, name="sb_attn_fwd",
        grid=(B, SB_WIDTH // LANES, S // qb),
        in_specs=[q_spec, k_spec, v_spec],
        out_specs=blk,
        out_shape=jax.ShapeDtypeStruct((B * S, SB_WIDTH), F32),
        compiler_params=_params(("parallel", "parallel", "arbitrary")),
    )(proj, proj, proj)


def _attn_bwd(proj, o, do, B, S):
    qb = min(ATTN_QUERIES, S)
    nq = S // qb
    scale = HEAD_DIM ** -0.5

    def body(q_ref, k_ref, v_ref, o_ref, do_ref, dq_ref, dk_out, dv_out, dk_ref, dv_ref):
        i = pl.program_id(2)

        @pl.when(i == 0)
        def _():
            dk_ref[...] = jnp.zeros_like(dk_ref)
            dv_ref[...] = jnp.zeros_like(dv_ref)

        masks = _head_masks()
        u_gt = _key_tri(lambda r, c: r > c)
        u_ge = _key_tri(lambda r, c: r >= c)
        heads = range(len(masks))
        q16 = q_ref[...].astype(BF16)
        do16 = do_ref[...].astype(BF16)
        od = o_ref[...] * do16.astype(F32)
        totals = tuple(jnp.sum(od * m, axis=1, keepdims=True) for m in masks)

        def block(J, carry, row0):
            r0 = pl.multiple_of(J * ATTN_KEYS, ATTN_KEYS)
            kbh = _by_head(k_ref[pl.ds(r0, ATTN_KEYS), :], masks)
            vbh = _by_head(v_ref[pl.ds(r0, ATTN_KEYS), :], masks)
            lo = row0 or 0
            strict = None if row0 is None else _causal(qb - lo)
            dq, c_fail, c_p = _from_row(carry, lo)
            tot = _from_row(totals, lo)
            zs, Ls, Lms, As = _sb_weights(q16[lo:], kbh, c_fail, u_gt, strict, scale)
            Abs = [A.astype(BF16) for A in As]
            dA_all = _nt(do16[lo:], vbh)
            Ps = [Abs[h].astype(F32) * dA_all[:, h * ATTN_KEYS:(h + 1) * ATTN_KEYS] for h in heads]
            afters = [c_p[h] + _split_dot(Ps[h], u_ge) for h in heads]
            sigs = [jnp.exp(zs[h] + Ls[h]) for h in heads]
            dzs = [(Ps[h] * (1.0 - sigs[h]) - sigs[h] * (tot[h] - afters[h])) * scale for h in heads]
            if strict is not None:
                dzs = [jnp.where(strict, dz, 0.0) for dz in dzs]
            dz_all = jnp.concatenate([dz.astype(BF16) for dz in dzs], axis=1)
            dv_ref[pl.ds(r0, ATTN_KEYS), :] += _fold_heads(_tn(jnp.concatenate(Abs, axis=1), do16[lo:]), masks)
            dk_ref[pl.ds(r0, ATTN_KEYS), :] += _fold_heads(_tn(dz_all, q16[lo:]), masks)
            dq = dq + _nn(dz_all, kbh)
            c_fail = tuple(c_fail[h] + jnp.sum(Lms[h], axis=1, keepdims=True) for h in heads)
            c_p = tuple(c_p[h] + jnp.sum(Ps[h], axis=1, keepdims=True) for h in heads)
            return _onto_rows(carry, (dq, c_fail, c_p), lo)

        zc = tuple(jnp.zeros((qb, 1), F32) for _ in masks)
        carry = _key_walk(i, qb, block, (jnp.zeros((qb, LANES), F32), zc, zc))
        dq_ref[...] = carry[0].astype(dq_ref.dtype)

        @pl.when(i == nq - 1)
        def _():
            dk_out[...] = dk_ref[...].astype(dk_out.dtype)
            dv_out[...] = dv_ref[...].astype(dv_out.dtype)

    q_spec, k_spec, v_spec, blk, seq = _attn_specs(S, qb)
    return pl.pallas_call(
        body, name="sb_attn_bwd",
        grid=(B, SB_WIDTH // LANES, nq),
        in_specs=[q_spec, k_spec, v_spec, blk, blk],
        out_specs=[blk, seq, seq],
        out_shape=[jax.ShapeDtypeStruct((B * S, SB_WIDTH), BF16)] * 3,
        scratch_shapes=[pltpu.VMEM((S, LANES), F32), pltpu.VMEM((S, LANES), F32)],
        compiler_params=_params(("parallel", "parallel", "arbitrary")),
    )(proj, proj, proj, o, do)


_BATCHED = {"nn": "gmk,gkn->gmn", "nt": "gmk,gnk->gmn", "tn": "gkm,gkn->gmn"}


def _bdot_raw(a, b, kind, passes):
    e = functools.partial(jnp.einsum, _BATCHED[kind], preferred_element_type=F32)
    ah, bh = a.astype(BF16), b.astype(BF16)
    if passes == 1:
        return e(ah, bh)
    al, bl = (a - ah.astype(F32)).astype(BF16), (b - bh.astype(F32)).astype(BF16)
    return e(ah, bh) + e(ah, bl) + e(al, bh)


@functools.partial(jax.custom_vjp, nondiff_argnums=(2, 3))
def _bdot(a, b, kind, passes):
    return _bdot_raw(a, b, kind, passes)


def _bdot_fwd(a, b, kind, passes):
    return _bdot_raw(a, b, kind, passes), (a, b)


def _bdot_bwd(kind, passes, res, g):
    a, b = res
    if kind == "nn":
        return _bdot_raw(g, b, "nt", passes), _bdot_raw(a, g, "tn", passes)
    if kind == "nt":
        return _bdot_raw(g, b, "nn", passes), _bdot_raw(g, a, "tn", passes)
    return _bdot_raw(b, g, "nt", passes), _bdot_raw(a, g, "nn", passes)


_bdot.defvjp(_bdot_fwd, _bdot_bwd)


def _wkv_chunk(S0, r, lw, k, v, kap, a):
    G, C, N = r.shape
    row = lax.broadcasted_iota(jnp.int32, (C, C), 0)
    col = lax.broadcasted_iota(jnp.int32, (C, C), 1)
    incl = (col <= row).astype(F32)
    strict = (col < row).astype(F32)
    cum = _bdot(jnp.broadcast_to(incl, (G, C, C)), lw, "nn", 3)
    e_pos = jnp.exp(cum)
    e_neg = jnp.exp(-cum)
    al = -kap * jnp.exp(cum - lw)
    be = kap * a * e_neg
    kt = k * e_neg
    rt = r * e_pos
    bk = jnp.concatenate([be, kt], axis=1)
    mask = jnp.concatenate([jnp.concatenate([strict, strict], axis=1), jnp.concatenate([incl, incl], axis=1)], axis=0)
    m_all = _bdot(jnp.concatenate([al, rt], axis=1), bk, "nt", 3) * mask
    m_ab, m_ak = m_all[:, :C, :C], m_all[:, :C, C:]
    m_rb, m_rk = m_all[:, C:, :C], m_all[:, C:, C:]
    S0t = jnp.swapaxes(S0, 1, 2)
    sa = _bdot(jnp.concatenate([al, m_ak], axis=2), jnp.concatenate([S0t, v], axis=1), "nn", 3)
    p = m_ab
    steps = max(1, (C - 1).bit_length())
    for j in range(steps):
        sa = sa + _bdot(p, sa, "nn", 1)
        if j + 1 < steps:
            p = _bdot(p, p, "nn", 1)
    y = _bdot(jnp.concatenate([rt, m_rb, m_rk], axis=2), jnp.concatenate([S0t, sa, v], axis=1), "nn", 3)
    S1 = (S0 + _bdot(jnp.concatenate([sa, v], axis=1), bk, "tn", 1)) * e_pos[:, C - 1:C, :]
    return y, S1


def _split_heads(x):
    return jnp.stack([x[:, h * HEAD_DIM:(h + 1) * HEAD_DIM] for h in range(x.shape[1] // HEAD_DIM)], axis=0)


def _merge_heads(x):
    return jnp.concatenate([x[h] for h in range(x.shape[0])], axis=1)


def _seq_heads(ref):
    return jnp.concatenate([_split_heads(ref[s]) for s in range(ref.shape[0])], axis=0)


def _store_seq_heads(ref, x):
    heads = x.shape[0] // ref.shape[0]
    for s in range(ref.shape[0]):
        ref[s] = _merge_heads(x[s * heads:(s + 1) * heads])


def _wkv_fwd(r, lw, k, v, kap, a, B, S):
    C, H, N = WKV_CHUNK, RW_WIDTH // HEAD_DIM, HEAD_DIM
    nc = S // C
    Q = min(WKV_SEQS, B)

    def body(r_ref, lw_ref, k_ref, v_ref, kap_ref, a_ref, y_ref, st_ref, s_scr):
        @pl.when(pl.program_id(1) == 0)
        def _():
            s_scr[...] = jnp.zeros_like(s_scr)

        S0 = s_scr[...]
        for s in range(Q):
            st_ref[s, 0] = S0[s * H:(s + 1) * H]
        args = [_seq_heads(ref) for ref in (r_ref, lw_ref, k_ref, v_ref, kap_ref, a_ref)]
        y, S1 = _wkv_chunk(S0, *args)
        s_scr[...] = S1
        _store_seq_heads(y_ref, y)

    row_spec = pl.BlockSpec((Q, C, RW_WIDTH), lambda b, c: (b, c, 0))
    seqs = lambda t: t.reshape(B, S, RW_WIDTH)
    y, states = pl.pallas_call(
        body, name="wkv_fwd",
        grid=(B // Q, nc),
        in_specs=[row_spec] * 6,
        out_specs=[row_spec, pl.BlockSpec((Q, 1, H, N, N), lambda b, c: (b, c, 0, 0, 0))],
        out_shape=[jax.ShapeDtypeStruct((B, S, RW_WIDTH), F32), jax.ShapeDtypeStruct((B, nc, H, N, N), F32)],
        scratch_shapes=[pltpu.VMEM((Q * H, N, N), F32)],
        compiler_params=_params(("arbitrary", "arbitrary")),
    )(*map(seqs, (r, lw, k, v, kap, a)))
    return y.reshape(B * S, RW_WIDTH), states


def _wkv_bwd(r, lw, k, v, kap, a, states, dy, B, S):
    C, H, N = WKV_CHUNK, RW_WIDTH // HEAD_DIM, HEAD_DIM
    nc = S // C
    Q = min(WKV_SEQS, B)

    def body(r_ref, lw_ref, k_ref, v_ref, kap_ref, a_ref, st_ref, dy_ref,
             dr_ref, dlw_ref, dk_ref, dv_ref, dkap_ref, da_ref, ds_scr):
        @pl.when(pl.program_id(1) == 0)
        def _():
            ds_scr[...] = jnp.zeros_like(ds_scr)

        args = [_seq_heads(ref) for ref in (r_ref, lw_ref, k_ref, v_ref, kap_ref, a_ref)]
        S0 = jnp.concatenate([st_ref[s, 0] for s in range(Q)], axis=0)
        _, vjp = jax.vjp(_wkv_chunk, S0, *args)
        g = vjp((_seq_heads(dy_ref), ds_scr[...]))
        ds_scr[...] = g[0]
        for ref, gv in zip((dr_ref, dlw_ref, dk_ref, dv_ref, dkap_ref, da_ref), g[1:]):
            _store_seq_heads(ref, gv)

    row_spec = pl.BlockSpec((Q, C, RW_WIDTH), lambda b, c: (b, nc - 1 - c, 0))
    st_spec = pl.BlockSpec((Q, 1, H, N, N), lambda b, c: (b, nc - 1 - c, 0, 0, 0))
    seqs = lambda t: t.reshape(B, S, RW_WIDTH)
    res = pl.pallas_call(
        body, name="wkv_bwd",
        grid=(B // Q, nc),
        in_specs=[row_spec] * 6 + [st_spec, row_spec],
        out_specs=[row_spec] * 6,
        out_shape=[jax.ShapeDtypeStruct((B, S, RW_WIDTH), F32)] * 6,
        scratch_shapes=[pltpu.VMEM((Q * H, N, N), F32)],
        compiler_params=_params(("arbitrary", "arbitrary"), vmem=WKV_BWD_VMEM),
    )(*map(seqs, (r, lw, k, v, kap, a)), states, seqs(dy))
    return [t.reshape(B * S, RW_WIDTH) for t in res]


HBM = pl.BlockSpec(memory_space=pl.ANY)


def _place():
    return lax.axis_index("x"), lax.axis_index("y"), lax.axis_index("c")


def _other_chips(x, y):
    return [(1 - x, y), (x, 1 - y), (1 - x, 1 - y)]


def _all_gather_chips(shards):
    n = len(shards)

    def body(*refs):
        ins, outs = refs[:n], refs[n:2 * n]
        ici_send, ici_recv, d2d_send, d2d_recv, local = refs[2 * n:]
        x, y, c = _place()
        me = 2 * x + y
        sib = (x, y, 1 - c)
        chips = _other_chips(x, y)
        started, copies = [], []
        for w in range(n):
            cp = pltpu.make_async_copy(ins[w].at[c], outs[w].at[me, c], local.at[w])
            cp.start()
            copies.append(cp)
            for j, (px, py) in enumerate(chips):
                rd = pltpu.make_async_remote_copy(
                    src_ref=ins[w].at[c], dst_ref=outs[w].at[me, c], send_sem=ici_send.at[3 * w + j],
                    recv_sem=ici_recv.at[3 * w + j], device_id=(px, py, c), device_id_type=MESH)
                rd.start()
                started.append(rd)
            rd = pltpu.make_async_remote_copy(
                src_ref=ins[w].at[c], dst_ref=outs[w].at[me, c], send_sem=d2d_send.at[4 * w + 3],
                recv_sem=d2d_recv.at[4 * w + 3], device_id=sib, device_id_type=MESH)
            rd.start()
            started.append(rd)
        for w in range(n):
            for j, (px, py) in enumerate(chips):
                src = 2 * px + py
                pltpu.make_async_remote_copy(
                    src_ref=ins[w].at[c], dst_ref=outs[w].at[src, c], send_sem=ici_send.at[3 * w + j],
                    recv_sem=ici_recv.at[3 * w + j], device_id=(px, py, c), device_id_type=MESH).wait_recv()
                rd = pltpu.make_async_remote_copy(
                    src_ref=outs[w].at[src, c], dst_ref=outs[w].at[src, c], send_sem=d2d_send.at[4 * w + j],
                    recv_sem=d2d_recv.at[4 * w + j], device_id=sib, device_id_type=MESH)
                rd.start()
                started.append(rd)
        for w in range(n):
            for j, (px, py) in enumerate(chips):
                pltpu.make_async_remote_copy(
                    src_ref=ins[w].at[c], dst_ref=outs[w].at[2 * px + py, 1 - c], send_sem=d2d_send.at[4 * w + j],
                    recv_sem=d2d_recv.at[4 * w + j], device_id=sib, device_id_type=MESH).wait_recv()
            pltpu.make_async_remote_copy(
                src_ref=ins[w].at[c], dst_ref=outs[w].at[me, 1 - c], send_sem=d2d_send.at[4 * w + 3],
                recv_sem=d2d_recv.at[4 * w + 3], device_id=sib, device_id_type=MESH).wait_recv()
        for rd in started:
            rd.wait_send()
        for cp in copies:
            cp.wait()

    return pl.pallas_call(
        body, name="gather_weights",
        in_specs=[HBM] * n, out_specs=[HBM] * n,
        out_shape=[jax.ShapeDtypeStruct((N_CHIPS,) + s.shape, s.dtype) for s in shards],
        scratch_shapes=[pltpu.SemaphoreType.DMA((3 * n,)), pltpu.SemaphoreType.DMA((3 * n,)),
                        pltpu.SemaphoreType.DMA((4 * n,)), pltpu.SemaphoreType.DMA((4 * n,)),
                        pltpu.SemaphoreType.DMA((n,))],
        compiler_params=pltpu.CompilerParams(has_side_effects=True),
    )(*shards)


def _pair_split(grads):
    n = len(grads)

    def body(*refs):
        ins, theirs = refs[:n], refs[n:2 * n]
        send, recv = refs[2 * n:]
        x, y, c = _place()
        sib = (x, y, 1 - c)
        rds = []
        for w in range(n):
            rd = pltpu.make_async_remote_copy(
                src_ref=ins[w].at[:, 1 - c], dst_ref=theirs[w], send_sem=send.at[w], recv_sem=recv.at[w],
                device_id=sib, device_id_type=MESH)
            rd.start()
            rds.append(rd)
        for rd in rds:
            rd.wait_recv()
        for rd in rds:
            rd.wait_send()

    return pl.pallas_call(
        body, name="grad_pair_split",
        in_specs=[HBM] * n, out_specs=[HBM] * n,
        out_shape=[jax.ShapeDtypeStruct((g.shape[0],) + g.shape[2:], g.dtype) for g in grads],
        scratch_shapes=[pltpu.SemaphoreType.DMA((n,)), pltpu.SemaphoreType.DMA((n,))],
        compiler_params=pltpu.CompilerParams(has_side_effects=True),
    )(*grads)


def _chip_scatter(parts):
    n = len(parts)

    def body(*refs):
        ins, outs = refs[:n], refs[n:2 * n]
        send, recv = refs[2 * n:]
        x, y, c = _place()
        me = 2 * x + y
        rds = []
        for w in range(n):
            for j, (px, py) in enumerate(_other_chips(x, y)):
                s = 3 * w + j
                rd = pltpu.make_async_remote_copy(
                    src_ref=ins[w].at[2 * px + py], dst_ref=outs[w].at[j], send_sem=send.at[s], recv_sem=recv.at[s],
                    device_id=(px, py, c), device_id_type=MESH)
                rd.start()
                rds.append(rd)
        for w in range(n):
            for j, (px, py) in enumerate(_other_chips(x, y)):
                s = 3 * w + j
                pltpu.make_async_remote_copy(
                    src_ref=ins[w].at[me], dst_ref=outs[w].at[j], send_sem=send.at[s], recv_sem=recv.at[s],
                    device_id=(px, py, c), device_id_type=MESH).wait_recv()
        for rd in rds:
            rd.wait_send()

    return pl.pallas_call(
        body, name="grad_chip_scatter",
        in_specs=[HBM] * n, out_specs=[HBM] * n,
        out_shape=[jax.ShapeDtypeStruct((N_CHIPS - 1,) + p.shape[1:], p.dtype) for p in parts],
        scratch_shapes=[pltpu.SemaphoreType.DMA((3 * n,)), pltpu.SemaphoreType.DMA((3 * n,))],
        compiler_params=pltpu.CompilerParams(has_side_effects=True),
    )(*parts)


def _pair_join(bufs):
    n = len(bufs)

    def body(*refs):
        ins, outs = refs[:n], refs[n:2 * n]
        send, recv = refs[2 * n:]
        x, y, c = _place()
        sib = (x, y, 1 - c)
        rds = []
        for w in range(n):
            rd = pltpu.make_async_remote_copy(
                src_ref=ins[w].at[c], dst_ref=outs[w].at[c], send_sem=send.at[w], recv_sem=recv.at[w],
                device_id=sib, device_id_type=MESH)
            rd.start()
            rds.append(rd)
        for w in range(n):
            pltpu.make_async_remote_copy(
                src_ref=ins[w].at[c], dst_ref=outs[w].at[1 - c], send_sem=send.at[w], recv_sem=recv.at[w],
                device_id=sib, device_id_type=MESH).wait_recv()
        for rd in rds:
            rd.wait_send()

    return pl.pallas_call(
        body, name="grad_pair_join",
        in_specs=[HBM] * n, out_specs=[HBM] * n,
        out_shape=[jax.ShapeDtypeStruct(b.shape, b.dtype) for b in bufs],
        input_output_aliases={w: w for w in range(n)},
        scratch_shapes=[pltpu.SemaphoreType.DMA((n,)), pltpu.SemaphoreType.DMA((n,))],
        compiler_params=pltpu.CompilerParams(has_side_effects=True),
    )(*bufs)


def _all_reduce_small(packed):
    R = packed.shape[0]

    def body(x_ref, o_ref, buf, send, recv):
        x, y, c = _place()
        me = 4 * x + 2 * y + c
        buf[me] = x_ref[...]
        rds = []
        for rel in range(1, N_DEV):
            fx, fy, fc = (rel >> 2) & 1, (rel >> 1) & 1, rel & 1
            peer = (1 - x if fx else x, 1 - y if fy else y, 1 - c if fc else c)
            rd = pltpu.make_async_remote_copy(
                src_ref=x_ref, dst_ref=buf.at[me], send_sem=send.at[rel - 1], recv_sem=recv.at[rel - 1],
                device_id=peer, device_id_type=MESH)
            rd.start()
            rds.append((rd, peer))
        for rel in range(1, N_DEV):
            rd, (px, py, pc) = rds[rel - 1]
            pltpu.make_async_remote_copy(
                src_ref=x_ref, dst_ref=buf.at[4 * px + 2 * py + pc], send_sem=send.at[rel - 1], recv_sem=recv.at[rel - 1],
                device_id=(px, py, pc), device_id_type=MESH).wait_recv()
        for rd, _ in rds:
            rd.wait_send()
        total = buf[0]
        for d in range(1, N_DEV):
            total = total + buf[d]
        o_ref[...] = total

    return pl.pallas_call(
        body, name="all_reduce_small",
        in_specs=[pl.BlockSpec(memory_space=pltpu.VMEM)],
        out_specs=pl.BlockSpec(memory_space=pltpu.VMEM),
        out_shape=jax.ShapeDtypeStruct(packed.shape, F32),
        scratch_shapes=[pltpu.VMEM((N_DEV, R, LANES), F32), pltpu.SemaphoreType.DMA((N_DEV - 1,)),
                        pltpu.SemaphoreType.DMA((N_DEV - 1,))],
        compiler_params=pltpu.CompilerParams(has_side_effects=True),
    )(packed)


def _pair_sum(name, split, theirs, core):
    n_chip, _, Rh, C = split.shape
    tile = _div_tile(Rh, 256, 2 * SUBLANES)
    nt = Rh // tile

    def body(core_ref, a_ref, b_ref, o_ref):
        o_ref[...] = (a_ref[...] + b_ref[...]).astype(o_ref.dtype)

    return pl.pallas_call(
        body, name=name,
        grid_spec=pltpu.PrefetchScalarGridSpec(
            num_scalar_prefetch=1,
            grid=(n_chip, nt),
            in_specs=[pl.BlockSpec((None, None, tile, C), lambda j, i, core_ref: (j, core_ref[0], i, 0)),
                      pl.BlockSpec((None, tile, C), lambda j, i, core_ref: (j, i, 0))],
            out_specs=pl.BlockSpec((None, tile, C), lambda j, i, core_ref: (j, i, 0)),
        ),
        out_shape=jax.ShapeDtypeStruct((n_chip, Rh, C), BF16),
        compiler_params=_params(("parallel", "parallel")),
    )(core, split, theirs)


def _chip_sum(name, own, landed, core):
    n_in, Rh, C = landed.shape
    tile = _div_tile(Rh, 256, 2 * SUBLANES)

    def body(core_ref, *refs):
        total = refs[0][...].astype(F32)
        for ref in refs[1:n_in + 1]:
            total = total + ref[...].astype(F32)
        refs[n_in + 1][...] = total

    slot = lambda j: pl.BlockSpec((None, tile, C), lambda i, core_ref: (j, i, 0))
    return pl.pallas_call(
        body, name=name,
        grid_spec=pltpu.PrefetchScalarGridSpec(
            num_scalar_prefetch=1,
            grid=(Rh // tile,),
            in_specs=[pl.BlockSpec((None, tile, C), lambda i, core_ref: (core_ref[1], i, 0))]
                     + [slot(j) for j in range(n_in)],
            out_specs=pl.BlockSpec((None, tile, C), lambda i, core_ref: (core_ref[0], i, 0)),
        ),
        out_shape=jax.ShapeDtypeStruct((2, Rh, C), F32),
        compiler_params=_params(("parallel",)),
    )(core, own, *([landed] * n_in))


def _adamw(name, w, g, m, v):
    R, C = w.shape
    tile = _div_tile(R, 256, SUBLANES)
    c1 = 1.0 / (1.0 - ADAM_B1 ** ADAM_STEP)
    c2 = 1.0 / (1.0 - ADAM_B2 ** ADAM_STEP)

    def body(w_ref, g_ref, m_ref, v_ref, d_ref, nm_ref, nv_ref):
        g_ = g_ref[...]
        nm = ADAM_B1 * m_ref[...] + (1.0 - ADAM_B1) * g_
        nv = ADAM_B2 * v_ref[...] + (1.0 - ADAM_B2) * (g_ * g_)
        d_ref[...] = -ADAM_LR * ((nm * c1) / (jnp.sqrt(nv * c2) + ADAM_EPS) + ADAM_WD * w_ref[...])
        nm_ref[...] = nm
        nv_ref[...] = nv

    spec = pl.BlockSpec((tile, C), lambda i: (i, 0))
    return pl.pallas_call(
        body, name=name,
        grid=(R // tile,),
        in_specs=[spec] * 4, out_specs=[spec] * 3,
        out_shape=[jax.ShapeDtypeStruct((R, C), F32)] * 3,
        compiler_params=_params(("parallel",)),
    )(w, g, m, v)


def _cols_to_shards(full):
    K, N = full.shape
    return full.reshape(K, N_CHIPS, N // N_CHIPS).transpose(1, 0, 2)


def _shards_to_cols(sh):
    return sh.transpose(1, 0, 2).reshape(sh.shape[1], -1)


def _rows_to_shards(full):
    return full.reshape(N_CHIPS, full.shape[0] // N_CHIPS, full.shape[1])


SMALL = ["norm_mix_pre", "b_gate", "mu_rw", "w0", "a0", "k_k", "k_a", "r_k", "lnx_w", "lnx_b",
         "norm_mix_post", "norm_ffn_pre", "norm_ffn_post"]
BIG = ["w_in", "w_up", "a_up", "g_up", "w_sb_out", "w_rw_out", "w_o", "w_ffn_gate", "w_ffn_up", "w_ffn_down"]
ROW_SHARDED = ("w_o", "w_ffn_down")
ORDER = ["norm_mix_pre", "w_in", "b_gate", "mu_rw", "w0", "w_up", "a0", "a_up", "g_up", "k_k", "k_a", "r_k",
         "lnx_w", "lnx_b", "w_sb_out", "w_rw_out", "w_o", "norm_mix_post", "norm_ffn_pre", "w_ffn_gate",
         "w_ffn_up", "w_ffn_down", "norm_ffn_post"]


def _pack_small(vals, extra_rows=0):
    rows = jnp.concatenate([vals[n].reshape(-1, LANES) for n in SMALL], axis=0)
    pad = (-(rows.shape[0] + extra_rows)) % SUBLANES + extra_rows
    return jnp.pad(rows, ((0, pad), (0, 0)))


def _unpack_small(packed, shapes):
    out, r = {}, 0
    for n in SMALL:
        size = 1
        for s in shapes[n]:
            size *= s
        out[n] = packed[r:r + size // LANES].reshape(shapes[n])
        r += size // LANES
    return out


def kernel(x, norm_mix_pre, w_in, b_gate, mu_rw, w0, w_up, a0, a_up, g_up, k_k, k_a, r_k, lnx_w, lnx_b, w_sb_out, w_rw_out, w_o, norm_mix_post, norm_ffn_pre, w_ffn_gate, w_ffn_up, w_ffn_down, norm_ffn_post, loss_target, m_norm_mix_pre, m_w_in, m_b_gate, m_mu_rw, m_w0, m_w_up, m_a0, m_a_up, m_g_up, m_k_k, m_k_a, m_r_k, m_lnx_w, m_lnx_b, m_w_sb_out, m_w_rw_out, m_w_o, m_norm_mix_post, m_norm_ffn_pre, m_w_ffn_gate, m_w_ffn_up, m_w_ffn_down, m_norm_ffn_post, v_norm_mix_pre, v_w_in, v_b_gate, v_mu_rw, v_w0, v_w_up, v_a0, v_a_up, v_g_up, v_k_k, v_k_a, v_r_k, v_lnx_w, v_lnx_b, v_w_sb_out, v_w_rw_out, v_w_o, v_norm_mix_post, v_norm_ffn_pre, v_w_ffn_gate, v_w_ffn_up, v_w_ffn_down, v_norm_ffn_post):
    W = dict(norm_mix_pre=norm_mix_pre, w_in=w_in, b_gate=b_gate, mu_rw=mu_rw, w0=w0, w_up=w_up, a0=a0, a_up=a_up,
             g_up=g_up, k_k=k_k, k_a=k_a, r_k=r_k, lnx_w=lnx_w, lnx_b=lnx_b, w_sb_out=w_sb_out, w_rw_out=w_rw_out,
             w_o=w_o, norm_mix_post=norm_mix_post, norm_ffn_pre=norm_ffn_pre, w_ffn_gate=w_ffn_gate,
             w_ffn_up=w_ffn_up, w_ffn_down=w_ffn_down, norm_ffn_post=norm_ffn_post)
    Mo = dict(norm_mix_pre=m_norm_mix_pre, w_in=m_w_in, b_gate=m_b_gate, mu_rw=m_mu_rw, w0=m_w0, w_up=m_w_up, a0=m_a0,
              a_up=m_a_up, g_up=m_g_up, k_k=m_k_k, k_a=m_k_a, r_k=m_r_k, lnx_w=m_lnx_w, lnx_b=m_lnx_b,
              w_sb_out=m_w_sb_out, w_rw_out=m_w_rw_out, w_o=m_w_o, norm_mix_post=m_norm_mix_post,
              norm_ffn_pre=m_norm_ffn_pre, w_ffn_gate=m_w_ffn_gate, w_ffn_up=m_w_ffn_up, w_ffn_down=m_w_ffn_down,
              norm_ffn_post=m_norm_ffn_post)
    Vo = dict(norm_mix_pre=v_norm_mix_pre, w_in=v_w_in, b_gate=v_b_gate, mu_rw=v_mu_rw, w0=v_w0, w_up=v_w_up, a0=v_a0,
              a_up=v_a_up, g_up=v_g_up, k_k=v_k_k, k_a=v_k_a, r_k=v_r_k, lnx_w=v_lnx_w, lnx_b=v_lnx_b,
              w_sb_out=v_w_sb_out, w_rw_out=v_w_rw_out, w_o=v_w_o, norm_mix_post=v_norm_mix_post,
              norm_ffn_pre=v_norm_ffn_pre, w_ffn_gate=v_w_ffn_gate, w_ffn_up=v_w_ffn_up, w_ffn_down=v_w_ffn_down,
              norm_ffn_post=v_norm_ffn_post)
    shapes = {n: W[n].shape for n in ORDER}
    B, S, D = x.shape
    T = B * S
    x2 = x.reshape(T, D)
    tgt = loss_target.reshape(T, D)
    vec = {n: W[n].reshape(1, -1) for n in SMALL}

    halved = [W[n][0].astype(BF16).reshape(2, W[n].shape[1] // 2, W[n].shape[2]) for n in BIG]
    full = {}
    for n, gth in zip(BIG, _all_gather_chips(halved)):
        gth = gth.reshape((N_CHIPS,) + W[n].shape[1:])
        full[n] = gth.reshape(-1, gth.shape[2]) if n in ROW_SHARDED else _shards_to_cols(gth)
    w_sb, w_rw, w_gt = full["w_in"][:, :SB_COLS], full["w_in"][:, SB_COLS:SB_COLS + RW_COLS], full["w_in"][:, SB_COLS + RW_COLS:]
    lora_rows = {"w_up": 0, "a_up": 64, "g_up": 128}
    lora = {n: jnp.pad(full[n], ((r0, LORA_COLS - r0 - full[n].shape[0]), (0, 0))) for n, r0 in lora_rows.items()}
    mu = vec["mu_rw"]
    mu_parts = [mu[:, :512], mu[:, 512:1024], mu[:, 1024:1536], mu[:, 1536:]]
    b1, b2 = vec["b_gate"][:, :D], vec["b_gate"][:, D:]

    (h1,) = _rowwise("norm_mix_pre", _f_norm, [(x2, D, 0)], [vec["norm_mix_pre"]], [(D, BF16)])
    p_sb = _mm("proj_sb", h1, w_sb)
    p_rw = _mm("proj_rw", h1, w_rw)
    p_gt = _mm("proj_gate", h1, w_gt)
    o_sb = _attn_fwd(p_sb, B, S)
    pre_params = mu_parts + [vec["w0"], lora["w_up"], vec["a0"], lora["a_up"], lora["g_up"], vec["k_k"], vec["k_a"]]
    r_, lw_, k2_, v_, kap_, a_, g_ = _rw_pre(p_rw, pre_params, S)
    y_wkv, states = _wkv_fwd(r_, lw_, k2_, v_, kap_, a_, B, S)
    post_rows = [(y_wkv, 512, 0), (r_, 512, 0), (k2_, 512, 0), (v_, 512, 0), (g_, 512, 0)]
    post_params = [vec["lnx_w"], vec["lnx_b"], vec["r_k"]]
    (o_rw,) = _rowwise("rw_post", _f_rwpost, post_rows, post_params, [(512, BF16)])
    m1 = _mm("mix_sb_out", o_sb, full["w_sb_out"])
    m2 = _mm("mix_rw_out", o_rw, full["w_rw_out"])
    merge_rows = [(p_gt, D, 0), (p_gt, D, 1), (m1, D, 0), (m2, D, 0)]
    (merged,) = _rowwise("merge", _f_merge, merge_rows, [b1, b2], [(D, BF16)])
    u = _mm("mix_out", merged, full["w_o"])
    post1_params = [vec["norm_mix_post"], vec["norm_ffn_pre"]]
    x1, h2 = _rowwise("post_mix", _f_post1, [(x2, D, 0), (u, D, 0)], post1_params, [(D, F32), (D, BF16)])
    ag, au, sw = _mm_fused("ffn_in", [h2, h2], [full["w_ffn_gate"], full["w_ffn_up"]], [BF16] * 3,
                           epilogue=lambda gu, _: (gu[0], gu[1], _f_swiglu(*gu)[0]))
    f = _mm("ffn_down", sw, full["w_ffn_down"])
    loss_part, dx1, df, dg4 = _loss_head(x1, f, tgt, vec["norm_ffn_post"])

    gbig, gsmall = {}, {"norm_ffn_post": dg4}
    gbig["w_ffn_down"] = _mm("g_ffn_down", sw, df, ta=True)

    def swiglu_back(dsw, gu):
        return jax.vjp(_f_swiglu, *gu)[1]((dsw[0],))

    dag, dau = _mm_fused("ffn_back", [df], [full["w_ffn_down"]], [BF16] * 2, tb=True, extras=[ag, au],
                         epilogue=swiglu_back)
    (dh2,) = _mm_fused("d_h2", [dag, dau], [full["w_ffn_gate"], full["w_ffn_up"]], [F32], tb=True, add=True)
    gbig["w_ffn_gate"] = _mm("g_ffn_gate", h2, dag, ta=True)
    gbig["w_ffn_up"] = _mm("g_ffn_up", h2, dau, ta=True)
    (dx_res, du), (dg2, dg3) = _rowwise_vjp("post_mix_bwd", _f_post1, [(x2, D, 0), (u, D, 0)], post1_params,
                                            [[dx1], [dh2]], [True, True], [True, True], bf16_rows=(1,))
    gsmall["norm_mix_post"], gsmall["norm_ffn_pre"] = dg2, dg3
    dmerged = _mm("d_merged", du, full["w_o"], tb=True)
    gbig["w_o"] = _mm("g_w_o", merged, du, ta=True)
    (dpg1, dpg2, dm1, dm2), (db1, db2) = _rowwise_vjp("merge_bwd", _f_merge, merge_rows, [b1, b2], [[dmerged]],
                                                      [True] * 4, [True, True], bf16_rows=(0, 1, 2, 3))
    gsmall["b_gate"] = jnp.concatenate([db1, db2], axis=1)
    do_sb = _mm("d_o_sb", dm1, full["w_sb_out"], tb=True)
    do_rw = _mm("d_o_rw", dm2, full["w_rw_out"], tb=True)
    gbig["w_sb_out"] = _mm("g_sb_out", o_sb, dm1, ta=True)
    gbig["w_rw_out"] = _mm("g_rw_out", o_rw, dm2, ta=True)
    (dy_wkv, dr_a, dk2_a, dv_a, dg_), (dlnx_w, dlnx_b, dr_k) = _rowwise_vjp(
        "rw_post_bwd", _f_rwpost, post_rows, post_params, [[do_rw]], [True] * 5, [True] * 3)
    gsmall["lnx_w"], gsmall["lnx_b"], gsmall["r_k"] = dlnx_w, dlnx_b, dr_k
    dr_b, dlw, dk2_b, dv_b, dkap, da = _wkv_bwd(r_, lw_, k2_, v_, kap_, a_, states, dy_wkv, B, S)
    pre_cts = [[dr_a, dr_b], [dlw], [dk2_a, dk2_b], [dv_a, dv_b], [dkap], [da], [dg_]]
    dp_rw, dpre_params = _rw_pre_bwd(p_rw, pre_params, pre_cts, S)
    gsmall["mu_rw"] = jnp.concatenate(dpre_params[:4], axis=1)
    gsmall["w0"], gsmall["a0"], gsmall["k_k"], gsmall["k_a"] = dpre_params[4], dpre_params[6], dpre_params[9], dpre_params[10]
    glora = {"w_up": dpre_params[5][0:64], "a_up": dpre_params[7][64:128], "g_up": dpre_params[8][128:256]}
    dq, dk, dv = _attn_bwd(p_sb, o_sb, do_sb, B, S)
    (dh1,) = _mm_fused("d_h1_sb", [dq, dk, dv], [w_sb[:, :512], w_sb[:, 512:1024], w_sb[:, 1024:]], [F32],
                       tb=True, add=True)
    dh1 = _mm("d_h1_rw", dp_rw, w_rw, tb=True, acc=dh1)
    (dh1,) = _mm_fused("d_h1_gate", [dpg1, dpg2], [w_gt[:, :D], w_gt[:, D:]], [F32], tb=True, add=True,
                       extras=[dh1], epilogue=lambda p, e: (p[0] + e[0],))
    gbig["w_in"] = jnp.concatenate(
        [_mm("g_in_" + tag, h1, d, ta=True)
         for tag, d in (("q", dq), ("k", dk), ("v", dv), ("rw", dp_rw), ("g1", dpg1), ("g2", dpg2))], axis=1)
    (grad_x2,), (dg1,) = _rowwise_vjp("norm_mix_pre_bwd", _f_norm, [(x2, D, 0)], [vec["norm_mix_pre"]], [[dh1]],
                                      [True], [True], add_to={0: dx_res})
    gsmall["norm_mix_pre"] = dg1
    gbig.update(glora)

    split = []
    for n in BIG:
        g = _rows_to_shards(gbig[n]) if n in ROW_SHARDED else _cols_to_shards(gbig[n])
        split.append(g.reshape(N_CHIPS, 2, g.shape[1] // 2, g.shape[2]))
    core = jnp.stack([lax.axis_index("c"), 2 * lax.axis_index("x") + lax.axis_index("y")]).astype(jnp.int32)
    theirs = _pair_split(split)
    chip_sums = [_pair_sum("pair_sum_" + n, a, b, core) for n, a, b in zip(BIG, split, theirs)]
    landed = _chip_scatter(chip_sums)
    joined = _pair_join([_chip_sum("chip_sum_" + n, own, got, core) for n, own, got in zip(BIG, chip_sums, landed)])
    grads = {n: j.reshape(W[n].shape[1:]) for n, j in zip(BIG, joined)}

    small_local = _pack_small({n: gsmall[n] for n in SMALL}, extra_rows=1)
    loss_row = small_local.shape[0] - 1
    small_local = small_local.at[loss_row].set(loss_part[0])
    small_sum = _all_reduce_small(small_local)
    loss = small_sum[loss_row, 0]

    delta, new_m, new_v = {}, {}, {}
    for n in BIG:
        d_, m_, v2_ = _adamw("adamw_" + n, W[n][0], grads[n], Mo[n][0], Vo[n][0])
        delta[n], new_m[n], new_v[n] = d_[None], m_[None], v2_[None]
        grads[n] = grads[n][None]
    pk = lambda src: _pack_small({n: src[n] for n in SMALL}, extra_rows=1)
    d_s, m_s, v_s = _adamw("adamw_small", pk(W), small_sum.at[loss_row].set(0.0), pk(Mo), pk(Vo))
    for dst, packed in ((grads, small_sum), (delta, d_s), (new_m, m_s), (new_v, v_s)):
        dst.update(_unpack_small(packed, shapes))

    return (loss, grad_x2.reshape(B, S, D), *[grads[n] for n in ORDER], *[delta[n] for n in ORDER],
            *[new_m[n] for n in ORDER], *[new_v[n] for n in ORDER])
```

```python
import functools

import jax
import jax.numpy as jnp
from jax import lax
from jax.experimental import pallas as pl
from jax.experimental.pallas import tpu as pltpu

F32 = jnp.float32
BF16 = jnp.bfloat16
MESH = pl.DeviceIdType.MESH

D_MODEL = 1024
SB_HEADS = 8
HEAD_DIM = 64
SB_WIDTH = SB_HEADS * HEAD_DIM
RW_WIDTH = 512
LORA_COLS = 256
SB_COLS = 3 * SB_WIDTH
RW_COLS = 3 * RW_WIDTH + LORA_COLS
GATE_COLS = 2 * D_MODEL
D_FF = 2816
RMS_EPS = 1e-6
GN_EPS = HEAD_DIM * 1e-5
WKV_CHUNK = 64
WKV_SEQS = 4
ATTN_QUERIES = 512
ATTN_KEYS = 128
LANES = 128
SUBLANES = 8
N_CHIPS = 4
N_DEV = 8

ADAM_LR = 0.001
ADAM_B1 = 0.9
ADAM_B2 = 0.999
ADAM_EPS = 1e-08
ADAM_WD = 0.01
ADAM_STEP = 10

VMEM_LIMIT = 48 * 1024 * 1024
WKV_BWD_VMEM = 58 * 1024 * 1024


def _params(sem=None, vmem=VMEM_LIMIT, **kw):
    if sem is not None:
        kw["dimension_semantics"] = sem
    return pltpu.CompilerParams(vmem_limit_bytes=vmem, **kw)


def _div_tile(dim, pref, mult=LANES):
    if dim <= pref:
        return dim
    t = pref - pref % mult
    while t >= mult:
        if dim % t == 0:
            return t
        t -= mult
    return dim


def _dot(a, b, dims):
    return lax.dot_general(a, b, (dims, ((), ())), preferred_element_type=F32)


def _mm(name, a, b, *, ta=False, tb=False, acc=None, out_dtype=F32):
    if ta:
        K, M = a.shape
    else:
        M, K = a.shape
    N = b.shape[0] if tb else b.shape[1]
    if ta:
        tm, tn, tk = _div_tile(M, 1408), _div_tile(N, 1408), _div_tile(K, 512)
    else:
        tm, tn, tk = _div_tile(M, 512), _div_tile(N, 1408), _div_tile(K, 1408)
    nk = K // tk
    dims = ((0,) if ta else (1,), (1,) if tb else (0,))
    has_acc = acc is not None

    def body(*refs):
        a_ref, b_ref = refs[0], refs[1]
        part = _dot(a_ref[...].astype(BF16), b_ref[...].astype(BF16), dims)
        if nk == 1:
            o_ref = refs[-1]
            o_ref[...] = (part + refs[2][...] if has_acc else part).astype(o_ref.dtype)
            return
        o_ref, scr = refs[-2], refs[-1]
        k = pl.program_id(2)

        @pl.when(k == 0)
        def _():
            scr[...] = part + refs[2][...] if has_acc else part

        @pl.when(k > 0)
        def _():
            scr[...] += part

        @pl.when(k == nk - 1)
        def _():
            o_ref[...] = scr[...].astype(o_ref.dtype)

    a_spec = pl.BlockSpec((tk, tm), lambda i, j, k: (k, i)) if ta else pl.BlockSpec((tm, tk), lambda i, j, k: (i, k))
    b_spec = pl.BlockSpec((tn, tk), lambda i, j, k: (j, k)) if tb else pl.BlockSpec((tk, tn), lambda i, j, k: (k, j))
    o_spec = pl.BlockSpec((tm, tn), lambda i, j, k: (i, j))
    return pl.pallas_call(
        body, name=name,
        grid=(M // tm, N // tn, nk),
        in_specs=[a_spec, b_spec] + ([o_spec] if has_acc else []),
        out_specs=o_spec,
        out_shape=jax.ShapeDtypeStruct((M, N), out_dtype),
        scratch_shapes=[pltpu.VMEM((tm, tn), F32)] if nk > 1 else [],
        compiler_params=_params(("parallel", "parallel", "arbitrary")),
    )(*([a, b] + ([acc] if has_acc else [])))


def _mm_fused(name, lhs, rhs, outs, *, tb=False, add=False, extras=(), epilogue=None):
    M, K = lhs[0].shape
    N = rhs[0].shape[0] if tb else rhs[0].shape[1]
    tm, tn, tk = _div_tile(M, 512), _div_tile(N, 1408), _div_tile(K, 1408)
    nk = K // tk
    n_l, n_e, n_o = len(lhs), len(extras), len(outs)
    n_acc = 1 if add else n_l
    dims = ((1,), (1,) if tb else (0,))

    def body(*refs):
        l_refs, r_refs = refs[:n_l], refs[n_l:2 * n_l]
        e_refs = refs[2 * n_l:2 * n_l + n_e]
        o_refs = refs[2 * n_l + n_e:2 * n_l + n_e + n_o]
        scr = refs[2 * n_l + n_e + n_o:]
        parts = [_dot(l[...].astype(BF16), r[...].astype(BF16), dims) for l, r in zip(l_refs, r_refs)]
        if add:
            parts = [functools.reduce(lambda u, v: u + v, parts)]

        def finish(vals):
            res = epilogue(vals, [e[...].astype(F32) for e in e_refs]) if epilogue else vals
            for ref, val in zip(o_refs, res):
                ref[...] = val.astype(ref.dtype)

        if nk == 1:
            finish(parts)
            return
        k = pl.program_id(2)

        @pl.when(k == 0)
        def _():
            for s, part in zip(scr, parts):
                s[...] = part

        @pl.when(k > 0)
        def _():
            for s, part in zip(scr, parts):
                s[...] += part

        @pl.when(k == nk - 1)
        def _():
            finish([s[...] for s in scr])

    a_spec = pl.BlockSpec((tm, tk), lambda i, j, k: (i, k))
    b_spec = pl.BlockSpec((tn, tk), lambda i, j, k: (j, k)) if tb else pl.BlockSpec((tk, tn), lambda i, j, k: (k, j))
    o_spec = pl.BlockSpec((tm, tn), lambda i, j, k: (i, j))
    return pl.pallas_call(
        body, name=name,
        grid=(M // tm, N // tn, nk),
        in_specs=[a_spec] * n_l + [b_spec] * n_l + [o_spec] * n_e,
        out_specs=[o_spec] * n_o,
        out_shape=[jax.ShapeDtypeStruct((M, N), dt) for dt in outs],
        scratch_shapes=[pltpu.VMEM((tm, tn), F32)] * (n_acc if nk > 1 else 0),
        compiler_params=_params(("parallel", "parallel", "arbitrary")),
    )(*lhs, *rhs, *extras)


def _row_spec(tile, width, colblk):
    return pl.BlockSpec((tile, width), lambda i: (i, colblk))


def _full_spec(shape):
    return pl.BlockSpec(shape, lambda i: (0,) * len(shape))


def _rowwise(name, fn, rows, params, outs, tile=256):
    T = rows[0][0].shape[0]
    tile = min(tile, T)
    n_r, n_p = len(rows), len(params)

    def body(*refs):
        r = [x[...].astype(F32) for x in refs[:n_r]]
        p = [x[...].astype(F32) for x in refs[n_r:n_r + n_p]]
        for o_ref, val in zip(refs[n_r + n_p:], fn(*r, *p)):
            o_ref[...] = val.astype(o_ref.dtype)

    return pl.pallas_call(
        body, name=name,
        grid=(T // tile,),
        in_specs=[_row_spec(tile, w, cb) for _, w, cb in rows] + [_full_spec(p.shape) for p in params],
        out_specs=[_row_spec(tile, w, 0) for w, _ in outs],
        out_shape=[jax.ShapeDtypeStruct((T, w), dt) for w, dt in outs],
        compiler_params=_params(("parallel",)),
    )(*([a for a, _, _ in rows] + list(params)))


def _rowwise_vjp(name, fn, rows, params, cts, need_rows, need_params, add_to=None, tile=256, bf16_rows=()):
    add_to = add_to or {}
    T = rows[0][0].shape[0]
    tile = min(tile, T)
    n_r, n_p = len(rows), len(params)
    ct_flat = [c for group in cts for c in group]
    ct_sizes = [len(group) for group in cts]
    add_idx = sorted(add_to)
    row_out = [i for i in range(n_r) if need_rows[i]]
    par_out = [i for i in range(n_p) if need_params[i]]
    n_ct, n_add = len(ct_flat), len(add_idx)

    def body(*refs):
        pos = 0
        r = [x[...].astype(F32) for x in refs[pos:pos + n_r]]
        pos += n_r
        p = [x[...].astype(F32) for x in refs[pos:pos + n_p]]
        pos += n_p
        ct_vals = [x[...].astype(F32) for x in refs[pos:pos + n_ct]]
        pos += n_ct
        adds = {i: x[...] for i, x in zip(add_idx, refs[pos:pos + n_add])}
        pos += n_add
        drow_refs = refs[pos:pos + len(row_out)]
        pos += len(row_out)
        dpar_refs = refs[pos:pos + len(par_out)]
        ct_in, q = [], 0
        for n in ct_sizes:
            ct_in.append(functools.reduce(lambda u, v: u + v, ct_vals[q:q + n]))
            q += n
        _, vjp = jax.vjp(fn, *r, *p)
        grads = vjp(tuple(ct_in))
        for ref, i in zip(drow_refs, row_out):
            g = grads[i]
            ref[...] = (g + adds[i] if i in adds else g).astype(ref.dtype)

        @pl.when(pl.program_id(0) == 0)
        def _():
            for ref in dpar_refs:
                ref[...] = jnp.zeros_like(ref)

        for ref, i in zip(dpar_refs, par_out):
            ref[...] += grads[n_r + i]

    ct_widths = [c.shape[1] for c in ct_flat]
    in_specs = ([_row_spec(tile, w, cb) for _, w, cb in rows] + [_full_spec(p.shape) for p in params]
                + [_row_spec(tile, w, 0) for w in ct_widths] + [_row_spec(tile, rows[i][1], 0) for i in add_idx])
    out_specs = [_row_spec(tile, rows[i][1], 0) for i in row_out] + [_full_spec(params[i].shape) for i in par_out]
    out_shape = ([jax.ShapeDtypeStruct((T, rows[i][1]), BF16 if i in bf16_rows else F32) for i in row_out]
                 + [jax.ShapeDtypeStruct(params[i].shape, F32) for i in par_out])
    res = pl.pallas_call(
        body, name=name,
        grid=(T // tile,),
        in_specs=in_specs, out_specs=out_specs, out_shape=out_shape,
        compiler_params=_params(("arbitrary",)),
    )(*([a for a, _, _ in rows] + list(params) + ct_flat + [add_to[i] for i in add_idx]))
    return res[:len(row_out)], res[len(row_out):]


def _sigmoid(x):
    return 0.5 * (jnp.tanh(0.5 * x) + 1.0)


def _softplus(x):
    return jnp.maximum(x, 0.0) + jnp.log(1.0 + jnp.exp(-jnp.abs(x)))


def _rms(x, g):
    return x * lax.rsqrt(jnp.mean(x * x, axis=-1, keepdims=True) + RMS_EPS) * g


def _segsum_impl(x):
    n, w = x.shape[-1], 2 * LANES
    r = lax.shift_right_logical(lax.broadcasted_iota(jnp.int32, (w, w), 0), 6)
    c = lax.shift_right_logical(lax.broadcasted_iota(jnp.int32, (w, w), 1), 6)
    bd = (r == c).astype(BF16)
    hi = x.astype(BF16)
    rest = x - hi.astype(F32)
    mid = rest.astype(BF16)
    lo = (rest - mid.astype(F32)).astype(BF16)
    nn = ((1,), (0,))
    blocks = [_dot(hi[:, j:j + w], bd, nn) + _dot(mid[:, j:j + w], bd, nn) + _dot(lo[:, j:j + w], bd, nn)
              for j in range(0, n, w)]
    return jnp.concatenate(blocks, axis=1)


@jax.custom_vjp
def _segsum(x):
    return _segsum_impl(x)


_segsum.defvjp(lambda x: (_segsum_impl(x), None), lambda _, g: (_segsum_impl(g),))


@jax.custom_vjp
def _mmb(a, w):
    return _dot(a.astype(BF16), w.astype(BF16), ((1,), (0,)))


def _mmb_fwd(a, w):
    return _mmb(a, w), (a, w)


def _mmb_bwd(res, g):
    a, w = res
    gb = g.astype(BF16)
    return _dot(gb, w.astype(BF16), ((1,), (1,))), _dot(a.astype(BF16), gb, ((0,), (0,)))


_mmb.defvjp(_mmb_fwd, _mmb_bwd)


def _f_norm(x, g):
    return (_rms(x, g),)


def _f_post1(x, u, g2, g3):
    x1 = x + _rms(u, g2)
    return x1, _rms(x1, g3)


def _f_swiglu(ag, au):
    return (ag * _sigmoid(ag) * au,)


def _f_merge(pg1, pg2, m1, m2, b1, b2):
    return (_sigmoid(pg1 + b1) * m1 + _sigmoid(pg2 + b2) * m2,)


def _f_out(x1, f, g4):
    return (x1 + _rms(f, g4),)


def _f_rwpre(pr, pk, pv, pz, qr, qk, qv, qz, mur, muk, muv, muz, w0, wup, a0, aup, gup, k_k, k_a):
    r = pr + (qr - pr) * mur
    k = pk + (qk - pk) * muk
    v = pv + (qv - pv) * muv
    z = pz + (qz - pz) * muz
    w_raw = w0 + _mmb(jnp.tanh(z), wup)
    lw = -jnp.exp(-_softplus(-w_raw) - 0.5)
    a = _sigmoid(a0 + _mmb(z, aup))
    g = _mmb(_sigmoid(z), gup)
    kk = k * k_k
    kap = kk * lax.rsqrt(jnp.maximum(_segsum(kk * kk), 1e-24))
    k2 = k * (1.0 + (a - 1.0) * k_a)
    return r, lw, k2, v, kap, a, g


def _f_rwpost(y, r, k2, v, g, lnx_w, lnx_b, r_k):
    inv = 1.0 / HEAD_DIM
    yc = y - _segsum(y) * inv
    var = _segsum(yc * yc) * inv
    yn = yc * lax.rsqrt(var + GN_EPS) * lnx_w + lnx_b
    bonus = _segsum(r * k2 * r_k) * v
    return ((yn + bonus) * g,)


RW_GROUPS = (0, 512, 1024, 1536, RW_COLS)


def _column_groups(p):
    return [p[:, a:b] for a, b in zip(RW_GROUPS[:-1], RW_GROUPS[1:])]


def _previous_tokens(p, halo, first_of_sequence):
    rows = lax.broadcasted_iota(jnp.int32, (p.shape[0], 1), 0)
    before = jnp.where(first_of_sequence, 0.0, halo[SUBLANES - 1:SUBLANES, :])
    return jnp.where(rows == 0, before, pltpu.roll(p, 1, axis=0))


def _halo_spec(tile, order):
    per = tile // SUBLANES
    return pl.BlockSpec((SUBLANES, RW_COLS), lambda i: (jnp.maximum(order(i) * per - 1, 0), 0))


def _rw_pre(p_rw, params, S, tile=128):
    T = p_rw.shape[0]
    tile = min(tile, T)
    assert S % tile == 0
    n_p = len(params)

    def body(*refs):
        p_ref, halo_ref = refs[0], refs[1]
        par = [x[...].astype(F32) for x in refs[2:2 + n_p]]
        p = p_ref[...]
        first = lax.rem(pl.program_id(0) * tile, S) == 0
        prev = _previous_tokens(p, halo_ref[...], first)
        for o_ref, val in zip(refs[2 + n_p:], _f_rwpre(*_column_groups(p), *_column_groups(prev), *par)):
            o_ref[...] = val

    out_spec = pl.BlockSpec((tile, RW_WIDTH), lambda i: (i, 0))
    return pl.pallas_call(
        body, name="rw_pre",
        grid=(T // tile,),
        in_specs=[pl.BlockSpec((tile, RW_COLS), lambda i: (i, 0)), _halo_spec(tile, lambda i: i)]
                 + [_full_spec(q.shape) for q in params],
        out_specs=[out_spec] * 7,
        out_shape=[jax.ShapeDtypeStruct((T, RW_WIDTH), F32)] * 7,
        compiler_params=_params(("parallel",)),
    )(p_rw, p_rw, *params)


def _rw_pre_bwd(p_rw, params, cts, S, tile=128):
    T = p_rw.shape[0]
    tile = min(tile, T)
    assert S % tile == 0
    nt = T // tile
    n_p = len(params)
    ct_flat = [c for group in cts for c in group]
    ct_sizes = [len(group) for group in cts]
    n_ct = len(ct_flat)

    def body(*refs):
        p_ref, halo_ref = refs[0], refs[1]
        par = [x[...].astype(F32) for x in refs[2:2 + n_p]]
        ct_vals = [x[...] for x in refs[2 + n_p:2 + n_p + n_ct]]
        dp_ref = refs[2 + n_p + n_ct]
        dpar_refs = refs[3 + n_p + n_ct:3 + 2 * n_p + n_ct]
        carry = refs[-1]
        step = pl.program_id(0)

        @pl.when(step == 0)
        def _():
            carry[...] = jnp.zeros_like(carry)
            for ref in dpar_refs:
                ref[...] = jnp.zeros_like(ref)

        ct_in, q = [], 0
        for n in ct_sizes:
            ct_in.append(functools.reduce(lambda u, v: u + v, ct_vals[q:q + n]))
            q += n
        p = p_ref[...]
        first = lax.rem((nt - 1 - step) * tile, S) == 0
        prev = _previous_tokens(p, halo_ref[...], first)
        _, vjp = jax.vjp(_f_rwpre, *_column_groups(p), *_column_groups(prev), *par)
        grads = vjp(tuple(ct_in))
        d_here = jnp.concatenate(grads[0:4], axis=1)
        d_prev = jnp.concatenate(grads[4:8], axis=1)
        rows = lax.broadcasted_iota(jnp.int32, (tile, 1), 0)
        from_next = jnp.where(rows == tile - 1, carry[0:1, :], pltpu.roll(d_prev, tile - 1, axis=0))
        dp_ref[...] = (d_here + from_next).astype(dp_ref.dtype)
        carry[...] = jnp.broadcast_to(jnp.where(first, 0.0, d_prev[0:1, :]), carry.shape)
        for ref, g in zip(dpar_refs, grads[8:]):
            ref[...] += g

    back = lambda i: nt - 1 - i
    row = lambda w: pl.BlockSpec((tile, w), lambda i: (back(i), 0))
    res = pl.pallas_call(
        body, name="rw_pre_bwd",
        grid=(nt,),
        in_specs=[row(RW_COLS), _halo_spec(tile, back)] + [_full_spec(q.shape) for q in params]
                 + [row(RW_WIDTH)] * n_ct,
        out_specs=[row(RW_COLS)] + [_full_spec(q.shape) for q in params],
        out_shape=[jax.ShapeDtypeStruct((T, RW_COLS), BF16)] + [jax.ShapeDtypeStruct(q.shape, F32) for q in params],
        scratch_shapes=[pltpu.VMEM((SUBLANES, RW_COLS), F32)],
        compiler_params=_params(("arbitrary",)),
    )(p_rw, p_rw, *params, *ct_flat)
    return res[0], res[1:]


def _loss_head(x1, f, target, g4, tile=256):
    T, D = x1.shape
    tile = min(tile, T)

    def body(x1_ref, f_ref, t_ref, g_ref, loss_ref, dx1_ref, df_ref, dg_ref):
        (y,), vjp = jax.vjp(_f_out, x1_ref[...], f_ref[...], g_ref[...])
        err = y - t_ref[...]
        dx1, df, dg = vjp((err * (1.0 / D),))
        dx1_ref[...] = dx1
        df_ref[...] = df.astype(df_ref.dtype)

        @pl.when(pl.program_id(0) == 0)
        def _():
            loss_ref[...] = jnp.zeros_like(loss_ref)
            dg_ref[...] = jnp.zeros_like(dg_ref)

        part = jnp.sum(jnp.sum(err * err, axis=1, keepdims=True), axis=0, keepdims=True) * (0.5 / D)
        loss_ref[...] += jnp.broadcast_to(part, loss_ref.shape)
        dg_ref[...] += dg

    row = pl.BlockSpec((tile, D), lambda i: (i, 0))
    return pl.pallas_call(
        body, name="loss_head",
        grid=(T // tile,),
        in_specs=[row, row, row, _full_spec(g4.shape)],
        out_specs=[_full_spec((SUBLANES, LANES)), row, row, _full_spec(g4.shape)],
        out_shape=[jax.ShapeDtypeStruct((SUBLANES, LANES), F32), jax.ShapeDtypeStruct((T, D), F32),
                   jax.ShapeDtypeStruct((T, D), BF16), jax.ShapeDtypeStruct(g4.shape, F32)],
        compiler_params=_params(("arbitrary",)),
    )(x1, f, target, g4)


def _nn(a, b):
    return _dot(a, b, ((1,), (0,)))


def _nt(a, b):
    return _dot(a, b, ((1,), (1,)))


def _tn(a, b):
    return _dot(a, b, ((0,), (0,)))


def _split_dot(x, u2):
    hi = x.astype(BF16)
    lo = (x - hi.astype(F32)).astype(BF16)
    return _nn(jnp.concatenate([hi, lo], axis=1), u2)


def _by_head(x, masks):
    return jnp.concatenate([(x * m).astype(BF16) for m in masks], axis=0)


def _fold_heads(x2, masks):
    R = x2.shape[0] // len(masks)
    return functools.reduce(lambda u, v: u + v, [x2[h * R:(h + 1) * R] * m for h, m in enumerate(masks)])


def _head_masks():
    lane = lax.broadcasted_iota(jnp.int32, (1, LANES), 1)
    return [((lane >= h * HEAD_DIM) & (lane < (h + 1) * HEAD_DIM)).astype(F32) for h in range(LANES // HEAD_DIM)]


def _key_tri(op):
    row = lax.broadcasted_iota(jnp.int32, (ATTN_KEYS, ATTN_KEYS), 0)
    col = lax.broadcasted_iota(jnp.int32, (ATTN_KEYS, ATTN_KEYS), 1)
    u = op(row, col).astype(BF16)
    return jnp.concatenate([u, u], axis=0)


def _causal(rows):
    row = lax.broadcasted_iota(jnp.int32, (rows, ATTN_KEYS), 0)
    col = lax.broadcasted_iota(jnp.int32, (rows, ATTN_KEYS), 1)
    return col < row


def _from_row(tree, r):
    return jax.tree.map(lambda x: x[r:], tree)


def _onto_rows(old, new, r):
    return jax.tree.map(lambda o, n: jnp.concatenate([o[:r], n], axis=0) if r else n, old, new)


def _sb_weights(qb16, kbh, c_fails, u_gt, strict, scale):
    z_all = _nt(qb16, kbh) * scale
    zs = [z_all[:, h * ATTN_KEYS:(h + 1) * ATTN_KEYS] for h in range(len(c_fails))]
    Ls = [jnp.minimum(-z, 0.0) - jnp.log(1.0 + jnp.exp(-jnp.abs(z))) for z in zs]
    Lms = Ls if strict is None else [jnp.where(strict, L, 0.0) for L in Ls]
    cums = [_split_dot(Lm, u_gt) for Lm in Lms]
    As = [jnp.exp(z + L + c + cum) for z, L, c, cum in zip(zs, Ls, c_fails, cums)]
    if strict is not None:
        As = [jnp.where(strict, A, 0.0) for A in As]
    return zs, Ls, Lms, As


def _attn_specs(S, qb):
    nq = S // qb
    q_spec = pl.BlockSpec((qb, LANES), lambda b, p, i: (b * nq + i, p))
    k_spec = pl.BlockSpec((S, LANES), lambda b, p, i: (b, SB_WIDTH // LANES + p))
    v_spec = pl.BlockSpec((S, LANES), lambda b, p, i: (b, 2 * SB_WIDTH // LANES + p))
    seq = pl.BlockSpec((S, LANES), lambda b, p, i: (b, p))
    return q_spec, k_spec, v_spec, q_spec, seq


def _key_walk(i, qb, block, carry):
    per = qb // ATTN_KEYS
    for sub in reversed(range(per)):
        carry = block(i * per + sub, carry, sub * ATTN_KEYS)
    return lax.fori_loop(0, i * per, lambda j, c: block(i * per - 1 - j, c, None), carry)


def _attn_fwd(proj, B, S):
    qb = min(ATTN_QUERIES, S)
    scale = HEAD_DIM ** -0.5

    def body(q_ref, k_ref, v_ref, o_ref):
        i = pl.program_id(2)
        masks = _head_masks()
        u_gt = _key_tri(lambda r, c: r > c)
        q16 = q_ref[...].astype(BF16)

        def block(J, carry, row0):
            r0 = pl.multiple_of(J * ATTN_KEYS, ATTN_KEYS)
            kbh = _by_head(k_ref[pl.ds(r0, ATTN_KEYS), :], masks)
            vbh = _by_head(v_ref[pl.ds(r0, ATTN_KEYS), :], masks)
            lo = row0 or 0
            strict = None if row0 is None else _causal(qb - lo)
            acc, cs = _from_row(carry, lo)
            _, _, Lms, As = _sb_weights(q16[lo:], kbh, cs, u_gt, strict, scale)
            acc = acc + _nn(jnp.concatenate([A.astype(BF16) for A in As], axis=1), vbh)
            cs = tuple(c + jnp.sum(Lm, axis=1, keepdims=True) for c, Lm in zip(cs, Lms))
            return _onto_rows(carry, (acc, cs), lo)

        zero_c = tuple(jnp.zeros((qb, 1), F32) for _ in masks)
        carry = _key_walk(i, qb, block, (jnp.zeros((qb, LANES), F32), zero_c))
        o_ref[...] = carry[0]

    q_spec, k_spec, v_spec, blk, _ = _attn_specs(S, qb)
    return pl.pallas_call(
        body, name="sb_attn_fwd",
        grid=(B, SB_WIDTH // LANES, S // qb),
        in_specs=[q_spec, k_spec, v_spec],
        out_specs=blk,
        out_shape=jax.ShapeDtypeStruct((B * S, SB_WIDTH), F32),
        compiler_params=_params(("parallel", "parallel", "arbitrary")),
    )(proj, proj, proj)


def _attn_bwd(proj, o, do, B, S):
    qb = min(ATTN_QUERIES, S)
    nq = S // qb
    scale = HEAD_DIM ** -0.5

    def body(q_ref, k_ref, v_ref, o_ref, do_ref, dq_ref, dk_out, dv_out, dk_ref, dv_ref):
        i = pl.program_id(2)

        @pl.when(i == 0)
        def _():
            dk_ref[...] = jnp.zeros_like(dk_ref)
            dv_ref[...] = jnp.zeros_like(dv_ref)

        masks = _head_masks()
        u_gt = _key_tri(lambda r, c: r > c)
        u_ge = _key_tri(lambda r, c: r >= c)
        heads = range(len(masks))
        q16 = q_ref[...].astype(BF16)
        do16 = do_ref[...].astype(BF16)
        od = o_ref[...] * do16.astype(F32)
        totals = tuple(jnp.sum(od * m, axis=1, keepdims=True) for m in masks)

        def block(J, carry, row0):
            r0 = pl.multiple_of(J * ATTN_KEYS, ATTN_KEYS)
            kbh = _by_head(k_ref[pl.ds(r0, ATTN_KEYS), :], masks)
            vbh = _by_head(v_ref[pl.ds(r0, ATTN_KEYS), :], masks)
            lo = row0 or 0
            strict = None if row0 is None else _causal(qb - lo)
            dq, c_fail, c_p = _from_row(carry, lo)
            tot = _from_row(totals, lo)
            zs, Ls, Lms, As = _sb_weights(q16[lo:], kbh, c_fail, u_gt, strict, scale)
            Abs = [A.astype(BF16) for A in As]
            dA_all = _nt(do16[lo:], vbh)
            Ps = [Abs[h].astype(F32) * dA_all[:, h * ATTN_KEYS:(h + 1) * ATTN_KEYS] for h in heads]
            afters = [c_p[h] + _split_dot(Ps[h], u_ge) for h in heads]
            sigs = [jnp.exp(zs[h] + Ls[h]) for h in heads]
            dzs = [(Ps[h] * (1.0 - sigs[h]) - sigs[h] * (tot[h] - afters[h])) * scale for h in heads]
            if strict is not None:
                dzs = [jnp.where(strict, dz, 0.0) for dz in dzs]
            dz_all = jnp.concatenate([dz.astype(BF16) for dz in dzs], axis=1)
            dv_ref[pl.ds(r0, ATTN_KEYS), :] += _fold_heads(_tn(jnp.concatenate(Abs, axis=1), do16[lo:]), masks)
            dk_ref[pl.ds(r0, ATTN_KEYS), :] += _fold_heads(_tn(dz_all, q16[lo:]), masks)
            dq = dq + _nn(dz_all, kbh)
            c_fail = tuple(c_fail[h] + jnp.sum(Lms[h], axis=1, keepdims=True) for h in heads)
            c_p = tuple(c_p[h] + jnp.sum(Ps[h], axis=1, keepdims=True) for h in heads)
            return _onto_rows(carry, (dq, c_fail, c_p), lo)

        zc = tuple(jnp.zeros((qb, 1), F32) for _ in masks)
        carry = _key_walk(i, qb, block, (jnp.zeros((qb, LANES), F32), zc, zc))
        dq_ref[...] = carry[0].astype(dq_ref.dtype)

        @pl.when(i == nq - 1)
        def _():
            dk_out[...] = dk_ref[...].astype(dk_out.dtype)
            dv_out[...] = dv_ref[...].astype(dv_out.dtype)

    q_spec, k_spec, v_spec, blk, seq = _attn_specs(S, qb)
    return pl.pallas_call(
        body, name="sb_attn_bwd",
        grid=(B, SB_WIDTH // LANES, nq),
        in_specs=[q_spec, k_spec, v_spec, blk, blk],
        out_specs=[blk, seq, seq],
        out_shape=[jax.ShapeDtypeStruct((B * S, SB_WIDTH), BF16)] * 3,
        scratch_shapes=[pltpu.VMEM((S, LANES), F32), pltpu.VMEM((S, LANES), F32)],
        compiler_params=_params(("parallel", "parallel", "arbitrary")),
    )(proj, proj, proj, o, do)


_BATCHED = {"nn": "gmk,gkn->gmn", "nt": "gmk,gnk->gmn", "tn": "gkm,gkn->gmn"}


def _bdot_raw(a, b, kind, passes):
    e = functools.partial(jnp.einsum, _BATCHED[kind], preferred_element_type=F32)
    ah, bh = a.astype(BF16), b.astype(BF16)
    if passes == 1:
        return e(ah, bh)
    al, bl = (a - ah.astype(F32)).astype(BF16), (b - bh.astype(F32)).astype(BF16)
    return e(ah, bh) + e(ah, bl) + e(al, bh)


@functools.partial(jax.custom_vjp, nondiff_argnums=(2, 3))
def _bdot(a, b, kind, passes):
    return _bdot_raw(a, b, kind, passes)


def _bdot_fwd(a, b, kind, passes):
    return _bdot_raw(a, b, kind, passes), (a, b)


def _bdot_bwd(kind, passes, res, g):
    a, b = res
    if kind == "nn":
        return _bdot_raw(g, b, "nt", passes), _bdot_raw(a, g, "tn", passes)
    if kind == "nt":
        return _bdot_raw(g, b, "nn", passes), _bdot_raw(g, a, "tn", passes)
    return _bdot_raw(b, g, "nt", passes), _bdot_raw(a, g, "nn", passes)


_bdot.defvjp(_bdot_fwd, _bdot_bwd)


def _wkv_chunk(S0, r, lw, k, v, kap, a):
    G, C, N = r.shape
    row = lax.broadcasted_iota(jnp.int32, (C, C), 0)
    col = lax.broadcasted_iota(jnp.int32, (C, C), 1)
    incl = (col <= row).astype(F32)
    strict = (col < row).astype(F32)
    cum = _bdot(jnp.broadcast_to(incl, (G, C, C)), lw, "nn", 3)
    e_pos = jnp.exp(cum)
    e_neg = jnp.exp(-cum)
    al = -kap * jnp.exp(cum - lw)
    be = kap * a * e_neg
    kt = k * e_neg
    rt = r * e_pos
    bk = jnp.concatenate([be, kt], axis=1)
    mask = jnp.concatenate([jnp.concatenate([strict, strict], axis=1), jnp.concatenate([incl, incl], axis=1)], axis=0)
    m_all = _bdot(jnp.concatenate([al, rt], axis=1), bk, "nt", 3) * mask
    m_ab, m_ak = m_all[:, :C, :C], m_all[:, :C, C:]
    m_rb, m_rk = m_all[:, C:, :C], m_all[:, C:, C:]
    S0t = jnp.swapaxes(S0, 1, 2)
    sa = _bdot(jnp.concatenate([al, m_ak], axis=2), jnp.concatenate([S0t, v], axis=1), "nn", 3)
    p = m_ab
    steps = max(1, (C - 1).bit_length())
    for j in range(steps):
        sa = sa + _bdot(p, sa, "nn", 1)
        if j + 1 < steps:
            p = _bdot(p, p, "nn", 1)
    y = _bdot(jnp.concatenate([rt, m_rb, m_rk], axis=2), jnp.concatenate([S0t, sa, v], axis=1), "nn", 3)
    S1 = (S0 + _bdot(jnp.concatenate([sa, v], axis=1), bk, "tn", 1)) * e_pos[:, C - 1:C, :]
    return y, S1


def _split_heads(x):
    return jnp.stack([x[:, h * HEAD_DIM:(h + 1) * HEAD_DIM] for h in range(x.shape[1] // HEAD_DIM)], axis=0)


def _merge_heads(x):
    return jnp.concatenate([x[h] for h in range(x.shape[0])], axis=1)


def _seq_heads(ref):
    return jnp.concatenate([_split_heads(ref[s]) for s in range(ref.shape[0])], axis=0)


def _store_seq_heads(ref, x):
    heads = x.shape[0] // ref.shape[0]
    for s in range(ref.shape[0]):
        ref[s] = _merge_heads(x[s * heads:(s + 1) * heads])


def _wkv_fwd(r, lw, k, v, kap, a, B, S):
    C, H, N = WKV_CHUNK, RW_WIDTH // HEAD_DIM, HEAD_DIM
    nc = S // C
    Q = min(WKV_SEQS, B)

    def body(r_ref, lw_ref, k_ref, v_ref, kap_ref, a_ref, y_ref, st_ref, s_scr):
        @pl.when(pl.program_id(1) == 0)
        def _():
            s_scr[...] = jnp.zeros_like(s_scr)

        S0 = s_scr[...]
        for s in range(Q):
            st_ref[s, 0] = S0[s * H:(s + 1) * H]
        args = [_seq_heads(ref) for ref in (r_ref, lw_ref, k_ref, v_ref, kap_ref, a_ref)]
        y, S1 = _wkv_chunk(S0, *args)
        s_scr[...] = S1
        _store_seq_heads(y_ref, y)

    row_spec = pl.BlockSpec((Q, C, RW_WIDTH), lambda b, c: (b, c, 0))
    seqs = lambda t: t.reshape(B, S, RW_WIDTH)
    y, states = pl.pallas_call(
        body, name="wkv_fwd",
        grid=(B // Q, nc),
        in_specs=[row_spec] * 6,
        out_specs=[row_spec, pl.BlockSpec((Q, 1, H, N, N), lambda b, c: (b, c, 0, 0, 0))],
        out_shape=[jax.ShapeDtypeStruct((B, S, RW_WIDTH), F32), jax.ShapeDtypeStruct((B, nc, H, N, N), F32)],
        scratch_shapes=[pltpu.VMEM((Q * H, N, N), F32)],
        compiler_params=_params(("arbitrary", "arbitrary")),
    )(*map(seqs, (r, lw, k, v, kap, a)))
    return y.reshape(B * S, RW_WIDTH), states


def _wkv_bwd(r, lw, k, v, kap, a, states, dy, B, S):
    C, H, N = WKV_CHUNK, RW_WIDTH // HEAD_DIM, HEAD_DIM
    nc = S // C
    Q = min(WKV_SEQS, B)

    def body(r_ref, lw_ref, k_ref, v_ref, kap_ref, a_ref, st_ref, dy_ref,
             dr_ref, dlw_ref, dk_ref, dv_ref, dkap_ref, da_ref, ds_scr):
        @pl.when(pl.program_id(1) == 0)
        def _():
            ds_scr[...] = jnp.zeros_like(ds_scr)

        args = [_seq_heads(ref) for ref in (r_ref, lw_ref, k_ref, v_ref, kap_ref, a_ref)]
        S0 = jnp.concatenate([st_ref[s, 0] for s in range(Q)], axis=0)
        _, vjp = jax.vjp(_wkv_chunk, S0, *args)
        g = vjp((_seq_heads(dy_ref), ds_scr[...]))
        ds_scr[...] = g[0]
        for ref, gv in zip((dr_ref, dlw_ref, dk_ref, dv_ref, dkap_ref, da_ref), g[1:]):
            _store_seq_heads(ref, gv)

    row_spec = pl.BlockSpec((Q, C, RW_WIDTH), lambda b, c: (b, nc - 1 - c, 0))
    st_spec = pl.BlockSpec((Q, 1, H, N, N), lambda b, c: (b, nc - 1 - c, 0, 0, 0))
    seqs = lambda t: t.reshape(B, S, RW_WIDTH)
    res = pl.pallas_call(
        body, name="wkv_bwd",
        grid=(B // Q, nc),
        in_specs=[row_spec] * 6 + [st_spec, row_spec],
        out_specs=[row_spec] * 6,
        out_shape=[jax.ShapeDtypeStruct((B, S, RW_WIDTH), F32)] * 6,
        scratch_shapes=[pltpu.VMEM((Q * H, N, N), F32)],
        compiler_params=_params(("arbitrary", "arbitrary"), vmem=WKV_BWD_VMEM),
    )(*map(seqs, (r, lw, k, v, kap, a)), states, seqs(dy))
    return [t.reshape(B * S, RW_WIDTH) for t in res]


HBM = pl.BlockSpec(memory_space=pl.ANY)


def _place():
    return lax.axis_index("x"), lax.axis_index("y"), lax.axis_index("c")


def _other_chips(x, y):
    return [(1 - x, y), (x, 1 - y), (1 - x, 1 - y)]


def _all_gather_chips(shards):
    n = len(shards)

    def body(*refs):
        ins, outs = refs[:n], refs[n:2 * n]
        ici_send, ici_recv, d2d_send, d2d_recv, local = refs[2 * n:]
        x, y, c = _place()
        me = 2 * x + y
        sib = (x, y, 1 - c)
        chips = _other_chips(x, y)
        started, copies = [], []
        for w in range(n):
            cp = pltpu.make_async_copy(ins[w].at[c], outs[w].at[me, c], local.at[w])
            cp.start()
            copies.append(cp)
            for j, (px, py) in enumerate(chips):
                rd = pltpu.make_async_remote_copy(
                    src_ref=ins[w].at[c], dst_ref=outs[w].at[me, c], send_sem=ici_send.at[3 * w + j],
                    recv_sem=ici_recv.at[3 * w + j], device_id=(px, py, c), device_id_type=MESH)
                rd.start()
                started.append(rd)
            rd = pltpu.make_async_remote_copy(
                src_ref=ins[w].at[c], dst_ref=outs[w].at[me, c], send_sem=d2d_send.at[4 * w + 3],
                recv_sem=d2d_recv.at[4 * w + 3], device_id=sib, device_id_type=MESH)
            rd.start()
            started.append(rd)
        for w in range(n):
            for j, (px, py) in enumerate(chips):
                src = 2 * px + py
                pltpu.make_async_remote_copy(
                    src_ref=ins[w].at[c], dst_ref=outs[w].at[src, c], send_sem=ici_send.at[3 * w + j],
                    recv_sem=ici_recv.at[3 * w + j], device_id=(px, py, c), device_id_type=MESH).wait_recv()
                rd = pltpu.make_async_remote_copy(
                    src_ref=outs[w].at[src, c], dst_ref=outs[w].at[src, c], send_sem=d2d_send.at[4 * w + j],
                    recv_sem=d2d_recv.at[4 * w + j], device_id=sib, device_id_type=MESH)
                rd.start()
                started.append(rd)
        for w in range(n):
            for j, (px, py) in enumerate(chips):
                pltpu.make_async_remote_copy(
                    src_ref=ins[w].at[c], dst_ref=outs[w].at[2 * px + py, 1 - c], send_sem=d2d_send.at[4 * w + j],
                    recv_sem=d2d_recv.at[4 * w + j], device_id=sib, device_id_type=MESH).wait_recv()
            pltpu.make_async_remote_copy(
                src_ref=ins[w].at[c], dst_ref=outs[w].at[me, 1 - c], send_sem=d2d_send.at[4 * w + 3],
                recv_sem=d2d_recv.at[4 * w + 3], device_id=sib, device_id_type=MESH).wait_recv()
        for rd in started:
            rd.wait_send()
        for cp in copies:
            cp.wait()

    return pl.pallas_call(
        body, name="gather_weights",
        in_specs=[HBM] * n, out_specs=[HBM] * n,
        out_shape=[jax.ShapeDtypeStruct((N_CHIPS,) + s.shape, s.dtype) for s in shards],
        scratch_shapes=[pltpu.SemaphoreType.DMA((3 * n,)), pltpu.SemaphoreType.DMA((3 * n,)),
                        pltpu.SemaphoreType.DMA((4 * n,)), pltpu.SemaphoreType.DMA((4 * n,)),
                        pltpu.SemaphoreType.DMA((n,))],
        compiler_params=pltpu.CompilerParams(has_side_effects=True),
    )(*shards)


def _pair_split(grads):
    n = len(grads)

    def body(*refs):
        ins, theirs = refs[:n], refs[n:2 * n]
        send, recv = refs[2 * n:]
        x, y, c = _place()
        sib = (x, y, 1 - c)
        rds = []
        for w in range(n):
            rd = pltpu.make_async_remote_copy(
                src_ref=ins[w].at[:, 1 - c], dst_ref=theirs[w], send_sem=send.at[w], recv_sem=recv.at[w],
                device_id=sib, device_id_type=MESH)
            rd.start()
            rds.append(rd)
        for rd in rds:
            rd.wait_recv()
        for rd in rds:
            rd.wait_send()

    return pl.pallas_call(
        body, name="grad_pair_split",
        in_specs=[HBM] * n, out_specs=[HBM] * n,
        out_shape=[jax.ShapeDtypeStruct((g.shape[0],) + g.shape[2:], g.dtype) for g in grads],
        scratch_shapes=[pltpu.SemaphoreType.DMA((n,)), pltpu.SemaphoreType.DMA((n,))],
        compiler_params=pltpu.CompilerParams(has_side_effects=True),
    )(*grads)


def _chip_scatter(parts):
    n = len(parts)

    def body(*refs):
        ins, outs = refs[:n], refs[n:2 * n]
        send, recv = refs[2 * n:]
        x, y, c = _place()
        me = 2 * x + y
        rds = []
        for w in range(n):
            for j, (px, py) in enumerate(_other_chips(x, y)):
                s = 3 * w + j
                rd = pltpu.make_async_remote_copy(
                    src_ref=ins[w].at[2 * px + py], dst_ref=outs[w].at[j], send_sem=send.at[s], recv_sem=recv.at[s],
                    device_id=(px, py, c), device_id_type=MESH)
                rd.start()
                rds.append(rd)
        for w in range(n):
            for j, (px, py) in enumerate(_other_chips(x, y)):
                s = 3 * w + j
                pltpu.make_async_remote_copy(
                    src_ref=ins[w].at[me], dst_ref=outs[w].at[j], send_sem=send.at[s], recv_sem=recv.at[s],
                    device_id=(px, py, c), device_id_type=MESH).wait_recv()
        for rd in rds:
            rd.wait_send()

    return pl.pallas_call(
        body, name="grad_chip_scatter",
        in_specs=[HBM] * n, out_specs=[HBM] * n,
        out_shape=[jax.ShapeDtypeStruct((N_CHIPS - 1,) + p.shape[1:], p.dtype) for p in parts],
        scratch_shapes=[pltpu.SemaphoreType.DMA((3 * n,)), pltpu.SemaphoreType.DMA((3 * n,))],
        compiler_params=pltpu.CompilerParams(has_side_effects=True),
    )(*parts)


def _pair_join(bufs):
    n = len(bufs)

    def body(*refs):
        ins, outs = refs[:n], refs[n:2 * n]
        send, recv = refs[2 * n:]
        x, y, c = _place()
        sib = (x, y, 1 - c)
        rds = []
        for w in range(n):
            rd = pltpu.make_async_remote_copy(
                src_ref=ins[w].at[c], dst_ref=outs[w].at[c], send_sem=send.at[w], recv_sem=recv.at[w],
                device_id=sib, device_id_type=MESH)
            rd.start()
            rds.append(rd)
        for w in range(n):
            pltpu.make_async_remote_copy(
                src_ref=ins[w].at[c], dst_ref=outs[w].at[1 - c], send_sem=send.at[w], recv_sem=recv.at[w],
                device_id=sib, device_id_type=MESH).wait_recv()
        for rd in rds:
            rd.wait_send()

    return pl.pallas_call(
        body, name="grad_pair_join",
        in_specs=[HBM] * n, out_specs=[HBM] * n,
        out_shape=[jax.ShapeDtypeStruct(b.shape, b.dtype) for b in bufs],
        input_output_aliases={w: w for w in range(n)},
        scratch_shapes=[pltpu.SemaphoreType.DMA((n,)), pltpu.SemaphoreType.DMA((n,))],
        compiler_params=pltpu.CompilerParams(has_side_effects=True),
    )(*bufs)


def _all_reduce_small(packed):
    R = packed.shape[0]

    def body(x_ref, o_ref, buf, send, recv):
        x, y, c = _place()
        me = 4 * x + 2 * y + c
        buf[me] = x_ref[...]
        rds = []
        for rel in range(1, N_DEV):
            fx, fy, fc = (rel >> 2) & 1, (rel >> 1) & 1, rel & 1
            peer = (1 - x if fx else x, 1 - y if fy else y, 1 - c if fc else c)
            rd = pltpu.make_async_remote_copy(
                src_ref=x_ref, dst_ref=buf.at[me], send_sem=send.at[rel - 1], recv_sem=recv.at[rel - 1],
                device_id=peer, device_id_type=MESH)
            rd.start()
            rds.append((rd, peer))
        for rel in range(1, N_DEV):
            rd, (px, py, pc) = rds[rel - 1]
            pltpu.make_async_remote_copy(
                src_ref=x_ref, dst_ref=buf.at[4 * px + 2 * py + pc], send_sem=send.at[rel - 1], recv_sem=recv.at[rel - 1],
                device_id=(px, py, pc), device_id_type=MESH).wait_recv()
        for rd, _ in rds:
            rd.wait_send()
        total = buf[0]
        for d in range(1, N_DEV):
            total = total + buf[d]
        o_ref[...] = total

    return pl.pallas_call(
        body, name="all_reduce_small",
        in_specs=[pl.BlockSpec(memory_space=pltpu.VMEM)],
        out_specs=pl.BlockSpec(memory_space=pltpu.VMEM),
        out_shape=jax.ShapeDtypeStruct(packed.shape, F32),
        scratch_shapes=[pltpu.VMEM((N_DEV, R, LANES), F32), pltpu.SemaphoreType.DMA((N_DEV - 1,)),
                        pltpu.SemaphoreType.DMA((N_DEV - 1,))],
        compiler_params=pltpu.CompilerParams(has_side_effects=True),
    )(packed)


def _pair_sum(name, split, theirs, core):
    n_chip, _, Rh, C = split.shape
    tile = _div_tile(Rh, 256, 2 * SUBLANES)
    nt = Rh // tile

    def body(core_ref, a_ref, b_ref, o_ref):
        o_ref[...] = (a_ref[...] + b_ref[...]).astype(o_ref.dtype)

    return pl.pallas_call(
        body, name=name,
        grid_spec=pltpu.PrefetchScalarGridSpec(
            num_scalar_prefetch=1,
            grid=(n_chip, nt),
            in_specs=[pl.BlockSpec((None, None, tile, C), lambda j, i, core_ref: (j, core_ref[0], i, 0)),
                      pl.BlockSpec((None, tile, C), lambda j, i, core_ref: (j, i, 0))],
            out_specs=pl.BlockSpec((None, tile, C), lambda j, i, core_ref: (j, i, 0)),
        ),
        out_shape=jax.ShapeDtypeStruct((n_chip, Rh, C), BF16),
        compiler_params=_params(("parallel", "parallel")),
    )(core, split, theirs)


def _chip_sum(name, own, landed, core):
    n_in, Rh, C = landed.shape
    tile = _div_tile(Rh, 256, 2 * SUBLANES)

    def body(core_ref, *refs):
        total = refs[0][...].astype(F32)
        for ref in refs[1:n_in + 1]:
            total = total + ref[...].astype(F32)
        refs[n_in + 1][...] = total

    slot = lambda j: pl.BlockSpec((None, tile, C), lambda i, core_ref: (j, i, 0))
    return pl.pallas_call(
        body, name=name,
        grid_spec=pltpu.PrefetchScalarGridSpec(
            num_scalar_prefetch=1,
            grid=(Rh // tile,),
            in_specs=[pl.BlockSpec((None, tile, C), lambda i, core_ref: (core_ref[1], i, 0))]
                     + [slot(j) for j in range(n_in)],
            out_specs=pl.BlockSpec((None, tile, C), lambda i, core_ref: (core_ref[0], i, 0)),
        ),
        out_shape=jax.ShapeDtypeStruct((2, Rh, C), F32),
        compiler_params=_params(("parallel",)),
    )(core, own, *([landed] * n_in))


def _adamw(name, w, g, m, v):
    R, C = w.shape
    tile = _div_tile(R, 256, SUBLANES)
    c1 = 1.0 / (1.0 - ADAM_B1 ** ADAM_STEP)
    c2 = 1.0 / (1.0 - ADAM_B2 ** ADAM_STEP)

    def body(w_ref, g_ref, m_ref, v_ref, d_ref, nm_ref, nv_ref):
        g_ = g_ref[...]
        nm = ADAM_B1 * m_ref[...] + (1.0 - ADAM_B1) * g_
        nv = ADAM_B2 * v_ref[...] + (1.0 - ADAM_B2) * (g_ * g_)
        d_ref[...] = -ADAM_LR * ((nm * c1) / (jnp.sqrt(nv * c2) + ADAM_EPS) + ADAM_WD * w_ref[...])
        nm_ref[...] = nm
        nv_ref[...] = nv

    spec = pl.BlockSpec((tile, C), lambda i: (i, 0))
    return pl.pallas_call(
        body, name=name,
        grid=(R // tile,),
        in_specs=[spec] * 4, out_specs=[spec] * 3,
        out_shape=[jax.ShapeDtypeStruct((R, C), F32)] * 3,
        compiler_params=_params(("parallel",)),
    )(w, g, m, v)


SMALL =["norm_mix_pre", "b_gate", "mu_rw", "w0", "a0", "k_k", "k_a", "r_k", "lnx_w", "lnx_b",
         "norm_mix_post", "norm_ffn_pre", "norm_ffn_post"]
BIG = ["w_in", "w_up", "a_up", "g_up", "w_sb_out", "w_rw_out", "w_o", "w_ffn_gate", "w_ffn_up", "w_ffn_down"]
ROW_SHARDED = ("w_o", "w_ffn_down")
ORDER = ["norm_mix_pre", "w_in", "b_gate", "mu_rw", "w0", "w_up", "a0", "a_up", "g_up", "k_k", "k_a", "r_k",
         "lnx_w", "lnx_b", "w_sb_out", "w_rw_out", "w_o", "norm_mix_post", "norm_ffn_pre", "w_ffn_gate",
         "w_ffn_up", "w_ffn_down", "norm_ffn_post"]


def _pack_small(vals, extra_rows=0):
    rows = jnp.concatenate([vals[n].reshape(-1, LANES) for n in SMALL], axis=0)
    pad = (-(rows.shape[0] + extra_rows)) % SUBLANES + extra_rows
    return jnp.pad(rows, ((0, pad), (0, 0)))


def _unpack_small(packed, shapes):
    out, r = {}, 0
    for n in SMALL:
        size = 1
        for s in shapes[n]:
            size *= s
        out[n] = packed[r:r + size // LANES].reshape(shapes[n])
        r += size // LANES
    return out


def kernel(x, norm_mix_pre, w_in, b_gate, mu_rw, w0, w_up, a0, a_up, g_up, k_k, k_a, r_k, lnx_w, lnx_b, w_sb_out, w_rw_out, w_o, norm_mix_post, norm_ffn_pre, w_ffn_gate, w_ffn_up, w_ffn_down, norm_ffn_post, loss_target, m_norm_mix_pre, m_w_in, m_b_gate, m_mu_rw, m_w0, m_w_up, m_a0, m_a_up, m_g_up, m_k_k, m_k_a, m_r_k, m_lnx_w, m_lnx_b, m_w_sb_out, m_w_rw_out, m_w_o, m_norm_mix_post, m_norm_ffn_pre, m_w_ffn_gate, m_w_ffn_up, m_w_ffn_down, m_norm_ffn_post, v_norm_mix_pre, v_w_in, v_b_gate, v_mu_rw, v_w0, v_w_up, v_a0, v_a_up, v_g_up, v_k_k, v_k_a, v_r_k, v_lnx_w, v_lnx_b, v_w_sb_out, v_w_rw_out, v_w_o, v_norm_mix_post, v_norm_ffn_pre, v_w_ffn_gate, v_w_ffn_up, v_w_ffn_down, v_norm_ffn_post):
    W = dict(norm_mix_pre=norm_mix_pre, w_in=w_in, b_gate=b_gate, mu_rw=mu_rw, w0=w0, w_up=w_up, a0=a0, a_up=a_up,
             g_up=g_up, k_k=k_k, k_a=k_a, r_k=r_k, lnx_w=lnx_w, lnx_b=lnx_b, w_sb_out=w_sb_out, w_rw_out=w_rw_out,
             w_o=w_o, norm_mix_post=norm_mix_post, norm_ffn_pre=norm_ffn_pre, w_ffn_gate=w_ffn_gate,
             w_ffn_up=w_ffn_up, w_ffn_down=w_ffn_down, norm_ffn_post=norm_ffn_post)
    Mo = dict(norm_mix_pre=m_norm_mix_pre, w_in=m_w_in, b_gate=m_b_gate, mu_rw=m_mu_rw, w0=m_w0, w_up=m_w_up, a0=m_a0,
              a_up=m_a_up, g_up=m_g_up, k_k=m_k_k, k_a=m_k_a, r_k=m_r_k, lnx_w=m_lnx_w, lnx_b=m_lnx_b,
              w_sb_out=m_w_sb_out, w_rw_out=m_w_rw_out, w_o=m_w_o, norm_mix_post=m_norm_mix_post,
              norm_ffn_pre=m_norm_ffn_pre, w_ffn_gate=m_w_ffn_gate, w_ffn_up=m_w_ffn_up, w_ffn_down=m_w_ffn_down,
              norm_ffn_post=m_norm_ffn_post)
    Vo = dict(norm_mix_pre=v_norm_mix_pre, w_in=v_w_in, b_gate=v_b_gate, mu_rw=v_mu_rw, w0=v_w0, w_up=v_w_up, a0=v_a0,
              a_up=v_a_up, g_up=v_g_up, k_k=v_k_k, k_a=v_k_a, r_k=v_r_k, lnx_w=v_lnx_w, lnx_b=v_lnx_b,
              w_sb_out=v_w_sb_out, w_rw_out=v_w_rw_out, w_o=v_w_o, norm_mix_post=v_norm_mix_post,
              norm_ffn_pre=v_norm_ffn_pre, w_ffn_gate=v_w_ffn_gate, w_ffn_up=v_w_ffn_up, w_ffn_down=v_w_ffn_down,
              norm_ffn_post=v_norm_ffn_post)
    shapes = {n: W[n].shape for n in ORDER}
    B, S, D = x.shape
    T = B * S
    x2 = x.reshape(T, D)
    tgt = loss_target.reshape(T, D)
    vec = {n: W[n].reshape(1, -1) for n in SMALL}

    work = lambda t, n: t[0] if n in ROW_SHARDED else jnp.swapaxes(t[0], 0, 1)
    halved = [work(W[n], n).astype(BF16) for n in BIG]
    halved = [h.reshape(2, h.shape[0] // 2, h.shape[1]) for h in halved]
    full = {n: gth.reshape(-1, gth.shape[3]) for n, gth in zip(BIG, _all_gather_chips(halved))}
    w_in_t = full["w_in"]
    w_sb_t, w_rw_t, w_gt_t = w_in_t[:SB_COLS], w_in_t[SB_COLS:SB_COLS + RW_COLS], w_in_t[SB_COLS + RW_COLS:]
    lora_rows = {"w_up": 0, "a_up": 64, "g_up": 128}
    lora = {n: jnp.pad(full[n].T, ((r0, LORA_COLS - r0 - full[n].shape[1]), (0, 0))) for n, r0 in lora_rows.items()}
    mu = vec["mu_rw"]
    mu_parts = [mu[:, :512], mu[:, 512:1024], mu[:, 1024:1536], mu[:, 1536:]]
    b1, b2 = vec["b_gate"][:, :D], vec["b_gate"][:, D:]

    (h1,) = _rowwise("norm_mix_pre", _f_norm, [(x2, D, 0)], [vec["norm_mix_pre"]], [(D, BF16)])
    p_sb = _mm("proj_sb", h1, w_sb_t, tb=True)
    p_rw = _mm("proj_rw", h1, w_rw_t, tb=True)
    p_gt = _mm("proj_gate", h1, w_gt_t, tb=True)
    o_sb = _attn_fwd(p_sb, B, S)
    pre_params = mu_parts + [vec["w0"], lora["w_up"], vec["a0"], lora["a_up"], lora["g_up"], vec["k_k"], vec["k_a"]]
    r_, lw_, k2_, v_, kap_, a_, g_ = _rw_pre(p_rw, pre_params, S)
    y_wkv, states = _wkv_fwd(r_, lw_, k2_, v_, kap_, a_, B, S)
    post_rows = [(y_wkv, 512, 0), (r_, 512, 0), (k2_, 512, 0), (v_, 512, 0), (g_, 512, 0)]
    post_params = [vec["lnx_w"], vec["lnx_b"], vec["r_k"]]
    (o_rw,) = _rowwise("rw_post", _f_rwpost, post_rows, post_params, [(512, BF16)])
    m1 = _mm("mix_sb_out", o_sb, full["w_sb_out"], tb=True)
    m2 = _mm("mix_rw_out", o_rw, full["w_rw_out"], tb=True)
    merge_rows = [(p_gt, D, 0), (p_gt, D, 1), (m1, D, 0), (m2, D, 0)]
    (merged,) = _rowwise("merge", _f_merge, merge_rows, [b1, b2], [(D, BF16)])
    u = _mm("mix_out", merged, full["w_o"])
    post1_params = [vec["norm_mix_post"], vec["norm_ffn_pre"]]
    x1, h2 = _rowwise("post_mix", _f_post1, [(x2, D, 0), (u, D, 0)], post1_params, [(D, F32), (D, BF16)])
    ag, au, sw = _mm_fused("ffn_in", [h2, h2], [full["w_ffn_gate"], full["w_ffn_up"]], [BF16] * 3, tb=True,
                           epilogue=lambda gu, _: (gu[0], gu[1], _f_swiglu(*gu)[0]))
    f = _mm("ffn_down", sw, full["w_ffn_down"])
    loss_part, dx1, df, dg4 = _loss_head(x1, f, tgt, vec["norm_ffn_post"])

    gbig, gsmall = {}, {"norm_ffn_post": dg4}
    gbig["w_ffn_down"] = _mm("g_ffn_down", sw, df, ta=True)

    def swiglu_back(dsw, gu):
        return jax.vjp(_f_swiglu, *gu)[1]((dsw[0],))

    dag, dau = _mm_fused("ffn_back", [df], [full["w_ffn_down"]], [BF16] * 2, tb=True, extras=[ag, au],
                         epilogue=swiglu_back)
    (dh2,) = _mm_fused("d_h2", [dag, dau], [full["w_ffn_gate"], full["w_ffn_up"]], [F32], add=True)
    gbig["w_ffn_gate"] = _mm("g_ffn_gate", dag, h2, ta=True)
    gbig["w_ffn_up"] = _mm("g_ffn_up", dau, h2, ta=True)
    (dx_res, du), (dg2, dg3) = _rowwise_vjp("post_mix_bwd", _f_post1, [(x2, D, 0), (u, D, 0)], post1_params,
                                            [[dx1], [dh2]], [True, True], [True, True], bf16_rows=(1,))
    gsmall["norm_mix_post"], gsmall["norm_ffn_pre"] = dg2, dg3
    dmerged = _mm("d_merged", du, full["w_o"], tb=True)
    gbig["w_o"] = _mm("g_w_o", merged, du, ta=True)
    (dpg1, dpg2, dm1, dm2), (db1, db2) = _rowwise_vjp("merge_bwd", _f_merge, merge_rows, [b1, b2], [[dmerged]],
                                                      [True] * 4, [True, True], bf16_rows=(0, 1, 2, 3))
    gsmall["b_gate"] = jnp.concatenate([db1, db2], axis=1)
    do_sb = _mm("d_o_sb", dm1, full["w_sb_out"])
    do_rw = _mm("d_o_rw", dm2, full["w_rw_out"])
    gbig["w_sb_out"] = _mm("g_sb_out", dm1, o_sb, ta=True)
    gbig["w_rw_out"] = _mm("g_rw_out", dm2, o_rw, ta=True)
    (dy_wkv, dr_a, dk2_a, dv_a, dg_), (dlnx_w, dlnx_b, dr_k) = _rowwise_vjp(
        "rw_post_bwd", _f_rwpost, post_rows, post_params, [[do_rw]], [True] * 5, [True] * 3)
    gsmall["lnx_w"], gsmall["lnx_b"], gsmall["r_k"] = dlnx_w, dlnx_b, dr_k
    dr_b, dlw, dk2_b, dv_b, dkap, da = _wkv_bwd(r_, lw_, k2_, v_, kap_, a_, states, dy_wkv, B, S)
    pre_cts = [[dr_a, dr_b], [dlw], [dk2_a, dk2_b], [dv_a, dv_b], [dkap], [da], [dg_]]
    dp_rw, dpre_params = _rw_pre_bwd(p_rw, pre_params, pre_cts, S)
    gsmall["mu_rw"] = jnp.concatenate(dpre_params[:4], axis=1)
    gsmall["w0"], gsmall["a0"], gsmall["k_k"], gsmall["k_a"] = dpre_params[4], dpre_params[6], dpre_params[9], dpre_params[10]
    glora = {"w_up": dpre_params[5][0:64].T, "a_up": dpre_params[7][64:128].T, "g_up": dpre_params[8][128:256].T}
    dq, dk, dv = _attn_bwd(p_sb, o_sb, do_sb, B, S)
    (dh1,) = _mm_fused("d_h1_sb", [dq, dk, dv], [w_sb_t[:512], w_sb_t[512:1024], w_sb_t[1024:]], [F32], add=True)
    dh1 = _mm("d_h1_rw", dp_rw, w_rw_t, acc=dh1)
    (dh1,) = _mm_fused("d_h1_gate", [dpg1, dpg2], [w_gt_t[:D], w_gt_t[D:]], [F32], add=True,
                       extras=[dh1], epilogue=lambda p, e: (p[0] + e[0],))
    gbig["w_in"] = jnp.concatenate(
        [_mm("g_in_" + tag, d, h1, ta=True)
         for tag, d in (("q", dq), ("k", dk), ("v", dv), ("rw", dp_rw), ("g1", dpg1), ("g2", dpg2))], axis=0)
    (grad_x2,), (dg1,) = _rowwise_vjp("norm_mix_pre_bwd", _f_norm, [(x2, D, 0)], [vec["norm_mix_pre"]], [[dh1]],
                                      [True], [True], add_to={0: dx_res})
    gsmall["norm_mix_pre"] = dg1
    gbig.update(glora)

    split = [gbig[n].reshape(N_CHIPS, 2, gbig[n].shape[0] // (2 * N_CHIPS), gbig[n].shape[1]) for n in BIG]
    core = jnp.stack([lax.axis_index("c"), 2 * lax.axis_index("x") + lax.axis_index("y")]).astype(jnp.int32)
    theirs = _pair_split(split)
    chip_sums = [_pair_sum("pair_sum_" + n, a, b, core) for n, a, b in zip(BIG, split, theirs)]
    landed = _chip_scatter(chip_sums)
    joined = _pair_join([_chip_sum("chip_sum_" + n, own, got, core) for n, own, got in zip(BIG, chip_sums, landed)])
    grads = {n: j.reshape(-1, j.shape[2]) for n, j in zip(BIG, joined)}

    small_local = _pack_small({n: gsmall[n] for n in SMALL}, extra_rows=1)
    loss_row = small_local.shape[0] - 1
    small_local = small_local.at[loss_row].set(loss_part[0])
    small_sum = _all_reduce_small(small_local)
    loss = small_sum[loss_row, 0]

    delta, new_m, new_v = {}, {}, {}
    unwork = lambda t, n: (t if n in ROW_SHARDED else jnp.swapaxes(t, 0, 1))[None]
    for n in BIG:
        d_, m_, v2_ = _adamw("adamw_" + n, work(W[n], n), grads[n], work(Mo[n], n), work(Vo[n], n))
        delta[n], new_m[n], new_v[n], grads[n] = (unwork(t, n) for t in (d_, m_, v2_, grads[n]))
    pk = lambda src: _pack_small({n: src[n] for n in SMALL}, extra_rows=1)
    d_s, m_s, v_s = _adamw("adamw_small", pk(W), small_sum.at[loss_row].set(0.0), pk(Mo), pk(Vo))
    for dst, packed in ((grads, small_sum), (delta, d_s), (new_m, m_s), (new_v, v_s)):
        dst.update(_unpack_small(packed, shapes))

    return (loss, grad_x2.reshape(B, S, D), *[grads[n] for n in ORDER], *[delta[n] for n in ORDER],
            *[new_m[n] for n in ORDER], *[new_v[n] for n in ORDER])
```

```python
import functools

import jax
import jax.numpy as jnp
from jax import lax
from jax.experimental import pallas as pl
from jax.experimental.pallas import tpu as pltpu

F32 = jnp.float32
BF16 = jnp.bfloat16
MESH = pl.DeviceIdType.MESH

D_MODEL = 1024
SB_HEADS = 8
HEAD_DIM = 64
SB_WIDTH = SB_HEADS * HEAD_DIM
RW_WIDTH = 512
LORA_COLS = 256
SB_COLS = 3 * SB_WIDTH
RW_COLS = 3 * RW_WIDTH + LORA_COLS
GATE_COLS = 2 * D_MODEL
D_FF = 2816
RMS_EPS = 1e-6
GN_EPS = HEAD_DIM * 1e-5
WKV_CHUNK = 64
WKV_SEQS = 4
ATTN_QUERIES = 512
ATTN_KEYS = 128
LANES = 128
SUBLANES = 8
N_CHIPS = 4
N_DEV = 8

ADAM_LR = 0.001
ADAM_B1 = 0.9
ADAM_B2 = 0.999
ADAM_EPS = 1e-08
ADAM_WD = 0.01
ADAM_STEP = 10

VMEM_LIMIT = 48 * 1024 * 1024
WKV_BWD_VMEM = 58 * 1024 * 1024


def _params(sem=None, vmem=VMEM_LIMIT, **kw):
    if sem is not None:
        kw["dimension_semantics"] = sem
    return pltpu.CompilerParams(vmem_limit_bytes=vmem, **kw)


def _div_tile(dim, pref, mult=LANES):
    if dim <= pref:
        return dim
    t = pref - pref % mult
    while t >= mult:
        if dim % t == 0:
            return t
        t -= mult
    return dim


def _dot(a, b, dims):
    return lax.dot_general(a, b, (dims, ((), ())), preferred_element_type=F32)


def _mm(name, a, b, *, ta=False, tb=False, acc=None, out_dtype=F32):
    if ta:
        K, M = a.shape
    else:
        M, K = a.shape
    N = b.shape[0] if tb else b.shape[1]
    if ta:
        tm, tn, tk = _div_tile(M, 1408), _div_tile(N, 1408), _div_tile(K, 512)
    else:
        tm, tn, tk = _div_tile(M, 512), _div_tile(N, 1408), _div_tile(K, 1408)
    nk = K // tk
    dims = ((0,) if ta else (1,), (1,) if tb else (0,))
    has_acc = acc is not None

    def body(*refs):
        a_ref, b_ref = refs[0], refs[1]
        part = _dot(a_ref[...].astype(BF16), b_ref[...].astype(BF16), dims)
        if nk == 1:
            o_ref = refs[-1]
            o_ref[...] = (part + refs[2][...] if has_acc else part).astype(o_ref.dtype)
            return
        o_ref, scr = refs[-2], refs[-1]
        k = pl.program_id(2)

        @pl.when(k == 0)
        def _():
            scr[...] = part + refs[2][...] if has_acc else part

        @pl.when(k > 0)
        def _():
            scr[...] += part

        @pl.when(k == nk - 1)
        def _():
            o_ref[...] = scr[...].astype(o_ref.dtype)

    a_spec = pl.BlockSpec((tk, tm), lambda i, j, k: (k, i)) if ta else pl.BlockSpec((tm, tk), lambda i, j, k: (i, k))
    b_spec = pl.BlockSpec((tn, tk), lambda i, j, k: (j, k)) if tb else pl.BlockSpec((tk, tn), lambda i, j, k: (k, j))
    o_spec = pl.BlockSpec((tm, tn), lambda i, j, k: (i, j))
    return pl.pallas_call(
        body, name=name,
        grid=(M // tm, N // tn, nk),
        in_specs=[a_spec, b_spec] + ([o_spec] if has_acc else []),
        out_specs=o_spec,
        out_shape=jax.ShapeDtypeStruct((M, N), out_dtype),
        scratch_shapes=[pltpu.VMEM((tm, tn), F32)] if nk > 1 else [],
        compiler_params=_params(("parallel", "parallel", "arbitrary")),
    )(*([a, b] + ([acc] if has_acc else [])))


def _mm_fused(name, lhs, rhs, outs, *, tb=False, add=False, extras=(), epilogue=None):
    M, K = lhs[0].shape
    N = rhs[0].shape[0] if tb else rhs[0].shape[1]
    tm, tn, tk = _div_tile(M, 512), _div_tile(N, 1408), _div_tile(K, 1408)
    nk = K // tk
    n_l, n_e, n_o = len(lhs), len(extras), len(outs)
    n_acc = 1 if add else n_l
    dims = ((1,), (1,) if tb else (0,))

    def body(*refs):
        l_refs, r_refs = refs[:n_l], refs[n_l:2 * n_l]
        e_refs = refs[2 * n_l:2 * n_l + n_e]
        o_refs = refs[2 * n_l + n_e:2 * n_l + n_e + n_o]
        scr = refs[2 * n_l + n_e + n_o:]
        parts = [_dot(l[...].astype(BF16), r[...].astype(BF16), dims) for l, r in zip(l_refs, r_refs)]
        if add:
            parts = [functools.reduce(lambda u, v: u + v, parts)]

        def finish(vals):
            res = epilogue(vals, [e[...].astype(F32) for e in e_refs]) if epilogue else vals
            for ref, val in zip(o_refs, res):
                ref[...] = val.astype(ref.dtype)

        if nk == 1:
            finish(parts)
            return
        k = pl.program_id(2)

        @pl.when(k == 0)
        def _():
            for s, part in zip(scr, parts):
                s[...] = part

        @pl.when(k > 0)
        def _():
            for s, part in zip(scr, parts):
                s[...] += part

        @pl.when(k == nk - 1)
        def _():
            finish([s[...] for s in scr])

    a_spec = pl.BlockSpec((tm, tk), lambda i, j, k: (i, k))
    b_spec = pl.BlockSpec((tn, tk), lambda i, j, k: (j, k)) if tb else pl.BlockSpec((tk, tn), lambda i, j, k: (k, j))
    o_spec = pl.BlockSpec((tm, tn), lambda i, j, k: (i, j))
    return pl.pallas_call(
        body, name=name,
        grid=(M // tm, N // tn, nk),
        in_specs=[a_spec] * n_l + [b_spec] * n_l + [o_spec] * n_e,
        out_specs=[o_spec] * n_o,
        out_shape=[jax.ShapeDtypeStruct((M, N), dt) for dt in outs],
        scratch_shapes=[pltpu.VMEM((tm, tn), F32)] * (n_acc if nk > 1 else 0),
        compiler_params=_params(("parallel", "parallel", "arbitrary")),
    )(*lhs, *rhs, *extras)


def _row_spec(tile, width, colblk):
    return pl.BlockSpec((tile, width), lambda i: (i, colblk))


def _full_spec(shape):
    return pl.BlockSpec(shape, lambda i: (0,) * len(shape))


def _rowwise(name, fn, rows, params, outs, tile=256):
    T = rows[0][0].shape[0]
    tile = min(tile, T)
    n_r, n_p = len(rows), len(params)

    def body(*refs):
        r = [x[...].astype(F32) for x in refs[:n_r]]
        p = [x[...].astype(F32) for x in refs[n_r:n_r + n_p]]
        for o_ref, val in zip(refs[n_r + n_p:], fn(*r, *p)):
            o_ref[...] = val.astype(o_ref.dtype)

    return pl.pallas_call(
        body, name=name,
        grid=(T // tile,),
        in_specs=[_row_spec(tile, w, cb) for _, w, cb in rows] + [_full_spec(p.shape) for p in params],
        out_specs=[_row_spec(tile, w, 0) for w, _ in outs],
        out_shape=[jax.ShapeDtypeStruct((T, w), dt) for w, dt in outs],
        compiler_params=_params(("parallel",)),
    )(*([a for a, _, _ in rows] + list(params)))


def _rowwise_vjp(name, fn, rows, params, cts, need_rows, need_params, add_to=None, tile=256, bf16_rows=()):
    add_to = add_to or {}
    T = rows[0][0].shape[0]
    tile = min(tile, T)
    n_r, n_p = len(rows), len(params)
    ct_flat = [c for group in cts for c in group]
    ct_sizes = [len(group) for group in cts]
    add_idx = sorted(add_to)
    row_out = [i for i in range(n_r) if need_rows[i]]
    par_out = [i for i in range(n_p) if need_params[i]]
    n_ct, n_add = len(ct_flat), len(add_idx)

    def body(*refs):
        pos = 0
        r = [x[...].astype(F32) for x in refs[pos:pos + n_r]]
        pos += n_r
        p = [x[...].astype(F32) for x in refs[pos:pos + n_p]]
        pos += n_p
        ct_vals = [x[...].astype(F32) for x in refs[pos:pos + n_ct]]
        pos += n_ct
        adds = {i: x[...] for i, x in zip(add_idx, refs[pos:pos + n_add])}
        pos += n_add
        drow_refs = refs[pos:pos + len(row_out)]
        pos += len(row_out)
        dpar_refs = refs[pos:pos + len(par_out)]
        ct_in, q = [], 0
        for n in ct_sizes:
            ct_in.append(functools.reduce(lambda u, v: u + v, ct_vals[q:q + n]))
            q += n
        _, vjp = jax.vjp(fn, *r, *p)
        grads = vjp(tuple(ct_in))
        for ref, i in zip(drow_refs, row_out):
            g = grads[i]
            ref[...] = (g + adds[i] if i in adds else g).astype(ref.dtype)

        @pl.when(pl.program_id(0) == 0)
        def _():
            for ref in dpar_refs:
                ref[...] = jnp.zeros_like(ref)

        for ref, i in zip(dpar_refs, par_out):
            ref[...] += grads[n_r + i]

    ct_widths = [c.shape[1] for c in ct_flat]
    in_specs = ([_row_spec(tile, w, cb) for _, w, cb in rows] + [_full_spec(p.shape) for p in params]
                + [_row_spec(tile, w, 0) for w in ct_widths] + [_row_spec(tile, rows[i][1], 0) for i in add_idx])
    out_specs = [_row_spec(tile, rows[i][1], 0) for i in row_out] + [_full_spec(params[i].shape) for i in par_out]
    out_shape = ([jax.ShapeDtypeStruct((T, rows[i][1]), BF16 if i in bf16_rows else F32) for i in row_out]
                 + [jax.ShapeDtypeStruct(params[i].shape, F32) for i in par_out])
    res = pl.pallas_call(
        body, name=name,
        grid=(T // tile,),
        in_specs=in_specs, out_specs=out_specs, out_shape=out_shape,
        compiler_params=_params(("arbitrary",)),
    )(*([a for a, _, _ in rows] + list(params) + ct_flat + [add_to[i] for i in add_idx]))
    return res[:len(row_out)], res[len(row_out):]


def _sigmoid(x):
    return 0.5 * (jnp.tanh(0.5 * x) + 1.0)


def _softplus(x):
    return jnp.maximum(x, 0.0) + jnp.log(1.0 + jnp.exp(-jnp.abs(x)))


def _rms(x, g):
    return x * lax.rsqrt(jnp.mean(x * x, axis=-1, keepdims=True) + RMS_EPS) * g


def _segsum_impl(x):
    n, w = x.shape[-1], 2 * LANES
    r = lax.shift_right_logical(lax.broadcasted_iota(jnp.int32, (w, w), 0), 6)
    c = lax.shift_right_logical(lax.broadcasted_iota(jnp.int32, (w, w), 1), 6)
    bd = (r == c).astype(BF16)
    hi = x.astype(BF16)
    rest = x - hi.astype(F32)
    mid = rest.astype(BF16)
    lo = (rest - mid.astype(F32)).astype(BF16)
    nn = ((1,), (0,))
    blocks = [_dot(hi[:, j:j + w], bd, nn) + _dot(mid[:, j:j + w], bd, nn) + _dot(lo[:, j:j + w], bd, nn)
              for j in range(0, n, w)]
    return jnp.concatenate(blocks, axis=1)


@jax.custom_vjp
def _segsum(x):
    return _segsum_impl(x)


_segsum.defvjp(lambda x: (_segsum_impl(x), None), lambda _, g: (_segsum_impl(g),))


@jax.custom_vjp
def _mmb(a, w):
    return _dot(a.astype(BF16), w.astype(BF16), ((1,), (0,)))


def _mmb_fwd(a, w):
    return _mmb(a, w), (a, w)


def _mmb_bwd(res, g):
    a, w = res
    gb = g.astype(BF16)
    return _dot(gb, w.astype(BF16), ((1,), (1,))), _dot(a.astype(BF16), gb, ((0,), (0,)))


_mmb.defvjp(_mmb_fwd, _mmb_bwd)


def _f_norm(x, g):
    return (_rms(x, g),)


def _f_post1(x, u, g2, g3):
    x1 = x + _rms(u, g2)
    return x1, _rms(x1, g3)


def _f_swiglu(ag, au):
    return (ag * _sigmoid(ag) * au,)


def _f_merge(pg1, pg2, m1, m2, b1, b2):
    return (_sigmoid(pg1 + b1) * m1 + _sigmoid(pg2 + b2) * m2,)


def _f_out(x1, f, g4):
    return (x1 + _rms(f, g4),)


def _f_rwpre(pr, pk, pv, pz, qr, qk, qv, qz, mur, muk, muv, muz, w0, wup, a0, aup, gup, k_k, k_a):
    r = pr + (qr - pr) * mur
    k = pk + (qk - pk) * muk
    v = pv + (qv - pv) * muv
    z = pz + (qz - pz) * muz
    w_raw = w0 + _mmb(jnp.tanh(z), wup)
    lw = -jnp.exp(-_softplus(-w_raw) - 0.5)
    a = _sigmoid(a0 + _mmb(z, aup))
    g = _mmb(_sigmoid(z), gup)
    kk = k * k_k
    kap = kk * lax.rsqrt(jnp.maximum(_segsum(kk * kk), 1e-24))
    k2 = k * (1.0 + (a - 1.0) * k_a)
    return r, lw, k2, v, kap, a, g


def _f_rwpost(y, r, k2, v, g, lnx_w, lnx_b, r_k):
    inv = 1.0 / HEAD_DIM
    yc = y - _segsum(y) * inv
    var = _segsum(yc * yc) * inv
    yn = yc * lax.rsqrt(var + GN_EPS) * lnx_w + lnx_b
    bonus = _segsum(r * k2 * r_k) * v
    return ((yn + bonus) * g,)


RW_GROUPS = (0, 512, 1024, 1536, RW_COLS)


def _column_groups(p):
    return [p[:, a:b] for a, b in zip(RW_GROUPS[:-1], RW_GROUPS[1:])]


def _previous_tokens(p, halo, first_of_sequence):
    rows = lax.broadcasted_iota(jnp.int32, (p.shape[0], 1), 0)
    before = jnp.where(first_of_sequence, 0.0, halo[SUBLANES - 1:SUBLANES, :])
    return jnp.where(rows == 0, before, pltpu.roll(p, 1, axis=0))


def _halo_spec(tile, order):
    per = tile // SUBLANES
    return pl.BlockSpec((SUBLANES, RW_COLS), lambda i: (jnp.maximum(order(i) * per - 1, 0), 0))


def _rw_pre(p_rw, params, S, tile=128):
    T = p_rw.shape[0]
    tile = min(tile, T)
    assert S % tile == 0
    n_p = len(params)

    def body(*refs):
        p_ref, halo_ref = refs[0], refs[1]
        par = [x[...].astype(F32) for x in refs[2:2 + n_p]]
        p = p_ref[...]
        first = lax.rem(pl.program_id(0) * tile, S) == 0
        prev = _previous_tokens(p, halo_ref[...], first)
        for o_ref, val in zip(refs[2 + n_p:], _f_rwpre(*_column_groups(p), *_column_groups(prev), *par)):
            o_ref[...] = val

    out_spec = pl.BlockSpec((tile, RW_WIDTH), lambda i: (i, 0))
    return pl.pallas_call(
        body, name="rw_pre",
        grid=(T // tile,),
        in_specs=[pl.BlockSpec((tile, RW_COLS), lambda i: (i, 0)), _halo_spec(tile, lambda i: i)]
                 + [_full_spec(q.shape) for q in params],
        out_specs=[out_spec] * 7,
        out_shape=[jax.ShapeDtypeStruct((T, RW_WIDTH), F32)] * 7,
        compiler_params=_params(("parallel",)),
    )(p_rw, p_rw, *params)


def _rw_pre_bwd(p_rw, params, cts, S, tile=128):
    T = p_rw.shape[0]
    tile = min(tile, T)
    assert S % tile == 0
    nt = T // tile
    n_p = len(params)
    ct_flat = [c for group in cts for c in group]
    ct_sizes = [len(group) for group in cts]
    n_ct = len(ct_flat)

    def body(*refs):
        p_ref, halo_ref = refs[0], refs[1]
        par = [x[...].astype(F32) for x in refs[2:2 + n_p]]
        ct_vals = [x[...] for x in refs[2 + n_p:2 + n_p + n_ct]]
        dp_ref = refs[2 + n_p + n_ct]
        dpar_refs = refs[3 + n_p + n_ct:3 + 2 * n_p + n_ct]
        carry = refs[-1]
        step = pl.program_id(0)

        @pl.when(step == 0)
        def _():
            carry[...] = jnp.zeros_like(carry)
            for ref in dpar_refs:
                ref[...] = jnp.zeros_like(ref)

        ct_in, q = [], 0
        for n in ct_sizes:
            ct_in.append(functools.reduce(lambda u, v: u + v, ct_vals[q:q + n]))
            q += n
        p = p_ref[...]
        first = lax.rem((nt - 1 - step) * tile, S) == 0
        prev = _previous_tokens(p, halo_ref[...], first)
        _, vjp = jax.vjp(_f_rwpre, *_column_groups(p), *_column_groups(prev), *par)
        grads = vjp(tuple(ct_in))
        d_here = jnp.concatenate(grads[0:4], axis=1)
        d_prev = jnp.concatenate(grads[4:8], axis=1)
        rows = lax.broadcasted_iota(jnp.int32, (tile, 1), 0)
        from_next = jnp.where(rows == tile - 1, carry[0:1, :], pltpu.roll(d_prev, tile - 1, axis=0))
        dp_ref[...] = (d_here + from_next).astype(dp_ref.dtype)
        carry[...] = jnp.broadcast_to(jnp.where(first, 0.0, d_prev[0:1, :]), carry.shape)
        for ref, g in zip(dpar_refs, grads[8:]):
            ref[...] += g

    back = lambda i: nt - 1 - i
    row = lambda w: pl.BlockSpec((tile, w), lambda i: (back(i), 0))
    res = pl.pallas_call(
        body, name="rw_pre_bwd",
        grid=(nt,),
        in_specs=[row(RW_COLS), _halo_spec(tile, back)] + [_full_spec(q.shape) for q in params]
                 + [row(RW_WIDTH)] * n_ct,
        out_specs=[row(RW_COLS)] + [_full_spec(q.shape) for q in params],
        out_shape=[jax.ShapeDtypeStruct((T, RW_COLS), BF16)] + [jax.ShapeDtypeStruct(q.shape, F32) for q in params],
        scratch_shapes=[pltpu.VMEM((SUBLANES, RW_COLS), F32)],
        compiler_params=_params(("arbitrary",)),
    )(p_rw, p_rw, *params, *ct_flat)
    return res[0], res[1:]


def _loss_head(x1, f, target, g4, tile=256):
    T, D = x1.shape
    tile = min(tile, T)

    def body(x1_ref, f_ref, t_ref, g_ref, loss_ref, dx1_ref, df_ref, dg_ref):
        (y,), vjp = jax.vjp(_f_out, x1_ref[...], f_ref[...], g_ref[...])
        err = y - t_ref[...]
        dx1, df, dg = vjp((err * (1.0 / D),))
        dx1_ref[...] = dx1
        df_ref[...] = df.astype(df_ref.dtype)

        @pl.when(pl.program_id(0) == 0)
        def _():
            loss_ref[...] = jnp.zeros_like(loss_ref)
            dg_ref[...] = jnp.zeros_like(dg_ref)

        part = jnp.sum(jnp.sum(err * err, axis=1, keepdims=True), axis=0, keepdims=True) * (0.5 / D)
        loss_ref[...] += jnp.broadcast_to(part, loss_ref.shape)
        dg_ref[...] += dg

    row = pl.BlockSpec((tile, D), lambda i: (i, 0))
    return pl.pallas_call(
        body, name="loss_head",
        grid=(T // tile,),
        in_specs=[row, row, row, _full_spec(g4.shape)],
        out_specs=[_full_spec((SUBLANES, LANES)), row, row, _full_spec(g4.shape)],
        out_shape=[jax.ShapeDtypeStruct((SUBLANES, LANES), F32), jax.ShapeDtypeStruct((T, D), F32),
                   jax.ShapeDtypeStruct((T, D), BF16), jax.ShapeDtypeStruct(g4.shape, F32)],
        compiler_params=_params(("arbitrary",)),
    )(x1, f, target, g4)


def _nn(a, b):
    return _dot(a, b, ((1,), (0,)))


def _nt(a, b):
    return _dot(a, b, ((1,), (1,)))


def _tn(a, b):
    return _dot(a, b, ((0,), (0,)))


def _split_dot(x, u2):
    hi = x.astype(BF16)
    lo = (x - hi.astype(F32)).astype(BF16)
    return _nn(jnp.concatenate([hi, lo], axis=1), u2)


def _by_head(x, masks):
    return jnp.concatenate([(x * m).astype(BF16) for m in masks], axis=0)


def _fold_heads(x2, masks):
    R = x2.shape[0] // len(masks)
    return functools.reduce(lambda u, v: u + v, [x2[h * R:(h + 1) * R] * m for h, m in enumerate(masks)])


def _head_masks():
    lane = lax.broadcasted_iota(jnp.int32, (1, LANES), 1)
    return [((lane >= h * HEAD_DIM) & (lane < (h + 1) * HEAD_DIM)).astype(F32) for h in range(LANES // HEAD_DIM)]


def _key_tri(op):
    row = lax.broadcasted_iota(jnp.int32, (ATTN_KEYS, ATTN_KEYS), 0)
    col = lax.broadcasted_iota(jnp.int32, (ATTN_KEYS, ATTN_KEYS), 1)
    u = op(row, col).astype(BF16)
    return jnp.concatenate([u, u], axis=0)


def _causal(rows):
    row = lax.broadcasted_iota(jnp.int32, (rows, ATTN_KEYS), 0)
    col = lax.broadcasted_iota(jnp.int32, (rows, ATTN_KEYS), 1)
    return col < row


def _from_row(tree, r):
    return jax.tree.map(lambda x: x[r:], tree)


def _onto_rows(old, new, r):
    return jax.tree.map(lambda o, n: jnp.concatenate([o[:r], n], axis=0) if r else n, old, new)


def _sb_weights(qb16, kbh, c_fails, u_gt, strict):
    z_all = _nt(qb16, kbh)
    zs = [z_all[:, h * ATTN_KEYS:(h + 1) * ATTN_KEYS] for h in range(len(c_fails))]
    Ls = [jnp.minimum(-z, 0.0) - jnp.log(1.0 + jnp.exp(-jnp.abs(z))) for z in zs]
    Lms = Ls if strict is None else [jnp.where(strict, L, 0.0) for L in Ls]
    cums = [_split_dot(Lm, u_gt) for Lm in Lms]
    As = [jnp.exp(z + L + c + cum) for z, L, c, cum in zip(zs, Ls, c_fails, cums)]
    if strict is not None:
        As = [jnp.where(strict, A, 0.0) for A in As]
    return zs, Ls, Lms, As


def _attn_specs(S, qb):
    nq = S // qb
    q_spec = pl.BlockSpec((qb, LANES), lambda b, p, i: (b * nq + i, p))
    k_spec = pl.BlockSpec((S, LANES), lambda b, p, i: (b, SB_WIDTH // LANES + p))
    v_spec = pl.BlockSpec((S, LANES), lambda b, p, i: (b, 2 * SB_WIDTH // LANES + p))
    seq = pl.BlockSpec((S, LANES), lambda b, p, i: (b, p))
    return q_spec, k_spec, v_spec, q_spec, seq


def _key_walk(i, qb, block, carry):
    per = qb // ATTN_KEYS
    for sub in reversed(range(per)):
        carry = block(i * per + sub, carry, sub * ATTN_KEYS)
    return lax.fori_loop(0, i * per, lambda j, c: block(i * per - 1 - j, c, None), carry)


def _attn_fwd(proj, B, S):
    qb = min(ATTN_QUERIES, S)
    scale = HEAD_DIM ** -0.5

    def body(q_ref, k_ref, v_ref, o_ref):
        i = pl.program_id(2)
        masks = _head_masks()
        u_gt = _key_tri(lambda r, c: r > c)
        q16 = (q_ref[...] * scale).astype(BF16)

        def block(J, carry, row0):
            r0 = pl.multiple_of(J * ATTN_KEYS, ATTN_KEYS)
            kbh = _by_head(k_ref[pl.ds(r0, ATTN_KEYS), :], masks)
            vbh = _by_head(v_ref[pl.ds(r0, ATTN_KEYS), :], masks)
            lo = row0 or 0
            strict = None if row0 is None else _causal(qb - lo)
            acc, cs = _from_row(carry, lo)
            _, _, Lms, As = _sb_weights(q16[lo:], kbh, cs, u_gt, strict)
            acc = acc + _nn(jnp.concatenate([A.astype(BF16) for A in As], axis=1), vbh)
            cs = tuple(c + jnp.sum(Lm, axis=1, keepdims=True) for c, Lm in zip(cs, Lms))
            return _onto_rows(carry, (acc, cs), lo)

        zero_c = tuple(jnp.zeros((qb, 1), F32) for _ in masks)
        carry = _key_walk(i, qb, block, (jnp.zeros((qb, LANES), F32), zero_c))
        o_ref[...] = carry[0]

    q_spec, k_spec, v_spec, blk, _ = _attn_specs(S, qb)
    return pl.pallas_call(
        body, name="sb_attn_fwd",
        grid=(B, SB_WIDTH // LANES, S // qb),
        in_specs=[q_spec, k_spec, v_spec],
        out_specs=blk,
        out_shape=jax.ShapeDtypeStruct((B * S, SB_WIDTH), F32),
        compiler_params=_params(("parallel", "parallel", "arbitrary")),
    )(proj, proj, proj)


def _attn_bwd(proj, o, do, B, S):
    qb = min(ATTN_QUERIES, S)
    nq = S // qb
    scale = HEAD_DIM ** -0.5

    def body(q_ref, k_ref, v_ref, o_ref, do_ref, dq_ref, dk_out, dv_out, dk_ref, dv_ref):
        i = pl.program_id(2)

        @pl.when(i == 0)
        def _():
            dk_ref[...] = jnp.zeros_like(dk_ref)
            dv_ref[...] = jnp.zeros_like(dv_ref)

        masks = _head_masks()
        u_gt = _key_tri(lambda r, c: r > c)
        u_ge = _key_tri(lambda r, c: r >= c)
        heads = range(len(masks))
        q16 = (q_ref[...] * scale).astype(BF16)
        do16 = do_ref[...].astype(BF16)
        od = o_ref[...] * do16.astype(F32)
        totals = tuple(jnp.sum(od * m, axis=1, keepdims=True) for m in masks)

        def block(J, carry, row0):
            r0 = pl.multiple_of(J * ATTN_KEYS, ATTN_KEYS)
            kbh = _by_head(k_ref[pl.ds(r0, ATTN_KEYS), :], masks)
            vbh = _by_head(v_ref[pl.ds(r0, ATTN_KEYS), :], masks)
            lo = row0 or 0
            strict = None if row0 is None else _causal(qb - lo)
            dq, c_fail, c_p = _from_row(carry, lo)
            tot = _from_row(totals, lo)
            zs, Ls, Lms, As = _sb_weights(q16[lo:], kbh, c_fail, u_gt, strict)
            Abs = [A.astype(BF16) for A in As]
            dA_all = _nt(do16[lo:], vbh)
            Ps = [Abs[h].astype(F32) * dA_all[:, h * ATTN_KEYS:(h + 1) * ATTN_KEYS] for h in heads]
            afters = [c_p[h] + _split_dot(Ps[h], u_ge) for h in heads]
            sigs = [jnp.exp(zs[h] + Ls[h]) for h in heads]
            dzs = [Ps[h] * (1.0 - sigs[h]) - sigs[h] * (tot[h] - afters[h]) for h in heads]
            if strict is not None:
                dzs = [jnp.where(strict, dz, 0.0) for dz in dzs]
            dz_all = jnp.concatenate([dz.astype(BF16) for dz in dzs], axis=1)
            dv_ref[pl.ds(r0, ATTN_KEYS), :] += _fold_heads(_tn(jnp.concatenate(Abs, axis=1), do16[lo:]), masks)
            dk_ref[pl.ds(r0, ATTN_KEYS), :] += _fold_heads(_tn(dz_all, q16[lo:]), masks)
            dq = dq + _nn(dz_all, kbh)
            c_fail = tuple(c_fail[h] + jnp.sum(Lms[h], axis=1, keepdims=True) for h in heads)
            c_p = tuple(c_p[h] + jnp.sum(Ps[h], axis=1, keepdims=True) for h in heads)
            return _onto_rows(carry, (dq, c_fail, c_p), lo)

        zc = tuple(jnp.zeros((qb, 1), F32) for _ in masks)
        carry = _key_walk(i, qb, block, (jnp.zeros((qb, LANES), F32), zc, zc))
        dq_ref[...] = (carry[0] * scale).astype(dq_ref.dtype)

        @pl.when(i == nq - 1)
        def _():
            dk_out[...] = dk_ref[...].astype(dk_out.dtype)
            dv_out[...] = dv_ref[...].astype(dv_out.dtype)

    q_spec, k_spec, v_spec, blk, seq = _attn_specs(S, qb)
    return pl.pallas_call(
        body, name="sb_attn_bwd",
        grid=(B, SB_WIDTH // LANES, nq),
        in_specs=[q_spec, k_spec, v_spec, blk, blk],
        out_specs=[blk, seq, seq],
        out_shape=[jax.ShapeDtypeStruct((B * S, SB_WIDTH), BF16)] * 3,
        scratch_shapes=[pltpu.VMEM((S, LANES), F32), pltpu.VMEM((S, LANES), F32)],
        compiler_params=_params(("parallel", "parallel", "arbitrary")),
    )(proj, proj, proj, o, do)


_BATCHED = {"nn": "gmk,gkn->gmn", "nt": "gmk,gnk->gmn", "tn": "gkm,gkn->gmn"}


def _bdot_raw(a, b, kind, passes):
    e = functools.partial(jnp.einsum, _BATCHED[kind], preferred_element_type=F32)
    ah, bh = a.astype(BF16), b.astype(BF16)
    if passes == 1:
        return e(ah, bh)
    al, bl = (a - ah.astype(F32)).astype(BF16), (b - bh.astype(F32)).astype(BF16)
    return e(ah, bh) + e(ah, bl) + e(al, bh)


@functools.partial(jax.custom_vjp, nondiff_argnums=(2, 3))
def _bdot(a, b, kind, passes):
    return _bdot_raw(a, b, kind, passes)


def _bdot_fwd(a, b, kind, passes):
    return _bdot_raw(a, b, kind, passes), (a, b)


def _bdot_bwd(kind, passes, res, g):
    a, b = res
    if kind == "nn":
        return _bdot_raw(g, b, "nt", passes), _bdot_raw(a, g, "tn", passes)
    if kind == "nt":
        return _bdot_raw(g, b, "nn", passes), _bdot_raw(g, a, "tn", passes)
    return _bdot_raw(b, g, "nt", passes), _bdot_raw(a, g, "nn", passes)


_bdot.defvjp(_bdot_fwd, _bdot_bwd)


def _wkv_chunk(S0, r, lw, k, v, kap, a):
    G, C, N = r.shape
    row = lax.broadcasted_iota(jnp.int32, (C, C), 0)
    col = lax.broadcasted_iota(jnp.int32, (C, C), 1)
    incl = (col <= row).astype(F32)
    strict = (col < row).astype(F32)
    cum = _bdot(jnp.broadcast_to(incl, (G, C, C)), lw, "nn", 3)
    e_pos = jnp.exp(cum)
    e_neg = jnp.exp(-cum)
    al = -kap * jnp.exp(cum - lw)
    be = kap * a * e_neg
    kt = k * e_neg
    rt = r * e_pos
    bk = jnp.concatenate([be, kt], axis=1)
    mask = jnp.concatenate([jnp.concatenate([strict, strict], axis=1), jnp.concatenate([incl, incl], axis=1)], axis=0)
    m_all = _bdot(jnp.concatenate([al, rt], axis=1), bk, "nt", 3) * mask
    m_ab, m_ak = m_all[:, :C, :C], m_all[:, :C, C:]
    m_rb, m_rk = m_all[:, C:, :C], m_all[:, C:, C:]
    S0t = jnp.swapaxes(S0, 1, 2)
    sa = _bdot(jnp.concatenate([al, m_ak], axis=2), jnp.concatenate([S0t, v], axis=1), "nn", 3)
    p = m_ab
    steps = max(1, (C - 1).bit_length())
    for j in range(steps):
        sa = sa + _bdot(p, sa, "nn", 1)
        if j + 1 < steps:
            p = _bdot(p, p, "nn", 1)
    y = _bdot(jnp.concatenate([rt, m_rb, m_rk], axis=2), jnp.concatenate([S0t, sa, v], axis=1), "nn", 3)
    S1 = (S0 + _bdot(jnp.concatenate([sa, v], axis=1), bk, "tn", 1)) * e_pos[:, C - 1:C, :]
    return y, S1


def _split_heads(x):
    return jnp.stack([x[:, h * HEAD_DIM:(h + 1) * HEAD_DIM] for h in range(x.shape[1] // HEAD_DIM)], axis=0)


def _merge_heads(x):
    return jnp.concatenate([x[h] for h in range(x.shape[0])], axis=1)


def _seq_heads(ref):
    return jnp.concatenate([_split_heads(ref[s]) for s in range(ref.shape[0])], axis=0)


def _store_seq_heads(ref, x):
    heads = x.shape[0] // ref.shape[0]
    for s in range(ref.shape[0]):
        ref[s] = _merge_heads(x[s * heads:(s + 1) * heads])


def _wkv_fwd(r, lw, k, v, kap, a, B, S):
    C, H, N = WKV_CHUNK, RW_WIDTH // HEAD_DIM, HEAD_DIM
    nc = S // C
    Q = min(WKV_SEQS, B)

    def body(r_ref, lw_ref, k_ref, v_ref, kap_ref, a_ref, y_ref, st_ref, s_scr):
        @pl.when(pl.program_id(1) == 0)
        def _():
            s_scr[...] = jnp.zeros_like(s_scr)

        S0 = s_scr[...]
        for s in range(Q):
            st_ref[s, 0] = S0[s * H:(s + 1) * H]
        args = [_seq_heads(ref) for ref in (r_ref, lw_ref, k_ref, v_ref, kap_ref, a_ref)]
        y, S1 = _wkv_chunk(S0, *args)
        s_scr[...] = S1
        _store_seq_heads(y_ref, y)

    row_spec = pl.BlockSpec((Q, C, RW_WIDTH), lambda b, c: (b, c, 0))
    seqs = lambda t: t.reshape(B, S, RW_WIDTH)
    y, states = pl.pallas_call(
        body, name="wkv_fwd",
        grid=(B // Q, nc),
        in_specs=[row_spec] * 6,
        out_specs=[row_spec, pl.BlockSpec((Q, 1, H, N, N), lambda b, c: (b, c, 0, 0, 0))],
        out_shape=[jax.ShapeDtypeStruct((B, S, RW_WIDTH), F32), jax.ShapeDtypeStruct((B, nc, H, N, N), F32)],
        scratch_shapes=[pltpu.VMEM((Q * H, N, N), F32)],
        compiler_params=_params(("arbitrary", "arbitrary")),
    )(*map(seqs, (r, lw, k, v, kap, a)))
    return y.reshape(B * S, RW_WIDTH), states


def _wkv_bwd(r, lw, k, v, kap, a, states, dy, B, S):
    C, H, N = WKV_CHUNK, RW_WIDTH // HEAD_DIM, HEAD_DIM
    nc = S // C
    Q = min(WKV_SEQS, B)

    def body(r_ref, lw_ref, k_ref, v_ref, kap_ref, a_ref, st_ref, dy_ref,
             dr_ref, dlw_ref, dk_ref, dv_ref, dkap_ref, da_ref, ds_scr):
        @pl.when(pl.program_id(1) == 0)
        def _():
            ds_scr[...] = jnp.zeros_like(ds_scr)

        args = [_seq_heads(ref) for ref in (r_ref, lw_ref, k_ref, v_ref, kap_ref, a_ref)]
        S0 = jnp.concatenate([st_ref[s, 0] for s in range(Q)], axis=0)
        _, vjp = jax.vjp(_wkv_chunk, S0, *args)
        g = vjp((_seq_heads(dy_ref), ds_scr[...]))
        ds_scr[...] = g[0]
        for ref, gv in zip((dr_ref, dlw_ref, dk_ref, dv_ref, dkap_ref, da_ref), g[1:]):
            _store_seq_heads(ref, gv)

    row_spec = pl.BlockSpec((Q, C, RW_WIDTH), lambda b, c: (b, nc - 1 - c, 0))
    st_spec = pl.BlockSpec((Q, 1, H, N, N), lambda b, c: (b, nc - 1 - c, 0, 0, 0))
    seqs = lambda t: t.reshape(B, S, RW_WIDTH)
    res = pl.pallas_call(
        body, name="wkv_bwd",
        grid=(B // Q, nc),
        in_specs=[row_spec] * 6 + [st_spec, row_spec],
        out_specs=[row_spec] * 6,
        out_shape=[jax.ShapeDtypeStruct((B, S, RW_WIDTH), F32)] * 6,
        scratch_shapes=[pltpu.VMEM((Q * H, N, N), F32)],
        compiler_params=_params(("arbitrary", "arbitrary"), vmem=WKV_BWD_VMEM),
    )(*map(seqs, (r, lw, k, v, kap, a)), states, seqs(dy))
    return [t.reshape(B * S, RW_WIDTH) for t in res]


HBM = pl.BlockSpec(memory_space=pl.ANY)


def _place():
    return lax.axis_index("x"), lax.axis_index("y"), lax.axis_index("c")


def _other_chips(x, y):
    return [(1 - x, y), (x, 1 - y), (1 - x, 1 - y)]


def _all_gather_chips(shards):
    n = len(shards)

    def body(*refs):
        ins, outs = refs[:n], refs[n:2 * n]
        ici_send, ici_recv, d2d_send, d2d_recv, local = refs[2 * n:]
        x, y, c = _place()
        me = 2 * x + y
        sib = (x, y, 1 - c)
        chips = _other_chips(x, y)
        started, copies = [], []
        for w in range(n):
            cp = pltpu.make_async_copy(ins[w].at[c], outs[w].at[me, c], local.at[w])
            cp.start()
            copies.append(cp)
            for j, (px, py) in enumerate(chips):
                rd = pltpu.make_async_remote_copy(
                    src_ref=ins[w].at[c], dst_ref=outs[w].at[me, c], send_sem=ici_send.at[3 * w + j],
                    recv_sem=ici_recv.at[3 * w + j], device_id=(px, py, c), device_id_type=MESH)
                rd.start()
                started.append(rd)
            rd = pltpu.make_async_remote_copy(
                src_ref=ins[w].at[c], dst_ref=outs[w].at[me, c], send_sem=d2d_send.at[4 * w + 3],
                recv_sem=d2d_recv.at[4 * w + 3], device_id=sib, device_id_type=MESH)
            rd.start()
            started.append(rd)
        for w in range(n):
            for j, (px, py) in enumerate(chips):
                src = 2 * px + py
                pltpu.make_async_remote_copy(
                    src_ref=ins[w].at[c], dst_ref=outs[w].at[src, c], send_sem=ici_send.at[3 * w + j],
                    recv_sem=ici_recv.at[3 * w + j], device_id=(px, py, c), device_id_type=MESH).wait_recv()
                rd = pltpu.make_async_remote_copy(
                    src_ref=outs[w].at[src, c], dst_ref=outs[w].at[src, c], send_sem=d2d_send.at[4 * w + j],
                    recv_sem=d2d_recv.at[4 * w + j], device_id=sib, device_id_type=MESH)
                rd.start()
                started.append(rd)
        for w in range(n):
            for j, (px, py) in enumerate(chips):
                pltpu.make_async_remote_copy(
                    src_ref=ins[w].at[c], dst_ref=outs[w].at[2 * px + py, 1 - c], send_sem=d2d_send.at[4 * w + j],
                    recv_sem=d2d_recv.at[4 * w + j], device_id=sib, device_id_type=MESH).wait_recv()
            pltpu.make_async_remote_copy(
                src_ref=ins[w].at[c], dst_ref=outs[w].at[me, 1 - c], send_sem=d2d_send.at[4 * w + 3],
                recv_sem=d2d_recv.at[4 * w + 3], device_id=sib, device_id_type=MESH).wait_recv()
        for rd in started:
            rd.wait_send()
        for cp in copies:
            cp.wait()

    return pl.pallas_call(
        body, name="gather_weights",
        in_specs=[HBM] * n, out_specs=[HBM] * n,
        out_shape=[jax.ShapeDtypeStruct((N_CHIPS,) + s.shape, s.dtype) for s in shards],
        scratch_shapes=[pltpu.SemaphoreType.DMA((3 * n,)), pltpu.SemaphoreType.DMA((3 * n,)),
                        pltpu.SemaphoreType.DMA((4 * n,)), pltpu.SemaphoreType.DMA((4 * n,)),
                        pltpu.SemaphoreType.DMA((n,))],
        compiler_params=pltpu.CompilerParams(has_side_effects=True),
    )(*shards)


def _pair_split(grads):
    n = len(grads)

    def body(*refs):
        ins, theirs = refs[:n], refs[n:2 * n]
        send, recv = refs[2 * n:]
        x, y, c = _place()
        sib = (x, y, 1 - c)
        rds = []
        for w in range(n):
            rd = pltpu.make_async_remote_copy(
                src_ref=ins[w].at[:, 1 - c], dst_ref=theirs[w], send_sem=send.at[w], recv_sem=recv.at[w],
                device_id=sib, device_id_type=MESH)
            rd.start()
            rds.append(rd)
        for rd in rds:
            rd.wait_recv()
        for rd in rds:
            rd.wait_send()

    return pl.pallas_call(
        body, name="grad_pair_split",
        in_specs=[HBM] * n, out_specs=[HBM] * n,
        out_shape=[jax.ShapeDtypeStruct((g.shape[0],) + g.shape[2:], g.dtype) for g in grads],
        scratch_shapes=[pltpu.SemaphoreType.DMA((n,)), pltpu.SemaphoreType.DMA((n,))],
        compiler_params=pltpu.CompilerParams(has_side_effects=True),
    )(*grads)


def _chip_scatter(parts):
    n = len(parts)

    def body(*refs):
        ins, outs = refs[:n], refs[n:2 * n]
        send, recv = refs[2 * n:]
        x, y, c = _place()
        me = 2 * x + y
        rds = []
        for w in range(n):
            for j, (px, py) in enumerate(_other_chips(x, y)):
                s = 3 * w + j
                rd = pltpu.make_async_remote_copy(
                    src_ref=ins[w].at[2 * px + py], dst_ref=outs[w].at[j], send_sem=send.at[s], recv_sem=recv.at[s],
                    device_id=(px, py, c), device_id_type=MESH)
                rd.start()
                rds.append(rd)
        for w in range(n):
            for j, (px, py) in enumerate(_other_chips(x, y)):
                s = 3 * w + j
                pltpu.make_async_remote_copy(
                    src_ref=ins[w].at[me], dst_ref=outs[w].at[j], send_sem=send.at[s], recv_sem=recv.at[s],
                    device_id=(px, py, c), device_id_type=MESH).wait_recv()
        for rd in rds:
            rd.wait_send()

    return pl.pallas_call(
        body, name="grad_chip_scatter",
        in_specs=[HBM] * n, out_specs=[HBM] * n,
        out_shape=[jax.ShapeDtypeStruct((N_CHIPS - 1,) + p.shape[1:], p.dtype) for p in parts],
        scratch_shapes=[pltpu.SemaphoreType.DMA((3 * n,)), pltpu.SemaphoreType.DMA((3 * n,))],
        compiler_params=pltpu.CompilerParams(has_side_effects=True),
    )(*parts)


def _pair_join(bufs):
    n = len(bufs)

    def body(*refs):
        ins, outs = refs[:n], refs[n:2 * n]
        send, recv = refs[2 * n:]
        x, y, c = _place()
        sib = (x, y, 1 - c)
        rds = []
        for w in range(n):
            rd = pltpu.make_async_remote_copy(
                src_ref=ins[w].at[c], dst_ref=outs[w].at[c], send_sem=send.at[w], recv_sem=recv.at[w],
                device_id=sib, device_id_type=MESH)
            rd.start()
            rds.append(rd)
        for w in range(n):
            pltpu.make_async_remote_copy(
                src_ref=ins[w].at[c], dst_ref=outs[w].at[1 - c], send_sem=send.at[w], recv_sem=recv.at[w],
                device_id=sib, device_id_type=MESH).wait_recv()
        for rd in rds:
            rd.wait_send()

    return pl.pallas_call(
        body, name="grad_pair_join",
        in_specs=[HBM] * n, out_specs=[HBM] * n,
        out_shape=[jax.ShapeDtypeStruct(b.shape, b.dtype) for b in bufs],
        input_output_aliases={w: w for w in range(n)},
        scratch_shapes=[pltpu.SemaphoreType.DMA((n,)), pltpu.SemaphoreType.DMA((n,))],
        compiler_params=pltpu.CompilerParams(has_side_effects=True),
    )(*bufs)


def _all_reduce_small(packed):
    R = packed.shape[0]

    def body(x_ref, o_ref, buf, send, recv):
        x, y, c = _place()
        me = 4 * x + 2 * y + c
        buf[me] = x_ref[...]
        rds = []
        for rel in range(1, N_DEV):
            fx, fy, fc = (rel >> 2) & 1, (rel >> 1) & 1, rel & 1
            peer = (1 - x if fx else x, 1 - y if fy else y, 1 - c if fc else c)
            rd = pltpu.make_async_remote_copy(
                src_ref=x_ref, dst_ref=buf.at[me], send_sem=send.at[rel - 1], recv_sem=recv.at[rel - 1],
                device_id=peer, device_id_type=MESH)
            rd.start()
            rds.append((rd, peer))
        for rel in range(1, N_DEV):
            rd, (px, py, pc) = rds[rel - 1]
            pltpu.make_async_remote_copy(
                src_ref=x_ref, dst_ref=buf.at[4 * px + 2 * py + pc], send_sem=send.at[rel - 1], recv_sem=recv.at[rel - 1],
                device_id=(px, py, pc), device_id_type=MESH).wait_recv()
        for rd, _ in rds:
            rd.wait_send()
        total = buf[0]
        for d in range(1, N_DEV):
            total = total + buf[d]
        o_ref[...] = total

    return pl.pallas_call(
        body, name="all_reduce_small",
        in_specs=[pl.BlockSpec(memory_space=pltpu.VMEM)],
        out_specs=pl.BlockSpec(memory_space=pltpu.VMEM),
        out_shape=jax.ShapeDtypeStruct(packed.shape, F32),
        scratch_shapes=[pltpu.VMEM((N_DEV, R, LANES), F32), pltpu.SemaphoreType.DMA((N_DEV - 1,)),
                        pltpu.SemaphoreType.DMA((N_DEV - 1,))],
        compiler_params=pltpu.CompilerParams(has_side_effects=True),
    )(packed)


def _pair_sum(name, split, theirs, core):
    n_chip, _, Rh, C = split.shape
    tile = _div_tile(Rh, 256, 2 * SUBLANES)
    nt = Rh // tile

    def body(core_ref, a_ref, b_ref, o_ref):
        o_ref[...] = (a_ref[...] + b_ref[...]).astype(o_ref.dtype)

    return pl.pallas_call(
        body, name=name,
        grid_spec=pltpu.PrefetchScalarGridSpec(
            num_scalar_prefetch=1,
            grid=(n_chip, nt),
            in_specs=[pl.BlockSpec((None, None, tile, C), lambda j, i, core_ref: (j, core_ref[0], i, 0)),
                      pl.BlockSpec((None, tile, C), lambda j, i, core_ref: (j, i, 0))],
            out_specs=pl.BlockSpec((None, tile, C), lambda j, i, core_ref: (j, i, 0)),
        ),
        out_shape=jax.ShapeDtypeStruct((n_chip, Rh, C), BF16),
        compiler_params=_params(("parallel", "parallel")),
    )(core, split, theirs)


def _chip_sum(name, own, landed, core):
    n_in, Rh, C = landed.shape
    tile = _div_tile(Rh, 256, 2 * SUBLANES)

    def body(core_ref, *refs):
        total = refs[0][...].astype(F32)
        for ref in refs[1:n_in + 1]:
            total = total + ref[...].astype(F32)
        refs[n_in + 1][...] = total

    slot = lambda j: pl.BlockSpec((None, tile, C), lambda i, core_ref: (j, i, 0))
    return pl.pallas_call(
        body, name=name,
        grid_spec=pltpu.PrefetchScalarGridSpec(
            num_scalar_prefetch=1,
            grid=(Rh // tile,),
            in_specs=[pl.BlockSpec((None, tile, C), lambda i, core_ref: (core_ref[1], i, 0))]
                     + [slot(j) for j in range(n_in)],
            out_specs=pl.BlockSpec((None, tile, C), lambda i, core_ref: (core_ref[0], i, 0)),
        ),
        out_shape=jax.ShapeDtypeStruct((2, Rh, C), F32),
        compiler_params=_params(("parallel",)),
    )(core, own, *([landed] * n_in))


def _adamw(name, w, g, m, v):
    R, C = w.shape
    tile = _div_tile(R, 256, SUBLANES)
    c1 = 1.0 / (1.0 - ADAM_B1 ** ADAM_STEP)
    c2 = 1.0 / (1.0 - ADAM_B2 ** ADAM_STEP)

    def body(w_ref, g_ref, m_ref, v_ref, d_ref, nm_ref, nv_ref):
        g_ = g_ref[...]
        nm = ADAM_B1 * m_ref[...] + (1.0 - ADAM_B1) * g_
        nv = ADAM_B2 * v_ref[...] + (1.0 - ADAM_B2) * (g_ * g_)
        d_ref[...] = -ADAM_LR * ((nm * c1) / (jnp.sqrt(nv * c2) + ADAM_EPS) + ADAM_WD * w_ref[...])
        nm_ref[...] = nm
        nv_ref[...] = nv

    spec = pl.BlockSpec((tile, C), lambda i: (i, 0))
    return pl.pallas_call(
        body, name=name,
        grid=(R // tile,),
        in_specs=[spec] * 4, out_specs=[spec] * 3,
        out_shape=[jax.ShapeDtypeStruct((R, C), F32)] * 3,
        compiler_params=_params(("parallel",)),
    )(w, g, m, v)


SMALL =["norm_mix_pre", "b_gate", "mu_rw", "w0", "a0", "k_k", "k_a", "r_k", "lnx_w", "lnx_b",
         "norm_mix_post", "norm_ffn_pre", "norm_ffn_post"]
BIG = ["w_in", "w_up", "a_up", "g_up", "w_sb_out", "w_rw_out", "w_o", "w_ffn_gate", "w_ffn_up", "w_ffn_down"]
ROW_SHARDED = ("w_o", "w_ffn_down")
ORDER = ["norm_mix_pre", "w_in", "b_gate", "mu_rw", "w0", "w_up", "a0", "a_up", "g_up", "k_k", "k_a", "r_k",
         "lnx_w", "lnx_b", "w_sb_out", "w_rw_out", "w_o", "norm_mix_post", "norm_ffn_pre", "w_ffn_gate",
         "w_ffn_up", "w_ffn_down", "norm_ffn_post"]


def _pack_small(vals, extra_rows=0):
    rows = jnp.concatenate([vals[n].reshape(-1, LANES) for n in SMALL], axis=0)
    pad = (-(rows.shape[0] + extra_rows)) % SUBLANES + extra_rows
    return jnp.pad(rows, ((0, pad), (0, 0)))


def _unpack_small(packed, shapes):
    out, r = {}, 0
    for n in SMALL:
        size = 1
        for s in shapes[n]:
            size *= s
        out[n] = packed[r:r + size // LANES].reshape(shapes[n])
        r += size // LANES
    return out


def kernel(x, norm_mix_pre, w_in, b_gate, mu_rw, w0, w_up, a0, a_up, g_up, k_k, k_a, r_k, lnx_w, lnx_b, w_sb_out, w_rw_out, w_o, norm_mix_post, norm_ffn_pre, w_ffn_gate, w_ffn_up, w_ffn_down, norm_ffn_post, loss_target, m_norm_mix_pre, m_w_in, m_b_gate, m_mu_rw, m_w0, m_w_up, m_a0, m_a_up, m_g_up, m_k_k, m_k_a, m_r_k, m_lnx_w, m_lnx_b, m_w_sb_out, m_w_rw_out, m_w_o, m_norm_mix_post, m_norm_ffn_pre, m_w_ffn_gate, m_w_ffn_up, m_w_ffn_down, m_norm_ffn_post, v_norm_mix_pre, v_w_in, v_b_gate, v_mu_rw, v_w0, v_w_up, v_a0, v_a_up, v_g_up, v_k_k, v_k_a, v_r_k, v_lnx_w, v_lnx_b, v_w_sb_out, v_w_rw_out, v_w_o, v_norm_mix_post, v_norm_ffn_pre, v_w_ffn_gate, v_w_ffn_up, v_w_ffn_down, v_norm_ffn_post):
    W = dict(norm_mix_pre=norm_mix_pre, w_in=w_in, b_gate=b_gate, mu_rw=mu_rw, w0=w0, w_up=w_up, a0=a0, a_up=a_up,
             g_up=g_up, k_k=k_k, k_a=k_a, r_k=r_k, lnx_w=lnx_w, lnx_b=lnx_b, w_sb_out=w_sb_out, w_rw_out=w_rw_out,
             w_o=w_o, norm_mix_post=norm_mix_post, norm_ffn_pre=norm_ffn_pre, w_ffn_gate=w_ffn_gate,
             w_ffn_up=w_ffn_up, w_ffn_down=w_ffn_down, norm_ffn_post=norm_ffn_post)
    Mo = dict(norm_mix_pre=m_norm_mix_pre, w_in=m_w_in, b_gate=m_b_gate, mu_rw=m_mu_rw, w0=m_w0, w_up=m_w_up, a0=m_a0,
              a_up=m_a_up, g_up=m_g_up, k_k=m_k_k, k_a=m_k_a, r_k=m_r_k, lnx_w=m_lnx_w, lnx_b=m_lnx_b,
              w_sb_out=m_w_sb_out, w_rw_out=m_w_rw_out, w_o=m_w_o, norm_mix_post=m_norm_mix_post,
              norm_ffn_pre=m_norm_ffn_pre, w_ffn_gate=m_w_ffn_gate, w_ffn_up=m_w_ffn_up, w_ffn_down=m_w_ffn_down,
              norm_ffn_post=m_norm_ffn_post)
    Vo = dict(norm_mix_pre=v_norm_mix_pre, w_in=v_w_in, b_gate=v_b_gate, mu_rw=v_mu_rw, w0=v_w0, w_up=v_w_up, a0=v_a0,
              a_up=v_a_up, g_up=v_g_up, k_k=v_k_k, k_a=v_k_a, r_k=v_r_k, lnx_w=v_lnx_w, lnx_b=v_lnx_b,
              w_sb_out=v_w_sb_out, w_rw_out=v_w_rw_out, w_o=v_w_o, norm_mix_post=v_norm_mix_post,
              norm_ffn_pre=v_norm_ffn_pre, w_ffn_gate=v_w_ffn_gate, w_ffn_up=v_w_ffn_up, w_ffn_down=v_w_ffn_down,
              norm_ffn_post=v_norm_ffn_post)
    shapes = {n: W[n].shape for n in ORDER}
    B, S, D = x.shape
    T = B * S
    x2 = x.reshape(T, D)
    tgt = loss_target.reshape(T, D)
    vec = {n: W[n].reshape(1, -1) for n in SMALL}

    work = lambda t, n: t[0] if n in ROW_SHARDED else jnp.swapaxes(t[0], 0, 1)
    halved = [work(W[n], n).astype(BF16) for n in BIG]
    halved = [h.reshape(2, h.shape[0] // 2, h.shape[1]) for h in halved]
    full = {n: gth.reshape(-1, gth.shape[3]) for n, gth in zip(BIG, _all_gather_chips(halved))}
    w_in_t = full["w_in"]
    w_sb_t, w_rw_t, w_gt_t = w_in_t[:SB_COLS], w_in_t[SB_COLS:SB_COLS + RW_COLS], w_in_t[SB_COLS + RW_COLS:]
    lora_rows = {"w_up": 0, "a_up": 64, "g_up": 128}
    lora = {n: jnp.pad(full[n].T, ((r0, LORA_COLS - r0 - full[n].shape[1]), (0, 0))) for n, r0 in lora_rows.items()}
    mu = vec["mu_rw"]
    mu_parts = [mu[:, :512], mu[:, 512:1024], mu[:, 1024:1536], mu[:, 1536:]]
    b1, b2 = vec["b_gate"][:, :D], vec["b_gate"][:, D:]

    (h1,) = _rowwise("norm_mix_pre", _f_norm, [(x2, D, 0)], [vec["norm_mix_pre"]], [(D, BF16)])
    p_sb = _mm("proj_sb", h1, w_sb_t, tb=True, out_dtype=BF16)
    p_rw = _mm("proj_rw", h1, w_rw_t, tb=True)
    p_gt = _mm("proj_gate", h1, w_gt_t, tb=True, out_dtype=BF16)
    o_sb = _attn_fwd(p_sb, B, S)
    pre_params = mu_parts + [vec["w0"], lora["w_up"], vec["a0"], lora["a_up"], lora["g_up"], vec["k_k"], vec["k_a"]]
    r_, lw_, k2_, v_, kap_, a_, g_ = _rw_pre(p_rw, pre_params, S)
    y_wkv, states = _wkv_fwd(r_, lw_, k2_, v_, kap_, a_, B, S)
    post_rows = [(y_wkv, 512, 0), (r_, 512, 0), (k2_, 512, 0), (v_, 512, 0), (g_, 512, 0)]
    post_params = [vec["lnx_w"], vec["lnx_b"], vec["r_k"]]
    (o_rw,) = _rowwise("rw_post", _f_rwpost, post_rows, post_params, [(512, BF16)])
    m1 = _mm("mix_sb_out", o_sb, full["w_sb_out"], tb=True, out_dtype=BF16)
    m2 = _mm("mix_rw_out", o_rw, full["w_rw_out"], tb=True, out_dtype=BF16)
    merge_rows = [(p_gt, D, 0), (p_gt, D, 1), (m1, D, 0), (m2, D, 0)]
    (merged,) = _rowwise("merge", _f_merge, merge_rows, [b1, b2], [(D, BF16)])
    u = _mm("mix_out", merged, full["w_o"])
    post1_params = [vec["norm_mix_post"], vec["norm_ffn_pre"]]
    x1, h2 = _rowwise("post_mix", _f_post1, [(x2, D, 0), (u, D, 0)], post1_params, [(D, F32), (D, BF16)])
    ag, au, sw = _mm_fused("ffn_in", [h2, h2], [full["w_ffn_gate"], full["w_ffn_up"]], [BF16] * 3, tb=True,
                           epilogue=lambda gu, _: (gu[0], gu[1], _f_swiglu(*gu)[0]))
    f = _mm("ffn_down", sw, full["w_ffn_down"])
    loss_part, dx1, df, dg4 = _loss_head(x1, f, tgt, vec["norm_ffn_post"])

    gbig, gsmall = {}, {"norm_ffn_post": dg4}
    gbig["w_ffn_down"] = _mm("g_ffn_down", sw, df, ta=True)

    def swiglu_back(dsw, gu):
        return jax.vjp(_f_swiglu, *gu)[1]((dsw[0],))

    dag, dau = _mm_fused("ffn_back", [df], [full["w_ffn_down"]], [BF16] * 2, tb=True, extras=[ag, au],
                         epilogue=swiglu_back)
    (dh2,) = _mm_fused("d_h2", [dag, dau], [full["w_ffn_gate"], full["w_ffn_up"]], [F32], add=True)
    gbig["w_ffn_gate"] = _mm("g_ffn_gate", dag, h2, ta=True)
    gbig["w_ffn_up"] = _mm("g_ffn_up", dau, h2, ta=True)
    (dx_res, du), (dg2, dg3) = _rowwise_vjp("post_mix_bwd", _f_post1, [(x2, D, 0), (u, D, 0)], post1_params,
                                            [[dx1], [dh2]], [True, True], [True, True], bf16_rows=(1,))
    gsmall["norm_mix_post"], gsmall["norm_ffn_pre"] = dg2, dg3
    dmerged = _mm("d_merged", du, full["w_o"], tb=True, out_dtype=BF16)
    gbig["w_o"] = _mm("g_w_o", merged, du, ta=True)
    (dpg1, dpg2, dm1, dm2), (db1, db2) = _rowwise_vjp("merge_bwd", _f_merge, merge_rows, [b1, b2], [[dmerged]],
                                                      [True] * 4, [True, True], bf16_rows=(0, 1, 2, 3))
    gsmall["b_gate"] = jnp.concatenate([db1, db2], axis=1)
    do_sb = _mm("d_o_sb", dm1, full["w_sb_out"])
    do_rw = _mm("d_o_rw", dm2, full["w_rw_out"])
    gbig["w_sb_out"] = _mm("g_sb_out", dm1, o_sb, ta=True)
    gbig["w_rw_out"] = _mm("g_rw_out", dm2, o_rw, ta=True)
    (dy_wkv, dr_a, dk2_a, dv_a, dg_), (dlnx_w, dlnx_b, dr_k) = _rowwise_vjp(
        "rw_post_bwd", _f_rwpost, post_rows, post_params, [[do_rw]], [True] * 5, [True] * 3)
    gsmall["lnx_w"], gsmall["lnx_b"], gsmall["r_k"] = dlnx_w, dlnx_b, dr_k
    dr_b, dlw, dk2_b, dv_b, dkap, da = _wkv_bwd(r_, lw_, k2_, v_, kap_, a_, states, dy_wkv, B, S)
    pre_cts = [[dr_a, dr_b], [dlw], [dk2_a, dk2_b], [dv_a, dv_b], [dkap], [da], [dg_]]
    dp_rw, dpre_params = _rw_pre_bwd(p_rw, pre_params, pre_cts, S)
    gsmall["mu_rw"] = jnp.concatenate(dpre_params[:4], axis=1)
    gsmall["w0"], gsmall["a0"], gsmall["k_k"], gsmall["k_a"] = dpre_params[4], dpre_params[6], dpre_params[9], dpre_params[10]
    glora = {"w_up": dpre_params[5][0:64].T, "a_up": dpre_params[7][64:128].T, "g_up": dpre_params[8][128:256].T}
    dq, dk, dv = _attn_bwd(p_sb, o_sb, do_sb, B, S)
    (dh1,) = _mm_fused("d_h1_sb", [dq, dk, dv], [w_sb_t[:512], w_sb_t[512:1024], w_sb_t[1024:]], [F32], add=True)
    dh1 = _mm("d_h1_rw", dp_rw, w_rw_t, acc=dh1)
    (dh1,) = _mm_fused("d_h1_gate", [dpg1, dpg2], [w_gt_t[:D], w_gt_t[D:]], [F32], add=True,
                       extras=[dh1], epilogue=lambda p, e: (p[0] + e[0],))
    gbig["w_in"] = jnp.concatenate(
        [_mm("g_in_" + tag, d, h1, ta=True)
         for tag, d in (("q", dq), ("k", dk), ("v", dv), ("rw", dp_rw), ("g1", dpg1), ("g2", dpg2))], axis=0)
    (grad_x2,), (dg1,) = _rowwise_vjp("norm_mix_pre_bwd", _f_norm, [(x2, D, 0)], [vec["norm_mix_pre"]], [[dh1]],
                                      [True], [True], add_to={0: dx_res})
    gsmall["norm_mix_pre"] = dg1
    gbig.update(glora)

    split = [gbig[n].reshape(N_CHIPS, 2, gbig[n].shape[0] // (2 * N_CHIPS), gbig[n].shape[1]) for n in BIG]
    core = jnp.stack([lax.axis_index("c"), 2 * lax.axis_index("x") + lax.axis_index("y")]).astype(jnp.int32)
    theirs = _pair_split(split)
    chip_sums = [_pair_sum("pair_sum_" + n, a, b, core) for n, a, b in zip(BIG, split, theirs)]
    landed = _chip_scatter(chip_sums)
    joined = _pair_join([_chip_sum("chip_sum_" + n, own, got, core) for n, own, got in zip(BIG, chip_sums, landed)])
    grads = {n: j.reshape(-1, j.shape[2]) for n, j in zip(BIG, joined)}

    small_local = _pack_small({n: gsmall[n] for n in SMALL}, extra_rows=1)
    loss_row = small_local.shape[0] - 1
    small_local = small_local.at[loss_row].set(loss_part[0])
    small_sum = _all_reduce_small(small_local)
    loss = small_sum[loss_row, 0]

    delta, new_m, new_v = {}, {}, {}
    unwork = lambda t, n: (t if n in ROW_SHARDED else jnp.swapaxes(t, 0, 1))[None]
    for n in BIG:
        d_, m_, v2_ = _adamw("adamw_" + n, work(W[n], n), grads[n], work(Mo[n], n), work(Vo[n], n))
        delta[n], new_m[n], new_v[n], grads[n] = (unwork(t, n) for t in (d_, m_, v2_, grads[n]))
    pk = lambda src: _pack_small({n: src[n] for n in SMALL}, extra_rows=1)
    d_s, m_s, v_s = _adamw("adamw_small", pk(W), small_sum.at[loss_row].set(0.0), pk(Mo), pk(Vo))
    for dst, packed in ((grads, small_sum), (delta, d_s), (new_m, m_s), (new_v, v_s)):
        dst.update(_unpack_small(packed, shapes))

    return (loss, grad_x2.reshape(B, S, D), *[grads[n] for n in ORDER], *[delta[n] for n in ORDER],
            *[new_m[n] for n in ORDER], *[new_v[n] for n in ORDER])
```

```python
import functools

import jax
import jax.numpy as jnp
from jax import lax
from jax.experimental import pallas as pl
from jax.experimental.pallas import tpu as pltpu

F32 = jnp.float32
BF16 = jnp.bfloat16
MESH = pl.DeviceIdType.MESH

D_MODEL = 1024
SB_HEADS = 8
HEAD_DIM = 64
SB_WIDTH = SB_HEADS * HEAD_DIM
RW_WIDTH = 512
LORA_COLS = 256
SB_COLS = 3 * SB_WIDTH
RW_COLS = 3 * RW_WIDTH + LORA_COLS
GATE_COLS = 2 * D_MODEL
D_FF = 2816
RMS_EPS = 1e-6
GN_EPS = HEAD_DIM * 1e-5
WKV_CHUNK = 64
WKV_SEQS = 4
ATTN_QUERIES = 512
ATTN_KEYS = 128
LANES = 128
SUBLANES = 8
N_CHIPS = 4
N_DEV = 8

ADAM_LR = 0.001
ADAM_B1 = 0.9
ADAM_B2 = 0.999
ADAM_EPS = 1e-08
ADAM_WD = 0.01
ADAM_STEP = 10

VMEM_LIMIT = 48 * 1024 * 1024
WKV_BWD_VMEM = 58 * 1024 * 1024


def _params(sem=None, vmem=VMEM_LIMIT, **kw):
    if sem is not None:
        kw["dimension_semantics"] = sem
    return pltpu.CompilerParams(vmem_limit_bytes=vmem, **kw)


def _div_tile(dim, pref, mult=LANES):
    if dim <= pref:
        return dim
    t = pref - pref % mult
    while t >= mult:
        if dim % t == 0:
            return t
        t -= mult
    return dim


def _dot(a, b, dims):
    return lax.dot_general(a, b, (dims, ((), ())), preferred_element_type=F32)


def _mm(name, a, b, *, ta=False, tb=False, acc=None, out_dtype=F32):
    if ta:
        K, M = a.shape
    else:
        M, K = a.shape
    N = b.shape[0] if tb else b.shape[1]
    if ta:
        tm, tn, tk = _div_tile(M, 1408), _div_tile(N, 1408), _div_tile(K, 512)
    else:
        tm, tn, tk = _div_tile(M, 512), _div_tile(N, 1408), _div_tile(K, 1408)
    nk = K // tk
    dims = ((0,) if ta else (1,), (1,) if tb else (0,))
    has_acc = acc is not None

    def body(*refs):
        a_ref, b_ref = refs[0], refs[1]
        part = _dot(a_ref[...].astype(BF16), b_ref[...].astype(BF16), dims)
        if nk == 1:
            o_ref = refs[-1]
            o_ref[...] = (part + refs[2][...] if has_acc else part).astype(o_ref.dtype)
            return
        o_ref, scr = refs[-2], refs[-1]
        k = pl.program_id(2)

        @pl.when(k == 0)
        def _():
            scr[...] = part + refs[2][...] if has_acc else part

        @pl.when(k > 0)
        def _():
            scr[...] += part

        @pl.when(k == nk - 1)
        def _():
            o_ref[...] = scr[...].astype(o_ref.dtype)

    a_spec = pl.BlockSpec((tk, tm), lambda i, j, k: (k, i)) if ta else pl.BlockSpec((tm, tk), lambda i, j, k: (i, k))
    b_spec = pl.BlockSpec((tn, tk), lambda i, j, k: (j, k)) if tb else pl.BlockSpec((tk, tn), lambda i, j, k: (k, j))
    o_spec = pl.BlockSpec((tm, tn), lambda i, j, k: (i, j))
    return pl.pallas_call(
        body, name=name,
        grid=(M // tm, N // tn, nk),
        in_specs=[a_spec, b_spec] + ([o_spec] if has_acc else []),
        out_specs=o_spec,
        out_shape=jax.ShapeDtypeStruct((M, N), out_dtype),
        scratch_shapes=[pltpu.VMEM((tm, tn), F32)] if nk > 1 else [],
        compiler_params=_params(("parallel", "parallel", "arbitrary")),
    )(*([a, b] + ([acc] if has_acc else [])))


def _mm_fused(name, lhs, rhs, outs, *, tb=False, add=False, extras=(), epilogue=None):
    M, K = lhs[0].shape
    N = rhs[0].shape[0] if tb else rhs[0].shape[1]
    tm, tn, tk = _div_tile(M, 512), _div_tile(N, 1408), _div_tile(K, 1408)
    nk = K // tk
    n_l, n_e, n_o = len(lhs), len(extras), len(outs)
    n_acc = 1 if add else n_l
    dims = ((1,), (1,) if tb else (0,))

    def body(*refs):
        l_refs, r_refs = refs[:n_l], refs[n_l:2 * n_l]
        e_refs = refs[2 * n_l:2 * n_l + n_e]
        o_refs = refs[2 * n_l + n_e:2 * n_l + n_e + n_o]
        scr = refs[2 * n_l + n_e + n_o:]
        parts = [_dot(l[...].astype(BF16), r[...].astype(BF16), dims) for l, r in zip(l_refs, r_refs)]
        if add:
            parts = [functools.reduce(lambda u, v: u + v, parts)]

        def finish(vals):
            res = epilogue(vals, [e[...].astype(F32) for e in e_refs]) if epilogue else vals
            for ref, val in zip(o_refs, res):
                ref[...] = val.astype(ref.dtype)

        if nk == 1:
            finish(parts)
            return
        k = pl.program_id(2)

        @pl.when(k == 0)
        def _():
            for s, part in zip(scr, parts):
                s[...] = part

        @pl.when(k > 0)
        def _():
            for s, part in zip(scr, parts):
                s[...] += part

        @pl.when(k == nk - 1)
        def _():
            finish([s[...] for s in scr])

    a_spec = pl.BlockSpec((tm, tk), lambda i, j, k: (i, k))
    b_spec = pl.BlockSpec((tn, tk), lambda i, j, k: (j, k)) if tb else pl.BlockSpec((tk, tn), lambda i, j, k: (k, j))
    o_spec = pl.BlockSpec((tm, tn), lambda i, j, k: (i, j))
    return pl.pallas_call(
        body, name=name,
        grid=(M // tm, N // tn, nk),
        in_specs=[a_spec] * n_l + [b_spec] * n_l + [o_spec] * n_e,
        out_specs=[o_spec] * n_o,
        out_shape=[jax.ShapeDtypeStruct((M, N), dt) for dt in outs],
        scratch_shapes=[pltpu.VMEM((tm, tn), F32)] * (n_acc if nk > 1 else 0),
        compiler_params=_params(("parallel", "parallel", "arbitrary")),
    )(*lhs, *rhs, *extras)


def _row_spec(tile, width, colblk):
    return pl.BlockSpec((tile, width), lambda i: (i, colblk))


def _full_spec(shape):
    return pl.BlockSpec(shape, lambda i: (0,) * len(shape))


def _rowwise(name, fn, rows, params, outs, tile=256):
    T = rows[0][0].shape[0]
    tile = min(tile, T)
    n_r, n_p = len(rows), len(params)

    def body(*refs):
        r = [x[...].astype(F32) for x in refs[:n_r]]
        p = [x[...].astype(F32) for x in refs[n_r:n_r + n_p]]
        for o_ref, val in zip(refs[n_r + n_p:], fn(*r, *p)):
            o_ref[...] = val.astype(o_ref.dtype)

    return pl.pallas_call(
        body, name=name,
        grid=(T // tile,),
        in_specs=[_row_spec(tile, w, cb) for _, w, cb in rows] + [_full_spec(p.shape) for p in params],
        out_specs=[_row_spec(tile, w, 0) for w, _ in outs],
        out_shape=[jax.ShapeDtypeStruct((T, w), dt) for w, dt in outs],
        compiler_params=_params(("parallel",)),
    )(*([a for a, _, _ in rows] + list(params)))


def _rowwise_vjp(name, fn, rows, params, cts, need_rows, need_params, add_to=None, tile=256, bf16_rows=()):
    add_to = add_to or {}
    T = rows[0][0].shape[0]
    tile = min(tile, T)
    n_r, n_p = len(rows), len(params)
    ct_flat = [c for group in cts for c in group]
    ct_sizes = [len(group) for group in cts]
    add_idx = sorted(add_to)
    row_out = [i for i in range(n_r) if need_rows[i]]
    par_out = [i for i in range(n_p) if need_params[i]]
    n_ct, n_add = len(ct_flat), len(add_idx)

    def body(*refs):
        pos = 0
        r = [x[...].astype(F32) for x in refs[pos:pos + n_r]]
        pos += n_r
        p = [x[...].astype(F32) for x in refs[pos:pos + n_p]]
        pos += n_p
        ct_vals = [x[...].astype(F32) for x in refs[pos:pos + n_ct]]
        pos += n_ct
        adds = {i: x[...] for i, x in zip(add_idx, refs[pos:pos + n_add])}
        pos += n_add
        drow_refs = refs[pos:pos + len(row_out)]
        pos += len(row_out)
        dpar_refs = refs[pos:pos + len(par_out)]
        ct_in, q = [], 0
        for n in ct_sizes:
            ct_in.append(functools.reduce(lambda u, v: u + v, ct_vals[q:q + n]))
            q += n
        _, vjp = jax.vjp(fn, *r, *p)
        grads = vjp(tuple(ct_in))
        for ref, i in zip(drow_refs, row_out):
            g = grads[i]
            ref[...] = (g + adds[i] if i in adds else g).astype(ref.dtype)

        @pl.when(pl.program_id(0) == 0)
        def _():
            for ref in dpar_refs:
                ref[...] = jnp.zeros_like(ref)

        for ref, i in zip(dpar_refs, par_out):
            ref[...] += grads[n_r + i]

    ct_widths = [c.shape[1] for c in ct_flat]
    in_specs = ([_row_spec(tile, w, cb) for _, w, cb in rows] + [_full_spec(p.shape) for p in params]
                + [_row_spec(tile, w, 0) for w in ct_widths] + [_row_spec(tile, rows[i][1], 0) for i in add_idx])
    out_specs = [_row_spec(tile, rows[i][1], 0) for i in row_out] + [_full_spec(params[i].shape) for i in par_out]
    out_shape = ([jax.ShapeDtypeStruct((T, rows[i][1]), BF16 if i in bf16_rows else F32) for i in row_out]
                 + [jax.ShapeDtypeStruct(params[i].shape, F32) for i in par_out])
    res = pl.pallas_call(
        body, name=name,
        grid=(T // tile,),
        in_specs=in_specs, out_specs=out_specs, out_shape=out_shape,
        compiler_params=_params(("arbitrary",)),
    )(*([a for a, _, _ in rows] + list(params) + ct_flat + [add_to[i] for i in add_idx]))
    return res[:len(row_out)], res[len(row_out):]


def _sigmoid(x):
    return 0.5 * (jnp.tanh(0.5 * x) + 1.0)


def _softplus(x):
    return jnp.maximum(x, 0.0) + jnp.log(1.0 + jnp.exp(-jnp.abs(x)))


def _rms(x, g):
    return x * lax.rsqrt(jnp.mean(x * x, axis=-1, keepdims=True) + RMS_EPS) * g


def _segsum_impl(x):
    n, w = x.shape[-1], 2 * LANES
    r = lax.shift_right_logical(lax.broadcasted_iota(jnp.int32, (w, w), 0), 6)
    c = lax.shift_right_logical(lax.broadcasted_iota(jnp.int32, (w, w), 1), 6)
    bd = (r == c).astype(BF16)
    hi = x.astype(BF16)
    rest = x - hi.astype(F32)
    mid = rest.astype(BF16)
    lo = (rest - mid.astype(F32)).astype(BF16)
    nn = ((1,), (0,))
    blocks = [_dot(hi[:, j:j + w], bd, nn) + _dot(mid[:, j:j + w], bd, nn) + _dot(lo[:, j:j + w], bd, nn)
              for j in range(0, n, w)]
    return jnp.concatenate(blocks, axis=1)


@jax.custom_vjp
def _segsum(x):
    return _segsum_impl(x)


_segsum.defvjp(lambda x: (_segsum_impl(x), None), lambda _, g: (_segsum_impl(g),))


@jax.custom_vjp
def _mmb(a, w):
    return _dot(a.astype(BF16), w.astype(BF16), ((1,), (0,)))


def _mmb_fwd(a, w):
    return _mmb(a, w), (a, w)


def _mmb_bwd(res, g):
    a, w = res
    gb = g.astype(BF16)
    return _dot(gb, w.astype(BF16), ((1,), (1,))), _dot(a.astype(BF16), gb, ((0,), (0,)))


_mmb.defvjp(_mmb_fwd, _mmb_bwd)


def _f_norm(x, g):
    return (_rms(x, g),)


def _f_post1(x, u, g2, g3):
    x1 = x + _rms(u, g2)
    return x1, _rms(x1, g3)


def _f_swiglu(ag, au):
    return (ag * _sigmoid(ag) * au,)


def _f_merge(pg1, pg2, m1, m2, b1, b2):
    return (_sigmoid(pg1 + b1) * m1 + _sigmoid(pg2 + b2) * m2,)


def _f_out(x1, f, g4):
    return (x1 + _rms(f, g4),)


def _f_rwpre(pr, pk, pv, pz, qr, qk, qv, qz, mur, muk, muv, muz, w0, wup, a0, aup, gup, k_k, k_a):
    r = pr + (qr - pr) * mur
    k = pk + (qk - pk) * muk
    v = pv + (qv - pv) * muv
    z = pz + (qz - pz) * muz
    w_raw = w0 + _mmb(jnp.tanh(z), wup)
    lw = -jnp.exp(-_softplus(-w_raw) - 0.5)
    a = _sigmoid(a0 + _mmb(z, aup))
    g = _mmb(_sigmoid(z), gup)
    kk = k * k_k
    kap = kk * lax.rsqrt(jnp.maximum(_segsum(kk * kk), 1e-24))
    k2 = k * (1.0 + (a - 1.0) * k_a)
    return r, lw, k2, v, kap, a, g


def _f_rwpost(y, r, k2, v, g, lnx_w, lnx_b, r_k):
    inv = 1.0 / HEAD_DIM
    yc = y - _segsum(y) * inv
    var = _segsum(yc * yc) * inv
    yn = yc * lax.rsqrt(var + GN_EPS) * lnx_w + lnx_b
    bonus = _segsum(r * k2 * r_k) * v
    return ((yn + bonus) * g,)


RW_GROUPS = (0, 512, 1024, 1536, RW_COLS)


def _column_groups(p):
    return [p[:, a:b] for a, b in zip(RW_GROUPS[:-1], RW_GROUPS[1:])]


def _previous_tokens(p, halo, first_of_sequence):
    rows = lax.broadcasted_iota(jnp.int32, (p.shape[0], 1), 0)
    before = jnp.where(first_of_sequence, 0.0, halo[SUBLANES - 1:SUBLANES, :])
    return jnp.where(rows == 0, before, pltpu.roll(p, 1, axis=0))


def _halo_spec(tile, order):
    per = tile // SUBLANES
    return pl.BlockSpec((SUBLANES, RW_COLS), lambda i: (jnp.maximum(order(i) * per - 1, 0), 0))


def _rw_pre(p_rw, params, S, tile=128):
    T = p_rw.shape[0]
    tile = min(tile, T)
    assert S % tile == 0
    n_p = len(params)

    def body(*refs):
        p_ref, halo_ref = refs[0], refs[1]
        par = [x[...].astype(F32) for x in refs[2:2 + n_p]]
        p = p_ref[...]
        first = lax.rem(pl.program_id(0) * tile, S) == 0
        prev = _previous_tokens(p, halo_ref[...], first)
        for o_ref, val in zip(refs[2 + n_p:], _f_rwpre(*_column_groups(p), *_column_groups(prev), *par)):
            o_ref[...] = val

    out_spec = pl.BlockSpec((tile, RW_WIDTH), lambda i: (i, 0))
    return pl.pallas_call(
        body, name="rw_pre",
        grid=(T // tile,),
        in_specs=[pl.BlockSpec((tile, RW_COLS), lambda i: (i, 0)), _halo_spec(tile, lambda i: i)]
                 + [_full_spec(q.shape) for q in params],
        out_specs=[out_spec] * 7,
        out_shape=[jax.ShapeDtypeStruct((T, RW_WIDTH), F32)] * 7,
        compiler_params=_params(("parallel",)),
    )(p_rw, p_rw, *params)


def _rw_pre_bwd(p_rw, params, cts, S, tile=128):
    T = p_rw.shape[0]
    tile = min(tile, T)
    assert S % tile == 0
    nt = T // tile
    n_p = len(params)
    ct_flat = [c for group in cts for c in group]
    ct_sizes = [len(group) for group in cts]
    n_ct = len(ct_flat)

    def body(*refs):
        p_ref, halo_ref = refs[0], refs[1]
        par = [x[...].astype(F32) for x in refs[2:2 + n_p]]
        ct_vals = [x[...] for x in refs[2 + n_p:2 + n_p + n_ct]]
        dp_ref = refs[2 + n_p + n_ct]
        dpar_refs = refs[3 + n_p + n_ct:3 + 2 * n_p + n_ct]
        carry = refs[-1]
        step = pl.program_id(0)

        @pl.when(step == 0)
        def _():
            carry[...] = jnp.zeros_like(carry)
            for ref in dpar_refs:
                ref[...] = jnp.zeros_like(ref)

        ct_in, q = [], 0
        for n in ct_sizes:
            ct_in.append(functools.reduce(lambda u, v: u + v, ct_vals[q:q + n]))
            q += n
        p = p_ref[...]
        first = lax.rem((nt - 1 - step) * tile, S) == 0
        prev = _previous_tokens(p, halo_ref[...], first)
        _, vjp = jax.vjp(_f_rwpre, *_column_groups(p), *_column_groups(prev), *par)
        grads = vjp(tuple(ct_in))
        d_here = jnp.concatenate(grads[0:4], axis=1)
        d_prev = jnp.concatenate(grads[4:8], axis=1)
        rows = lax.broadcasted_iota(jnp.int32, (tile, 1), 0)
        from_next = jnp.where(rows == tile - 1, carry[0:1, :], pltpu.roll(d_prev, tile - 1, axis=0))
        dp_ref[...] = (d_here + from_next).astype(dp_ref.dtype)
        carry[...] = jnp.broadcast_to(jnp.where(first, 0.0, d_prev[0:1, :]), carry.shape)
        for ref, g in zip(dpar_refs, grads[8:]):
            ref[...] += g

    back = lambda i: nt - 1 - i
    row = lambda w: pl.BlockSpec((tile, w), lambda i: (back(i), 0))
    res = pl.pallas_call(
        body, name="rw_pre_bwd",
        grid=(nt,),
        in_specs=[row(RW_COLS), _halo_spec(tile, back)] + [_full_spec(q.shape) for q in params]
                 + [row(RW_WIDTH)] * n_ct,
        out_specs=[row(RW_COLS)] + [_full_spec(q.shape) for q in params],
        out_shape=[jax.ShapeDtypeStruct((T, RW_COLS), BF16)] + [jax.ShapeDtypeStruct(q.shape, F32) for q in params],
        scratch_shapes=[pltpu.VMEM((SUBLANES, RW_COLS), F32)],
        compiler_params=_params(("arbitrary",)),
    )(p_rw, p_rw, *params, *ct_flat)
    return res[0], res[1:]


def _loss_head(x1, f, target, g4, tile=256):
    T, D = x1.shape
    tile = min(tile, T)

    def body(x1_ref, f_ref, t_ref, g_ref, loss_ref, dx1_ref, df_ref, dg_ref):
        (y,), vjp = jax.vjp(_f_out, x1_ref[...], f_ref[...], g_ref[...])
        err = y - t_ref[...]
        dx1, df, dg = vjp((err * (1.0 / D),))
        dx1_ref[...] = dx1
        df_ref[...] = df.astype(df_ref.dtype)

        @pl.when(pl.program_id(0) == 0)
        def _():
            loss_ref[...] = jnp.zeros_like(loss_ref)
            dg_ref[...] = jnp.zeros_like(dg_ref)

        part = jnp.sum(jnp.sum(err * err, axis=1, keepdims=True), axis=0, keepdims=True) * (0.5 / D)
        loss_ref[...] += jnp.broadcast_to(part, loss_ref.shape)
        dg_ref[...] += dg

    row = pl.BlockSpec((tile, D), lambda i: (i, 0))
    return pl.pallas_call(
        body, name="loss_head",
        grid=(T // tile,),
        in_specs=[row, row, row, _full_spec(g4.shape)],
        out_specs=[_full_spec((SUBLANES, LANES)), row, row, _full_spec(g4.shape)],
        out_shape=[jax.ShapeDtypeStruct((SUBLANES, LANES), F32), jax.ShapeDtypeStruct((T, D), F32),
                   jax.ShapeDtypeStruct((T, D), BF16), jax.ShapeDtypeStruct(g4.shape, F32)],
        compiler_params=_params(("arbitrary",)),
    )(x1, f, target, g4)


def _nn(a, b):
    return _dot(a, b, ((1,), (0,)))


def _nt(a, b):
    return _dot(a, b, ((1,), (1,)))


def _tn(a, b):
    return _dot(a, b, ((0,), (0,)))


def _split_dot(x, u2):
    hi = x.astype(BF16)
    lo = (x - hi.astype(F32)).astype(BF16)
    return _nn(jnp.concatenate([hi, lo], axis=1), u2)


def _by_head(x, masks):
    return jnp.concatenate([(x * m).astype(BF16) for m in masks], axis=0)


def _fold_heads(x2, masks):
    R = x2.shape[0] // len(masks)
    return functools.reduce(lambda u, v: u + v, [x2[h * R:(h + 1) * R] * m for h, m in enumerate(masks)])


def _head_masks():
    lane = lax.broadcasted_iota(jnp.int32, (1, LANES), 1)
    return [((lane >= h * HEAD_DIM) & (lane < (h + 1) * HEAD_DIM)).astype(F32) for h in range(LANES // HEAD_DIM)]


def _key_tri(op):
    row = lax.broadcasted_iota(jnp.int32, (ATTN_KEYS, ATTN_KEYS), 0)
    col = lax.broadcasted_iota(jnp.int32, (ATTN_KEYS, ATTN_KEYS), 1)
    u = op(row, col).astype(BF16)
    return jnp.concatenate([u, u], axis=0)


def _causal(rows):
    row = lax.broadcasted_iota(jnp.int32, (rows, ATTN_KEYS), 0)
    col = lax.broadcasted_iota(jnp.int32, (rows, ATTN_KEYS), 1)
    return col < row


def _from_row(tree, r):
    return jax.tree.map(lambda x: x[r:], tree)


def _onto_rows(old, new, r):
    return jax.tree.map(lambda o, n: jnp.concatenate([o[:r], n], axis=0) if r else n, old, new)


def _sb_weights(qb16, kbh, c_fails, u_gt, strict):
    z_all = _nt(qb16, kbh)
    zs = [z_all[:, h * ATTN_KEYS:(h + 1) * ATTN_KEYS] for h in range(len(c_fails))]
    Ls = [jnp.minimum(-z, 0.0) - jnp.log(1.0 + jnp.exp(-jnp.abs(z))) for z in zs]
    Lms = Ls if strict is None else [jnp.where(strict, L, 0.0) for L in Ls]
    cums = [_split_dot(Lm, u_gt) for Lm in Lms]
    As = [jnp.exp(z + L + c + cum) for z, L, c, cum in zip(zs, Ls, c_fails, cums)]
    if strict is not None:
        As = [jnp.where(strict, A, 0.0) for A in As]
    return zs, Ls, Lms, As


def _attn_specs(S, qb):
    nq = S // qb
    q_spec = pl.BlockSpec((qb, LANES), lambda b, p, i: (b * nq + i, p))
    k_spec = pl.BlockSpec((S, LANES), lambda b, p, i: (b, SB_WIDTH // LANES + p))
    v_spec = pl.BlockSpec((S, LANES), lambda b, p, i: (b, 2 * SB_WIDTH // LANES + p))
    seq = pl.BlockSpec((S, LANES), lambda b, p, i: (b, p))
    return q_spec, k_spec, v_spec, q_spec, seq


def _key_walk(i, qb, block, carry):
    per = qb // ATTN_KEYS
    for sub in reversed(range(per)):
        carry = block(i * per + sub, carry, sub * ATTN_KEYS)
    return lax.fori_loop(0, i * per, lambda j, c: block(i * per - 1 - j, c, None), carry)


def _attn_fwd(proj, B, S):
    qb = min(ATTN_QUERIES, S)
    scale = HEAD_DIM ** -0.5

    def body(q_ref, k_ref, v_ref, o_ref):
        i = pl.program_id(2)
        masks = _head_masks()
        u_gt = _key_tri(lambda r, c: r > c)
        q16 = (q_ref[...] * scale).astype(BF16)

        def block(J, carry, row0):
            r0 = pl.multiple_of(J * ATTN_KEYS, ATTN_KEYS)
            kbh = _by_head(k_ref[pl.ds(r0, ATTN_KEYS), :], masks)
            vbh = _by_head(v_ref[pl.ds(r0, ATTN_KEYS), :], masks)
            lo = row0 or 0
            strict = None if row0 is None else _causal(qb - lo)
            acc, cs = _from_row(carry, lo)
            _, _, Lms, As = _sb_weights(q16[lo:], kbh, cs, u_gt, strict)
            acc = acc + _nn(jnp.concatenate([A.astype(BF16) for A in As], axis=1), vbh)
            cs = tuple(c + jnp.sum(Lm, axis=1, keepdims=True) for c, Lm in zip(cs, Lms))
            return _onto_rows(carry, (acc, cs), lo)

        zero_c = tuple(jnp.zeros((qb, 1), F32) for _ in masks)
        carry = _key_walk(i, qb, block, (jnp.zeros((qb, LANES), F32), zero_c))
        o_ref[...] = carry[0]

    q_spec, k_spec, v_spec, blk, _ = _attn_specs(S, qb)
    return pl.pallas_call(
        body, name="sb_attn_fwd",
        grid=(B, SB_WIDTH // LANES, S // qb),
        in_specs=[q_spec, k_spec, v_spec],
        out_specs=blk,
        out_shape=jax.ShapeDtypeStruct((B * S, SB_WIDTH), F32),
        compiler_params=_params(("parallel", "parallel", "arbitrary")),
    )(proj, proj, proj)


def _attn_bwd(proj, o, do, B, S):
    qb = min(ATTN_QUERIES, S)
    nq = S // qb
    scale = HEAD_DIM ** -0.5

    def body(q_ref, k_ref, v_ref, o_ref, do_ref, dq_ref, dk_out, dv_out, dk_ref, dv_ref):
        i = pl.program_id(2)

        @pl.when(i == 0)
        def _():
            dk_ref[...] = jnp.zeros_like(dk_ref)
            dv_ref[...] = jnp.zeros_like(dv_ref)

        masks = _head_masks()
        u_gt = _key_tri(lambda r, c: r > c)
        u_ge = _key_tri(lambda r, c: r >= c)
        heads = range(len(masks))
        q16 = (q_ref[...] * scale).astype(BF16)
        do16 = do_ref[...].astype(BF16)
        od = o_ref[...] * do16.astype(F32)
        totals = tuple(jnp.sum(od * m, axis=1, keepdims=True) for m in masks)

        def block(J, carry, row0):
            r0 = pl.multiple_of(J * ATTN_KEYS, ATTN_KEYS)
            kbh = _by_head(k_ref[pl.ds(r0, ATTN_KEYS), :], masks)
            vbh = _by_head(v_ref[pl.ds(r0, ATTN_KEYS), :], masks)
            lo = row0 or 0
            strict = None if row0 is None else _causal(qb - lo)
            dq, c_fail, c_p = _from_row(carry, lo)
            tot = _from_row(totals, lo)
            zs, Ls, Lms, As = _sb_weights(q16[lo:], kbh, c_fail, u_gt, strict)
            Abs = [A.astype(BF16) for A in As]
            dA_all = _nt(do16[lo:], vbh)
            Ps = [Abs[h].astype(F32) * dA_all[:, h * ATTN_KEYS:(h + 1) * ATTN_KEYS] for h in heads]
            afters = [c_p[h] + _split_dot(Ps[h], u_ge) for h in heads]
            sigs = [jnp.exp(zs[h] + Ls[h]) for h in heads]
            dzs = [Ps[h] * (1.0 - sigs[h]) - sigs[h] * (tot[h] - afters[h]) for h in heads]
            if strict is not None:
                dzs = [jnp.where(strict, dz, 0.0) for dz in dzs]
            dz_all = jnp.concatenate([dz.astype(BF16) for dz in dzs], axis=1)
            dv_ref[pl.ds(r0, ATTN_KEYS), :] += _fold_heads(_tn(jnp.concatenate(Abs, axis=1), do16[lo:]), masks)
            dk_ref[pl.ds(r0, ATTN_KEYS), :] += _fold_heads(_tn(dz_all, q16[lo:]), masks)
            dq = dq + _nn(dz_all, kbh)
            c_fail = tuple(c_fail[h] + jnp.sum(Lms[h], axis=1, keepdims=True) for h in heads)
            c_p = tuple(c_p[h] + jnp.sum(Ps[h], axis=1, keepdims=True) for h in heads)
            return _onto_rows(carry, (dq, c_fail, c_p), lo)

        zc = tuple(jnp.zeros((qb, 1), F32) for _ in masks)
        carry = _key_walk(i, qb, block, (jnp.zeros((qb, LANES), F32), zc, zc))
        dq_ref[...] = (carry[0] * scale).astype(dq_ref.dtype)

        @pl.when(i == nq - 1)
        def _():
            dk_out[...] = dk_ref[...].astype(dk_out.dtype)
            dv_out[...] = dv_ref[...].astype(dv_out.dtype)

    q_spec, k_spec, v_spec, blk, seq = _attn_specs(S, qb)
    return pl.pallas_call(
        body, name="sb_attn_bwd",
        grid=(B, SB_WIDTH // LANES, nq),
        in_specs=[q_spec, k_spec, v_spec, blk, blk],
        out_specs=[blk, seq, seq],
        out_shape=[jax.ShapeDtypeStruct((B * S, SB_WIDTH), BF16)] * 3,
        scratch_shapes=[pltpu.VMEM((S, LANES), F32), pltpu.VMEM((S, LANES), F32)],
        compiler_params=_params(("parallel", "parallel", "arbitrary")),
    )(proj, proj, proj, o, do)


_BATCHED = {"nn": "gmk,gkn->gmn", "nt": "gmk,gnk->gmn", "tn": "gkm,gkn->gmn"}


def _bdot_raw(a, b, kind, passes):
    e = functools.partial(jnp.einsum, _BATCHED[kind], preferred_element_type=F32)
    ah, bh = a.astype(BF16), b.astype(BF16)
    if passes == 1:
        return e(ah, bh)
    al, bl = (a - ah.astype(F32)).astype(BF16), (b - bh.astype(F32)).astype(BF16)
    return e(ah, bh) + e(ah, bl) + e(al, bh)


@functools.partial(jax.custom_vjp, nondiff_argnums=(2, 3))
def _bdot(a, b, kind, passes):
    return _bdot_raw(a, b, kind, passes)


def _bdot_fwd(a, b, kind, passes):
    return _bdot_raw(a, b, kind, passes), (a, b)


def _bdot_bwd(kind, passes, res, g):
    a, b = res
    if kind == "nn":
        return _bdot_raw(g, b, "nt", passes), _bdot_raw(a, g, "tn", passes)
    if kind == "nt":
        return _bdot_raw(g, b, "nn", passes), _bdot_raw(g, a, "tn", passes)
    return _bdot_raw(b, g, "nt", passes), _bdot_raw(a, g, "nn", passes)


_bdot.defvjp(_bdot_fwd, _bdot_bwd)


def _solve_powers(m):
    powers = [m]
    for _ in range(max(1, (m.shape[1] - 1).bit_length()) - 1):
        powers.append(_bdot_raw(powers[-1], powers[-1], "nn", 1))
    return powers


def _solve_fwd(m, rhs):
    powers = _solve_powers(m)
    x = rhs
    for p in powers:
        x = x + _bdot_raw(p, x, "nn", 1)
    return x, (powers, x)


def _solve_bwd(res, g):
    powers, x = res
    for p in powers:
        g = g + _bdot_raw(p, g, "tn", 1)
    return _bdot_raw(g, x, "nt", 1), g


@jax.custom_vjp
def _unit_lower_solve(m, rhs):
    return _solve_fwd(m, rhs)[0]


_unit_lower_solve.defvjp(_solve_fwd, _solve_bwd)


def _wkv_chunk(S0, r, lw, k, v, kap, a):
    G, C, N = r.shape
    row = lax.broadcasted_iota(jnp.int32, (C, C), 0)
    col = lax.broadcasted_iota(jnp.int32, (C, C), 1)
    incl = (col <= row).astype(F32)
    strict = (col < row).astype(F32)
    cum = _bdot(jnp.broadcast_to(incl, (G, C, C)), lw, "nn", 3)
    e_pos = jnp.exp(cum)
    e_neg = jnp.exp(-cum)
    al = -kap * jnp.exp(cum - lw)
    be = kap * a * e_neg
    kt = k * e_neg
    rt = r * e_pos
    bk = jnp.concatenate([be, kt], axis=1)
    mask = jnp.concatenate([jnp.concatenate([strict, strict], axis=1), jnp.concatenate([incl, incl], axis=1)], axis=0)
    m_all = _bdot(jnp.concatenate([al, rt], axis=1), bk, "nt", 3) * mask
    m_ab, m_ak = m_all[:, :C, :C], m_all[:, :C, C:]
    m_rb, m_rk = m_all[:, C:, :C], m_all[:, C:, C:]
    S0t = jnp.swapaxes(S0, 1, 2)
    sa = _unit_lower_solve(m_ab, _bdot(jnp.concatenate([al, m_ak], axis=2), jnp.concatenate([S0t, v], axis=1), "nn", 3))
    y =_bdot(jnp.concatenate([rt, m_rb, m_rk], axis=2), jnp.concatenate([S0t, sa, v], axis=1), "nn", 3)
    S1 = (S0 + _bdot(jnp.concatenate([sa, v], axis=1), bk, "tn", 1)) * e_pos[:, C - 1:C, :]
    return y, S1


def _split_heads(x):
    return jnp.stack([x[:, h * HEAD_DIM:(h + 1) * HEAD_DIM] for h in range(x.shape[1] // HEAD_DIM)], axis=0)


def _merge_heads(x):
    return jnp.concatenate([x[h] for h in range(x.shape[0])], axis=1)


def _seq_heads(ref):
    return jnp.concatenate([_split_heads(ref[s]) for s in range(ref.shape[0])], axis=0)


def _store_seq_heads(ref, x):
    heads = x.shape[0] // ref.shape[0]
    for s in range(ref.shape[0]):
        ref[s] = _merge_heads(x[s * heads:(s + 1) * heads])


def _wkv_fwd(r, lw, k, v, kap, a, B, S):
    C, H, N = WKV_CHUNK, RW_WIDTH // HEAD_DIM, HEAD_DIM
    nc = S // C
    Q = min(WKV_SEQS, B)

    def body(r_ref, lw_ref, k_ref, v_ref, kap_ref, a_ref, y_ref, st_ref, s_scr):
        @pl.when(pl.program_id(1) == 0)
        def _():
            s_scr[...] = jnp.zeros_like(s_scr)

        S0 = s_scr[...]
        for s in range(Q):
            st_ref[s, 0] = S0[s * H:(s + 1) * H]
        args = [_seq_heads(ref) for ref in (r_ref, lw_ref, k_ref, v_ref, kap_ref, a_ref)]
        y, S1 = _wkv_chunk(S0, *args)
        s_scr[...] = S1
        _store_seq_heads(y_ref, y)

    row_spec = pl.BlockSpec((Q, C, RW_WIDTH), lambda b, c: (b, c, 0))
    seqs = lambda t: t.reshape(B, S, RW_WIDTH)
    y, states = pl.pallas_call(
        body, name="wkv_fwd",
        grid=(B // Q, nc),
        in_specs=[row_spec] * 6,
        out_specs=[row_spec, pl.BlockSpec((Q, 1, H, N, N), lambda b, c: (b, c, 0, 0, 0))],
        out_shape=[jax.ShapeDtypeStruct((B, S, RW_WIDTH), F32), jax.ShapeDtypeStruct((B, nc, H, N, N), F32)],
        scratch_shapes=[pltpu.VMEM((Q * H, N, N), F32)],
        compiler_params=_params(("arbitrary", "arbitrary")),
    )(*map(seqs, (r, lw, k, v, kap, a)))
    return y.reshape(B * S, RW_WIDTH), states


def _wkv_bwd(r, lw, k, v, kap, a, states, dy, B, S):
    C, H, N = WKV_CHUNK, RW_WIDTH // HEAD_DIM, HEAD_DIM
    nc = S // C
    Q = min(WKV_SEQS, B)

    def body(r_ref, lw_ref, k_ref, v_ref, kap_ref, a_ref, st_ref, dy_ref,
             dr_ref, dlw_ref, dk_ref, dv_ref, dkap_ref, da_ref, ds_scr):
        @pl.when(pl.program_id(1) == 0)
        def _():
            ds_scr[...] = jnp.zeros_like(ds_scr)

        args = [_seq_heads(ref) for ref in (r_ref, lw_ref, k_ref, v_ref, kap_ref, a_ref)]
        S0 = jnp.concatenate([st_ref[s, 0] for s in range(Q)], axis=0)
        _, vjp = jax.vjp(_wkv_chunk, S0, *args)
        g = vjp((_seq_heads(dy_ref), ds_scr[...]))
        ds_scr[...] = g[0]
        for ref, gv in zip((dr_ref, dlw_ref, dk_ref, dv_ref, dkap_ref, da_ref), g[1:]):
            _store_seq_heads(ref, gv)

    row_spec = pl.BlockSpec((Q, C, RW_WIDTH), lambda b, c: (b, nc - 1 - c, 0))
    st_spec = pl.BlockSpec((Q, 1, H, N, N), lambda b, c: (b, nc - 1 - c, 0, 0, 0))
    seqs = lambda t: t.reshape(B, S, RW_WIDTH)
    res = pl.pallas_call(
        body, name="wkv_bwd",
        grid=(B // Q, nc),
        in_specs=[row_spec] * 6 + [st_spec, row_spec],
        out_specs=[row_spec] * 6,
        out_shape=[jax.ShapeDtypeStruct((B, S, RW_WIDTH), F32)] * 6,
        scratch_shapes=[pltpu.VMEM((Q * H, N, N), F32)],
        compiler_params=_params(("arbitrary", "arbitrary"), vmem=WKV_BWD_VMEM),
    )(*map(seqs, (r, lw, k, v, kap, a)), states, seqs(dy))
    return [t.reshape(B * S, RW_WIDTH) for t in res]


HBM = pl.BlockSpec(memory_space=pl.ANY)


def _place():
    return lax.axis_index("x"), lax.axis_index("y"), lax.axis_index("c")


def _other_chips(x, y):
    return [(1 - x, y), (x, 1 - y), (1 - x, 1 - y)]


def _all_gather_chips(shards):
    n = len(shards)

    def body(*refs):
        ins, outs = refs[:n], refs[n:2 * n]
        ici_send, ici_recv, d2d_send, d2d_recv, local = refs[2 * n:]
        x, y, c = _place()
        me = 2 * x + y
        sib = (x, y, 1 - c)
        chips = _other_chips(x, y)
        started, copies = [], []
        for w in range(n):
            cp = pltpu.make_async_copy(ins[w].at[c], outs[w].at[me, c], local.at[w])
            cp.start()
            copies.append(cp)
            for j, (px, py) in enumerate(chips):
                rd = pltpu.make_async_remote_copy(
                    src_ref=ins[w].at[c], dst_ref=outs[w].at[me, c], send_sem=ici_send.at[3 * w + j],
                    recv_sem=ici_recv.at[3 * w + j], device_id=(px, py, c), device_id_type=MESH)
                rd.start()
                started.append(rd)
            rd = pltpu.make_async_remote_copy(
                src_ref=ins[w].at[c], dst_ref=outs[w].at[me, c], send_sem=d2d_send.at[4 * w + 3],
                recv_sem=d2d_recv.at[4 * w + 3], device_id=sib, device_id_type=MESH)
            rd.start()
            started.append(rd)
        for w in range(n):
            for j, (px, py) in enumerate(chips):
                src = 2 * px + py
                pltpu.make_async_remote_copy(
                    src_ref=ins[w].at[c], dst_ref=outs[w].at[src, c], send_sem=ici_send.at[3 * w + j],
                    recv_sem=ici_recv.at[3 * w + j], device_id=(px, py, c), device_id_type=MESH).wait_recv()
                rd = pltpu.make_async_remote_copy(
                    src_ref=outs[w].at[src, c], dst_ref=outs[w].at[src, c], send_sem=d2d_send.at[4 * w + j],
                    recv_sem=d2d_recv.at[4 * w + j], device_id=sib, device_id_type=MESH)
                rd.start()
                started.append(rd)
        for w in range(n):
            for j, (px, py) in enumerate(chips):
                pltpu.make_async_remote_copy(
                    src_ref=ins[w].at[c], dst_ref=outs[w].at[2 * px + py, 1 - c], send_sem=d2d_send.at[4 * w + j],
                    recv_sem=d2d_recv.at[4 * w + j], device_id=sib, device_id_type=MESH).wait_recv()
            pltpu.make_async_remote_copy(
                src_ref=ins[w].at[c], dst_ref=outs[w].at[me, 1 - c], send_sem=d2d_send.at[4 * w + 3],
                recv_sem=d2d_recv.at[4 * w + 3], device_id=sib, device_id_type=MESH).wait_recv()
        for rd in started:
            rd.wait_send()
        for cp in copies:
            cp.wait()

    return pl.pallas_call(
        body, name="gather_weights",
        in_specs=[HBM] * n, out_specs=[HBM] * n,
        out_shape=[jax.ShapeDtypeStruct((N_CHIPS,) + s.shape, s.dtype) for s in shards],
        scratch_shapes=[pltpu.SemaphoreType.DMA((3 * n,)), pltpu.SemaphoreType.DMA((3 * n,)),
                        pltpu.SemaphoreType.DMA((4 * n,)), pltpu.SemaphoreType.DMA((4 * n,)),
                        pltpu.SemaphoreType.DMA((n,))],
        compiler_params=pltpu.CompilerParams(has_side_effects=True),
    )(*shards)


def _pair_split(grads):
    n = len(grads)

    def body(*refs):
        ins, theirs = refs[:n], refs[n:2 * n]
        send, recv = refs[2 * n:]
        x, y, c = _place()
        sib = (x, y, 1 - c)
        rds = []
        for w in range(n):
            rd = pltpu.make_async_remote_copy(
                src_ref=ins[w].at[:, 1 - c], dst_ref=theirs[w], send_sem=send.at[w], recv_sem=recv.at[w],
                device_id=sib, device_id_type=MESH)
            rd.start()
            rds.append(rd)
        for rd in rds:
            rd.wait_recv()
        for rd in rds:
            rd.wait_send()

    return pl.pallas_call(
        body, name="grad_pair_split",
        in_specs=[HBM] * n, out_specs=[HBM] * n,
        out_shape=[jax.ShapeDtypeStruct((g.shape[0],) + g.shape[2:], g.dtype) for g in grads],
        scratch_shapes=[pltpu.SemaphoreType.DMA((n,)), pltpu.SemaphoreType.DMA((n,))],
        compiler_params=pltpu.CompilerParams(has_side_effects=True),
    )(*grads)


def _chip_scatter(parts):
    n = len(parts)

    def body(*refs):
        ins, outs = refs[:n], refs[n:2 * n]
        send, recv = refs[2 * n:]
        x, y, c = _place()
        me = 2 * x + y
        rds = []
        for w in range(n):
            for j, (px, py) in enumerate(_other_chips(x, y)):
                s = 3 * w + j
                rd = pltpu.make_async_remote_copy(
                    src_ref=ins[w].at[2 * px + py], dst_ref=outs[w].at[j], send_sem=send.at[s], recv_sem=recv.at[s],
                    device_id=(px, py, c), device_id_type=MESH)
                rd.start()
                rds.append(rd)
        for w in range(n):
            for j, (px, py) in enumerate(_other_chips(x, y)):
                s = 3 * w + j
                pltpu.make_async_remote_copy(
                    src_ref=ins[w].at[me], dst_ref=outs[w].at[j], send_sem=send.at[s], recv_sem=recv.at[s],
                    device_id=(px, py, c), device_id_type=MESH).wait_recv()
        for rd in rds:
            rd.wait_send()

    return pl.pallas_call(
        body, name="grad_chip_scatter",
        in_specs=[HBM] * n, out_specs=[HBM] * n,
        out_shape=[jax.ShapeDtypeStruct((N_CHIPS - 1,) + p.shape[1:], p.dtype) for p in parts],
        scratch_shapes=[pltpu.SemaphoreType.DMA((3 * n,)), pltpu.SemaphoreType.DMA((3 * n,))],
        compiler_params=pltpu.CompilerParams(has_side_effects=True),
    )(*parts)


def _pair_join(bufs):
    n = len(bufs)

    def body(*refs):
        ins, outs = refs[:n], refs[n:2 * n]
        send, recv = refs[2 * n:]
        x, y, c = _place()
        sib = (x, y, 1 - c)
        rds = []
        for w in range(n):
            rd = pltpu.make_async_remote_copy(
                src_ref=ins[w].at[c], dst_ref=outs[w].at[c], send_sem=send.at[w], recv_sem=recv.at[w],
                device_id=sib, device_id_type=MESH)
            rd.start()
            rds.append(rd)
        for w in range(n):
            pltpu.make_async_remote_copy(
                src_ref=ins[w].at[c], dst_ref=outs[w].at[1 - c], send_sem=send.at[w], recv_sem=recv.at[w],
                device_id=sib, device_id_type=MESH).wait_recv()
        for rd in rds:
            rd.wait_send()

    return pl.pallas_call(
        body, name="grad_pair_join",
        in_specs=[HBM] * n, out_specs=[HBM] * n,
        out_shape=[jax.ShapeDtypeStruct(b.shape, b.dtype) for b in bufs],
        input_output_aliases={w: w for w in range(n)},
        scratch_shapes=[pltpu.SemaphoreType.DMA((n,)), pltpu.SemaphoreType.DMA((n,))],
        compiler_params=pltpu.CompilerParams(has_side_effects=True),
    )(*bufs)


def _all_reduce_small(packed):
    R = packed.shape[0]

    def body(x_ref, o_ref, buf, send, recv):
        x, y, c = _place()
        me = 4 * x + 2 * y + c
        buf[me] = x_ref[...]
        rds = []
        for rel in range(1, N_DEV):
            fx, fy, fc = (rel >> 2) & 1, (rel >> 1) & 1, rel & 1
            peer = (1 - x if fx else x, 1 - y if fy else y, 1 - c if fc else c)
            rd = pltpu.make_async_remote_copy(
                src_ref=x_ref, dst_ref=buf.at[me], send_sem=send.at[rel - 1], recv_sem=recv.at[rel - 1],
                device_id=peer, device_id_type=MESH)
            rd.start()
            rds.append((rd, peer))
        for rel in range(1, N_DEV):
            rd, (px, py, pc) = rds[rel - 1]
            pltpu.make_async_remote_copy(
                src_ref=x_ref, dst_ref=buf.at[4 * px + 2 * py + pc], send_sem=send.at[rel - 1], recv_sem=recv.at[rel - 1],
                device_id=(px, py, pc), device_id_type=MESH).wait_recv()
        for rd, _ in rds:
            rd.wait_send()
        total = buf[0]
        for d in range(1, N_DEV):
            total = total + buf[d]
        o_ref[...] = total

    return pl.pallas_call(
        body, name="all_reduce_small",
        in_specs=[pl.BlockSpec(memory_space=pltpu.VMEM)],
        out_specs=pl.BlockSpec(memory_space=pltpu.VMEM),
        out_shape=jax.ShapeDtypeStruct(packed.shape, F32),
        scratch_shapes=[pltpu.VMEM((N_DEV, R, LANES), F32), pltpu.SemaphoreType.DMA((N_DEV - 1,)),
                        pltpu.SemaphoreType.DMA((N_DEV - 1,))],
        compiler_params=pltpu.CompilerParams(has_side_effects=True),
    )(packed)


def _pair_sum(name, split, theirs, core):
    n_chip, _, Rh, C = split.shape
    tile = _div_tile(Rh, 256, 2 * SUBLANES)
    nt = Rh // tile

    def body(core_ref, a_ref, b_ref, o_ref):
        o_ref[...] = (a_ref[...] + b_ref[...]).astype(o_ref.dtype)

    return pl.pallas_call(
        body, name=name,
        grid_spec=pltpu.PrefetchScalarGridSpec(
            num_scalar_prefetch=1,
            grid=(n_chip, nt),
            in_specs=[pl.BlockSpec((None, None, tile, C), lambda j, i, core_ref: (j, core_ref[0], i, 0)),
                      pl.BlockSpec((None, tile, C), lambda j, i, core_ref: (j, i, 0))],
            out_specs=pl.BlockSpec((None, tile, C), lambda j, i, core_ref: (j, i, 0)),
        ),
        out_shape=jax.ShapeDtypeStruct((n_chip, Rh, C), BF16),
        compiler_params=_params(("parallel", "parallel")),
    )(core, split, theirs)


def _chip_sum(name, own, landed, core):
    n_in, Rh, C = landed.shape
    tile = _div_tile(Rh, 256, 2 * SUBLANES)

    def body(core_ref, *refs):
        total = refs[0][...].astype(F32)
        for ref in refs[1:n_in + 1]:
            total = total + ref[...].astype(F32)
        refs[n_in + 1][...] = total

    slot = lambda j: pl.BlockSpec((None, tile, C), lambda i, core_ref: (j, i, 0))
    return pl.pallas_call(
        body, name=name,
        grid_spec=pltpu.PrefetchScalarGridSpec(
            num_scalar_prefetch=1,
            grid=(Rh // tile,),
            in_specs=[pl.BlockSpec((None, tile, C), lambda i, core_ref: (core_ref[1], i, 0))]
                     + [slot(j) for j in range(n_in)],
            out_specs=pl.BlockSpec((None, tile, C), lambda i, core_ref: (core_ref[0], i, 0)),
        ),
        out_shape=jax.ShapeDtypeStruct((2, Rh, C), F32),
        compiler_params=_params(("parallel",)),
    )(core, own, *([landed] * n_in))


def _adamw(name, w, g, m, v):
    R, C = w.shape
    tile = _div_tile(R, 256, SUBLANES)
    c1 = 1.0 / (1.0 - ADAM_B1 ** ADAM_STEP)
    c2 = 1.0 / (1.0 - ADAM_B2 ** ADAM_STEP)

    def body(w_ref, g_ref, m_ref, v_ref, d_ref, nm_ref, nv_ref):
        g_ = g_ref[...]
        nm = ADAM_B1 * m_ref[...] + (1.0 - ADAM_B1) * g_
        nv = ADAM_B2 * v_ref[...] + (1.0 - ADAM_B2) * (g_ * g_)
        d_ref[...] = -ADAM_LR * ((nm * c1) / (jnp.sqrt(nv * c2) + ADAM_EPS) + ADAM_WD * w_ref[...])
        nm_ref[...] = nm
        nv_ref[...] = nv

    spec = pl.BlockSpec((tile, C), lambda i: (i, 0))
    return pl.pallas_call(
        body, name=name,
        grid=(R // tile,),
        in_specs=[spec] * 4, out_specs=[spec] * 3,
        out_shape=[jax.ShapeDtypeStruct((R, C), F32)] * 3,
        compiler_params=_params(("parallel",)),
    )(w, g, m, v)


SMALL =["norm_mix_pre", "b_gate", "mu_rw", "w0", "a0", "k_k", "k_a", "r_k", "lnx_w", "lnx_b",
         "norm_mix_post", "norm_ffn_pre", "norm_ffn_post"]
BIG = ["w_in", "w_up", "a_up", "g_up", "w_sb_out", "w_rw_out", "w_o", "w_ffn_gate", "w_ffn_up", "w_ffn_down"]
ROW_SHARDED = ("w_o", "w_ffn_down")
ORDER = ["norm_mix_pre", "w_in", "b_gate", "mu_rw", "w0", "w_up", "a0", "a_up", "g_up", "k_k", "k_a", "r_k",
         "lnx_w", "lnx_b", "w_sb_out", "w_rw_out", "w_o", "norm_mix_post", "norm_ffn_pre", "w_ffn_gate",
         "w_ffn_up", "w_ffn_down", "norm_ffn_post"]


def _pack_small(vals, extra_rows=0):
    rows = jnp.concatenate([vals[n].reshape(-1, LANES) for n in SMALL], axis=0)
    pad = (-(rows.shape[0] + extra_rows)) % SUBLANES + extra_rows
    return jnp.pad(rows, ((0, pad), (0, 0)))


def _unpack_small(packed, shapes):
    out, r = {}, 0
    for n in SMALL:
        size = 1
        for s in shapes[n]:
            size *= s
        out[n] = packed[r:r + size // LANES].reshape(shapes[n])
        r += size // LANES
    return out


def kernel(x, norm_mix_pre, w_in, b_gate, mu_rw, w0, w_up, a0, a_up, g_up, k_k, k_a, r_k, lnx_w, lnx_b, w_sb_out, w_rw_out, w_o, norm_mix_post, norm_ffn_pre, w_ffn_gate, w_ffn_up, w_ffn_down, norm_ffn_post, loss_target, m_norm_mix_pre, m_w_in, m_b_gate, m_mu_rw, m_w0, m_w_up, m_a0, m_a_up, m_g_up, m_k_k, m_k_a, m_r_k, m_lnx_w, m_lnx_b, m_w_sb_out, m_w_rw_out, m_w_o, m_norm_mix_post, m_norm_ffn_pre, m_w_ffn_gate, m_w_ffn_up, m_w_ffn_down, m_norm_ffn_post, v_norm_mix_pre, v_w_in, v_b_gate, v_mu_rw, v_w0, v_w_up, v_a0, v_a_up, v_g_up, v_k_k, v_k_a, v_r_k, v_lnx_w, v_lnx_b, v_w_sb_out, v_w_rw_out, v_w_o, v_norm_mix_post, v_norm_ffn_pre, v_w_ffn_gate, v_w_ffn_up, v_w_ffn_down, v_norm_ffn_post):
    W = dict(norm_mix_pre=norm_mix_pre, w_in=w_in, b_gate=b_gate, mu_rw=mu_rw, w0=w0, w_up=w_up, a0=a0, a_up=a_up,
             g_up=g_up, k_k=k_k, k_a=k_a, r_k=r_k, lnx_w=lnx_w, lnx_b=lnx_b, w_sb_out=w_sb_out, w_rw_out=w_rw_out,
             w_o=w_o, norm_mix_post=norm_mix_post, norm_ffn_pre=norm_ffn_pre, w_ffn_gate=w_ffn_gate,
             w_ffn_up=w_ffn_up, w_ffn_down=w_ffn_down, norm_ffn_post=norm_ffn_post)
    Mo = dict(norm_mix_pre=m_norm_mix_pre, w_in=m_w_in, b_gate=m_b_gate, mu_rw=m_mu_rw, w0=m_w0, w_up=m_w_up, a0=m_a0,
              a_up=m_a_up, g_up=m_g_up, k_k=m_k_k, k_a=m_k_a, r_k=m_r_k, lnx_w=m_lnx_w, lnx_b=m_lnx_b,
              w_sb_out=m_w_sb_out, w_rw_out=m_w_rw_out, w_o=m_w_o, norm_mix_post=m_norm_mix_post,
              norm_ffn_pre=m_norm_ffn_pre, w_ffn_gate=m_w_ffn_gate, w_ffn_up=m_w_ffn_up, w_ffn_down=m_w_ffn_down,
              norm_ffn_post=m_norm_ffn_post)
    Vo = dict(norm_mix_pre=v_norm_mix_pre, w_in=v_w_in, b_gate=v_b_gate, mu_rw=v_mu_rw, w0=v_w0, w_up=v_w_up, a0=v_a0,
              a_up=v_a_up, g_up=v_g_up, k_k=v_k_k, k_a=v_k_a, r_k=v_r_k, lnx_w=v_lnx_w, lnx_b=v_lnx_b,
              w_sb_out=v_w_sb_out, w_rw_out=v_w_rw_out, w_o=v_w_o, norm_mix_post=v_norm_mix_post,
              norm_ffn_pre=v_norm_ffn_pre, w_ffn_gate=v_w_ffn_gate, w_ffn_up=v_w_ffn_up, w_ffn_down=v_w_ffn_down,
              norm_ffn_post=v_norm_ffn_post)
    shapes = {n: W[n].shape for n in ORDER}
    B, S, D = x.shape
    T = B * S
    x2 = x.reshape(T, D)
    tgt = loss_target.reshape(T, D)
    vec = {n: W[n].reshape(1, -1) for n in SMALL}

    work = lambda t, n: t[0] if n in ROW_SHARDED else jnp.swapaxes(t[0], 0, 1)
    halved = [work(W[n], n).astype(BF16) for n in BIG]
    halved = [h.reshape(2, h.shape[0] // 2, h.shape[1]) for h in halved]
    full = {n: gth.reshape(-1, gth.shape[3]) for n, gth in zip(BIG, _all_gather_chips(halved))}
    w_in_t = full["w_in"]
    w_sb_t, w_rw_t, w_gt_t = w_in_t[:SB_COLS], w_in_t[SB_COLS:SB_COLS + RW_COLS], w_in_t[SB_COLS + RW_COLS:]
    lora_rows = {"w_up": 0, "a_up": 64, "g_up": 128}
    lora = {n: jnp.pad(full[n].T, ((r0, LORA_COLS - r0 - full[n].shape[1]), (0, 0))) for n, r0 in lora_rows.items()}
    mu = vec["mu_rw"]
    mu_parts = [mu[:, :512], mu[:, 512:1024], mu[:, 1024:1536], mu[:, 1536:]]
    b1, b2 = vec["b_gate"][:, :D], vec["b_gate"][:, D:]

    (h1,) = _rowwise("norm_mix_pre", _f_norm, [(x2, D, 0)], [vec["norm_mix_pre"]], [(D, BF16)])
    p_sb = _mm("proj_sb", h1, w_sb_t, tb=True, out_dtype=BF16)
    p_rw = _mm("proj_rw", h1, w_rw_t, tb=True)
    p_gt = _mm("proj_gate", h1, w_gt_t, tb=True, out_dtype=BF16)
    o_sb = _attn_fwd(p_sb, B, S)
    pre_params = mu_parts + [vec["w0"], lora["w_up"], vec["a0"], lora["a_up"], lora["g_up"], vec["k_k"], vec["k_a"]]
    r_, lw_, k2_, v_, kap_, a_, g_ = _rw_pre(p_rw, pre_params, S)
    y_wkv, states = _wkv_fwd(r_, lw_, k2_, v_, kap_, a_, B, S)
    post_rows = [(y_wkv, 512, 0), (r_, 512, 0), (k2_, 512, 0), (v_, 512, 0), (g_, 512, 0)]
    post_params = [vec["lnx_w"], vec["lnx_b"], vec["r_k"]]
    (o_rw,) = _rowwise("rw_post", _f_rwpost, post_rows, post_params, [(512, BF16)])
    m1 = _mm("mix_sb_out", o_sb, full["w_sb_out"], tb=True, out_dtype=BF16)
    m2 = _mm("mix_rw_out", o_rw, full["w_rw_out"], tb=True, out_dtype=BF16)
    merge_rows = [(p_gt, D, 0), (p_gt, D, 1), (m1, D, 0), (m2, D, 0)]
    (merged,) = _rowwise("merge", _f_merge, merge_rows, [b1, b2], [(D, BF16)])
    u = _mm("mix_out", merged, full["w_o"])
    post1_params = [vec["norm_mix_post"], vec["norm_ffn_pre"]]
    x1, h2 = _rowwise("post_mix", _f_post1, [(x2, D, 0), (u, D, 0)], post1_params, [(D, F32), (D, BF16)])
    ag, au, sw = _mm_fused("ffn_in", [h2, h2], [full["w_ffn_gate"], full["w_ffn_up"]], [BF16] * 3, tb=True,
                           epilogue=lambda gu, _: (gu[0], gu[1], _f_swiglu(*gu)[0]))
    f = _mm("ffn_down", sw, full["w_ffn_down"])
    loss_part, dx1, df, dg4 = _loss_head(x1, f, tgt, vec["norm_ffn_post"])

    gbig, gsmall = {}, {"norm_ffn_post": dg4}
    gbig["w_ffn_down"] = _mm("g_ffn_down", sw, df, ta=True)

    def swiglu_back(dsw, gu):
        return jax.vjp(_f_swiglu, *gu)[1]((dsw[0],))

    dag, dau = _mm_fused("ffn_back", [df], [full["w_ffn_down"]], [BF16] * 2, tb=True, extras=[ag, au],
                         epilogue=swiglu_back)
    (dh2,) = _mm_fused("d_h2", [dag, dau], [full["w_ffn_gate"], full["w_ffn_up"]], [F32], add=True)
    gbig["w_ffn_gate"] = _mm("g_ffn_gate", dag, h2, ta=True)
    gbig["w_ffn_up"] = _mm("g_ffn_up", dau, h2, ta=True)
    (dx_res, du), (dg2, dg3) = _rowwise_vjp("post_mix_bwd", _f_post1, [(x2, D, 0), (u, D, 0)], post1_params,
                                            [[dx1], [dh2]], [True, True], [True, True], bf16_rows=(1,))
    gsmall["norm_mix_post"], gsmall["norm_ffn_pre"] = dg2, dg3
    dmerged = _mm("d_merged", du, full["w_o"], tb=True, out_dtype=BF16)
    gbig["w_o"] = _mm("g_w_o", merged, du, ta=True)
    (dpg1, dpg2, dm1, dm2), (db1, db2) = _rowwise_vjp("merge_bwd", _f_merge, merge_rows, [b1, b2], [[dmerged]],
                                                      [True] * 4, [True, True], bf16_rows=(0, 1, 2, 3))
    gsmall["b_gate"] = jnp.concatenate([db1, db2], axis=1)
    do_sb = _mm("d_o_sb", dm1, full["w_sb_out"])
    do_rw = _mm("d_o_rw", dm2, full["w_rw_out"])
    gbig["w_sb_out"] = _mm("g_sb_out", dm1, o_sb, ta=True)
    gbig["w_rw_out"] = _mm("g_rw_out", dm2, o_rw, ta=True)
    (dy_wkv, dr_a, dk2_a, dv_a, dg_), (dlnx_w, dlnx_b, dr_k) = _rowwise_vjp(
        "rw_post_bwd", _f_rwpost, post_rows, post_params, [[do_rw]], [True] * 5, [True] * 3)
    gsmall["lnx_w"], gsmall["lnx_b"], gsmall["r_k"] = dlnx_w, dlnx_b, dr_k
    dr_b, dlw, dk2_b, dv_b, dkap, da = _wkv_bwd(r_, lw_, k2_, v_, kap_, a_, states, dy_wkv, B, S)
    pre_cts = [[dr_a, dr_b], [dlw], [dk2_a, dk2_b], [dv_a, dv_b], [dkap], [da], [dg_]]
    dp_rw, dpre_params = _rw_pre_bwd(p_rw, pre_params, pre_cts, S)
    gsmall["mu_rw"] = jnp.concatenate(dpre_params[:4], axis=1)
    gsmall["w0"], gsmall["a0"], gsmall["k_k"], gsmall["k_a"] = dpre_params[4], dpre_params[6], dpre_params[9], dpre_params[10]
    glora = {"w_up": dpre_params[5][0:64].T, "a_up": dpre_params[7][64:128].T, "g_up": dpre_params[8][128:256].T}
    dq, dk, dv = _attn_bwd(p_sb, o_sb, do_sb, B, S)
    (dh1,) = _mm_fused("d_h1_sb", [dq, dk, dv], [w_sb_t[:512], w_sb_t[512:1024], w_sb_t[1024:]], [F32], add=True)
    dh1 = _mm("d_h1_rw", dp_rw, w_rw_t, acc=dh1)
    (dh1,) = _mm_fused("d_h1_gate", [dpg1, dpg2], [w_gt_t[:D], w_gt_t[D:]], [F32], add=True,
                       extras=[dh1], epilogue=lambda p, e: (p[0] + e[0],))
    gbig["w_in"] = jnp.concatenate(
        [_mm("g_in_" + tag, d, h1, ta=True)
         for tag, d in (("q", dq), ("k", dk), ("v", dv), ("rw", dp_rw), ("g1", dpg1), ("g2", dpg2))], axis=0)
    (grad_x2,), (dg1,) = _rowwise_vjp("norm_mix_pre_bwd", _f_norm, [(x2, D, 0)], [vec["norm_mix_pre"]], [[dh1]],
                                      [True], [True], add_to={0: dx_res})
    gsmall["norm_mix_pre"] = dg1
    gbig.update(glora)

    split = [gbig[n].reshape(N_CHIPS, 2, gbig[n].shape[0] // (2 * N_CHIPS), gbig[n].shape[1]) for n in BIG]
    core = jnp.stack([lax.axis_index("c"), 2 * lax.axis_index("x") + lax.axis_index("y")]).astype(jnp.int32)
    theirs = _pair_split(split)
    chip_sums = [_pair_sum("pair_sum_" + n, a, b, core) for n, a, b in zip(BIG, split, theirs)]
    landed = _chip_scatter(chip_sums)
    joined = _pair_join([_chip_sum("chip_sum_" + n, own, got, core) for n, own, got in zip(BIG, chip_sums, landed)])
    grads = {n: j.reshape(-1, j.shape[2]) for n, j in zip(BIG, joined)}

    small_local = _pack_small({n: gsmall[n] for n in SMALL}, extra_rows=1)
    loss_row = small_local.shape[0] - 1
    small_local = small_local.at[loss_row].set(loss_part[0])
    small_sum = _all_reduce_small(small_local)
    loss = small_sum[loss_row, 0]

    delta, new_m, new_v = {}, {}, {}
    unwork = lambda t, n: (t if n in ROW_SHARDED else jnp.swapaxes(t, 0, 1))[None]
    for n in BIG:
        d_, m_, v2_ = _adamw("adamw_" + n, work(W[n], n), grads[n], work(Mo[n], n), work(Vo[n], n))
        delta[n], new_m[n], new_v[n], grads[n] = (unwork(t, n) for t in (d_, m_, v2_, grads[n]))
    pk = lambda src: _pack_small({n: src[n] for n in SMALL}, extra_rows=1)
    d_s, m_s, v_s = _adamw("adamw_small", pk(W), small_sum.at[loss_row].set(0.0), pk(Mo), pk(Vo))
    for dst, packed in ((grads, small_sum), (delta, d_s), (new_m, m_s), (new_v, v_s)):
        dst.update(_unpack_small(packed, shapes))

    return (loss, grad_x2.reshape(B, S, D), *[grads[n] for n in ORDER], *[delta[n] for n in ORDER],
            *[new_m[n] for n in ORDER], *[new_v[n] for n in ORDER])
```

```python
import functools

import jax
import jax.numpy as jnp
from jax import lax
from jax.experimental import pallas as pl
from jax.experimental.pallas import tpu as pltpu

F32 = jnp.float32
BF16 = jnp.bfloat16
MESH = pl.DeviceIdType.MESH

D_MODEL = 1024
SB_HEADS = 8
HEAD_DIM = 64
SB_WIDTH = SB_HEADS * HEAD_DIM
RW_WIDTH = 512
LORA_COLS = 256
SB_COLS = 3 * SB_WIDTH
RW_COLS = 3 * RW_WIDTH + LORA_COLS
GATE_COLS = 2 * D_MODEL
D_FF = 2816
RMS_EPS = 1e-6
GN_EPS = HEAD_DIM * 1e-5
WKV_CHUNK = 64
WKV_SEQS = 4
ATTN_QUERIES = 512
ATTN_KEYS = 128
ATTN_DEAD = -120.0
LANES = 128
SUBLANES = 8
N_CHIPS = 4
N_DEV = 8

ADAM_LR = 0.001
ADAM_B1 = 0.9
ADAM_B2 = 0.999
ADAM_EPS = 1e-08
ADAM_WD = 0.01
ADAM_STEP = 10

VMEM_LIMIT = 48 * 1024 * 1024
WKV_BWD_VMEM = 58 * 1024 * 1024


def _params(sem=None, vmem=VMEM_LIMIT, **kw):
    if sem is not None:
        kw["dimension_semantics"] = sem
    return pltpu.CompilerParams(vmem_limit_bytes=vmem, **kw)


def _div_tile(dim, pref, mult=LANES):
    if dim <= pref:
        return dim
    t = pref - pref % mult
    while t >= mult:
        if dim % t == 0:
            return t
        t -= mult
    return dim


def _dot(a, b, dims):
    return lax.dot_general(a, b, (dims, ((), ())), preferred_element_type=F32)


def _mm(name, a, b, *, ta=False, tb=False, acc=None, out_dtype=F32):
    if ta:
        K, M = a.shape
    else:
        M, K = a.shape
    N = b.shape[0] if tb else b.shape[1]
    if ta:
        tm, tn, tk = _div_tile(M, 1408), _div_tile(N, 1408), _div_tile(K, 512)
    else:
        tm, tn, tk = _div_tile(M, 512), _div_tile(N, 1408), _div_tile(K, 1408)
    nk = K // tk
    dims = ((0,) if ta else (1,), (1,) if tb else (0,))
    has_acc = acc is not None

    def body(*refs):
        a_ref, b_ref = refs[0], refs[1]
        part = _dot(a_ref[...].astype(BF16), b_ref[...].astype(BF16), dims)
        if nk == 1:
            o_ref = refs[-1]
            o_ref[...] = (part + refs[2][...] if has_acc else part).astype(o_ref.dtype)
            return
        o_ref, scr = refs[-2], refs[-1]
        k = pl.program_id(2)

        @pl.when(k == 0)
        def _():
            scr[...] = part + refs[2][...] if has_acc else part

        @pl.when(k > 0)
        def _():
            scr[...] += part

        @pl.when(k == nk - 1)
        def _():
            o_ref[...] = scr[...].astype(o_ref.dtype)

    a_spec = pl.BlockSpec((tk, tm), lambda i, j, k: (k, i)) if ta else pl.BlockSpec((tm, tk), lambda i, j, k: (i, k))
    b_spec = pl.BlockSpec((tn, tk), lambda i, j, k: (j, k)) if tb else pl.BlockSpec((tk, tn), lambda i, j, k: (k, j))
    o_spec = pl.BlockSpec((tm, tn), lambda i, j, k: (i, j))
    return pl.pallas_call(
        body, name=name,
        grid=(M // tm, N // tn, nk),
        in_specs=[a_spec, b_spec] + ([o_spec] if has_acc else []),
        out_specs=o_spec,
        out_shape=jax.ShapeDtypeStruct((M, N), out_dtype),
        scratch_shapes=[pltpu.VMEM((tm, tn), F32)] if nk > 1 else [],
        compiler_params=_params(("parallel", "parallel", "arbitrary")),
    )(*([a, b] + ([acc] if has_acc else [])))


def _mm_fused(name, lhs, rhs, outs, *, tb=False, add=False, extras=(), epilogue=None):
    M, K = lhs[0].shape
    N = rhs[0].shape[0] if tb else rhs[0].shape[1]
    tm, tn, tk = _div_tile(M, 512), _div_tile(N, 1408), _div_tile(K, 1408)
    nk = K // tk
    n_l, n_e, n_o = len(lhs), len(extras), len(outs)
    n_acc = 1 if add else n_l
    dims = ((1,), (1,) if tb else (0,))

    def body(*refs):
        l_refs, r_refs = refs[:n_l], refs[n_l:2 * n_l]
        e_refs = refs[2 * n_l:2 * n_l + n_e]
        o_refs = refs[2 * n_l + n_e:2 * n_l + n_e + n_o]
        scr = refs[2 * n_l + n_e + n_o:]
        parts = [_dot(l[...].astype(BF16), r[...].astype(BF16), dims) for l, r in zip(l_refs, r_refs)]
        if add:
            parts = [functools.reduce(lambda u, v: u + v, parts)]

        def finish(vals):
            res = epilogue(vals, [e[...].astype(F32) for e in e_refs]) if epilogue else vals
            for ref, val in zip(o_refs, res):
                ref[...] = val.astype(ref.dtype)

        if nk == 1:
            finish(parts)
            return
        k = pl.program_id(2)

        @pl.when(k == 0)
        def _():
            for s, part in zip(scr, parts):
                s[...] = part

        @pl.when(k > 0)
        def _():
            for s, part in zip(scr, parts):
                s[...] += part

        @pl.when(k == nk - 1)
        def _():
            finish([s[...] for s in scr])

    a_spec = pl.BlockSpec((tm, tk), lambda i, j, k: (i, k))
    b_spec = pl.BlockSpec((tn, tk), lambda i, j, k: (j, k)) if tb else pl.BlockSpec((tk, tn), lambda i, j, k: (k, j))
    o_spec = pl.BlockSpec((tm, tn), lambda i, j, k: (i, j))
    return pl.pallas_call(
        body, name=name,
        grid=(M // tm, N // tn, nk),
        in_specs=[a_spec] * n_l + [b_spec] * n_l + [o_spec] * n_e,
        out_specs=[o_spec] * n_o,
        out_shape=[jax.ShapeDtypeStruct((M, N), dt) for dt in outs],
        scratch_shapes=[pltpu.VMEM((tm, tn), F32)] * (n_acc if nk > 1 else 0),
        compiler_params=_params(("parallel", "parallel", "arbitrary")),
    )(*lhs, *rhs, *extras)


def _row_spec(tile, width, colblk):
    return pl.BlockSpec((tile, width), lambda i: (i, colblk))


def _full_spec(shape):
    return pl.BlockSpec(shape, lambda i: (0,) * len(shape))


def _rowwise(name, fn, rows, params, outs, tile=256):
    T = rows[0][0].shape[0]
    tile = min(tile, T)
    n_r, n_p = len(rows), len(params)

    def body(*refs):
        r = [x[...].astype(F32) for x in refs[:n_r]]
        p = [x[...].astype(F32) for x in refs[n_r:n_r + n_p]]
        for o_ref, val in zip(refs[n_r + n_p:], fn(*r, *p)):
            o_ref[...] = val.astype(o_ref.dtype)

    return pl.pallas_call(
        body, name=name,
        grid=(T // tile,),
        in_specs=[_row_spec(tile, w, cb) for _, w, cb in rows] + [_full_spec(p.shape) for p in params],
        out_specs=[_row_spec(tile, w, 0) for w, _ in outs],
        out_shape=[jax.ShapeDtypeStruct((T, w), dt) for w, dt in outs],
        compiler_params=_params(("parallel",)),
    )(*([a for a, _, _ in rows] + list(params)))


def _rowwise_vjp(name, fn, rows, params, cts, need_rows, need_params, add_to=None, tile=256, bf16_rows=()):
    add_to = add_to or {}
    T = rows[0][0].shape[0]
    tile = min(tile, T)
    n_r, n_p = len(rows), len(params)
    ct_flat = [c for group in cts for c in group]
    ct_sizes = [len(group) for group in cts]
    add_idx = sorted(add_to)
    row_out = [i for i in range(n_r) if need_rows[i]]
    par_out = [i for i in range(n_p) if need_params[i]]
    n_ct, n_add = len(ct_flat), len(add_idx)

    def body(*refs):
        pos = 0
        r = [x[...].astype(F32) for x in refs[pos:pos + n_r]]
        pos += n_r
        p = [x[...].astype(F32) for x in refs[pos:pos + n_p]]
        pos += n_p
        ct_vals = [x[...].astype(F32) for x in refs[pos:pos + n_ct]]
        pos += n_ct
        adds = {i: x[...] for i, x in zip(add_idx, refs[pos:pos + n_add])}
        pos += n_add
        drow_refs = refs[pos:pos + len(row_out)]
        pos += len(row_out)
        dpar_refs = refs[pos:pos + len(par_out)]
        ct_in, q = [], 0
        for n in ct_sizes:
            ct_in.append(functools.reduce(lambda u, v: u + v, ct_vals[q:q + n]))
            q += n
        _, vjp = jax.vjp(fn, *r, *p)
        grads = vjp(tuple(ct_in))
        for ref, i in zip(drow_refs, row_out):
            g = grads[i]
            ref[...] = (g + adds[i] if i in adds else g).astype(ref.dtype)

        @pl.when(pl.program_id(0) == 0)
        def _():
            for ref in dpar_refs:
                ref[...] = jnp.zeros_like(ref)

        for ref, i in zip(dpar_refs, par_out):
            ref[...] += grads[n_r + i]

    ct_widths = [c.shape[1] for c in ct_flat]
    in_specs = ([_row_spec(tile, w, cb) for _, w, cb in rows] + [_full_spec(p.shape) for p in params]
                + [_row_spec(tile, w, 0) for w in ct_widths] + [_row_spec(tile, rows[i][1], 0) for i in add_idx])
    out_specs = [_row_spec(tile, rows[i][1], 0) for i in row_out] + [_full_spec(params[i].shape) for i in par_out]
    out_shape = ([jax.ShapeDtypeStruct((T, rows[i][1]), BF16 if i in bf16_rows else F32) for i in row_out]
                 + [jax.ShapeDtypeStruct(params[i].shape, F32) for i in par_out])
    res = pl.pallas_call(
        body, name=name,
        grid=(T // tile,),
        in_specs=in_specs, out_specs=out_specs, out_shape=out_shape,
        compiler_params=_params(("arbitrary",)),
    )(*([a for a, _, _ in rows] + list(params) + ct_flat + [add_to[i] for i in add_idx]))
    return res[:len(row_out)], res[len(row_out):]


def _sigmoid(x):
    return 0.5 * (jnp.tanh(0.5 * x) + 1.0)


def _softplus(x):
    return jnp.maximum(x, 0.0) + jnp.log(1.0 + jnp.exp(-jnp.abs(x)))


def _rms(x, g):
    return x * lax.rsqrt(jnp.mean(x * x, axis=-1, keepdims=True) + RMS_EPS) * g


def _segsum_impl(x):
    n, w = x.shape[-1], 2 * LANES
    r = lax.shift_right_logical(lax.broadcasted_iota(jnp.int32, (w, w), 0), 6)
    c = lax.shift_right_logical(lax.broadcasted_iota(jnp.int32, (w, w), 1), 6)
    bd = (r == c).astype(BF16)
    hi = x.astype(BF16)
    rest = x - hi.astype(F32)
    mid = rest.astype(BF16)
    lo = (rest - mid.astype(F32)).astype(BF16)
    nn = ((1,), (0,))
    blocks = [_dot(hi[:, j:j + w], bd, nn) + _dot(mid[:, j:j + w], bd, nn) + _dot(lo[:, j:j + w], bd, nn)
              for j in range(0, n, w)]
    return jnp.concatenate(blocks, axis=1)


@jax.custom_vjp
def _segsum(x):
    return _segsum_impl(x)


_segsum.defvjp(lambda x: (_segsum_impl(x), None), lambda _, g: (_segsum_impl(g),))


@jax.custom_vjp
def _mmb(a, w):
    return _dot(a.astype(BF16), w.astype(BF16), ((1,), (0,)))


def _mmb_fwd(a, w):
    return _mmb(a, w), (a, w)


def _mmb_bwd(res, g):
    a, w = res
    gb = g.astype(BF16)
    return _dot(gb, w.astype(BF16), ((1,), (1,))), _dot(a.astype(BF16), gb, ((0,), (0,)))


_mmb.defvjp(_mmb_fwd, _mmb_bwd)


def _f_norm(x, g):
    return (_rms(x, g),)


def _f_post1(x, u, g2, g3):
    x1 = x + _rms(u, g2)
    return x1, _rms(x1, g3)


def _f_swiglu(ag, au):
    return (ag * _sigmoid(ag) * au,)


def _f_merge(pg1, pg2, m1, m2, b1, b2):
    return (_sigmoid(pg1 + b1) * m1 + _sigmoid(pg2 + b2) * m2,)


def _f_out(x1, f, g4):
    return (x1 + _rms(f, g4),)


def _f_rwpre(pr, pk, pv, pz, qr, qk, qv, qz, mur, muk, muv, muz, w0, wup, a0, aup, gup, k_k, k_a):
    r = pr + (qr - pr) * mur
    k = pk + (qk - pk) * muk
    v = pv + (qv - pv) * muv
    z = pz + (qz - pz) * muz
    w_raw = w0 + _mmb(jnp.tanh(z), wup)
    lw = -jnp.exp(-_softplus(-w_raw) - 0.5)
    a = _sigmoid(a0 + _mmb(z, aup))
    g = _mmb(_sigmoid(z), gup)
    kk = k * k_k
    kap = kk * lax.rsqrt(jnp.maximum(_segsum(kk * kk), 1e-24))
    k2 = k * (1.0 + (a - 1.0) * k_a)
    return r, lw, k2, v, kap, a, g


def _f_rwpost(y, r, k2, v, g, lnx_w, lnx_b, r_k):
    inv = 1.0 / HEAD_DIM
    yc = y - _segsum(y) * inv
    var = _segsum(yc * yc) * inv
    yn = yc * lax.rsqrt(var + GN_EPS) * lnx_w + lnx_b
    bonus = _segsum(r * k2 * r_k) * v
    return ((yn + bonus) * g,)


RW_GROUPS = (0, 512, 1024, 1536, RW_COLS)


def _column_groups(p):
    return [p[:, a:b] for a, b in zip(RW_GROUPS[:-1], RW_GROUPS[1:])]


def _previous_tokens(p, halo, first_of_sequence):
    rows = lax.broadcasted_iota(jnp.int32, (p.shape[0], 1), 0)
    before = jnp.where(first_of_sequence, 0.0, halo[SUBLANES - 1:SUBLANES, :])
    return jnp.where(rows == 0, before, pltpu.roll(p, 1, axis=0))


def _halo_spec(tile, order):
    per = tile // SUBLANES
    return pl.BlockSpec((SUBLANES, RW_COLS), lambda i: (jnp.maximum(order(i) * per - 1, 0), 0))


def _rw_pre(p_rw, params, S, tile=128):
    T = p_rw.shape[0]
    tile = min(tile, T)
    assert S % tile == 0
    n_p = len(params)

    def body(*refs):
        p_ref, halo_ref = refs[0], refs[1]
        par = [x[...].astype(F32) for x in refs[2:2 + n_p]]
        p = p_ref[...]
        first = lax.rem(pl.program_id(0) * tile, S) == 0
        prev = _previous_tokens(p, halo_ref[...], first)
        for o_ref, val in zip(refs[2 + n_p:], _f_rwpre(*_column_groups(p), *_column_groups(prev), *par)):
            o_ref[...] = val

    out_spec = pl.BlockSpec((tile, RW_WIDTH), lambda i: (i, 0))
    return pl.pallas_call(
        body, name="rw_pre",
        grid=(T // tile,),
        in_specs=[pl.BlockSpec((tile, RW_COLS), lambda i: (i, 0)), _halo_spec(tile, lambda i: i)]
                 + [_full_spec(q.shape) for q in params],
        out_specs=[out_spec] * 7,
        out_shape=[jax.ShapeDtypeStruct((T, RW_WIDTH), F32)] * 7,
        compiler_params=_params(("parallel",)),
    )(p_rw, p_rw, *params)


def _rw_pre_bwd(p_rw, params, cts, S, tile=128):
    T = p_rw.shape[0]
    tile = min(tile, T)
    assert S % tile == 0
    nt = T // tile
    n_p = len(params)
    ct_flat = [c for group in cts for c in group]
    ct_sizes = [len(group) for group in cts]
    n_ct = len(ct_flat)

    def body(*refs):
        p_ref, halo_ref = refs[0], refs[1]
        par = [x[...].astype(F32) for x in refs[2:2 + n_p]]
        ct_vals = [x[...] for x in refs[2 + n_p:2 + n_p + n_ct]]
        dp_ref = refs[2 + n_p + n_ct]
        dpar_refs = refs[3 + n_p + n_ct:3 + 2 * n_p + n_ct]
        carry = refs[-1]
        step = pl.program_id(0)

        @pl.when(step == 0)
        def _():
            carry[...] = jnp.zeros_like(carry)
            for ref in dpar_refs:
                ref[...] = jnp.zeros_like(ref)

        ct_in, q = [], 0
        for n in ct_sizes:
            ct_in.append(functools.reduce(lambda u, v: u + v, ct_vals[q:q + n]))
            q += n
        p = p_ref[...]
        first = lax.rem((nt - 1 - step) * tile, S) == 0
        prev = _previous_tokens(p, halo_ref[...], first)
        _, vjp = jax.vjp(_f_rwpre, *_column_groups(p), *_column_groups(prev), *par)
        grads = vjp(tuple(ct_in))
        d_here = jnp.concatenate(grads[0:4], axis=1)
        d_prev = jnp.concatenate(grads[4:8], axis=1)
        rows = lax.broadcasted_iota(jnp.int32, (tile, 1), 0)
        from_next = jnp.where(rows == tile - 1, carry[0:1, :], pltpu.roll(d_prev, tile - 1, axis=0))
        dp_ref[...] = (d_here + from_next).astype(dp_ref.dtype)
        carry[...] = jnp.broadcast_to(jnp.where(first, 0.0, d_prev[0:1, :]), carry.shape)
        for ref, g in zip(dpar_refs, grads[8:]):
            ref[...] += g

    back = lambda i: nt - 1 - i
    row = lambda w: pl.BlockSpec((tile, w), lambda i: (back(i), 0))
    res = pl.pallas_call(
        body, name="rw_pre_bwd",
        grid=(nt,),
        in_specs=[row(RW_COLS), _halo_spec(tile, back)] + [_full_spec(q.shape) for q in params]
                 + [row(RW_WIDTH)] * n_ct,
        out_specs=[row(RW_COLS)] + [_full_spec(q.shape) for q in params],
        out_shape=[jax.ShapeDtypeStruct((T, RW_COLS), BF16)] + [jax.ShapeDtypeStruct(q.shape, F32) for q in params],
        scratch_shapes=[pltpu.VMEM((SUBLANES, RW_COLS), F32)],
        compiler_params=_params(("arbitrary",)),
    )(p_rw, p_rw, *params, *ct_flat)
    return res[0], res[1:]


def _loss_head(x1, f, target, g4, tile=256):
    T, D = x1.shape
    tile = min(tile, T)

    def body(x1_ref, f_ref, t_ref, g_ref, loss_ref, dx1_ref, df_ref, dg_ref):
        (y,), vjp = jax.vjp(_f_out, x1_ref[...], f_ref[...], g_ref[...])
        err = y - t_ref[...]
        dx1, df, dg = vjp((err * (1.0 / D),))
        dx1_ref[...] = dx1
        df_ref[...] = df.astype(df_ref.dtype)

        @pl.when(pl.program_id(0) == 0)
        def _():
            loss_ref[...] = jnp.zeros_like(loss_ref)
            dg_ref[...] = jnp.zeros_like(dg_ref)

        part = jnp.sum(jnp.sum(err * err, axis=1, keepdims=True), axis=0, keepdims=True) * (0.5 / D)
        loss_ref[...] += jnp.broadcast_to(part, loss_ref.shape)
        dg_ref[...] += dg

    row = pl.BlockSpec((tile, D), lambda i: (i, 0))
    return pl.pallas_call(
        body, name="loss_head",
        grid=(T // tile,),
        in_specs=[row, row, row, _full_spec(g4.shape)],
        out_specs=[_full_spec((SUBLANES, LANES)), row, row, _full_spec(g4.shape)],
        out_shape=[jax.ShapeDtypeStruct((SUBLANES, LANES), F32), jax.ShapeDtypeStruct((T, D), F32),
                   jax.ShapeDtypeStruct((T, D), BF16), jax.ShapeDtypeStruct(g4.shape, F32)],
        compiler_params=_params(("arbitrary",)),
    )(x1, f, target, g4)


def _nn(a, b):
    return _dot(a, b, ((1,), (0,)))


def _nt(a, b):
    return _dot(a, b, ((1,), (1,)))


def _tn(a, b):
    return _dot(a, b, ((0,), (0,)))


def _split_dot(x, u2):
    hi = x.astype(BF16)
    lo = (x - hi.astype(F32)).astype(BF16)
    return _nn(jnp.concatenate([hi, lo], axis=1), u2)


def _by_head(x, masks):
    return jnp.concatenate([(x * m).astype(BF16) for m in masks], axis=0)


def _fold_heads(x2, masks):
    R = x2.shape[0] // len(masks)
    return functools.reduce(lambda u, v: u + v, [x2[h * R:(h + 1) * R] * m for h, m in enumerate(masks)])


def _head_masks():
    lane = lax.broadcasted_iota(jnp.int32, (1, LANES), 1)
    return [((lane >= h * HEAD_DIM) & (lane < (h + 1) * HEAD_DIM)).astype(F32) for h in range(LANES // HEAD_DIM)]


def _key_tri(op):
    row = lax.broadcasted_iota(jnp.int32, (ATTN_KEYS, ATTN_KEYS), 0)
    col = lax.broadcasted_iota(jnp.int32, (ATTN_KEYS, ATTN_KEYS), 1)
    u = op(row, col).astype(BF16)
    return jnp.concatenate([u, u], axis=0)


def _causal(rows):
    row = lax.broadcasted_iota(jnp.int32, (rows, ATTN_KEYS), 0)
    col = lax.broadcasted_iota(jnp.int32, (rows, ATTN_KEYS), 1)
    return col < row


def _from_row(tree, r):
    return jax.tree.map(lambda x: x[r:], tree)


def _onto_rows(old, new, r):
    return jax.tree.map(lambda o, n: jnp.concatenate([o[:r], n], axis=0) if r else n, old, new)


def _sb_weights(qb16, kbh, c_fails, u_gt, strict):
    z_all = _nt(qb16, kbh)
    zs = [z_all[:, h * ATTN_KEYS:(h + 1) * ATTN_KEYS] for h in range(len(c_fails))]
    Ls = [jnp.minimum(-z, 0.0) - jnp.log(1.0 + jnp.exp(-jnp.abs(z))) for z in zs]
    Lms = Ls if strict is None else [jnp.where(strict, L, 0.0) for L in Ls]
    cums = [_split_dot(Lm, u_gt) for Lm in Lms]
    As = [jnp.exp(z + L + c + cum) for z, L, c, cum in zip(zs, Ls, c_fails, cums)]
    if strict is not None:
        As = [jnp.where(strict, A, 0.0) for A in As]
    return zs, Ls, Lms, As


def _attn_specs(S, qb):
    nq = S // qb
    q_spec = pl.BlockSpec((qb, LANES), lambda b, p, i: (b * nq + i, p))
    k_spec = pl.BlockSpec((S, LANES), lambda b, p, i: (b, SB_WIDTH // LANES + p))
    v_spec = pl.BlockSpec((S, LANES), lambda b, p, i: (b, 2 * SB_WIDTH // LANES + p))
    seq = pl.BlockSpec((S, LANES), lambda b, p, i: (b, p))
    return q_spec, k_spec, v_spec, q_spec, seq


def _key_walk(i, qb, block, carry, fails):
    per = qb // ATTN_KEYS
    for sub in reversed(range(per)):
        carry = block(i * per + sub, carry, sub * ATTN_KEYS)
    n = i * per

    def alive(c):
        return jnp.max(functools.reduce(jnp.maximum, fails(c))) > ATTN_DEAD

    def cond(state):
        return jnp.logical_and(state[0] < n, state[1])

    def body(state):
        c = block(n - 1 - state[0], state[2], None)
        return state[0] + 1, alive(c), c

    return lax.while_loop(cond, body, (jnp.int32(0), alive(carry), carry))[2]


def _attn_fwd(proj, B, S):
    qb = min(ATTN_QUERIES, S)
    scale = HEAD_DIM ** -0.5

    def body(q_ref, k_ref, v_ref, o_ref):
        i = pl.program_id(2)
        masks = _head_masks()
        u_gt = _key_tri(lambda r, c: r > c)
        q16 = (q_ref[...] * scale).astype(BF16)

        def block(J, carry, row0):
            r0 = pl.multiple_of(J * ATTN_KEYS, ATTN_KEYS)
            kbh = _by_head(k_ref[pl.ds(r0, ATTN_KEYS), :], masks)
            vbh = _by_head(v_ref[pl.ds(r0, ATTN_KEYS), :], masks)
            lo = row0 or 0
            strict = None if row0 is None else _causal(qb - lo)
            acc, cs = _from_row(carry, lo)
            _, _, Lms, As = _sb_weights(q16[lo:], kbh, cs, u_gt, strict)
            acc = acc + _nn(jnp.concatenate([A.astype(BF16) for A in As], axis=1), vbh)
            cs = tuple(c + jnp.sum(Lm, axis=1, keepdims=True) for c, Lm in zip(cs, Lms))
            return _onto_rows(carry, (acc, cs), lo)

        zero_c = tuple(jnp.zeros((qb, 1), F32) for _ in masks)
        carry = _key_walk(i, qb, block, (jnp.zeros((qb, LANES), F32), zero_c), lambda c: c[1])
        o_ref[...] = carry[0]

    q_spec, k_spec, v_spec, blk, _ = _attn_specs(S, qb)
    return pl.pallas_call(
        body, name="sb_attn_fwd",
        grid=(B, SB_WIDTH // LANES, S // qb),
        in_specs=[q_spec, k_spec, v_spec],
        out_specs=blk,
        out_shape=jax.ShapeDtypeStruct((B * S, SB_WIDTH), F32),
        compiler_params=_params(("parallel", "parallel", "arbitrary")),
    )(proj, proj, proj)


def _attn_bwd(proj, o, do, B, S):
    qb = min(ATTN_QUERIES, S)
    nq = S // qb
    scale = HEAD_DIM ** -0.5

    def body(q_ref, k_ref, v_ref, o_ref, do_ref, dq_ref, dk_out, dv_out, dk_ref, dv_ref):
        i = pl.program_id(2)

        @pl.when(i == 0)
        def _():
            dk_ref[...] = jnp.zeros_like(dk_ref)
            dv_ref[...] = jnp.zeros_like(dv_ref)

        masks = _head_masks()
        u_gt = _key_tri(lambda r, c: r > c)
        u_ge = _key_tri(lambda r, c: r >= c)
        heads = range(len(masks))
        q16 = (q_ref[...] * scale).astype(BF16)
        do16 = do_ref[...].astype(BF16)
        od = o_ref[...] * do16.astype(F32)
        totals = tuple(jnp.sum(od * m, axis=1, keepdims=True) for m in masks)

        def block(J, carry, row0):
            r0 = pl.multiple_of(J * ATTN_KEYS, ATTN_KEYS)
            kbh = _by_head(k_ref[pl.ds(r0, ATTN_KEYS), :], masks)
            vbh = _by_head(v_ref[pl.ds(r0, ATTN_KEYS), :], masks)
            lo = row0 or 0
            strict = None if row0 is None else _causal(qb - lo)
            dq, c_fail, c_p = _from_row(carry, lo)
            tot = _from_row(totals, lo)
            zs, Ls, Lms, As = _sb_weights(q16[lo:], kbh, c_fail, u_gt, strict)
            Abs = [A.astype(BF16) for A in As]
            dA_all = _nt(do16[lo:], vbh)
            Ps = [Abs[h].astype(F32) * dA_all[:, h * ATTN_KEYS:(h + 1) * ATTN_KEYS] for h in heads]
            afters = [c_p[h] + _split_dot(Ps[h], u_ge) for h in heads]
            sigs = [jnp.exp(zs[h] + Ls[h]) for h in heads]
            dzs = [Ps[h] * (1.0 - sigs[h]) - sigs[h] * (tot[h] - afters[h]) for h in heads]
            if strict is not None:
                dzs = [jnp.where(strict, dz, 0.0) for dz in dzs]
            dz_all = jnp.concatenate([dz.astype(BF16) for dz in dzs], axis=1)
            dv_ref[pl.ds(r0, ATTN_KEYS), :] += _fold_heads(_tn(jnp.concatenate(Abs, axis=1), do16[lo:]), masks)
            dk_ref[pl.ds(r0, ATTN_KEYS), :] += _fold_heads(_tn(dz_all, q16[lo:]), masks)
            dq = dq + _nn(dz_all, kbh)
            c_fail = tuple(c_fail[h] + jnp.sum(Lms[h], axis=1, keepdims=True) for h in heads)
            c_p = tuple(c_p[h] + jnp.sum(Ps[h], axis=1, keepdims=True) for h in heads)
            return _onto_rows(carry, (dq, c_fail, c_p), lo)

        zc = tuple(jnp.zeros((qb, 1), F32) for _ in masks)
        carry = _key_walk(i, qb, block, (jnp.zeros((qb, LANES), F32), zc, zc), lambda c: c[1])
        dq_ref[...] = (carry[0] * scale).astype(dq_ref.dtype)

        @pl.when(i == nq - 1)
        def _():
            dk_out[...] = dk_ref[...].astype(dk_out.dtype)
            dv_out[...] = dv_ref[...].astype(dv_out.dtype)

    q_spec, k_spec, v_spec, blk, seq = _attn_specs(S, qb)
    return pl.pallas_call(
        body, name="sb_attn_bwd",
        grid=(B, SB_WIDTH // LANES, nq),
        in_specs=[q_spec, k_spec, v_spec, blk, blk],
        out_specs=[blk, seq, seq],
        out_shape=[jax.ShapeDtypeStruct((B * S, SB_WIDTH), BF16)] * 3,
        scratch_shapes=[pltpu.VMEM((S, LANES), F32), pltpu.VMEM((S, LANES), F32)],
        compiler_params=_params(("parallel", "parallel", "arbitrary")),
    )(proj, proj, proj, o, do)


_BATCHED = {"nn": "gmk,gkn->gmn", "nt": "gmk,gnk->gmn", "tn": "gkm,gkn->gmn"}


def _bdot_raw(a, b, kind, passes):
    e = functools.partial(jnp.einsum, _BATCHED[kind], preferred_element_type=F32)
    ah, bh = a.astype(BF16), b.astype(BF16)
    if passes == 1:
        return e(ah, bh)
    al, bl = (a - ah.astype(F32)).astype(BF16), (b - bh.astype(F32)).astype(BF16)
    return e(ah, bh) + e(ah, bl) + e(al, bh)


@functools.partial(jax.custom_vjp, nondiff_argnums=(2, 3))
def _bdot(a, b, kind, passes):
    return _bdot_raw(a, b, kind, passes)


def _bdot_fwd(a, b, kind, passes):
    return _bdot_raw(a, b, kind, passes), (a, b)


def _bdot_bwd(kind, passes, res, g):
    a, b = res
    if kind == "nn":
        return _bdot_raw(g, b, "nt", passes), _bdot_raw(a, g, "tn", passes)
    if kind == "nt":
        return _bdot_raw(g, b, "nn", passes), _bdot_raw(g, a, "tn", passes)
    return _bdot_raw(b, g, "nt", passes), _bdot_raw(a, g, "nn", passes)


_bdot.defvjp(_bdot_fwd, _bdot_bwd)


def _solve_powers(m):
    powers = [m]
    for _ in range(max(1, (m.shape[1] - 1).bit_length()) - 1):
        powers.append(_bdot_raw(powers[-1], powers[-1], "nn", 1))
    return powers


def _solve_fwd(m, rhs):
    powers = _solve_powers(m)
    x = rhs
    for p in powers:
        x = x + _bdot_raw(p, x, "nn", 1)
    return x, (powers, x)


def _solve_bwd(res, g):
    powers, x = res
    for p in powers:
        g = g + _bdot_raw(p, g, "tn", 1)
    return _bdot_raw(g, x, "nt", 1), g


@jax.custom_vjp
def _unit_lower_solve(m, rhs):
    return _solve_fwd(m, rhs)[0]


_unit_lower_solve.defvjp(_solve_fwd, _solve_bwd)


def _wkv_chunk(S0, r, lw, k, v, kap, a):
    G, C, N = r.shape
    row = lax.broadcasted_iota(jnp.int32, (C, C), 0)
    col = lax.broadcasted_iota(jnp.int32, (C, C), 1)
    incl = (col <= row).astype(F32)
    strict = (col < row).astype(F32)
    cum = _bdot(jnp.broadcast_to(incl, (G, C, C)), lw, "nn", 3)
    e_pos = jnp.exp(cum)
    e_neg = jnp.exp(-cum)
    al = -kap * jnp.exp(cum - lw)
    be = kap * a * e_neg
    kt = k * e_neg
    rt = r * e_pos
    bk = jnp.concatenate([be, kt], axis=1)
    mask = jnp.concatenate([jnp.concatenate([strict, strict], axis=1), jnp.concatenate([incl, incl], axis=1)], axis=0)
    m_all = _bdot(jnp.concatenate([al, rt], axis=1), bk, "nt", 3) * mask
    m_ab, m_ak = m_all[:, :C, :C], m_all[:, :C, C:]
    m_rb, m_rk = m_all[:, C:, :C], m_all[:, C:, C:]
    S0t = jnp.swapaxes(S0, 1, 2)
    sa = _unit_lower_solve(m_ab, _bdot(jnp.concatenate([al, m_ak], axis=2), jnp.concatenate([S0t, v], axis=1), "nn", 3))
    y =_bdot(jnp.concatenate([rt, m_rb, m_rk], axis=2), jnp.concatenate([S0t, sa, v], axis=1), "nn", 3)
    S1 = (S0 + _bdot(jnp.concatenate([sa, v], axis=1), bk, "tn", 1)) * e_pos[:, C - 1:C, :]
    return y, S1


def _split_heads(x):
    return jnp.stack([x[:, h * HEAD_DIM:(h + 1) * HEAD_DIM] for h in range(x.shape[1] // HEAD_DIM)], axis=0)


def _merge_heads(x):
    return jnp.concatenate([x[h] for h in range(x.shape[0])], axis=1)


def _seq_heads(ref):
    return jnp.concatenate([_split_heads(ref[s]) for s in range(ref.shape[0])], axis=0)


def _store_seq_heads(ref, x):
    heads = x.shape[0] // ref.shape[0]
    for s in range(ref.shape[0]):
        ref[s] = _merge_heads(x[s * heads:(s + 1) * heads])


def _wkv_fwd(r, lw, k, v, kap, a, B, S):
    C, H, N = WKV_CHUNK, RW_WIDTH // HEAD_DIM, HEAD_DIM
    nc = S // C
    Q = min(WKV_SEQS, B)

    def body(r_ref, lw_ref, k_ref, v_ref, kap_ref, a_ref, y_ref, st_ref, s_scr):
        @pl.when(pl.program_id(1) == 0)
        def _():
            s_scr[...] = jnp.zeros_like(s_scr)

        S0 = s_scr[...]
        for s in range(Q):
            st_ref[s, 0] = S0[s * H:(s + 1) * H]
        args = [_seq_heads(ref) for ref in (r_ref, lw_ref, k_ref, v_ref, kap_ref, a_ref)]
        y, S1 = _wkv_chunk(S0, *args)
        s_scr[...] = S1
        _store_seq_heads(y_ref, y)

    row_spec = pl.BlockSpec((Q, C, RW_WIDTH), lambda b, c: (b, c, 0))
    seqs = lambda t: t.reshape(B, S, RW_WIDTH)
    y, states = pl.pallas_call(
        body, name="wkv_fwd",
        grid=(B // Q, nc),
        in_specs=[row_spec] * 6,
        out_specs=[row_spec, pl.BlockSpec((Q, 1, H, N, N), lambda b, c: (b, c, 0, 0, 0))],
        out_shape=[jax.ShapeDtypeStruct((B, S, RW_WIDTH), F32), jax.ShapeDtypeStruct((B, nc, H, N, N), F32)],
        scratch_shapes=[pltpu.VMEM((Q * H, N, N), F32)],
        compiler_params=_params(("arbitrary", "arbitrary")),
    )(*map(seqs, (r, lw, k, v, kap, a)))
    return y.reshape(B * S, RW_WIDTH), states


def _wkv_bwd(r, lw, k, v, kap, a, states, dy, B, S):
    C, H, N = WKV_CHUNK, RW_WIDTH // HEAD_DIM, HEAD_DIM
    nc = S // C
    Q = min(WKV_SEQS, B)

    def body(r_ref, lw_ref, k_ref, v_ref, kap_ref, a_ref, st_ref, dy_ref,
             dr_ref, dlw_ref, dk_ref, dv_ref, dkap_ref, da_ref, ds_scr):
        @pl.when(pl.program_id(1) == 0)
        def _():
            ds_scr[...] = jnp.zeros_like(ds_scr)

        args = [_seq_heads(ref) for ref in (r_ref, lw_ref, k_ref, v_ref, kap_ref, a_ref)]
        S0 = jnp.concatenate([st_ref[s, 0] for s in range(Q)], axis=0)
        _, vjp = jax.vjp(_wkv_chunk, S0, *args)
        g = vjp((_seq_heads(dy_ref), ds_scr[...]))
        ds_scr[...] = g[0]
        for ref, gv in zip((dr_ref, dlw_ref, dk_ref, dv_ref, dkap_ref, da_ref), g[1:]):
            _store_seq_heads(ref, gv)

    row_spec = pl.BlockSpec((Q, C, RW_WIDTH), lambda b, c: (b, nc - 1 - c, 0))
    st_spec = pl.BlockSpec((Q, 1, H, N, N), lambda b, c: (b, nc - 1 - c, 0, 0, 0))
    seqs = lambda t: t.reshape(B, S, RW_WIDTH)
    res = pl.pallas_call(
        body, name="wkv_bwd",
        grid=(B // Q, nc),
        in_specs=[row_spec] * 6 + [st_spec, row_spec],
        out_specs=[row_spec] * 6,
        out_shape=[jax.ShapeDtypeStruct((B, S, RW_WIDTH), F32)] * 6,
        scratch_shapes=[pltpu.VMEM((Q * H, N, N), F32)],
        compiler_params=_params(("arbitrary", "arbitrary"), vmem=WKV_BWD_VMEM),
    )(*map(seqs, (r, lw, k, v, kap, a)), states, seqs(dy))
    return [t.reshape(B * S, RW_WIDTH) for t in res]


HBM = pl.BlockSpec(memory_space=pl.ANY)


def _place():
    return lax.axis_index("x"), lax.axis_index("y"), lax.axis_index("c")


def _other_chips(x, y):
    return [(1 - x, y), (x, 1 - y), (1 - x, 1 - y)]


def _all_gather_chips(shards):
    n = len(shards)

    def body(*refs):
        ins, outs = refs[:n], refs[n:2 * n]
        ici_send, ici_recv, d2d_send, d2d_recv, local = refs[2 * n:]
        x, y, c = _place()
        me = 2 * x + y
        sib = (x, y, 1 - c)
        chips = _other_chips(x, y)
        started, copies = [], []
        for w in range(n):
            cp = pltpu.make_async_copy(ins[w].at[c], outs[w].at[me, c], local.at[w])
            cp.start()
            copies.append(cp)
            for j, (px, py) in enumerate(chips):
                rd = pltpu.make_async_remote_copy(
                    src_ref=ins[w].at[c], dst_ref=outs[w].at[me, c], send_sem=ici_send.at[3 * w + j],
                    recv_sem=ici_recv.at[3 * w + j], device_id=(px, py, c), device_id_type=MESH)
                rd.start()
                started.append(rd)
            rd = pltpu.make_async_remote_copy(
                src_ref=ins[w].at[c], dst_ref=outs[w].at[me, c], send_sem=d2d_send.at[4 * w + 3],
                recv_sem=d2d_recv.at[4 * w + 3], device_id=sib, device_id_type=MESH)
            rd.start()
            started.append(rd)
        for w in range(n):
            for j, (px, py) in enumerate(chips):
                src = 2 * px + py
                pltpu.make_async_remote_copy(
                    src_ref=ins[w].at[c], dst_ref=outs[w].at[src, c], send_sem=ici_send.at[3 * w + j],
                    recv_sem=ici_recv.at[3 * w + j], device_id=(px, py, c), device_id_type=MESH).wait_recv()
                rd = pltpu.make_async_remote_copy(
                    src_ref=outs[w].at[src, c], dst_ref=outs[w].at[src, c], send_sem=d2d_send.at[4 * w + j],
                    recv_sem=d2d_recv.at[4 * w + j], device_id=sib, device_id_type=MESH)
                rd.start()
                started.append(rd)
        for w in range(n):
            for j, (px, py) in enumerate(chips):
                pltpu.make_async_remote_copy(
                    src_ref=ins[w].at[c], dst_ref=outs[w].at[2 * px + py, 1 - c], send_sem=d2d_send.at[4 * w + j],
                    recv_sem=d2d_recv.at[4 * w + j], device_id=sib, device_id_type=MESH).wait_recv()
            pltpu.make_async_remote_copy(
                src_ref=ins[w].at[c], dst_ref=outs[w].at[me, 1 - c], send_sem=d2d_send.at[4 * w + 3],
                recv_sem=d2d_recv.at[4 * w + 3], device_id=sib, device_id_type=MESH).wait_recv()
        for rd in started:
            rd.wait_send()
        for cp in copies:
            cp.wait()

    return pl.pallas_call(
        body, name="gather_weights",
        in_specs=[HBM] * n, out_specs=[HBM] * n,
        out_shape=[jax.ShapeDtypeStruct((N_CHIPS,) + s.shape, s.dtype) for s in shards],
        scratch_shapes=[pltpu.SemaphoreType.DMA((3 * n,)), pltpu.SemaphoreType.DMA((3 * n,)),
                        pltpu.SemaphoreType.DMA((4 * n,)), pltpu.SemaphoreType.DMA((4 * n,)),
                        pltpu.SemaphoreType.DMA((n,))],
        compiler_params=pltpu.CompilerParams(has_side_effects=True),
    )(*shards)


def _pair_split(grads):
    n = len(grads)

    def body(*refs):
        ins, theirs = refs[:n], refs[n:2 * n]
        send, recv = refs[2 * n:]
        x, y, c = _place()
        sib = (x, y, 1 - c)
        rds = []
        for w in range(n):
            rd = pltpu.make_async_remote_copy(
                src_ref=ins[w].at[:, 1 - c], dst_ref=theirs[w], send_sem=send.at[w], recv_sem=recv.at[w],
                device_id=sib, device_id_type=MESH)
            rd.start()
            rds.append(rd)
        for rd in rds:
            rd.wait_recv()
        for rd in rds:
            rd.wait_send()

    return pl.pallas_call(
        body, name="grad_pair_split",
        in_specs=[HBM] * n, out_specs=[HBM] * n,
        out_shape=[jax.ShapeDtypeStruct((g.shape[0],) + g.shape[2:], g.dtype) for g in grads],
        scratch_shapes=[pltpu.SemaphoreType.DMA((n,)), pltpu.SemaphoreType.DMA((n,))],
        compiler_params=pltpu.CompilerParams(has_side_effects=True),
    )(*grads)


def _chip_scatter(parts):
    n = len(parts)

    def body(*refs):
        ins, outs = refs[:n], refs[n:2 * n]
        send, recv = refs[2 * n:]
        x, y, c = _place()
        me = 2 * x + y
        rds = []
        for w in range(n):
            for j, (px, py) in enumerate(_other_chips(x, y)):
                s = 3 * w + j
                rd = pltpu.make_async_remote_copy(
                    src_ref=ins[w].at[2 * px + py], dst_ref=outs[w].at[j], send_sem=send.at[s], recv_sem=recv.at[s],
                    device_id=(px, py, c), device_id_type=MESH)
                rd.start()
                rds.append(rd)
        for w in range(n):
            for j, (px, py) in enumerate(_other_chips(x, y)):
                s = 3 * w + j
                pltpu.make_async_remote_copy(
                    src_ref=ins[w].at[me], dst_ref=outs[w].at[j], send_sem=send.at[s], recv_sem=recv.at[s],
                    device_id=(px, py, c), device_id_type=MESH).wait_recv()
        for rd in rds:
            rd.wait_send()

    return pl.pallas_call(
        body, name="grad_chip_scatter",
        in_specs=[HBM] * n, out_specs=[HBM] * n,
        out_shape=[jax.ShapeDtypeStruct((N_CHIPS - 1,) + p.shape[1:], p.dtype) for p in parts],
        scratch_shapes=[pltpu.SemaphoreType.DMA((3 * n,)), pltpu.SemaphoreType.DMA((3 * n,))],
        compiler_params=pltpu.CompilerParams(has_side_effects=True),
    )(*parts)


def _pair_join(bufs):
    n = len(bufs)

    def body(*refs):
        ins, outs = refs[:n], refs[n:2 * n]
        send, recv = refs[2 * n:]
        x, y, c = _place()
        sib = (x, y, 1 - c)
        rds = []
        for w in range(n):
            rd = pltpu.make_async_remote_copy(
                src_ref=ins[w].at[c], dst_ref=outs[w].at[c], send_sem=send.at[w], recv_sem=recv.at[w],
                device_id=sib, device_id_type=MESH)
            rd.start()
            rds.append(rd)
        for w in range(n):
            pltpu.make_async_remote_copy(
                src_ref=ins[w].at[c], dst_ref=outs[w].at[1 - c], send_sem=send.at[w], recv_sem=recv.at[w],
                device_id=sib, device_id_type=MESH).wait_recv()
        for rd in rds:
            rd.wait_send()

    return pl.pallas_call(
        body, name="grad_pair_join",
        in_specs=[HBM] * n, out_specs=[HBM] * n,
        out_shape=[jax.ShapeDtypeStruct(b.shape, b.dtype) for b in bufs],
        input_output_aliases={w: w for w in range(n)},
        scratch_shapes=[pltpu.SemaphoreType.DMA((n,)), pltpu.SemaphoreType.DMA((n,))],
        compiler_params=pltpu.CompilerParams(has_side_effects=True),
    )(*bufs)


def _all_reduce_small(packed):
    R = packed.shape[0]

    def body(x_ref, o_ref, buf, send, recv):
        x, y, c = _place()
        me = 4 * x + 2 * y + c
        buf[me] = x_ref[...]
        rds = []
        for rel in range(1, N_DEV):
            fx, fy, fc = (rel >> 2) & 1, (rel >> 1) & 1, rel & 1
            peer = (1 - x if fx else x, 1 - y if fy else y, 1 - c if fc else c)
            rd = pltpu.make_async_remote_copy(
                src_ref=x_ref, dst_ref=buf.at[me], send_sem=send.at[rel - 1], recv_sem=recv.at[rel - 1],
                device_id=peer, device_id_type=MESH)
            rd.start()
            rds.append((rd, peer))
        for rel in range(1, N_DEV):
            rd, (px, py, pc) = rds[rel - 1]
            pltpu.make_async_remote_copy(
                src_ref=x_ref, dst_ref=buf.at[4 * px + 2 * py + pc], send_sem=send.at[rel - 1], recv_sem=recv.at[rel - 1],
                device_id=(px, py, pc), device_id_type=MESH).wait_recv()
        for rd, _ in rds:
            rd.wait_send()
        total = buf[0]
        for d in range(1, N_DEV):
            total = total + buf[d]
        o_ref[...] = total

    return pl.pallas_call(
        body, name="all_reduce_small",
        in_specs=[pl.BlockSpec(memory_space=pltpu.VMEM)],
        out_specs=pl.BlockSpec(memory_space=pltpu.VMEM),
        out_shape=jax.ShapeDtypeStruct(packed.shape, F32),
        scratch_shapes=[pltpu.VMEM((N_DEV, R, LANES), F32), pltpu.SemaphoreType.DMA((N_DEV - 1,)),
                        pltpu.SemaphoreType.DMA((N_DEV - 1,))],
        compiler_params=pltpu.CompilerParams(has_side_effects=True),
    )(packed)


def _pair_sum(name, split, theirs, core):
    n_chip, _, Rh, C = split.shape
    tile = _div_tile(Rh, 256, 2 * SUBLANES)
    nt = Rh // tile

    def body(core_ref, a_ref, b_ref, o_ref):
        o_ref[...] = (a_ref[...] + b_ref[...]).astype(o_ref.dtype)

    return pl.pallas_call(
        body, name=name,
        grid_spec=pltpu.PrefetchScalarGridSpec(
            num_scalar_prefetch=1,
            grid=(n_chip, nt),
            in_specs=[pl.BlockSpec((None, None, tile, C), lambda j, i, core_ref: (j, core_ref[0], i, 0)),
                      pl.BlockSpec((None, tile, C), lambda j, i, core_ref: (j, i, 0))],
            out_specs=pl.BlockSpec((None, tile, C), lambda j, i, core_ref: (j, i, 0)),
        ),
        out_shape=jax.ShapeDtypeStruct((n_chip, Rh, C), BF16),
        compiler_params=_params(("parallel", "parallel")),
    )(core, split, theirs)


def _chip_sum(name, own, landed, core):
    n_in, Rh, C = landed.shape
    tile = _div_tile(Rh, 256, 2 * SUBLANES)

    def body(core_ref, *refs):
        total = refs[0][...].astype(F32)
        for ref in refs[1:n_in + 1]:
            total = total + ref[...].astype(F32)
        refs[n_in + 1][...] = total

    slot = lambda j: pl.BlockSpec((None, tile, C), lambda i, core_ref: (j, i, 0))
    return pl.pallas_call(
        body, name=name,
        grid_spec=pltpu.PrefetchScalarGridSpec(
            num_scalar_prefetch=1,
            grid=(Rh // tile,),
            in_specs=[pl.BlockSpec((None, tile, C), lambda i, core_ref: (core_ref[1], i, 0))]
                     + [slot(j) for j in range(n_in)],
            out_specs=pl.BlockSpec((None, tile, C), lambda i, core_ref: (core_ref[0], i, 0)),
        ),
        out_shape=jax.ShapeDtypeStruct((2, Rh, C), F32),
        compiler_params=_params(("parallel",)),
    )(core, own, *([landed] * n_in))


def _adamw(name, w, g, m, v):
    R, C = w.shape
    tile = _div_tile(R, 256, SUBLANES)
    c1 = 1.0 / (1.0 - ADAM_B1 ** ADAM_STEP)
    c2 = 1.0 / (1.0 - ADAM_B2 ** ADAM_STEP)

    def body(w_ref, g_ref, m_ref, v_ref, d_ref, nm_ref, nv_ref):
        g_ = g_ref[...]
        nm = ADAM_B1 * m_ref[...] + (1.0 - ADAM_B1) * g_
        nv = ADAM_B2 * v_ref[...] + (1.0 - ADAM_B2) * (g_ * g_)
        d_ref[...] = -ADAM_LR * ((nm * c1) / (jnp.sqrt(nv * c2) + ADAM_EPS) + ADAM_WD * w_ref[...])
        nm_ref[...] = nm
        nv_ref[...] = nv

    spec = pl.BlockSpec((tile, C), lambda i: (i, 0))
    return pl.pallas_call(
        body, name=name,
        grid=(R // tile,),
        in_specs=[spec] * 4, out_specs=[spec] * 3,
        out_shape=[jax.ShapeDtypeStruct((R, C), F32)] * 3,
        compiler_params=_params(("parallel",)),
    )(w, g, m, v)


SMALL =["norm_mix_pre", "b_gate", "mu_rw", "w0", "a0", "k_k", "k_a", "r_k", "lnx_w", "lnx_b",
         "norm_mix_post", "norm_ffn_pre", "norm_ffn_post"]
BIG = ["w_in", "w_up", "a_up", "g_up", "w_sb_out", "w_rw_out", "w_o", "w_ffn_gate", "w_ffn_up", "w_ffn_down"]
ROW_SHARDED = ("w_o", "w_ffn_down")
ORDER = ["norm_mix_pre", "w_in", "b_gate", "mu_rw", "w0", "w_up", "a0", "a_up", "g_up", "k_k", "k_a", "r_k",
         "lnx_w", "lnx_b", "w_sb_out", "w_rw_out", "w_o", "norm_mix_post", "norm_ffn_pre", "w_ffn_gate",
         "w_ffn_up", "w_ffn_down", "norm_ffn_post"]


def _pack_small(vals, extra_rows=0):
    rows = jnp.concatenate([vals[n].reshape(-1, LANES) for n in SMALL], axis=0)
    pad = (-(rows.shape[0] + extra_rows)) % SUBLANES + extra_rows
    return jnp.pad(rows, ((0, pad), (0, 0)))


def _unpack_small(packed, shapes):
    out, r = {}, 0
    for n in SMALL:
        size = 1
        for s in shapes[n]:
            size *= s
        out[n] = packed[r:r + size // LANES].reshape(shapes[n])
        r += size // LANES
    return out


def kernel(x, norm_mix_pre, w_in, b_gate, mu_rw, w0, w_up, a0, a_up, g_up, k_k, k_a, r_k, lnx_w, lnx_b, w_sb_out, w_rw_out, w_o, norm_mix_post, norm_ffn_pre, w_ffn_gate, w_ffn_up, w_ffn_down, norm_ffn_post, loss_target, m_norm_mix_pre, m_w_in, m_b_gate, m_mu_rw, m_w0, m_w_up, m_a0, m_a_up, m_g_up, m_k_k, m_k_a, m_r_k, m_lnx_w, m_lnx_b, m_w_sb_out, m_w_rw_out, m_w_o, m_norm_mix_post, m_norm_ffn_pre, m_w_ffn_gate, m_w_ffn_up, m_w_ffn_down, m_norm_ffn_post, v_norm_mix_pre, v_w_in, v_b_gate, v_mu_rw, v_w0, v_w_up, v_a0, v_a_up, v_g_up, v_k_k, v_k_a, v_r_k, v_lnx_w, v_lnx_b, v_w_sb_out, v_w_rw_out, v_w_o, v_norm_mix_post, v_norm_ffn_pre, v_w_ffn_gate, v_w_ffn_up, v_w_ffn_down, v_norm_ffn_post):
    W = dict(norm_mix_pre=norm_mix_pre, w_in=w_in, b_gate=b_gate, mu_rw=mu_rw, w0=w0, w_up=w_up, a0=a0, a_up=a_up,
             g_up=g_up, k_k=k_k, k_a=k_a, r_k=r_k, lnx_w=lnx_w, lnx_b=lnx_b, w_sb_out=w_sb_out, w_rw_out=w_rw_out,
             w_o=w_o, norm_mix_post=norm_mix_post, norm_ffn_pre=norm_ffn_pre, w_ffn_gate=w_ffn_gate,
             w_ffn_up=w_ffn_up, w_ffn_down=w_ffn_down, norm_ffn_post=norm_ffn_post)
    Mo = dict(norm_mix_pre=m_norm_mix_pre, w_in=m_w_in, b_gate=m_b_gate, mu_rw=m_mu_rw, w0=m_w0, w_up=m_w_up, a0=m_a0,
              a_up=m_a_up, g_up=m_g_up, k_k=m_k_k, k_a=m_k_a, r_k=m_r_k, lnx_w=m_lnx_w, lnx_b=m_lnx_b,
              w_sb_out=m_w_sb_out, w_rw_out=m_w_rw_out, w_o=m_w_o, norm_mix_post=m_norm_mix_post,
              norm_ffn_pre=m_norm_ffn_pre, w_ffn_gate=m_w_ffn_gate, w_ffn_up=m_w_ffn_up, w_ffn_down=m_w_ffn_down,
              norm_ffn_post=m_norm_ffn_post)
    Vo = dict(norm_mix_pre=v_norm_mix_pre, w_in=v_w_in, b_gate=v_b_gate, mu_rw=v_mu_rw, w0=v_w0, w_up=v_w_up, a0=v_a0,
              a_up=v_a_up, g_up=v_g_up, k_k=v_k_k, k_a=v_k_a, r_k=v_r_k, lnx_w=v_lnx_w, lnx_b=v_lnx_b,
              w_sb_out=v_w_sb_out, w_rw_out=v_w_rw_out, w_o=v_w_o, norm_mix_post=v_norm_mix_post,
              norm_ffn_pre=v_norm_ffn_pre, w_ffn_gate=v_w_ffn_gate, w_ffn_up=v_w_ffn_up, w_ffn_down=v_w_ffn_down,
              norm_ffn_post=v_norm_ffn_post)
    shapes = {n: W[n].shape for n in ORDER}
    B, S, D = x.shape
    T = B * S
    x2 = x.reshape(T, D)
    tgt = loss_target.reshape(T, D)
    vec = {n: W[n].reshape(1, -1) for n in SMALL}

    work = lambda t, n: t[0] if n in ROW_SHARDED else jnp.swapaxes(t[0], 0, 1)
    halved = [work(W[n], n).astype(BF16) for n in BIG]
    halved = [h.reshape(2, h.shape[0] // 2, h.shape[1]) for h in halved]
    full = {n: gth.reshape(-1, gth.shape[3]) for n, gth in zip(BIG, _all_gather_chips(halved))}
    w_in_t = full["w_in"]
    w_sb_t, w_rw_t, w_gt_t = w_in_t[:SB_COLS], w_in_t[SB_COLS:SB_COLS + RW_COLS], w_in_t[SB_COLS + RW_COLS:]
    lora_rows = {"w_up": 0, "a_up": 64, "g_up": 128}
    lora = {n: jnp.pad(full[n].T, ((r0, LORA_COLS - r0 - full[n].shape[1]), (0, 0))) for n, r0 in lora_rows.items()}
    mu = vec["mu_rw"]
    mu_parts = [mu[:, :512], mu[:, 512:1024], mu[:, 1024:1536], mu[:, 1536:]]
    b1, b2 = vec["b_gate"][:, :D], vec["b_gate"][:, D:]

    (h1,) = _rowwise("norm_mix_pre", _f_norm, [(x2, D, 0)], [vec["norm_mix_pre"]], [(D, BF16)])
    p_sb = _mm("proj_sb", h1, w_sb_t, tb=True, out_dtype=BF16)
    p_rw = _mm("proj_rw", h1, w_rw_t, tb=True)
    p_gt = _mm("proj_gate", h1, w_gt_t, tb=True, out_dtype=BF16)
    o_sb = _attn_fwd(p_sb, B, S)
    pre_params = mu_parts + [vec["w0"], lora["w_up"], vec["a0"], lora["a_up"], lora["g_up"], vec["k_k"], vec["k_a"]]
    r_, lw_, k2_, v_, kap_, a_, g_ = _rw_pre(p_rw, pre_params, S)
    y_wkv, states = _wkv_fwd(r_, lw_, k2_, v_, kap_, a_, B, S)
    post_rows = [(y_wkv, 512, 0), (r_, 512, 0), (k2_, 512, 0), (v_, 512, 0), (g_, 512, 0)]
    post_params = [vec["lnx_w"], vec["lnx_b"], vec["r_k"]]
    (o_rw,) = _rowwise("rw_post", _f_rwpost, post_rows, post_params, [(512, BF16)])
    m1 = _mm("mix_sb_out", o_sb, full["w_sb_out"], tb=True, out_dtype=BF16)
    m2 = _mm("mix_rw_out", o_rw, full["w_rw_out"], tb=True, out_dtype=BF16)
    merge_rows = [(p_gt, D, 0), (p_gt, D, 1), (m1, D, 0), (m2, D, 0)]
    (merged,) = _rowwise("merge", _f_merge, merge_rows, [b1, b2], [(D, BF16)])
    u = _mm("mix_out", merged, full["w_o"])
    post1_params = [vec["norm_mix_post"], vec["norm_ffn_pre"]]
    x1, h2 = _rowwise("post_mix", _f_post1, [(x2, D, 0), (u, D, 0)], post1_params, [(D, F32), (D, BF16)])
    ag, au, sw = _mm_fused("ffn_in", [h2, h2], [full["w_ffn_gate"], full["w_ffn_up"]], [BF16] * 3, tb=True,
                           epilogue=lambda gu, _: (gu[0], gu[1], _f_swiglu(*gu)[0]))
    f = _mm("ffn_down", sw, full["w_ffn_down"])
    loss_part, dx1, df, dg4 = _loss_head(x1, f, tgt, vec["norm_ffn_post"])

    gbig, gsmall = {}, {"norm_ffn_post": dg4}
    gbig["w_ffn_down"] = _mm("g_ffn_down", sw, df, ta=True)

    def swiglu_back(dsw, gu):
        return jax.vjp(_f_swiglu, *gu)[1]((dsw[0],))

    dag, dau = _mm_fused("ffn_back", [df], [full["w_ffn_down"]], [BF16] * 2, tb=True, extras=[ag, au],
                         epilogue=swiglu_back)
    (dh2,) = _mm_fused("d_h2", [dag, dau], [full["w_ffn_gate"], full["w_ffn_up"]], [F32], add=True)
    gbig["w_ffn_gate"] = _mm("g_ffn_gate", dag, h2, ta=True)
    gbig["w_ffn_up"] = _mm("g_ffn_up", dau, h2, ta=True)
    (dx_res, du), (dg2, dg3) = _rowwise_vjp("post_mix_bwd", _f_post1, [(x2, D, 0), (u, D, 0)], post1_params,
                                            [[dx1], [dh2]], [True, True], [True, True], bf16_rows=(1,))
    gsmall["norm_mix_post"], gsmall["norm_ffn_pre"] = dg2, dg3
    dmerged = _mm("d_merged", du, full["w_o"], tb=True, out_dtype=BF16)
    gbig["w_o"] = _mm("g_w_o", merged, du, ta=True)
    (dpg1, dpg2, dm1, dm2), (db1, db2) = _rowwise_vjp("merge_bwd", _f_merge, merge_rows, [b1, b2], [[dmerged]],
                                                      [True] * 4, [True, True], bf16_rows=(0, 1, 2, 3))
    gsmall["b_gate"] = jnp.concatenate([db1, db2], axis=1)
    do_sb = _mm("d_o_sb", dm1, full["w_sb_out"])
    do_rw = _mm("d_o_rw", dm2, full["w_rw_out"])
    gbig["w_sb_out"] = _mm("g_sb_out", dm1, o_sb, ta=True)
    gbig["w_rw_out"] = _mm("g_rw_out", dm2, o_rw, ta=True)
    (dy_wkv, dr_a, dk2_a, dv_a, dg_), (dlnx_w, dlnx_b, dr_k) = _rowwise_vjp(
        "rw_post_bwd", _f_rwpost, post_rows, post_params, [[do_rw]], [True] * 5, [True] * 3)
    gsmall["lnx_w"], gsmall["lnx_b"], gsmall["r_k"] = dlnx_w, dlnx_b, dr_k
    dr_b, dlw, dk2_b, dv_b, dkap, da = _wkv_bwd(r_, lw_, k2_, v_, kap_, a_, states, dy_wkv, B, S)
    pre_cts = [[dr_a, dr_b], [dlw], [dk2_a, dk2_b], [dv_a, dv_b], [dkap], [da], [dg_]]
    dp_rw, dpre_params = _rw_pre_bwd(p_rw, pre_params, pre_cts, S)
    gsmall["mu_rw"] = jnp.concatenate(dpre_params[:4], axis=1)
    gsmall["w0"], gsmall["a0"], gsmall["k_k"], gsmall["k_a"] = dpre_params[4], dpre_params[6], dpre_params[9], dpre_params[10]
    glora = {"w_up": dpre_params[5][0:64].T, "a_up": dpre_params[7][64:128].T, "g_up": dpre_params[8][128:256].T}
    dq, dk, dv = _attn_bwd(p_sb, o_sb, do_sb, B, S)
    (dh1,) = _mm_fused("d_h1_sb", [dq, dk, dv], [w_sb_t[:512], w_sb_t[512:1024], w_sb_t[1024:]], [F32], add=True)
    dh1 = _mm("d_h1_rw", dp_rw, w_rw_t, acc=dh1)
    (dh1,) = _mm_fused("d_h1_gate", [dpg1, dpg2], [w_gt_t[:D], w_gt_t[D:]], [F32], add=True,
                       extras=[dh1], epilogue=lambda p, e: (p[0] + e[0],))
    gbig["w_in"] = jnp.concatenate(
        [_mm("g_in_" + tag, d, h1, ta=True)
         for tag, d in (("q", dq), ("k", dk), ("v", dv), ("rw", dp_rw), ("g1", dpg1), ("g2", dpg2))], axis=0)
    (grad_x2,), (dg1,) = _rowwise_vjp("norm_mix_pre_bwd", _f_norm, [(x2, D, 0)], [vec["norm_mix_pre"]], [[dh1]],
                                      [True], [True], add_to={0: dx_res})
    gsmall["norm_mix_pre"] = dg1
    gbig.update(glora)

    split = [gbig[n].reshape(N_CHIPS, 2, gbig[n].shape[0] // (2 * N_CHIPS), gbig[n].shape[1]) for n in BIG]
    core = jnp.stack([lax.axis_index("c"), 2 * lax.axis_index("x") + lax.axis_index("y")]).astype(jnp.int32)
    theirs = _pair_split(split)
    chip_sums = [_pair_sum("pair_sum_" + n, a, b, core) for n, a, b in zip(BIG, split, theirs)]
    landed = _chip_scatter(chip_sums)
    joined = _pair_join([_chip_sum("chip_sum_" + n, own, got, core) for n, own, got in zip(BIG, chip_sums, landed)])
    grads = {n: j.reshape(-1, j.shape[2]) for n, j in zip(BIG, joined)}

    small_local = _pack_small({n: gsmall[n] for n in SMALL}, extra_rows=1)
    loss_row = small_local.shape[0] - 1
    small_local = small_local.at[loss_row].set(loss_part[0])
    small_sum = _all_reduce_small(small_local)
    loss = small_sum[loss_row, 0]

    delta, new_m, new_v = {}, {}, {}
    unwork = lambda t, n: (t if n in ROW_SHARDED else jnp.swapaxes(t, 0, 1))[None]
    for n in BIG:
        d_, m_, v2_ = _adamw("adamw_" + n, work(W[n], n), grads[n], work(Mo[n], n), work(Vo[n], n))
        delta[n], new_m[n], new_v[n], grads[n] = (unwork(t, n) for t in (d_, m_, v2_, grads[n]))
    pk = lambda src: _pack_small({n: src[n] for n in SMALL}, extra_rows=1)
    d_s, m_s, v_s = _adamw("adamw_small", pk(W), small_sum.at[loss_row].set(0.0), pk(Mo), pk(Vo))
    for dst, packed in ((grads, small_sum), (delta, d_s), (new_m, m_s), (new_v, v_s)):
        dst.update(_unpack_small(packed, shapes))

    return (loss, grad_x2.reshape(B, S, D), *[grads[n] for n in ORDER], *[delta[n] for n in ORDER],
            *[new_m[n] for n in ORDER], *[new_v[n] for n in ORDER])
```

```python
import functools

import jax
import jax.numpy as jnp
from jax import lax
from jax.experimental import pallas as pl
from jax.experimental.pallas import tpu as pltpu

F32 = jnp.float32
BF16 = jnp.bfloat16
MESH = pl.DeviceIdType.MESH

D_MODEL = 1024
SB_HEADS = 8
HEAD_DIM = 64
SB_WIDTH = SB_HEADS * HEAD_DIM
RW_WIDTH = 512
LORA_COLS = 256
SB_COLS = 3 * SB_WIDTH
RW_COLS = 3 * RW_WIDTH + LORA_COLS
GATE_COLS = 2 * D_MODEL
D_FF = 2816
RMS_EPS = 1e-6
GN_EPS = HEAD_DIM * 1e-5
WKV_CHUNK = 64
WKV_SEQS = 4
ATTN_QUERIES = 256
ATTN_KEYS = 128
ATTN_DEAD = -120.0
LANES = 128
SUBLANES = 8
N_CHIPS = 4
N_DEV = 8

ADAM_LR = 0.001
ADAM_B1 = 0.9
ADAM_B2 = 0.999
ADAM_EPS = 1e-08
ADAM_WD = 0.01
ADAM_STEP = 10

VMEM_LIMIT = 48 * 1024 * 1024
WKV_BWD_VMEM = 58 * 1024 * 1024


def _params(sem=None, vmem=VMEM_LIMIT, **kw):
    if sem is not None:
        kw["dimension_semantics"] = sem
    return pltpu.CompilerParams(vmem_limit_bytes=vmem, **kw)


def _div_tile(dim, pref, mult=LANES):
    if dim <= pref:
        return dim
    t = pref - pref % mult
    while t >= mult:
        if dim % t == 0:
            return t
        t -= mult
    return dim


def _dot(a, b, dims):
    return lax.dot_general(a, b, (dims, ((), ())), preferred_element_type=F32)


def _mm(name, a, b, *, ta=False, tb=False, acc=None, out_dtype=F32):
    if ta:
        K, M = a.shape
    else:
        M, K = a.shape
    N = b.shape[0] if tb else b.shape[1]
    if ta:
        tm, tn, tk = _div_tile(M, 1408), _div_tile(N, 1408), _div_tile(K, 512)
    else:
        tm, tn, tk = _div_tile(M, 512), _div_tile(N, 1408), _div_tile(K, 1408)
    nk = K // tk
    dims = ((0,) if ta else (1,), (1,) if tb else (0,))
    has_acc = acc is not None

    def body(*refs):
        a_ref, b_ref = refs[0], refs[1]
        part = _dot(a_ref[...].astype(BF16), b_ref[...].astype(BF16), dims)
        if nk == 1:
            o_ref = refs[-1]
            o_ref[...] = (part + refs[2][...] if has_acc else part).astype(o_ref.dtype)
            return
        o_ref, scr = refs[-2], refs[-1]
        k = pl.program_id(2)

        @pl.when(k == 0)
        def _():
            scr[...] = part + refs[2][...] if has_acc else part

        @pl.when(k > 0)
        def _():
            scr[...] += part

        @pl.when(k == nk - 1)
        def _():
            o_ref[...] = scr[...].astype(o_ref.dtype)

    a_spec = pl.BlockSpec((tk, tm), lambda i, j, k: (k, i)) if ta else pl.BlockSpec((tm, tk), lambda i, j, k: (i, k))
    b_spec = pl.BlockSpec((tn, tk), lambda i, j, k: (j, k)) if tb else pl.BlockSpec((tk, tn), lambda i, j, k: (k, j))
    o_spec = pl.BlockSpec((tm, tn), lambda i, j, k: (i, j))
    return pl.pallas_call(
        body, name=name,
        grid=(M // tm, N // tn, nk),
        in_specs=[a_spec, b_spec] + ([o_spec] if has_acc else []),
        out_specs=o_spec,
        out_shape=jax.ShapeDtypeStruct((M, N), out_dtype),
        scratch_shapes=[pltpu.VMEM((tm, tn), F32)] if nk > 1 else [],
        compiler_params=_params(("parallel", "parallel", "arbitrary")),
    )(*([a, b] + ([acc] if has_acc else [])))


def _mm_fused(name, lhs, rhs, outs, *, tb=False, add=False, extras=(), epilogue=None):
    M, K = lhs[0].shape
    N = rhs[0].shape[0] if tb else rhs[0].shape[1]
    tm, tn, tk = _div_tile(M, 512), _div_tile(N, 1408), _div_tile(K, 1408)
    nk = K // tk
    n_l, n_e, n_o = len(lhs), len(extras), len(outs)
    n_acc = 1 if add else n_l
    dims = ((1,), (1,) if tb else (0,))

    def body(*refs):
        l_refs, r_refs = refs[:n_l], refs[n_l:2 * n_l]
        e_refs = refs[2 * n_l:2 * n_l + n_e]
        o_refs = refs[2 * n_l + n_e:2 * n_l + n_e + n_o]
        scr = refs[2 * n_l + n_e + n_o:]
        parts = [_dot(l[...].astype(BF16), r[...].astype(BF16), dims) for l, r in zip(l_refs, r_refs)]
        if add:
            parts = [functools.reduce(lambda u, v: u + v, parts)]

        def finish(vals):
            res = epilogue(vals, [e[...].astype(F32) for e in e_refs]) if epilogue else vals
            for ref, val in zip(o_refs, res):
                ref[...] = val.astype(ref.dtype)

        if nk == 1:
            finish(parts)
            return
        k = pl.program_id(2)

        @pl.when(k == 0)
        def _():
            for s, part in zip(scr, parts):
                s[...] = part

        @pl.when(k > 0)
        def _():
            for s, part in zip(scr, parts):
                s[...] += part

        @pl.when(k == nk - 1)
        def _():
            finish([s[...] for s in scr])

    a_spec = pl.BlockSpec((tm, tk), lambda i, j, k: (i, k))
    b_spec = pl.BlockSpec((tn, tk), lambda i, j, k: (j, k)) if tb else pl.BlockSpec((tk, tn), lambda i, j, k: (k, j))
    o_spec = pl.BlockSpec((tm, tn), lambda i, j, k: (i, j))
    return pl.pallas_call(
        body, name=name,
        grid=(M // tm, N // tn, nk),
        in_specs=[a_spec] * n_l + [b_spec] * n_l + [o_spec] * n_e,
        out_specs=[o_spec] * n_o,
        out_shape=[jax.ShapeDtypeStruct((M, N), dt) for dt in outs],
        scratch_shapes=[pltpu.VMEM((tm, tn), F32)] * (n_acc if nk > 1 else 0),
        compiler_params=_params(("parallel", "parallel", "arbitrary")),
    )(*lhs, *rhs, *extras)


def _row_spec(tile, width, colblk):
    return pl.BlockSpec((tile, width), lambda i: (i, colblk))


def _full_spec(shape):
    return pl.BlockSpec(shape, lambda i: (0,) * len(shape))


def _rowwise(name, fn, rows, params, outs, tile=256):
    T = rows[0][0].shape[0]
    tile = min(tile, T)
    n_r, n_p = len(rows), len(params)

    def body(*refs):
        r = [x[...].astype(F32) for x in refs[:n_r]]
        p = [x[...].astype(F32) for x in refs[n_r:n_r + n_p]]
        for o_ref, val in zip(refs[n_r + n_p:], fn(*r, *p)):
            o_ref[...] = val.astype(o_ref.dtype)

    return pl.pallas_call(
        body, name=name,
        grid=(T // tile,),
        in_specs=[_row_spec(tile, w, cb) for _, w, cb in rows] + [_full_spec(p.shape) for p in params],
        out_specs=[_row_spec(tile, w, 0) for w, _ in outs],
        out_shape=[jax.ShapeDtypeStruct((T, w), dt) for w, dt in outs],
        compiler_params=_params(("parallel",)),
    )(*([a for a, _, _ in rows] + list(params)))


def _rowwise_vjp(name, fn, rows, params, cts, need_rows, need_params, add_to=None, tile=256, bf16_rows=()):
    add_to = add_to or {}
    T = rows[0][0].shape[0]
    tile = min(tile, T)
    n_r, n_p = len(rows), len(params)
    ct_flat = [c for group in cts for c in group]
    ct_sizes = [len(group) for group in cts]
    add_idx = sorted(add_to)
    row_out = [i for i in range(n_r) if need_rows[i]]
    par_out = [i for i in range(n_p) if need_params[i]]
    n_ct, n_add = len(ct_flat), len(add_idx)

    def body(*refs):
        pos = 0
        r = [x[...].astype(F32) for x in refs[pos:pos + n_r]]
        pos += n_r
        p = [x[...].astype(F32) for x in refs[pos:pos + n_p]]
        pos += n_p
        ct_vals = [x[...].astype(F32) for x in refs[pos:pos + n_ct]]
        pos += n_ct
        adds = {i: x[...] for i, x in zip(add_idx, refs[pos:pos + n_add])}
        pos += n_add
        drow_refs = refs[pos:pos + len(row_out)]
        pos += len(row_out)
        dpar_refs = refs[pos:pos + len(par_out)]
        ct_in, q = [], 0
        for n in ct_sizes:
            ct_in.append(functools.reduce(lambda u, v: u + v, ct_vals[q:q + n]))
            q += n
        _, vjp = jax.vjp(fn, *r, *p)
        grads = vjp(tuple(ct_in))
        for ref, i in zip(drow_refs, row_out):
            g = grads[i]
            ref[...] = (g + adds[i] if i in adds else g).astype(ref.dtype)

        @pl.when(pl.program_id(0) == 0)
        def _():
            for ref in dpar_refs:
                ref[...] = jnp.zeros_like(ref)

        for ref, i in zip(dpar_refs, par_out):
            ref[...] += grads[n_r + i]

    ct_widths = [c.shape[1] for c in ct_flat]
    in_specs = ([_row_spec(tile, w, cb) for _, w, cb in rows] + [_full_spec(p.shape) for p in params]
                + [_row_spec(tile, w, 0) for w in ct_widths] + [_row_spec(tile, rows[i][1], 0) for i in add_idx])
    out_specs = [_row_spec(tile, rows[i][1], 0) for i in row_out] + [_full_spec(params[i].shape) for i in par_out]
    out_shape = ([jax.ShapeDtypeStruct((T, rows[i][1]), BF16 if i in bf16_rows else F32) for i in row_out]
                 + [jax.ShapeDtypeStruct(params[i].shape, F32) for i in par_out])
    res = pl.pallas_call(
        body, name=name,
        grid=(T // tile,),
        in_specs=in_specs, out_specs=out_specs, out_shape=out_shape,
        compiler_params=_params(("arbitrary",)),
    )(*([a for a, _, _ in rows] + list(params) + ct_flat + [add_to[i] for i in add_idx]))
    return res[:len(row_out)], res[len(row_out):]


def _sigmoid(x):
    return 0.5 * (jnp.tanh(0.5 * x) + 1.0)


def _softplus(x):
    return jnp.maximum(x, 0.0) + jnp.log(1.0 + jnp.exp(-jnp.abs(x)))


def _rms(x, g):
    return x * lax.rsqrt(jnp.mean(x * x, axis=-1, keepdims=True) + RMS_EPS) * g


def _segsum_impl(x):
    n, w = x.shape[-1], 2 * LANES
    r = lax.shift_right_logical(lax.broadcasted_iota(jnp.int32, (w, w), 0), 6)
    c = lax.shift_right_logical(lax.broadcasted_iota(jnp.int32, (w, w), 1), 6)
    bd = (r == c).astype(BF16)
    hi = x.astype(BF16)
    rest = x - hi.astype(F32)
    mid = rest.astype(BF16)
    lo = (rest - mid.astype(F32)).astype(BF16)
    nn = ((1,), (0,))
    blocks = [_dot(hi[:, j:j + w], bd, nn) + _dot(mid[:, j:j + w], bd, nn) + _dot(lo[:, j:j + w], bd, nn)
              for j in range(0, n, w)]
    return jnp.concatenate(blocks, axis=1)


@jax.custom_vjp
def _segsum(x):
    return _segsum_impl(x)


_segsum.defvjp(lambda x: (_segsum_impl(x), None), lambda _, g: (_segsum_impl(g),))


@jax.custom_vjp
def _mmb(a, w):
    return _dot(a.astype(BF16), w.astype(BF16), ((1,), (0,)))


def _mmb_fwd(a, w):
    return _mmb(a, w), (a, w)


def _mmb_bwd(res, g):
    a, w = res
    gb = g.astype(BF16)
    return _dot(gb, w.astype(BF16), ((1,), (1,))), _dot(a.astype(BF16), gb, ((0,), (0,)))


_mmb.defvjp(_mmb_fwd, _mmb_bwd)


def _f_norm(x, g):
    return (_rms(x, g),)


def _f_post1(x, u, g2, g3):
    x1 = x + _rms(u, g2)
    return x1, _rms(x1, g3)


def _f_swiglu(ag, au):
    return (ag * _sigmoid(ag) * au,)


def _f_merge(pg1, pg2, m1, m2, b1, b2):
    return (_sigmoid(pg1 + b1) * m1 + _sigmoid(pg2 + b2) * m2,)


def _f_out(x1, f, g4):
    return (x1 + _rms(f, g4),)


def _f_rwpre(pr, pk, pv, pz, qr, qk, qv, qz, mur, muk, muv, muz, w0, wup, a0, aup, gup, k_k, k_a):
    r = pr + (qr - pr) * mur
    k = pk + (qk - pk) * muk
    v = pv + (qv - pv) * muv
    z = pz + (qz - pz) * muz
    w_raw = w0 + _mmb(jnp.tanh(z), wup)
    lw = -jnp.exp(-_softplus(-w_raw) - 0.5)
    a = _sigmoid(a0 + _mmb(z, aup))
    g = _mmb(_sigmoid(z), gup)
    kk = k * k_k
    kap = kk * lax.rsqrt(jnp.maximum(_segsum(kk * kk), 1e-24))
    k2 = k * (1.0 + (a - 1.0) * k_a)
    return r, lw, k2, v, kap, a, g


def _f_rwpost(y, r, k2, v, g, lnx_w, lnx_b, r_k):
    inv = 1.0 / HEAD_DIM
    yc = y - _segsum(y) * inv
    var = _segsum(yc * yc) * inv
    yn = yc * lax.rsqrt(var + GN_EPS) * lnx_w + lnx_b
    bonus = _segsum(r * k2 * r_k) * v
    return ((yn + bonus) * g,)


RW_GROUPS = (0, 512, 1024, 1536, RW_COLS)


def _column_groups(p):
    return [p[:, a:b] for a, b in zip(RW_GROUPS[:-1], RW_GROUPS[1:])]


def _previous_tokens(p, halo, first_of_sequence):
    rows = lax.broadcasted_iota(jnp.int32, (p.shape[0], 1), 0)
    before = jnp.where(first_of_sequence, 0.0, halo[SUBLANES - 1:SUBLANES, :])
    return jnp.where(rows == 0, before, pltpu.roll(p, 1, axis=0))


def _halo_spec(tile, order):
    per = tile // SUBLANES
    return pl.BlockSpec((SUBLANES, RW_COLS), lambda i: (jnp.maximum(order(i) * per - 1, 0), 0))


def _rw_pre(p_rw, params, S, tile=128):
    T = p_rw.shape[0]
    tile = min(tile, T)
    assert S % tile == 0
    n_p = len(params)

    def body(*refs):
        p_ref, halo_ref = refs[0], refs[1]
        par = [x[...].astype(F32) for x in refs[2:2 + n_p]]
        p = p_ref[...]
        first = lax.rem(pl.program_id(0) * tile, S) == 0
        prev = _previous_tokens(p, halo_ref[...], first)
        for o_ref, val in zip(refs[2 + n_p:], _f_rwpre(*_column_groups(p), *_column_groups(prev), *par)):
            o_ref[...] = val

    out_spec = pl.BlockSpec((tile, RW_WIDTH), lambda i: (i, 0))
    return pl.pallas_call(
        body, name="rw_pre",
        grid=(T // tile,),
        in_specs=[pl.BlockSpec((tile, RW_COLS), lambda i: (i, 0)), _halo_spec(tile, lambda i: i)]
                 + [_full_spec(q.shape) for q in params],
        out_specs=[out_spec] * 7,
        out_shape=[jax.ShapeDtypeStruct((T, RW_WIDTH), F32)] * 7,
        compiler_params=_params(("parallel",)),
    )(p_rw, p_rw, *params)


def _rw_pre_bwd(p_rw, params, cts, S, tile=128):
    T = p_rw.shape[0]
    tile = min(tile, T)
    assert S % tile == 0
    nt = T // tile
    n_p = len(params)
    ct_flat = [c for group in cts for c in group]
    ct_sizes = [len(group) for group in cts]
    n_ct = len(ct_flat)

    def body(*refs):
        p_ref, halo_ref = refs[0], refs[1]
        par = [x[...].astype(F32) for x in refs[2:2 + n_p]]
        ct_vals = [x[...] for x in refs[2 + n_p:2 + n_p + n_ct]]
        dp_ref = refs[2 + n_p + n_ct]
        dpar_refs = refs[3 + n_p + n_ct:3 + 2 * n_p + n_ct]
        carry = refs[-1]
        step = pl.program_id(0)

        @pl.when(step == 0)
        def _():
            carry[...] = jnp.zeros_like(carry)
            for ref in dpar_refs:
                ref[...] = jnp.zeros_like(ref)

        ct_in, q = [], 0
        for n in ct_sizes:
            ct_in.append(functools.reduce(lambda u, v: u + v, ct_vals[q:q + n]))
            q += n
        p = p_ref[...]
        first = lax.rem((nt - 1 - step) * tile, S) == 0
        prev = _previous_tokens(p, halo_ref[...], first)
        _, vjp = jax.vjp(_f_rwpre, *_column_groups(p), *_column_groups(prev), *par)
        grads = vjp(tuple(ct_in))
        d_here = jnp.concatenate(grads[0:4], axis=1)
        d_prev = jnp.concatenate(grads[4:8], axis=1)
        rows = lax.broadcasted_iota(jnp.int32, (tile, 1), 0)
        from_next = jnp.where(rows == tile - 1, carry[0:1, :], pltpu.roll(d_prev, tile - 1, axis=0))
        dp_ref[...] = (d_here + from_next).astype(dp_ref.dtype)
        carry[...] = jnp.broadcast_to(jnp.where(first, 0.0, d_prev[0:1, :]), carry.shape)
        for ref, g in zip(dpar_refs, grads[8:]):
            ref[...] += g

    back = lambda i: nt - 1 - i
    row = lambda w: pl.BlockSpec((tile, w), lambda i: (back(i), 0))
    res = pl.pallas_call(
        body, name="rw_pre_bwd",
        grid=(nt,),
        in_specs=[row(RW_COLS), _halo_spec(tile, back)] + [_full_spec(q.shape) for q in params]
                 + [row(RW_WIDTH)] * n_ct,
        out_specs=[row(RW_COLS)] + [_full_spec(q.shape) for q in params],
        out_shape=[jax.ShapeDtypeStruct((T, RW_COLS), BF16)] + [jax.ShapeDtypeStruct(q.shape, F32) for q in params],
        scratch_shapes=[pltpu.VMEM((SUBLANES, RW_COLS), F32)],
        compiler_params=_params(("arbitrary",)),
    )(p_rw, p_rw, *params, *ct_flat)
    return res[0], res[1:]


def _loss_head(x1, f, target, g4, tile=256):
    T, D = x1.shape
    tile = min(tile, T)

    def body(x1_ref, f_ref, t_ref, g_ref, loss_ref, dx1_ref, df_ref, dg_ref):
        (y,), vjp = jax.vjp(_f_out, x1_ref[...], f_ref[...], g_ref[...])
        err = y - t_ref[...]
        dx1, df, dg = vjp((err * (1.0 / D),))
        dx1_ref[...] = dx1
        df_ref[...] = df.astype(df_ref.dtype)

        @pl.when(pl.program_id(0) == 0)
        def _():
            loss_ref[...] = jnp.zeros_like(loss_ref)
            dg_ref[...] = jnp.zeros_like(dg_ref)

        part = jnp.sum(jnp.sum(err * err, axis=1, keepdims=True), axis=0, keepdims=True) * (0.5 / D)
        loss_ref[...] += jnp.broadcast_to(part, loss_ref.shape)
        dg_ref[...] += dg

    row = pl.BlockSpec((tile, D), lambda i: (i, 0))
    return pl.pallas_call(
        body, name="loss_head",
        grid=(T // tile,),
        in_specs=[row, row, row, _full_spec(g4.shape)],
        out_specs=[_full_spec((SUBLANES, LANES)), row, row, _full_spec(g4.shape)],
        out_shape=[jax.ShapeDtypeStruct((SUBLANES, LANES), F32), jax.ShapeDtypeStruct((T, D), F32),
                   jax.ShapeDtypeStruct((T, D), BF16), jax.ShapeDtypeStruct(g4.shape, F32)],
        compiler_params=_params(("arbitrary",)),
    )(x1, f, target, g4)


def _nn(a, b):
    return _dot(a, b, ((1,), (0,)))


def _nt(a, b):
    return _dot(a, b, ((1,), (1,)))


def _tn(a, b):
    return _dot(a, b, ((0,), (0,)))


def _split_dot(x, u2):
    hi = x.astype(BF16)
    lo = (x - hi.astype(F32)).astype(BF16)
    return _nn(jnp.concatenate([hi, lo], axis=1), u2)


def _by_head(x, masks):
    return jnp.concatenate([(x * m).astype(BF16) for m in masks], axis=0)


def _fold_heads(x2, masks):
    R = x2.shape[0] // len(masks)
    return functools.reduce(lambda u, v: u + v, [x2[h * R:(h + 1) * R] * m for h, m in enumerate(masks)])


def _head_masks():
    lane = lax.broadcasted_iota(jnp.int32, (1, LANES), 1)
    return [((lane >= h * HEAD_DIM) & (lane < (h + 1) * HEAD_DIM)).astype(F32) for h in range(LANES // HEAD_DIM)]


def _key_tri(op):
    row = lax.broadcasted_iota(jnp.int32, (ATTN_KEYS, ATTN_KEYS), 0)
    col = lax.broadcasted_iota(jnp.int32, (ATTN_KEYS, ATTN_KEYS), 1)
    u = op(row, col).astype(BF16)
    return jnp.concatenate([u, u], axis=0)


def _causal(rows):
    row = lax.broadcasted_iota(jnp.int32, (rows, ATTN_KEYS), 0)
    col = lax.broadcasted_iota(jnp.int32, (rows, ATTN_KEYS), 1)
    return col < row


def _from_row(tree, r):
    return jax.tree.map(lambda x: x[r:], tree)


def _onto_rows(old, new, r):
    return jax.tree.map(lambda o, n: jnp.concatenate([o[:r], n], axis=0) if r else n, old, new)


def _sb_weights(qb16, kbh, c_fails, u_gt, strict):
    z_all = _nt(qb16, kbh)
    zs = [z_all[:, h * ATTN_KEYS:(h + 1) * ATTN_KEYS] for h in range(len(c_fails))]
    Ls = [jnp.minimum(-z, 0.0) - jnp.log(1.0 + jnp.exp(-jnp.abs(z))) for z in zs]
    Lms = Ls if strict is None else [jnp.where(strict, L, 0.0) for L in Ls]
    cums = [_split_dot(Lm, u_gt) for Lm in Lms]
    As = [jnp.exp(z + L + c + cum) for z, L, c, cum in zip(zs, Ls, c_fails, cums)]
    if strict is not None:
        As = [jnp.where(strict, A, 0.0) for A in As]
    return zs, Ls, Lms, As


def _attn_specs(S, qb):
    nq = S // qb
    q_spec = pl.BlockSpec((qb, LANES), lambda b, p, i: (b * nq + i, p))
    k_spec = pl.BlockSpec((S, LANES), lambda b, p, i: (b, SB_WIDTH // LANES + p))
    v_spec = pl.BlockSpec((S, LANES), lambda b, p, i: (b, 2 * SB_WIDTH // LANES + p))
    seq = pl.BlockSpec((S, LANES), lambda b, p, i: (b, p))
    return q_spec, k_spec, v_spec, q_spec, seq


def _key_walk(i, qb, block, carry, fails):
    per = qb // ATTN_KEYS
    for sub in reversed(range(per)):
        carry = block(i * per + sub, carry, sub * ATTN_KEYS)
    n = i * per

    def alive(c):
        return jnp.max(functools.reduce(jnp.maximum, fails(c))) > ATTN_DEAD

    def cond(state):
        return jnp.logical_and(state[0] < n, state[1])

    def body(state):
        c = block(n - 1 - state[0], state[2], None)
        return state[0] + 1, alive(c), c

    return lax.while_loop(cond, body, (jnp.int32(0), alive(carry), carry))[2]


def _attn_fwd(proj, B, S):
    qb = min(ATTN_QUERIES, S)
    scale = HEAD_DIM ** -0.5

    def body(q_ref, k_ref, v_ref, o_ref):
        i = pl.program_id(2)
        masks = _head_masks()
        u_gt = _key_tri(lambda r, c: r > c)
        q16 = (q_ref[...] * scale).astype(BF16)

        def block(J, carry, row0):
            r0 = pl.multiple_of(J * ATTN_KEYS, ATTN_KEYS)
            kbh = _by_head(k_ref[pl.ds(r0, ATTN_KEYS), :], masks)
            vbh = _by_head(v_ref[pl.ds(r0, ATTN_KEYS), :], masks)
            lo = row0 or 0
            strict = None if row0 is None else _causal(qb - lo)
            acc, cs = _from_row(carry, lo)
            _, _, Lms, As = _sb_weights(q16[lo:], kbh, cs, u_gt, strict)
            acc = acc + _nn(jnp.concatenate([A.astype(BF16) for A in As], axis=1), vbh)
            cs = tuple(c + jnp.sum(Lm, axis=1, keepdims=True) for c, Lm in zip(cs, Lms))
            return _onto_rows(carry, (acc, cs), lo)

        zero_c = tuple(jnp.zeros((qb, 1), F32) for _ in masks)
        carry = _key_walk(i, qb, block, (jnp.zeros((qb, LANES), F32), zero_c), lambda c: c[1])
        o_ref[...] = carry[0]

    q_spec, k_spec, v_spec, blk, _ = _attn_specs(S, qb)
    return pl.pallas_call(
        body, name="sb_attn_fwd",
        grid=(B, SB_WIDTH // LANES, S // qb),
        in_specs=[q_spec, k_spec, v_spec],
        out_specs=blk,
        out_shape=jax.ShapeDtypeStruct((B * S, SB_WIDTH), F32),
        compiler_params=_params(("parallel", "parallel", "arbitrary")),
    )(proj, proj, proj)


def _attn_bwd(proj, o, do, B, S):
    qb = min(ATTN_QUERIES, S)
    nq = S // qb
    scale = HEAD_DIM ** -0.5

    def body(q_ref, k_ref, v_ref, o_ref, do_ref, dq_ref, dk_out, dv_out, dk_ref, dv_ref):
        i = pl.program_id(2)

        @pl.when(i == 0)
        def _():
            dk_ref[...] = jnp.zeros_like(dk_ref)
            dv_ref[...] = jnp.zeros_like(dv_ref)

        masks = _head_masks()
        u_gt = _key_tri(lambda r, c: r > c)
        u_ge = _key_tri(lambda r, c: r >= c)
        heads = range(len(masks))
        q16 = (q_ref[...] * scale).astype(BF16)
        do16 = do_ref[...].astype(BF16)
        od = o_ref[...] * do16.astype(F32)
        totals = tuple(jnp.sum(od * m, axis=1, keepdims=True) for m in masks)

        def block(J, carry, row0):
            r0 = pl.multiple_of(J * ATTN_KEYS, ATTN_KEYS)
            kbh = _by_head(k_ref[pl.ds(r0, ATTN_KEYS), :], masks)
            vbh = _by_head(v_ref[pl.ds(r0, ATTN_KEYS), :], masks)
            lo = row0 or 0
            strict = None if row0 is None else _causal(qb - lo)
            dq, c_fail, c_p = _from_row(carry, lo)
            tot = _from_row(totals, lo)
            zs, Ls, Lms, As = _sb_weights(q16[lo:], kbh, c_fail, u_gt, strict)
            Abs = [A.astype(BF16) for A in As]
            dA_all = _nt(do16[lo:], vbh)
            Ps = [Abs[h].astype(F32) * dA_all[:, h * ATTN_KEYS:(h + 1) * ATTN_KEYS] for h in heads]
            afters = [c_p[h] + _split_dot(Ps[h], u_ge) for h in heads]
            sigs = [jnp.exp(zs[h] + Ls[h]) for h in heads]
            dzs = [Ps[h] * (1.0 - sigs[h]) - sigs[h] * (tot[h] - afters[h]) for h in heads]
            if strict is not None:
                dzs = [jnp.where(strict, dz, 0.0) for dz in dzs]
            dz_all = jnp.concatenate([dz.astype(BF16) for dz in dzs], axis=1)
            dv_ref[pl.ds(r0, ATTN_KEYS), :] += _fold_heads(_tn(jnp.concatenate(Abs, axis=1), do16[lo:]), masks)
            dk_ref[pl.ds(r0, ATTN_KEYS), :] += _fold_heads(_tn(dz_all, q16[lo:]), masks)
            dq = dq + _nn(dz_all, kbh)
            c_fail = tuple(c_fail[h] + jnp.sum(Lms[h], axis=1, keepdims=True) for h in heads)
            c_p = tuple(c_p[h] + jnp.sum(Ps[h], axis=1, keepdims=True) for h in heads)
            return _onto_rows(carry, (dq, c_fail, c_p), lo)

        zc = tuple(jnp.zeros((qb, 1), F32) for _ in masks)
        carry = _key_walk(i, qb, block, (jnp.zeros((qb, LANES), F32), zc, zc), lambda c: c[1])
        dq_ref[...] = (carry[0] * scale).astype(dq_ref.dtype)

        @pl.when(i == nq - 1)
        def _():
            dk_out[...] = dk_ref[...].astype(dk_out.dtype)
            dv_out[...] = dv_ref[...].astype(dv_out.dtype)

    q_spec, k_spec, v_spec, blk, seq = _attn_specs(S, qb)
    return pl.pallas_call(
        body, name="sb_attn_bwd",
        grid=(B, SB_WIDTH // LANES, nq),
        in_specs=[q_spec, k_spec, v_spec, blk, blk],
        out_specs=[blk, seq, seq],
        out_shape=[jax.ShapeDtypeStruct((B * S, SB_WIDTH), BF16)] * 3,
        scratch_shapes=[pltpu.VMEM((S, LANES), F32), pltpu.VMEM((S, LANES), F32)],
        compiler_params=_params(("parallel", "parallel", "arbitrary")),
    )(proj, proj, proj, o, do)


_BATCHED = {"nn": "gmk,gkn->gmn", "nt": "gmk,gnk->gmn", "tn": "gkm,gkn->gmn"}


def _bdot_raw(a, b, kind, passes):
    e = functools.partial(jnp.einsum, _BATCHED[kind], preferred_element_type=F32)
    ah, bh = a.astype(BF16), b.astype(BF16)
    if passes == 1:
        return e(ah, bh)
    al, bl = (a - ah.astype(F32)).astype(BF16), (b - bh.astype(F32)).astype(BF16)
    return e(ah, bh) + e(ah, bl) + e(al, bh)


@functools.partial(jax.custom_vjp, nondiff_argnums=(2, 3))
def _bdot(a, b, kind, passes):
    return _bdot_raw(a, b, kind, passes)


def _bdot_fwd(a, b, kind, passes):
    return _bdot_raw(a, b, kind, passes), (a, b)


def _bdot_bwd(kind, passes, res, g):
    a, b = res
    if kind == "nn":
        return _bdot_raw(g, b, "nt", passes), _bdot_raw(a, g, "tn", passes)
    if kind == "nt":
        return _bdot_raw(g, b, "nn", passes), _bdot_raw(g, a, "tn", passes)
    return _bdot_raw(b, g, "nt", passes), _bdot_raw(a, g, "nn", passes)


_bdot.defvjp(_bdot_fwd, _bdot_bwd)


def _solve_powers(m):
    powers = [m]
    for _ in range(max(1, (m.shape[1] - 1).bit_length()) - 1):
        powers.append(_bdot_raw(powers[-1], powers[-1], "nn", 1))
    return powers


def _solve_fwd(m, rhs):
    powers = _solve_powers(m)
    x = rhs
    for p in powers:
        x = x + _bdot_raw(p, x, "nn", 1)
    return x, (powers, x)


def _solve_bwd(res, g):
    powers, x = res
    for p in powers:
        g = g + _bdot_raw(p, g, "tn", 1)
    return _bdot_raw(g, x, "nt", 1), g


@jax.custom_vjp
def _unit_lower_solve(m, rhs):
    return _solve_fwd(m, rhs)[0]


_unit_lower_solve.defvjp(_solve_fwd, _solve_bwd)


def _wkv_chunk(S0, r, lw, k, v, kap, a):
    G, C, N = r.shape
    row = lax.broadcasted_iota(jnp.int32, (C, C), 0)
    col = lax.broadcasted_iota(jnp.int32, (C, C), 1)
    incl = (col <= row).astype(F32)
    strict = (col < row).astype(F32)
    cum = _bdot(jnp.broadcast_to(incl, (G, C, C)), lw, "nn", 3)
    e_pos = jnp.exp(cum)
    e_neg = jnp.exp(-cum)
    al = -kap * jnp.exp(cum - lw)
    be = kap * a * e_neg
    kt = k * e_neg
    rt = r * e_pos
    bk = jnp.concatenate([be, kt], axis=1)
    mask = jnp.concatenate([jnp.concatenate([strict, strict], axis=1), jnp.concatenate([incl, incl], axis=1)], axis=0)
    m_all = _bdot(jnp.concatenate([al, rt], axis=1), bk, "nt", 3) * mask
    m_ab, m_ak = m_all[:, :C, :C], m_all[:, :C, C:]
    m_rb, m_rk = m_all[:, C:, :C], m_all[:, C:, C:]
    S0t = jnp.swapaxes(S0, 1, 2)
    sa = _unit_lower_solve(m_ab, _bdot(jnp.concatenate([al, m_ak], axis=2), jnp.concatenate([S0t, v], axis=1), "nn", 3))
    y =_bdot(jnp.concatenate([rt, m_rb, m_rk], axis=2), jnp.concatenate([S0t, sa, v], axis=1), "nn", 3)
    S1 = (S0 + _bdot(jnp.concatenate([sa, v], axis=1), bk, "tn", 1)) * e_pos[:, C - 1:C, :]
    return y, S1


def _split_heads(x):
    return jnp.stack([x[:, h * HEAD_DIM:(h + 1) * HEAD_DIM] for h in range(x.shape[1] // HEAD_DIM)], axis=0)


def _merge_heads(x):
    return jnp.concatenate([x[h] for h in range(x.shape[0])], axis=1)


def _seq_heads(ref):
    return jnp.concatenate([_split_heads(ref[s]) for s in range(ref.shape[0])], axis=0)


def _store_seq_heads(ref, x):
    heads = x.shape[0] // ref.shape[0]
    for s in range(ref.shape[0]):
        ref[s] = _merge_heads(x[s * heads:(s + 1) * heads])


def _wkv_fwd(r, lw, k, v, kap, a, B, S):
    C, H, N = WKV_CHUNK, RW_WIDTH // HEAD_DIM, HEAD_DIM
    nc = S // C
    Q = min(WKV_SEQS, B)

    def body(r_ref, lw_ref, k_ref, v_ref, kap_ref, a_ref, y_ref, st_ref, s_scr):
        @pl.when(pl.program_id(1) == 0)
        def _():
            s_scr[...] = jnp.zeros_like(s_scr)

        S0 = s_scr[...]
        for s in range(Q):
            st_ref[s, 0] = S0[s * H:(s + 1) * H]
        args = [_seq_heads(ref) for ref in (r_ref, lw_ref, k_ref, v_ref, kap_ref, a_ref)]
        y, S1 = _wkv_chunk(S0, *args)
        s_scr[...] = S1
        _store_seq_heads(y_ref, y)

    row_spec = pl.BlockSpec((Q, C, RW_WIDTH), lambda b, c: (b, c, 0))
    seqs = lambda t: t.reshape(B, S, RW_WIDTH)
    y, states = pl.pallas_call(
        body, name="wkv_fwd",
        grid=(B // Q, nc),
        in_specs=[row_spec] * 6,
        out_specs=[row_spec, pl.BlockSpec((Q, 1, H, N, N), lambda b, c: (b, c, 0, 0, 0))],
        out_shape=[jax.ShapeDtypeStruct((B, S, RW_WIDTH), F32), jax.ShapeDtypeStruct((B, nc, H, N, N), F32)],
        scratch_shapes=[pltpu.VMEM((Q * H, N, N), F32)],
        compiler_params=_params(("arbitrary", "arbitrary")),
    )(*map(seqs, (r, lw, k, v, kap, a)))
    return y.reshape(B * S, RW_WIDTH), states


def _wkv_bwd(r, lw, k, v, kap, a, states, dy, B, S):
    C, H, N = WKV_CHUNK, RW_WIDTH // HEAD_DIM, HEAD_DIM
    nc = S // C
    Q = min(WKV_SEQS, B)

    def body(r_ref, lw_ref, k_ref, v_ref, kap_ref, a_ref, st_ref, dy_ref,
             dr_ref, dlw_ref, dk_ref, dv_ref, dkap_ref, da_ref, ds_scr):
        @pl.when(pl.program_id(1) == 0)
        def _():
            ds_scr[...] = jnp.zeros_like(ds_scr)

        args = [_seq_heads(ref) for ref in (r_ref, lw_ref, k_ref, v_ref, kap_ref, a_ref)]
        S0 = jnp.concatenate([st_ref[s, 0] for s in range(Q)], axis=0)
        _, vjp = jax.vjp(_wkv_chunk, S0, *args)
        g = vjp((_seq_heads(dy_ref), ds_scr[...]))
        ds_scr[...] = g[0]
        for ref, gv in zip((dr_ref, dlw_ref, dk_ref, dv_ref, dkap_ref, da_ref), g[1:]):
            _store_seq_heads(ref, gv)

    row_spec = pl.BlockSpec((Q, C, RW_WIDTH), lambda b, c: (b, nc - 1 - c, 0))
    st_spec = pl.BlockSpec((Q, 1, H, N, N), lambda b, c: (b, nc - 1 - c, 0, 0, 0))
    seqs = lambda t: t.reshape(B, S, RW_WIDTH)
    res = pl.pallas_call(
        body, name="wkv_bwd",
        grid=(B // Q, nc),
        in_specs=[row_spec] * 6 + [st_spec, row_spec],
        out_specs=[row_spec] * 6,
        out_shape=[jax.ShapeDtypeStruct((B, S, RW_WIDTH), F32)] * 6,
        scratch_shapes=[pltpu.VMEM((Q * H, N, N), F32)],
        compiler_params=_params(("arbitrary", "arbitrary"), vmem=WKV_BWD_VMEM),
    )(*map(seqs, (r, lw, k, v, kap, a)), states, seqs(dy))
    return [t.reshape(B * S, RW_WIDTH) for t in res]


HBM = pl.BlockSpec(memory_space=pl.ANY)


def _place():
    return lax.axis_index("x"), lax.axis_index("y"), lax.axis_index("c")


def _other_chips(x, y):
    return [(1 - x, y), (x, 1 - y), (1 - x, 1 - y)]


def _all_gather_chips(shards):
    n = len(shards)

    def body(*refs):
        ins, outs = refs[:n], refs[n:2 * n]
        ici_send, ici_recv, d2d_send, d2d_recv, local = refs[2 * n:]
        x, y, c = _place()
        me = 2 * x + y
        sib = (x, y, 1 - c)
        chips = _other_chips(x, y)
        started, copies = [], []
        for w in range(n):
            cp = pltpu.make_async_copy(ins[w].at[c], outs[w].at[me, c], local.at[w])
            cp.start()
            copies.append(cp)
            for j, (px, py) in enumerate(chips):
                rd = pltpu.make_async_remote_copy(
                    src_ref=ins[w].at[c], dst_ref=outs[w].at[me, c], send_sem=ici_send.at[3 * w + j],
                    recv_sem=ici_recv.at[3 * w + j], device_id=(px, py, c), device_id_type=MESH)
                rd.start()
                started.append(rd)
            rd = pltpu.make_async_remote_copy(
                src_ref=ins[w].at[c], dst_ref=outs[w].at[me, c], send_sem=d2d_send.at[4 * w + 3],
                recv_sem=d2d_recv.at[4 * w + 3], device_id=sib, device_id_type=MESH)
            rd.start()
            started.append(rd)
        for w in range(n):
            for j, (px, py) in enumerate(chips):
                src = 2 * px + py
                pltpu.make_async_remote_copy(
                    src_ref=ins[w].at[c], dst_ref=outs[w].at[src, c], send_sem=ici_send.at[3 * w + j],
                    recv_sem=ici_recv.at[3 * w + j], device_id=(px, py, c), device_id_type=MESH).wait_recv()
                rd = pltpu.make_async_remote_copy(
                    src_ref=outs[w].at[src, c], dst_ref=outs[w].at[src, c], send_sem=d2d_send.at[4 * w + j],
                    recv_sem=d2d_recv.at[4 * w + j], device_id=sib, device_id_type=MESH)
                rd.start()
                started.append(rd)
        for w in range(n):
            for j, (px, py) in enumerate(chips):
                pltpu.make_async_remote_copy(
                    src_ref=ins[w].at[c], dst_ref=outs[w].at[2 * px + py, 1 - c], send_sem=d2d_send.at[4 * w + j],
                    recv_sem=d2d_recv.at[4 * w + j], device_id=sib, device_id_type=MESH).wait_recv()
            pltpu.make_async_remote_copy(
                src_ref=ins[w].at[c], dst_ref=outs[w].at[me, 1 - c], send_sem=d2d_send.at[4 * w + 3],
                recv_sem=d2d_recv.at[4 * w + 3], device_id=sib, device_id_type=MESH).wait_recv()
        for rd in started:
            rd.wait_send()
        for cp in copies:
            cp.wait()

    return pl.pallas_call(
        body, name="gather_weights",
        in_specs=[HBM] * n, out_specs=[HBM] * n,
        out_shape=[jax.ShapeDtypeStruct((N_CHIPS,) + s.shape, s.dtype) for s in shards],
        scratch_shapes=[pltpu.SemaphoreType.DMA((3 * n,)), pltpu.SemaphoreType.DMA((3 * n,)),
                        pltpu.SemaphoreType.DMA((4 * n,)), pltpu.SemaphoreType.DMA((4 * n,)),
                        pltpu.SemaphoreType.DMA((n,))],
        compiler_params=pltpu.CompilerParams(has_side_effects=True),
    )(*shards)


def _pair_split(grads):
    n = len(grads)

    def body(*refs):
        ins, theirs = refs[:n], refs[n:2 * n]
        send, recv = refs[2 * n:]
        x, y, c = _place()
        sib = (x, y, 1 - c)
        rds = []
        for w in range(n):
            rd = pltpu.make_async_remote_copy(
                src_ref=ins[w].at[:, 1 - c], dst_ref=theirs[w], send_sem=send.at[w], recv_sem=recv.at[w],
                device_id=sib, device_id_type=MESH)
            rd.start()
            rds.append(rd)
        for rd in rds:
            rd.wait_recv()
        for rd in rds:
            rd.wait_send()

    return pl.pallas_call(
        body, name="grad_pair_split",
        in_specs=[HBM] * n, out_specs=[HBM] * n,
        out_shape=[jax.ShapeDtypeStruct((g.shape[0],) + g.shape[2:], g.dtype) for g in grads],
        scratch_shapes=[pltpu.SemaphoreType.DMA((n,)), pltpu.SemaphoreType.DMA((n,))],
        compiler_params=pltpu.CompilerParams(has_side_effects=True),
    )(*grads)


def _chip_scatter(parts):
    n = len(parts)

    def body(*refs):
        ins, outs = refs[:n], refs[n:2 * n]
        send, recv = refs[2 * n:]
        x, y, c = _place()
        me = 2 * x + y
        rds = []
        for w in range(n):
            for j, (px, py) in enumerate(_other_chips(x, y)):
                s = 3 * w + j
                rd = pltpu.make_async_remote_copy(
                    src_ref=ins[w].at[2 * px + py], dst_ref=outs[w].at[j], send_sem=send.at[s], recv_sem=recv.at[s],
                    device_id=(px, py, c), device_id_type=MESH)
                rd.start()
                rds.append(rd)
        for w in range(n):
            for j, (px, py) in enumerate(_other_chips(x, y)):
                s = 3 * w + j
                pltpu.make_async_remote_copy(
                    src_ref=ins[w].at[me], dst_ref=outs[w].at[j], send_sem=send.at[s], recv_sem=recv.at[s],
                    device_id=(px, py, c), device_id_type=MESH).wait_recv()
        for rd in rds:
            rd.wait_send()

    return pl.pallas_call(
        body, name="grad_chip_scatter",
        in_specs=[HBM] * n, out_specs=[HBM] * n,
        out_shape=[jax.ShapeDtypeStruct((N_CHIPS - 1,) + p.shape[1:], p.dtype) for p in parts],
        scratch_shapes=[pltpu.SemaphoreType.DMA((3 * n,)), pltpu.SemaphoreType.DMA((3 * n,))],
        compiler_params=pltpu.CompilerParams(has_side_effects=True),
    )(*parts)


def _pair_join(bufs):
    n = len(bufs)

    def body(*refs):
        ins, outs = refs[:n], refs[n:2 * n]
        send, recv = refs[2 * n:]
        x, y, c = _place()
        sib = (x, y, 1 - c)
        rds = []
        for w in range(n):
            rd = pltpu.make_async_remote_copy(
                src_ref=ins[w].at[c], dst_ref=outs[w].at[c], send_sem=send.at[w], recv_sem=recv.at[w],
                device_id=sib, device_id_type=MESH)
            rd.start()
            rds.append(rd)
        for w in range(n):
            pltpu.make_async_remote_copy(
                src_ref=ins[w].at[c], dst_ref=outs[w].at[1 - c], send_sem=send.at[w], recv_sem=recv.at[w],
                device_id=sib, device_id_type=MESH).wait_recv()
        for rd in rds:
            rd.wait_send()

    return pl.pallas_call(
        body, name="grad_pair_join",
        in_specs=[HBM] * n, out_specs=[HBM] * n,
        out_shape=[jax.ShapeDtypeStruct(b.shape, b.dtype) for b in bufs],
        input_output_aliases={w: w for w in range(n)},
        scratch_shapes=[pltpu.SemaphoreType.DMA((n,)), pltpu.SemaphoreType.DMA((n,))],
        compiler_params=pltpu.CompilerParams(has_side_effects=True),
    )(*bufs)


def _all_reduce_small(packed):
    R = packed.shape[0]

    def body(x_ref, o_ref, buf, send, recv):
        x, y, c = _place()
        me = 4 * x + 2 * y + c
        buf[me] = x_ref[...]
        rds = []
        for rel in range(1, N_DEV):
            fx, fy, fc = (rel >> 2) & 1, (rel >> 1) & 1, rel & 1
            peer = (1 - x if fx else x, 1 - y if fy else y, 1 - c if fc else c)
            rd = pltpu.make_async_remote_copy(
                src_ref=x_ref, dst_ref=buf.at[me], send_sem=send.at[rel - 1], recv_sem=recv.at[rel - 1],
                device_id=peer, device_id_type=MESH)
            rd.start()
            rds.append((rd, peer))
        for rel in range(1, N_DEV):
            rd, (px, py, pc) = rds[rel - 1]
            pltpu.make_async_remote_copy(
                src_ref=x_ref, dst_ref=buf.at[4 * px + 2 * py + pc], send_sem=send.at[rel - 1], recv_sem=recv.at[rel - 1],
                device_id=(px, py, pc), device_id_type=MESH).wait_recv()
        for rd, _ in rds:
            rd.wait_send()
        total = buf[0]
        for d in range(1, N_DEV):
            total = total + buf[d]
        o_ref[...] = total

    return pl.pallas_call(
        body, name="all_reduce_small",
        in_specs=[pl.BlockSpec(memory_space=pltpu.VMEM)],
        out_specs=pl.BlockSpec(memory_space=pltpu.VMEM),
        out_shape=jax.ShapeDtypeStruct(packed.shape, F32),
        scratch_shapes=[pltpu.VMEM((N_DEV, R, LANES), F32), pltpu.SemaphoreType.DMA((N_DEV - 1,)),
                        pltpu.SemaphoreType.DMA((N_DEV - 1,))],
        compiler_params=pltpu.CompilerParams(has_side_effects=True),
    )(packed)


def _pair_sum(name, split, theirs, core):
    n_chip, _, Rh, C = split.shape
    tile = _div_tile(Rh, 256, 2 * SUBLANES)
    nt = Rh // tile

    def body(core_ref, a_ref, b_ref, o_ref):
        o_ref[...] = (a_ref[...] + b_ref[...]).astype(o_ref.dtype)

    return pl.pallas_call(
        body, name=name,
        grid_spec=pltpu.PrefetchScalarGridSpec(
            num_scalar_prefetch=1,
            grid=(n_chip, nt),
            in_specs=[pl.BlockSpec((None, None, tile, C), lambda j, i, core_ref: (j, core_ref[0], i, 0)),
                      pl.BlockSpec((None, tile, C), lambda j, i, core_ref: (j, i, 0))],
            out_specs=pl.BlockSpec((None, tile, C), lambda j, i, core_ref: (j, i, 0)),
        ),
        out_shape=jax.ShapeDtypeStruct((n_chip, Rh, C), BF16),
        compiler_params=_params(("parallel", "parallel")),
    )(core, split, theirs)


def _chip_sum(name, own, landed, core):
    n_in, Rh, C = landed.shape
    tile = _div_tile(Rh, 256, 2 * SUBLANES)

    def body(core_ref, *refs):
        total = refs[0][...].astype(F32)
        for ref in refs[1:n_in + 1]:
            total = total + ref[...].astype(F32)
        refs[n_in + 1][...] = total

    slot = lambda j: pl.BlockSpec((None, tile, C), lambda i, core_ref: (j, i, 0))
    return pl.pallas_call(
        body, name=name,
        grid_spec=pltpu.PrefetchScalarGridSpec(
            num_scalar_prefetch=1,
            grid=(Rh // tile,),
            in_specs=[pl.BlockSpec((None, tile, C), lambda i, core_ref: (core_ref[1], i, 0))]
                     + [slot(j) for j in range(n_in)],
            out_specs=pl.BlockSpec((None, tile, C), lambda i, core_ref: (core_ref[0], i, 0)),
        ),
        out_shape=jax.ShapeDtypeStruct((2, Rh, C), F32),
        compiler_params=_params(("parallel",)),
    )(core, own, *([landed] * n_in))


def _adamw(name, w, g, m, v):
    R, C = w.shape
    tile = _div_tile(R, 256, SUBLANES)
    c1 = 1.0 / (1.0 - ADAM_B1 ** ADAM_STEP)
    c2 = 1.0 / (1.0 - ADAM_B2 ** ADAM_STEP)

    def body(w_ref, g_ref, m_ref, v_ref, d_ref, nm_ref, nv_ref):
        g_ = g_ref[...]
        nm = ADAM_B1 * m_ref[...] + (1.0 - ADAM_B1) * g_
        nv = ADAM_B2 * v_ref[...] + (1.0 - ADAM_B2) * (g_ * g_)
        d_ref[...] = -ADAM_LR * ((nm * c1) / (jnp.sqrt(nv * c2) + ADAM_EPS) + ADAM_WD * w_ref[...])
        nm_ref[...] = nm
        nv_ref[...] = nv

    spec = pl.BlockSpec((tile, C), lambda i: (i, 0))
    return pl.pallas_call(
        body, name=name,
        grid=(R // tile,),
        in_specs=[spec] * 4, out_specs=[spec] * 3,
        out_shape=[jax.ShapeDtypeStruct((R, C), F32)] * 3,
        compiler_params=_params(("parallel",)),
    )(w, g, m, v)


SMALL =["norm_mix_pre", "b_gate", "mu_rw", "w0", "a0", "k_k", "k_a", "r_k", "lnx_w", "lnx_b",
         "norm_mix_post", "norm_ffn_pre", "norm_ffn_post"]
BIG = ["w_in", "w_up", "a_up", "g_up", "w_sb_out", "w_rw_out", "w_o", "w_ffn_gate", "w_ffn_up", "w_ffn_down"]
ROW_SHARDED = ("w_o", "w_ffn_down")
ORDER = ["norm_mix_pre", "w_in", "b_gate", "mu_rw", "w0", "w_up", "a0", "a_up", "g_up", "k_k", "k_a", "r_k",
         "lnx_w", "lnx_b", "w_sb_out", "w_rw_out", "w_o", "norm_mix_post", "norm_ffn_pre", "w_ffn_gate",
         "w_ffn_up", "w_ffn_down", "norm_ffn_post"]


def _pack_small(vals, extra_rows=0):
    rows = jnp.concatenate([vals[n].reshape(-1, LANES) for n in SMALL], axis=0)
    pad = (-(rows.shape[0] + extra_rows)) % SUBLANES + extra_rows
    return jnp.pad(rows, ((0, pad), (0, 0)))


def _unpack_small(packed, shapes):
    out, r = {}, 0
    for n in SMALL:
        size = 1
        for s in shapes[n]:
            size *= s
        out[n] = packed[r:r + size // LANES].reshape(shapes[n])
        r += size // LANES
    return out


def kernel(x, norm_mix_pre, w_in, b_gate, mu_rw, w0, w_up, a0, a_up, g_up, k_k, k_a, r_k, lnx_w, lnx_b, w_sb_out, w_rw_out, w_o, norm_mix_post, norm_ffn_pre, w_ffn_gate, w_ffn_up, w_ffn_down, norm_ffn_post, loss_target, m_norm_mix_pre, m_w_in, m_b_gate, m_mu_rw, m_w0, m_w_up, m_a0, m_a_up, m_g_up, m_k_k, m_k_a, m_r_k, m_lnx_w, m_lnx_b, m_w_sb_out, m_w_rw_out, m_w_o, m_norm_mix_post, m_norm_ffn_pre, m_w_ffn_gate, m_w_ffn_up, m_w_ffn_down, m_norm_ffn_post, v_norm_mix_pre, v_w_in, v_b_gate, v_mu_rw, v_w0, v_w_up, v_a0, v_a_up, v_g_up, v_k_k, v_k_a, v_r_k, v_lnx_w, v_lnx_b, v_w_sb_out, v_w_rw_out, v_w_o, v_norm_mix_post, v_norm_ffn_pre, v_w_ffn_gate, v_w_ffn_up, v_w_ffn_down, v_norm_ffn_post):
    W = dict(norm_mix_pre=norm_mix_pre, w_in=w_in, b_gate=b_gate, mu_rw=mu_rw, w0=w0, w_up=w_up, a0=a0, a_up=a_up,
             g_up=g_up, k_k=k_k, k_a=k_a, r_k=r_k, lnx_w=lnx_w, lnx_b=lnx_b, w_sb_out=w_sb_out, w_rw_out=w_rw_out,
             w_o=w_o, norm_mix_post=norm_mix_post, norm_ffn_pre=norm_ffn_pre, w_ffn_gate=w_ffn_gate,
             w_ffn_up=w_ffn_up, w_ffn_down=w_ffn_down, norm_ffn_post=norm_ffn_post)
    Mo = dict(norm_mix_pre=m_norm_mix_pre, w_in=m_w_in, b_gate=m_b_gate, mu_rw=m_mu_rw, w0=m_w0, w_up=m_w_up, a0=m_a0,
              a_up=m_a_up, g_up=m_g_up, k_k=m_k_k, k_a=m_k_a, r_k=m_r_k, lnx_w=m_lnx_w, lnx_b=m_lnx_b,
              w_sb_out=m_w_sb_out, w_rw_out=m_w_rw_out, w_o=m_w_o, norm_mix_post=m_norm_mix_post,
              norm_ffn_pre=m_norm_ffn_pre, w_ffn_gate=m_w_ffn_gate, w_ffn_up=m_w_ffn_up, w_ffn_down=m_w_ffn_down,
              norm_ffn_post=m_norm_ffn_post)
    Vo = dict(norm_mix_pre=v_norm_mix_pre, w_in=v_w_in, b_gate=v_b_gate, mu_rw=v_mu_rw, w0=v_w0, w_up=v_w_up, a0=v_a0,
              a_up=v_a_up, g_up=v_g_up, k_k=v_k_k, k_a=v_k_a, r_k=v_r_k, lnx_w=v_lnx_w, lnx_b=v_lnx_b,
              w_sb_out=v_w_sb_out, w_rw_out=v_w_rw_out, w_o=v_w_o, norm_mix_post=v_norm_mix_post,
              norm_ffn_pre=v_norm_ffn_pre, w_ffn_gate=v_w_ffn_gate, w_ffn_up=v_w_ffn_up, w_ffn_down=v_w_ffn_down,
              norm_ffn_post=v_norm_ffn_post)
    shapes = {n: W[n].shape for n in ORDER}
    B, S, D = x.shape
    T = B * S
    x2 = x.reshape(T, D)
    tgt = loss_target.reshape(T, D)
    vec = {n: W[n].reshape(1, -1) for n in SMALL}

    work = lambda t, n: t[0] if n in ROW_SHARDED else jnp.swapaxes(t[0], 0, 1)
    halved = [work(W[n], n).astype(BF16) for n in BIG]
    halved = [h.reshape(2, h.shape[0] // 2, h.shape[1]) for h in halved]
    full = {n: gth.reshape(-1, gth.shape[3]) for n, gth in zip(BIG, _all_gather_chips(halved))}
    w_in_t = full["w_in"]
    w_sb_t, w_rw_t, w_gt_t = w_in_t[:SB_COLS], w_in_t[SB_COLS:SB_COLS + RW_COLS], w_in_t[SB_COLS + RW_COLS:]
    lora_rows = {"w_up": 0, "a_up": 64, "g_up": 128}
    lora = {n: jnp.pad(full[n].T, ((r0, LORA_COLS - r0 - full[n].shape[1]), (0, 0))) for n, r0 in lora_rows.items()}
    mu = vec["mu_rw"]
    mu_parts = [mu[:, :512], mu[:, 512:1024], mu[:, 1024:1536], mu[:, 1536:]]
    b1, b2 = vec["b_gate"][:, :D], vec["b_gate"][:, D:]

    (h1,) = _rowwise("norm_mix_pre", _f_norm, [(x2, D, 0)], [vec["norm_mix_pre"]], [(D, BF16)], tile=512)
    p_sb = _mm("proj_sb", h1, w_sb_t, tb=True, out_dtype=BF16)
    p_rw = _mm("proj_rw", h1, w_rw_t, tb=True)
    p_gt = _mm("proj_gate", h1, w_gt_t, tb=True, out_dtype=BF16)
    o_sb = _attn_fwd(p_sb, B, S)
    pre_params = mu_parts + [vec["w0"], lora["w_up"], vec["a0"], lora["a_up"], lora["g_up"], vec["k_k"], vec["k_a"]]
    r_, lw_, k2_, v_, kap_, a_, g_ = _rw_pre(p_rw, pre_params, S)
    y_wkv, states = _wkv_fwd(r_, lw_, k2_, v_, kap_, a_, B, S)
    post_rows = [(y_wkv, 512, 0), (r_, 512, 0), (k2_, 512, 0), (v_, 512, 0), (g_, 512, 0)]
    post_params = [vec["lnx_w"], vec["lnx_b"], vec["r_k"]]
    (o_rw,) = _rowwise("rw_post", _f_rwpost, post_rows, post_params, [(512, BF16)])
    m1 = _mm("mix_sb_out", o_sb, full["w_sb_out"], tb=True, out_dtype=BF16)
    m2 = _mm("mix_rw_out", o_rw, full["w_rw_out"], tb=True, out_dtype=BF16)
    merge_rows = [(p_gt, D, 0), (p_gt, D, 1), (m1, D, 0), (m2, D, 0)]
    (merged,) = _rowwise("merge", _f_merge, merge_rows, [b1, b2], [(D, BF16)])
    u = _mm("mix_out", merged, full["w_o"])
    post1_params = [vec["norm_mix_post"], vec["norm_ffn_pre"]]
    x1, h2 = _rowwise("post_mix", _f_post1, [(x2, D, 0), (u, D, 0)], post1_params, [(D, F32), (D, BF16)], tile=512)
    ag, au, sw = _mm_fused("ffn_in", [h2, h2], [full["w_ffn_gate"], full["w_ffn_up"]], [BF16] * 3, tb=True,
                           epilogue=lambda gu, _: (gu[0], gu[1], _f_swiglu(*gu)[0]))
    f = _mm("ffn_down", sw, full["w_ffn_down"])
    loss_part, dx1, df, dg4 = _loss_head(x1, f, tgt, vec["norm_ffn_post"])

    gbig, gsmall = {}, {"norm_ffn_post": dg4}
    gbig["w_ffn_down"] = _mm("g_ffn_down", sw, df, ta=True)

    def swiglu_back(dsw, gu):
        return jax.vjp(_f_swiglu, *gu)[1]((dsw[0],))

    dag, dau = _mm_fused("ffn_back", [df], [full["w_ffn_down"]], [BF16] * 2, tb=True, extras=[ag, au],
                         epilogue=swiglu_back)
    (dh2,) = _mm_fused("d_h2", [dag, dau], [full["w_ffn_gate"], full["w_ffn_up"]], [F32], add=True)
    gbig["w_ffn_gate"] = _mm("g_ffn_gate", dag, h2, ta=True)
    gbig["w_ffn_up"] = _mm("g_ffn_up", dau, h2, ta=True)
    (dx_res, du), (dg2, dg3) = _rowwise_vjp("post_mix_bwd", _f_post1, [(x2, D, 0), (u, D, 0)], post1_params,
                                            [[dx1], [dh2]], [True, True], [True, True], bf16_rows=(1,))
    gsmall["norm_mix_post"], gsmall["norm_ffn_pre"] = dg2, dg3
    dmerged = _mm("d_merged", du, full["w_o"], tb=True, out_dtype=BF16)
    gbig["w_o"] = _mm("g_w_o", merged, du, ta=True)
    (dpg1, dpg2, dm1, dm2), (db1, db2) = _rowwise_vjp("merge_bwd", _f_merge, merge_rows, [b1, b2], [[dmerged]],
                                                      [True] * 4, [True, True], bf16_rows=(0, 1, 2, 3))
    gsmall["b_gate"] = jnp.concatenate([db1, db2], axis=1)
    do_sb = _mm("d_o_sb", dm1, full["w_sb_out"])
    do_rw = _mm("d_o_rw", dm2, full["w_rw_out"])
    gbig["w_sb_out"] = _mm("g_sb_out", dm1, o_sb, ta=True)
    gbig["w_rw_out"] = _mm("g_rw_out", dm2, o_rw, ta=True)
    (dy_wkv, dr_a, dk2_a, dv_a, dg_), (dlnx_w, dlnx_b, dr_k) = _rowwise_vjp(
        "rw_post_bwd", _f_rwpost, post_rows, post_params, [[do_rw]], [True] * 5, [True] * 3)
    gsmall["lnx_w"], gsmall["lnx_b"], gsmall["r_k"] = dlnx_w, dlnx_b, dr_k
    dr_b, dlw, dk2_b, dv_b, dkap, da = _wkv_bwd(r_, lw_, k2_, v_, kap_, a_, states, dy_wkv, B, S)
    pre_cts = [[dr_a, dr_b], [dlw], [dk2_a, dk2_b], [dv_a, dv_b], [dkap], [da], [dg_]]
    dp_rw, dpre_params = _rw_pre_bwd(p_rw, pre_params, pre_cts, S)
    gsmall["mu_rw"] = jnp.concatenate(dpre_params[:4], axis=1)
    gsmall["w0"], gsmall["a0"], gsmall["k_k"], gsmall["k_a"] = dpre_params[4], dpre_params[6], dpre_params[9], dpre_params[10]
    glora = {"w_up": dpre_params[5][0:64].T, "a_up": dpre_params[7][64:128].T, "g_up": dpre_params[8][128:256].T}
    dq, dk, dv = _attn_bwd(p_sb, o_sb, do_sb, B, S)
    (dh1,) = _mm_fused("d_h1_sb", [dq, dk, dv], [w_sb_t[:512], w_sb_t[512:1024], w_sb_t[1024:]], [F32], add=True)
    dh1 = _mm("d_h1_rw", dp_rw, w_rw_t, acc=dh1)
    (dh1,) = _mm_fused("d_h1_gate", [dpg1, dpg2], [w_gt_t[:D], w_gt_t[D:]], [F32], add=True,
                       extras=[dh1], epilogue=lambda p, e: (p[0] + e[0],))
    gbig["w_in"] = jnp.concatenate(
        [_mm("g_in_" + tag, d, h1, ta=True)
         for tag, d in (("q", dq), ("k", dk), ("v", dv), ("rw", dp_rw), ("g1", dpg1), ("g2", dpg2))], axis=0)
    (grad_x2,), (dg1,) = _rowwise_vjp("norm_mix_pre_bwd", _f_norm, [(x2, D, 0)], [vec["norm_mix_pre"]], [[dh1]],
                                      [True], [True], add_to={0: dx_res})
    gsmall["norm_mix_pre"] = dg1
    gbig.update(glora)

    split = [gbig[n].reshape(N_CHIPS, 2, gbig[n].shape[0] // (2 * N_CHIPS), gbig[n].shape[1]) for n in BIG]
    core = jnp.stack([lax.axis_index("c"), 2 * lax.axis_index("x") + lax.axis_index("y")]).astype(jnp.int32)
    theirs = _pair_split(split)
    chip_sums = [_pair_sum("pair_sum_" + n, a, b, core) for n, a, b in zip(BIG, split, theirs)]
    landed = _chip_scatter(chip_sums)
    joined = _pair_join([_chip_sum("chip_sum_" + n, own, got, core) for n, own, got in zip(BIG, chip_sums, landed)])
    grads = {n: j.reshape(-1, j.shape[2]) for n, j in zip(BIG, joined)}

    small_local = _pack_small({n: gsmall[n] for n in SMALL}, extra_rows=1)
    loss_row = small_local.shape[0] - 1
    small_local = small_local.at[loss_row].set(loss_part[0])
    small_sum = _all_reduce_small(small_local)
    loss = small_sum[loss_row, 0]

    delta, new_m, new_v = {}, {}, {}
    unwork = lambda t, n: (t if n in ROW_SHARDED else jnp.swapaxes(t, 0, 1))[None]
    for n in BIG:
        d_, m_, v2_ = _adamw("adamw_" + n, work(W[n], n), grads[n], work(Mo[n], n), work(Vo[n], n))
        delta[n], new_m[n], new_v[n], grads[n] = (unwork(t, n) for t in (d_, m_, v2_, grads[n]))
    pk = lambda src: _pack_small({n: src[n] for n in SMALL}, extra_rows=1)
    d_s, m_s, v_s = _adamw("adamw_small", pk(W), small_sum.at[loss_row].set(0.0), pk(Mo), pk(Vo))
    for dst, packed in ((grads, small_sum), (delta, d_s), (new_m, m_s), (new_v, v_s)):
        dst.update(_unpack_small(packed, shapes))

    return (loss, grad_x2.reshape(B, S, D), *[grads[n] for n in ORDER], *[delta[n] for n in ORDER],
            *[new_m[n] for n in ORDER], *[new_v[n] for n in ORDER])
```

```python
import functools

import jax
import jax.numpy as jnp
from jax import lax
from jax.experimental import pallas as pl
from jax.experimental.pallas import tpu as pltpu

F32 = jnp.float32
BF16 = jnp.bfloat16
MESH = pl.DeviceIdType.MESH

D_MODEL = 1024
SB_HEADS = 8
HEAD_DIM = 64
SB_WIDTH = SB_HEADS * HEAD_DIM
RW_WIDTH = 512
LORA_COLS = 256
SB_COLS = 3 * SB_WIDTH
RW_COLS = 3 * RW_WIDTH + LORA_COLS
GATE_COLS = 2 * D_MODEL
D_FF = 2816
RMS_EPS = 1e-6
GN_EPS = HEAD_DIM * 1e-5
WKV_CHUNK = 64
WKV_SEQS = 4
ATTN_QUERIES = 512
ATTN_KEYS = 128
ATTN_DEAD = -120.0
LANES = 128
SUBLANES = 8
N_CHIPS = 4
N_DEV = 8

ADAM_LR = 0.001
ADAM_B1 = 0.9
ADAM_B2 = 0.999
ADAM_EPS = 1e-08
ADAM_WD = 0.01
ADAM_STEP = 10

VMEM_LIMIT = 48 * 1024 * 1024
WKV_BWD_VMEM = 58 * 1024 * 1024


def _params(sem=None, vmem=VMEM_LIMIT, **kw):
    if sem is not None:
        kw["dimension_semantics"] = sem
    return pltpu.CompilerParams(vmem_limit_bytes=vmem, **kw)


def _div_tile(dim, pref, mult=LANES):
    if dim <= pref:
        return dim
    t = pref - pref % mult
    while t >= mult:
        if dim % t == 0:
            return t
        t -= mult
    return dim


def _dot(a, b, dims):
    return lax.dot_general(a, b, (dims, ((), ())), preferred_element_type=F32)


def _mm(name, a, b, *, ta=False, tb=False, acc=None, out_dtype=F32):
    if ta:
        K, M = a.shape
    else:
        M, K = a.shape
    N = b.shape[0] if tb else b.shape[1]
    if ta:
        tm, tn, tk = _div_tile(M, 1408), _div_tile(N, 1408), _div_tile(K, 512)
    else:
        tm, tn, tk = _div_tile(M, 512), _div_tile(N, 1408), _div_tile(K, 1408)
    nk = K // tk
    dims = ((0,) if ta else (1,), (1,) if tb else (0,))
    has_acc = acc is not None

    def body(*refs):
        a_ref, b_ref = refs[0], refs[1]
        part = _dot(a_ref[...].astype(BF16), b_ref[...].astype(BF16), dims)
        if nk == 1:
            o_ref = refs[-1]
            o_ref[...] = (part + refs[2][...] if has_acc else part).astype(o_ref.dtype)
            return
        o_ref, scr = refs[-2], refs[-1]
        k = pl.program_id(2)

        @pl.when(k == 0)
        def _():
            scr[...] = part + refs[2][...] if has_acc else part

        @pl.when(k > 0)
        def _():
            scr[...] += part

        @pl.when(k == nk - 1)
        def _():
            o_ref[...] = scr[...].astype(o_ref.dtype)

    a_spec = pl.BlockSpec((tk, tm), lambda i, j, k: (k, i)) if ta else pl.BlockSpec((tm, tk), lambda i, j, k: (i, k))
    b_spec = pl.BlockSpec((tn, tk), lambda i, j, k: (j, k)) if tb else pl.BlockSpec((tk, tn), lambda i, j, k: (k, j))
    o_spec = pl.BlockSpec((tm, tn), lambda i, j, k: (i, j))
    return pl.pallas_call(
        body, name=name,
        grid=(M // tm, N // tn, nk),
        in_specs=[a_spec, b_spec] + ([o_spec] if has_acc else []),
        out_specs=o_spec,
        out_shape=jax.ShapeDtypeStruct((M, N), out_dtype),
        scratch_shapes=[pltpu.VMEM((tm, tn), F32)] if nk > 1 else [],
        compiler_params=_params(("parallel", "parallel", "arbitrary")),
    )(*([a, b] + ([acc] if has_acc else [])))


def _mm_fused(name, lhs, rhs, outs, *, tb=False, add=False, extras=(), epilogue=None):
    M, K = lhs[0].shape
    N = rhs[0].shape[0] if tb else rhs[0].shape[1]
    tm, tn, tk = _div_tile(M, 512), _div_tile(N, 1408), _div_tile(K, 1408)
    nk = K // tk
    n_l, n_e, n_o = len(lhs), len(extras), len(outs)
    n_acc = 1 if add else n_l
    dims = ((1,), (1,) if tb else (0,))

    def body(*refs):
        l_refs, r_refs = refs[:n_l], refs[n_l:2 * n_l]
        e_refs = refs[2 * n_l:2 * n_l + n_e]
        o_refs = refs[2 * n_l + n_e:2 * n_l + n_e + n_o]
        scr = refs[2 * n_l + n_e + n_o:]
        parts = [_dot(l[...].astype(BF16), r[...].astype(BF16), dims) for l, r in zip(l_refs, r_refs)]
        if add:
            parts = [functools.reduce(lambda u, v: u + v, parts)]

        def finish(vals):
            res = epilogue(vals, [e[...].astype(F32) for e in e_refs]) if epilogue else vals
            for ref, val in zip(o_refs, res):
                ref[...] = val.astype(ref.dtype)

        if nk == 1:
            finish(parts)
            return
        k = pl.program_id(2)

        @pl.when(k == 0)
        def _():
            for s, part in zip(scr, parts):
                s[...] = part

        @pl.when(k > 0)
        def _():
            for s, part in zip(scr, parts):
                s[...] += part

        @pl.when(k == nk - 1)
        def _():
            finish([s[...] for s in scr])

    a_spec = pl.BlockSpec((tm, tk), lambda i, j, k: (i, k))
    b_spec = pl.BlockSpec((tn, tk), lambda i, j, k: (j, k)) if tb else pl.BlockSpec((tk, tn), lambda i, j, k: (k, j))
    o_spec = pl.BlockSpec((tm, tn), lambda i, j, k: (i, j))
    return pl.pallas_call(
        body, name=name,
        grid=(M // tm, N // tn, nk),
        in_specs=[a_spec] * n_l + [b_spec] * n_l + [o_spec] * n_e,
        out_specs=[o_spec] * n_o,
        out_shape=[jax.ShapeDtypeStruct((M, N), dt) for dt in outs],
        scratch_shapes=[pltpu.VMEM((tm, tn), F32)] * (n_acc if nk > 1 else 0),
        compiler_params=_params(("parallel", "parallel", "arbitrary")),
    )(*lhs, *rhs, *extras)


def _row_spec(tile, width, colblk):
    return pl.BlockSpec((tile, width), lambda i: (i, colblk))


def _full_spec(shape):
    return pl.BlockSpec(shape, lambda i: (0,) * len(shape))


def _rowwise(name, fn, rows, params, outs, tile=256):
    T = rows[0][0].shape[0]
    tile = min(tile, T)
    n_r, n_p = len(rows), len(params)

    def body(*refs):
        r = [x[...].astype(F32) for x in refs[:n_r]]
        p = [x[...].astype(F32) for x in refs[n_r:n_r + n_p]]
        for o_ref, val in zip(refs[n_r + n_p:], fn(*r, *p)):
            o_ref[...] = val.astype(o_ref.dtype)

    return pl.pallas_call(
        body, name=name,
        grid=(T // tile,),
        in_specs=[_row_spec(tile, w, cb) for _, w, cb in rows] + [_full_spec(p.shape) for p in params],
        out_specs=[_row_spec(tile, w, 0) for w, _ in outs],
        out_shape=[jax.ShapeDtypeStruct((T, w), dt) for w, dt in outs],
        compiler_params=_params(("parallel",)),
    )(*([a for a, _, _ in rows] + list(params)))


def _rowwise_vjp(name, fn, rows, params, cts, need_rows, need_params, add_to=None, tile=256, bf16_rows=()):
    add_to = add_to or {}
    T = rows[0][0].shape[0]
    tile = min(tile, T)
    n_r, n_p = len(rows), len(params)
    ct_flat = [c for group in cts for c in group]
    ct_sizes = [len(group) for group in cts]
    add_idx = sorted(add_to)
    row_out = [i for i in range(n_r) if need_rows[i]]
    par_out = [i for i in range(n_p) if need_params[i]]
    n_ct, n_add = len(ct_flat), len(add_idx)

    def body(*refs):
        pos = 0
        r = [x[...].astype(F32) for x in refs[pos:pos + n_r]]
        pos += n_r
        p = [x[...].astype(F32) for x in refs[pos:pos + n_p]]
        pos += n_p
        ct_vals = [x[...].astype(F32) for x in refs[pos:pos + n_ct]]
        pos += n_ct
        adds = {i: x[...] for i, x in zip(add_idx, refs[pos:pos + n_add])}
        pos += n_add
        drow_refs = refs[pos:pos + len(row_out)]
        pos += len(row_out)
        dpar_refs = refs[pos:pos + len(par_out)]
        ct_in, q = [], 0
        for n in ct_sizes:
            ct_in.append(functools.reduce(lambda u, v: u + v, ct_vals[q:q + n]))
            q += n
        _, vjp = jax.vjp(fn, *r, *p)
        grads = vjp(tuple(ct_in))
        for ref, i in zip(drow_refs, row_out):
            g = grads[i]
            ref[...] = (g + adds[i] if i in adds else g).astype(ref.dtype)

        @pl.when(pl.program_id(0) == 0)
        def _():
            for ref in dpar_refs:
                ref[...] = jnp.zeros_like(ref)

        for ref, i in zip(dpar_refs, par_out):
            ref[...] += grads[n_r + i]

    ct_widths = [c.shape[1] for c in ct_flat]
    in_specs = ([_row_spec(tile, w, cb) for _, w, cb in rows] + [_full_spec(p.shape) for p in params]
                + [_row_spec(tile, w, 0) for w in ct_widths] + [_row_spec(tile, rows[i][1], 0) for i in add_idx])
    out_specs = [_row_spec(tile, rows[i][1], 0) for i in row_out] + [_full_spec(params[i].shape) for i in par_out]
    out_shape = ([jax.ShapeDtypeStruct((T, rows[i][1]), BF16 if i in bf16_rows else F32) for i in row_out]
                 + [jax.ShapeDtypeStruct(params[i].shape, F32) for i in par_out])
    res = pl.pallas_call(
        body, name=name,
        grid=(T // tile,),
        in_specs=in_specs, out_specs=out_specs, out_shape=out_shape,
        compiler_params=_params(("arbitrary",)),
    )(*([a for a, _, _ in rows] + list(params) + ct_flat + [add_to[i] for i in add_idx]))
    return res[:len(row_out)], res[len(row_out):]


def _sigmoid(x):
    return 0.5 * (jnp.tanh(0.5 * x) + 1.0)


def _softplus(x):
    return jnp.maximum(x, 0.0) + jnp.log(1.0 + jnp.exp(-jnp.abs(x)))


def _rms(x, g):
    return x * lax.rsqrt(jnp.mean(x * x, axis=-1, keepdims=True) + RMS_EPS) * g


def _segsum_impl(x):
    n, w = x.shape[-1], 2 * LANES
    r = lax.shift_right_logical(lax.broadcasted_iota(jnp.int32, (w, w), 0), 6)
    c = lax.shift_right_logical(lax.broadcasted_iota(jnp.int32, (w, w), 1), 6)
    bd = (r == c).astype(BF16)
    hi = x.astype(BF16)
    rest = x - hi.astype(F32)
    mid = rest.astype(BF16)
    lo = (rest - mid.astype(F32)).astype(BF16)
    nn = ((1,), (0,))
    blocks = [_dot(hi[:, j:j + w], bd, nn) + _dot(mid[:, j:j + w], bd, nn) + _dot(lo[:, j:j + w], bd, nn)
              for j in range(0, n, w)]
    return jnp.concatenate(blocks, axis=1)


@jax.custom_vjp
def _segsum(x):
    return _segsum_impl(x)


_segsum.defvjp(lambda x: (_segsum_impl(x), None), lambda _, g: (_segsum_impl(g),))


@jax.custom_vjp
def _mmb(a, w):
    return _dot(a.astype(BF16), w.astype(BF16), ((1,), (0,)))


def _mmb_fwd(a, w):
    return _mmb(a, w), (a, w)


def _mmb_bwd(res, g):
    a, w = res
    gb = g.astype(BF16)
    return _dot(gb, w.astype(BF16), ((1,), (1,))), _dot(a.astype(BF16), gb, ((0,), (0,)))


_mmb.defvjp(_mmb_fwd, _mmb_bwd)


def _f_norm(x, g):
    return (_rms(x, g),)


def _f_post1(x, u, g2, g3):
    x1 = x + _rms(u, g2)
    return x1, _rms(x1, g3)


def _f_swiglu(ag, au):
    return (ag * _sigmoid(ag) * au,)


def _f_merge(pg1, pg2, m1, m2, b1, b2):
    return (_sigmoid(pg1 + b1) * m1 + _sigmoid(pg2 + b2) * m2,)


def _f_out(x1, f, g4):
    return (x1 + _rms(f, g4),)


def _f_rwpre(pr, pk, pv, pz, qr, qk, qv, qz, mur, muk, muv, muz, w0, wup, a0, aup, gup, k_k, k_a):
    r = pr + (qr - pr) * mur
    k = pk + (qk - pk) * muk
    v = pv + (qv - pv) * muv
    z = pz + (qz - pz) * muz
    w_raw = w0 + _mmb(jnp.tanh(z), wup)
    lw = -jnp.exp(-_softplus(-w_raw) - 0.5)
    a = _sigmoid(a0 + _mmb(z, aup))
    g = _mmb(_sigmoid(z), gup)
    kk = k * k_k
    kap = kk * lax.rsqrt(jnp.maximum(_segsum(kk * kk), 1e-24))
    k2 = k * (1.0 + (a - 1.0) * k_a)
    return r, lw, k2, v, kap, a, g


def _f_rwpost(y, r, k2, v, g, lnx_w, lnx_b, r_k):
    inv = 1.0 / HEAD_DIM
    yc = y - _segsum(y) * inv
    var = _segsum(yc * yc) * inv
    yn = yc * lax.rsqrt(var + GN_EPS) * lnx_w + lnx_b
    bonus = _segsum(r * k2 * r_k) * v
    return ((yn + bonus) * g,)


RW_GROUPS = (0, 512, 1024, 1536, RW_COLS)


def _column_groups(p):
    return [p[:, a:b] for a, b in zip(RW_GROUPS[:-1], RW_GROUPS[1:])]


def _previous_tokens(p, halo, first_of_sequence):
    rows = lax.broadcasted_iota(jnp.int32, (p.shape[0], 1), 0)
    before = jnp.where(first_of_sequence, 0.0, halo[SUBLANES - 1:SUBLANES, :])
    return jnp.where(rows == 0, before, pltpu.roll(p, 1, axis=0))


def _halo_spec(tile, order):
    per = tile // SUBLANES
    return pl.BlockSpec((SUBLANES, RW_COLS), lambda i: (jnp.maximum(order(i) * per - 1, 0), 0))


def _rw_pre(p_rw, params, S, tile=128):
    T = p_rw.shape[0]
    tile = min(tile, T)
    assert S % tile == 0
    n_p = len(params)

    def body(*refs):
        p_ref, halo_ref = refs[0], refs[1]
        par = [x[...].astype(F32) for x in refs[2:2 + n_p]]
        p = p_ref[...]
        first = lax.rem(pl.program_id(0) * tile, S) == 0
        prev = _previous_tokens(p, halo_ref[...], first)
        for o_ref, val in zip(refs[2 + n_p:], _f_rwpre(*_column_groups(p), *_column_groups(prev), *par)):
            o_ref[...] = val

    out_spec = pl.BlockSpec((tile, RW_WIDTH), lambda i: (i, 0))
    return pl.pallas_call(
        body, name="rw_pre",
        grid=(T // tile,),
        in_specs=[pl.BlockSpec((tile, RW_COLS), lambda i: (i, 0)), _halo_spec(tile, lambda i: i)]
                 + [_full_spec(q.shape) for q in params],
        out_specs=[out_spec] * 7,
        out_shape=[jax.ShapeDtypeStruct((T, RW_WIDTH), F32)] * 7,
        compiler_params=_params(("parallel",)),
    )(p_rw, p_rw, *params)


def _rw_pre_bwd(p_rw, params, cts, S, tile=128):
    T = p_rw.shape[0]
    tile = min(tile, T)
    assert S % tile == 0
    nt = T // tile
    n_p = len(params)
    ct_flat = [c for group in cts for c in group]
    ct_sizes = [len(group) for group in cts]
    n_ct = len(ct_flat)

    def body(*refs):
        p_ref, halo_ref = refs[0], refs[1]
        par = [x[...].astype(F32) for x in refs[2:2 + n_p]]
        ct_vals = [x[...] for x in refs[2 + n_p:2 + n_p + n_ct]]
        dp_ref = refs[2 + n_p + n_ct]
        dpar_refs = refs[3 + n_p + n_ct:3 + 2 * n_p + n_ct]
        carry = refs[-1]
        step = pl.program_id(0)

        @pl.when(step == 0)
        def _():
            carry[...] = jnp.zeros_like(carry)
            for ref in dpar_refs:
                ref[...] = jnp.zeros_like(ref)

        ct_in, q = [], 0
        for n in ct_sizes:
            ct_in.append(functools.reduce(lambda u, v: u + v, ct_vals[q:q + n]))
            q += n
        p = p_ref[...]
        first = lax.rem((nt - 1 - step) * tile, S) == 0
        prev = _previous_tokens(p, halo_ref[...], first)
        _, vjp = jax.vjp(_f_rwpre, *_column_groups(p), *_column_groups(prev), *par)
        grads = vjp(tuple(ct_in))
        d_here = jnp.concatenate(grads[0:4], axis=1)
        d_prev = jnp.concatenate(grads[4:8], axis=1)
        rows = lax.broadcasted_iota(jnp.int32, (tile, 1), 0)
        from_next = jnp.where(rows == tile - 1, carry[0:1, :], pltpu.roll(d_prev, tile - 1, axis=0))
        dp_ref[...] = (d_here + from_next).astype(dp_ref.dtype)
        carry[...] = jnp.broadcast_to(jnp.where(first, 0.0, d_prev[0:1, :]), carry.shape)
        for ref, g in zip(dpar_refs, grads[8:]):
            ref[...] += g

    back = lambda i: nt - 1 - i
    row = lambda w: pl.BlockSpec((tile, w), lambda i: (back(i), 0))
    res = pl.pallas_call(
        body, name="rw_pre_bwd",
        grid=(nt,),
        in_specs=[row(RW_COLS), _halo_spec(tile, back)] + [_full_spec(q.shape) for q in params]
                 + [row(RW_WIDTH)] * n_ct,
        out_specs=[row(RW_COLS)] + [_full_spec(q.shape) for q in params],
        out_shape=[jax.ShapeDtypeStruct((T, RW_COLS), BF16)] + [jax.ShapeDtypeStruct(q.shape, F32) for q in params],
        scratch_shapes=[pltpu.VMEM((SUBLANES, RW_COLS), F32)],
        compiler_params=_params(("arbitrary",)),
    )(p_rw, p_rw, *params, *ct_flat)
    return res[0], res[1:]


def _loss_head(x1, f, target, g4, tile=256):
    T, D = x1.shape
    tile = min(tile, T)

    def body(x1_ref, f_ref, t_ref, g_ref, loss_ref, dx1_ref, df_ref, dg_ref):
        (y,), vjp = jax.vjp(_f_out, x1_ref[...], f_ref[...], g_ref[...])
        err = y - t_ref[...]
        dx1, df, dg = vjp((err * (1.0 / D),))
        dx1_ref[...] = dx1
        df_ref[...] = df.astype(df_ref.dtype)

        @pl.when(pl.program_id(0) == 0)
        def _():
            loss_ref[...] = jnp.zeros_like(loss_ref)
            dg_ref[...] = jnp.zeros_like(dg_ref)

        part = jnp.sum(jnp.sum(err * err, axis=1, keepdims=True), axis=0, keepdims=True) * (0.5 / D)
        loss_ref[...] += jnp.broadcast_to(part, loss_ref.shape)
        dg_ref[...] += dg

    row = pl.BlockSpec((tile, D), lambda i: (i, 0))
    return pl.pallas_call(
        body, name="loss_head",
        grid=(T // tile,),
        in_specs=[row, row, row, _full_spec(g4.shape)],
        out_specs=[_full_spec((SUBLANES, LANES)), row, row, _full_spec(g4.shape)],
        out_shape=[jax.ShapeDtypeStruct((SUBLANES, LANES), F32), jax.ShapeDtypeStruct((T, D), F32),
                   jax.ShapeDtypeStruct((T, D), BF16), jax.ShapeDtypeStruct(g4.shape, F32)],
        compiler_params=_params(("arbitrary",)),
    )(x1, f, target, g4)


def _nn(a, b):
    return _dot(a, b, ((1,), (0,)))


def _nt(a, b):
    return _dot(a, b, ((1,), (1,)))


def _tn(a, b):
    return _dot(a, b, ((0,), (0,)))


def _split_dot(x, u2):
    hi = x.astype(BF16)
    lo = (x - hi.astype(F32)).astype(BF16)
    return _nn(jnp.concatenate([hi, lo], axis=1), u2)


def _by_head(x, masks):
    return jnp.concatenate([(x * m).astype(BF16) for m in masks], axis=0)


def _fold_heads(x2, masks):
    R = x2.shape[0] // len(masks)
    return functools.reduce(lambda u, v: u + v, [x2[h * R:(h + 1) * R] * m for h, m in enumerate(masks)])


def _head_masks():
    lane = lax.broadcasted_iota(jnp.int32, (1, LANES), 1)
    return [((lane >= h * HEAD_DIM) & (lane < (h + 1) * HEAD_DIM)).astype(F32) for h in range(LANES // HEAD_DIM)]


def _key_tri(op):
    row = lax.broadcasted_iota(jnp.int32, (ATTN_KEYS, ATTN_KEYS), 0)
    col = lax.broadcasted_iota(jnp.int32, (ATTN_KEYS, ATTN_KEYS), 1)
    u = op(row, col).astype(BF16)
    return jnp.concatenate([u, u], axis=0)


def _causal(rows):
    row = lax.broadcasted_iota(jnp.int32, (rows, ATTN_KEYS), 0)
    col = lax.broadcasted_iota(jnp.int32, (rows, ATTN_KEYS), 1)
    return col < row


def _from_row(tree, r):
    return jax.tree.map(lambda x: x[r:], tree)


def _onto_rows(old, new, r):
    return jax.tree.map(lambda o, n: jnp.concatenate([o[:r], n], axis=0) if r else n, old, new)


def _sb_weights(qb16, kbh, c_fails, u_gt, strict):
    z_all = _nt(qb16, kbh)
    zs = [z_all[:, h * ATTN_KEYS:(h + 1) * ATTN_KEYS] for h in range(len(c_fails))]
    Ls = [jnp.minimum(-z, 0.0) - jnp.log(1.0 + jnp.exp(-jnp.abs(z))) for z in zs]
    Lms = Ls if strict is None else [jnp.where(strict, L, 0.0) for L in Ls]
    cums = [_split_dot(Lm, u_gt) for Lm in Lms]
    As = [jnp.exp(z + L + c + cum) for z, L, c, cum in zip(zs, Ls, c_fails, cums)]
    if strict is not None:
        As = [jnp.where(strict, A, 0.0) for A in As]
    return zs, Ls, Lms, As


def _attn_specs(S, qb):
    nq = S // qb
    q_spec = pl.BlockSpec((qb, LANES), lambda b, p, i: (b * nq + i, p))
    k_spec = pl.BlockSpec((S, LANES), lambda b, p, i: (b, SB_WIDTH // LANES + p))
    v_spec = pl.BlockSpec((S, LANES), lambda b, p, i: (b, 2 * SB_WIDTH // LANES + p))
    seq = pl.BlockSpec((S, LANES), lambda b, p, i: (b, p))
    return q_spec, k_spec, v_spec, q_spec, seq


def _key_walk(i, qb, block, carry, fails):
    per = qb // ATTN_KEYS
    for sub in reversed(range(per)):
        carry = block(i * per + sub, carry, sub * ATTN_KEYS)
    n = i * per

    def alive(c):
        return jnp.max(functools.reduce(jnp.maximum, fails(c))) > ATTN_DEAD

    def cond(state):
        return jnp.logical_and(state[0] < n, state[1])

    def body(state):
        c = block(n - 1 - state[0], state[2], None)
        return state[0] + 1, alive(c), c

    return lax.while_loop(cond, body, (jnp.int32(0), alive(carry), carry))[2]


def _attn_fwd(proj, B, S):
    qb = min(ATTN_QUERIES, S)
    scale = HEAD_DIM ** -0.5

    def body(q_ref, k_ref, v_ref, o_ref):
        i = pl.program_id(2)
        masks = _head_masks()
        u_gt = _key_tri(lambda r, c: r > c)
        q16 = (q_ref[...] * scale).astype(BF16)

        def block(J, carry, row0):
            r0 = pl.multiple_of(J * ATTN_KEYS, ATTN_KEYS)
            kbh = _by_head(k_ref[pl.ds(r0, ATTN_KEYS), :], masks)
            vbh = _by_head(v_ref[pl.ds(r0, ATTN_KEYS), :], masks)
            lo = row0 or 0
            strict = None if row0 is None else _causal(qb - lo)
            acc, cs = _from_row(carry, lo)
            _, _, Lms, As = _sb_weights(q16[lo:], kbh, cs, u_gt, strict)
            acc = acc + _nn(jnp.concatenate([A.astype(BF16) for A in As], axis=1), vbh)
            cs = tuple(c + jnp.sum(Lm, axis=1, keepdims=True) for c, Lm in zip(cs, Lms))
            return _onto_rows(carry, (acc, cs), lo)

        zero_c = tuple(jnp.zeros((qb, 1), F32) for _ in masks)
        carry = _key_walk(i, qb, block, (jnp.zeros((qb, LANES), F32), zero_c), lambda c: c[1])
        o_ref[...] = carry[0]

    q_spec, k_spec, v_spec, blk, _ = _attn_specs(S, qb)
    return pl.pallas_call(
        body, name="sb_attn_fwd",
        grid=(B, SB_WIDTH // LANES, S // qb),
        in_specs=[q_spec, k_spec, v_spec],
        out_specs=blk,
        out_shape=jax.ShapeDtypeStruct((B * S, SB_WIDTH), F32),
        compiler_params=_params(("parallel", "parallel", "arbitrary")),
    )(proj, proj, proj)


def _attn_bwd(proj, o, do, B, S):
    qb = min(ATTN_QUERIES, S)
    nq = S // qb
    scale = HEAD_DIM ** -0.5

    def body(q_ref, k_ref, v_ref, o_ref, do_ref, dq_ref, dk_out, dv_out, dk_ref, dv_ref):
        i = pl.program_id(2)

        @pl.when(i == 0)
        def _():
            dk_ref[...] = jnp.zeros_like(dk_ref)
            dv_ref[...] = jnp.zeros_like(dv_ref)

        masks = _head_masks()
        u_gt = _key_tri(lambda r, c: r > c)
        u_ge = _key_tri(lambda r, c: r >= c)
        heads = range(len(masks))
        q16 = (q_ref[...] * scale).astype(BF16)
        do16 = do_ref[...].astype(BF16)
        od = o_ref[...] * do16.astype(F32)
        totals = tuple(jnp.sum(od * m, axis=1, keepdims=True) for m in masks)

        def block(J, carry, row0):
            r0 = pl.multiple_of(J * ATTN_KEYS, ATTN_KEYS)
            kbh = _by_head(k_ref[pl.ds(r0, ATTN_KEYS), :], masks)
            vbh = _by_head(v_ref[pl.ds(r0, ATTN_KEYS), :], masks)
            lo = row0 or 0
            strict = None if row0 is None else _causal(qb - lo)
            dq, c_fail, c_p = _from_row(carry, lo)
            tot = _from_row(totals, lo)
            zs, Ls, Lms, As = _sb_weights(q16[lo:], kbh, c_fail, u_gt, strict)
            Abs = [A.astype(BF16) for A in As]
            dA_all = _nt(do16[lo:], vbh)
            Ps = [Abs[h].astype(F32) * dA_all[:, h * ATTN_KEYS:(h + 1) * ATTN_KEYS] for h in heads]
            afters = [c_p[h] + _split_dot(Ps[h], u_ge) for h in heads]
            sigs = [jnp.exp(zs[h] + Ls[h]) for h in heads]
            dzs = [Ps[h] * (1.0 - sigs[h]) - sigs[h] * (tot[h] - afters[h]) for h in heads]
            if strict is not None:
                dzs = [jnp.where(strict, dz, 0.0) for dz in dzs]
            dz_all = jnp.concatenate([dz.astype(BF16) for dz in dzs], axis=1)
            dv_ref[pl.ds(r0, ATTN_KEYS), :] += _fold_heads(_tn(jnp.concatenate(Abs, axis=1), do16[lo:]), masks)
            dk_ref[pl.ds(r0, ATTN_KEYS), :] += _fold_heads(_tn(dz_all, q16[lo:]), masks)
            dq = dq + _nn(dz_all, kbh)
            c_fail = tuple(c_fail[h] + jnp.sum(Lms[h], axis=1, keepdims=True) for h in heads)
            c_p = tuple(c_p[h] + jnp.sum(Ps[h], axis=1, keepdims=True) for h in heads)
            return _onto_rows(carry, (dq, c_fail, c_p), lo)

        zc = tuple(jnp.zeros((qb, 1), F32) for _ in masks)
        carry = _key_walk(i, qb, block, (jnp.zeros((qb, LANES), F32), zc, zc), lambda c: c[1])
        dq_ref[...] = (carry[0] * scale).astype(dq_ref.dtype)

        @pl.when(i == nq - 1)
        def _():
            dk_out[...] = dk_ref[...].astype(dk_out.dtype)
            dv_out[...] = dv_ref[...].astype(dv_out.dtype)

    q_spec, k_spec, v_spec, blk, seq = _attn_specs(S, qb)
    return pl.pallas_call(
        body, name="sb_attn_bwd",
        grid=(B, SB_WIDTH // LANES, nq),
        in_specs=[q_spec, k_spec, v_spec, blk, blk],
        out_specs=[blk, seq, seq],
        out_shape=[jax.ShapeDtypeStruct((B * S, SB_WIDTH), BF16)] * 3,
        scratch_shapes=[pltpu.VMEM((S, LANES), F32), pltpu.VMEM((S, LANES), F32)],
        compiler_params=_params(("parallel", "parallel", "arbitrary")),
    )(proj, proj, proj, o, do)


_BATCHED = {"nn": "gmk,gkn->gmn", "nt": "gmk,gnk->gmn", "tn": "gkm,gkn->gmn"}


def _bdot_raw(a, b, kind, passes):
    e = functools.partial(jnp.einsum, _BATCHED[kind], preferred_element_type=F32)
    ah, bh = a.astype(BF16), b.astype(BF16)
    if passes == 1:
        return e(ah, bh)
    al, bl = (a - ah.astype(F32)).astype(BF16), (b - bh.astype(F32)).astype(BF16)
    return e(ah, bh) + e(ah, bl) + e(al, bh)


@functools.partial(jax.custom_vjp, nondiff_argnums=(2, 3))
def _bdot(a, b, kind, passes):
    return _bdot_raw(a, b, kind, passes)


def _bdot_fwd(a, b, kind, passes):
    return _bdot_raw(a, b, kind, passes), (a, b)


def _bdot_bwd(kind, passes, res, g):
    a, b = res
    if kind == "nn":
        return _bdot_raw(g, b, "nt", passes), _bdot_raw(a, g, "tn", passes)
    if kind == "nt":
        return _bdot_raw(g, b, "nn", passes), _bdot_raw(g, a, "tn", passes)
    return _bdot_raw(b, g, "nt", passes), _bdot_raw(a, g, "nn", passes)


_bdot.defvjp(_bdot_fwd, _bdot_bwd)


def _solve_powers(m):
    powers = [m]
    for _ in range(max(1, (m.shape[1] - 1).bit_length()) - 1):
        powers.append(_bdot_raw(powers[-1], powers[-1], "nn", 1))
    return powers


def _solve_fwd(m, rhs):
    powers = _solve_powers(m)
    x = rhs
    for p in powers:
        x = x + _bdot_raw(p, x, "nn", 1)
    return x, (powers, x)


def _solve_bwd(res, g):
    powers, x = res
    for p in powers:
        g = g + _bdot_raw(p, g, "tn", 1)
    return _bdot_raw(g, x, "nt", 1), g


@jax.custom_vjp
def _unit_lower_solve(m, rhs):
    return _solve_fwd(m, rhs)[0]


_unit_lower_solve.defvjp(_solve_fwd, _solve_bwd)


def _wkv_chunk(S0, r, lw, k, v, kap, a):
    G, C, N = r.shape
    row = lax.broadcasted_iota(jnp.int32, (C, C), 0)
    col = lax.broadcasted_iota(jnp.int32, (C, C), 1)
    incl = (col <= row).astype(F32)
    strict = (col < row).astype(F32)
    cum = _bdot(jnp.broadcast_to(incl, (G, C, C)), lw, "nn", 3)
    e_pos = jnp.exp(cum)
    e_neg = jnp.exp(-cum)
    al = -kap * jnp.exp(cum - lw)
    be = kap * a * e_neg
    kt = k * e_neg
    rt = r * e_pos
    bk = jnp.concatenate([be, kt], axis=1)
    mask = jnp.concatenate([jnp.concatenate([strict, strict], axis=1), jnp.concatenate([incl, incl], axis=1)], axis=0)
    m_all = _bdot(jnp.concatenate([al, rt], axis=1), bk, "nt", 3) * mask
    m_ab, m_ak = m_all[:, :C, :C], m_all[:, :C, C:]
    m_rb, m_rk = m_all[:, C:, :C], m_all[:, C:, C:]
    S0t = jnp.swapaxes(S0, 1, 2)
    sa = _unit_lower_solve(m_ab, _bdot(jnp.concatenate([al, m_ak], axis=2), jnp.concatenate([S0t, v], axis=1), "nn", 3))
    y =_bdot(jnp.concatenate([rt, m_rb, m_rk], axis=2), jnp.concatenate([S0t, sa, v], axis=1), "nn", 3)
    S1 = (S0 + _bdot(jnp.concatenate([sa, v], axis=1), bk, "tn", 1)) * e_pos[:, C - 1:C, :]
    return y, S1


def _split_heads(x):
    return jnp.stack([x[:, h * HEAD_DIM:(h + 1) * HEAD_DIM] for h in range(x.shape[1] // HEAD_DIM)], axis=0)


def _merge_heads(x):
    return jnp.concatenate([x[h] for h in range(x.shape[0])], axis=1)


def _seq_heads(ref):
    return jnp.concatenate([_split_heads(ref[s]) for s in range(ref.shape[0])], axis=0)


def _store_seq_heads(ref, x):
    heads = x.shape[0] // ref.shape[0]
    for s in range(ref.shape[0]):
        ref[s] = _merge_heads(x[s * heads:(s + 1) * heads])


def _wkv_fwd(r, lw, k, v, kap, a, B, S):
    C, H, N = WKV_CHUNK, RW_WIDTH // HEAD_DIM, HEAD_DIM
    nc = S // C
    Q = min(WKV_SEQS, B)

    def body(r_ref, lw_ref, k_ref, v_ref, kap_ref, a_ref, y_ref, st_ref, s_scr):
        @pl.when(pl.program_id(1) == 0)
        def _():
            s_scr[...] = jnp.zeros_like(s_scr)

        S0 = s_scr[...]
        for s in range(Q):
            st_ref[s, 0] = S0[s * H:(s + 1) * H]
        args = [_seq_heads(ref) for ref in (r_ref, lw_ref, k_ref, v_ref, kap_ref, a_ref)]
        y, S1 = _wkv_chunk(S0, *args)
        s_scr[...] = S1
        _store_seq_heads(y_ref, y)

    row_spec = pl.BlockSpec((Q, C, RW_WIDTH), lambda b, c: (b, c, 0))
    seqs = lambda t: t.reshape(B, S, RW_WIDTH)
    y, states = pl.pallas_call(
        body, name="wkv_fwd",
        grid=(B // Q, nc),
        in_specs=[row_spec] * 6,
        out_specs=[row_spec, pl.BlockSpec((Q, 1, H, N, N), lambda b, c: (b, c, 0, 0, 0))],
        out_shape=[jax.ShapeDtypeStruct((B, S, RW_WIDTH), F32), jax.ShapeDtypeStruct((B, nc, H, N, N), F32)],
        scratch_shapes=[pltpu.VMEM((Q * H, N, N), F32)],
        compiler_params=_params(("arbitrary", "arbitrary")),
    )(*map(seqs, (r, lw, k, v, kap, a)))
    return y.reshape(B * S, RW_WIDTH), states


def _wkv_bwd(r, lw, k, v, kap, a, states, dy, B, S):
    C, H, N = WKV_CHUNK, RW_WIDTH // HEAD_DIM, HEAD_DIM
    nc = S // C
    Q = min(WKV_SEQS, B)

    def body(r_ref, lw_ref, k_ref, v_ref, kap_ref, a_ref, st_ref, dy_ref,
             dr_ref, dlw_ref, dk_ref, dv_ref, dkap_ref, da_ref, ds_scr):
        @pl.when(pl.program_id(1) == 0)
        def _():
            ds_scr[...] = jnp.zeros_like(ds_scr)

        args = [_seq_heads(ref) for ref in (r_ref, lw_ref, k_ref, v_ref, kap_ref, a_ref)]
        S0 = jnp.concatenate([st_ref[s, 0] for s in range(Q)], axis=0)
        _, vjp = jax.vjp(_wkv_chunk, S0, *args)
        g = vjp((_seq_heads(dy_ref), ds_scr[...]))
        ds_scr[...] = g[0]
        for ref, gv in zip((dr_ref, dlw_ref, dk_ref, dv_ref, dkap_ref, da_ref), g[1:]):
            _store_seq_heads(ref, gv)

    row_spec = pl.BlockSpec((Q, C, RW_WIDTH), lambda b, c: (b, nc - 1 - c, 0))
    st_spec = pl.BlockSpec((Q, 1, H, N, N), lambda b, c: (b, nc - 1 - c, 0, 0, 0))
    seqs = lambda t: t.reshape(B, S, RW_WIDTH)
    res = pl.pallas_call(
        body, name="wkv_bwd",
        grid=(B // Q, nc),
        in_specs=[row_spec] * 6 + [st_spec, row_spec],
        out_specs=[row_spec] * 6,
        out_shape=[jax.ShapeDtypeStruct((B, S, RW_WIDTH), F32)] * 6,
        scratch_shapes=[pltpu.VMEM((Q * H, N, N), F32)],
        compiler_params=_params(("arbitrary", "arbitrary"), vmem=WKV_BWD_VMEM),
    )(*map(seqs, (r, lw, k, v, kap, a)), states, seqs(dy))
    return [t.reshape(B * S, RW_WIDTH) for t in res]


HBM = pl.BlockSpec(memory_space=pl.ANY)


def _place():
    return lax.axis_index("x"), lax.axis_index("y"), lax.axis_index("c")


def _other_chips(x, y):
    return [(1 - x, y), (x, 1 - y), (1 - x, 1 - y)]


def _all_gather_chips(shards):
    n = len(shards)

    def body(*refs):
        ins, outs = refs[:n], refs[n:2 * n]
        ici_send, ici_recv, d2d_send, d2d_recv, local = refs[2 * n:]
        x, y, c = _place()
        me = 2 * x + y
        sib = (x, y, 1 - c)
        chips = _other_chips(x, y)
        started, copies = [], []
        for w in range(n):
            cp = pltpu.make_async_copy(ins[w].at[c], outs[w].at[me, c], local.at[w])
            cp.start()
            copies.append(cp)
            for j, (px, py) in enumerate(chips):
                rd = pltpu.make_async_remote_copy(
                    src_ref=ins[w].at[c], dst_ref=outs[w].at[me, c], send_sem=ici_send.at[3 * w + j],
                    recv_sem=ici_recv.at[3 * w + j], device_id=(px, py, c), device_id_type=MESH)
                rd.start()
                started.append(rd)
            rd = pltpu.make_async_remote_copy(
                src_ref=ins[w].at[c], dst_ref=outs[w].at[me, c], send_sem=d2d_send.at[4 * w + 3],
                recv_sem=d2d_recv.at[4 * w + 3], device_id=sib, device_id_type=MESH)
            rd.start()
            started.append(rd)
        for w in range(n):
            for j, (px, py) in enumerate(chips):
                src = 2 * px + py
                pltpu.make_async_remote_copy(
                    src_ref=ins[w].at[c], dst_ref=outs[w].at[src, c], send_sem=ici_send.at[3 * w + j],
                    recv_sem=ici_recv.at[3 * w + j], device_id=(px, py, c), device_id_type=MESH).wait_recv()
                rd = pltpu.make_async_remote_copy(
                    src_ref=outs[w].at[src, c], dst_ref=outs[w].at[src, c], send_sem=d2d_send.at[4 * w + j],
                    recv_sem=d2d_recv.at[4 * w + j], device_id=sib, device_id_type=MESH)
                rd.start()
                started.append(rd)
        for w in range(n):
            for j, (px, py) in enumerate(chips):
                pltpu.make_async_remote_copy(
                    src_ref=ins[w].at[c], dst_ref=outs[w].at[2 * px + py, 1 - c], send_sem=d2d_send.at[4 * w + j],
                    recv_sem=d2d_recv.at[4 * w + j], device_id=sib, device_id_type=MESH).wait_recv()
            pltpu.make_async_remote_copy(
                src_ref=ins[w].at[c], dst_ref=outs[w].at[me, 1 - c], send_sem=d2d_send.at[4 * w + 3],
                recv_sem=d2d_recv.at[4 * w + 3], device_id=sib, device_id_type=MESH).wait_recv()
        for rd in started:
            rd.wait_send()
        for cp in copies:
            cp.wait()

    return pl.pallas_call(
        body, name="gather_weights",
        in_specs=[HBM] * n, out_specs=[HBM] * n,
        out_shape=[jax.ShapeDtypeStruct((N_CHIPS,) + s.shape, s.dtype) for s in shards],
        scratch_shapes=[pltpu.SemaphoreType.DMA((3 * n,)), pltpu.SemaphoreType.DMA((3 * n,)),
                        pltpu.SemaphoreType.DMA((4 * n,)), pltpu.SemaphoreType.DMA((4 * n,)),
                        pltpu.SemaphoreType.DMA((n,))],
        compiler_params=pltpu.CompilerParams(has_side_effects=True),
    )(*shards)


def _pair_split(grads):
    n = len(grads)

    def body(*refs):
        ins, theirs = refs[:n], refs[n:2 * n]
        send, recv = refs[2 * n:]
        x, y, c = _place()
        sib = (x, y, 1 - c)
        rds = []
        for w in range(n):
            rd = pltpu.make_async_remote_copy(
                src_ref=ins[w].at[:, 1 - c], dst_ref=theirs[w], send_sem=send.at[w], recv_sem=recv.at[w],
                device_id=sib, device_id_type=MESH)
            rd.start()
            rds.append(rd)
        for rd in rds:
            rd.wait_recv()
        for rd in rds:
            rd.wait_send()

    return pl.pallas_call(
        body, name="grad_pair_split",
        in_specs=[HBM] * n, out_specs=[HBM] * n,
        out_shape=[jax.ShapeDtypeStruct((g.shape[0],) + g.shape[2:], g.dtype) for g in grads],
        scratch_shapes=[pltpu.SemaphoreType.DMA((n,)), pltpu.SemaphoreType.DMA((n,))],
        compiler_params=pltpu.CompilerParams(has_side_effects=True),
    )(*grads)


def _chip_scatter(parts):
    n = len(parts)

    def body(*refs):
        ins, outs = refs[:n], refs[n:2 * n]
        send, recv = refs[2 * n:]
        x, y, c = _place()
        me = 2 * x + y
        rds = []
        for w in range(n):
            for j, (px, py) in enumerate(_other_chips(x, y)):
                s = 3 * w + j
                rd = pltpu.make_async_remote_copy(
                    src_ref=ins[w].at[2 * px + py], dst_ref=outs[w].at[j], send_sem=send.at[s], recv_sem=recv.at[s],
                    device_id=(px, py, c), device_id_type=MESH)
                rd.start()
                rds.append(rd)
        for w in range(n):
            for j, (px, py) in enumerate(_other_chips(x, y)):
                s = 3 * w + j
                pltpu.make_async_remote_copy(
                    src_ref=ins[w].at[me], dst_ref=outs[w].at[j], send_sem=send.at[s], recv_sem=recv.at[s],
                    device_id=(px, py, c), device_id_type=MESH).wait_recv()
        for rd in rds:
            rd.wait_send()

    return pl.pallas_call(
        body, name="grad_chip_scatter",
        in_specs=[HBM] * n, out_specs=[HBM] * n,
        out_shape=[jax.ShapeDtypeStruct((N_CHIPS - 1,) + p.shape[1:], p.dtype) for p in parts],
        scratch_shapes=[pltpu.SemaphoreType.DMA((3 * n,)), pltpu.SemaphoreType.DMA((3 * n,))],
        compiler_params=pltpu.CompilerParams(has_side_effects=True),
    )(*parts)


def _pair_join(bufs):
    n = len(bufs)

    def body(*refs):
        ins, outs = refs[:n], refs[n:2 * n]
        send, recv = refs[2 * n:]
        x, y, c = _place()
        sib = (x, y, 1 - c)
        rds = []
        for w in range(n):
            rd = pltpu.make_async_remote_copy(
                src_ref=ins[w].at[c], dst_ref=outs[w].at[c], send_sem=send.at[w], recv_sem=recv.at[w],
                device_id=sib, device_id_type=MESH)
            rd.start()
            rds.append(rd)
        for w in range(n):
            pltpu.make_async_remote_copy(
                src_ref=ins[w].at[c], dst_ref=outs[w].at[1 - c], send_sem=send.at[w], recv_sem=recv.at[w],
                device_id=sib, device_id_type=MESH).wait_recv()
        for rd in rds:
            rd.wait_send()

    return pl.pallas_call(
        body, name="grad_pair_join",
        in_specs=[HBM] * n, out_specs=[HBM] * n,
        out_shape=[jax.ShapeDtypeStruct(b.shape, b.dtype) for b in bufs],
        input_output_aliases={w: w for w in range(n)},
        scratch_shapes=[pltpu.SemaphoreType.DMA((n,)), pltpu.SemaphoreType.DMA((n,))],
        compiler_params=pltpu.CompilerParams(has_side_effects=True),
    )(*bufs)


def _all_reduce_small(packed):
    R = packed.shape[0]

    def body(x_ref, o_ref, buf, send, recv):
        x, y, c = _place()
        me = 4 * x + 2 * y + c
        buf[me] = x_ref[...]
        rds = []
        for rel in range(1, N_DEV):
            fx, fy, fc = (rel >> 2) & 1, (rel >> 1) & 1, rel & 1
            peer = (1 - x if fx else x, 1 - y if fy else y, 1 - c if fc else c)
            rd = pltpu.make_async_remote_copy(
                src_ref=x_ref, dst_ref=buf.at[me], send_sem=send.at[rel - 1], recv_sem=recv.at[rel - 1],
                device_id=peer, device_id_type=MESH)
            rd.start()
            rds.append((rd, peer))
        for rel in range(1, N_DEV):
            rd, (px, py, pc) = rds[rel - 1]
            pltpu.make_async_remote_copy(
                src_ref=x_ref, dst_ref=buf.at[4 * px + 2 * py + pc], send_sem=send.at[rel - 1], recv_sem=recv.at[rel - 1],
                device_id=(px, py, pc), device_id_type=MESH).wait_recv()
        for rd, _ in rds:
            rd.wait_send()
        total = buf[0]
        for d in range(1, N_DEV):
            total = total + buf[d]
        o_ref[...] = total

    return pl.pallas_call(
        body, name="all_reduce_small",
        in_specs=[pl.BlockSpec(memory_space=pltpu.VMEM)],
        out_specs=pl.BlockSpec(memory_space=pltpu.VMEM),
        out_shape=jax.ShapeDtypeStruct(packed.shape, F32),
        scratch_shapes=[pltpu.VMEM((N_DEV, R, LANES), F32), pltpu.SemaphoreType.DMA((N_DEV - 1,)),
                        pltpu.SemaphoreType.DMA((N_DEV - 1,))],
        compiler_params=pltpu.CompilerParams(has_side_effects=True),
    )(packed)


def _pair_sum(name, split, theirs, core):
    n_chip, _, Rh, C = split.shape
    tile = _div_tile(Rh, 256, 2 * SUBLANES)
    nt = Rh // tile

    def body(core_ref, a_ref, b_ref, o_ref):
        o_ref[...] = (a_ref[...] + b_ref[...]).astype(o_ref.dtype)

    return pl.pallas_call(
        body, name=name,
        grid_spec=pltpu.PrefetchScalarGridSpec(
            num_scalar_prefetch=1,
            grid=(n_chip, nt),
            in_specs=[pl.BlockSpec((None, None, tile, C), lambda j, i, core_ref: (j, core_ref[0], i, 0)),
                      pl.BlockSpec((None, tile, C), lambda j, i, core_ref: (j, i, 0))],
            out_specs=pl.BlockSpec((None, tile, C), lambda j, i, core_ref: (j, i, 0)),
        ),
        out_shape=jax.ShapeDtypeStruct((n_chip, Rh, C), BF16),
        compiler_params=_params(("parallel", "parallel")),
    )(core, split, theirs)


def _chip_sum(name, own, landed, core):
    n_in, Rh, C = landed.shape
    tile = _div_tile(Rh, 256, 2 * SUBLANES)

    def body(core_ref, *refs):
        total = refs[0][...].astype(F32)
        for ref in refs[1:n_in + 1]:
            total = total + ref[...].astype(F32)
        refs[n_in + 1][...] = total

    slot = lambda j: pl.BlockSpec((None, tile, C), lambda i, core_ref: (j, i, 0))
    return pl.pallas_call(
        body, name=name,
        grid_spec=pltpu.PrefetchScalarGridSpec(
            num_scalar_prefetch=1,
            grid=(Rh // tile,),
            in_specs=[pl.BlockSpec((None, tile, C), lambda i, core_ref: (core_ref[1], i, 0))]
                     + [slot(j) for j in range(n_in)],
            out_specs=pl.BlockSpec((None, tile, C), lambda i, core_ref: (core_ref[0], i, 0)),
        ),
        out_shape=jax.ShapeDtypeStruct((2, Rh, C), F32),
        compiler_params=_params(("parallel",)),
    )(core, own, *([landed] * n_in))


def _adamw(name, w, g, m, v):
    R, C = w.shape
    tile = _div_tile(R, 256, SUBLANES)
    c1 = 1.0 / (1.0 - ADAM_B1 ** ADAM_STEP)
    c2 = 1.0 / (1.0 - ADAM_B2 ** ADAM_STEP)

    def body(w_ref, g_ref, m_ref, v_ref, d_ref, nm_ref, nv_ref):
        g_ = g_ref[...]
        nm = ADAM_B1 * m_ref[...] + (1.0 - ADAM_B1) * g_
        nv = ADAM_B2 * v_ref[...] + (1.0 - ADAM_B2) * (g_ * g_)
        d_ref[...] = -ADAM_LR * ((nm * c1) / (jnp.sqrt(nv * c2) + ADAM_EPS) + ADAM_WD * w_ref[...])
        nm_ref[...] = nm
        nv_ref[...] = nv

    spec = pl.BlockSpec((tile, C), lambda i: (i, 0))
    return pl.pallas_call(
        body, name=name,
        grid=(R // tile,),
        in_specs=[spec] * 4, out_specs=[spec] * 3,
        out_shape=[jax.ShapeDtypeStruct((R, C), F32)] * 3,
        compiler_params=_params(("parallel",)),
    )(w, g, m, v)


SMALL =["norm_mix_pre", "b_gate", "mu_rw", "w0", "a0", "k_k", "k_a", "r_k", "lnx_w", "lnx_b",
         "norm_mix_post", "norm_ffn_pre", "norm_ffn_post"]
BIG = ["w_in", "w_up", "a_up", "g_up", "w_sb_out", "w_rw_out", "w_o", "w_ffn_gate", "w_ffn_up", "w_ffn_down"]
ROW_SHARDED = ("w_o", "w_ffn_down")
ORDER = ["norm_mix_pre", "w_in", "b_gate", "mu_rw", "w0", "w_up", "a0", "a_up", "g_up", "k_k", "k_a", "r_k",
         "lnx_w", "lnx_b", "w_sb_out", "w_rw_out", "w_o", "norm_mix_post", "norm_ffn_pre", "w_ffn_gate",
         "w_ffn_up", "w_ffn_down", "norm_ffn_post"]


def _pack_small(vals, extra_rows=0):
    rows = jnp.concatenate([vals[n].reshape(-1, LANES) for n in SMALL], axis=0)
    pad = (-(rows.shape[0] + extra_rows)) % SUBLANES + extra_rows
    return jnp.pad(rows, ((0, pad), (0, 0)))


def _unpack_small(packed, shapes):
    out, r = {}, 0
    for n in SMALL:
        size = 1
        for s in shapes[n]:
            size *= s
        out[n] = packed[r:r + size // LANES].reshape(shapes[n])
        r += size // LANES
    return out


def kernel(x, norm_mix_pre, w_in, b_gate, mu_rw, w0, w_up, a0, a_up, g_up, k_k, k_a, r_k, lnx_w, lnx_b, w_sb_out, w_rw_out, w_o, norm_mix_post, norm_ffn_pre, w_ffn_gate, w_ffn_up, w_ffn_down, norm_ffn_post, loss_target, m_norm_mix_pre, m_w_in, m_b_gate, m_mu_rw, m_w0, m_w_up, m_a0, m_a_up, m_g_up, m_k_k, m_k_a, m_r_k, m_lnx_w, m_lnx_b, m_w_sb_out, m_w_rw_out, m_w_o, m_norm_mix_post, m_norm_ffn_pre, m_w_ffn_gate, m_w_ffn_up, m_w_ffn_down, m_norm_ffn_post, v_norm_mix_pre, v_w_in, v_b_gate, v_mu_rw, v_w0, v_w_up, v_a0, v_a_up, v_g_up, v_k_k, v_k_a, v_r_k, v_lnx_w, v_lnx_b, v_w_sb_out, v_w_rw_out, v_w_o, v_norm_mix_post, v_norm_ffn_pre, v_w_ffn_gate, v_w_ffn_up, v_w_ffn_down, v_norm_ffn_post):
    W = dict(norm_mix_pre=norm_mix_pre, w_in=w_in, b_gate=b_gate, mu_rw=mu_rw, w0=w0, w_up=w_up, a0=a0, a_up=a_up,
             g_up=g_up, k_k=k_k, k_a=k_a, r_k=r_k, lnx_w=lnx_w, lnx_b=lnx_b, w_sb_out=w_sb_out, w_rw_out=w_rw_out,
             w_o=w_o, norm_mix_post=norm_mix_post, norm_ffn_pre=norm_ffn_pre, w_ffn_gate=w_ffn_gate,
             w_ffn_up=w_ffn_up, w_ffn_down=w_ffn_down, norm_ffn_post=norm_ffn_post)
    Mo = dict(norm_mix_pre=m_norm_mix_pre, w_in=m_w_in, b_gate=m_b_gate, mu_rw=m_mu_rw, w0=m_w0, w_up=m_w_up, a0=m_a0,
              a_up=m_a_up, g_up=m_g_up, k_k=m_k_k, k_a=m_k_a, r_k=m_r_k, lnx_w=m_lnx_w, lnx_b=m_lnx_b,
              w_sb_out=m_w_sb_out, w_rw_out=m_w_rw_out, w_o=m_w_o, norm_mix_post=m_norm_mix_post,
              norm_ffn_pre=m_norm_ffn_pre, w_ffn_gate=m_w_ffn_gate, w_ffn_up=m_w_ffn_up, w_ffn_down=m_w_ffn_down,
              norm_ffn_post=m_norm_ffn_post)
    Vo = dict(norm_mix_pre=v_norm_mix_pre, w_in=v_w_in, b_gate=v_b_gate, mu_rw=v_mu_rw, w0=v_w0, w_up=v_w_up, a0=v_a0,
              a_up=v_a_up, g_up=v_g_up, k_k=v_k_k, k_a=v_k_a, r_k=v_r_k, lnx_w=v_lnx_w, lnx_b=v_lnx_b,
              w_sb_out=v_w_sb_out, w_rw_out=v_w_rw_out, w_o=v_w_o, norm_mix_post=v_norm_mix_post,
              norm_ffn_pre=v_norm_ffn_pre, w_ffn_gate=v_w_ffn_gate, w_ffn_up=v_w_ffn_up, w_ffn_down=v_w_ffn_down,
              norm_ffn_post=v_norm_ffn_post)
    shapes = {n: W[n].shape for n in ORDER}
    B, S, D = x.shape
    T = B * S
    x2 = x.reshape(T, D)
    tgt = loss_target.reshape(T, D)
    vec = {n: W[n].reshape(1, -1) for n in SMALL}

    work = lambda t, n: t[0] if n in ROW_SHARDED else jnp.swapaxes(t[0], 0, 1)
    halved = [work(W[n], n).astype(BF16) for n in BIG]
    halved = [h.reshape(2, h.shape[0] // 2, h.shape[1]) for h in halved]
    full = {n: gth.reshape(-1, gth.shape[3]) for n, gth in zip(BIG, _all_gather_chips(halved))}
    w_in_t = full["w_in"]
    w_sb_t, w_rw_t, w_gt_t = w_in_t[:SB_COLS], w_in_t[SB_COLS:SB_COLS + RW_COLS], w_in_t[SB_COLS + RW_COLS:]
    lora_rows = {"w_up": 0, "a_up": 64, "g_up": 128}
    lora = {n: jnp.pad(full[n].T, ((r0, LORA_COLS - r0 - full[n].shape[1]), (0, 0))) for n, r0 in lora_rows.items()}
    mu = vec["mu_rw"]
    mu_parts = [mu[:, :512], mu[:, 512:1024], mu[:, 1024:1536], mu[:, 1536:]]
    b1, b2 = vec["b_gate"][:, :D], vec["b_gate"][:, D:]

    (h1,) = _rowwise("norm_mix_pre", _f_norm, [(x2, D, 0)], [vec["norm_mix_pre"]], [(D, BF16)], tile=512)
    p_sb = _mm("proj_sb", h1, w_sb_t, tb=True, out_dtype=BF16)
    p_rw = _mm("proj_rw", h1, w_rw_t, tb=True)
    p_gt = _mm("proj_gate", h1, w_gt_t, tb=True, out_dtype=BF16)
    o_sb = _attn_fwd(p_sb, B, S)
    pre_params = mu_parts + [vec["w0"], lora["w_up"], vec["a0"], lora["a_up"], lora["g_up"], vec["k_k"], vec["k_a"]]
    r_, lw_, k2_, v_, kap_, a_, g_ = _rw_pre(p_rw, pre_params, S)
    y_wkv, states = _wkv_fwd(r_, lw_, k2_, v_, kap_, a_, B, S)
    post_rows = [(y_wkv, 512, 0), (r_, 512, 0), (k2_, 512, 0), (v_, 512, 0), (g_, 512, 0)]
    post_params = [vec["lnx_w"], vec["lnx_b"], vec["r_k"]]
    (o_rw,) = _rowwise("rw_post", _f_rwpost, post_rows, post_params, [(512, BF16)])
    m1 = _mm("mix_sb_out", o_sb, full["w_sb_out"], tb=True, out_dtype=BF16)
    m2 = _mm("mix_rw_out", o_rw, full["w_rw_out"], tb=True, out_dtype=BF16)
    merge_rows = [(p_gt, D, 0), (p_gt, D, 1), (m1, D, 0), (m2, D, 0)]
    (merged,) = _rowwise("merge", _f_merge, merge_rows, [b1, b2], [(D, BF16)])
    u = _mm("mix_out", merged, full["w_o"])
    post1_params = [vec["norm_mix_post"], vec["norm_ffn_pre"]]
    x1, h2 = _rowwise("post_mix", _f_post1, [(x2, D, 0), (u, D, 0)], post1_params, [(D, F32), (D, BF16)], tile=512)
    ag, au, sw = _mm_fused("ffn_in", [h2, h2], [full["w_ffn_gate"], full["w_ffn_up"]], [BF16] * 3, tb=True,
                           epilogue=lambda gu, _: (gu[0], gu[1], _f_swiglu(*gu)[0]))
    f = _mm("ffn_down", sw, full["w_ffn_down"])
    loss_part, dx1, df, dg4 = _loss_head(x1, f, tgt, vec["norm_ffn_post"])

    gbig, gsmall = {}, {"norm_ffn_post": dg4}
    gbig["w_ffn_down"] = _mm("g_ffn_down", sw, df, ta=True)

    def swiglu_back(dsw, gu):
        return jax.vjp(_f_swiglu, *gu)[1]((dsw[0],))

    dag, dau = _mm_fused("ffn_back", [df], [full["w_ffn_down"]], [BF16] * 2, tb=True, extras=[ag, au],
                         epilogue=swiglu_back)
    (dh2,) = _mm_fused("d_h2", [dag, dau], [full["w_ffn_gate"], full["w_ffn_up"]], [F32], add=True)
    gbig["w_ffn_gate"] = _mm("g_ffn_gate", dag, h2, ta=True)
    gbig["w_ffn_up"] = _mm("g_ffn_up", dau, h2, ta=True)
    (dx_res, du), (dg2, dg3) = _rowwise_vjp("post_mix_bwd", _f_post1, [(x2, D, 0), (u, D, 0)], post1_params,
                                            [[dx1], [dh2]], [True, True], [True, True], bf16_rows=(1,))
    gsmall["norm_mix_post"], gsmall["norm_ffn_pre"] = dg2, dg3
    dmerged = _mm("d_merged", du, full["w_o"], tb=True, out_dtype=BF16)
    gbig["w_o"] = _mm("g_w_o", merged, du, ta=True)
    (dpg1, dpg2, dm1, dm2), (db1, db2) = _rowwise_vjp("merge_bwd", _f_merge, merge_rows, [b1, b2], [[dmerged]],
                                                      [True] * 4, [True, True], bf16_rows=(0, 1, 2, 3))
    gsmall["b_gate"] = jnp.concatenate([db1, db2], axis=1)
    do_sb = _mm("d_o_sb", dm1, full["w_sb_out"])
    do_rw = _mm("d_o_rw", dm2, full["w_rw_out"])
    gbig["w_sb_out"] = _mm("g_sb_out", dm1, o_sb, ta=True)
    gbig["w_rw_out"] = _mm("g_rw_out", dm2, o_rw, ta=True)
    (dy_wkv, dr_a, dk2_a, dv_a, dg_), (dlnx_w, dlnx_b, dr_k) = _rowwise_vjp(
        "rw_post_bwd", _f_rwpost, post_rows, post_params, [[do_rw]], [True] * 5, [True] * 3)
    gsmall["lnx_w"], gsmall["lnx_b"], gsmall["r_k"] = dlnx_w, dlnx_b, dr_k
    dr_b, dlw, dk2_b, dv_b, dkap, da = _wkv_bwd(r_, lw_, k2_, v_, kap_, a_, states, dy_wkv, B, S)
    pre_cts = [[dr_a, dr_b], [dlw], [dk2_a, dk2_b], [dv_a, dv_b], [dkap], [da], [dg_]]
    dp_rw, dpre_params = _rw_pre_bwd(p_rw, pre_params, pre_cts, S)
    gsmall["mu_rw"] = jnp.concatenate(dpre_params[:4], axis=1)
    gsmall["w0"], gsmall["a0"], gsmall["k_k"], gsmall["k_a"] = dpre_params[4], dpre_params[6], dpre_params[9], dpre_params[10]
    glora = {"w_up": dpre_params[5][0:64].T, "a_up": dpre_params[7][64:128].T, "g_up": dpre_params[8][128:256].T}
    dq, dk, dv = _attn_bwd(p_sb, o_sb, do_sb, B, S)
    (dh1,) = _mm_fused("d_h1_sb", [dq, dk, dv], [w_sb_t[:512], w_sb_t[512:1024], w_sb_t[1024:]], [F32], add=True)
    dh1 = _mm("d_h1_rw", dp_rw, w_rw_t, acc=dh1)
    (dh1,) = _mm_fused("d_h1_gate", [dpg1, dpg2], [w_gt_t[:D], w_gt_t[D:]], [F32], add=True,
                       extras=[dh1], epilogue=lambda p, e: (p[0] + e[0],))
    gbig["w_in"] = jnp.concatenate(
        [_mm("g_in_" + tag, d, h1, ta=True)
         for tag, d in (("q", dq), ("k", dk), ("v", dv), ("rw", dp_rw), ("g1", dpg1), ("g2", dpg2))], axis=0)
    (grad_x2,), (dg1,) = _rowwise_vjp("norm_mix_pre_bwd", _f_norm, [(x2, D, 0)], [vec["norm_mix_pre"]], [[dh1]],
                                      [True], [True], add_to={0: dx_res})
    gsmall["norm_mix_pre"] = dg1
    gbig.update(glora)

    split = [gbig[n].reshape(N_CHIPS, 2, gbig[n].shape[0] // (2 * N_CHIPS), gbig[n].shape[1]) for n in BIG]
    core = jnp.stack([lax.axis_index("c"), 2 * lax.axis_index("x") + lax.axis_index("y")]).astype(jnp.int32)
    theirs = _pair_split(split)
    chip_sums = [_pair_sum("pair_sum_" + n, a, b, core) for n, a, b in zip(BIG, split, theirs)]
    landed = _chip_scatter(chip_sums)
    joined = _pair_join([_chip_sum("chip_sum_" + n, own, got, core) for n, own, got in zip(BIG, chip_sums, landed)])
    grads = {n: j.reshape(-1, j.shape[2]) for n, j in zip(BIG, joined)}

    small_local = _pack_small({n: gsmall[n] for n in SMALL}, extra_rows=1)
    loss_row = small_local.shape[0] - 1
    small_local = small_local.at[loss_row].set(loss_part[0])
    small_sum = _all_reduce_small(small_local)
    loss = small_sum[loss_row, 0]

    delta, new_m, new_v = {}, {}, {}
    unwork = lambda t, n: (t if n in ROW_SHARDED else jnp.swapaxes(t, 0, 1))[None]
    for n in BIG:
        d_, m_, v2_ = _adamw("adamw_" + n, work(W[n], n), grads[n], work(Mo[n], n), work(Vo[n], n))
        delta[n], new_m[n], new_v[n], grads[n] = (unwork(t, n) for t in (d_, m_, v2_, grads[n]))
    pk = lambda src: _pack_small({n: src[n] for n in SMALL}, extra_rows=1)
    d_s, m_s, v_s = _adamw("adamw_small", pk(W), small_sum.at[loss_row].set(0.0), pk(Mo), pk(Vo))
    for dst, packed in ((grads, small_sum), (delta, d_s), (new_m, m_s), (new_v, v_s)):
        dst.update(_unpack_small(packed, shapes))

    return (loss, grad_x2.reshape(B, S, D), *[grads[n] for n in ORDER], *[delta[n] for n in ORDER],
            *[new_m[n] for n in ORDER], *[new_v[n] for n in ORDER])
```

```python
import functools

import jax
import jax.numpy as jnp
from jax import lax
from jax.experimental import pallas as pl
from jax.experimental.pallas import tpu as pltpu

F32 = jnp.float32
BF16 = jnp.bfloat16
MESH = pl.DeviceIdType.MESH

D_MODEL = 1024
SB_HEADS = 8
HEAD_DIM = 64
SB_WIDTH = SB_HEADS * HEAD_DIM
RW_WIDTH = 512
LORA_COLS = 256
SB_COLS = 3 * SB_WIDTH
RW_COLS = 3 * RW_WIDTH + LORA_COLS
GATE_COLS = 2 * D_MODEL
D_FF = 2816
RMS_EPS = 1e-6
GN_EPS = HEAD_DIM * 1e-5
WKV_CHUNK = 64
WKV_SEQS = 4
ATTN_QUERIES = 512
ATTN_KEYS = 128
ATTN_DEAD = -120.0
LANES = 128
SUBLANES = 8
N_CHIPS = 4
N_DEV = 8

ADAM_LR = 0.001
ADAM_B1 = 0.9
ADAM_B2 = 0.999
ADAM_EPS = 1e-08
ADAM_WD = 0.01
ADAM_STEP = 10

VMEM_LIMIT = 48 * 1024 * 1024
WKV_BWD_VMEM = 58 * 1024 * 1024


def _params(sem=None, vmem=VMEM_LIMIT, **kw):
    if sem is not None:
        kw["dimension_semantics"] = sem
    return pltpu.CompilerParams(vmem_limit_bytes=vmem, **kw)


def _div_tile(dim, pref, mult=LANES):
    if dim <= pref:
        return dim
    t = pref - pref % mult
    while t >= mult:
        if dim % t == 0:
            return t
        t -= mult
    return dim


def _dot(a, b, dims):
    return lax.dot_general(a, b, (dims, ((), ())), preferred_element_type=F32)


def _tile_order(n_i, n_j, n_k):
    if n_i > n_j:
        return (n_j, n_i, n_k), lambda f: (lambda j, i, k: f(i, j, k))
    return (n_i, n_j, n_k), lambda f: f


def _mm(name, a, b, *, ta=False, tb=False, acc=None, out_dtype=F32):
    if ta:
        K, M = a.shape
    else:
        M, K = a.shape
    N = b.shape[0] if tb else b.shape[1]
    if ta:
        tm, tn, tk = _div_tile(M, 1408), _div_tile(N, 1408), _div_tile(K, 512)
    else:
        tm, tn, tk = _div_tile(M, 512), _div_tile(N, 1408), _div_tile(K, 1408)
    nk = K // tk
    dims = ((0,) if ta else (1,), (1,) if tb else (0,))
    has_acc = acc is not None

    def body(*refs):
        a_ref, b_ref = refs[0], refs[1]
        part = _dot(a_ref[...].astype(BF16), b_ref[...].astype(BF16), dims)
        if nk == 1:
            o_ref = refs[-1]
            o_ref[...] = (part + refs[2][...] if has_acc else part).astype(o_ref.dtype)
            return
        o_ref, scr = refs[-2], refs[-1]
        k = pl.program_id(2)

        @pl.when(k == 0)
        def _():
            scr[...] = part + refs[2][...] if has_acc else part

        @pl.when(k > 0)
        def _():
            scr[...] += part

        @pl.when(k == nk - 1)
        def _():
            o_ref[...] = scr[...].astype(o_ref.dtype)

    grid, at = _tile_order(M // tm, N // tn, nk)
    a_spec = pl.BlockSpec((tk, tm), at(lambda i, j, k: (k, i))) if ta else pl.BlockSpec((tm, tk), at(lambda i, j, k: (i, k)))
    b_spec = pl.BlockSpec((tn, tk), at(lambda i, j, k: (j, k))) if tb else pl.BlockSpec((tk, tn), at(lambda i, j, k: (k, j)))
    o_spec = pl.BlockSpec((tm, tn), at(lambda i, j, k: (i, j)))
    return pl.pallas_call(
        body, name=name,
        grid=grid,
        in_specs=[a_spec, b_spec] + ([o_spec] if has_acc else []),
        out_specs=o_spec,
        out_shape=jax.ShapeDtypeStruct((M, N), out_dtype),
        scratch_shapes=[pltpu.VMEM((tm, tn), F32)] if nk > 1 else [],
        compiler_params=_params(("parallel", "parallel", "arbitrary")),
    )(*([a, b] + ([acc] if has_acc else [])))


def _mm_fused(name, lhs, rhs, outs, *, tb=False, add=False, extras=(), epilogue=None):
    M, K = lhs[0].shape
    N = rhs[0].shape[0] if tb else rhs[0].shape[1]
    tm, tn, tk = _div_tile(M, 512), _div_tile(N, 1408), _div_tile(K, 1408)
    nk = K // tk
    n_l, n_e, n_o = len(lhs), len(extras), len(outs)
    n_acc = 1 if add else n_l
    dims = ((1,), (1,) if tb else (0,))

    def body(*refs):
        l_refs, r_refs = refs[:n_l], refs[n_l:2 * n_l]
        e_refs = refs[2 * n_l:2 * n_l + n_e]
        o_refs = refs[2 * n_l + n_e:2 * n_l + n_e + n_o]
        scr = refs[2 * n_l + n_e + n_o:]
        parts = [_dot(l[...].astype(BF16), r[...].astype(BF16), dims) for l, r in zip(l_refs, r_refs)]
        if add:
            parts = [functools.reduce(lambda u, v: u + v, parts)]

        def finish(vals):
            res = epilogue(vals, [e[...].astype(F32) for e in e_refs]) if epilogue else vals
            for ref, val in zip(o_refs, res):
                ref[...] = val.astype(ref.dtype)

        if nk == 1:
            finish(parts)
            return
        k = pl.program_id(2)

        @pl.when(k == 0)
        def _():
            for s, part in zip(scr, parts):
                s[...] = part

        @pl.when(k > 0)
        def _():
            for s, part in zip(scr, parts):
                s[...] += part

        @pl.when(k == nk - 1)
        def _():
            finish([s[...] for s in scr])

    grid, at = _tile_order(M // tm, N // tn, nk)
    a_spec = pl.BlockSpec((tm, tk), at(lambda i, j, k: (i, k)))
    b_spec = pl.BlockSpec((tn, tk), at(lambda i, j, k: (j, k))) if tb else pl.BlockSpec((tk, tn), at(lambda i, j, k: (k, j)))
    o_spec = pl.BlockSpec((tm, tn), at(lambda i, j, k: (i, j)))
    return pl.pallas_call(
        body, name=name,
        grid=grid,
        in_specs=[a_spec] * n_l + [b_spec] * n_l + [o_spec] * n_e,
        out_specs=[o_spec] * n_o,
        out_shape=[jax.ShapeDtypeStruct((M, N), dt) for dt in outs],
        scratch_shapes=[pltpu.VMEM((tm, tn), F32)] * (n_acc if nk > 1 else 0),
        compiler_params=_params(("parallel", "parallel", "arbitrary")),
    )(*lhs, *rhs, *extras)


def _row_spec(tile, width, colblk):
    return pl.BlockSpec((tile, width), lambda i: (i, colblk))


def _full_spec(shape):
    return pl.BlockSpec(shape, lambda i: (0,) * len(shape))


def _rowwise(name, fn, rows, params, outs, tile=256):
    T = rows[0][0].shape[0]
    tile = min(tile, T)
    n_r, n_p = len(rows), len(params)

    def body(*refs):
        r = [x[...].astype(F32) for x in refs[:n_r]]
        p = [x[...].astype(F32) for x in refs[n_r:n_r + n_p]]
        for o_ref, val in zip(refs[n_r + n_p:], fn(*r, *p)):
            o_ref[...] = val.astype(o_ref.dtype)

    return pl.pallas_call(
        body, name=name,
        grid=(T // tile,),
        in_specs=[_row_spec(tile, w, cb) for _, w, cb in rows] + [_full_spec(p.shape) for p in params],
        out_specs=[_row_spec(tile, w, 0) for w, _ in outs],
        out_shape=[jax.ShapeDtypeStruct((T, w), dt) for w, dt in outs],
        compiler_params=_params(("parallel",)),
    )(*([a for a, _, _ in rows] + list(params)))


def _rowwise_vjp(name, fn, rows, params, cts, need_rows, need_params, add_to=None, tile=256, bf16_rows=()):
    add_to = add_to or {}
    T = rows[0][0].shape[0]
    tile = min(tile, T)
    n_r, n_p = len(rows), len(params)
    ct_flat = [c for group in cts for c in group]
    ct_sizes = [len(group) for group in cts]
    add_idx = sorted(add_to)
    row_out = [i for i in range(n_r) if need_rows[i]]
    par_out = [i for i in range(n_p) if need_params[i]]
    n_ct, n_add = len(ct_flat), len(add_idx)

    def body(*refs):
        pos = 0
        r = [x[...].astype(F32) for x in refs[pos:pos + n_r]]
        pos += n_r
        p = [x[...].astype(F32) for x in refs[pos:pos + n_p]]
        pos += n_p
        ct_vals = [x[...].astype(F32) for x in refs[pos:pos + n_ct]]
        pos += n_ct
        adds = {i: x[...] for i, x in zip(add_idx, refs[pos:pos + n_add])}
        pos += n_add
        drow_refs = refs[pos:pos + len(row_out)]
        pos += len(row_out)
        dpar_refs = refs[pos:pos + len(par_out)]
        ct_in, q = [], 0
        for n in ct_sizes:
            ct_in.append(functools.reduce(lambda u, v: u + v, ct_vals[q:q + n]))
            q += n
        _, vjp = jax.vjp(fn, *r, *p)
        grads = vjp(tuple(ct_in))
        for ref, i in zip(drow_refs, row_out):
            g = grads[i]
            ref[...] = (g + adds[i] if i in adds else g).astype(ref.dtype)

        @pl.when(pl.program_id(0) == 0)
        def _():
            for ref in dpar_refs:
                ref[...] = jnp.zeros_like(ref)

        for ref, i in zip(dpar_refs, par_out):
            ref[...] += grads[n_r + i]

    ct_widths = [c.shape[1] for c in ct_flat]
    in_specs = ([_row_spec(tile, w, cb) for _, w, cb in rows] + [_full_spec(p.shape) for p in params]
                + [_row_spec(tile, w, 0) for w in ct_widths] + [_row_spec(tile, rows[i][1], 0) for i in add_idx])
    out_specs = [_row_spec(tile, rows[i][1], 0) for i in row_out] + [_full_spec(params[i].shape) for i in par_out]
    out_shape = ([jax.ShapeDtypeStruct((T, rows[i][1]), BF16 if i in bf16_rows else F32) for i in row_out]
                 + [jax.ShapeDtypeStruct(params[i].shape, F32) for i in par_out])
    res = pl.pallas_call(
        body, name=name,
        grid=(T // tile,),
        in_specs=in_specs, out_specs=out_specs, out_shape=out_shape,
        compiler_params=_params(("arbitrary",)),
    )(*([a for a, _, _ in rows] + list(params) + ct_flat + [add_to[i] for i in add_idx]))
    return res[:len(row_out)], res[len(row_out):]


def _sigmoid(x):
    return 0.5 * (jnp.tanh(0.5 * x) + 1.0)


def _softplus(x):
    return jnp.maximum(x, 0.0) + jnp.log(1.0 + jnp.exp(-jnp.abs(x)))


def _rms(x, g):
    return x * lax.rsqrt(jnp.mean(x * x, axis=-1, keepdims=True) + RMS_EPS) * g


def _segsum_impl(x):
    n, w = x.shape[-1], 2 * LANES
    r = lax.shift_right_logical(lax.broadcasted_iota(jnp.int32, (w, w), 0), 6)
    c = lax.shift_right_logical(lax.broadcasted_iota(jnp.int32, (w, w), 1), 6)
    bd = (r == c).astype(BF16)
    hi = x.astype(BF16)
    rest = x - hi.astype(F32)
    mid = rest.astype(BF16)
    lo = (rest - mid.astype(F32)).astype(BF16)
    nn = ((1,), (0,))
    blocks = [_dot(hi[:, j:j + w], bd, nn) + _dot(mid[:, j:j + w], bd, nn) + _dot(lo[:, j:j + w], bd, nn)
              for j in range(0, n, w)]
    return jnp.concatenate(blocks, axis=1)


@jax.custom_vjp
def _segsum(x):
    return _segsum_impl(x)


_segsum.defvjp(lambda x: (_segsum_impl(x), None), lambda _, g: (_segsum_impl(g),))


@jax.custom_vjp
def _mmb(a, w):
    return _dot(a.astype(BF16), w.astype(BF16), ((1,), (0,)))


def _mmb_fwd(a, w):
    return _mmb(a, w), (a, w)


def _mmb_bwd(res, g):
    a, w = res
    gb = g.astype(BF16)
    return _dot(gb, w.astype(BF16), ((1,), (1,))), _dot(a.astype(BF16), gb, ((0,), (0,)))


_mmb.defvjp(_mmb_fwd, _mmb_bwd)


def _f_norm(x, g):
    return (_rms(x, g),)


def _f_post1(x, u, g2, g3):
    x1 = x + _rms(u, g2)
    return x1, _rms(x1, g3)


def _f_swiglu(ag, au):
    return (ag * _sigmoid(ag) * au,)


def _f_merge(pg1, pg2, m1, m2, b1, b2):
    return (_sigmoid(pg1 + b1) * m1 + _sigmoid(pg2 + b2) * m2,)


def _f_out(x1, f, g4):
    return (x1 + _rms(f, g4),)


def _f_rwpre(pr, pk, pv, pz, qr, qk, qv, qz, mur, muk, muv, muz, w0, wup, a0, aup, gup, k_k, k_a):
    r = pr + (qr - pr) * mur
    k = pk + (qk - pk) * muk
    v = pv + (qv - pv) * muv
    z = pz + (qz - pz) * muz
    w_raw = w0 + _mmb(jnp.tanh(z), wup)
    lw = -jnp.exp(-_softplus(-w_raw) - 0.5)
    a = _sigmoid(a0 + _mmb(z, aup))
    g = _mmb(_sigmoid(z), gup)
    kk = k * k_k
    kap = kk * lax.rsqrt(jnp.maximum(_segsum(kk * kk), 1e-24))
    k2 = k * (1.0 + (a - 1.0) * k_a)
    return r, lw, k2, v, kap, a, g


def _f_rwpost(y, r, k2, v, g, lnx_w, lnx_b, r_k):
    inv = 1.0 / HEAD_DIM
    yc = y - _segsum(y) * inv
    var = _segsum(yc * yc) * inv
    yn = yc * lax.rsqrt(var + GN_EPS) * lnx_w + lnx_b
    bonus = _segsum(r * k2 * r_k) * v
    return ((yn + bonus) * g,)


RW_GROUPS = (0, 512, 1024, 1536, RW_COLS)


def _column_groups(p):
    return [p[:, a:b] for a, b in zip(RW_GROUPS[:-1], RW_GROUPS[1:])]


def _previous_tokens(p, halo, first_of_sequence):
    rows = lax.broadcasted_iota(jnp.int32, (p.shape[0], 1), 0)
    before = jnp.where(first_of_sequence, 0.0, halo[SUBLANES - 1:SUBLANES, :])
    return jnp.where(rows == 0, before, pltpu.roll(p, 1, axis=0))


def _halo_spec(tile, order):
    per = tile // SUBLANES
    return pl.BlockSpec((SUBLANES, RW_COLS), lambda i: (jnp.maximum(order(i) * per - 1, 0), 0))


def _rw_pre(p_rw, params, S, tile=128):
    T = p_rw.shape[0]
    tile = min(tile, T)
    assert S % tile == 0
    n_p = len(params)

    def body(*refs):
        p_ref, halo_ref = refs[0], refs[1]
        par = [x[...].astype(F32) for x in refs[2:2 + n_p]]
        p = p_ref[...]
        first = lax.rem(pl.program_id(0) * tile, S) == 0
        prev = _previous_tokens(p, halo_ref[...], first)
        for o_ref, val in zip(refs[2 + n_p:], _f_rwpre(*_column_groups(p), *_column_groups(prev), *par)):
            o_ref[...] = val

    out_spec = pl.BlockSpec((tile, RW_WIDTH), lambda i: (i, 0))
    return pl.pallas_call(
        body, name="rw_pre",
        grid=(T // tile,),
        in_specs=[pl.BlockSpec((tile, RW_COLS), lambda i: (i, 0)), _halo_spec(tile, lambda i: i)]
                 + [_full_spec(q.shape) for q in params],
        out_specs=[out_spec] * 7,
        out_shape=[jax.ShapeDtypeStruct((T, RW_WIDTH), F32)] * 7,
        compiler_params=_params(("parallel",)),
    )(p_rw, p_rw, *params)


def _rw_pre_bwd(p_rw, params, cts, S, tile=128):
    T = p_rw.shape[0]
    tile = min(tile, T)
    assert S % tile == 0
    nt = T // tile
    n_p = len(params)
    ct_flat = [c for group in cts for c in group]
    ct_sizes = [len(group) for group in cts]
    n_ct = len(ct_flat)

    def body(*refs):
        p_ref, halo_ref = refs[0], refs[1]
        par = [x[...].astype(F32) for x in refs[2:2 + n_p]]
        ct_vals = [x[...] for x in refs[2 + n_p:2 + n_p + n_ct]]
        dp_ref = refs[2 + n_p + n_ct]
        dpar_refs = refs[3 + n_p + n_ct:3 + 2 * n_p + n_ct]
        carry = refs[-1]
        step = pl.program_id(0)

        @pl.when(step == 0)
        def _():
            carry[...] = jnp.zeros_like(carry)
            for ref in dpar_refs:
                ref[...] = jnp.zeros_like(ref)

        ct_in, q = [], 0
        for n in ct_sizes:
            ct_in.append(functools.reduce(lambda u, v: u + v, ct_vals[q:q + n]))
            q += n
        p = p_ref[...]
        first = lax.rem((nt - 1 - step) * tile, S) == 0
        prev = _previous_tokens(p, halo_ref[...], first)
        _, vjp = jax.vjp(_f_rwpre, *_column_groups(p), *_column_groups(prev), *par)
        grads = vjp(tuple(ct_in))
        d_here = jnp.concatenate(grads[0:4], axis=1)
        d_prev = jnp.concatenate(grads[4:8], axis=1)
        rows = lax.broadcasted_iota(jnp.int32, (tile, 1), 0)
        from_next = jnp.where(rows == tile - 1, carry[0:1, :], pltpu.roll(d_prev, tile - 1, axis=0))
        dp_ref[...] = (d_here + from_next).astype(dp_ref.dtype)
        carry[...] = jnp.broadcast_to(jnp.where(first, 0.0, d_prev[0:1, :]), carry.shape)
        for ref, g in zip(dpar_refs, grads[8:]):
            ref[...] += g

    back = lambda i: nt - 1 - i
    row = lambda w: pl.BlockSpec((tile, w), lambda i: (back(i), 0))
    res = pl.pallas_call(
        body, name="rw_pre_bwd",
        grid=(nt,),
        in_specs=[row(RW_COLS), _halo_spec(tile, back)] + [_full_spec(q.shape) for q in params]
                 + [row(RW_WIDTH)] * n_ct,
        out_specs=[row(RW_COLS)] + [_full_spec(q.shape) for q in params],
        out_shape=[jax.ShapeDtypeStruct((T, RW_COLS), BF16)] + [jax.ShapeDtypeStruct(q.shape, F32) for q in params],
        scratch_shapes=[pltpu.VMEM((SUBLANES, RW_COLS), F32)],
        compiler_params=_params(("arbitrary",)),
    )(p_rw, p_rw, *params, *ct_flat)
    return res[0], res[1:]


def _loss_head(x1, f, target, g4, tile=256):
    T, D = x1.shape
    tile = min(tile, T)

    def body(x1_ref, f_ref, t_ref, g_ref, loss_ref, dx1_ref, df_ref, dg_ref):
        (y,), vjp = jax.vjp(_f_out, x1_ref[...], f_ref[...], g_ref[...])
        err = y - t_ref[...]
        dx1, df, dg = vjp((err * (1.0 / D),))
        dx1_ref[...] = dx1
        df_ref[...] = df.astype(df_ref.dtype)

        @pl.when(pl.program_id(0) == 0)
        def _():
            loss_ref[...] = jnp.zeros_like(loss_ref)
            dg_ref[...] = jnp.zeros_like(dg_ref)

        part = jnp.sum(jnp.sum(err * err, axis=1, keepdims=True), axis=0, keepdims=True) * (0.5 / D)
        loss_ref[...] += jnp.broadcast_to(part, loss_ref.shape)
        dg_ref[...] += dg

    row = pl.BlockSpec((tile, D), lambda i: (i, 0))
    return pl.pallas_call(
        body, name="loss_head",
        grid=(T // tile,),
        in_specs=[row, row, row, _full_spec(g4.shape)],
        out_specs=[_full_spec((SUBLANES, LANES)), row, row, _full_spec(g4.shape)],
        out_shape=[jax.ShapeDtypeStruct((SUBLANES, LANES), F32), jax.ShapeDtypeStruct((T, D), F32),
                   jax.ShapeDtypeStruct((T, D), BF16), jax.ShapeDtypeStruct(g4.shape, F32)],
        compiler_params=_params(("arbitrary",)),
    )(x1, f, target, g4)


def _nn(a, b):
    return _dot(a, b, ((1,), (0,)))


def _nt(a, b):
    return _dot(a, b, ((1,), (1,)))


def _tn(a, b):
    return _dot(a, b, ((0,), (0,)))


def _split_dot(x, u2):
    hi = x.astype(BF16)
    lo = (x - hi.astype(F32)).astype(BF16)
    return _nn(jnp.concatenate([hi, lo], axis=1), u2)


def _by_head(x, masks):
    return jnp.concatenate([(x * m).astype(BF16) for m in masks], axis=0)


def _fold_heads(x2, masks):
    R = x2.shape[0] // len(masks)
    return functools.reduce(lambda u, v: u + v, [x2[h * R:(h + 1) * R] * m for h, m in enumerate(masks)])


def _head_masks():
    lane = lax.broadcasted_iota(jnp.int32, (1, LANES), 1)
    return [((lane >= h * HEAD_DIM) & (lane < (h + 1) * HEAD_DIM)).astype(F32) for h in range(LANES // HEAD_DIM)]


def _key_tri(op):
    row = lax.broadcasted_iota(jnp.int32, (ATTN_KEYS, ATTN_KEYS), 0)
    col = lax.broadcasted_iota(jnp.int32, (ATTN_KEYS, ATTN_KEYS), 1)
    u = op(row, col).astype(BF16)
    return jnp.concatenate([u, u], axis=0)


def _causal(rows):
    row = lax.broadcasted_iota(jnp.int32, (rows, ATTN_KEYS), 0)
    col = lax.broadcasted_iota(jnp.int32, (rows, ATTN_KEYS), 1)
    return col < row


def _from_row(tree, r):
    return jax.tree.map(lambda x: x[r:], tree)


def _onto_rows(old, new, r):
    return jax.tree.map(lambda o, n: jnp.concatenate([o[:r], n], axis=0) if r else n, old, new)


def _sb_weights(qb16, kbh, c_fails, u_gt, strict):
    z_all = _nt(qb16, kbh)
    zs = [z_all[:, h * ATTN_KEYS:(h + 1) * ATTN_KEYS] for h in range(len(c_fails))]
    Ls = [jnp.minimum(-z, 0.0) - jnp.log(1.0 + jnp.exp(-jnp.abs(z))) for z in zs]
    Lms = Ls if strict is None else [jnp.where(strict, L, 0.0) for L in Ls]
    cums = [_split_dot(Lm, u_gt) for Lm in Lms]
    As = [jnp.exp(z + L + c + cum) for z, L, c, cum in zip(zs, Ls, c_fails, cums)]
    if strict is not None:
        As = [jnp.where(strict, A, 0.0) for A in As]
    return zs, Ls, Lms, As


def _attn_specs(S, qb):
    nq = S // qb
    q_spec = pl.BlockSpec((qb, LANES), lambda b, p, i: (b * nq + i, p))
    k_spec = pl.BlockSpec((S, LANES), lambda b, p, i: (b, SB_WIDTH // LANES + p))
    v_spec = pl.BlockSpec((S, LANES), lambda b, p, i: (b, 2 * SB_WIDTH // LANES + p))
    seq = pl.BlockSpec((S, LANES), lambda b, p, i: (b, p))
    return q_spec, k_spec, v_spec, q_spec, seq


def _key_walk(i, qb, block, carry, fails):
    per = qb // ATTN_KEYS
    for sub in reversed(range(per)):
        carry = block(i * per + sub, carry, sub * ATTN_KEYS)
    n = i * per

    def alive(c):
        return jnp.max(functools.reduce(jnp.maximum, fails(c))) > ATTN_DEAD

    def cond(state):
        return jnp.logical_and(state[0] < n, state[1])

    def body(state):
        c = block(n - 1 - state[0], state[2], None)
        return state[0] + 1, alive(c), c

    return lax.while_loop(cond, body, (jnp.int32(0), alive(carry), carry))[2]


def _attn_fwd(proj, B, S):
    qb = min(ATTN_QUERIES, S)
    scale = HEAD_DIM ** -0.5

    def body(q_ref, k_ref, v_ref, o_ref):
        i = pl.program_id(2)
        masks = _head_masks()
        u_gt = _key_tri(lambda r, c: r > c)
        q16 = (q_ref[...] * scale).astype(BF16)

        def block(J, carry, row0):
            r0 = pl.multiple_of(J * ATTN_KEYS, ATTN_KEYS)
            kbh = _by_head(k_ref[pl.ds(r0, ATTN_KEYS), :], masks)
            vbh = _by_head(v_ref[pl.ds(r0, ATTN_KEYS), :], masks)
            lo = row0 or 0
            strict = None if row0 is None else _causal(qb - lo)
            acc, cs = _from_row(carry, lo)
            _, _, Lms, As = _sb_weights(q16[lo:], kbh, cs, u_gt, strict)
            acc = acc + _nn(jnp.concatenate([A.astype(BF16) for A in As], axis=1), vbh)
            cs = tuple(c + jnp.sum(Lm, axis=1, keepdims=True) for c, Lm in zip(cs, Lms))
            return _onto_rows(carry, (acc, cs), lo)

        zero_c = tuple(jnp.zeros((qb, 1), F32) for _ in masks)
        carry = _key_walk(i, qb, block, (jnp.zeros((qb, LANES), F32), zero_c), lambda c: c[1])
        o_ref[...] = carry[0]

    q_spec, k_spec, v_spec, blk, _ = _attn_specs(S, qb)
    return pl.pallas_call(
        body, name="sb_attn_fwd",
        grid=(B, SB_WIDTH // LANES, S // qb),
        in_specs=[q_spec, k_spec, v_spec],
        out_specs=blk,
        out_shape=jax.ShapeDtypeStruct((B * S, SB_WIDTH), F32),
        compiler_params=_params(("parallel", "parallel", "arbitrary")),
    )(proj, proj, proj)


def _attn_bwd(proj, o, do, B, S):
    qb = min(ATTN_QUERIES, S)
    nq = S // qb
    scale = HEAD_DIM ** -0.5

    def body(q_ref, k_ref, v_ref, o_ref, do_ref, dq_ref, dk_out, dv_out, dk_ref, dv_ref):
        i = pl.program_id(2)

        @pl.when(i == 0)
        def _():
            dk_ref[...] = jnp.zeros_like(dk_ref)
            dv_ref[...] = jnp.zeros_like(dv_ref)

        masks = _head_masks()
        u_gt = _key_tri(lambda r, c: r > c)
        u_ge = _key_tri(lambda r, c: r >= c)
        heads = range(len(masks))
        q16 = (q_ref[...] * scale).astype(BF16)
        do16 = do_ref[...].astype(BF16)
        od = o_ref[...] * do16.astype(F32)
        totals = tuple(jnp.sum(od * m, axis=1, keepdims=True) for m in masks)

        def block(J, carry, row0):
            r0 = pl.multiple_of(J * ATTN_KEYS, ATTN_KEYS)
            kbh = _by_head(k_ref[pl.ds(r0, ATTN_KEYS), :], masks)
            vbh = _by_head(v_ref[pl.ds(r0, ATTN_KEYS), :], masks)
            lo = row0 or 0
            strict = None if row0 is None else _causal(qb - lo)
            dq, c_fail, c_p = _from_row(carry, lo)
            tot = _from_row(totals, lo)
            zs, Ls, Lms, As = _sb_weights(q16[lo:], kbh, c_fail, u_gt, strict)
            Abs = [A.astype(BF16) for A in As]
            dA_all = _nt(do16[lo:], vbh)
            Ps = [Abs[h].astype(F32) * dA_all[:, h * ATTN_KEYS:(h + 1) * ATTN_KEYS] for h in heads]
            afters = [c_p[h] + _split_dot(Ps[h], u_ge) for h in heads]
            sigs = [jnp.exp(zs[h] + Ls[h]) for h in heads]
            dzs = [Ps[h] * (1.0 - sigs[h]) - sigs[h] * (tot[h] - afters[h]) for h in heads]
            if strict is not None:
                dzs = [jnp.where(strict, dz, 0.0) for dz in dzs]
            dz_all = jnp.concatenate([dz.astype(BF16) for dz in dzs], axis=1)
            dv_ref[pl.ds(r0, ATTN_KEYS), :] += _fold_heads(_tn(jnp.concatenate(Abs, axis=1), do16[lo:]), masks)
            dk_ref[pl.ds(r0, ATTN_KEYS), :] += _fold_heads(_tn(dz_all, q16[lo:]), masks)
            dq = dq + _nn(dz_all, kbh)
            c_fail = tuple(c_fail[h] + jnp.sum(Lms[h], axis=1, keepdims=True) for h in heads)
            c_p = tuple(c_p[h] + jnp.sum(Ps[h], axis=1, keepdims=True) for h in heads)
            return _onto_rows(carry, (dq, c_fail, c_p), lo)

        zc = tuple(jnp.zeros((qb, 1), F32) for _ in masks)
        carry = _key_walk(i, qb, block, (jnp.zeros((qb, LANES), F32), zc, zc), lambda c: c[1])
        dq_ref[...] = (carry[0] * scale).astype(dq_ref.dtype)

        @pl.when(i == nq - 1)
        def _():
            dk_out[...] = dk_ref[...].astype(dk_out.dtype)
            dv_out[...] = dv_ref[...].astype(dv_out.dtype)

    q_spec, k_spec, v_spec, blk, seq = _attn_specs(S, qb)
    return pl.pallas_call(
        body, name="sb_attn_bwd",
        grid=(B, SB_WIDTH // LANES, nq),
        in_specs=[q_spec, k_spec, v_spec, blk, blk],
        out_specs=[blk, seq, seq],
        out_shape=[jax.ShapeDtypeStruct((B * S, SB_WIDTH), BF16)] * 3,
        scratch_shapes=[pltpu.VMEM((S, LANES), F32), pltpu.VMEM((S, LANES), F32)],
        compiler_params=_params(("parallel", "parallel", "arbitrary")),
    )(proj, proj, proj, o, do)


_BATCHED = {"nn": "gmk,gkn->gmn", "nt": "gmk,gnk->gmn", "tn": "gkm,gkn->gmn"}


def _bdot_raw(a, b, kind, passes):
    e = functools.partial(jnp.einsum, _BATCHED[kind], preferred_element_type=F32)
    ah, bh = a.astype(BF16), b.astype(BF16)
    if passes == 1:
        return e(ah, bh)
    al, bl = (a - ah.astype(F32)).astype(BF16), (b - bh.astype(F32)).astype(BF16)
    return e(ah, bh) + e(ah, bl) + e(al, bh)


@functools.partial(jax.custom_vjp, nondiff_argnums=(2, 3))
def _bdot(a, b, kind, passes):
    return _bdot_raw(a, b, kind, passes)


def _bdot_fwd(a, b, kind, passes):
    return _bdot_raw(a, b, kind, passes), (a, b)


def _bdot_bwd(kind, passes, res, g):
    a, b = res
    if kind == "nn":
        return _bdot_raw(g, b, "nt", passes), _bdot_raw(a, g, "tn", passes)
    if kind == "nt":
        return _bdot_raw(g, b, "nn", passes), _bdot_raw(g, a, "tn", passes)
    return _bdot_raw(b, g, "nt", passes), _bdot_raw(a, g, "nn", passes)


_bdot.defvjp(_bdot_fwd, _bdot_bwd)


def _solve_powers(m):
    powers = [m]
    for _ in range(max(1, (m.shape[1] - 1).bit_length()) - 1):
        powers.append(_bdot_raw(powers[-1], powers[-1], "nn", 1))
    return powers


def _solve_fwd(m, rhs):
    powers = _solve_powers(m)
    x = rhs
    for p in powers:
        x = x + _bdot_raw(p, x, "nn", 1)
    return x, (powers, x)


def _solve_bwd(res, g):
    powers, x = res
    for p in powers:
        g = g + _bdot_raw(p, g, "tn", 1)
    return _bdot_raw(g, x, "nt", 1), g


@jax.custom_vjp
def _unit_lower_solve(m, rhs):
    return _solve_fwd(m, rhs)[0]


_unit_lower_solve.defvjp(_solve_fwd, _solve_bwd)


def _wkv_chunk(S0, r, lw, k, v, kap, a):
    G, C, N = r.shape
    row = lax.broadcasted_iota(jnp.int32, (C, C), 0)
    col = lax.broadcasted_iota(jnp.int32, (C, C), 1)
    incl = (col <= row).astype(F32)
    strict = (col < row).astype(F32)
    cum = _bdot(jnp.broadcast_to(incl, (G, C, C)), lw, "nn", 3)
    e_pos = jnp.exp(cum)
    e_neg = jnp.exp(-cum)
    al = -kap * jnp.exp(cum - lw)
    be = kap * a * e_neg
    kt = k * e_neg
    rt = r * e_pos
    bk = jnp.concatenate([be, kt], axis=1)
    mask = jnp.concatenate([jnp.concatenate([strict, strict], axis=1), jnp.concatenate([incl, incl], axis=1)], axis=0)
    m_all = _bdot(jnp.concatenate([al, rt], axis=1), bk, "nt", 3) * mask
    m_ab, m_ak = m_all[:, :C, :C], m_all[:, :C, C:]
    m_rb, m_rk = m_all[:, C:, :C], m_all[:, C:, C:]
    S0t = jnp.swapaxes(S0, 1, 2)
    sa = _unit_lower_solve(m_ab, _bdot(jnp.concatenate([al, m_ak], axis=2), jnp.concatenate([S0t, v], axis=1), "nn", 3))
    y =_bdot(jnp.concatenate([rt, m_rb, m_rk], axis=2), jnp.concatenate([S0t, sa, v], axis=1), "nn", 3)
    S1 = (S0 + _bdot(jnp.concatenate([sa, v], axis=1), bk, "tn", 1)) * e_pos[:, C - 1:C, :]
    return y, S1


def _split_heads(x):
    return jnp.stack([x[:, h * HEAD_DIM:(h + 1) * HEAD_DIM] for h in range(x.shape[1] // HEAD_DIM)], axis=0)


def _merge_heads(x):
    return jnp.concatenate([x[h] for h in range(x.shape[0])], axis=1)


def _seq_heads(ref):
    return jnp.concatenate([_split_heads(ref[s]) for s in range(ref.shape[0])], axis=0)


def _store_seq_heads(ref, x):
    heads = x.shape[0] // ref.shape[0]
    for s in range(ref.shape[0]):
        ref[s] = _merge_heads(x[s * heads:(s + 1) * heads])


def _wkv_fwd(r, lw, k, v, kap, a, B, S):
    C, H, N = WKV_CHUNK, RW_WIDTH // HEAD_DIM, HEAD_DIM
    nc = S // C
    Q = min(WKV_SEQS, B)

    def body(r_ref, lw_ref, k_ref, v_ref, kap_ref, a_ref, y_ref, st_ref, s_scr):
        @pl.when(pl.program_id(1) == 0)
        def _():
            s_scr[...] = jnp.zeros_like(s_scr)

        S0 = s_scr[...]
        for s in range(Q):
            st_ref[s, 0] = S0[s * H:(s + 1) * H]
        args = [_seq_heads(ref) for ref in (r_ref, lw_ref, k_ref, v_ref, kap_ref, a_ref)]
        y, S1 = _wkv_chunk(S0, *args)
        s_scr[...] = S1
        _store_seq_heads(y_ref, y)

    row_spec = pl.BlockSpec((Q, C, RW_WIDTH), lambda b, c: (b, c, 0))
    seqs = lambda t: t.reshape(B, S, RW_WIDTH)
    y, states = pl.pallas_call(
        body, name="wkv_fwd",
        grid=(B // Q, nc),
        in_specs=[row_spec] * 6,
        out_specs=[row_spec, pl.BlockSpec((Q, 1, H, N, N), lambda b, c: (b, c, 0, 0, 0))],
        out_shape=[jax.ShapeDtypeStruct((B, S, RW_WIDTH), F32), jax.ShapeDtypeStruct((B, nc, H, N, N), F32)],
        scratch_shapes=[pltpu.VMEM((Q * H, N, N), F32)],
        compiler_params=_params(("arbitrary", "arbitrary")),
    )(*map(seqs, (r, lw, k, v, kap, a)))
    return y.reshape(B * S, RW_WIDTH), states


def _wkv_bwd(r, lw, k, v, kap, a, states, dy, B, S):
    C, H, N = WKV_CHUNK, RW_WIDTH // HEAD_DIM, HEAD_DIM
    nc = S // C
    Q = min(WKV_SEQS, B)

    def body(r_ref, lw_ref, k_ref, v_ref, kap_ref, a_ref, st_ref, dy_ref,
             dr_ref, dlw_ref, dk_ref, dv_ref, dkap_ref, da_ref, ds_scr):
        @pl.when(pl.program_id(1) == 0)
        def _():
            ds_scr[...] = jnp.zeros_like(ds_scr)

        args = [_seq_heads(ref) for ref in (r_ref, lw_ref, k_ref, v_ref, kap_ref, a_ref)]
        S0 = jnp.concatenate([st_ref[s, 0] for s in range(Q)], axis=0)
        _, vjp = jax.vjp(_wkv_chunk, S0, *args)
        g = vjp((_seq_heads(dy_ref), ds_scr[...]))
        ds_scr[...] = g[0]
        for ref, gv in zip((dr_ref, dlw_ref, dk_ref, dv_ref, dkap_ref, da_ref), g[1:]):
            _store_seq_heads(ref, gv)

    row_spec = pl.BlockSpec((Q, C, RW_WIDTH), lambda b, c: (b, nc - 1 - c, 0))
    st_spec = pl.BlockSpec((Q, 1, H, N, N), lambda b, c: (b, nc - 1 - c, 0, 0, 0))
    seqs = lambda t: t.reshape(B, S, RW_WIDTH)
    res = pl.pallas_call(
        body, name="wkv_bwd",
        grid=(B // Q, nc),
        in_specs=[row_spec] * 6 + [st_spec, row_spec],
        out_specs=[row_spec] * 6,
        out_shape=[jax.ShapeDtypeStruct((B, S, RW_WIDTH), F32)] * 6,
        scratch_shapes=[pltpu.VMEM((Q * H, N, N), F32)],
        compiler_params=_params(("arbitrary", "arbitrary"), vmem=WKV_BWD_VMEM),
    )(*map(seqs, (r, lw, k, v, kap, a)), states, seqs(dy))
    return [t.reshape(B * S, RW_WIDTH) for t in res]


HBM = pl.BlockSpec(memory_space=pl.ANY)


def _place():
    return lax.axis_index("x"), lax.axis_index("y"), lax.axis_index("c")


def _other_chips(x, y):
    return [(1 - x, y), (x, 1 - y), (1 - x, 1 - y)]


def _all_gather_chips(shards):
    n = len(shards)

    def body(*refs):
        ins, outs = refs[:n], refs[n:2 * n]
        ici_send, ici_recv, d2d_send, d2d_recv, local = refs[2 * n:]
        x, y, c = _place()
        me = 2 * x + y
        sib = (x, y, 1 - c)
        chips = _other_chips(x, y)
        started, copies = [], []
        for w in range(n):
            cp = pltpu.make_async_copy(ins[w].at[c], outs[w].at[me, c], local.at[w])
            cp.start()
            copies.append(cp)
            for j, (px, py) in enumerate(chips):
                rd = pltpu.make_async_remote_copy(
                    src_ref=ins[w].at[c], dst_ref=outs[w].at[me, c], send_sem=ici_send.at[3 * w + j],
                    recv_sem=ici_recv.at[3 * w + j], device_id=(px, py, c), device_id_type=MESH)
                rd.start()
                started.append(rd)
            rd = pltpu.make_async_remote_copy(
                src_ref=ins[w].at[c], dst_ref=outs[w].at[me, c], send_sem=d2d_send.at[4 * w + 3],
                recv_sem=d2d_recv.at[4 * w + 3], device_id=sib, device_id_type=MESH)
            rd.start()
            started.append(rd)
        for w in range(n):
            for j, (px, py) in enumerate(chips):
                src = 2 * px + py
                pltpu.make_async_remote_copy(
                    src_ref=ins[w].at[c], dst_ref=outs[w].at[src, c], send_sem=ici_send.at[3 * w + j],
                    recv_sem=ici_recv.at[3 * w + j], device_id=(px, py, c), device_id_type=MESH).wait_recv()
                rd = pltpu.make_async_remote_copy(
                    src_ref=outs[w].at[src, c], dst_ref=outs[w].at[src, c], send_sem=d2d_send.at[4 * w + j],
                    recv_sem=d2d_recv.at[4 * w + j], device_id=sib, device_id_type=MESH)
                rd.start()
                started.append(rd)
        for w in range(n):
            for j, (px, py) in enumerate(chips):
                pltpu.make_async_remote_copy(
                    src_ref=ins[w].at[c], dst_ref=outs[w].at[2 * px + py, 1 - c], send_sem=d2d_send.at[4 * w + j],
                    recv_sem=d2d_recv.at[4 * w + j], device_id=sib, device_id_type=MESH).wait_recv()
            pltpu.make_async_remote_copy(
                src_ref=ins[w].at[c], dst_ref=outs[w].at[me, 1 - c], send_sem=d2d_send.at[4 * w + 3],
                recv_sem=d2d_recv.at[4 * w + 3], device_id=sib, device_id_type=MESH).wait_recv()
        for rd in started:
            rd.wait_send()
        for cp in copies:
            cp.wait()

    return pl.pallas_call(
        body, name="gather_weights",
        in_specs=[HBM] * n, out_specs=[HBM] * n,
        out_shape=[jax.ShapeDtypeStruct((N_CHIPS,) + s.shape, s.dtype) for s in shards],
        scratch_shapes=[pltpu.SemaphoreType.DMA((3 * n,)), pltpu.SemaphoreType.DMA((3 * n,)),
                        pltpu.SemaphoreType.DMA((4 * n,)), pltpu.SemaphoreType.DMA((4 * n,)),
                        pltpu.SemaphoreType.DMA((n,))],
        compiler_params=pltpu.CompilerParams(has_side_effects=True),
    )(*shards)


def _pair_split(grads):
    n = len(grads)

    def body(*refs):
        ins, theirs = refs[:n], refs[n:2 * n]
        send, recv = refs[2 * n:]
        x, y, c = _place()
        sib = (x, y, 1 - c)
        rds = []
        for w in range(n):
            rd = pltpu.make_async_remote_copy(
                src_ref=ins[w].at[:, 1 - c], dst_ref=theirs[w], send_sem=send.at[w], recv_sem=recv.at[w],
                device_id=sib, device_id_type=MESH)
            rd.start()
            rds.append(rd)
        for rd in rds:
            rd.wait_recv()
        for rd in rds:
            rd.wait_send()

    return pl.pallas_call(
        body, name="grad_pair_split",
        in_specs=[HBM] * n, out_specs=[HBM] * n,
        out_shape=[jax.ShapeDtypeStruct((g.shape[0],) + g.shape[2:], g.dtype) for g in grads],
        scratch_shapes=[pltpu.SemaphoreType.DMA((n,)), pltpu.SemaphoreType.DMA((n,))],
        compiler_params=pltpu.CompilerParams(has_side_effects=True),
    )(*grads)


def _chip_scatter(parts):
    n = len(parts)

    def body(*refs):
        ins, outs = refs[:n], refs[n:2 * n]
        send, recv = refs[2 * n:]
        x, y, c = _place()
        me = 2 * x + y
        rds = []
        for w in range(n):
            for j, (px, py) in enumerate(_other_chips(x, y)):
                s = 3 * w + j
                rd = pltpu.make_async_remote_copy(
                    src_ref=ins[w].at[2 * px + py], dst_ref=outs[w].at[j], send_sem=send.at[s], recv_sem=recv.at[s],
                    device_id=(px, py, c), device_id_type=MESH)
                rd.start()
                rds.append(rd)
        for w in range(n):
            for j, (px, py) in enumerate(_other_chips(x, y)):
                s = 3 * w + j
                pltpu.make_async_remote_copy(
                    src_ref=ins[w].at[me], dst_ref=outs[w].at[j], send_sem=send.at[s], recv_sem=recv.at[s],
                    device_id=(px, py, c), device_id_type=MESH).wait_recv()
        for rd in rds:
            rd.wait_send()

    return pl.pallas_call(
        body, name="grad_chip_scatter",
        in_specs=[HBM] * n, out_specs=[HBM] * n,
        out_shape=[jax.ShapeDtypeStruct((N_CHIPS - 1,) + p.shape[1:], p.dtype) for p in parts],
        scratch_shapes=[pltpu.SemaphoreType.DMA((3 * n,)), pltpu.SemaphoreType.DMA((3 * n,))],
        compiler_params=pltpu.CompilerParams(has_side_effects=True),
    )(*parts)


def _pair_join(bufs):
    n = len(bufs)

    def body(*refs):
        ins, outs = refs[:n], refs[n:2 * n]
        send, recv = refs[2 * n:]
        x, y, c = _place()
        sib = (x, y, 1 - c)
        rds = []
        for w in range(n):
            rd = pltpu.make_async_remote_copy(
                src_ref=ins[w].at[c], dst_ref=outs[w].at[c], send_sem=send.at[w], recv_sem=recv.at[w],
                device_id=sib, device_id_type=MESH)
            rd.start()
            rds.append(rd)
        for w in range(n):
            pltpu.make_async_remote_copy(
                src_ref=ins[w].at[c], dst_ref=outs[w].at[1 - c], send_sem=send.at[w], recv_sem=recv.at[w],
                device_id=sib, device_id_type=MESH).wait_recv()
        for rd in rds:
            rd.wait_send()

    return pl.pallas_call(
        body, name="grad_pair_join",
        in_specs=[HBM] * n, out_specs=[HBM] * n,
        out_shape=[jax.ShapeDtypeStruct(b.shape, b.dtype) for b in bufs],
        input_output_aliases={w: w for w in range(n)},
        scratch_shapes=[pltpu.SemaphoreType.DMA((n,)), pltpu.SemaphoreType.DMA((n,))],
        compiler_params=pltpu.CompilerParams(has_side_effects=True),
    )(*bufs)


def _all_reduce_small(packed):
    R = packed.shape[0]

    def body(x_ref, o_ref, buf, send, recv):
        x, y, c = _place()
        me = 4 * x + 2 * y + c
        buf[me] = x_ref[...]
        rds = []
        for rel in range(1, N_DEV):
            fx, fy, fc = (rel >> 2) & 1, (rel >> 1) & 1, rel & 1
            peer = (1 - x if fx else x, 1 - y if fy else y, 1 - c if fc else c)
            rd = pltpu.make_async_remote_copy(
                src_ref=x_ref, dst_ref=buf.at[me], send_sem=send.at[rel - 1], recv_sem=recv.at[rel - 1],
                device_id=peer, device_id_type=MESH)
            rd.start()
            rds.append((rd, peer))
        for rel in range(1, N_DEV):
            rd, (px, py, pc) = rds[rel - 1]
            pltpu.make_async_remote_copy(
                src_ref=x_ref, dst_ref=buf.at[4 * px + 2 * py + pc], send_sem=send.at[rel - 1], recv_sem=recv.at[rel - 1],
                device_id=(px, py, pc), device_id_type=MESH).wait_recv()
        for rd, _ in rds:
            rd.wait_send()
        total = buf[0]
        for d in range(1, N_DEV):
            total = total + buf[d]
        o_ref[...] = total

    return pl.pallas_call(
        body, name="all_reduce_small",
        in_specs=[pl.BlockSpec(memory_space=pltpu.VMEM)],
        out_specs=pl.BlockSpec(memory_space=pltpu.VMEM),
        out_shape=jax.ShapeDtypeStruct(packed.shape, F32),
        scratch_shapes=[pltpu.VMEM((N_DEV, R, LANES), F32), pltpu.SemaphoreType.DMA((N_DEV - 1,)),
                        pltpu.SemaphoreType.DMA((N_DEV - 1,))],
        compiler_params=pltpu.CompilerParams(has_side_effects=True),
    )(packed)


def _pair_sum(name, split, theirs, core):
    n_chip, _, Rh, C = split.shape
    tile = _div_tile(Rh, 256, 2 * SUBLANES)
    nt = Rh // tile

    def body(core_ref, a_ref, b_ref, o_ref):
        o_ref[...] = (a_ref[...] + b_ref[...]).astype(o_ref.dtype)

    return pl.pallas_call(
        body, name=name,
        grid_spec=pltpu.PrefetchScalarGridSpec(
            num_scalar_prefetch=1,
            grid=(n_chip, nt),
            in_specs=[pl.BlockSpec((None, None, tile, C), lambda j, i, core_ref: (j, core_ref[0], i, 0)),
                      pl.BlockSpec((None, tile, C), lambda j, i, core_ref: (j, i, 0))],
            out_specs=pl.BlockSpec((None, tile, C), lambda j, i, core_ref: (j, i, 0)),
        ),
        out_shape=jax.ShapeDtypeStruct((n_chip, Rh, C), BF16),
        compiler_params=_params(("parallel", "parallel")),
    )(core, split, theirs)


def _chip_sum(name, own, landed, core):
    n_in, Rh, C = landed.shape
    tile = _div_tile(Rh, 256, 2 * SUBLANES)

    def body(core_ref, *refs):
        total = refs[0][...].astype(F32)
        for ref in refs[1:n_in + 1]:
            total = total + ref[...].astype(F32)
        refs[n_in + 1][...] = total

    slot = lambda j: pl.BlockSpec((None, tile, C), lambda i, core_ref: (j, i, 0))
    return pl.pallas_call(
        body, name=name,
        grid_spec=pltpu.PrefetchScalarGridSpec(
            num_scalar_prefetch=1,
            grid=(Rh // tile,),
            in_specs=[pl.BlockSpec((None, tile, C), lambda i, core_ref: (core_ref[1], i, 0))]
                     + [slot(j) for j in range(n_in)],
            out_specs=pl.BlockSpec((None, tile, C), lambda i, core_ref: (core_ref[0], i, 0)),
        ),
        out_shape=jax.ShapeDtypeStruct((2, Rh, C), F32),
        compiler_params=_params(("parallel",)),
    )(core, own, *([landed] * n_in))


def _adamw(name, w, g, m, v):
    R, C = w.shape
    tile = _div_tile(R, 256, SUBLANES)
    c1 = 1.0 / (1.0 - ADAM_B1 ** ADAM_STEP)
    c2 = 1.0 / (1.0 - ADAM_B2 ** ADAM_STEP)

    def body(w_ref, g_ref, m_ref, v_ref, d_ref, nm_ref, nv_ref):
        g_ = g_ref[...]
        nm = ADAM_B1 * m_ref[...] + (1.0 - ADAM_B1) * g_
        nv = ADAM_B2 * v_ref[...] + (1.0 - ADAM_B2) * (g_ * g_)
        d_ref[...] = -ADAM_LR * ((nm * c1) / (jnp.sqrt(nv * c2) + ADAM_EPS) + ADAM_WD * w_ref[...])
        nm_ref[...] = nm
        nv_ref[...] = nv

    spec = pl.BlockSpec((tile, C), lambda i: (i, 0))
    return pl.pallas_call(
        body, name=name,
        grid=(R // tile,),
        in_specs=[spec] * 4, out_specs=[spec] * 3,
        out_shape=[jax.ShapeDtypeStruct((R, C), F32)] * 3,
        compiler_params=_params(("parallel",)),
    )(w, g, m, v)


SMALL =["norm_mix_pre", "b_gate", "mu_rw", "w0", "a0", "k_k", "k_a", "r_k", "lnx_w", "lnx_b",
         "norm_mix_post", "norm_ffn_pre", "norm_ffn_post"]
BIG = ["w_in", "w_up", "a_up", "g_up", "w_sb_out", "w_rw_out", "w_o", "w_ffn_gate", "w_ffn_up", "w_ffn_down"]
ROW_SHARDED = ("w_o", "w_ffn_down")
ORDER = ["norm_mix_pre", "w_in", "b_gate", "mu_rw", "w0", "w_up", "a0", "a_up", "g_up", "k_k", "k_a", "r_k",
         "lnx_w", "lnx_b", "w_sb_out", "w_rw_out", "w_o", "norm_mix_post", "norm_ffn_pre", "w_ffn_gate",
         "w_ffn_up", "w_ffn_down", "norm_ffn_post"]


def _pack_small(vals, extra_rows=0):
    rows = jnp.concatenate([vals[n].reshape(-1, LANES) for n in SMALL], axis=0)
    pad = (-(rows.shape[0] + extra_rows)) % SUBLANES + extra_rows
    return jnp.pad(rows, ((0, pad), (0, 0)))


def _unpack_small(packed, shapes):
    out, r = {}, 0
    for n in SMALL:
        size = 1
        for s in shapes[n]:
            size *= s
        out[n] = packed[r:r + size // LANES].reshape(shapes[n])
        r += size // LANES
    return out


def kernel(x, norm_mix_pre, w_in, b_gate, mu_rw, w0, w_up, a0, a_up, g_up, k_k, k_a, r_k, lnx_w, lnx_b, w_sb_out, w_rw_out, w_o, norm_mix_post, norm_ffn_pre, w_ffn_gate, w_ffn_up, w_ffn_down, norm_ffn_post, loss_target, m_norm_mix_pre, m_w_in, m_b_gate, m_mu_rw, m_w0, m_w_up, m_a0, m_a_up, m_g_up, m_k_k, m_k_a, m_r_k, m_lnx_w, m_lnx_b, m_w_sb_out, m_w_rw_out, m_w_o, m_norm_mix_post, m_norm_ffn_pre, m_w_ffn_gate, m_w_ffn_up, m_w_ffn_down, m_norm_ffn_post, v_norm_mix_pre, v_w_in, v_b_gate, v_mu_rw, v_w0, v_w_up, v_a0, v_a_up, v_g_up, v_k_k, v_k_a, v_r_k, v_lnx_w, v_lnx_b, v_w_sb_out, v_w_rw_out, v_w_o, v_norm_mix_post, v_norm_ffn_pre, v_w_ffn_gate, v_w_ffn_up, v_w_ffn_down, v_norm_ffn_post):
    W = dict(norm_mix_pre=norm_mix_pre, w_in=w_in, b_gate=b_gate, mu_rw=mu_rw, w0=w0, w_up=w_up, a0=a0, a_up=a_up,
             g_up=g_up, k_k=k_k, k_a=k_a, r_k=r_k, lnx_w=lnx_w, lnx_b=lnx_b, w_sb_out=w_sb_out, w_rw_out=w_rw_out,
             w_o=w_o, norm_mix_post=norm_mix_post, norm_ffn_pre=norm_ffn_pre, w_ffn_gate=w_ffn_gate,
             w_ffn_up=w_ffn_up, w_ffn_down=w_ffn_down, norm_ffn_post=norm_ffn_post)
    Mo = dict(norm_mix_pre=m_norm_mix_pre, w_in=m_w_in, b_gate=m_b_gate, mu_rw=m_mu_rw, w0=m_w0, w_up=m_w_up, a0=m_a0,
              a_up=m_a_up, g_up=m_g_up, k_k=m_k_k, k_a=m_k_a, r_k=m_r_k, lnx_w=m_lnx_w, lnx_b=m_lnx_b,
              w_sb_out=m_w_sb_out, w_rw_out=m_w_rw_out, w_o=m_w_o, norm_mix_post=m_norm_mix_post,
              norm_ffn_pre=m_norm_ffn_pre, w_ffn_gate=m_w_ffn_gate, w_ffn_up=m_w_ffn_up, w_ffn_down=m_w_ffn_down,
              norm_ffn_post=m_norm_ffn_post)
    Vo = dict(norm_mix_pre=v_norm_mix_pre, w_in=v_w_in, b_gate=v_b_gate, mu_rw=v_mu_rw, w0=v_w0, w_up=v_w_up, a0=v_a0,
              a_up=v_a_up, g_up=v_g_up, k_k=v_k_k, k_a=v_k_a, r_k=v_r_k, lnx_w=v_lnx_w, lnx_b=v_lnx_b,
              w_sb_out=v_w_sb_out, w_rw_out=v_w_rw_out, w_o=v_w_o, norm_mix_post=v_norm_mix_post,
              norm_ffn_pre=v_norm_ffn_pre, w_ffn_gate=v_w_ffn_gate, w_ffn_up=v_w_ffn_up, w_ffn_down=v_w_ffn_down,
              norm_ffn_post=v_norm_ffn_post)
    shapes = {n: W[n].shape for n in ORDER}
    B, S, D = x.shape
    T = B * S
    x2 = x.reshape(T, D)
    tgt = loss_target.reshape(T, D)
    vec = {n: W[n].reshape(1, -1) for n in SMALL}

    work = lambda t, n: t[0] if n in ROW_SHARDED else jnp.swapaxes(t[0], 0, 1)
    halved = [work(W[n], n).astype(BF16) for n in BIG]
    halved = [h.reshape(2, h.shape[0] // 2, h.shape[1]) for h in halved]
    full = {n: gth.reshape(-1, gth.shape[3]) for n, gth in zip(BIG, _all_gather_chips(halved))}
    w_in_t = full["w_in"]
    w_sb_t, w_rw_t, w_gt_t = w_in_t[:SB_COLS], w_in_t[SB_COLS:SB_COLS + RW_COLS], w_in_t[SB_COLS + RW_COLS:]
    lora_rows = {"w_up": 0, "a_up": 64, "g_up": 128}
    lora = {n: jnp.pad(full[n].T, ((r0, LORA_COLS - r0 - full[n].shape[1]), (0, 0))) for n, r0 in lora_rows.items()}
    mu = vec["mu_rw"]
    mu_parts = [mu[:, :512], mu[:, 512:1024], mu[:, 1024:1536], mu[:, 1536:]]
    b1, b2 = vec["b_gate"][:, :D], vec["b_gate"][:, D:]

    (h1,) = _rowwise("norm_mix_pre", _f_norm, [(x2, D, 0)], [vec["norm_mix_pre"]], [(D, BF16)], tile=512)
    p_sb = _mm("proj_sb", h1, w_sb_t, tb=True, out_dtype=BF16)
    p_rw = _mm("proj_rw", h1, w_rw_t, tb=True)
    p_gt = _mm("proj_gate", h1, w_gt_t, tb=True, out_dtype=BF16)
    o_sb = _attn_fwd(p_sb, B, S)
    pre_params = mu_parts + [vec["w0"], lora["w_up"], vec["a0"], lora["a_up"], lora["g_up"], vec["k_k"], vec["k_a"]]
    r_, lw_, k2_, v_, kap_, a_, g_ = _rw_pre(p_rw, pre_params, S)
    y_wkv, states = _wkv_fwd(r_, lw_, k2_, v_, kap_, a_, B, S)
    post_rows = [(y_wkv, 512, 0), (r_, 512, 0), (k2_, 512, 0), (v_, 512, 0), (g_, 512, 0)]
    post_params = [vec["lnx_w"], vec["lnx_b"], vec["r_k"]]
    (o_rw,) = _rowwise("rw_post", _f_rwpost, post_rows, post_params, [(512, BF16)])
    m1 = _mm("mix_sb_out", o_sb, full["w_sb_out"], tb=True, out_dtype=BF16)
    m2 = _mm("mix_rw_out", o_rw, full["w_rw_out"], tb=True, out_dtype=BF16)
    merge_rows = [(p_gt, D, 0), (p_gt, D, 1), (m1, D, 0), (m2, D, 0)]
    (merged,) = _rowwise("merge", _f_merge, merge_rows, [b1, b2], [(D, BF16)])
    u = _mm("mix_out", merged, full["w_o"])
    post1_params = [vec["norm_mix_post"], vec["norm_ffn_pre"]]
    x1, h2 = _rowwise("post_mix", _f_post1, [(x2, D, 0), (u, D, 0)], post1_params, [(D, F32), (D, BF16)], tile=512)
    ag, au, sw = _mm_fused("ffn_in", [h2, h2], [full["w_ffn_gate"], full["w_ffn_up"]], [BF16] * 3, tb=True,
                           epilogue=lambda gu, _: (gu[0], gu[1], _f_swiglu(*gu)[0]))
    f = _mm("ffn_down", sw, full["w_ffn_down"])
    loss_part, dx1, df, dg4 = _loss_head(x1, f, tgt, vec["norm_ffn_post"])

    gbig, gsmall = {}, {"norm_ffn_post": dg4}
    gbig["w_ffn_down"] = _mm("g_ffn_down", sw, df, ta=True)

    def swiglu_back(dsw, gu):
        return jax.vjp(_f_swiglu, *gu)[1]((dsw[0],))

    dag, dau = _mm_fused("ffn_back", [df], [full["w_ffn_down"]], [BF16] * 2, tb=True, extras=[ag, au],
                         epilogue=swiglu_back)
    (dh2,) = _mm_fused("d_h2", [dag, dau], [full["w_ffn_gate"], full["w_ffn_up"]], [F32], add=True)
    gbig["w_ffn_gate"] = _mm("g_ffn_gate", dag, h2, ta=True)
    gbig["w_ffn_up"] = _mm("g_ffn_up", dau, h2, ta=True)
    (dx_res, du), (dg2, dg3) = _rowwise_vjp("post_mix_bwd", _f_post1, [(x2, D, 0), (u, D, 0)], post1_params,
                                            [[dx1], [dh2]], [True, True], [True, True], bf16_rows=(1,))
    gsmall["norm_mix_post"], gsmall["norm_ffn_pre"] = dg2, dg3
    dmerged = _mm("d_merged", du, full["w_o"], tb=True, out_dtype=BF16)
    gbig["w_o"] = _mm("g_w_o", merged, du, ta=True)
    (dpg1, dpg2, dm1, dm2), (db1, db2) = _rowwise_vjp("merge_bwd", _f_merge, merge_rows, [b1, b2], [[dmerged]],
                                                      [True] * 4, [True, True], bf16_rows=(0, 1, 2, 3))
    gsmall["b_gate"] = jnp.concatenate([db1, db2], axis=1)
    do_sb = _mm("d_o_sb", dm1, full["w_sb_out"])
    do_rw = _mm("d_o_rw", dm2, full["w_rw_out"])
    gbig["w_sb_out"] = _mm("g_sb_out", dm1, o_sb, ta=True)
    gbig["w_rw_out"] = _mm("g_rw_out", dm2, o_rw, ta=True)
    (dy_wkv, dr_a, dk2_a, dv_a, dg_), (dlnx_w, dlnx_b, dr_k) = _rowwise_vjp(
        "rw_post_bwd", _f_rwpost, post_rows, post_params, [[do_rw]], [True] * 5, [True] * 3)
    gsmall["lnx_w"], gsmall["lnx_b"], gsmall["r_k"] = dlnx_w, dlnx_b, dr_k
    dr_b, dlw, dk2_b, dv_b, dkap, da = _wkv_bwd(r_, lw_, k2_, v_, kap_, a_, states, dy_wkv, B, S)
    pre_cts = [[dr_a, dr_b], [dlw], [dk2_a, dk2_b], [dv_a, dv_b], [dkap], [da], [dg_]]
    dp_rw, dpre_params = _rw_pre_bwd(p_rw, pre_params, pre_cts, S)
    gsmall["mu_rw"] = jnp.concatenate(dpre_params[:4], axis=1)
    gsmall["w0"], gsmall["a0"], gsmall["k_k"], gsmall["k_a"] = dpre_params[4], dpre_params[6], dpre_params[9], dpre_params[10]
    glora = {"w_up": dpre_params[5][0:64].T, "a_up": dpre_params[7][64:128].T, "g_up": dpre_params[8][128:256].T}
    dq, dk, dv = _attn_bwd(p_sb, o_sb, do_sb, B, S)
    (dh1,) = _mm_fused("d_h1_sb", [dq, dk, dv], [w_sb_t[:512], w_sb_t[512:1024], w_sb_t[1024:]], [F32], add=True)
    dh1 = _mm("d_h1_rw", dp_rw, w_rw_t, acc=dh1)
    (dh1,) = _mm_fused("d_h1_gate", [dpg1, dpg2], [w_gt_t[:D], w_gt_t[D:]], [F32], add=True,
                       extras=[dh1], epilogue=lambda p, e: (p[0] + e[0],))
    gbig["w_in"] = jnp.concatenate(
        [_mm("g_in_" + tag, d, h1, ta=True)
         for tag, d in (("q", dq), ("k", dk), ("v", dv), ("rw", dp_rw), ("g1", dpg1), ("g2", dpg2))], axis=0)
    (grad_x2,), (dg1,) = _rowwise_vjp("norm_mix_pre_bwd", _f_norm, [(x2, D, 0)], [vec["norm_mix_pre"]], [[dh1]],
                                      [True], [True], add_to={0: dx_res})
    gsmall["norm_mix_pre"] = dg1
    gbig.update(glora)

    split = [gbig[n].reshape(N_CHIPS, 2, gbig[n].shape[0] // (2 * N_CHIPS), gbig[n].shape[1]) for n in BIG]
    core = jnp.stack([lax.axis_index("c"), 2 * lax.axis_index("x") + lax.axis_index("y")]).astype(jnp.int32)
    theirs = _pair_split(split)
    chip_sums = [_pair_sum("pair_sum_" + n, a, b, core) for n, a, b in zip(BIG, split, theirs)]
    landed = _chip_scatter(chip_sums)
    joined = _pair_join([_chip_sum("chip_sum_" + n, own, got, core) for n, own, got in zip(BIG, chip_sums, landed)])
    grads = {n: j.reshape(-1, j.shape[2]) for n, j in zip(BIG, joined)}

    small_local = _pack_small({n: gsmall[n] for n in SMALL}, extra_rows=1)
    loss_row = small_local.shape[0] - 1
    small_local = small_local.at[loss_row].set(loss_part[0])
    small_sum = _all_reduce_small(small_local)
    loss = small_sum[loss_row, 0]

    delta, new_m, new_v = {}, {}, {}
    unwork = lambda t, n: (t if n in ROW_SHARDED else jnp.swapaxes(t, 0, 1))[None]
    for n in BIG:
        d_, m_, v2_ = _adamw("adamw_" + n, work(W[n], n), grads[n], work(Mo[n], n), work(Vo[n], n))
        delta[n], new_m[n], new_v[n], grads[n] = (unwork(t, n) for t in (d_, m_, v2_, grads[n]))
    pk = lambda src: _pack_small({n: src[n] for n in SMALL}, extra_rows=1)
    d_s, m_s, v_s = _adamw("adamw_small", pk(W), small_sum.at[loss_row].set(0.0), pk(Mo), pk(Vo))
    for dst, packed in ((grads, small_sum), (delta, d_s), (new_m, m_s), (new_v, v_s)):
        dst.update(_unpack_small(packed, shapes))

    return (loss, grad_x2.reshape(B, S, D), *[grads[n] for n in ORDER], *[delta[n] for n in ORDER],
            *[new_m[n] for n in ORDER], *[new_v[n] for n in ORDER])
```

```python
import functools

import jax
import jax.numpy as jnp
from jax import lax
from jax.experimental import pallas as pl
from jax.experimental.pallas import tpu as pltpu

F32 = jnp.float32
BF16 = jnp.bfloat16
MESH = pl.DeviceIdType.MESH

D_MODEL = 1024
SB_HEADS = 8
HEAD_DIM = 64
SB_WIDTH = SB_HEADS * HEAD_DIM
RW_WIDTH = 512
LORA_COLS = 256
SB_COLS = 3 * SB_WIDTH
RW_COLS = 3 * RW_WIDTH + LORA_COLS
GATE_COLS = 2 * D_MODEL
D_FF = 2816
RMS_EPS = 1e-6
GN_EPS = HEAD_DIM * 1e-5
WKV_CHUNK = 64
WKV_SEQS = 4
ATTN_QUERIES = 512
ATTN_KEYS = 128
ATTN_DEAD = -120.0
LANES = 128
SUBLANES = 8
N_CHIPS = 4
N_DEV = 8

ADAM_LR = 0.001
ADAM_B1 = 0.9
ADAM_B2 = 0.999
ADAM_EPS = 1e-08
ADAM_WD = 0.01
ADAM_STEP = 10

VMEM_LIMIT = 48 * 1024 * 1024
WKV_BWD_VMEM = 58 * 1024 * 1024


def _params(sem=None, vmem=VMEM_LIMIT, **kw):
    if sem is not None:
        kw["dimension_semantics"] = sem
    return pltpu.CompilerParams(vmem_limit_bytes=vmem, **kw)


def _div_tile(dim, pref, mult=LANES):
    if dim <= pref:
        return dim
    t = pref - pref % mult
    while t >= mult:
        if dim % t == 0:
            return t
        t -= mult
    return dim


def _dot(a, b, dims):
    return lax.dot_general(a, b, (dims, ((), ())), preferred_element_type=F32)


def _mm(name, a, b, *, ta=False, tb=False, acc=None, out_dtype=F32):
    if ta:
        K, M = a.shape
    else:
        M, K = a.shape
    N = b.shape[0] if tb else b.shape[1]
    if ta:
        tm, tn, tk = _div_tile(M, 1408), _div_tile(N, 1408), _div_tile(K, 512)
    else:
        tm, tn, tk = _div_tile(M, 512), _div_tile(N, 1408), _div_tile(K, 1408)
    nk = K // tk
    dims = ((0,) if ta else (1,), (1,) if tb else (0,))
    has_acc = acc is not None

    def body(*refs):
        a_ref, b_ref = refs[0], refs[1]
        part = _dot(a_ref[...].astype(BF16), b_ref[...].astype(BF16), dims)
        if nk == 1:
            o_ref = refs[-1]
            o_ref[...] = (part + refs[2][...] if has_acc else part).astype(o_ref.dtype)
            return
        o_ref, scr = refs[-2], refs[-1]
        k = pl.program_id(2)

        @pl.when(k == 0)
        def _():
            scr[...] = part + refs[2][...] if has_acc else part

        @pl.when(k > 0)
        def _():
            scr[...] += part

        @pl.when(k == nk - 1)
        def _():
            o_ref[...] = scr[...].astype(o_ref.dtype)

    a_spec = pl.BlockSpec((tk, tm), lambda i, j, k: (k, i)) if ta else pl.BlockSpec((tm, tk), lambda i, j, k: (i, k))
    b_spec = pl.BlockSpec((tn, tk), lambda i, j, k: (j, k)) if tb else pl.BlockSpec((tk, tn), lambda i, j, k: (k, j))
    o_spec = pl.BlockSpec((tm, tn), lambda i, j, k: (i, j))
    return pl.pallas_call(
        body, name=name,
        grid=(M // tm, N // tn, nk),
        in_specs=[a_spec, b_spec] + ([o_spec] if has_acc else []),
        out_specs=o_spec,
        out_shape=jax.ShapeDtypeStruct((M, N), out_dtype),
        scratch_shapes=[pltpu.VMEM((tm, tn), F32)] if nk > 1 else [],
        compiler_params=_params(("parallel", "parallel", "arbitrary")),
    )(*([a, b] + ([acc] if has_acc else [])))


def _mm_fused(name, lhs, rhs, outs, *, tb=False, add=False, extras=(), epilogue=None):
    M, K = lhs[0].shape
    N = rhs[0].shape[0] if tb else rhs[0].shape[1]
    tm, tn, tk = _div_tile(M, 512), _div_tile(N, 1408), _div_tile(K, 1408)
    nk = K // tk
    n_l, n_e, n_o = len(lhs), len(extras), len(outs)
    n_acc = 1 if add else n_l
    dims = ((1,), (1,) if tb else (0,))

    def body(*refs):
        l_refs, r_refs = refs[:n_l], refs[n_l:2 * n_l]
        e_refs = refs[2 * n_l:2 * n_l + n_e]
        o_refs = refs[2 * n_l + n_e:2 * n_l + n_e + n_o]
        scr = refs[2 * n_l + n_e + n_o:]
        parts = [_dot(l[...].astype(BF16), r[...].astype(BF16), dims) for l, r in zip(l_refs, r_refs)]
        if add:
            parts = [functools.reduce(lambda u, v: u + v, parts)]

        def finish(vals):
            res = epilogue(vals, [e[...].astype(F32) for e in e_refs]) if epilogue else vals
            for ref, val in zip(o_refs, res):
                ref[...] = val.astype(ref.dtype)

        if nk == 1:
            finish(parts)
            return
        k = pl.program_id(2)

        @pl.when(k == 0)
        def _():
            for s, part in zip(scr, parts):
                s[...] = part

        @pl.when(k > 0)
        def _():
            for s, part in zip(scr, parts):
                s[...] += part

        @pl.when(k == nk - 1)
        def _():
            finish([s[...] for s in scr])

    a_spec = pl.BlockSpec((tm, tk), lambda i, j, k: (i, k))
    b_spec = pl.BlockSpec((tn, tk), lambda i, j, k: (j, k)) if tb else pl.BlockSpec((tk, tn), lambda i, j, k: (k, j))
    o_spec = pl.BlockSpec((tm, tn), lambda i, j, k: (i, j))
    return pl.pallas_call(
        body, name=name,
        grid=(M // tm, N // tn, nk),
        in_specs=[a_spec] * n_l + [b_spec] * n_l + [o_spec] * n_e,
        out_specs=[o_spec] * n_o,
        out_shape=[jax.ShapeDtypeStruct((M, N), dt) for dt in outs],
        scratch_shapes=[pltpu.VMEM((tm, tn), F32)] * (n_acc if nk > 1 else 0),
        compiler_params=_params(("parallel", "parallel", "arbitrary")),
    )(*lhs, *rhs, *extras)


def _row_spec(tile, width, colblk):
    return pl.BlockSpec((tile, width), lambda i: (i, colblk))


def _full_spec(shape):
    return pl.BlockSpec(shape, lambda i: (0,) * len(shape))


def _rowwise(name, fn, rows, params, outs, tile=256):
    T = rows[0][0].shape[0]
    tile = min(tile, T)
    n_r, n_p = len(rows), len(params)

    def body(*refs):
        r = [x[...].astype(F32) for x in refs[:n_r]]
        p = [x[...].astype(F32) for x in refs[n_r:n_r + n_p]]
        for o_ref, val in zip(refs[n_r + n_p:], fn(*r, *p)):
            o_ref[...] = val.astype(o_ref.dtype)

    return pl.pallas_call(
        body, name=name,
        grid=(T // tile,),
        in_specs=[_row_spec(tile, w, cb) for _, w, cb in rows] + [_full_spec(p.shape) for p in params],
        out_specs=[_row_spec(tile, w, 0) for w, _ in outs],
        out_shape=[jax.ShapeDtypeStruct((T, w), dt) for w, dt in outs],
        compiler_params=_params(("parallel",)),
    )(*([a for a, _, _ in rows] + list(params)))


def _rowwise_vjp(name, fn, rows, params, cts, need_rows, need_params, add_to=None, tile=256, bf16_rows=()):
    add_to = add_to or {}
    T = rows[0][0].shape[0]
    tile = min(tile, T)
    n_r, n_p = len(rows), len(params)
    ct_flat = [c for group in cts for c in group]
    ct_sizes = [len(group) for group in cts]
    add_idx = sorted(add_to)
    row_out = [i for i in range(n_r) if need_rows[i]]
    par_out = [i for i in range(n_p) if need_params[i]]
    n_ct, n_add = len(ct_flat), len(add_idx)

    def body(*refs):
        pos = 0
        r = [x[...].astype(F32) for x in refs[pos:pos + n_r]]
        pos += n_r
        p = [x[...].astype(F32) for x in refs[pos:pos + n_p]]
        pos += n_p
        ct_vals = [x[...].astype(F32) for x in refs[pos:pos + n_ct]]
        pos += n_ct
        adds = {i: x[...] for i, x in zip(add_idx, refs[pos:pos + n_add])}
        pos += n_add
        drow_refs = refs[pos:pos + len(row_out)]
        pos += len(row_out)
        dpar_refs = refs[pos:pos + len(par_out)]
        ct_in, q = [], 0
        for n in ct_sizes:
            ct_in.append(functools.reduce(lambda u, v: u + v, ct_vals[q:q + n]))
            q += n
        _, vjp = jax.vjp(fn, *r, *p)
        grads = vjp(tuple(ct_in))
        for ref, i in zip(drow_refs, row_out):
            g = grads[i]
            ref[...] = (g + adds[i] if i in adds else g).astype(ref.dtype)

        @pl.when(pl.program_id(0) == 0)
        def _():
            for ref in dpar_refs:
                ref[...] = jnp.zeros_like(ref)

        for ref, i in zip(dpar_refs, par_out):
            ref[...] += grads[n_r + i]

    ct_widths = [c.shape[1] for c in ct_flat]
    in_specs = ([_row_spec(tile, w, cb) for _, w, cb in rows] + [_full_spec(p.shape) for p in params]
                + [_row_spec(tile, w, 0) for w in ct_widths] + [_row_spec(tile, rows[i][1], 0) for i in add_idx])
    out_specs = [_row_spec(tile, rows[i][1], 0) for i in row_out] + [_full_spec(params[i].shape) for i in par_out]
    out_shape = ([jax.ShapeDtypeStruct((T, rows[i][1]), BF16 if i in bf16_rows else F32) for i in row_out]
                 + [jax.ShapeDtypeStruct(params[i].shape, F32) for i in par_out])
    res = pl.pallas_call(
        body, name=name,
        grid=(T // tile,),
        in_specs=in_specs, out_specs=out_specs, out_shape=out_shape,
        compiler_params=_params(("arbitrary",)),
    )(*([a for a, _, _ in rows] + list(params) + ct_flat + [add_to[i] for i in add_idx]))
    return res[:len(row_out)], res[len(row_out):]


def _sigmoid(x):
    return 0.5 * (jnp.tanh(0.5 * x) + 1.0)


def _softplus(x):
    return jnp.maximum(x, 0.0) + jnp.log(1.0 + jnp.exp(-jnp.abs(x)))


def _rms(x, g):
    return x * lax.rsqrt(jnp.mean(x * x, axis=-1, keepdims=True) + RMS_EPS) * g


def _segsum_impl(x):
    n, w = x.shape[-1], 2 * LANES
    r = lax.shift_right_logical(lax.broadcasted_iota(jnp.int32, (w, w), 0), 6)
    c = lax.shift_right_logical(lax.broadcasted_iota(jnp.int32, (w, w), 1), 6)
    bd = (r == c).astype(BF16)
    hi = x.astype(BF16)
    lo = (x - hi.astype(F32)).astype(BF16)
    nn = ((1,), (0,))
    blocks = [_dot(hi[:, j:j + w], bd, nn) + _dot(lo[:, j:j + w], bd, nn) for j in range(0, n, w)]
    return jnp.concatenate(blocks, axis=1)


@jax.custom_vjp
def _segsum(x):
    return _segsum_impl(x)


_segsum.defvjp(lambda x: (_segsum_impl(x), None), lambda _, g: (_segsum_impl(g),))


@jax.custom_vjp
def _mmb(a, w):
    return _dot(a.astype(BF16), w.astype(BF16), ((1,), (0,)))


def _mmb_fwd(a, w):
    return _mmb(a, w), (a, w)


def _mmb_bwd(res, g):
    a, w = res
    gb = g.astype(BF16)
    return _dot(gb, w.astype(BF16), ((1,), (1,))), _dot(a.astype(BF16), gb, ((0,), (0,)))


_mmb.defvjp(_mmb_fwd, _mmb_bwd)


def _f_norm(x, g):
    return (_rms(x, g),)


def _f_post1(x, u, g2, g3):
    x1 = x + _rms(u, g2)
    return x1, _rms(x1, g3)


def _f_swiglu(ag, au):
    return (ag * _sigmoid(ag) * au,)


def _f_merge(pg1, pg2, m1, m2, b1, b2):
    return (_sigmoid(pg1 + b1) * m1 + _sigmoid(pg2 + b2) * m2,)


def _f_out(x1, f, g4):
    return (x1 + _rms(f, g4),)


def _f_rwpre(pr, pk, pv, pz, qr, qk, qv, qz, mur, muk, muv, muz, w0, wup, a0, aup, gup, k_k, k_a):
    r = pr + (qr - pr) * mur
    k = pk + (qk - pk) * muk
    v = pv + (qv - pv) * muv
    z = pz + (qz - pz) * muz
    w_raw = w0 + _mmb(jnp.tanh(z), wup)
    lw = -jnp.exp(-_softplus(-w_raw) - 0.5)
    a = _sigmoid(a0 + _mmb(z, aup))
    g = _mmb(_sigmoid(z), gup)
    kk = k * k_k
    kap = kk * lax.rsqrt(jnp.maximum(_segsum(kk * kk), 1e-24))
    k2 = k * (1.0 + (a - 1.0) * k_a)
    return r, lw, k2, v, kap, a, g


def _f_rwpost(y, r, k2, v, g, lnx_w, lnx_b, r_k):
    inv = 1.0 / HEAD_DIM
    yc = y - _segsum(y) * inv
    var = _segsum(yc * yc) * inv
    yn = yc * lax.rsqrt(var + GN_EPS) * lnx_w + lnx_b
    bonus = _segsum(r * k2 * r_k) * v
    return ((yn + bonus) * g,)


RW_GROUPS = (0, 512, 1024, 1536, RW_COLS)


def _column_groups(p):
    return [p[:, a:b] for a, b in zip(RW_GROUPS[:-1], RW_GROUPS[1:])]


def _previous_tokens(p, halo, first_of_sequence):
    rows = lax.broadcasted_iota(jnp.int32, (p.shape[0], 1), 0)
    before = jnp.where(first_of_sequence, 0.0, halo[SUBLANES - 1:SUBLANES, :])
    return jnp.where(rows == 0, before, pltpu.roll(p, 1, axis=0))


def _halo_spec(tile, order):
    per = tile // SUBLANES
    return pl.BlockSpec((SUBLANES, RW_COLS), lambda i: (jnp.maximum(order(i) * per - 1, 0), 0))


def _rw_pre(p_rw, params, S, tile=128):
    T = p_rw.shape[0]
    tile = min(tile, T)
    assert S % tile == 0
    n_p = len(params)

    def body(*refs):
        p_ref, halo_ref = refs[0], refs[1]
        par = [x[...].astype(F32) for x in refs[2:2 + n_p]]
        p = p_ref[...]
        first = lax.rem(pl.program_id(0) * tile, S) == 0
        prev = _previous_tokens(p, halo_ref[...], first)
        for o_ref, val in zip(refs[2 + n_p:], _f_rwpre(*_column_groups(p), *_column_groups(prev), *par)):
            o_ref[...] = val

    out_spec = pl.BlockSpec((tile, RW_WIDTH), lambda i: (i, 0))
    return pl.pallas_call(
        body, name="rw_pre",
        grid=(T // tile,),
        in_specs=[pl.BlockSpec((tile, RW_COLS), lambda i: (i, 0)), _halo_spec(tile, lambda i: i)]
                 + [_full_spec(q.shape) for q in params],
        out_specs=[out_spec] * 7,
        out_shape=[jax.ShapeDtypeStruct((T, RW_WIDTH), F32)] * 7,
        compiler_params=_params(("parallel",)),
    )(p_rw, p_rw, *params)


def _rw_pre_bwd(p_rw, params, cts, S, tile=128):
    T = p_rw.shape[0]
    tile = min(tile, T)
    assert S % tile == 0
    nt = T // tile
    n_p = len(params)
    ct_flat = [c for group in cts for c in group]
    ct_sizes = [len(group) for group in cts]
    n_ct = len(ct_flat)

    def body(*refs):
        p_ref, halo_ref = refs[0], refs[1]
        par = [x[...].astype(F32) for x in refs[2:2 + n_p]]
        ct_vals = [x[...] for x in refs[2 + n_p:2 + n_p + n_ct]]
        dp_ref = refs[2 + n_p + n_ct]
        dpar_refs = refs[3 + n_p + n_ct:3 + 2 * n_p + n_ct]
        carry = refs[-1]
        step = pl.program_id(0)

        @pl.when(step == 0)
        def _():
            carry[...] = jnp.zeros_like(carry)
            for ref in dpar_refs:
                ref[...] = jnp.zeros_like(ref)

        ct_in, q = [], 0
        for n in ct_sizes:
            ct_in.append(functools.reduce(lambda u, v: u + v, ct_vals[q:q + n]))
            q += n
        p = p_ref[...]
        first = lax.rem((nt - 1 - step) * tile, S) == 0
        prev = _previous_tokens(p, halo_ref[...], first)
        _, vjp = jax.vjp(_f_rwpre, *_column_groups(p), *_column_groups(prev), *par)
        grads = vjp(tuple(ct_in))
        d_here = jnp.concatenate(grads[0:4], axis=1)
        d_prev = jnp.concatenate(grads[4:8], axis=1)
        rows = lax.broadcasted_iota(jnp.int32, (tile, 1), 0)
        from_next = jnp.where(rows == tile - 1, carry[0:1, :], pltpu.roll(d_prev, tile - 1, axis=0))
        dp_ref[...] = (d_here + from_next).astype(dp_ref.dtype)
        carry[...] = jnp.broadcast_to(jnp.where(first, 0.0, d_prev[0:1, :]), carry.shape)
        for ref, g in zip(dpar_refs, grads[8:]):
            ref[...] += g

    back = lambda i: nt - 1 - i
    row = lambda w: pl.BlockSpec((tile, w), lambda i: (back(i), 0))
    res = pl.pallas_call(
        body, name="rw_pre_bwd",
        grid=(nt,),
        in_specs=[row(RW_COLS), _halo_spec(tile, back)] + [_full_spec(q.shape) for q in params]
                 + [row(RW_WIDTH)] * n_ct,
        out_specs=[row(RW_COLS)] + [_full_spec(q.shape) for q in params],
        out_shape=[jax.ShapeDtypeStruct((T, RW_COLS), BF16)] + [jax.ShapeDtypeStruct(q.shape, F32) for q in params],
        scratch_shapes=[pltpu.VMEM((SUBLANES, RW_COLS), F32)],
        compiler_params=_params(("arbitrary",)),
    )(p_rw, p_rw, *params, *ct_flat)
    return res[0], res[1:]


def _loss_head(x1, f, target, g4, tile=256):
    T, D = x1.shape
    tile = min(tile, T)

    def body(x1_ref, f_ref, t_ref, g_ref, loss_ref, dx1_ref, df_ref, dg_ref):
        (y,), vjp = jax.vjp(_f_out, x1_ref[...], f_ref[...], g_ref[...])
        err = y - t_ref[...]
        dx1, df, dg = vjp((err * (1.0 / D),))
        dx1_ref[...] = dx1
        df_ref[...] = df.astype(df_ref.dtype)

        @pl.when(pl.program_id(0) == 0)
        def _():
            loss_ref[...] = jnp.zeros_like(loss_ref)
            dg_ref[...] = jnp.zeros_like(dg_ref)

        part = jnp.sum(jnp.sum(err * err, axis=1, keepdims=True), axis=0, keepdims=True) * (0.5 / D)
        loss_ref[...] += jnp.broadcast_to(part, loss_ref.shape)
        dg_ref[...] += dg

    row = pl.BlockSpec((tile, D), lambda i: (i, 0))
    return pl.pallas_call(
        body, name="loss_head",
        grid=(T // tile,),
        in_specs=[row, row, row, _full_spec(g4.shape)],
        out_specs=[_full_spec((SUBLANES, LANES)), row, row, _full_spec(g4.shape)],
        out_shape=[jax.ShapeDtypeStruct((SUBLANES, LANES), F32), jax.ShapeDtypeStruct((T, D), F32),
                   jax.ShapeDtypeStruct((T, D), BF16), jax.ShapeDtypeStruct(g4.shape, F32)],
        compiler_params=_params(("arbitrary",)),
    )(x1, f, target, g4)


def _nn(a, b):
    return _dot(a, b, ((1,), (0,)))


def _nt(a, b):
    return _dot(a, b, ((1,), (1,)))


def _tn(a, b):
    return _dot(a, b, ((0,), (0,)))


def _split_dot(x, u2):
    hi = x.astype(BF16)
    lo = (x - hi.astype(F32)).astype(BF16)
    return _nn(jnp.concatenate([hi, lo], axis=1), u2)


def _by_head(x, masks):
    return jnp.concatenate([(x * m).astype(BF16) for m in masks], axis=0)


def _fold_heads(x2, masks):
    R = x2.shape[0] // len(masks)
    return functools.reduce(lambda u, v: u + v, [x2[h * R:(h + 1) * R] * m for h, m in enumerate(masks)])


def _head_masks():
    lane = lax.broadcasted_iota(jnp.int32, (1, LANES), 1)
    return [((lane >= h * HEAD_DIM) & (lane < (h + 1) * HEAD_DIM)).astype(F32) for h in range(LANES // HEAD_DIM)]


def _key_tri(op):
    row = lax.broadcasted_iota(jnp.int32, (ATTN_KEYS, ATTN_KEYS), 0)
    col = lax.broadcasted_iota(jnp.int32, (ATTN_KEYS, ATTN_KEYS), 1)
    u = op(row, col).astype(BF16)
    return jnp.concatenate([u, u], axis=0)


def _causal(rows):
    row = lax.broadcasted_iota(jnp.int32, (rows, ATTN_KEYS), 0)
    col = lax.broadcasted_iota(jnp.int32, (rows, ATTN_KEYS), 1)
    return col < row


def _from_row(tree, r):
    return jax.tree.map(lambda x: x[r:], tree)


def _onto_rows(old, new, r):
    return jax.tree.map(lambda o, n: jnp.concatenate([o[:r], n], axis=0) if r else n, old, new)


def _sb_weights(qb16, kbh, c_fails, u_gt, strict):
    z_all = _nt(qb16, kbh)
    zs = [z_all[:, h * ATTN_KEYS:(h + 1) * ATTN_KEYS] for h in range(len(c_fails))]
    Ls = [jnp.minimum(-z, 0.0) - jnp.log(1.0 + jnp.exp(-jnp.abs(z))) for z in zs]
    Lms = Ls if strict is None else [jnp.where(strict, L, 0.0) for L in Ls]
    cums = [_split_dot(Lm, u_gt) for Lm in Lms]
    As = [jnp.exp(z + L + c + cum) for z, L, c, cum in zip(zs, Ls, c_fails, cums)]
    if strict is not None:
        As = [jnp.where(strict, A, 0.0) for A in As]
    return zs, Ls, Lms, As


def _attn_specs(S, qb):
    nq = S // qb
    q_spec = pl.BlockSpec((qb, LANES), lambda b, p, i: (b * nq + i, p))
    k_spec = pl.BlockSpec((S, LANES), lambda b, p, i: (b, SB_WIDTH // LANES + p))
    v_spec = pl.BlockSpec((S, LANES), lambda b, p, i: (b, 2 * SB_WIDTH // LANES + p))
    seq = pl.BlockSpec((S, LANES), lambda b, p, i: (b, p))
    return q_spec, k_spec, v_spec, q_spec, seq


def _key_walk(i, qb, block, carry, fails):
    per = qb // ATTN_KEYS
    for sub in reversed(range(per)):
        carry = block(i * per + sub, carry, sub * ATTN_KEYS)
    n = i * per

    def alive(c):
        return jnp.max(functools.reduce(jnp.maximum, fails(c))) > ATTN_DEAD

    def cond(state):
        return jnp.logical_and(state[0] < n, state[1])

    def body(state):
        c = block(n - 1 - state[0], state[2], None)
        return state[0] + 1, alive(c), c

    return lax.while_loop(cond, body, (jnp.int32(0), alive(carry), carry))[2]


def _attn_fwd(proj, B, S):
    qb = min(ATTN_QUERIES, S)
    scale = HEAD_DIM ** -0.5

    def body(q_ref, k_ref, v_ref, o_ref):
        i = pl.program_id(2)
        masks = _head_masks()
        u_gt = _key_tri(lambda r, c: r > c)
        q16 = (q_ref[...] * scale).astype(BF16)

        def block(J, carry, row0):
            r0 = pl.multiple_of(J * ATTN_KEYS, ATTN_KEYS)
            kbh = _by_head(k_ref[pl.ds(r0, ATTN_KEYS), :], masks)
            vbh = _by_head(v_ref[pl.ds(r0, ATTN_KEYS), :], masks)
            lo = row0 or 0
            strict = None if row0 is None else _causal(qb - lo)
            acc, cs = _from_row(carry, lo)
            _, _, Lms, As = _sb_weights(q16[lo:], kbh, cs, u_gt, strict)
            acc = acc + _nn(jnp.concatenate([A.astype(BF16) for A in As], axis=1), vbh)
            cs = tuple(c + jnp.sum(Lm, axis=1, keepdims=True) for c, Lm in zip(cs, Lms))
            return _onto_rows(carry, (acc, cs), lo)

        zero_c = tuple(jnp.zeros((qb, 1), F32) for _ in masks)
        carry = _key_walk(i, qb, block, (jnp.zeros((qb, LANES), F32), zero_c), lambda c: c[1])
        o_ref[...] = carry[0]

    q_spec, k_spec, v_spec, blk, _ = _attn_specs(S, qb)
    return pl.pallas_call(
        body, name="sb_attn_fwd",
        grid=(B, SB_WIDTH // LANES, S // qb),
        in_specs=[q_spec, k_spec, v_spec],
        out_specs=blk,
        out_shape=jax.ShapeDtypeStruct((B * S, SB_WIDTH), F32),
        compiler_params=_params(("parallel", "parallel", "arbitrary")),
    )(proj, proj, proj)


def _attn_bwd(proj, o, do, B, S):
    qb = min(ATTN_QUERIES, S)
    nq = S // qb
    scale = HEAD_DIM ** -0.5

    def body(q_ref, k_ref, v_ref, o_ref, do_ref, dq_ref, dk_out, dv_out, dk_ref, dv_ref):
        i = pl.program_id(2)

        @pl.when(i == 0)
        def _():
            dk_ref[...] = jnp.zeros_like(dk_ref)
            dv_ref[...] = jnp.zeros_like(dv_ref)

        masks = _head_masks()
        u_gt = _key_tri(lambda r, c: r > c)
        u_ge = _key_tri(lambda r, c: r >= c)
        heads = range(len(masks))
        q16 = (q_ref[...] * scale).astype(BF16)
        do16 = do_ref[...].astype(BF16)
        od = o_ref[...] * do16.astype(F32)
        totals = tuple(jnp.sum(od * m, axis=1, keepdims=True) for m in masks)

        def block(J, carry, row0):
            r0 = pl.multiple_of(J * ATTN_KEYS, ATTN_KEYS)
            kbh = _by_head(k_ref[pl.ds(r0, ATTN_KEYS), :], masks)
            vbh = _by_head(v_ref[pl.ds(r0, ATTN_KEYS), :], masks)
            lo = row0 or 0
            strict = None if row0 is None else _causal(qb - lo)
            dq, c_fail, c_p = _from_row(carry, lo)
            tot = _from_row(totals, lo)
            zs, Ls, Lms, As = _sb_weights(q16[lo:], kbh, c_fail, u_gt, strict)
            Abs = [A.astype(BF16) for A in As]
            dA_all = _nt(do16[lo:], vbh)
            Ps = [Abs[h].astype(F32) * dA_all[:, h * ATTN_KEYS:(h + 1) * ATTN_KEYS] for h in heads]
            afters = [c_p[h] + _split_dot(Ps[h], u_ge) for h in heads]
            sigs = [jnp.exp(zs[h] + Ls[h]) for h in heads]
            dzs = [Ps[h] * (1.0 - sigs[h]) - sigs[h] * (tot[h] - afters[h]) for h in heads]
            if strict is not None:
                dzs = [jnp.where(strict, dz, 0.0) for dz in dzs]
            dz_all = jnp.concatenate([dz.astype(BF16) for dz in dzs], axis=1)
            dv_ref[pl.ds(r0, ATTN_KEYS), :] += _fold_heads(_tn(jnp.concatenate(Abs, axis=1), do16[lo:]), masks)
            dk_ref[pl.ds(r0, ATTN_KEYS), :] += _fold_heads(_tn(dz_all, q16[lo:]), masks)
            dq = dq + _nn(dz_all, kbh)
            c_fail = tuple(c_fail[h] + jnp.sum(Lms[h], axis=1, keepdims=True) for h in heads)
            c_p = tuple(c_p[h] + jnp.sum(Ps[h], axis=1, keepdims=True) for h in heads)
            return _onto_rows(carry, (dq, c_fail, c_p), lo)

        zc = tuple(jnp.zeros((qb, 1), F32) for _ in masks)
        carry = _key_walk(i, qb, block, (jnp.zeros((qb, LANES), F32), zc, zc), lambda c: c[1])
        dq_ref[...] = (carry[0] * scale).astype(dq_ref.dtype)

        @pl.when(i == nq - 1)
        def _():
            dk_out[...] = dk_ref[...].astype(dk_out.dtype)
            dv_out[...] = dv_ref[...].astype(dv_out.dtype)

    q_spec, k_spec, v_spec, blk, seq = _attn_specs(S, qb)
    return pl.pallas_call(
        body, name="sb_attn_bwd",
        grid=(B, SB_WIDTH // LANES, nq),
        in_specs=[q_spec, k_spec, v_spec, blk, blk],
        out_specs=[blk, seq, seq],
        out_shape=[jax.ShapeDtypeStruct((B * S, SB_WIDTH), BF16)] * 3,
        scratch_shapes=[pltpu.VMEM((S, LANES), F32), pltpu.VMEM((S, LANES), F32)],
        compiler_params=_params(("parallel", "parallel", "arbitrary")),
    )(proj, proj, proj, o, do)


_BATCHED = {"nn": "gmk,gkn->gmn", "nt": "gmk,gnk->gmn", "tn": "gkm,gkn->gmn"}


def _bdot_raw(a, b, kind, passes):
    e = functools.partial(jnp.einsum, _BATCHED[kind], preferred_element_type=F32)
    ah, bh = a.astype(BF16), b.astype(BF16)
    if passes == 1:
        return e(ah, bh)
    al, bl = (a - ah.astype(F32)).astype(BF16), (b - bh.astype(F32)).astype(BF16)
    return e(ah, bh) + e(ah, bl) + e(al, bh)


def _cumsum_rows(x, kind):
    G, C, _ = x.shape
    row = lax.broadcasted_iota(jnp.int32, (C, C), 0)
    col = lax.broadcasted_iota(jnp.int32, (C, C), 1)
    tri = jnp.broadcast_to((col <= row).astype(BF16), (G, C, C))
    e = functools.partial(jnp.einsum, _BATCHED[kind], preferred_element_type=F32)
    hi = x.astype(BF16)
    return e(tri, hi) + e(tri, (x - hi.astype(F32)).astype(BF16))


@jax.custom_vjp
def _running_sum(x):
    return _cumsum_rows(x, "nn")


_running_sum.defvjp(lambda x: (_cumsum_rows(x, "nn"), None), lambda _, g: (_cumsum_rows(g, "tn"),))


@functools.partial(jax.custom_vjp, nondiff_argnums=(2, 3))
def _bdot(a, b, kind, passes):
    return _bdot_raw(a, b, kind, passes)


def _bdot_fwd(a, b, kind, passes):
    return _bdot_raw(a, b, kind, passes), (a, b)


def _bdot_bwd(kind, passes, res, g):
    a, b = res
    if kind == "nn":
        return _bdot_raw(g, b, "nt", passes), _bdot_raw(a, g, "tn", passes)
    if kind == "nt":
        return _bdot_raw(g, b, "nn", passes), _bdot_raw(g, a, "tn", passes)
    return _bdot_raw(b, g, "nt", passes), _bdot_raw(a, g, "nn", passes)


_bdot.defvjp(_bdot_fwd, _bdot_bwd)


def _solve_powers(m):
    powers = [m]
    for _ in range(max(1, (m.shape[1] - 1).bit_length()) - 1):
        powers.append(_bdot_raw(powers[-1], powers[-1], "nn", 1))
    return powers


def _solve_fwd(m, rhs):
    powers = _solve_powers(m)
    x = rhs
    for p in powers:
        x = x + _bdot_raw(p, x, "nn", 1)
    return x, (powers, x)


def _solve_bwd(res, g):
    powers, x = res
    for p in powers:
        g = g + _bdot_raw(p, g, "tn", 1)
    return _bdot_raw(g, x, "nt", 1), g


@jax.custom_vjp
def _unit_lower_solve(m, rhs):
    return _solve_fwd(m, rhs)[0]


_unit_lower_solve.defvjp(_solve_fwd, _solve_bwd)


def _wkv_chunk(S0, r, lw, k, v, kap, a):
    G, C, N = r.shape
    row = lax.broadcasted_iota(jnp.int32, (C, C), 0)
    col = lax.broadcasted_iota(jnp.int32, (C, C), 1)
    incl = (col <= row).astype(F32)
    strict = (col < row).astype(F32)
    cum = _running_sum(lw)
    e_pos = jnp.exp(cum)
    e_neg = jnp.exp(-cum)
    al = -kap * jnp.exp(cum - lw)
    be = kap * a * e_neg
    kt = k * e_neg
    rt = r * e_pos
    bk = jnp.concatenate([be, kt], axis=1)
    mask = jnp.concatenate([jnp.concatenate([strict, strict], axis=1), jnp.concatenate([incl, incl], axis=1)], axis=0)
    m_all = _bdot(jnp.concatenate([al, rt], axis=1), bk, "nt", 3) * mask
    m_ab, m_ak = m_all[:, :C, :C], m_all[:, :C, C:]
    m_rb, m_rk = m_all[:, C:, :C], m_all[:, C:, C:]
    S0t = jnp.swapaxes(S0, 1, 2)
    sa = _unit_lower_solve(m_ab, _bdot(jnp.concatenate([al, m_ak], axis=2), jnp.concatenate([S0t, v], axis=1), "nn", 3))
    y =_bdot(jnp.concatenate([rt, m_rb, m_rk], axis=2), jnp.concatenate([S0t, sa, v], axis=1), "nn", 3)
    S1 = (S0 + _bdot(jnp.concatenate([sa, v], axis=1), bk, "tn", 1)) * e_pos[:, C - 1:C, :]
    return y, S1


def _split_heads(x):
    return jnp.stack([x[:, h * HEAD_DIM:(h + 1) * HEAD_DIM] for h in range(x.shape[1] // HEAD_DIM)], axis=0)


def _merge_heads(x):
    return jnp.concatenate([x[h] for h in range(x.shape[0])], axis=1)


def _seq_heads(ref):
    return jnp.concatenate([_split_heads(ref[s]) for s in range(ref.shape[0])], axis=0)


def _store_seq_heads(ref, x):
    heads = x.shape[0] // ref.shape[0]
    for s in range(ref.shape[0]):
        ref[s] = _merge_heads(x[s * heads:(s + 1) * heads])


def _wkv_fwd(r, lw, k, v, kap, a, B, S):
    C, H, N = WKV_CHUNK, RW_WIDTH // HEAD_DIM, HEAD_DIM
    nc = S // C
    Q = min(WKV_SEQS, B)

    def body(r_ref, lw_ref, k_ref, v_ref, kap_ref, a_ref, y_ref, st_ref, s_scr):
        @pl.when(pl.program_id(1) == 0)
        def _():
            s_scr[...] = jnp.zeros_like(s_scr)

        S0 = s_scr[...]
        for s in range(Q):
            st_ref[s, 0] = S0[s * H:(s + 1) * H]
        args = [_seq_heads(ref) for ref in (r_ref, lw_ref, k_ref, v_ref, kap_ref, a_ref)]
        y, S1 = _wkv_chunk(S0, *args)
        s_scr[...] = S1
        _store_seq_heads(y_ref, y)

    row_spec = pl.BlockSpec((Q, C, RW_WIDTH), lambda b, c: (b, c, 0))
    seqs = lambda t: t.reshape(B, S, RW_WIDTH)
    y, states = pl.pallas_call(
        body, name="wkv_fwd",
        grid=(B // Q, nc),
        in_specs=[row_spec] * 6,
        out_specs=[row_spec, pl.BlockSpec((Q, 1, H, N, N), lambda b, c: (b, c, 0, 0, 0))],
        out_shape=[jax.ShapeDtypeStruct((B, S, RW_WIDTH), F32), jax.ShapeDtypeStruct((B, nc, H, N, N), F32)],
        scratch_shapes=[pltpu.VMEM((Q * H, N, N), F32)],
        compiler_params=_params(("arbitrary", "arbitrary")),
    )(*map(seqs, (r, lw, k, v, kap, a)))
    return y.reshape(B * S, RW_WIDTH), states


def _wkv_bwd(r, lw, k, v, kap, a, states, dy, B, S):
    C, H, N = WKV_CHUNK, RW_WIDTH // HEAD_DIM, HEAD_DIM
    nc = S // C
    Q = min(WKV_SEQS, B)

    def body(r_ref, lw_ref, k_ref, v_ref, kap_ref, a_ref, st_ref, dy_ref,
             dr_ref, dlw_ref, dk_ref, dv_ref, dkap_ref, da_ref, ds_scr):
        @pl.when(pl.program_id(1) == 0)
        def _():
            ds_scr[...] = jnp.zeros_like(ds_scr)

        args = [_seq_heads(ref) for ref in (r_ref, lw_ref, k_ref, v_ref, kap_ref, a_ref)]
        S0 = jnp.concatenate([st_ref[s, 0] for s in range(Q)], axis=0)
        _, vjp = jax.vjp(_wkv_chunk, S0, *args)
        g = vjp((_seq_heads(dy_ref), ds_scr[...]))
        ds_scr[...] = g[0]
        for ref, gv in zip((dr_ref, dlw_ref, dk_ref, dv_ref, dkap_ref, da_ref), g[1:]):
            _store_seq_heads(ref, gv)

    row_spec = pl.BlockSpec((Q, C, RW_WIDTH), lambda b, c: (b, nc - 1 - c, 0))
    st_spec = pl.BlockSpec((Q, 1, H, N, N), lambda b, c: (b, nc - 1 - c, 0, 0, 0))
    seqs = lambda t: t.reshape(B, S, RW_WIDTH)
    res = pl.pallas_call(
        body, name="wkv_bwd",
        grid=(B // Q, nc),
        in_specs=[row_spec] * 6 + [st_spec, row_spec],
        out_specs=[row_spec] * 6,
        out_shape=[jax.ShapeDtypeStruct((B, S, RW_WIDTH), F32)] * 6,
        scratch_shapes=[pltpu.VMEM((Q * H, N, N), F32)],
        compiler_params=_params(("arbitrary", "arbitrary"), vmem=WKV_BWD_VMEM),
    )(*map(seqs, (r, lw, k, v, kap, a)), states, seqs(dy))
    return [t.reshape(B * S, RW_WIDTH) for t in res]


HBM = pl.BlockSpec(memory_space=pl.ANY)


def _place():
    return lax.axis_index("x"), lax.axis_index("y"), lax.axis_index("c")


def _other_chips(x, y):
    return [(1 - x, y), (x, 1 - y), (1 - x, 1 - y)]


def _all_gather_chips(shards):
    n = len(shards)

    def body(*refs):
        ins, outs = refs[:n], refs[n:2 * n]
        ici_send, ici_recv, d2d_send, d2d_recv, local = refs[2 * n:]
        x, y, c = _place()
        me = 2 * x + y
        sib = (x, y, 1 - c)
        chips = _other_chips(x, y)
        started, copies = [], []
        for w in range(n):
            cp = pltpu.make_async_copy(ins[w].at[c], outs[w].at[me, c], local.at[w])
            cp.start()
            copies.append(cp)
            for j, (px, py) in enumerate(chips):
                rd = pltpu.make_async_remote_copy(
                    src_ref=ins[w].at[c], dst_ref=outs[w].at[me, c], send_sem=ici_send.at[3 * w + j],
                    recv_sem=ici_recv.at[3 * w + j], device_id=(px, py, c), device_id_type=MESH)
                rd.start()
                started.append(rd)
            rd = pltpu.make_async_remote_copy(
                src_ref=ins[w].at[c], dst_ref=outs[w].at[me, c], send_sem=d2d_send.at[4 * w + 3],
                recv_sem=d2d_recv.at[4 * w + 3], device_id=sib, device_id_type=MESH)
            rd.start()
            started.append(rd)
        for w in range(n):
            for j, (px, py) in enumerate(chips):
                src = 2 * px + py
                pltpu.make_async_remote_copy(
                    src_ref=ins[w].at[c], dst_ref=outs[w].at[src, c], send_sem=ici_send.at[3 * w + j],
                    recv_sem=ici_recv.at[3 * w + j], device_id=(px, py, c), device_id_type=MESH).wait_recv()
                rd = pltpu.make_async_remote_copy(
                    src_ref=outs[w].at[src, c], dst_ref=outs[w].at[src, c], send_sem=d2d_send.at[4 * w + j],
                    recv_sem=d2d_recv.at[4 * w + j], device_id=sib, device_id_type=MESH)
                rd.start()
                started.append(rd)
        for w in range(n):
            for j, (px, py) in enumerate(chips):
                pltpu.make_async_remote_copy(
                    src_ref=ins[w].at[c], dst_ref=outs[w].at[2 * px + py, 1 - c], send_sem=d2d_send.at[4 * w + j],
                    recv_sem=d2d_recv.at[4 * w + j], device_id=sib, device_id_type=MESH).wait_recv()
            pltpu.make_async_remote_copy(
                src_ref=ins[w].at[c], dst_ref=outs[w].at[me, 1 - c], send_sem=d2d_send.at[4 * w + 3],
                recv_sem=d2d_recv.at[4 * w + 3], device_id=sib, device_id_type=MESH).wait_recv()
        for rd in started:
            rd.wait_send()
        for cp in copies:
            cp.wait()

    return pl.pallas_call(
        body, name="gather_weights",
        in_specs=[HBM] * n, out_specs=[HBM] * n,
        out_shape=[jax.ShapeDtypeStruct((N_CHIPS,) + s.shape, s.dtype) for s in shards],
        scratch_shapes=[pltpu.SemaphoreType.DMA((3 * n,)), pltpu.SemaphoreType.DMA((3 * n,)),
                        pltpu.SemaphoreType.DMA((4 * n,)), pltpu.SemaphoreType.DMA((4 * n,)),
                        pltpu.SemaphoreType.DMA((n,))],
        compiler_params=pltpu.CompilerParams(has_side_effects=True),
    )(*shards)


def _pair_split(grads):
    n = len(grads)

    def body(*refs):
        ins, theirs = refs[:n], refs[n:2 * n]
        send, recv = refs[2 * n:]
        x, y, c = _place()
        sib = (x, y, 1 - c)
        rds = []
        for w in range(n):
            rd = pltpu.make_async_remote_copy(
                src_ref=ins[w].at[:, 1 - c], dst_ref=theirs[w], send_sem=send.at[w], recv_sem=recv.at[w],
                device_id=sib, device_id_type=MESH)
            rd.start()
            rds.append(rd)
        for rd in rds:
            rd.wait_recv()
        for rd in rds:
            rd.wait_send()

    return pl.pallas_call(
        body, name="grad_pair_split",
        in_specs=[HBM] * n, out_specs=[HBM] * n,
        out_shape=[jax.ShapeDtypeStruct((g.shape[0],) + g.shape[2:], g.dtype) for g in grads],
        scratch_shapes=[pltpu.SemaphoreType.DMA((n,)), pltpu.SemaphoreType.DMA((n,))],
        compiler_params=pltpu.CompilerParams(has_side_effects=True),
    )(*grads)


def _chip_scatter(parts):
    n = len(parts)

    def body(*refs):
        ins, outs = refs[:n], refs[n:2 * n]
        send, recv = refs[2 * n:]
        x, y, c = _place()
        me = 2 * x + y
        rds = []
        for w in range(n):
            for j, (px, py) in enumerate(_other_chips(x, y)):
                s = 3 * w + j
                rd = pltpu.make_async_remote_copy(
                    src_ref=ins[w].at[2 * px + py], dst_ref=outs[w].at[j], send_sem=send.at[s], recv_sem=recv.at[s],
                    device_id=(px, py, c), device_id_type=MESH)
                rd.start()
                rds.append(rd)
        for w in range(n):
            for j, (px, py) in enumerate(_other_chips(x, y)):
                s = 3 * w + j
                pltpu.make_async_remote_copy(
                    src_ref=ins[w].at[me], dst_ref=outs[w].at[j], send_sem=send.at[s], recv_sem=recv.at[s],
                    device_id=(px, py, c), device_id_type=MESH).wait_recv()
        for rd in rds:
            rd.wait_send()

    return pl.pallas_call(
        body, name="grad_chip_scatter",
        in_specs=[HBM] * n, out_specs=[HBM] * n,
        out_shape=[jax.ShapeDtypeStruct((N_CHIPS - 1,) + p.shape[1:], p.dtype) for p in parts],
        scratch_shapes=[pltpu.SemaphoreType.DMA((3 * n,)), pltpu.SemaphoreType.DMA((3 * n,))],
        compiler_params=pltpu.CompilerParams(has_side_effects=True),
    )(*parts)


def _pair_join(bufs):
    n = len(bufs)

    def body(*refs):
        ins, outs = refs[:n], refs[n:2 * n]
        send, recv = refs[2 * n:]
        x, y, c = _place()
        sib = (x, y, 1 - c)
        rds = []
        for w in range(n):
            rd = pltpu.make_async_remote_copy(
                src_ref=ins[w].at[c], dst_ref=outs[w].at[c], send_sem=send.at[w], recv_sem=recv.at[w],
                device_id=sib, device_id_type=MESH)
            rd.start()
            rds.append(rd)
        for w in range(n):
            pltpu.make_async_remote_copy(
                src_ref=ins[w].at[c], dst_ref=outs[w].at[1 - c], send_sem=send.at[w], recv_sem=recv.at[w],
                device_id=sib, device_id_type=MESH).wait_recv()
        for rd in rds:
            rd.wait_send()

    return pl.pallas_call(
        body, name="grad_pair_join",
        in_specs=[HBM] * n, out_specs=[HBM] * n,
        out_shape=[jax.ShapeDtypeStruct(b.shape, b.dtype) for b in bufs],
        input_output_aliases={w: w for w in range(n)},
        scratch_shapes=[pltpu.SemaphoreType.DMA((n,)), pltpu.SemaphoreType.DMA((n,))],
        compiler_params=pltpu.CompilerParams(has_side_effects=True),
    )(*bufs)


def _all_reduce_small(packed):
    R = packed.shape[0]

    def body(x_ref, o_ref, buf, send, recv):
        x, y, c = _place()
        me = 4 * x + 2 * y + c
        buf[me] = x_ref[...]
        rds = []
        for rel in range(1, N_DEV):
            fx, fy, fc = (rel >> 2) & 1, (rel >> 1) & 1, rel & 1
            peer = (1 - x if fx else x, 1 - y if fy else y, 1 - c if fc else c)
            rd = pltpu.make_async_remote_copy(
                src_ref=x_ref, dst_ref=buf.at[me], send_sem=send.at[rel - 1], recv_sem=recv.at[rel - 1],
                device_id=peer, device_id_type=MESH)
            rd.start()
            rds.append((rd, peer))
        for rel in range(1, N_DEV):
            rd, (px, py, pc) = rds[rel - 1]
            pltpu.make_async_remote_copy(
                src_ref=x_ref, dst_ref=buf.at[4 * px + 2 * py + pc], send_sem=send.at[rel - 1], recv_sem=recv.at[rel - 1],
                device_id=(px, py, pc), device_id_type=MESH).wait_recv()
        for rd, _ in rds:
            rd.wait_send()
        total = buf[0]
        for d in range(1, N_DEV):
            total = total + buf[d]
        o_ref[...] = total

    return pl.pallas_call(
        body, name="all_reduce_small",
        in_specs=[pl.BlockSpec(memory_space=pltpu.VMEM)],
        out_specs=pl.BlockSpec(memory_space=pltpu.VMEM),
        out_shape=jax.ShapeDtypeStruct(packed.shape, F32),
        scratch_shapes=[pltpu.VMEM((N_DEV, R, LANES), F32), pltpu.SemaphoreType.DMA((N_DEV - 1,)),
                        pltpu.SemaphoreType.DMA((N_DEV - 1,))],
        compiler_params=pltpu.CompilerParams(has_side_effects=True),
    )(packed)


def _pair_sum(name, split, theirs, core):
    n_chip, _, Rh, C = split.shape
    tile = _div_tile(Rh, 256, 2 * SUBLANES)
    nt = Rh // tile

    def body(core_ref, a_ref, b_ref, o_ref):
        o_ref[...] = (a_ref[...] + b_ref[...]).astype(o_ref.dtype)

    return pl.pallas_call(
        body, name=name,
        grid_spec=pltpu.PrefetchScalarGridSpec(
            num_scalar_prefetch=1,
            grid=(n_chip, nt),
            in_specs=[pl.BlockSpec((None, None, tile, C), lambda j, i, core_ref: (j, core_ref[0], i, 0)),
                      pl.BlockSpec((None, tile, C), lambda j, i, core_ref: (j, i, 0))],
            out_specs=pl.BlockSpec((None, tile, C), lambda j, i, core_ref: (j, i, 0)),
        ),
        out_shape=jax.ShapeDtypeStruct((n_chip, Rh, C), BF16),
        compiler_params=_params(("parallel", "parallel")),
    )(core, split, theirs)


def _chip_sum(name, own, landed, core):
    n_in, Rh, C = landed.shape
    tile = _div_tile(Rh, 256, 2 * SUBLANES)

    def body(core_ref, *refs):
        total = refs[0][...].astype(F32)
        for ref in refs[1:n_in + 1]:
            total = total + ref[...].astype(F32)
        refs[n_in + 1][...] = total

    slot = lambda j: pl.BlockSpec((None, tile, C), lambda i, core_ref: (j, i, 0))
    return pl.pallas_call(
        body, name=name,
        grid_spec=pltpu.PrefetchScalarGridSpec(
            num_scalar_prefetch=1,
            grid=(Rh // tile,),
            in_specs=[pl.BlockSpec((None, tile, C), lambda i, core_ref: (core_ref[1], i, 0))]
                     + [slot(j) for j in range(n_in)],
            out_specs=pl.BlockSpec((None, tile, C), lambda i, core_ref: (core_ref[0], i, 0)),
        ),
        out_shape=jax.ShapeDtypeStruct((2, Rh, C), F32),
        compiler_params=_params(("parallel",)),
    )(core, own, *([landed] * n_in))


def _adamw(name, w, g, m, v):
    R, C = w.shape
    tile = _div_tile(R, 256, SUBLANES)
    c1 = 1.0 / (1.0 - ADAM_B1 ** ADAM_STEP)
    c2 = 1.0 / (1.0 - ADAM_B2 ** ADAM_STEP)

    def body(w_ref, g_ref, m_ref, v_ref, d_ref, nm_ref, nv_ref):
        g_ = g_ref[...]
        nm = ADAM_B1 * m_ref[...] + (1.0 - ADAM_B1) * g_
        nv = ADAM_B2 * v_ref[...] + (1.0 - ADAM_B2) * (g_ * g_)
        d_ref[...] = -ADAM_LR * ((nm * c1) / (jnp.sqrt(nv * c2) + ADAM_EPS) + ADAM_WD * w_ref[...])
        nm_ref[...] = nm
        nv_ref[...] = nv

    spec = pl.BlockSpec((tile, C), lambda i: (i, 0))
    return pl.pallas_call(
        body, name=name,
        grid=(R // tile,),
        in_specs=[spec] * 4, out_specs=[spec] * 3,
        out_shape=[jax.ShapeDtypeStruct((R, C), F32)] * 3,
        compiler_params=_params(("parallel",)),
    )(w, g, m, v)


SMALL =["norm_mix_pre", "b_gate", "mu_rw", "w0", "a0", "k_k", "k_a", "r_k", "lnx_w", "lnx_b",
         "norm_mix_post", "norm_ffn_pre", "norm_ffn_post"]
BIG = ["w_in", "w_up", "a_up", "g_up", "w_sb_out", "w_rw_out", "w_o", "w_ffn_gate", "w_ffn_up", "w_ffn_down"]
ROW_SHARDED = ("w_o", "w_ffn_down")
ORDER = ["norm_mix_pre", "w_in", "b_gate", "mu_rw", "w0", "w_up", "a0", "a_up", "g_up", "k_k", "k_a", "r_k",
         "lnx_w", "lnx_b", "w_sb_out", "w_rw_out", "w_o", "norm_mix_post", "norm_ffn_pre", "w_ffn_gate",
         "w_ffn_up", "w_ffn_down", "norm_ffn_post"]


def _pack_small(vals, extra_rows=0):
    rows = jnp.concatenate([vals[n].reshape(-1, LANES) for n in SMALL], axis=0)
    pad = (-(rows.shape[0] + extra_rows)) % SUBLANES + extra_rows
    return jnp.pad(rows, ((0, pad), (0, 0)))


def _unpack_small(packed, shapes):
    out, r = {}, 0
    for n in SMALL:
        size = 1
        for s in shapes[n]:
            size *= s
        out[n] = packed[r:r + size // LANES].reshape(shapes[n])
        r += size // LANES
    return out


def kernel(x, norm_mix_pre, w_in, b_gate, mu_rw, w0, w_up, a0, a_up, g_up, k_k, k_a, r_k, lnx_w, lnx_b, w_sb_out, w_rw_out, w_o, norm_mix_post, norm_ffn_pre, w_ffn_gate, w_ffn_up, w_ffn_down, norm_ffn_post, loss_target, m_norm_mix_pre, m_w_in, m_b_gate, m_mu_rw, m_w0, m_w_up, m_a0, m_a_up, m_g_up, m_k_k, m_k_a, m_r_k, m_lnx_w, m_lnx_b, m_w_sb_out, m_w_rw_out, m_w_o, m_norm_mix_post, m_norm_ffn_pre, m_w_ffn_gate, m_w_ffn_up, m_w_ffn_down, m_norm_ffn_post, v_norm_mix_pre, v_w_in, v_b_gate, v_mu_rw, v_w0, v_w_up, v_a0, v_a_up, v_g_up, v_k_k, v_k_a, v_r_k, v_lnx_w, v_lnx_b, v_w_sb_out, v_w_rw_out, v_w_o, v_norm_mix_post, v_norm_ffn_pre, v_w_ffn_gate, v_w_ffn_up, v_w_ffn_down, v_norm_ffn_post):
    W = dict(norm_mix_pre=norm_mix_pre, w_in=w_in, b_gate=b_gate, mu_rw=mu_rw, w0=w0, w_up=w_up, a0=a0, a_up=a_up,
             g_up=g_up, k_k=k_k, k_a=k_a, r_k=r_k, lnx_w=lnx_w, lnx_b=lnx_b, w_sb_out=w_sb_out, w_rw_out=w_rw_out,
             w_o=w_o, norm_mix_post=norm_mix_post, norm_ffn_pre=norm_ffn_pre, w_ffn_gate=w_ffn_gate,
             w_ffn_up=w_ffn_up, w_ffn_down=w_ffn_down, norm_ffn_post=norm_ffn_post)
    Mo = dict(norm_mix_pre=m_norm_mix_pre, w_in=m_w_in, b_gate=m_b_gate, mu_rw=m_mu_rw, w0=m_w0, w_up=m_w_up, a0=m_a0,
              a_up=m_a_up, g_up=m_g_up, k_k=m_k_k, k_a=m_k_a, r_k=m_r_k, lnx_w=m_lnx_w, lnx_b=m_lnx_b,
              w_sb_out=m_w_sb_out, w_rw_out=m_w_rw_out, w_o=m_w_o, norm_mix_post=m_norm_mix_post,
              norm_ffn_pre=m_norm_ffn_pre, w_ffn_gate=m_w_ffn_gate, w_ffn_up=m_w_ffn_up, w_ffn_down=m_w_ffn_down,
              norm_ffn_post=m_norm_ffn_post)
    Vo = dict(norm_mix_pre=v_norm_mix_pre, w_in=v_w_in, b_gate=v_b_gate, mu_rw=v_mu_rw, w0=v_w0, w_up=v_w_up, a0=v_a0,
              a_up=v_a_up, g_up=v_g_up, k_k=v_k_k, k_a=v_k_a, r_k=v_r_k, lnx_w=v_lnx_w, lnx_b=v_lnx_b,
              w_sb_out=v_w_sb_out, w_rw_out=v_w_rw_out, w_o=v_w_o, norm_mix_post=v_norm_mix_post,
              norm_ffn_pre=v_norm_ffn_pre, w_ffn_gate=v_w_ffn_gate, w_ffn_up=v_w_ffn_up, w_ffn_down=v_w_ffn_down,
              norm_ffn_post=v_norm_ffn_post)
    shapes = {n: W[n].shape for n in ORDER}
    B, S, D = x.shape
    T = B * S
    x2 = x.reshape(T, D)
    tgt = loss_target.reshape(T, D)
    vec = {n: W[n].reshape(1, -1) for n in SMALL}

    work = lambda t, n: t[0] if n in ROW_SHARDED else jnp.swapaxes(t[0], 0, 1)
    halved = [work(W[n], n).astype(BF16) for n in BIG]
    halved = [h.reshape(2, h.shape[0] // 2, h.shape[1]) for h in halved]
    full = {n: gth.reshape(-1, gth.shape[3]) for n, gth in zip(BIG, _all_gather_chips(halved))}
    w_in_t = full["w_in"]
    w_sb_t, w_rw_t, w_gt_t = w_in_t[:SB_COLS], w_in_t[SB_COLS:SB_COLS + RW_COLS], w_in_t[SB_COLS + RW_COLS:]
    lora_rows = {"w_up": 0, "a_up": 64, "g_up": 128}
    lora = {n: jnp.pad(full[n].T, ((r0, LORA_COLS - r0 - full[n].shape[1]), (0, 0))) for n, r0 in lora_rows.items()}
    mu = vec["mu_rw"]
    mu_parts = [mu[:, :512], mu[:, 512:1024], mu[:, 1024:1536], mu[:, 1536:]]
    b1, b2 = vec["b_gate"][:, :D], vec["b_gate"][:, D:]

    (h1,) = _rowwise("norm_mix_pre", _f_norm, [(x2, D, 0)], [vec["norm_mix_pre"]], [(D, BF16)], tile=512)
    p_sb = _mm("proj_sb", h1, w_sb_t, tb=True, out_dtype=BF16)
    p_rw = _mm("proj_rw", h1, w_rw_t, tb=True)
    p_gt = _mm("proj_gate", h1, w_gt_t, tb=True, out_dtype=BF16)
    o_sb = _attn_fwd(p_sb, B, S)
    pre_params = mu_parts + [vec["w0"], lora["w_up"], vec["a0"], lora["a_up"], lora["g_up"], vec["k_k"], vec["k_a"]]
    r_, lw_, k2_, v_, kap_, a_, g_ = _rw_pre(p_rw, pre_params, S)
    y_wkv, states = _wkv_fwd(r_, lw_, k2_, v_, kap_, a_, B, S)
    post_rows = [(y_wkv, 512, 0), (r_, 512, 0), (k2_, 512, 0), (v_, 512, 0), (g_, 512, 0)]
    post_params = [vec["lnx_w"], vec["lnx_b"], vec["r_k"]]
    (o_rw,) = _rowwise("rw_post", _f_rwpost, post_rows, post_params, [(512, BF16)])
    m1 = _mm("mix_sb_out", o_sb, full["w_sb_out"], tb=True, out_dtype=BF16)
    m2 = _mm("mix_rw_out", o_rw, full["w_rw_out"], tb=True, out_dtype=BF16)
    merge_rows = [(p_gt, D, 0), (p_gt, D, 1), (m1, D, 0), (m2, D, 0)]
    (merged,) = _rowwise("merge", _f_merge, merge_rows, [b1, b2], [(D, BF16)])
    u = _mm("mix_out", merged, full["w_o"])
    post1_params = [vec["norm_mix_post"], vec["norm_ffn_pre"]]
    x1, h2 = _rowwise("post_mix", _f_post1, [(x2, D, 0), (u, D, 0)], post1_params, [(D, F32), (D, BF16)], tile=512)
    ag, au, sw = _mm_fused("ffn_in", [h2, h2], [full["w_ffn_gate"], full["w_ffn_up"]], [BF16] * 3, tb=True,
                           epilogue=lambda gu, _: (gu[0], gu[1], _f_swiglu(*gu)[0]))
    f = _mm("ffn_down", sw, full["w_ffn_down"])
    loss_part, dx1, df, dg4 = _loss_head(x1, f, tgt, vec["norm_ffn_post"])

    gbig, gsmall = {}, {"norm_ffn_post": dg4}
    gbig["w_ffn_down"] = _mm("g_ffn_down", sw, df, ta=True)

    def swiglu_back(dsw, gu):
        return jax.vjp(_f_swiglu, *gu)[1]((dsw[0],))

    dag, dau = _mm_fused("ffn_back", [df], [full["w_ffn_down"]], [BF16] * 2, tb=True, extras=[ag, au],
                         epilogue=swiglu_back)
    (dh2,) = _mm_fused("d_h2", [dag, dau], [full["w_ffn_gate"], full["w_ffn_up"]], [F32], add=True)
    gbig["w_ffn_gate"] = _mm("g_ffn_gate", dag, h2, ta=True)
    gbig["w_ffn_up"] = _mm("g_ffn_up", dau, h2, ta=True)
    (dx_res, du), (dg2, dg3) = _rowwise_vjp("post_mix_bwd", _f_post1, [(x2, D, 0), (u, D, 0)], post1_params,
                                            [[dx1], [dh2]], [True, True], [True, True], bf16_rows=(1,))
    gsmall["norm_mix_post"], gsmall["norm_ffn_pre"] = dg2, dg3
    dmerged = _mm("d_merged", du, full["w_o"], tb=True, out_dtype=BF16)
    gbig["w_o"] = _mm("g_w_o", merged, du, ta=True)
    (dpg1, dpg2, dm1, dm2), (db1, db2) = _rowwise_vjp("merge_bwd", _f_merge, merge_rows, [b1, b2], [[dmerged]],
                                                      [True] * 4, [True, True], bf16_rows=(0, 1, 2, 3))
    gsmall["b_gate"] = jnp.concatenate([db1, db2], axis=1)
    do_sb = _mm("d_o_sb", dm1, full["w_sb_out"])
    do_rw = _mm("d_o_rw", dm2, full["w_rw_out"])
    gbig["w_sb_out"] = _mm("g_sb_out", dm1, o_sb, ta=True)
    gbig["w_rw_out"] = _mm("g_rw_out", dm2, o_rw, ta=True)
    (dy_wkv, dr_a, dk2_a, dv_a, dg_), (dlnx_w, dlnx_b, dr_k) = _rowwise_vjp(
        "rw_post_bwd", _f_rwpost, post_rows, post_params, [[do_rw]], [True] * 5, [True] * 3)
    gsmall["lnx_w"], gsmall["lnx_b"], gsmall["r_k"] = dlnx_w, dlnx_b, dr_k
    dr_b, dlw, dk2_b, dv_b, dkap, da = _wkv_bwd(r_, lw_, k2_, v_, kap_, a_, states, dy_wkv, B, S)
    pre_cts = [[dr_a, dr_b], [dlw], [dk2_a, dk2_b], [dv_a, dv_b], [dkap], [da], [dg_]]
    dp_rw, dpre_params = _rw_pre_bwd(p_rw, pre_params, pre_cts, S)
    gsmall["mu_rw"] = jnp.concatenate(dpre_params[:4], axis=1)
    gsmall["w0"], gsmall["a0"], gsmall["k_k"], gsmall["k_a"] = dpre_params[4], dpre_params[6], dpre_params[9], dpre_params[10]
    glora = {"w_up": dpre_params[5][0:64].T, "a_up": dpre_params[7][64:128].T, "g_up": dpre_params[8][128:256].T}
    dq, dk, dv = _attn_bwd(p_sb, o_sb, do_sb, B, S)
    (dh1,) = _mm_fused("d_h1_sb", [dq, dk, dv], [w_sb_t[:512], w_sb_t[512:1024], w_sb_t[1024:]], [F32], add=True)
    dh1 = _mm("d_h1_rw", dp_rw, w_rw_t, acc=dh1)
    (dh1,) = _mm_fused("d_h1_gate", [dpg1, dpg2], [w_gt_t[:D], w_gt_t[D:]], [F32], add=True,
                       extras=[dh1], epilogue=lambda p, e: (p[0] + e[0],))
    gbig["w_in"] = jnp.concatenate(
        [_mm("g_in_" + tag, d, h1, ta=True)
         for tag, d in (("q", dq), ("k", dk), ("v", dv), ("rw", dp_rw), ("g1", dpg1), ("g2", dpg2))], axis=0)
    (grad_x2,), (dg1,) = _rowwise_vjp("norm_mix_pre_bwd", _f_norm, [(x2, D, 0)], [vec["norm_mix_pre"]], [[dh1]],
                                      [True], [True], add_to={0: dx_res})
    gsmall["norm_mix_pre"] = dg1
    gbig.update(glora)

    split = [gbig[n].reshape(N_CHIPS, 2, gbig[n].shape[0] // (2 * N_CHIPS), gbig[n].shape[1]) for n in BIG]
    core = jnp.stack([lax.axis_index("c"), 2 * lax.axis_index("x") + lax.axis_index("y")]).astype(jnp.int32)
    theirs = _pair_split(split)
    chip_sums = [_pair_sum("pair_sum_" + n, a, b, core) for n, a, b in zip(BIG, split, theirs)]
    landed = _chip_scatter(chip_sums)
    joined = _pair_join([_chip_sum("chip_sum_" + n, own, got, core) for n, own, got in zip(BIG, chip_sums, landed)])
    grads = {n: j.reshape(-1, j.shape[2]) for n, j in zip(BIG, joined)}

    small_local = _pack_small({n: gsmall[n] for n in SMALL}, extra_rows=1)
    loss_row = small_local.shape[0] - 1
    small_local = small_local.at[loss_row].set(loss_part[0])
    small_sum = _all_reduce_small(small_local)
    loss = small_sum[loss_row, 0]

    delta, new_m, new_v = {}, {}, {}
    unwork = lambda t, n: (t if n in ROW_SHARDED else jnp.swapaxes(t, 0, 1))[None]
    for n in BIG:
        d_, m_, v2_ = _adamw("adamw_" + n, work(W[n], n), grads[n], work(Mo[n], n), work(Vo[n], n))
        delta[n], new_m[n], new_v[n], grads[n] = (unwork(t, n) for t in (d_, m_, v2_, grads[n]))
    pk = lambda src: _pack_small({n: src[n] for n in SMALL}, extra_rows=1)
    d_s, m_s, v_s = _adamw("adamw_small", pk(W), small_sum.at[loss_row].set(0.0), pk(Mo), pk(Vo))
    for dst, packed in ((grads, small_sum), (delta, d_s), (new_m, m_s), (new_v, v_s)):
        dst.update(_unpack_small(packed, shapes))

    return (loss, grad_x2.reshape(B, S, D), *[grads[n] for n in ORDER], *[delta[n] for n in ORDER],
            *[new_m[n] for n in ORDER], *[new_v[n] for n in ORDER])
```

```python
import functools

import jax
import jax.numpy as jnp
from jax import lax
from jax.experimental import pallas as pl
from jax.experimental.pallas import tpu as pltpu

F32 = jnp.float32
BF16 = jnp.bfloat16
MESH = pl.DeviceIdType.MESH

D_MODEL = 1024
SB_HEADS = 8
HEAD_DIM = 64
SB_WIDTH = SB_HEADS * HEAD_DIM
RW_WIDTH = 512
LORA_COLS = 256
SB_COLS = 3 * SB_WIDTH
RW_COLS = 3 * RW_WIDTH + LORA_COLS
GATE_COLS = 2 * D_MODEL
D_FF = 2816
RMS_EPS = 1e-6
GN_EPS = HEAD_DIM * 1e-5
WKV_CHUNK = 64
WKV_SEQS = 4
ATTN_QUERIES = 512
ATTN_KEYS = 128
ATTN_DEAD = -120.0
LANES = 128
SUBLANES = 8
N_CHIPS = 4
N_DEV = 8

ADAM_LR = 0.001
ADAM_B1 = 0.9
ADAM_B2 = 0.999
ADAM_EPS = 1e-08
ADAM_WD = 0.01
ADAM_STEP = 10

VMEM_LIMIT = 48 * 1024 * 1024
WKV_BWD_VMEM = 58 * 1024 * 1024


def _params(sem=None, vmem=VMEM_LIMIT, **kw):
    if sem is not None:
        kw["dimension_semantics"] = sem
    return pltpu.CompilerParams(vmem_limit_bytes=vmem, **kw)


def _div_tile(dim, pref, mult=LANES):
    if dim <= pref:
        return dim
    t = pref - pref % mult
    while t >= mult:
        if dim % t == 0:
            return t
        t -= mult
    return dim


def _dot(a, b, dims):
    return lax.dot_general(a, b, (dims, ((), ())), preferred_element_type=F32)


def _mm(name, a, b, *, ta=False, tb=False, acc=None, out_dtype=F32):
    if ta:
        K, M = a.shape
    else:
        M, K = a.shape
    N = b.shape[0] if tb else b.shape[1]
    if ta:
        tm, tn, tk = _div_tile(M, 1408), _div_tile(N, 1408), _div_tile(K, 512)
    else:
        tm, tn, tk = _div_tile(M, 512), _div_tile(N, 1408), _div_tile(K, 1408)
    nk = K // tk
    dims = ((0,) if ta else (1,), (1,) if tb else (0,))
    has_acc = acc is not None

    def body(*refs):
        a_ref, b_ref = refs[0], refs[1]
        part = _dot(a_ref[...].astype(BF16), b_ref[...].astype(BF16), dims)
        if nk == 1:
            o_ref = refs[-1]
            o_ref[...] = (part + refs[2][...] if has_acc else part).astype(o_ref.dtype)
            return
        o_ref, scr = refs[-2], refs[-1]
        k = pl.program_id(2)

        @pl.when(k == 0)
        def _():
            scr[...] = part + refs[2][...] if has_acc else part

        @pl.when(k > 0)
        def _():
            scr[...] += part

        @pl.when(k == nk - 1)
        def _():
            o_ref[...] = scr[...].astype(o_ref.dtype)

    a_spec = pl.BlockSpec((tk, tm), lambda i, j, k: (k, i)) if ta else pl.BlockSpec((tm, tk), lambda i, j, k: (i, k))
    b_spec = pl.BlockSpec((tn, tk), lambda i, j, k: (j, k)) if tb else pl.BlockSpec((tk, tn), lambda i, j, k: (k, j))
    o_spec = pl.BlockSpec((tm, tn), lambda i, j, k: (i, j))
    return pl.pallas_call(
        body, name=name,
        grid=(M // tm, N // tn, nk),
        in_specs=[a_spec, b_spec] + ([o_spec] if has_acc else []),
        out_specs=o_spec,
        out_shape=jax.ShapeDtypeStruct((M, N), out_dtype),
        scratch_shapes=[pltpu.VMEM((tm, tn), F32)] if nk > 1 else [],
        compiler_params=_params(("parallel", "parallel", "arbitrary")),
    )(*([a, b] + ([acc] if has_acc else [])))


def _mm_fused(name, lhs, rhs, outs, *, tb=False, add=False, extras=(), epilogue=None):
    M, K = lhs[0].shape
    N = rhs[0].shape[0] if tb else rhs[0].shape[1]
    tm, tn, tk = _div_tile(M, 512), _div_tile(N, 1408), _div_tile(K, 1408)
    nk = K // tk
    n_l, n_e, n_o = len(lhs), len(extras), len(outs)
    n_acc = 1 if add else n_l
    dims = ((1,), (1,) if tb else (0,))

    def body(*refs):
        l_refs, r_refs = refs[:n_l], refs[n_l:2 * n_l]
        e_refs = refs[2 * n_l:2 * n_l + n_e]
        o_refs = refs[2 * n_l + n_e:2 * n_l + n_e + n_o]
        scr = refs[2 * n_l + n_e + n_o:]
        parts = [_dot(l[...].astype(BF16), r[...].astype(BF16), dims) for l, r in zip(l_refs, r_refs)]
        if add:
            parts = [functools.reduce(lambda u, v: u + v, parts)]

        def finish(vals):
            res = epilogue(vals, [e[...].astype(F32) for e in e_refs]) if epilogue else vals
            for ref, val in zip(o_refs, res):
                ref[...] = val.astype(ref.dtype)

        if nk == 1:
            finish(parts)
            return
        k = pl.program_id(2)

        @pl.when(k == 0)
        def _():
            for s, part in zip(scr, parts):
                s[...] = part

        @pl.when(k > 0)
        def _():
            for s, part in zip(scr, parts):
                s[...] += part

        @pl.when(k == nk - 1)
        def _():
            finish([s[...] for s in scr])

    a_spec = pl.BlockSpec((tm, tk), lambda i, j, k: (i, k))
    b_spec = pl.BlockSpec((tn, tk), lambda i, j, k: (j, k)) if tb else pl.BlockSpec((tk, tn), lambda i, j, k: (k, j))
    o_spec = pl.BlockSpec((tm, tn), lambda i, j, k: (i, j))
    return pl.pallas_call(
        body, name=name,
        grid=(M // tm, N // tn, nk),
        in_specs=[a_spec] * n_l + [b_spec] * n_l + [o_spec] * n_e,
        out_specs=[o_spec] * n_o,
        out_shape=[jax.ShapeDtypeStruct((M, N), dt) for dt in outs],
        scratch_shapes=[pltpu.VMEM((tm, tn), F32)] * (n_acc if nk > 1 else 0),
        compiler_params=_params(("parallel", "parallel", "arbitrary")),
    )(*lhs, *rhs, *extras)


def _row_spec(tile, width, colblk):
    return pl.BlockSpec((tile, width), lambda i: (i, colblk))


def _full_spec(shape):
    return pl.BlockSpec(shape, lambda i: (0,) * len(shape))


def _rowwise(name, fn, rows, params, outs, tile=256):
    T = rows[0][0].shape[0]
    tile = min(tile, T)
    n_r, n_p = len(rows), len(params)

    def body(*refs):
        r = [x[...].astype(F32) for x in refs[:n_r]]
        p = [x[...].astype(F32) for x in refs[n_r:n_r + n_p]]
        for o_ref, val in zip(refs[n_r + n_p:], fn(*r, *p)):
            o_ref[...] = val.astype(o_ref.dtype)

    return pl.pallas_call(
        body, name=name,
        grid=(T // tile,),
        in_specs=[_row_spec(tile, w, cb) for _, w, cb in rows] + [_full_spec(p.shape) for p in params],
        out_specs=[_row_spec(tile, w, 0) for w, _ in outs],
        out_shape=[jax.ShapeDtypeStruct((T, w), dt) for w, dt in outs],
        compiler_params=_params(("parallel",)),
    )(*([a for a, _, _ in rows] + list(params)))


def _rowwise_vjp(name, fn, rows, params, cts, need_rows, need_params, add_to=None, tile=256, bf16_rows=()):
    add_to = add_to or {}
    T = rows[0][0].shape[0]
    tile = min(tile, T)
    n_r, n_p = len(rows), len(params)
    ct_flat = [c for group in cts for c in group]
    ct_sizes = [len(group) for group in cts]
    add_idx = sorted(add_to)
    row_out = [i for i in range(n_r) if need_rows[i]]
    par_out = [i for i in range(n_p) if need_params[i]]
    n_ct, n_add = len(ct_flat), len(add_idx)

    def body(*refs):
        pos = 0
        r = [x[...].astype(F32) for x in refs[pos:pos + n_r]]
        pos += n_r
        p = [x[...].astype(F32) for x in refs[pos:pos + n_p]]
        pos += n_p
        ct_vals = [x[...].astype(F32) for x in refs[pos:pos + n_ct]]
        pos += n_ct
        adds = {i: x[...] for i, x in zip(add_idx, refs[pos:pos + n_add])}
        pos += n_add
        drow_refs = refs[pos:pos + len(row_out)]
        pos += len(row_out)
        dpar_refs = refs[pos:pos + len(par_out)]
        ct_in, q = [], 0
        for n in ct_sizes:
            ct_in.append(functools.reduce(lambda u, v: u + v, ct_vals[q:q + n]))
            q += n
        _, vjp = jax.vjp(fn, *r, *p)
        grads = vjp(tuple(ct_in))
        for ref, i in zip(drow_refs, row_out):
            g = grads[i]
            ref[...] = (g + adds[i] if i in adds else g).astype(ref.dtype)

        @pl.when(pl.program_id(0) == 0)
        def _():
            for ref in dpar_refs:
                ref[...] = jnp.zeros_like(ref)

        for ref, i in zip(dpar_refs, par_out):
            ref[...] += grads[n_r + i]

    ct_widths = [c.shape[1] for c in ct_flat]
    in_specs = ([_row_spec(tile, w, cb) for _, w, cb in rows] + [_full_spec(p.shape) for p in params]
                + [_row_spec(tile, w, 0) for w in ct_widths] + [_row_spec(tile, rows[i][1], 0) for i in add_idx])
    out_specs = [_row_spec(tile, rows[i][1], 0) for i in row_out] + [_full_spec(params[i].shape) for i in par_out]
    out_shape = ([jax.ShapeDtypeStruct((T, rows[i][1]), BF16 if i in bf16_rows else F32) for i in row_out]
                 + [jax.ShapeDtypeStruct(params[i].shape, F32) for i in par_out])
    res = pl.pallas_call(
        body, name=name,
        grid=(T // tile,),
        in_specs=in_specs, out_specs=out_specs, out_shape=out_shape,
        compiler_params=_params(("arbitrary",)),
    )(*([a for a, _, _ in rows] + list(params) + ct_flat + [add_to[i] for i in add_idx]))
    return res[:len(row_out)], res[len(row_out):]


def _sigmoid(x):
    return 0.5 * (jnp.tanh(0.5 * x) + 1.0)


def _softplus(x):
    return jnp.maximum(x, 0.0) + jnp.log(1.0 + jnp.exp(-jnp.abs(x)))


def _rms(x, g):
    return x * lax.rsqrt(jnp.mean(x * x, axis=-1, keepdims=True) + RMS_EPS) * g


def _segsum_impl(x):
    n, w = x.shape[-1], 2 * LANES
    r = lax.shift_right_logical(lax.broadcasted_iota(jnp.int32, (w, w), 0), 6)
    c = lax.shift_right_logical(lax.broadcasted_iota(jnp.int32, (w, w), 1), 6)
    bd = (r == c).astype(BF16)
    hi = x.astype(BF16)
    lo = (x - hi.astype(F32)).astype(BF16)
    nn = ((1,), (0,))
    blocks = [_dot(hi[:, j:j + w], bd, nn) + _dot(lo[:, j:j + w], bd, nn) for j in range(0, n, w)]
    return jnp.concatenate(blocks, axis=1)


@jax.custom_vjp
def _segsum(x):
    return _segsum_impl(x)


_segsum.defvjp(lambda x: (_segsum_impl(x), None), lambda _, g: (_segsum_impl(g),))


@jax.custom_vjp
def _mmb(a, w):
    return _dot(a.astype(BF16), w.astype(BF16), ((1,), (0,)))


def _mmb_fwd(a, w):
    return _mmb(a, w), (a, w)


def _mmb_bwd(res, g):
    a, w = res
    gb = g.astype(BF16)
    return _dot(gb, w.astype(BF16), ((1,), (1,))), _dot(a.astype(BF16), gb, ((0,), (0,)))


_mmb.defvjp(_mmb_fwd, _mmb_bwd)


def _f_norm(x, g):
    return (_rms(x, g),)


def _f_post1(x, u, g2, g3):
    x1 = x + _rms(u, g2)
    return x1, _rms(x1, g3)


def _f_swiglu(ag, au):
    return (ag * _sigmoid(ag) * au,)


def _f_merge(pg1, pg2, m1, m2, b1, b2):
    return (_sigmoid(pg1 + b1) * m1 + _sigmoid(pg2 + b2) * m2,)


def _f_out(x1, f, g4):
    return (x1 + _rms(f, g4),)


def _f_rwpre(pr, pk, pv, pz, qr, qk, qv, qz, mur, muk, muv, muz, w0, wup, a0, aup, gup, k_k, k_a):
    r = pr + (qr - pr) * mur
    k = pk + (qk - pk) * muk
    v = pv + (qv - pv) * muv
    z = pz + (qz - pz) * muz
    w_raw = w0 + _mmb(jnp.tanh(z), wup)
    lw = -jnp.exp(-_softplus(-w_raw) - 0.5)
    a = _sigmoid(a0 + _mmb(z, aup))
    g = _mmb(_sigmoid(z), gup)
    kk = k * k_k
    kap = kk * lax.rsqrt(jnp.maximum(_segsum(kk * kk), 1e-24))
    k2 = k * (1.0 + (a - 1.0) * k_a)
    return r, lw, k2, v, kap, a, g


def _f_rwpost(y, r, k2, v, g, lnx_w, lnx_b, r_k):
    inv = 1.0 / HEAD_DIM
    yc = y - _segsum(y) * inv
    var = _segsum(yc * yc) * inv
    yn = yc * lax.rsqrt(var + GN_EPS) * lnx_w + lnx_b
    bonus = _segsum(r * k2 * r_k) * v
    return ((yn + bonus) * g,)


RW_GROUPS = (0, 512, 1024, 1536, RW_COLS)


def _column_groups(p):
    return [p[:, a:b] for a, b in zip(RW_GROUPS[:-1], RW_GROUPS[1:])]


def _previous_tokens(p, halo, first_of_sequence):
    rows = lax.broadcasted_iota(jnp.int32, (p.shape[0], 1), 0)
    before = jnp.where(first_of_sequence, 0.0, halo[SUBLANES - 1:SUBLANES, :])
    return jnp.where(rows == 0, before, pltpu.roll(p, 1, axis=0))


def _halo_spec(tile, order):
    per = tile // SUBLANES
    return pl.BlockSpec((SUBLANES, RW_COLS), lambda i: (jnp.maximum(order(i) * per - 1, 0), 0))


def _rw_pre(p_rw, params, S, tile=128):
    T = p_rw.shape[0]
    tile = min(tile, T)
    assert S % tile == 0
    n_p = len(params)

    def body(*refs):
        p_ref, halo_ref = refs[0], refs[1]
        par = [x[...].astype(F32) for x in refs[2:2 + n_p]]
        p = p_ref[...]
        first = lax.rem(pl.program_id(0) * tile, S) == 0
        prev = _previous_tokens(p, halo_ref[...], first)
        for o_ref, val in zip(refs[2 + n_p:], _f_rwpre(*_column_groups(p), *_column_groups(prev), *par)):
            o_ref[...] = val

    out_spec = pl.BlockSpec((tile, RW_WIDTH), lambda i: (i, 0))
    return pl.pallas_call(
        body, name="rw_pre",
        grid=(T // tile,),
        in_specs=[pl.BlockSpec((tile, RW_COLS), lambda i: (i, 0)), _halo_spec(tile, lambda i: i)]
                 + [_full_spec(q.shape) for q in params],
        out_specs=[out_spec] * 7,
        out_shape=[jax.ShapeDtypeStruct((T, RW_WIDTH), F32)] * 7,
        compiler_params=_params(("parallel",)),
    )(p_rw, p_rw, *params)


def _rw_pre_bwd(p_rw, params, cts, S, tile=128):
    T = p_rw.shape[0]
    tile = min(tile, T)
    assert S % tile == 0
    nt = T // tile
    n_p = len(params)
    ct_flat = [c for group in cts for c in group]
    ct_sizes = [len(group) for group in cts]
    n_ct = len(ct_flat)

    def body(*refs):
        p_ref, halo_ref = refs[0], refs[1]
        par = [x[...].astype(F32) for x in refs[2:2 + n_p]]
        ct_vals = [x[...] for x in refs[2 + n_p:2 + n_p + n_ct]]
        dp_ref = refs[2 + n_p + n_ct]
        dpar_refs = refs[3 + n_p + n_ct:3 + 2 * n_p + n_ct]
        carry = refs[-1]
        step = pl.program_id(0)

        @pl.when(step == 0)
        def _():
            carry[...] = jnp.zeros_like(carry)
            for ref in dpar_refs:
                ref[...] = jnp.zeros_like(ref)

        ct_in, q = [], 0
        for n in ct_sizes:
            ct_in.append(functools.reduce(lambda u, v: u + v, ct_vals[q:q + n]))
            q += n
        p = p_ref[...]
        first = lax.rem((nt - 1 - step) * tile, S) == 0
        prev = _previous_tokens(p, halo_ref[...], first)
        _, vjp = jax.vjp(_f_rwpre, *_column_groups(p), *_column_groups(prev), *par)
        grads = vjp(tuple(ct_in))
        d_here = jnp.concatenate(grads[0:4], axis=1)
        d_prev = jnp.concatenate(grads[4:8], axis=1)
        rows = lax.broadcasted_iota(jnp.int32, (tile, 1), 0)
        from_next = jnp.where(rows == tile - 1, carry[0:1, :], pltpu.roll(d_prev, tile - 1, axis=0))
        dp_ref[...] = (d_here + from_next).astype(dp_ref.dtype)
        carry[...] = jnp.broadcast_to(jnp.where(first, 0.0, d_prev[0:1, :]), carry.shape)
        for ref, g in zip(dpar_refs, grads[8:]):
            ref[...] += g

    back = lambda i: nt - 1 - i
    row = lambda w: pl.BlockSpec((tile, w), lambda i: (back(i), 0))
    res = pl.pallas_call(
        body, name="rw_pre_bwd",
        grid=(nt,),
        in_specs=[row(RW_COLS), _halo_spec(tile, back)] + [_full_spec(q.shape) for q in params]
                 + [row(RW_WIDTH)] * n_ct,
        out_specs=[row(RW_COLS)] + [_full_spec(q.shape) for q in params],
        out_shape=[jax.ShapeDtypeStruct((T, RW_COLS), BF16)] + [jax.ShapeDtypeStruct(q.shape, F32) for q in params],
        scratch_shapes=[pltpu.VMEM((SUBLANES, RW_COLS), F32)],
        compiler_params=_params(("arbitrary",)),
    )(p_rw, p_rw, *params, *ct_flat)
    return res[0], res[1:]


def _loss_head(x1, f, target, g4, tile=256):
    T, D = x1.shape
    tile = min(tile, T)

    def body(x1_ref, f_ref, t_ref, g_ref, loss_ref, dx1_ref, df_ref, dg_ref):
        (y,), vjp = jax.vjp(_f_out, x1_ref[...], f_ref[...], g_ref[...])
        err = y - t_ref[...]
        dx1, df, dg = vjp((err * (1.0 / D),))
        dx1_ref[...] = dx1
        df_ref[...] = df.astype(df_ref.dtype)

        @pl.when(pl.program_id(0) == 0)
        def _():
            loss_ref[...] = jnp.zeros_like(loss_ref)
            dg_ref[...] = jnp.zeros_like(dg_ref)

        part = jnp.sum(jnp.sum(err * err, axis=1, keepdims=True), axis=0, keepdims=True) * (0.5 / D)
        loss_ref[...] += jnp.broadcast_to(part, loss_ref.shape)
        dg_ref[...] += dg

    row = pl.BlockSpec((tile, D), lambda i: (i, 0))
    return pl.pallas_call(
        body, name="loss_head",
        grid=(T // tile,),
        in_specs=[row, row, row, _full_spec(g4.shape)],
        out_specs=[_full_spec((SUBLANES, LANES)), row, row, _full_spec(g4.shape)],
        out_shape=[jax.ShapeDtypeStruct((SUBLANES, LANES), F32), jax.ShapeDtypeStruct((T, D), F32),
                   jax.ShapeDtypeStruct((T, D), BF16), jax.ShapeDtypeStruct(g4.shape, F32)],
        compiler_params=_params(("arbitrary",)),
    )(x1, f, target, g4)


def _nn(a, b):
    return _dot(a, b, ((1,), (0,)))


def _nt(a, b):
    return _dot(a, b, ((1,), (1,)))


def _tn(a, b):
    return _dot(a, b, ((0,), (0,)))


def _split_dot(x, u2):
    hi = x.astype(BF16)
    lo = (x - hi.astype(F32)).astype(BF16)
    return _nn(jnp.concatenate([hi, lo], axis=1), u2)


def _by_head(x, masks):
    return jnp.concatenate([(x * m).astype(BF16) for m in masks], axis=0)


def _fold_heads(x2, masks):
    R = x2.shape[0] // len(masks)
    return functools.reduce(lambda u, v: u + v, [x2[h * R:(h + 1) * R] * m for h, m in enumerate(masks)])


def _head_masks():
    lane = lax.broadcasted_iota(jnp.int32, (1, LANES), 1)
    return [((lane >= h * HEAD_DIM) & (lane < (h + 1) * HEAD_DIM)).astype(F32) for h in range(LANES // HEAD_DIM)]


def _key_tri(op):
    row = lax.broadcasted_iota(jnp.int32, (ATTN_KEYS, ATTN_KEYS), 0)
    col = lax.broadcasted_iota(jnp.int32, (ATTN_KEYS, ATTN_KEYS), 1)
    u = op(row, col).astype(BF16)
    return jnp.concatenate([u, u], axis=0)


def _causal(rows):
    row = lax.broadcasted_iota(jnp.int32, (rows, ATTN_KEYS), 0)
    col = lax.broadcasted_iota(jnp.int32, (rows, ATTN_KEYS), 1)
    return col < row


def _from_row(tree, r):
    return jax.tree.map(lambda x: x[r:], tree)


def _onto_rows(old, new, r):
    return jax.tree.map(lambda o, n: jnp.concatenate([o[:r], n], axis=0) if r else n, old, new)


def _sb_weights(qb16, kbh, c_fails, u_gt, strict):
    z_all = _nt(qb16, kbh)
    zs = [z_all[:, h * ATTN_KEYS:(h + 1) * ATTN_KEYS] for h in range(len(c_fails))]
    Ls = [jnp.minimum(-z, 0.0) - jnp.log(1.0 + jnp.exp(-jnp.abs(z))) for z in zs]
    Lms = Ls if strict is None else [jnp.where(strict, L, 0.0) for L in Ls]
    cums = [_split_dot(Lm, u_gt) for Lm in Lms]
    As = [jnp.exp(z + L + c + cum) for z, L, c, cum in zip(zs, Ls, c_fails, cums)]
    if strict is not None:
        As = [jnp.where(strict, A, 0.0) for A in As]
    return zs, Ls, Lms, As


def _attn_specs(S, qb):
    nq = S // qb
    q_spec = pl.BlockSpec((qb, LANES), lambda b, p, i: (b * nq + i, p))
    k_spec = pl.BlockSpec((S, LANES), lambda b, p, i: (b, SB_WIDTH // LANES + p))
    v_spec = pl.BlockSpec((S, LANES), lambda b, p, i: (b, 2 * SB_WIDTH // LANES + p))
    seq = pl.BlockSpec((S, LANES), lambda b, p, i: (b, p))
    return q_spec, k_spec, v_spec, q_spec, seq


def _key_walk(i, qb, block, carry, fails):
    per = qb // ATTN_KEYS
    for sub in reversed(range(per)):
        carry = block(i * per + sub, carry, sub * ATTN_KEYS)
    n = i * per

    def alive(c):
        return jnp.max(functools.reduce(jnp.maximum, fails(c))) > ATTN_DEAD

    def cond(state):
        return jnp.logical_and(state[0] < n, state[1])

    def body(state):
        c = block(n - 1 - state[0], state[2], None)
        return state[0] + 1, alive(c), c

    return lax.while_loop(cond, body, (jnp.int32(0), alive(carry), carry))[2]


def _attn_fwd(proj, B, S):
    qb = min(ATTN_QUERIES, S)
    scale = HEAD_DIM ** -0.5

    def body(q_ref, k_ref, v_ref, o_ref):
        i = pl.program_id(2)
        masks = _head_masks()
        u_gt = _key_tri(lambda r, c: r > c)
        q16 = (q_ref[...] * scale).astype(BF16)

        def block(J, carry, row0):
            r0 = pl.multiple_of(J * ATTN_KEYS, ATTN_KEYS)
            kbh = _by_head(k_ref[pl.ds(r0, ATTN_KEYS), :], masks)
            vbh = _by_head(v_ref[pl.ds(r0, ATTN_KEYS), :], masks)
            lo = row0 or 0
            strict = None if row0 is None else _causal(qb - lo)
            acc, cs = _from_row(carry, lo)
            _, _, Lms, As = _sb_weights(q16[lo:], kbh, cs, u_gt, strict)
            acc = acc + _nn(jnp.concatenate([A.astype(BF16) for A in As], axis=1), vbh)
            cs = tuple(c + jnp.sum(Lm, axis=1, keepdims=True) for c, Lm in zip(cs, Lms))
            return _onto_rows(carry, (acc, cs), lo)

        zero_c = tuple(jnp.zeros((qb, 1), F32) for _ in masks)
        carry = _key_walk(i, qb, block, (jnp.zeros((qb, LANES), F32), zero_c), lambda c: c[1])
        o_ref[...] = carry[0]

    q_spec, k_spec, v_spec, blk, _ = _attn_specs(S, qb)
    return pl.pallas_call(
        body, name="sb_attn_fwd",
        grid=(B, SB_WIDTH // LANES, S // qb),
        in_specs=[q_spec, k_spec, v_spec],
        out_specs=blk,
        out_shape=jax.ShapeDtypeStruct((B * S, SB_WIDTH), F32),
        compiler_params=_params(("parallel", "parallel", "arbitrary")),
    )(proj, proj, proj)


def _attn_bwd(proj, o, do, B, S):
    qb = min(ATTN_QUERIES, S)
    nq = S // qb
    scale = HEAD_DIM ** -0.5

    def body(q_ref, k_ref, v_ref, o_ref, do_ref, dq_ref, dk_out, dv_out, dk_ref, dv_ref):
        i = pl.program_id(2)

        @pl.when(i == 0)
        def _():
            dk_ref[...] = jnp.zeros_like(dk_ref)
            dv_ref[...] = jnp.zeros_like(dv_ref)

        masks = _head_masks()
        u_gt = _key_tri(lambda r, c: r > c)
        u_ge = _key_tri(lambda r, c: r >= c)
        heads = range(len(masks))
        q16 = (q_ref[...] * scale).astype(BF16)
        do16 = do_ref[...].astype(BF16)
        od = o_ref[...] * do16.astype(F32)
        totals = tuple(jnp.sum(od * m, axis=1, keepdims=True) for m in masks)

        def block(J, carry, row0):
            r0 = pl.multiple_of(J * ATTN_KEYS, ATTN_KEYS)
            kbh = _by_head(k_ref[pl.ds(r0, ATTN_KEYS), :], masks)
            vbh = _by_head(v_ref[pl.ds(r0, ATTN_KEYS), :], masks)
            lo = row0 or 0
            strict = None if row0 is None else _causal(qb - lo)
            dq, c_fail, c_p = _from_row(carry, lo)
            tot = _from_row(totals, lo)
            zs, Ls, Lms, As = _sb_weights(q16[lo:], kbh, c_fail, u_gt, strict)
            Abs = [A.astype(BF16) for A in As]
            dA_all = _nt(do16[lo:], vbh)
            Ps = [Abs[h].astype(F32) * dA_all[:, h * ATTN_KEYS:(h + 1) * ATTN_KEYS] for h in heads]
            afters = [c_p[h] + _split_dot(Ps[h], u_ge) for h in heads]
            sigs = [jnp.exp(zs[h] + Ls[h]) for h in heads]
            dzs = [Ps[h] * (1.0 - sigs[h]) - sigs[h] * (tot[h] - afters[h]) for h in heads]
            if strict is not None:
                dzs = [jnp.where(strict, dz, 0.0) for dz in dzs]
            dz_all = jnp.concatenate([dz.astype(BF16) for dz in dzs], axis=1)
            dv_ref[pl.ds(r0, ATTN_KEYS), :] += _fold_heads(_tn(jnp.concatenate(Abs, axis=1), do16[lo:]), masks)
            dk_ref[pl.ds(r0, ATTN_KEYS), :] += _fold_heads(_tn(dz_all, q16[lo:]), masks)
            dq = dq + _nn(dz_all, kbh)
            c_fail = tuple(c_fail[h] + jnp.sum(Lms[h], axis=1, keepdims=True) for h in heads)
            c_p = tuple(c_p[h] + jnp.sum(Ps[h], axis=1, keepdims=True) for h in heads)
            return _onto_rows(carry, (dq, c_fail, c_p), lo)

        zc = tuple(jnp.zeros((qb, 1), F32) for _ in masks)
        carry = _key_walk(i, qb, block, (jnp.zeros((qb, LANES), F32), zc, zc), lambda c: c[1])
        dq_ref[...] = (carry[0] * scale).astype(dq_ref.dtype)

        @pl.when(i == nq - 1)
        def _():
            dk_out[...] = dk_ref[...].astype(dk_out.dtype)
            dv_out[...] = dv_ref[...].astype(dv_out.dtype)

    q_spec, k_spec, v_spec, blk, seq = _attn_specs(S, qb)
    return pl.pallas_call(
        body, name="sb_attn_bwd",
        grid=(B, SB_WIDTH // LANES, nq),
        in_specs=[q_spec, k_spec, v_spec, blk, blk],
        out_specs=[blk, seq, seq],
        out_shape=[jax.ShapeDtypeStruct((B * S, SB_WIDTH), BF16)] * 3,
        scratch_shapes=[pltpu.VMEM((S, LANES), F32), pltpu.VMEM((S, LANES), F32)],
        compiler_params=_params(("parallel", "parallel", "arbitrary")),
    )(proj, proj, proj, o, do)


_BATCHED = {"nn": "gmk,gkn->gmn", "nt": "gmk,gnk->gmn", "tn": "gkm,gkn->gmn"}


def _bdot_raw(a, b, kind, passes):
    e = functools.partial(jnp.einsum, _BATCHED[kind], preferred_element_type=F32)
    ah, bh = a.astype(BF16), b.astype(BF16)
    if passes == 1:
        return e(ah, bh)
    al, bl = (a - ah.astype(F32)).astype(BF16), (b - bh.astype(F32)).astype(BF16)
    return e(ah, bh) + e(ah, bl) + e(al, bh)


def _cumsum_rows(x, kind):
    G, C, _ = x.shape
    row = lax.broadcasted_iota(jnp.int32, (C, C), 0)
    col = lax.broadcasted_iota(jnp.int32, (C, C), 1)
    tri = jnp.broadcast_to((col <= row).astype(BF16), (G, C, C))
    e = functools.partial(jnp.einsum, _BATCHED[kind], preferred_element_type=F32)
    hi = x.astype(BF16)
    return e(tri, hi) + e(tri, (x - hi.astype(F32)).astype(BF16))


@jax.custom_vjp
def _running_sum(x):
    return _cumsum_rows(x, "nn")


_running_sum.defvjp(lambda x: (_cumsum_rows(x, "nn"), None), lambda _, g: (_cumsum_rows(g, "tn"),))


@functools.partial(jax.custom_vjp, nondiff_argnums=(2, 3))
def _bdot(a, b, kind, passes):
    return _bdot_raw(a, b, kind, passes)


def _bdot_fwd(a, b, kind, passes):
    return _bdot_raw(a, b, kind, passes), (a, b)


def _bdot_bwd(kind, passes, res, g):
    a, b = res
    if kind == "nn":
        return _bdot_raw(g, b, "nt", passes), _bdot_raw(a, g, "tn", passes)
    if kind == "nt":
        return _bdot_raw(g, b, "nn", passes), _bdot_raw(g, a, "tn", passes)
    return _bdot_raw(b, g, "nt", passes), _bdot_raw(a, g, "nn", passes)


_bdot.defvjp(_bdot_fwd, _bdot_bwd)


def _solve_powers(m):
    powers = [m]
    for _ in range(max(1, (m.shape[1] - 1).bit_length()) - 1):
        powers.append(_bdot_raw(powers[-1], powers[-1], "nn", 1))
    return powers


def _solve_fwd(m, rhs):
    powers = _solve_powers(m)
    x = rhs
    for p in powers:
        x = x + _bdot_raw(p, x, "nn", 1)
    return x, (powers, x)


def _solve_bwd(res, g):
    powers, x = res
    for p in powers:
        g = g + _bdot_raw(p, g, "tn", 1)
    return _bdot_raw(g, x, "nt", 1), g


@jax.custom_vjp
def _unit_lower_solve(m, rhs):
    return _solve_fwd(m, rhs)[0]


_unit_lower_solve.defvjp(_solve_fwd, _solve_bwd)


def _wkv_chunk(S0, r, lw, k, v, kap, a):
    G, C, N = r.shape
    row = lax.broadcasted_iota(jnp.int32, (C, C), 0)
    col = lax.broadcasted_iota(jnp.int32, (C, C), 1)
    incl = (col <= row).astype(F32)
    strict = (col < row).astype(F32)
    cum = _running_sum(lw)
    e_pos = jnp.exp(cum)
    e_neg = jnp.exp(-cum)
    al = -kap * jnp.exp(cum - lw)
    be = kap * a * e_neg
    kt = k * e_neg
    rt = r * e_pos
    bk = jnp.concatenate([be, kt], axis=1)
    mask = jnp.concatenate([jnp.concatenate([strict, strict], axis=1), jnp.concatenate([incl, incl], axis=1)], axis=0)
    m_all = _bdot(jnp.concatenate([al, rt], axis=1), bk, "nt", 3) * mask
    m_ab, m_ak = m_all[:, :C, :C], m_all[:, :C, C:]
    m_rb, m_rk = m_all[:, C:, :C], m_all[:, C:, C:]
    S0t = jnp.swapaxes(S0, 1, 2)
    sa = _unit_lower_solve(m_ab, _bdot(jnp.concatenate([al, m_ak], axis=2), jnp.concatenate([S0t, v], axis=1), "nn", 3))
    y =_bdot(jnp.concatenate([rt, m_rb, m_rk], axis=2), jnp.concatenate([S0t, sa, v], axis=1), "nn", 3)
    S1 = (S0 + _bdot(jnp.concatenate([sa, v], axis=1), bk, "tn", 1)) * e_pos[:, C - 1:C, :]
    return y, S1


def _split_heads(x):
    return jnp.stack([x[:, h * HEAD_DIM:(h + 1) * HEAD_DIM] for h in range(x.shape[1] // HEAD_DIM)], axis=0)


def _merge_heads(x):
    return jnp.concatenate([x[h] for h in range(x.shape[0])], axis=1)


def _seq_heads(ref):
    return jnp.concatenate([_split_heads(ref[s]) for s in range(ref.shape[0])], axis=0)


def _store_seq_heads(ref, x):
    heads = x.shape[0] // ref.shape[0]
    for s in range(ref.shape[0]):
        ref[s] = _merge_heads(x[s * heads:(s + 1) * heads])


def _wkv_fwd(r, lw, k, v, kap, a, B, S):
    C, H, N = WKV_CHUNK, RW_WIDTH // HEAD_DIM, HEAD_DIM
    nc = S // C
    Q = min(WKV_SEQS, B)

    def body(r_ref, lw_ref, k_ref, v_ref, kap_ref, a_ref, y_ref, st_ref, s_scr):
        @pl.when(pl.program_id(1) == 0)
        def _():
            s_scr[...] = jnp.zeros_like(s_scr)

        S0 = s_scr[...]
        for s in range(Q):
            st_ref[s, 0] = S0[s * H:(s + 1) * H]
        args = [_seq_heads(ref) for ref in (r_ref, lw_ref, k_ref, v_ref, kap_ref, a_ref)]
        y, S1 = _wkv_chunk(S0, *args)
        s_scr[...] = S1
        _store_seq_heads(y_ref, y)

    row_spec = pl.BlockSpec((Q, C, RW_WIDTH), lambda b, c: (b, c, 0))
    seqs = lambda t: t.reshape(B, S, RW_WIDTH)
    y, states = pl.pallas_call(
        body, name="wkv_fwd",
        grid=(B // Q, nc),
        in_specs=[row_spec] * 6,
        out_specs=[row_spec, pl.BlockSpec((Q, 1, H, N, N), lambda b, c: (b, c, 0, 0, 0))],
        out_shape=[jax.ShapeDtypeStruct((B, S, RW_WIDTH), F32), jax.ShapeDtypeStruct((B, nc, H, N, N), F32)],
        scratch_shapes=[pltpu.VMEM((Q * H, N, N), F32)],
        compiler_params=_params(("arbitrary", "arbitrary")),
    )(*map(seqs, (r, lw, k, v, kap, a)))
    return y.reshape(B * S, RW_WIDTH), states


def _wkv_bwd(r, lw, k, v, kap, a, states, dy, B, S):
    C, H, N = WKV_CHUNK, RW_WIDTH // HEAD_DIM, HEAD_DIM
    nc = S // C
    Q = min(WKV_SEQS, B)

    def body(r_ref, lw_ref, k_ref, v_ref, kap_ref, a_ref, st_ref, dy_ref,
             dr_ref, dlw_ref, dk_ref, dv_ref, dkap_ref, da_ref, ds_scr):
        @pl.when(pl.program_id(1) == 0)
        def _():
            ds_scr[...] = jnp.zeros_like(ds_scr)

        args = [_seq_heads(ref) for ref in (r_ref, lw_ref, k_ref, v_ref, kap_ref, a_ref)]
        S0 = jnp.concatenate([st_ref[s, 0] for s in range(Q)], axis=0)
        _, vjp = jax.vjp(_wkv_chunk, S0, *args)
        g = vjp((_seq_heads(dy_ref), ds_scr[...]))
        ds_scr[...] = g[0]
        for ref, gv in zip((dr_ref, dlw_ref, dk_ref, dv_ref, dkap_ref, da_ref), g[1:]):
            _store_seq_heads(ref, gv)

    row_spec = pl.BlockSpec((Q, C, RW_WIDTH), lambda b, c: (b, nc - 1 - c, 0))
    st_spec = pl.BlockSpec((Q, 1, H, N, N), lambda b, c: (b, nc - 1 - c, 0, 0, 0))
    seqs = lambda t: t.reshape(B, S, RW_WIDTH)
    res = pl.pallas_call(
        body, name="wkv_bwd",
        grid=(B // Q, nc),
        in_specs=[row_spec] * 6 + [st_spec, row_spec],
        out_specs=[row_spec] * 6,
        out_shape=[jax.ShapeDtypeStruct((B, S, RW_WIDTH), F32)] * 6,
        scratch_shapes=[pltpu.VMEM((Q * H, N, N), F32)],
        compiler_params=_params(("arbitrary", "arbitrary"), vmem=WKV_BWD_VMEM),
    )(*map(seqs, (r, lw, k, v, kap, a)), states, seqs(dy))
    return [t.reshape(B * S, RW_WIDTH) for t in res]


HBM = pl.BlockSpec(memory_space=pl.ANY)


def _place():
    return lax.axis_index("x"), lax.axis_index("y"), lax.axis_index("c")


def _other_chips(x, y):
    return [(1 - x, y), (x, 1 - y), (1 - x, 1 - y)]


def _all_gather_chips(shards):
    n = len(shards)

    def body(*refs):
        ins, outs = refs[:n], refs[n:2 * n]
        ici_send, ici_recv, d2d_send, d2d_recv, local = refs[2 * n:]
        x, y, c = _place()
        me = 2 * x + y
        sib = (x, y, 1 - c)
        chips = _other_chips(x, y)
        started, copies = [], []
        for w in range(n):
            cp = pltpu.make_async_copy(ins[w].at[c], outs[w].at[me, c], local.at[w])
            cp.start()
            copies.append(cp)
            for j, (px, py) in enumerate(chips):
                rd = pltpu.make_async_remote_copy(
                    src_ref=ins[w].at[c], dst_ref=outs[w].at[me, c], send_sem=ici_send.at[3 * w + j],
                    recv_sem=ici_recv.at[3 * w + j], device_id=(px, py, c), device_id_type=MESH)
                rd.start()
                started.append(rd)
            rd = pltpu.make_async_remote_copy(
                src_ref=ins[w].at[c], dst_ref=outs[w].at[me, c], send_sem=d2d_send.at[4 * w + 3],
                recv_sem=d2d_recv.at[4 * w + 3], device_id=sib, device_id_type=MESH)
            rd.start()
            started.append(rd)
        for w in range(n):
            for j, (px, py) in enumerate(chips):
                src = 2 * px + py
                pltpu.make_async_remote_copy(
                    src_ref=ins[w].at[c], dst_ref=outs[w].at[src, c], send_sem=ici_send.at[3 * w + j],
                    recv_sem=ici_recv.at[3 * w + j], device_id=(px, py, c), device_id_type=MESH).wait_recv()
                rd = pltpu.make_async_remote_copy(
                    src_ref=outs[w].at[src, c], dst_ref=outs[w].at[src, c], send_sem=d2d_send.at[4 * w + j],
                    recv_sem=d2d_recv.at[4 * w + j], device_id=sib, device_id_type=MESH)
                rd.start()
                started.append(rd)
        for w in range(n):
            for j, (px, py) in enumerate(chips):
                pltpu.make_async_remote_copy(
                    src_ref=ins[w].at[c], dst_ref=outs[w].at[2 * px + py, 1 - c], send_sem=d2d_send.at[4 * w + j],
                    recv_sem=d2d_recv.at[4 * w + j], device_id=sib, device_id_type=MESH).wait_recv()
            pltpu.make_async_remote_copy(
                src_ref=ins[w].at[c], dst_ref=outs[w].at[me, 1 - c], send_sem=d2d_send.at[4 * w + 3],
                recv_sem=d2d_recv.at[4 * w + 3], device_id=sib, device_id_type=MESH).wait_recv()
        for rd in started:
            rd.wait_send()
        for cp in copies:
            cp.wait()

    return pl.pallas_call(
        body, name="gather_weights",
        in_specs=[HBM] * n, out_specs=[HBM] * n,
        out_shape=[jax.ShapeDtypeStruct((N_CHIPS,) + s.shape, s.dtype) for s in shards],
        scratch_shapes=[pltpu.SemaphoreType.DMA((3 * n,)), pltpu.SemaphoreType.DMA((3 * n,)),
                        pltpu.SemaphoreType.DMA((4 * n,)), pltpu.SemaphoreType.DMA((4 * n,)),
                        pltpu.SemaphoreType.DMA((n,))],
        compiler_params=pltpu.CompilerParams(has_side_effects=True),
    )(*shards)


def _pair_split(grads):
    n = len(grads)

    def body(*refs):
        ins, theirs = refs[:n], refs[n:2 * n]
        send, recv = refs[2 * n:]
        x, y, c = _place()
        sib = (x, y, 1 - c)
        rds = []
        for w in range(n):
            rd = pltpu.make_async_remote_copy(
                src_ref=ins[w].at[:, 1 - c], dst_ref=theirs[w], send_sem=send.at[w], recv_sem=recv.at[w],
                device_id=sib, device_id_type=MESH)
            rd.start()
            rds.append(rd)
        for rd in rds:
            rd.wait_recv()
        for rd in rds:
            rd.wait_send()

    return pl.pallas_call(
        body, name="grad_pair_split",
        in_specs=[HBM] * n, out_specs=[HBM] * n,
        out_shape=[jax.ShapeDtypeStruct((g.shape[0],) + g.shape[2:], g.dtype) for g in grads],
        scratch_shapes=[pltpu.SemaphoreType.DMA((n,)), pltpu.SemaphoreType.DMA((n,))],
        compiler_params=pltpu.CompilerParams(has_side_effects=True),
    )(*grads)


def _chip_scatter(parts):
    n = len(parts)

    def body(*refs):
        ins, outs = refs[:n], refs[n:2 * n]
        send, recv = refs[2 * n:]
        x, y, c = _place()
        me = 2 * x + y
        rds = []
        for w in range(n):
            for j, (px, py) in enumerate(_other_chips(x, y)):
                s = 3 * w + j
                rd = pltpu.make_async_remote_copy(
                    src_ref=ins[w].at[2 * px + py], dst_ref=outs[w].at[j], send_sem=send.at[s], recv_sem=recv.at[s],
                    device_id=(px, py, c), device_id_type=MESH)
                rd.start()
                rds.append(rd)
        for w in range(n):
            for j, (px, py) in enumerate(_other_chips(x, y)):
                s = 3 * w + j
                pltpu.make_async_remote_copy(
                    src_ref=ins[w].at[me], dst_ref=outs[w].at[j], send_sem=send.at[s], recv_sem=recv.at[s],
                    device_id=(px, py, c), device_id_type=MESH).wait_recv()
        for rd in rds:
            rd.wait_send()

    return pl.pallas_call(
        body, name="grad_chip_scatter",
        in_specs=[HBM] * n, out_specs=[HBM] * n,
        out_shape=[jax.ShapeDtypeStruct((N_CHIPS - 1,) + p.shape[1:], p.dtype) for p in parts],
        scratch_shapes=[pltpu.SemaphoreType.DMA((3 * n,)), pltpu.SemaphoreType.DMA((3 * n,))],
        compiler_params=pltpu.CompilerParams(has_side_effects=True),
    )(*parts)


def _pair_join(bufs):
    n = len(bufs)

    def body(*refs):
        ins, outs = refs[:n], refs[n:2 * n]
        send, recv = refs[2 * n:]
        x, y, c = _place()
        sib = (x, y, 1 - c)
        rds = []
        for w in range(n):
            rd = pltpu.make_async_remote_copy(
                src_ref=ins[w].at[c], dst_ref=outs[w].at[c], send_sem=send.at[w], recv_sem=recv.at[w],
                device_id=sib, device_id_type=MESH)
            rd.start()
            rds.append(rd)
        for w in range(n):
            pltpu.make_async_remote_copy(
                src_ref=ins[w].at[c], dst_ref=outs[w].at[1 - c], send_sem=send.at[w], recv_sem=recv.at[w],
                device_id=sib, device_id_type=MESH).wait_recv()
        for rd in rds:
            rd.wait_send()

    return pl.pallas_call(
        body, name="grad_pair_join",
        in_specs=[HBM] * n, out_specs=[HBM] * n,
        out_shape=[jax.ShapeDtypeStruct(b.shape, b.dtype) for b in bufs],
        input_output_aliases={w: w for w in range(n)},
        scratch_shapes=[pltpu.SemaphoreType.DMA((n,)), pltpu.SemaphoreType.DMA((n,))],
        compiler_params=pltpu.CompilerParams(has_side_effects=True),
    )(*bufs)


def _all_reduce_small(packed):
    R = packed.shape[0]

    def body(x_ref, o_ref, buf, send, recv):
        x, y, c = _place()
        me = 4 * x + 2 * y + c
        buf[me] = x_ref[...]
        rds = []
        for rel in range(1, N_DEV):
            fx, fy, fc = (rel >> 2) & 1, (rel >> 1) & 1, rel & 1
            peer = (1 - x if fx else x, 1 - y if fy else y, 1 - c if fc else c)
            rd = pltpu.make_async_remote_copy(
                src_ref=x_ref, dst_ref=buf.at[me], send_sem=send.at[rel - 1], recv_sem=recv.at[rel - 1],
                device_id=peer, device_id_type=MESH)
            rd.start()
            rds.append((rd, peer))
        for rel in range(1, N_DEV):
            rd, (px, py, pc) = rds[rel - 1]
            pltpu.make_async_remote_copy(
                src_ref=x_ref, dst_ref=buf.at[4 * px + 2 * py + pc], send_sem=send.at[rel - 1], recv_sem=recv.at[rel - 1],
                device_id=(px, py, pc), device_id_type=MESH).wait_recv()
        for rd, _ in rds:
            rd.wait_send()
        total = buf[0]
        for d in range(1, N_DEV):
            total = total + buf[d]
        o_ref[...] = total

    return pl.pallas_call(
        body, name="all_reduce_small",
        in_specs=[pl.BlockSpec(memory_space=pltpu.VMEM)],
        out_specs=pl.BlockSpec(memory_space=pltpu.VMEM),
        out_shape=jax.ShapeDtypeStruct(packed.shape, F32),
        scratch_shapes=[pltpu.VMEM((N_DEV, R, LANES), F32), pltpu.SemaphoreType.DMA((N_DEV - 1,)),
                        pltpu.SemaphoreType.DMA((N_DEV - 1,))],
        compiler_params=pltpu.CompilerParams(has_side_effects=True),
    )(packed)


def _pair_sum(name, split, theirs, core):
    n_chip, _, Rh, C = split.shape
    tile = _div_tile(Rh, 256, 2 * SUBLANES)
    nt = Rh // tile

    def body(core_ref, a_ref, b_ref, o_ref):
        o_ref[...] = (a_ref[...] + b_ref[...]).astype(o_ref.dtype)

    return pl.pallas_call(
        body, name=name,
        grid_spec=pltpu.PrefetchScalarGridSpec(
            num_scalar_prefetch=1,
            grid=(n_chip, nt),
            in_specs=[pl.BlockSpec((None, None, tile, C), lambda j, i, core_ref: (j, core_ref[0], i, 0)),
                      pl.BlockSpec((None, tile, C), lambda j, i, core_ref: (j, i, 0))],
            out_specs=pl.BlockSpec((None, tile, C), lambda j, i, core_ref: (j, i, 0)),
        ),
        out_shape=jax.ShapeDtypeStruct((n_chip, Rh, C), BF16),
        compiler_params=_params(("parallel", "parallel")),
    )(core, split, theirs)


def _chip_sum(name, own, landed, core):
    n_in, Rh, C = landed.shape
    tile = _div_tile(Rh, 256, 2 * SUBLANES)

    def body(core_ref, *refs):
        total = refs[0][...].astype(F32)
        for ref in refs[1:n_in + 1]:
            total = total + ref[...].astype(F32)
        refs[n_in + 1][...] = total

    slot = lambda j: pl.BlockSpec((None, tile, C), lambda i, core_ref: (j, i, 0))
    return pl.pallas_call(
        body, name=name,
        grid_spec=pltpu.PrefetchScalarGridSpec(
            num_scalar_prefetch=1,
            grid=(Rh // tile,),
            in_specs=[pl.BlockSpec((None, tile, C), lambda i, core_ref: (core_ref[1], i, 0))]
                     + [slot(j) for j in range(n_in)],
            out_specs=pl.BlockSpec((None, tile, C), lambda i, core_ref: (core_ref[0], i, 0)),
        ),
        out_shape=jax.ShapeDtypeStruct((2, Rh, C), F32),
        compiler_params=_params(("parallel",)),
    )(core, own, *([landed] * n_in))


def _adamw(name, w, g, m, v):
    R, C = w.shape
    tile = _div_tile(R, 256, SUBLANES)
    c1 = 1.0 / (1.0 - ADAM_B1 ** ADAM_STEP)
    c2 = 1.0 / (1.0 - ADAM_B2 ** ADAM_STEP)

    def body(w_ref, g_ref, m_ref, v_ref, d_ref, nm_ref, nv_ref):
        g_ = g_ref[...]
        nm = ADAM_B1 * m_ref[...] + (1.0 - ADAM_B1) * g_
        nv = ADAM_B2 * v_ref[...] + (1.0 - ADAM_B2) * (g_ * g_)
        d_ref[...] = -ADAM_LR * ((nm * c1) / (jnp.sqrt(nv * c2) + ADAM_EPS) + ADAM_WD * w_ref[...])
        nm_ref[...] = nm
        nv_ref[...] = nv

    spec = pl.BlockSpec((tile, C), lambda i: (i, 0))
    return pl.pallas_call(
        body, name=name,
        grid=(R // tile,),
        in_specs=[spec] * 4, out_specs=[spec] * 3,
        out_shape=[jax.ShapeDtypeStruct((R, C), F32)] * 3,
        compiler_params=_params(("parallel",)),
    )(w, g, m, v)


SMALL =["norm_mix_pre", "b_gate", "mu_rw", "w0", "a0", "k_k", "k_a", "r_k", "lnx_w", "lnx_b",
         "norm_mix_post", "norm_ffn_pre", "norm_ffn_post"]
BIG = ["w_in", "w_up", "a_up", "g_up", "w_sb_out", "w_rw_out", "w_o", "w_ffn_gate", "w_ffn_up", "w_ffn_down"]
ROW_SHARDED = ("w_o", "w_ffn_down")
ORDER = ["norm_mix_pre", "w_in", "b_gate", "mu_rw", "w0", "w_up", "a0", "a_up", "g_up", "k_k", "k_a", "r_k",
         "lnx_w", "lnx_b", "w_sb_out", "w_rw_out", "w_o", "norm_mix_post", "norm_ffn_pre", "w_ffn_gate",
         "w_ffn_up", "w_ffn_down", "norm_ffn_post"]


def _pack_small(vals, extra_rows=0):
    rows = jnp.concatenate([vals[n].reshape(-1, LANES) for n in SMALL], axis=0)
    pad = (-(rows.shape[0] + extra_rows)) % SUBLANES + extra_rows
    return jnp.pad(rows, ((0, pad), (0, 0)))


def _unpack_small(packed, shapes):
    out, r = {}, 0
    for n in SMALL:
        size = 1
        for s in shapes[n]:
            size *= s
        out[n] = packed[r:r + size // LANES].reshape(shapes[n])
        r += size // LANES
    return out


def kernel(x, norm_mix_pre, w_in, b_gate, mu_rw, w0, w_up, a0, a_up, g_up, k_k, k_a, r_k, lnx_w, lnx_b, w_sb_out, w_rw_out, w_o, norm_mix_post, norm_ffn_pre, w_ffn_gate, w_ffn_up, w_ffn_down, norm_ffn_post, loss_target, m_norm_mix_pre, m_w_in, m_b_gate, m_mu_rw, m_w0, m_w_up, m_a0, m_a_up, m_g_up, m_k_k, m_k_a, m_r_k, m_lnx_w, m_lnx_b, m_w_sb_out, m_w_rw_out, m_w_o, m_norm_mix_post, m_norm_ffn_pre, m_w_ffn_gate, m_w_ffn_up, m_w_ffn_down, m_norm_ffn_post, v_norm_mix_pre, v_w_in, v_b_gate, v_mu_rw, v_w0, v_w_up, v_a0, v_a_up, v_g_up, v_k_k, v_k_a, v_r_k, v_lnx_w, v_lnx_b, v_w_sb_out, v_w_rw_out, v_w_o, v_norm_mix_post, v_norm_ffn_pre, v_w_ffn_gate, v_w_ffn_up, v_w_ffn_down, v_norm_ffn_post):
    W = dict(norm_mix_pre=norm_mix_pre, w_in=w_in, b_gate=b_gate, mu_rw=mu_rw, w0=w0, w_up=w_up, a0=a0, a_up=a_up,
             g_up=g_up, k_k=k_k, k_a=k_a, r_k=r_k, lnx_w=lnx_w, lnx_b=lnx_b, w_sb_out=w_sb_out, w_rw_out=w_rw_out,
             w_o=w_o, norm_mix_post=norm_mix_post, norm_ffn_pre=norm_ffn_pre, w_ffn_gate=w_ffn_gate,
             w_ffn_up=w_ffn_up, w_ffn_down=w_ffn_down, norm_ffn_post=norm_ffn_post)
    Mo = dict(norm_mix_pre=m_norm_mix_pre, w_in=m_w_in, b_gate=m_b_gate, mu_rw=m_mu_rw, w0=m_w0, w_up=m_w_up, a0=m_a0,
              a_up=m_a_up, g_up=m_g_up, k_k=m_k_k, k_a=m_k_a, r_k=m_r_k, lnx_w=m_lnx_w, lnx_b=m_lnx_b,
              w_sb_out=m_w_sb_out, w_rw_out=m_w_rw_out, w_o=m_w_o, norm_mix_post=m_norm_mix_post,
              norm_ffn_pre=m_norm_ffn_pre, w_ffn_gate=m_w_ffn_gate, w_ffn_up=m_w_ffn_up, w_ffn_down=m_w_ffn_down,
              norm_ffn_post=m_norm_ffn_post)
    Vo = dict(norm_mix_pre=v_norm_mix_pre, w_in=v_w_in, b_gate=v_b_gate, mu_rw=v_mu_rw, w0=v_w0, w_up=v_w_up, a0=v_a0,
              a_up=v_a_up, g_up=v_g_up, k_k=v_k_k, k_a=v_k_a, r_k=v_r_k, lnx_w=v_lnx_w, lnx_b=v_lnx_b,
              w_sb_out=v_w_sb_out, w_rw_out=v_w_rw_out, w_o=v_w_o, norm_mix_post=v_norm_mix_post,
              norm_ffn_pre=v_norm_ffn_pre, w_ffn_gate=v_w_ffn_gate, w_ffn_up=v_w_ffn_up, w_ffn_down=v_w_ffn_down,
              norm_ffn_post=v_norm_ffn_post)
    shapes = {n: W[n].shape for n in ORDER}
    B, S, D = x.shape
    T = B * S
    x2 = x.reshape(T, D)
    tgt = loss_target.reshape(T, D)
    vec = {n: W[n].reshape(1, -1) for n in SMALL}

    work = lambda t, n: t[0] if n in ROW_SHARDED else jnp.swapaxes(t[0], 0, 1)
    halved = [work(W[n], n).astype(BF16) for n in BIG]
    halved = [h.reshape(2, h.shape[0] // 2, h.shape[1]) for h in halved]
    full = {n: gth.reshape(-1, gth.shape[3]) for n, gth in zip(BIG, _all_gather_chips(halved))}
    w_in_t = full["w_in"]
    w_sb_t, w_rw_t, w_gt_t = w_in_t[:SB_COLS], w_in_t[SB_COLS:SB_COLS + RW_COLS], w_in_t[SB_COLS + RW_COLS:]
    lora_rows = {"w_up": 0, "a_up": 64, "g_up": 128}
    lora = {n: jnp.pad(full[n].T, ((r0, LORA_COLS - r0 - full[n].shape[1]), (0, 0))) for n, r0 in lora_rows.items()}
    mu = vec["mu_rw"]
    mu_parts = [mu[:, :512], mu[:, 512:1024], mu[:, 1024:1536], mu[:, 1536:]]
    b1, b2 = vec["b_gate"][:, :D], vec["b_gate"][:, D:]

    (h1,) = _rowwise("norm_mix_pre", _f_norm, [(x2, D, 0)], [vec["norm_mix_pre"]], [(D, BF16)], tile=512)
    p_sb = _mm("proj_sb", h1, w_sb_t, tb=True, out_dtype=BF16)
    p_rw = _mm("proj_rw", h1, w_rw_t, tb=True)
    p_gt = _mm("proj_gate", h1, w_gt_t, tb=True, out_dtype=BF16)
    o_sb = _attn_fwd(p_sb, B, S)
    pre_params = mu_parts + [vec["w0"], lora["w_up"], vec["a0"], lora["a_up"], lora["g_up"], vec["k_k"], vec["k_a"]]
    r_, lw_, k2_, v_, kap_, a_, g_ = _rw_pre(p_rw, pre_params, S)
    y_wkv, states = _wkv_fwd(r_, lw_, k2_, v_, kap_, a_, B, S)
    post_rows = [(y_wkv, 512, 0), (r_, 512, 0), (k2_, 512, 0), (v_, 512, 0), (g_, 512, 0)]
    post_params = [vec["lnx_w"], vec["lnx_b"], vec["r_k"]]
    (o_rw,) = _rowwise("rw_post", _f_rwpost, post_rows, post_params, [(512, BF16)], tile=512)
    m1 = _mm("mix_sb_out", o_sb, full["w_sb_out"], tb=True, out_dtype=BF16)
    m2 = _mm("mix_rw_out", o_rw, full["w_rw_out"], tb=True, out_dtype=BF16)
    merge_rows = [(p_gt, D, 0), (p_gt, D, 1), (m1, D, 0), (m2, D, 0)]
    (merged,) = _rowwise("merge", _f_merge, merge_rows, [b1, b2], [(D, BF16)], tile=512)
    u = _mm("mix_out", merged, full["w_o"])
    post1_params = [vec["norm_mix_post"], vec["norm_ffn_pre"]]
    x1, h2 = _rowwise("post_mix", _f_post1, [(x2, D, 0), (u, D, 0)], post1_params, [(D, F32), (D, BF16)], tile=512)
    ag, au, sw = _mm_fused("ffn_in", [h2, h2], [full["w_ffn_gate"], full["w_ffn_up"]], [BF16] * 3, tb=True,
                           epilogue=lambda gu, _: (gu[0], gu[1], _f_swiglu(*gu)[0]))
    f = _mm("ffn_down", sw, full["w_ffn_down"])
    loss_part, dx1, df, dg4 = _loss_head(x1, f, tgt, vec["norm_ffn_post"], tile=512)

    gbig, gsmall = {}, {"norm_ffn_post": dg4}
    gbig["w_ffn_down"] = _mm("g_ffn_down", sw, df, ta=True)

    def swiglu_back(dsw, gu):
        return jax.vjp(_f_swiglu, *gu)[1]((dsw[0],))

    dag, dau = _mm_fused("ffn_back", [df], [full["w_ffn_down"]], [BF16] * 2, tb=True, extras=[ag, au],
                         epilogue=swiglu_back)
    (dh2,) = _mm_fused("d_h2", [dag, dau], [full["w_ffn_gate"], full["w_ffn_up"]], [F32], add=True)
    gbig["w_ffn_gate"] = _mm("g_ffn_gate", dag, h2, ta=True)
    gbig["w_ffn_up"] = _mm("g_ffn_up", dau, h2, ta=True)
    (dx_res, du), (dg2, dg3) = _rowwise_vjp("post_mix_bwd", _f_post1, [(x2, D, 0), (u, D, 0)], post1_params,
                                            [[dx1], [dh2]], [True, True], [True, True], bf16_rows=(1,))
    gsmall["norm_mix_post"], gsmall["norm_ffn_pre"] = dg2, dg3
    dmerged = _mm("d_merged", du, full["w_o"], tb=True, out_dtype=BF16)
    gbig["w_o"] = _mm("g_w_o", merged, du, ta=True)
    (dpg1, dpg2, dm1, dm2), (db1, db2) = _rowwise_vjp("merge_bwd", _f_merge, merge_rows, [b1, b2], [[dmerged]],
                                                      [True] * 4, [True, True], bf16_rows=(0, 1, 2, 3))
    gsmall["b_gate"] = jnp.concatenate([db1, db2], axis=1)
    do_sb = _mm("d_o_sb", dm1, full["w_sb_out"])
    do_rw = _mm("d_o_rw", dm2, full["w_rw_out"])
    gbig["w_sb_out"] = _mm("g_sb_out", dm1, o_sb, ta=True)
    gbig["w_rw_out"] = _mm("g_rw_out", dm2, o_rw, ta=True)
    (dy_wkv, dr_a, dk2_a, dv_a, dg_), (dlnx_w, dlnx_b, dr_k) = _rowwise_vjp(
        "rw_post_bwd", _f_rwpost, post_rows, post_params, [[do_rw]], [True] * 5, [True] * 3)
    gsmall["lnx_w"], gsmall["lnx_b"], gsmall["r_k"] = dlnx_w, dlnx_b, dr_k
    dr_b, dlw, dk2_b, dv_b, dkap, da = _wkv_bwd(r_, lw_, k2_, v_, kap_, a_, states, dy_wkv, B, S)
    pre_cts = [[dr_a, dr_b], [dlw], [dk2_a, dk2_b], [dv_a, dv_b], [dkap], [da], [dg_]]
    dp_rw, dpre_params = _rw_pre_bwd(p_rw, pre_params, pre_cts, S)
    gsmall["mu_rw"] = jnp.concatenate(dpre_params[:4], axis=1)
    gsmall["w0"], gsmall["a0"], gsmall["k_k"], gsmall["k_a"] = dpre_params[4], dpre_params[6], dpre_params[9], dpre_params[10]
    glora = {"w_up": dpre_params[5][0:64].T, "a_up": dpre_params[7][64:128].T, "g_up": dpre_params[8][128:256].T}
    dq, dk, dv = _attn_bwd(p_sb, o_sb, do_sb, B, S)
    (dh1,) = _mm_fused("d_h1_sb", [dq, dk, dv], [w_sb_t[:512], w_sb_t[512:1024], w_sb_t[1024:]], [F32], add=True)
    dh1 = _mm("d_h1_rw", dp_rw, w_rw_t, acc=dh1)
    (dh1,) = _mm_fused("d_h1_gate", [dpg1, dpg2], [w_gt_t[:D], w_gt_t[D:]], [F32], add=True,
                       extras=[dh1], epilogue=lambda p, e: (p[0] + e[0],))
    gbig["w_in"] = jnp.concatenate(
        [_mm("g_in_" + tag, d, h1, ta=True)
         for tag, d in (("q", dq), ("k", dk), ("v", dv), ("rw", dp_rw), ("g1", dpg1), ("g2", dpg2))], axis=0)
    (grad_x2,), (dg1,) = _rowwise_vjp("norm_mix_pre_bwd", _f_norm, [(x2, D, 0)], [vec["norm_mix_pre"]], [[dh1]],
                                      [True], [True], add_to={0: dx_res}, tile=512)
    gsmall["norm_mix_pre"] = dg1
    gbig.update(glora)

    split = [gbig[n].reshape(N_CHIPS, 2, gbig[n].shape[0] // (2 * N_CHIPS), gbig[n].shape[1]) for n in BIG]
    core = jnp.stack([lax.axis_index("c"), 2 * lax.axis_index("x") + lax.axis_index("y")]).astype(jnp.int32)
    theirs = _pair_split(split)
    chip_sums = [_pair_sum("pair_sum_" + n, a, b, core) for n, a, b in zip(BIG, split, theirs)]
    landed = _chip_scatter(chip_sums)
    joined = _pair_join([_chip_sum("chip_sum_" + n, own, got, core) for n, own, got in zip(BIG, chip_sums, landed)])
    grads = {n: j.reshape(-1, j.shape[2]) for n, j in zip(BIG, joined)}

    small_local = _pack_small({n: gsmall[n] for n in SMALL}, extra_rows=1)
    loss_row = small_local.shape[0] - 1
    small_local = small_local.at[loss_row].set(loss_part[0])
    small_sum = _all_reduce_small(small_local)
    loss = small_sum[loss_row, 0]

    delta, new_m, new_v = {}, {}, {}
    unwork = lambda t, n: (t if n in ROW_SHARDED else jnp.swapaxes(t, 0, 1))[None]
    for n in BIG:
        d_, m_, v2_ = _adamw("adamw_" + n, work(W[n], n), grads[n], work(Mo[n], n), work(Vo[n], n))
        delta[n], new_m[n], new_v[n], grads[n] = (unwork(t, n) for t in (d_, m_, v2_, grads[n]))
    pk = lambda src: _pack_small({n: src[n] for n in SMALL}, extra_rows=1)
    d_s, m_s, v_s = _adamw("adamw_small", pk(W), small_sum.at[loss_row].set(0.0), pk(Mo), pk(Vo))
    for dst, packed in ((grads, small_sum), (delta, d_s), (new_m, m_s), (new_v, v_s)):
        dst.update(_unpack_small(packed, shapes))

    return (loss, grad_x2.reshape(B, S, D), *[grads[n] for n in ORDER], *[delta[n] for n in ORDER],
            *[new_m[n] for n in ORDER], *[new_v[n] for n in ORDER])
```

```python
import functools

import jax
import jax.numpy as jnp
from jax import lax
from jax.experimental import pallas as pl
from jax.experimental.pallas import tpu as pltpu

F32 = jnp.float32
BF16 = jnp.bfloat16
MESH = pl.DeviceIdType.MESH

D_MODEL = 1024
SB_HEADS = 8
HEAD_DIM = 64
SB_WIDTH = SB_HEADS * HEAD_DIM
RW_WIDTH = 512
LORA_COLS = 256
SB_COLS = 3 * SB_WIDTH
RW_COLS = 3 * RW_WIDTH + LORA_COLS
GATE_COLS = 2 * D_MODEL
D_FF = 2816
RMS_EPS = 1e-6
GN_EPS = HEAD_DIM * 1e-5
WKV_CHUNK = 64
WKV_SEQS = 4
ATTN_QUERIES = 512
ATTN_KEYS = 128
ATTN_DEAD = -120.0
LANES = 128
SUBLANES = 8
N_CHIPS = 4
N_DEV = 8

ADAM_LR = 0.001
ADAM_B1 = 0.9
ADAM_B2 = 0.999
ADAM_EPS = 1e-08
ADAM_WD = 0.01
ADAM_STEP = 10

VMEM_LIMIT = 48 * 1024 * 1024
WKV_BWD_VMEM = 58 * 1024 * 1024


def _params(sem=None, vmem=VMEM_LIMIT, **kw):
    if sem is not None:
        kw["dimension_semantics"] = sem
    return pltpu.CompilerParams(vmem_limit_bytes=vmem, **kw)


def _div_tile(dim, pref, mult=LANES):
    if dim <= pref:
        return dim
    t = pref - pref % mult
    while t >= mult:
        if dim % t == 0:
            return t
        t -= mult
    return dim


def _dot(a, b, dims):
    return lax.dot_general(a, b, (dims, ((), ())), preferred_element_type=F32)


def _mm(name, a, b, *, ta=False, tb=False, acc=None, out_dtype=F32):
    if ta:
        K, M = a.shape
    else:
        M, K = a.shape
    N = b.shape[0] if tb else b.shape[1]
    if ta:
        tm, tn, tk = _div_tile(M, 1408), _div_tile(N, 1408), _div_tile(K, 512)
    else:
        tm, tn, tk = _div_tile(M, 512), _div_tile(N, 1408), _div_tile(K, 1408)
    nk = K // tk
    dims = ((0,) if ta else (1,), (1,) if tb else (0,))
    has_acc = acc is not None

    def body(*refs):
        a_ref, b_ref = refs[0], refs[1]
        part = _dot(a_ref[...].astype(BF16), b_ref[...].astype(BF16), dims)
        if nk == 1:
            o_ref = refs[-1]
            o_ref[...] = (part + refs[2][...] if has_acc else part).astype(o_ref.dtype)
            return
        o_ref, scr = refs[-2], refs[-1]
        k = pl.program_id(2)

        @pl.when(k == 0)
        def _():
            scr[...] = part + refs[2][...] if has_acc else part

        @pl.when(k > 0)
        def _():
            scr[...] += part

        @pl.when(k == nk - 1)
        def _():
            o_ref[...] = scr[...].astype(o_ref.dtype)

    a_spec = pl.BlockSpec((tk, tm), lambda i, j, k: (k, i)) if ta else pl.BlockSpec((tm, tk), lambda i, j, k: (i, k))
    b_spec = pl.BlockSpec((tn, tk), lambda i, j, k: (j, k)) if tb else pl.BlockSpec((tk, tn), lambda i, j, k: (k, j))
    o_spec = pl.BlockSpec((tm, tn), lambda i, j, k: (i, j))
    return pl.pallas_call(
        body, name=name,
        grid=(M // tm, N // tn, nk),
        in_specs=[a_spec, b_spec] + ([o_spec] if has_acc else []),
        out_specs=o_spec,
        out_shape=jax.ShapeDtypeStruct((M, N), out_dtype),
        scratch_shapes=[pltpu.VMEM((tm, tn), F32)] if nk > 1 else [],
        compiler_params=_params(("parallel", "parallel", "arbitrary")),
    )(*([a, b] + ([acc] if has_acc else [])))


def _mm_fused(name, lhs, rhs, outs, *, tb=False, add=False, extras=(), epilogue=None):
    M, K = lhs[0].shape
    N = rhs[0].shape[0] if tb else rhs[0].shape[1]
    tm, tn, tk = _div_tile(M, 512), _div_tile(N, 1408), _div_tile(K, 1408)
    nk = K // tk
    n_l, n_e, n_o = len(lhs), len(extras), len(outs)
    n_acc = 1 if add else n_l
    dims = ((1,), (1,) if tb else (0,))

    def body(*refs):
        l_refs, r_refs = refs[:n_l], refs[n_l:2 * n_l]
        e_refs = refs[2 * n_l:2 * n_l + n_e]
        o_refs = refs[2 * n_l + n_e:2 * n_l + n_e + n_o]
        scr = refs[2 * n_l + n_e + n_o:]
        parts = [_dot(l[...].astype(BF16), r[...].astype(BF16), dims) for l, r in zip(l_refs, r_refs)]
        if add:
            parts = [functools.reduce(lambda u, v: u + v, parts)]

        def finish(vals):
            res = epilogue(vals, [e[...].astype(F32) for e in e_refs]) if epilogue else vals
            for ref, val in zip(o_refs, res):
                ref[...] = val.astype(ref.dtype)

        if nk == 1:
            finish(parts)
            return
        k = pl.program_id(2)

        @pl.when(k == 0)
        def _():
            for s, part in zip(scr, parts):
                s[...] = part

        @pl.when(k > 0)
        def _():
            for s, part in zip(scr, parts):
                s[...] += part

        @pl.when(k == nk - 1)
        def _():
            finish([s[...] for s in scr])

    a_spec = pl.BlockSpec((tm, tk), lambda i, j, k: (i, k))
    b_spec = pl.BlockSpec((tn, tk), lambda i, j, k: (j, k)) if tb else pl.BlockSpec((tk, tn), lambda i, j, k: (k, j))
    o_spec = pl.BlockSpec((tm, tn), lambda i, j, k: (i, j))
    return pl.pallas_call(
        body, name=name,
        grid=(M // tm, N // tn, nk),
        in_specs=[a_spec] * n_l + [b_spec] * n_l + [o_spec] * n_e,
        out_specs=[o_spec] * n_o,
        out_shape=[jax.ShapeDtypeStruct((M, N), dt) for dt in outs],
        scratch_shapes=[pltpu.VMEM((tm, tn), F32)] * (n_acc if nk > 1 else 0),
        compiler_params=_params(("parallel", "parallel", "arbitrary")),
    )(*lhs, *rhs, *extras)


def _row_spec(tile, width, colblk):
    return pl.BlockSpec((tile, width), lambda i: (i, colblk))


def _full_spec(shape):
    return pl.BlockSpec(shape, lambda i: (0,) * len(shape))


def _rowwise(name, fn, rows, params, outs, tile=256):
    T = rows[0][0].shape[0]
    tile = min(tile, T)
    n_r, n_p = len(rows), len(params)

    def body(*refs):
        r = [x[...].astype(F32) for x in refs[:n_r]]
        p = [x[...].astype(F32) for x in refs[n_r:n_r + n_p]]
        for o_ref, val in zip(refs[n_r + n_p:], fn(*r, *p)):
            o_ref[...] = val.astype(o_ref.dtype)

    return pl.pallas_call(
        body, name=name,
        grid=(T // tile,),
        in_specs=[_row_spec(tile, w, cb) for _, w, cb in rows] + [_full_spec(p.shape) for p in params],
        out_specs=[_row_spec(tile, w, 0) for w, _ in outs],
        out_shape=[jax.ShapeDtypeStruct((T, w), dt) for w, dt in outs],
        compiler_params=_params(("parallel",)),
    )(*([a for a, _, _ in rows] + list(params)))


def _rowwise_vjp(name, fn, rows, params, cts, need_rows, need_params, add_to=None, tile=256, bf16_rows=()):
    add_to = add_to or {}
    T = rows[0][0].shape[0]
    tile = min(tile, T)
    n_r, n_p = len(rows), len(params)
    ct_flat = [c for group in cts for c in group]
    ct_sizes = [len(group) for group in cts]
    add_idx = sorted(add_to)
    row_out = [i for i in range(n_r) if need_rows[i]]
    par_out = [i for i in range(n_p) if need_params[i]]
    n_ct, n_add = len(ct_flat), len(add_idx)

    def body(*refs):
        pos = 0
        r = [x[...].astype(F32) for x in refs[pos:pos + n_r]]
        pos += n_r
        p = [x[...].astype(F32) for x in refs[pos:pos + n_p]]
        pos += n_p
        ct_vals = [x[...].astype(F32) for x in refs[pos:pos + n_ct]]
        pos += n_ct
        adds = {i: x[...] for i, x in zip(add_idx, refs[pos:pos + n_add])}
        pos += n_add
        drow_refs = refs[pos:pos + len(row_out)]
        pos += len(row_out)
        dpar_refs = refs[pos:pos + len(par_out)]
        ct_in, q = [], 0
        for n in ct_sizes:
            ct_in.append(functools.reduce(lambda u, v: u + v, ct_vals[q:q + n]))
            q += n
        _, vjp = jax.vjp(fn, *r, *p)
        grads = vjp(tuple(ct_in))
        for ref, i in zip(drow_refs, row_out):
            g = grads[i]
            ref[...] = (g + adds[i] if i in adds else g).astype(ref.dtype)

        @pl.when(pl.program_id(0) == 0)
        def _():
            for ref in dpar_refs:
                ref[...] = jnp.zeros_like(ref)

        for ref, i in zip(dpar_refs, par_out):
            ref[...] += grads[n_r + i]

    ct_widths = [c.shape[1] for c in ct_flat]
    in_specs = ([_row_spec(tile, w, cb) for _, w, cb in rows] + [_full_spec(p.shape) for p in params]
                + [_row_spec(tile, w, 0) for w in ct_widths] + [_row_spec(tile, rows[i][1], 0) for i in add_idx])
    out_specs = [_row_spec(tile, rows[i][1], 0) for i in row_out] + [_full_spec(params[i].shape) for i in par_out]
    out_shape = ([jax.ShapeDtypeStruct((T, rows[i][1]), BF16 if i in bf16_rows else F32) for i in row_out]
                 + [jax.ShapeDtypeStruct(params[i].shape, F32) for i in par_out])
    res = pl.pallas_call(
        body, name=name,
        grid=(T // tile,),
        in_specs=in_specs, out_specs=out_specs, out_shape=out_shape,
        compiler_params=_params(("arbitrary",)),
    )(*([a for a, _, _ in rows] + list(params) + ct_flat + [add_to[i] for i in add_idx]))
    return res[:len(row_out)], res[len(row_out):]


def _sigmoid(x):
    return 0.5 * (jnp.tanh(0.5 * x) + 1.0)


def _softplus(x):
    return jnp.maximum(x, 0.0) + jnp.log(1.0 + jnp.exp(-jnp.abs(x)))


def _rms(x, g):
    return x * lax.rsqrt(jnp.mean(x * x, axis=-1, keepdims=True) + RMS_EPS) * g


def _segsum_impl(x):
    n, w = x.shape[-1], 2 * LANES
    r = lax.shift_right_logical(lax.broadcasted_iota(jnp.int32, (w, w), 0), 6)
    c = lax.shift_right_logical(lax.broadcasted_iota(jnp.int32, (w, w), 1), 6)
    bd = (r == c).astype(BF16)
    hi = x.astype(BF16)
    lo = (x - hi.astype(F32)).astype(BF16)
    nn = ((1,), (0,))
    blocks = [_dot(hi[:, j:j + w], bd, nn) + _dot(lo[:, j:j + w], bd, nn) for j in range(0, n, w)]
    return jnp.concatenate(blocks, axis=1)


@jax.custom_vjp
def _segsum(x):
    return _segsum_impl(x)


_segsum.defvjp(lambda x: (_segsum_impl(x), None), lambda _, g: (_segsum_impl(g),))


@jax.custom_vjp
def _mmb(a, w):
    return _dot(a.astype(BF16), w.astype(BF16), ((1,), (0,)))


def _mmb_fwd(a, w):
    return _mmb(a, w), (a, w)


def _mmb_bwd(res, g):
    a, w = res
    gb = g.astype(BF16)
    return _dot(gb, w.astype(BF16), ((1,), (1,))), _dot(a.astype(BF16), gb, ((0,), (0,)))


_mmb.defvjp(_mmb_fwd, _mmb_bwd)


def _f_norm(x, g):
    return (_rms(x, g),)


def _f_post1(x, u, g2, g3):
    x1 = x + _rms(u, g2)
    return x1, _rms(x1, g3)


def _f_swiglu(ag, au):
    return (ag * _sigmoid(ag) * au,)


def _f_merge(pg1, pg2, m1, m2, b1, b2):
    return (_sigmoid(pg1 + b1) * m1 + _sigmoid(pg2 + b2) * m2,)


def _f_out(x1, f, g4):
    return (x1 + _rms(f, g4),)


def _f_rwpre(pr, pk, pv, pz, qr, qk, qv, qz, mur, muk, muv, muz, w0, wup, a0, aup, gup, k_k, k_a):
    r = pr + (qr - pr) * mur
    k = pk + (qk - pk) * muk
    v = pv + (qv - pv) * muv
    z = pz + (qz - pz) * muz
    w_raw = w0 + _mmb(jnp.tanh(z), wup)
    lw = -jnp.exp(-_softplus(-w_raw) - 0.5)
    a = _sigmoid(a0 + _mmb(z, aup))
    g = _mmb(_sigmoid(z), gup)
    kk = k * k_k
    kap = kk * lax.rsqrt(jnp.maximum(_segsum(kk * kk), 1e-24))
    k2 = k * (1.0 + (a - 1.0) * k_a)
    return r, lw, k2, v, kap, a, g


def _f_rwpost(y, r, k2, v, g, lnx_w, lnx_b, r_k):
    inv = 1.0 / HEAD_DIM
    yc = y - _segsum(y) * inv
    var = _segsum(yc * yc) * inv
    yn = yc * lax.rsqrt(var + GN_EPS) * lnx_w + lnx_b
    bonus = _segsum(r * k2 * r_k) * v
    return ((yn + bonus) * g,)


RW_GROUPS = (0, 512, 1024, 1536, RW_COLS)


def _column_groups(p):
    return [p[:, a:b] for a, b in zip(RW_GROUPS[:-1], RW_GROUPS[1:])]


def _previous_tokens(p, halo, first_of_sequence):
    rows = lax.broadcasted_iota(jnp.int32, (p.shape[0], 1), 0)
    before = jnp.where(first_of_sequence, 0.0, halo[SUBLANES - 1:SUBLANES, :])
    return jnp.where(rows == 0, before, pltpu.roll(p, 1, axis=0))


def _halo_spec(tile, order):
    per = tile // SUBLANES
    return pl.BlockSpec((SUBLANES, RW_COLS), lambda i: (jnp.maximum(order(i) * per - 1, 0), 0))


def _rw_pre(p_rw, params, S, tile=128):
    T = p_rw.shape[0]
    tile = min(tile, T)
    assert S % tile == 0
    n_p = len(params)

    def body(*refs):
        p_ref, halo_ref = refs[0], refs[1]
        par = [x[...].astype(F32) for x in refs[2:2 + n_p]]
        p = p_ref[...]
        first = lax.rem(pl.program_id(0) * tile, S) == 0
        prev = _previous_tokens(p, halo_ref[...], first)
        for o_ref, val in zip(refs[2 + n_p:], _f_rwpre(*_column_groups(p), *_column_groups(prev), *par)):
            o_ref[...] = val

    out_spec = pl.BlockSpec((tile, RW_WIDTH), lambda i: (i, 0))
    return pl.pallas_call(
        body, name="rw_pre",
        grid=(T // tile,),
        in_specs=[pl.BlockSpec((tile, RW_COLS), lambda i: (i, 0)), _halo_spec(tile, lambda i: i)]
                 + [_full_spec(q.shape) for q in params],
        out_specs=[out_spec] * 7,
        out_shape=[jax.ShapeDtypeStruct((T, RW_WIDTH), F32)] * 7,
        compiler_params=_params(("parallel",)),
    )(p_rw, p_rw, *params)


def _rw_pre_bwd(p_rw, params, cts, S, tile=128):
    T = p_rw.shape[0]
    tile = min(tile, T)
    assert S % tile == 0
    nt = T // tile
    n_p = len(params)
    ct_flat = [c for group in cts for c in group]
    ct_sizes = [len(group) for group in cts]
    n_ct = len(ct_flat)

    def body(*refs):
        p_ref, halo_ref = refs[0], refs[1]
        par = [x[...].astype(F32) for x in refs[2:2 + n_p]]
        ct_vals = [x[...] for x in refs[2 + n_p:2 + n_p + n_ct]]
        dp_ref = refs[2 + n_p + n_ct]
        dpar_refs = refs[3 + n_p + n_ct:3 + 2 * n_p + n_ct]
        carry = refs[-1]
        step = pl.program_id(0)

        @pl.when(step == 0)
        def _():
            carry[...] = jnp.zeros_like(carry)
            for ref in dpar_refs:
                ref[...] = jnp.zeros_like(ref)

        ct_in, q = [], 0
        for n in ct_sizes:
            ct_in.append(functools.reduce(lambda u, v: u + v, ct_vals[q:q + n]))
            q += n
        p = p_ref[...]
        first = lax.rem((nt - 1 - step) * tile, S) == 0
        prev = _previous_tokens(p, halo_ref[...], first)
        _, vjp = jax.vjp(_f_rwpre, *_column_groups(p), *_column_groups(prev), *par)
        grads = vjp(tuple(ct_in))
        d_here = jnp.concatenate(grads[0:4], axis=1)
        d_prev = jnp.concatenate(grads[4:8], axis=1)
        rows = lax.broadcasted_iota(jnp.int32, (tile, 1), 0)
        from_next = jnp.where(rows == tile - 1, carry[0:1, :], pltpu.roll(d_prev, tile - 1, axis=0))
        dp_ref[...] = (d_here + from_next).astype(dp_ref.dtype)
        carry[...] = jnp.broadcast_to(jnp.where(first, 0.0, d_prev[0:1, :]), carry.shape)
        for ref, g in zip(dpar_refs, grads[8:]):
            ref[...] += g

    back = lambda i: nt - 1 - i
    row = lambda w: pl.BlockSpec((tile, w), lambda i: (back(i), 0))
    res = pl.pallas_call(
        body, name="rw_pre_bwd",
        grid=(nt,),
        in_specs=[row(RW_COLS), _halo_spec(tile, back)] + [_full_spec(q.shape) for q in params]
                 + [row(RW_WIDTH)] * n_ct,
        out_specs=[row(RW_COLS)] + [_full_spec(q.shape) for q in params],
        out_shape=[jax.ShapeDtypeStruct((T, RW_COLS), BF16)] + [jax.ShapeDtypeStruct(q.shape, F32) for q in params],
        scratch_shapes=[pltpu.VMEM((SUBLANES, RW_COLS), F32)],
        compiler_params=_params(("arbitrary",)),
    )(p_rw, p_rw, *params, *ct_flat)
    return res[0], res[1:]


def _loss_head(x1, f, target, g4, tile=256):
    T, D = x1.shape
    tile = min(tile, T)

    def body(x1_ref, f_ref, t_ref, g_ref, loss_ref, dx1_ref, df_ref, dg_ref):
        (y,), vjp = jax.vjp(_f_out, x1_ref[...], f_ref[...], g_ref[...])
        err = y - t_ref[...]
        dx1, df, dg = vjp((err * (1.0 / D),))
        dx1_ref[...] = dx1
        df_ref[...] = df.astype(df_ref.dtype)

        @pl.when(pl.program_id(0) == 0)
        def _():
            loss_ref[...] = jnp.zeros_like(loss_ref)
            dg_ref[...] = jnp.zeros_like(dg_ref)

        part = jnp.sum(jnp.sum(err * err, axis=1, keepdims=True), axis=0, keepdims=True) * (0.5 / D)
        loss_ref[...] += jnp.broadcast_to(part, loss_ref.shape)
        dg_ref[...] += dg

    row = pl.BlockSpec((tile, D), lambda i: (i, 0))
    return pl.pallas_call(
        body, name="loss_head",
        grid=(T // tile,),
        in_specs=[row, row, row, _full_spec(g4.shape)],
        out_specs=[_full_spec((SUBLANES, LANES)), row, row, _full_spec(g4.shape)],
        out_shape=[jax.ShapeDtypeStruct((SUBLANES, LANES), F32), jax.ShapeDtypeStruct((T, D), F32),
                   jax.ShapeDtypeStruct((T, D), BF16), jax.ShapeDtypeStruct(g4.shape, F32)],
        compiler_params=_params(("arbitrary",)),
    )(x1, f, target, g4)


def _nn(a, b):
    return _dot(a, b, ((1,), (0,)))


def _nt(a, b):
    return _dot(a, b, ((1,), (1,)))


def _tn(a, b):
    return _dot(a, b, ((0,), (0,)))


def _split_dot(x, u2):
    hi = x.astype(BF16)
    lo = (x - hi.astype(F32)).astype(BF16)
    return _nn(jnp.concatenate([hi, lo], axis=1), u2)


def _by_head(x, masks):
    return jnp.concatenate([(x * m).astype(BF16) for m in masks], axis=0)


def _fold_heads(x2, masks):
    R = x2.shape[0] // len(masks)
    return functools.reduce(lambda u, v: u + v, [x2[h * R:(h + 1) * R] * m for h, m in enumerate(masks)])


def _head_masks():
    lane = lax.broadcasted_iota(jnp.int32, (1, LANES), 1)
    return [((lane >= h * HEAD_DIM) & (lane < (h + 1) * HEAD_DIM)).astype(F32) for h in range(LANES // HEAD_DIM)]


def _key_tri(op):
    row = lax.broadcasted_iota(jnp.int32, (ATTN_KEYS, ATTN_KEYS), 0)
    col = lax.broadcasted_iota(jnp.int32, (ATTN_KEYS, ATTN_KEYS), 1)
    u = op(row, col).astype(BF16)
    return jnp.concatenate([u, u], axis=0)


def _causal(rows):
    row = lax.broadcasted_iota(jnp.int32, (rows, ATTN_KEYS), 0)
    col = lax.broadcasted_iota(jnp.int32, (rows, ATTN_KEYS), 1)
    return col < row


def _from_row(tree, r):
    return jax.tree.map(lambda x: x[r:], tree)


def _onto_rows(old, new, r):
    return jax.tree.map(lambda o, n: jnp.concatenate([o[:r], n], axis=0) if r else n, old, new)


def _sb_weights(qb16, kbh, c_fails, u_gt, strict):
    z_all = _nt(qb16, kbh)
    zs = [z_all[:, h * ATTN_KEYS:(h + 1) * ATTN_KEYS] for h in range(len(c_fails))]
    Ls = [jnp.minimum(-z, 0.0) - jnp.log(1.0 + jnp.exp(-jnp.abs(z))) for z in zs]
    Lms = Ls if strict is None else [jnp.where(strict, L, 0.0) for L in Ls]
    cums = [_split_dot(Lm, u_gt) for Lm in Lms]
    As = [jnp.exp(z + L + c + cum) for z, L, c, cum in zip(zs, Ls, c_fails, cums)]
    if strict is not None:
        As = [jnp.where(strict, A, 0.0) for A in As]
    return zs, Ls, Lms, As


def _attn_specs(S, qb):
    nq = S // qb
    q_spec = pl.BlockSpec((qb, LANES), lambda b, p, i: (b * nq + i, p))
    k_spec = pl.BlockSpec((S, LANES), lambda b, p, i: (b, SB_WIDTH // LANES + p))
    v_spec = pl.BlockSpec((S, LANES), lambda b, p, i: (b, 2 * SB_WIDTH // LANES + p))
    seq = pl.BlockSpec((S, LANES), lambda b, p, i: (b, p))
    return q_spec, k_spec, v_spec, q_spec, seq


def _key_walk(i, qb, block, carry, fails):
    per = qb // ATTN_KEYS
    for sub in reversed(range(per)):
        carry = block(i * per + sub, carry, sub * ATTN_KEYS)
    n = i * per

    def alive(c):
        return jnp.max(functools.reduce(jnp.maximum, fails(c))) > ATTN_DEAD

    def cond(state):
        return jnp.logical_and(state[0] < n, state[1])

    def body(state):
        c = block(n - 1 - state[0], state[2], None)
        return state[0] + 1, alive(c), c

    return lax.while_loop(cond, body, (jnp.int32(0), alive(carry), carry))[2]


def _attn_fwd(proj, B, S):
    qb = min(ATTN_QUERIES, S)
    scale = HEAD_DIM ** -0.5

    def body(q_ref, k_ref, v_ref, o_ref):
        i = pl.program_id(2)
        masks = _head_masks()
        u_gt = _key_tri(lambda r, c: r > c)
        q16 = (q_ref[...] * scale).astype(BF16)

        def block(J, carry, row0):
            r0 = pl.multiple_of(J * ATTN_KEYS, ATTN_KEYS)
            kbh = _by_head(k_ref[pl.ds(r0, ATTN_KEYS), :], masks)
            vbh = _by_head(v_ref[pl.ds(r0, ATTN_KEYS), :], masks)
            lo = row0 or 0
            strict = None if row0 is None else _causal(qb - lo)
            acc, cs = _from_row(carry, lo)
            _, _, Lms, As = _sb_weights(q16[lo:], kbh, cs, u_gt, strict)
            acc = acc + _nn(jnp.concatenate([A.astype(BF16) for A in As], axis=1), vbh)
            cs = tuple(c + jnp.sum(Lm, axis=1, keepdims=True) for c, Lm in zip(cs, Lms))
            return _onto_rows(carry, (acc, cs), lo)

        zero_c = tuple(jnp.zeros((qb, 1), F32) for _ in masks)
        carry = _key_walk(i, qb, block, (jnp.zeros((qb, LANES), F32), zero_c), lambda c: c[1])
        o_ref[...] = carry[0]

    q_spec, k_spec, v_spec, blk, _ = _attn_specs(S, qb)
    return pl.pallas_call(
        body, name="sb_attn_fwd",
        grid=(B, SB_WIDTH // LANES, S // qb),
        in_specs=[q_spec, k_spec, v_spec],
        out_specs=blk,
        out_shape=jax.ShapeDtypeStruct((B * S, SB_WIDTH), F32),
        compiler_params=_params(("parallel", "parallel", "arbitrary")),
    )(proj, proj, proj)


def _attn_bwd(proj, o, do, B, S):
    qb = min(ATTN_QUERIES, S)
    nq = S // qb
    scale = HEAD_DIM ** -0.5

    def body(q_ref, k_ref, v_ref, o_ref, do_ref, dq_ref, dk_out, dv_out, dk_ref, dv_ref):
        i = pl.program_id(2)

        @pl.when(i == 0)
        def _():
            dk_ref[...] = jnp.zeros_like(dk_ref)
            dv_ref[...] = jnp.zeros_like(dv_ref)

        masks = _head_masks()
        u_gt = _key_tri(lambda r, c: r > c)
        u_ge = _key_tri(lambda r, c: r >= c)
        heads = range(len(masks))
        q16 = (q_ref[...] * scale).astype(BF16)
        do16 = do_ref[...].astype(BF16)
        od = o_ref[...] * do16.astype(F32)
        totals = tuple(jnp.sum(od * m, axis=1, keepdims=True) for m in masks)

        def block(J, carry, row0):
            r0 = pl.multiple_of(J * ATTN_KEYS, ATTN_KEYS)
            kbh = _by_head(k_ref[pl.ds(r0, ATTN_KEYS), :], masks)
            vbh = _by_head(v_ref[pl.ds(r0, ATTN_KEYS), :], masks)
            lo = row0 or 0
            strict = None if row0 is None else _causal(qb - lo)
            dq, c_fail, c_p = _from_row(carry, lo)
            tot = _from_row(totals, lo)
            zs, Ls, Lms, As = _sb_weights(q16[lo:], kbh, c_fail, u_gt, strict)
            Abs = [A.astype(BF16) for A in As]
            dA_all = _nt(do16[lo:], vbh)
            Ps = [Abs[h].astype(F32) * dA_all[:, h * ATTN_KEYS:(h + 1) * ATTN_KEYS] for h in heads]
            afters = [c_p[h] + _split_dot(Ps[h], u_ge) for h in heads]
            sigs = [jnp.exp(zs[h] + Ls[h]) for h in heads]
            dzs = [Ps[h] * (1.0 - sigs[h]) - sigs[h] * (tot[h] - afters[h]) for h in heads]
            if strict is not None:
                dzs = [jnp.where(strict, dz, 0.0) for dz in dzs]
            dz_all = jnp.concatenate([dz.astype(BF16) for dz in dzs], axis=1)
            dv_ref[pl.ds(r0, ATTN_KEYS), :] += _fold_heads(_tn(jnp.concatenate(Abs, axis=1), do16[lo:]), masks)
            dk_ref[pl.ds(r0, ATTN_KEYS), :] += _fold_heads(_tn(dz_all, q16[lo:]), masks)
            dq = dq + _nn(dz_all, kbh)
            c_fail = tuple(c_fail[h] + jnp.sum(Lms[h], axis=1, keepdims=True) for h in heads)
            c_p = tuple(c_p[h] + jnp.sum(Ps[h], axis=1, keepdims=True) for h in heads)
            return _onto_rows(carry, (dq, c_fail, c_p), lo)

        zc = tuple(jnp.zeros((qb, 1), F32) for _ in masks)
        carry = _key_walk(i, qb, block, (jnp.zeros((qb, LANES), F32), zc, zc), lambda c: c[1])
        dq_ref[...] = (carry[0] * scale).astype(dq_ref.dtype)

        @pl.when(i == nq - 1)
        def _():
            dk_out[...] = dk_ref[...].astype(dk_out.dtype)
            dv_out[...] = dv_ref[...].astype(dv_out.dtype)

    q_spec, k_spec, v_spec, blk, seq = _attn_specs(S, qb)
    return pl.pallas_call(
        body, name="sb_attn_bwd",
        grid=(B, SB_WIDTH // LANES, nq),
        in_specs=[q_spec, k_spec, v_spec, blk, blk],
        out_specs=[blk, seq, seq],
        out_shape=[jax.ShapeDtypeStruct((B * S, SB_WIDTH), BF16)] * 3,
        scratch_shapes=[pltpu.VMEM((S, LANES), F32), pltpu.VMEM((S, LANES), F32)],
        compiler_params=_params(("parallel", "parallel", "arbitrary")),
    )(proj, proj, proj, o, do)


_BATCHED = {"nn": "gmk,gkn->gmn", "nt": "gmk,gnk->gmn", "tn": "gkm,gkn->gmn"}


def _bdot_raw(a, b, kind, passes):
    e = functools.partial(jnp.einsum, _BATCHED[kind], preferred_element_type=F32)
    ah, bh = a.astype(BF16), b.astype(BF16)
    if passes == 1:
        return e(ah, bh)
    al, bl = (a - ah.astype(F32)).astype(BF16), (b - bh.astype(F32)).astype(BF16)
    return e(ah, bh) + e(ah, bl) + e(al, bh)


def _cumsum_rows(x, kind):
    G, C, _ = x.shape
    row = lax.broadcasted_iota(jnp.int32, (C, C), 0)
    col = lax.broadcasted_iota(jnp.int32, (C, C), 1)
    tri = jnp.broadcast_to((col <= row).astype(BF16), (G, C, C))
    e = functools.partial(jnp.einsum, _BATCHED[kind], preferred_element_type=F32)
    hi = x.astype(BF16)
    return e(tri, hi) + e(tri, (x - hi.astype(F32)).astype(BF16))


@jax.custom_vjp
def _running_sum(x):
    return _cumsum_rows(x, "nn")


_running_sum.defvjp(lambda x: (_cumsum_rows(x, "nn"), None), lambda _, g: (_cumsum_rows(g, "tn"),))


@functools.partial(jax.custom_vjp, nondiff_argnums=(2, 3))
def _bdot(a, b, kind, passes):
    return _bdot_raw(a, b, kind, passes)


def _bdot_fwd(a, b, kind, passes):
    return _bdot_raw(a, b, kind, passes), (a, b)


def _bdot_bwd(kind, passes, res, g):
    a, b = res
    if kind == "nn":
        return _bdot_raw(g, b, "nt", passes), _bdot_raw(a, g, "tn", passes)
    if kind == "nt":
        return _bdot_raw(g, b, "nn", passes), _bdot_raw(g, a, "tn", passes)
    return _bdot_raw(b, g, "nt", passes), _bdot_raw(a, g, "nn", passes)


_bdot.defvjp(_bdot_fwd, _bdot_bwd)


def _solve_powers(m):
    powers = [m]
    for _ in range(max(1, (m.shape[1] - 1).bit_length()) - 1):
        powers.append(_bdot_raw(powers[-1], powers[-1], "nn", 1))
    return powers


def _solve_fwd(m, rhs):
    powers = _solve_powers(m)
    x = rhs
    for p in powers:
        x = x + _bdot_raw(p, x, "nn", 1)
    return x, (powers, x)


def _solve_bwd(res, g):
    powers, x = res
    for p in powers:
        g = g + _bdot_raw(p, g, "tn", 1)
    return _bdot_raw(g, x, "nt", 1), g


@jax.custom_vjp
def _unit_lower_solve(m, rhs):
    return _solve_fwd(m, rhs)[0]


_unit_lower_solve.defvjp(_solve_fwd, _solve_bwd)


def _wkv_chunk(S0, r, lw, k, v, kap, a):
    G, C, N = r.shape
    row = lax.broadcasted_iota(jnp.int32, (C, C), 0)
    col = lax.broadcasted_iota(jnp.int32, (C, C), 1)
    incl = (col <= row).astype(F32)
    strict = (col < row).astype(F32)
    cum = _running_sum(lw)
    e_pos = jnp.exp(cum)
    e_neg = jnp.exp(-cum)
    al = -kap * jnp.exp(cum - lw)
    be = kap * a * e_neg
    kt = k * e_neg
    rt = r * e_pos
    bk = jnp.concatenate([be, kt], axis=1)
    mask = jnp.concatenate([jnp.concatenate([strict, strict], axis=1), jnp.concatenate([incl, incl], axis=1)], axis=0)
    m_all = _bdot(jnp.concatenate([al, rt], axis=1), bk, "nt", 3) * mask
    m_ab, m_ak = m_all[:, :C, :C], m_all[:, :C, C:]
    m_rb, m_rk = m_all[:, C:, :C], m_all[:, C:, C:]
    S0t = jnp.swapaxes(S0, 1, 2)
    sa = _unit_lower_solve(m_ab, _bdot(jnp.concatenate([al, m_ak], axis=2), jnp.concatenate([S0t, v], axis=1), "nn", 3))
    y =_bdot(jnp.concatenate([rt, m_rb, m_rk], axis=2), jnp.concatenate([S0t, sa, v], axis=1), "nn", 3)
    S1 = (S0 + _bdot(jnp.concatenate([sa, v], axis=1), bk, "tn", 1)) * e_pos[:, C - 1:C, :]
    return y, S1


def _split_heads(x):
    return jnp.stack([x[:, h * HEAD_DIM:(h + 1) * HEAD_DIM] for h in range(x.shape[1] // HEAD_DIM)], axis=0)


def _merge_heads(x):
    return jnp.concatenate([x[h] for h in range(x.shape[0])], axis=1)


def _seq_heads(ref):
    return jnp.concatenate([_split_heads(ref[s]) for s in range(ref.shape[0])], axis=0)


def _store_seq_heads(ref, x):
    heads = x.shape[0] // ref.shape[0]
    for s in range(ref.shape[0]):
        ref[s] = _merge_heads(x[s * heads:(s + 1) * heads])


def _wkv_fwd(r, lw, k, v, kap, a, B, S):
    C, H, N = WKV_CHUNK, RW_WIDTH // HEAD_DIM, HEAD_DIM
    nc = S // C
    Q = min(WKV_SEQS, B)

    def body(r_ref, lw_ref, k_ref, v_ref, kap_ref, a_ref, y_ref, st_ref, s_scr):
        @pl.when(pl.program_id(1) == 0)
        def _():
            s_scr[...] = jnp.zeros_like(s_scr)

        S0 = s_scr[...]
        for s in range(Q):
            st_ref[s, 0] = S0[s * H:(s + 1) * H]
        args = [_seq_heads(ref) for ref in (r_ref, lw_ref, k_ref, v_ref, kap_ref, a_ref)]
        y, S1 = _wkv_chunk(S0, *args)
        s_scr[...] = S1
        _store_seq_heads(y_ref, y)

    row_spec = pl.BlockSpec((Q, C, RW_WIDTH), lambda b, c: (b, c, 0))
    seqs = lambda t: t.reshape(B, S, RW_WIDTH)
    y, states = pl.pallas_call(
        body, name="wkv_fwd",
        grid=(B // Q, nc),
        in_specs=[row_spec] * 6,
        out_specs=[row_spec, pl.BlockSpec((Q, 1, H, N, N), lambda b, c: (b, c, 0, 0, 0))],
        out_shape=[jax.ShapeDtypeStruct((B, S, RW_WIDTH), F32), jax.ShapeDtypeStruct((B, nc, H, N, N), F32)],
        scratch_shapes=[pltpu.VMEM((Q * H, N, N), F32)],
        compiler_params=_params(("arbitrary", "arbitrary")),
    )(*map(seqs, (r, lw, k, v, kap, a)))
    return y.reshape(B * S, RW_WIDTH), states


def _wkv_bwd(r, lw, k, v, kap, a, states, dy, B, S):
    C, H, N = WKV_CHUNK, RW_WIDTH // HEAD_DIM, HEAD_DIM
    nc = S // C
    Q = min(WKV_SEQS, B)

    def body(r_ref, lw_ref, k_ref, v_ref, kap_ref, a_ref, st_ref, dy_ref,
             dr_ref, dlw_ref, dk_ref, dv_ref, dkap_ref, da_ref, ds_scr):
        @pl.when(pl.program_id(1) == 0)
        def _():
            ds_scr[...] = jnp.zeros_like(ds_scr)

        args = [_seq_heads(ref) for ref in (r_ref, lw_ref, k_ref, v_ref, kap_ref, a_ref)]
        S0 = jnp.concatenate([st_ref[s, 0] for s in range(Q)], axis=0)
        _, vjp = jax.vjp(_wkv_chunk, S0, *args)
        g = vjp((_seq_heads(dy_ref), ds_scr[...]))
        ds_scr[...] = g[0]
        for ref, gv in zip((dr_ref, dlw_ref, dk_ref, dv_ref, dkap_ref, da_ref), g[1:]):
            _store_seq_heads(ref, gv)

    row_spec = pl.BlockSpec((Q, C, RW_WIDTH), lambda b, c: (b, nc - 1 - c, 0))
    st_spec = pl.BlockSpec((Q, 1, H, N, N), lambda b, c: (b, nc - 1 - c, 0, 0, 0))
    seqs = lambda t: t.reshape(B, S, RW_WIDTH)
    res = pl.pallas_call(
        body, name="wkv_bwd",
        grid=(B // Q, nc),
        in_specs=[row_spec] * 6 + [st_spec, row_spec],
        out_specs=[row_spec] * 6,
        out_shape=[jax.ShapeDtypeStruct((B, S, RW_WIDTH), F32)] * 6,
        scratch_shapes=[pltpu.VMEM((Q * H, N, N), F32)],
        compiler_params=_params(("arbitrary", "arbitrary"), vmem=WKV_BWD_VMEM),
    )(*map(seqs, (r, lw, k, v, kap, a)), states, seqs(dy))
    return [t.reshape(B * S, RW_WIDTH) for t in res]


HBM = pl.BlockSpec(memory_space=pl.ANY)


def _place():
    return lax.axis_index("x"), lax.axis_index("y"), lax.axis_index("c")


def _other_chips(x, y):
    return [(1 - x, y), (x, 1 - y), (1 - x, 1 - y)]


def _all_gather_chips(shards):
    n = len(shards)

    def body(*refs):
        ins, outs = refs[:n], refs[n:2 * n]
        ici_send, ici_recv, d2d_send, d2d_recv, local = refs[2 * n:]
        x, y, c = _place()
        me = 2 * x + y
        sib = (x, y, 1 - c)
        chips = _other_chips(x, y)
        started, copies = [], []
        for w in range(n):
            cp = pltpu.make_async_copy(ins[w].at[c], outs[w].at[me, c], local.at[w])
            cp.start()
            copies.append(cp)
            for j, (px, py) in enumerate(chips):
                rd = pltpu.make_async_remote_copy(
                    src_ref=ins[w].at[c], dst_ref=outs[w].at[me, c], send_sem=ici_send.at[3 * w + j],
                    recv_sem=ici_recv.at[3 * w + j], device_id=(px, py, c), device_id_type=MESH)
                rd.start()
                started.append(rd)
            rd = pltpu.make_async_remote_copy(
                src_ref=ins[w].at[c], dst_ref=outs[w].at[me, c], send_sem=d2d_send.at[4 * w + 3],
                recv_sem=d2d_recv.at[4 * w + 3], device_id=sib, device_id_type=MESH)
            rd.start()
            started.append(rd)
        for w in range(n):
            for j, (px, py) in enumerate(chips):
                src = 2 * px + py
                pltpu.make_async_remote_copy(
                    src_ref=ins[w].at[c], dst_ref=outs[w].at[src, c], send_sem=ici_send.at[3 * w + j],
                    recv_sem=ici_recv.at[3 * w + j], device_id=(px, py, c), device_id_type=MESH).wait_recv()
                rd = pltpu.make_async_remote_copy(
                    src_ref=outs[w].at[src, c], dst_ref=outs[w].at[src, c], send_sem=d2d_send.at[4 * w + j],
                    recv_sem=d2d_recv.at[4 * w + j], device_id=sib, device_id_type=MESH)
                rd.start()
                started.append(rd)
        for w in range(n):
            for j, (px, py) in enumerate(chips):
                pltpu.make_async_remote_copy(
                    src_ref=ins[w].at[c], dst_ref=outs[w].at[2 * px + py, 1 - c], send_sem=d2d_send.at[4 * w + j],
                    recv_sem=d2d_recv.at[4 * w + j], device_id=sib, device_id_type=MESH).wait_recv()
            pltpu.make_async_remote_copy(
                src_ref=ins[w].at[c], dst_ref=outs[w].at[me, 1 - c], send_sem=d2d_send.at[4 * w + 3],
                recv_sem=d2d_recv.at[4 * w + 3], device_id=sib, device_id_type=MESH).wait_recv()
        for rd in started:
            rd.wait_send()
        for cp in copies:
            cp.wait()

    return pl.pallas_call(
        body, name="gather_weights",
        in_specs=[HBM] * n, out_specs=[HBM] * n,
        out_shape=[jax.ShapeDtypeStruct((N_CHIPS,) + s.shape, s.dtype) for s in shards],
        scratch_shapes=[pltpu.SemaphoreType.DMA((3 * n,)), pltpu.SemaphoreType.DMA((3 * n,)),
                        pltpu.SemaphoreType.DMA((4 * n,)), pltpu.SemaphoreType.DMA((4 * n,)),
                        pltpu.SemaphoreType.DMA((n,))],
        compiler_params=pltpu.CompilerParams(has_side_effects=True),
    )(*shards)


def _pair_split(grads):
    n = len(grads)

    def body(*refs):
        ins, theirs = refs[:n], refs[n:2 * n]
        send, recv = refs[2 * n:]
        x, y, c = _place()
        sib = (x, y, 1 - c)
        rds = []
        for w in range(n):
            rd = pltpu.make_async_remote_copy(
                src_ref=ins[w].at[:, 1 - c], dst_ref=theirs[w], send_sem=send.at[w], recv_sem=recv.at[w],
                device_id=sib, device_id_type=MESH)
            rd.start()
            rds.append(rd)
        for rd in rds:
            rd.wait_recv()
        for rd in rds:
            rd.wait_send()

    return pl.pallas_call(
        body, name="grad_pair_split",
        in_specs=[HBM] * n, out_specs=[HBM] * n,
        out_shape=[jax.ShapeDtypeStruct((g.shape[0],) + g.shape[2:], g.dtype) for g in grads],
        scratch_shapes=[pltpu.SemaphoreType.DMA((n,)), pltpu.SemaphoreType.DMA((n,))],
        compiler_params=pltpu.CompilerParams(has_side_effects=True),
    )(*grads)


def _chip_scatter(parts):
    n = len(parts)

    def body(*refs):
        ins, outs = refs[:n], refs[n:2 * n]
        send, recv = refs[2 * n:]
        x, y, c = _place()
        me = 2 * x + y
        rds = []
        for w in range(n):
            for j, (px, py) in enumerate(_other_chips(x, y)):
                s = 3 * w + j
                rd = pltpu.make_async_remote_copy(
                    src_ref=ins[w].at[2 * px + py], dst_ref=outs[w].at[j], send_sem=send.at[s], recv_sem=recv.at[s],
                    device_id=(px, py, c), device_id_type=MESH)
                rd.start()
                rds.append(rd)
        for w in range(n):
            for j, (px, py) in enumerate(_other_chips(x, y)):
                s = 3 * w + j
                pltpu.make_async_remote_copy(
                    src_ref=ins[w].at[me], dst_ref=outs[w].at[j], send_sem=send.at[s], recv_sem=recv.at[s],
                    device_id=(px, py, c), device_id_type=MESH).wait_recv()
        for rd in rds:
            rd.wait_send()

    return pl.pallas_call(
        body, name="grad_chip_scatter",
        in_specs=[HBM] * n, out_specs=[HBM] * n,
        out_shape=[jax.ShapeDtypeStruct((N_CHIPS - 1,) + p.shape[1:], p.dtype) for p in parts],
        scratch_shapes=[pltpu.SemaphoreType.DMA((3 * n,)), pltpu.SemaphoreType.DMA((3 * n,))],
        compiler_params=pltpu.CompilerParams(has_side_effects=True),
    )(*parts)


def _pair_join(bufs):
    n = len(bufs)

    def body(*refs):
        ins, outs = refs[:n], refs[n:2 * n]
        send, recv = refs[2 * n:]
        x, y, c = _place()
        sib = (x, y, 1 - c)
        rds = []
        for w in range(n):
            rd = pltpu.make_async_remote_copy(
                src_ref=ins[w].at[c], dst_ref=outs[w].at[c], send_sem=send.at[w], recv_sem=recv.at[w],
                device_id=sib, device_id_type=MESH)
            rd.start()
            rds.append(rd)
        for w in range(n):
            pltpu.make_async_remote_copy(
                src_ref=ins[w].at[c], dst_ref=outs[w].at[1 - c], send_sem=send.at[w], recv_sem=recv.at[w],
                device_id=sib, device_id_type=MESH).wait_recv()
        for rd in rds:
            rd.wait_send()

    return pl.pallas_call(
        body, name="grad_pair_join",
        in_specs=[HBM] * n, out_specs=[HBM] * n,
        out_shape=[jax.ShapeDtypeStruct(b.shape, b.dtype) for b in bufs],
        input_output_aliases={w: w for w in range(n)},
        scratch_shapes=[pltpu.SemaphoreType.DMA((n,)), pltpu.SemaphoreType.DMA((n,))],
        compiler_params=pltpu.CompilerParams(has_side_effects=True),
    )(*bufs)


def _all_reduce_small(packed):
    R = packed.shape[0]

    def body(x_ref, o_ref, buf, send, recv):
        x, y, c = _place()
        me = 4 * x + 2 * y + c
        buf[me] = x_ref[...]
        rds = []
        for rel in range(1, N_DEV):
            fx, fy, fc = (rel >> 2) & 1, (rel >> 1) & 1, rel & 1
            peer = (1 - x if fx else x, 1 - y if fy else y, 1 - c if fc else c)
            rd = pltpu.make_async_remote_copy(
                src_ref=x_ref, dst_ref=buf.at[me], send_sem=send.at[rel - 1], recv_sem=recv.at[rel - 1],
                device_id=peer, device_id_type=MESH)
            rd.start()
            rds.append((rd, peer))
        for rel in range(1, N_DEV):
            rd, (px, py, pc) = rds[rel - 1]
            pltpu.make_async_remote_copy(
                src_ref=x_ref, dst_ref=buf.at[4 * px + 2 * py + pc], send_sem=send.at[rel - 1], recv_sem=recv.at[rel - 1],
                device_id=(px, py, pc), device_id_type=MESH).wait_recv()
        for rd, _ in rds:
            rd.wait_send()
        total = buf[0]
        for d in range(1, N_DEV):
            total = total + buf[d]
        o_ref[...] = total

    return pl.pallas_call(
        body, name="all_reduce_small",
        in_specs=[pl.BlockSpec(memory_space=pltpu.VMEM)],
        out_specs=pl.BlockSpec(memory_space=pltpu.VMEM),
        out_shape=jax.ShapeDtypeStruct(packed.shape, F32),
        scratch_shapes=[pltpu.VMEM((N_DEV, R, LANES), F32), pltpu.SemaphoreType.DMA((N_DEV - 1,)),
                        pltpu.SemaphoreType.DMA((N_DEV - 1,))],
        compiler_params=pltpu.CompilerParams(has_side_effects=True),
    )(packed)


def _pair_sum(name, split, theirs, core):
    n_chip, _, Rh, C = split.shape
    tile = _div_tile(Rh, 256, 2 * SUBLANES)
    nt = Rh // tile

    def body(core_ref, a_ref, b_ref, o_ref):
        o_ref[...] = (a_ref[...] + b_ref[...]).astype(o_ref.dtype)

    return pl.pallas_call(
        body, name=name,
        grid_spec=pltpu.PrefetchScalarGridSpec(
            num_scalar_prefetch=1,
            grid=(n_chip, nt),
            in_specs=[pl.BlockSpec((None, None, tile, C), lambda j, i, core_ref: (j, core_ref[0], i, 0)),
                      pl.BlockSpec((None, tile, C), lambda j, i, core_ref: (j, i, 0))],
            out_specs=pl.BlockSpec((None, tile, C), lambda j, i, core_ref: (j, i, 0)),
        ),
        out_shape=jax.ShapeDtypeStruct((n_chip, Rh, C), BF16),
        compiler_params=_params(("parallel", "parallel")),
    )(core, split, theirs)


def _chip_sum(name, own, landed, core):
    n_in, Rh, C = landed.shape
    tile = _div_tile(Rh, 256, 2 * SUBLANES)

    def body(core_ref, *refs):
        total = refs[0][...].astype(F32)
        for ref in refs[1:n_in + 1]:
            total = total + ref[...].astype(F32)
        refs[n_in + 1][...] = total

    slot = lambda j: pl.BlockSpec((None, tile, C), lambda i, core_ref: (j, i, 0))
    return pl.pallas_call(
        body, name=name,
        grid_spec=pltpu.PrefetchScalarGridSpec(
            num_scalar_prefetch=1,
            grid=(Rh // tile,),
            in_specs=[pl.BlockSpec((None, tile, C), lambda i, core_ref: (core_ref[1], i, 0))]
                     + [slot(j) for j in range(n_in)],
            out_specs=pl.BlockSpec((None, tile, C), lambda i, core_ref: (core_ref[0], i, 0)),
        ),
        out_shape=jax.ShapeDtypeStruct((2, Rh, C), F32),
        compiler_params=_params(("parallel",)),
    )(core, own, *([landed] * n_in))


def _adamw(name, w, g, m, v):
    R, C = w.shape
    tile = _div_tile(R, 256, SUBLANES)
    c1 = 1.0 / (1.0 - ADAM_B1 ** ADAM_STEP)
    c2 = 1.0 / (1.0 - ADAM_B2 ** ADAM_STEP)

    def body(w_ref, g_ref, m_ref, v_ref, d_ref, nm_ref, nv_ref):
        g_ = g_ref[...]
        nm = ADAM_B1 * m_ref[...] + (1.0 - ADAM_B1) * g_
        nv = ADAM_B2 * v_ref[...] + (1.0 - ADAM_B2) * (g_ * g_)
        d_ref[...] = -ADAM_LR * ((nm * c1) / (jnp.sqrt(nv * c2) + ADAM_EPS) + ADAM_WD * w_ref[...])
        nm_ref[...] = nm
        nv_ref[...] = nv

    spec = pl.BlockSpec((tile, C), lambda i: (i, 0))
    return pl.pallas_call(
        body, name=name,
        grid=(R // tile,),
        in_specs=[spec] * 4, out_specs=[spec] * 3,
        out_shape=[jax.ShapeDtypeStruct((R, C), F32)] * 3,
        compiler_params=_params(("parallel",)),
    )(w, g, m, v)


SMALL =["norm_mix_pre", "b_gate", "mu_rw", "w0", "a0", "k_k", "k_a", "r_k", "lnx_w", "lnx_b",
         "norm_mix_post", "norm_ffn_pre", "norm_ffn_post"]
BIG = ["w_in", "w_up", "a_up", "g_up", "w_sb_out", "w_rw_out", "w_o", "w_ffn_gate", "w_ffn_up", "w_ffn_down"]
ROW_SHARDED = ("w_o", "w_ffn_down")
ORDER = ["norm_mix_pre", "w_in", "b_gate", "mu_rw", "w0", "w_up", "a0", "a_up", "g_up", "k_k", "k_a", "r_k",
         "lnx_w", "lnx_b", "w_sb_out", "w_rw_out", "w_o", "norm_mix_post", "norm_ffn_pre", "w_ffn_gate",
         "w_ffn_up", "w_ffn_down", "norm_ffn_post"]


def _pack_small(vals, extra_rows=0):
    rows = jnp.concatenate([vals[n].reshape(-1, LANES) for n in SMALL], axis=0)
    pad = (-(rows.shape[0] + extra_rows)) % SUBLANES + extra_rows
    return jnp.pad(rows, ((0, pad), (0, 0)))


def _unpack_small(packed, shapes):
    out, r = {}, 0
    for n in SMALL:
        size = 1
        for s in shapes[n]:
            size *= s
        out[n] = packed[r:r + size // LANES].reshape(shapes[n])
        r += size // LANES
    return out


def kernel(x, norm_mix_pre, w_in, b_gate, mu_rw, w0, w_up, a0, a_up, g_up, k_k, k_a, r_k, lnx_w, lnx_b, w_sb_out, w_rw_out, w_o, norm_mix_post, norm_ffn_pre, w_ffn_gate, w_ffn_up, w_ffn_down, norm_ffn_post, loss_target, m_norm_mix_pre, m_w_in, m_b_gate, m_mu_rw, m_w0, m_w_up, m_a0, m_a_up, m_g_up, m_k_k, m_k_a, m_r_k, m_lnx_w, m_lnx_b, m_w_sb_out, m_w_rw_out, m_w_o, m_norm_mix_post, m_norm_ffn_pre, m_w_ffn_gate, m_w_ffn_up, m_w_ffn_down, m_norm_ffn_post, v_norm_mix_pre, v_w_in, v_b_gate, v_mu_rw, v_w0, v_w_up, v_a0, v_a_up, v_g_up, v_k_k, v_k_a, v_r_k, v_lnx_w, v_lnx_b, v_w_sb_out, v_w_rw_out, v_w_o, v_norm_mix_post, v_norm_ffn_pre, v_w_ffn_gate, v_w_ffn_up, v_w_ffn_down, v_norm_ffn_post):
    W = dict(norm_mix_pre=norm_mix_pre, w_in=w_in, b_gate=b_gate, mu_rw=mu_rw, w0=w0, w_up=w_up, a0=a0, a_up=a_up,
             g_up=g_up, k_k=k_k, k_a=k_a, r_k=r_k, lnx_w=lnx_w, lnx_b=lnx_b, w_sb_out=w_sb_out, w_rw_out=w_rw_out,
             w_o=w_o, norm_mix_post=norm_mix_post, norm_ffn_pre=norm_ffn_pre, w_ffn_gate=w_ffn_gate,
             w_ffn_up=w_ffn_up, w_ffn_down=w_ffn_down, norm_ffn_post=norm_ffn_post)
    Mo = dict(norm_mix_pre=m_norm_mix_pre, w_in=m_w_in, b_gate=m_b_gate, mu_rw=m_mu_rw, w0=m_w0, w_up=m_w_up, a0=m_a0,
              a_up=m_a_up, g_up=m_g_up, k_k=m_k_k, k_a=m_k_a, r_k=m_r_k, lnx_w=m_lnx_w, lnx_b=m_lnx_b,
              w_sb_out=m_w_sb_out, w_rw_out=m_w_rw_out, w_o=m_w_o, norm_mix_post=m_norm_mix_post,
              norm_ffn_pre=m_norm_ffn_pre, w_ffn_gate=m_w_ffn_gate, w_ffn_up=m_w_ffn_up, w_ffn_down=m_w_ffn_down,
              norm_ffn_post=m_norm_ffn_post)
    Vo = dict(norm_mix_pre=v_norm_mix_pre, w_in=v_w_in, b_gate=v_b_gate, mu_rw=v_mu_rw, w0=v_w0, w_up=v_w_up, a0=v_a0,
              a_up=v_a_up, g_up=v_g_up, k_k=v_k_k, k_a=v_k_a, r_k=v_r_k, lnx_w=v_lnx_w, lnx_b=v_lnx_b,
              w_sb_out=v_w_sb_out, w_rw_out=v_w_rw_out, w_o=v_w_o, norm_mix_post=v_norm_mix_post,
              norm_ffn_pre=v_norm_ffn_pre, w_ffn_gate=v_w_ffn_gate, w_ffn_up=v_w_ffn_up, w_ffn_down=v_w_ffn_down,
              norm_ffn_post=v_norm_ffn_post)
    shapes = {n: W[n].shape for n in ORDER}
    B, S, D = x.shape
    T = B * S
    x2 = x.reshape(T, D)
    tgt = loss_target.reshape(T, D)
    vec = {n: W[n].reshape(1, -1) for n in SMALL}

    work = lambda t, n: t[0] if n in ROW_SHARDED else jnp.swapaxes(t[0], 0, 1)
    halved = [work(W[n], n).astype(BF16) for n in BIG]
    halved = [h.reshape(2, h.shape[0] // 2, h.shape[1]) for h in halved]
    full = {n: gth.reshape(-1, gth.shape[3]) for n, gth in zip(BIG, _all_gather_chips(halved))}
    w_in_t = full["w_in"]
    w_sb_t, w_rw_t, w_gt_t = w_in_t[:SB_COLS], w_in_t[SB_COLS:SB_COLS + RW_COLS], w_in_t[SB_COLS + RW_COLS:]
    lora_rows = {"w_up": 0, "a_up": 64, "g_up": 128}
    lora = {n: jnp.pad(full[n].T, ((r0, LORA_COLS - r0 - full[n].shape[1]), (0, 0))) for n, r0 in lora_rows.items()}
    mu = vec["mu_rw"]
    mu_parts = [mu[:, :512], mu[:, 512:1024], mu[:, 1024:1536], mu[:, 1536:]]
    b1, b2 = vec["b_gate"][:, :D], vec["b_gate"][:, D:]

    (h1,) = _rowwise("norm_mix_pre", _f_norm, [(x2, D, 0)], [vec["norm_mix_pre"]], [(D, BF16)], tile=512)
    p_sb = _mm("proj_sb", h1, w_sb_t, tb=True, out_dtype=BF16)
    p_rw = _mm("proj_rw", h1, w_rw_t, tb=True)
    p_gt = _mm("proj_gate", h1, w_gt_t, tb=True, out_dtype=BF16)
    o_sb = _attn_fwd(p_sb, B, S)
    pre_params = mu_parts + [vec["w0"], lora["w_up"], vec["a0"], lora["a_up"], lora["g_up"], vec["k_k"], vec["k_a"]]
    r_, lw_, k2_, v_, kap_, a_, g_ = _rw_pre(p_rw, pre_params, S, tile=256)
    y_wkv, states = _wkv_fwd(r_, lw_, k2_, v_, kap_, a_, B, S)
    post_rows = [(y_wkv, 512, 0), (r_, 512, 0), (k2_, 512, 0), (v_, 512, 0), (g_, 512, 0)]
    post_params = [vec["lnx_w"], vec["lnx_b"], vec["r_k"]]
    (o_rw,) = _rowwise("rw_post", _f_rwpost, post_rows, post_params, [(512, BF16)], tile=512)
    m1 = _mm("mix_sb_out", o_sb, full["w_sb_out"], tb=True, out_dtype=BF16)
    m2 = _mm("mix_rw_out", o_rw, full["w_rw_out"], tb=True, out_dtype=BF16)
    merge_rows = [(p_gt, D, 0), (p_gt, D, 1), (m1, D, 0), (m2, D, 0)]
    (merged,) = _rowwise("merge", _f_merge, merge_rows, [b1, b2], [(D, BF16)], tile=512)
    u = _mm("mix_out", merged, full["w_o"])
    post1_params = [vec["norm_mix_post"], vec["norm_ffn_pre"]]
    x1, h2 = _rowwise("post_mix", _f_post1, [(x2, D, 0), (u, D, 0)], post1_params, [(D, F32), (D, BF16)], tile=512)
    ag, au, sw = _mm_fused("ffn_in", [h2, h2], [full["w_ffn_gate"], full["w_ffn_up"]], [BF16] * 3, tb=True,
                           epilogue=lambda gu, _: (gu[0], gu[1], _f_swiglu(*gu)[0]))
    f = _mm("ffn_down", sw, full["w_ffn_down"])
    loss_part, dx1, df, dg4 = _loss_head(x1, f, tgt, vec["norm_ffn_post"], tile=512)

    gbig, gsmall = {}, {"norm_ffn_post": dg4}
    gbig["w_ffn_down"] = _mm("g_ffn_down", sw, df, ta=True)

    def swiglu_back(dsw, gu):
        return jax.vjp(_f_swiglu, *gu)[1]((dsw[0],))

    dag, dau = _mm_fused("ffn_back", [df], [full["w_ffn_down"]], [BF16] * 2, tb=True, extras=[ag, au],
                         epilogue=swiglu_back)
    (dh2,) = _mm_fused("d_h2", [dag, dau], [full["w_ffn_gate"], full["w_ffn_up"]], [F32], add=True)
    gbig["w_ffn_gate"] = _mm("g_ffn_gate", dag, h2, ta=True)
    gbig["w_ffn_up"] = _mm("g_ffn_up", dau, h2, ta=True)
    (dx_res, du), (dg2, dg3) = _rowwise_vjp("post_mix_bwd", _f_post1, [(x2, D, 0), (u, D, 0)], post1_params,
                                            [[dx1], [dh2]], [True, True], [True, True], bf16_rows=(1,))
    gsmall["norm_mix_post"], gsmall["norm_ffn_pre"] = dg2, dg3
    dmerged = _mm("d_merged", du, full["w_o"], tb=True, out_dtype=BF16)
    gbig["w_o"] = _mm("g_w_o", merged, du, ta=True)
    (dpg1, dpg2, dm1, dm2), (db1, db2) = _rowwise_vjp("merge_bwd", _f_merge, merge_rows, [b1, b2], [[dmerged]],
                                                      [True] * 4, [True, True], bf16_rows=(0, 1, 2, 3))
    gsmall["b_gate"] = jnp.concatenate([db1, db2], axis=1)
    do_sb = _mm("d_o_sb", dm1, full["w_sb_out"])
    do_rw = _mm("d_o_rw", dm2, full["w_rw_out"])
    gbig["w_sb_out"] = _mm("g_sb_out", dm1, o_sb, ta=True)
    gbig["w_rw_out"] = _mm("g_rw_out", dm2, o_rw, ta=True)
    (dy_wkv, dr_a, dk2_a, dv_a, dg_), (dlnx_w, dlnx_b, dr_k) = _rowwise_vjp(
        "rw_post_bwd", _f_rwpost, post_rows, post_params, [[do_rw]], [True] * 5, [True] * 3)
    gsmall["lnx_w"], gsmall["lnx_b"], gsmall["r_k"] = dlnx_w, dlnx_b, dr_k
    dr_b, dlw, dk2_b, dv_b, dkap, da = _wkv_bwd(r_, lw_, k2_, v_, kap_, a_, states, dy_wkv, B, S)
    pre_cts = [[dr_a, dr_b], [dlw], [dk2_a, dk2_b], [dv_a, dv_b], [dkap], [da], [dg_]]
    dp_rw, dpre_params = _rw_pre_bwd(p_rw, pre_params, pre_cts, S)
    gsmall["mu_rw"] = jnp.concatenate(dpre_params[:4], axis=1)
    gsmall["w0"], gsmall["a0"], gsmall["k_k"], gsmall["k_a"] = dpre_params[4], dpre_params[6], dpre_params[9], dpre_params[10]
    glora = {"w_up": dpre_params[5][0:64].T, "a_up": dpre_params[7][64:128].T, "g_up": dpre_params[8][128:256].T}
    dq, dk, dv = _attn_bwd(p_sb, o_sb, do_sb, B, S)
    (dh1,) = _mm_fused("d_h1_sb", [dq, dk, dv], [w_sb_t[:512], w_sb_t[512:1024], w_sb_t[1024:]], [F32], add=True)
    dh1 = _mm("d_h1_rw", dp_rw, w_rw_t, acc=dh1)
    (dh1,) = _mm_fused("d_h1_gate", [dpg1, dpg2], [w_gt_t[:D], w_gt_t[D:]], [F32], add=True,
                       extras=[dh1], epilogue=lambda p, e: (p[0] + e[0],))
    gbig["w_in"] = jnp.concatenate(
        [_mm("g_in_" + tag, d, h1, ta=True)
         for tag, d in (("q", dq), ("k", dk), ("v", dv), ("rw", dp_rw), ("g1", dpg1), ("g2", dpg2))], axis=0)
    (grad_x2,), (dg1,) = _rowwise_vjp("norm_mix_pre_bwd", _f_norm, [(x2, D, 0)], [vec["norm_mix_pre"]], [[dh1]],
                                      [True], [True], add_to={0: dx_res}, tile=512)
    gsmall["norm_mix_pre"] = dg1
    gbig.update(glora)

    split = [gbig[n].reshape(N_CHIPS, 2, gbig[n].shape[0] // (2 * N_CHIPS), gbig[n].shape[1]) for n in BIG]
    core = jnp.stack([lax.axis_index("c"), 2 * lax.axis_index("x") + lax.axis_index("y")]).astype(jnp.int32)
    theirs = _pair_split(split)
    chip_sums = [_pair_sum("pair_sum_" + n, a, b, core) for n, a, b in zip(BIG, split, theirs)]
    landed = _chip_scatter(chip_sums)
    joined = _pair_join([_chip_sum("chip_sum_" + n, own, got, core) for n, own, got in zip(BIG, chip_sums, landed)])
    grads = {n: j.reshape(-1, j.shape[2]) for n, j in zip(BIG, joined)}

    small_local = _pack_small({n: gsmall[n] for n in SMALL}, extra_rows=1)
    loss_row = small_local.shape[0] - 1
    small_local = small_local.at[loss_row].set(loss_part[0])
    small_sum = _all_reduce_small(small_local)
    loss = small_sum[loss_row, 0]

    delta, new_m, new_v = {}, {}, {}
    unwork = lambda t, n: (t if n in ROW_SHARDED else jnp.swapaxes(t, 0, 1))[None]
    for n in BIG:
        d_, m_, v2_ = _adamw("adamw_" + n, work(W[n], n), grads[n], work(Mo[n], n), work(Vo[n], n))
        delta[n], new_m[n], new_v[n], grads[n] = (unwork(t, n) for t in (d_, m_, v2_, grads[n]))
    pk = lambda src: _pack_small({n: src[n] for n in SMALL}, extra_rows=1)
    d_s, m_s, v_s = _adamw("adamw_small", pk(W), small_sum.at[loss_row].set(0.0), pk(Mo), pk(Vo))
    for dst, packed in ((grads, small_sum), (delta, d_s), (new_m, m_s), (new_v, v_s)):
        dst.update(_unpack_small(packed, shapes))

    return (loss, grad_x2.reshape(B, S, D), *[grads[n] for n in ORDER], *[delta[n] for n in ORDER],
            *[new_m[n] for n in ORDER], *[new_v[n] for n in ORDER])
```

```python
import functools

import jax
import jax.numpy as jnp
from jax import lax
from jax.experimental import pallas as pl
from jax.experimental.pallas import tpu as pltpu

F32 = jnp.float32
BF16 = jnp.bfloat16
MESH = pl.DeviceIdType.MESH

D_MODEL = 1024
SB_HEADS = 8
HEAD_DIM = 64
SB_WIDTH = SB_HEADS * HEAD_DIM
RW_WIDTH = 512
LORA_COLS = 256
SB_COLS = 3 * SB_WIDTH
RW_COLS = 3 * RW_WIDTH + LORA_COLS
GATE_COLS = 2 * D_MODEL
D_FF = 2816
RMS_EPS = 1e-6
GN_EPS = HEAD_DIM * 1e-5
WKV_CHUNK = 64
WKV_SEQS = 4
ATTN_QUERIES = 512
ATTN_KEYS = 128
ATTN_DEAD = -120.0
LANES = 128
SUBLANES = 8
N_CHIPS = 4
N_DEV = 8

ADAM_LR = 0.001
ADAM_B1 = 0.9
ADAM_B2 = 0.999
ADAM_EPS = 1e-08
ADAM_WD = 0.01
ADAM_STEP = 10

VMEM_LIMIT = 48 * 1024 * 1024
WKV_BWD_VMEM = 58 * 1024 * 1024


def _params(sem=None, vmem=VMEM_LIMIT, **kw):
    if sem is not None:
        kw["dimension_semantics"] = sem
    return pltpu.CompilerParams(vmem_limit_bytes=vmem, **kw)


def _div_tile(dim, pref, mult=LANES):
    if dim <= pref:
        return dim
    t = pref - pref % mult
    while t >= mult:
        if dim % t == 0:
            return t
        t -= mult
    return dim


def _dot(a, b, dims):
    return lax.dot_general(a, b, (dims, ((), ())), preferred_element_type=F32)


def _mm(name, a, b, *, ta=False, tb=False, acc=None, out_dtype=F32):
    if ta:
        K, M = a.shape
    else:
        M, K = a.shape
    N = b.shape[0] if tb else b.shape[1]
    if ta:
        tm, tn, tk = _div_tile(M, 1408), _div_tile(N, 1408), _div_tile(K, 512)
    else:
        tm, tn, tk = _div_tile(M, 512), _div_tile(N, 1408), _div_tile(K, 1408)
    nk = K // tk
    dims = ((0,) if ta else (1,), (1,) if tb else (0,))
    has_acc = acc is not None

    def body(*refs):
        a_ref, b_ref = refs[0], refs[1]
        part = _dot(a_ref[...].astype(BF16), b_ref[...].astype(BF16), dims)
        if nk == 1:
            o_ref = refs[-1]
            o_ref[...] = (part + refs[2][...] if has_acc else part).astype(o_ref.dtype)
            return
        o_ref, scr = refs[-2], refs[-1]
        k = pl.program_id(2)

        @pl.when(k == 0)
        def _():
            scr[...] = part + refs[2][...] if has_acc else part

        @pl.when(k > 0)
        def _():
            scr[...] += part

        @pl.when(k == nk - 1)
        def _():
            o_ref[...] = scr[...].astype(o_ref.dtype)

    a_spec = pl.BlockSpec((tk, tm), lambda i, j, k: (k, i)) if ta else pl.BlockSpec((tm, tk), lambda i, j, k: (i, k))
    b_spec = pl.BlockSpec((tn, tk), lambda i, j, k: (j, k)) if tb else pl.BlockSpec((tk, tn), lambda i, j, k: (k, j))
    o_spec = pl.BlockSpec((tm, tn), lambda i, j, k: (i, j))
    return pl.pallas_call(
        body, name=name,
        grid=(M // tm, N // tn, nk),
        in_specs=[a_spec, b_spec] + ([o_spec] if has_acc else []),
        out_specs=o_spec,
        out_shape=jax.ShapeDtypeStruct((M, N), out_dtype),
        scratch_shapes=[pltpu.VMEM((tm, tn), F32)] if nk > 1 else [],
        compiler_params=_params(("parallel", "parallel", "arbitrary")),
    )(*([a, b] + ([acc] if has_acc else [])))


def _mm_fused(name, lhs, rhs, outs, *, tb=False, add=False, extras=(), epilogue=None):
    M, K = lhs[0].shape
    N = rhs[0].shape[0] if tb else rhs[0].shape[1]
    tm, tn, tk = _div_tile(M, 512), _div_tile(N, 1408), _div_tile(K, 1408)
    nk = K // tk
    n_l, n_e, n_o = len(lhs), len(extras), len(outs)
    n_acc = 1 if add else n_l
    dims = ((1,), (1,) if tb else (0,))

    def body(*refs):
        l_refs, r_refs = refs[:n_l], refs[n_l:2 * n_l]
        e_refs = refs[2 * n_l:2 * n_l + n_e]
        o_refs = refs[2 * n_l + n_e:2 * n_l + n_e + n_o]
        scr = refs[2 * n_l + n_e + n_o:]
        parts = [_dot(l[...].astype(BF16), r[...].astype(BF16), dims) for l, r in zip(l_refs, r_refs)]
        if add:
            parts = [functools.reduce(lambda u, v: u + v, parts)]

        def finish(vals):
            res = epilogue(vals, [e[...].astype(F32) for e in e_refs]) if epilogue else vals
            for ref, val in zip(o_refs, res):
                ref[...] = val.astype(ref.dtype)

        if nk == 1:
            finish(parts)
            return
        k = pl.program_id(2)

        @pl.when(k == 0)
        def _():
            for s, part in zip(scr, parts):
                s[...] = part

        @pl.when(k > 0)
        def _():
            for s, part in zip(scr, parts):
                s[...] += part

        @pl.when(k == nk - 1)
        def _():
            finish([s[...] for s in scr])

    a_spec = pl.BlockSpec((tm, tk), lambda i, j, k: (i, k))
    b_spec = pl.BlockSpec((tn, tk), lambda i, j, k: (j, k)) if tb else pl.BlockSpec((tk, tn), lambda i, j, k: (k, j))
    o_spec = pl.BlockSpec((tm, tn), lambda i, j, k: (i, j))
    return pl.pallas_call(
        body, name=name,
        grid=(M // tm, N // tn, nk),
        in_specs=[a_spec] * n_l + [b_spec] * n_l + [o_spec] * n_e,
        out_specs=[o_spec] * n_o,
        out_shape=[jax.ShapeDtypeStruct((M, N), dt) for dt in outs],
        scratch_shapes=[pltpu.VMEM((tm, tn), F32)] * (n_acc if nk > 1 else 0),
        compiler_params=_params(("parallel", "parallel", "arbitrary")),
    )(*lhs, *rhs, *extras)


def _row_spec(tile, width, colblk):
    return pl.BlockSpec((tile, width), lambda i: (i, colblk))


def _full_spec(shape):
    return pl.BlockSpec(shape, lambda i: (0,) * len(shape))


def _rowwise(name, fn, rows, params, outs, tile=256):
    T = rows[0][0].shape[0]
    tile = min(tile, T)
    n_r, n_p = len(rows), len(params)

    def body(*refs):
        r = [x[...].astype(F32) for x in refs[:n_r]]
        p = [x[...].astype(F32) for x in refs[n_r:n_r + n_p]]
        for o_ref, val in zip(refs[n_r + n_p:], fn(*r, *p)):
            o_ref[...] = val.astype(o_ref.dtype)

    return pl.pallas_call(
        body, name=name,
        grid=(T // tile,),
        in_specs=[_row_spec(tile, w, cb) for _, w, cb in rows] + [_full_spec(p.shape) for p in params],
        out_specs=[_row_spec(tile, w, 0) for w, _ in outs],
        out_shape=[jax.ShapeDtypeStruct((T, w), dt) for w, dt in outs],
        compiler_params=_params(("parallel",)),
    )(*([a for a, _, _ in rows] + list(params)))


def _rowwise_vjp(name, fn, rows, params, cts, need_rows, need_params, add_to=None, tile=256, bf16_rows=()):
    add_to = add_to or {}
    T = rows[0][0].shape[0]
    tile = min(tile, T)
    n_r, n_p = len(rows), len(params)
    ct_flat = [c for group in cts for c in group]
    ct_sizes = [len(group) for group in cts]
    add_idx = sorted(add_to)
    row_out = [i for i in range(n_r) if need_rows[i]]
    par_out = [i for i in range(n_p) if need_params[i]]
    n_ct, n_add = len(ct_flat), len(add_idx)

    def body(*refs):
        pos = 0
        r = [x[...].astype(F32) for x in refs[pos:pos + n_r]]
        pos += n_r
        p = [x[...].astype(F32) for x in refs[pos:pos + n_p]]
        pos += n_p
        ct_vals = [x[...].astype(F32) for x in refs[pos:pos + n_ct]]
        pos += n_ct
        adds = {i: x[...] for i, x in zip(add_idx, refs[pos:pos + n_add])}
        pos += n_add
        drow_refs = refs[pos:pos + len(row_out)]
        pos += len(row_out)
        dpar_refs = refs[pos:pos + len(par_out)]
        ct_in, q = [], 0
        for n in ct_sizes:
            ct_in.append(functools.reduce(lambda u, v: u + v, ct_vals[q:q + n]))
            q += n
        _, vjp = jax.vjp(fn, *r, *p)
        grads = vjp(tuple(ct_in))
        for ref, i in zip(drow_refs, row_out):
            g = grads[i]
            ref[...] = (g + adds[i] if i in adds else g).astype(ref.dtype)

        @pl.when(pl.program_id(0) == 0)
        def _():
            for ref in dpar_refs:
                ref[...] = jnp.zeros_like(ref)

        for ref, i in zip(dpar_refs, par_out):
            ref[...] += grads[n_r + i]

    ct_widths = [c.shape[1] for c in ct_flat]
    in_specs = ([_row_spec(tile, w, cb) for _, w, cb in rows] + [_full_spec(p.shape) for p in params]
                + [_row_spec(tile, w, 0) for w in ct_widths] + [_row_spec(tile, rows[i][1], 0) for i in add_idx])
    out_specs = [_row_spec(tile, rows[i][1], 0) for i in row_out] + [_full_spec(params[i].shape) for i in par_out]
    out_shape = ([jax.ShapeDtypeStruct((T, rows[i][1]), BF16 if i in bf16_rows else F32) for i in row_out]
                 + [jax.ShapeDtypeStruct(params[i].shape, F32) for i in par_out])
    res = pl.pallas_call(
        body, name=name,
        grid=(T // tile,),
        in_specs=in_specs, out_specs=out_specs, out_shape=out_shape,
        compiler_params=_params(("arbitrary",)),
    )(*([a for a, _, _ in rows] + list(params) + ct_flat + [add_to[i] for i in add_idx]))
    return res[:len(row_out)], res[len(row_out):]


def _sigmoid(x):
    return 0.5 * (jnp.tanh(0.5 * x) + 1.0)


def _softplus(x):
    return jnp.maximum(x, 0.0) + jnp.log(1.0 + jnp.exp(-jnp.abs(x)))


def _rms(x, g):
    return x * lax.rsqrt(jnp.mean(x * x, axis=-1, keepdims=True) + RMS_EPS) * g


def _segsum_impl(x):
    n, w = x.shape[-1], 2 * LANES
    r = lax.shift_right_logical(lax.broadcasted_iota(jnp.int32, (w, w), 0), 6)
    c = lax.shift_right_logical(lax.broadcasted_iota(jnp.int32, (w, w), 1), 6)
    bd = (r == c).astype(BF16)
    hi = x.astype(BF16)
    lo = (x - hi.astype(F32)).astype(BF16)
    nn = ((1,), (0,))
    blocks = [_dot(hi[:, j:j + w], bd, nn) + _dot(lo[:, j:j + w], bd, nn) for j in range(0, n, w)]
    return jnp.concatenate(blocks, axis=1)


@jax.custom_vjp
def _segsum(x):
    return _segsum_impl(x)


_segsum.defvjp(lambda x: (_segsum_impl(x), None), lambda _, g: (_segsum_impl(g),))


@jax.custom_vjp
def _mmb(a, w):
    return _dot(a.astype(BF16), w.astype(BF16), ((1,), (0,)))


def _mmb_fwd(a, w):
    return _mmb(a, w), (a, w)


def _mmb_bwd(res, g):
    a, w = res
    gb = g.astype(BF16)
    return _dot(gb, w.astype(BF16), ((1,), (1,))), _dot(a.astype(BF16), gb, ((0,), (0,)))


_mmb.defvjp(_mmb_fwd, _mmb_bwd)


def _f_norm(x, g):
    return (_rms(x, g),)


def _f_post1(x, u, g2, g3):
    x1 = x + _rms(u, g2)
    return x1, _rms(x1, g3)


def _f_swiglu(ag, au):
    return (ag * _sigmoid(ag) * au,)


def _f_merge(pg1, pg2, m1, m2, b1, b2):
    return (_sigmoid(pg1 + b1) * m1 + _sigmoid(pg2 + b2) * m2,)


def _f_out(x1, f, g4):
    return (x1 + _rms(f, g4),)


def _f_rwpre(pr, pk, pv, pz, qr, qk, qv, qz, mur, muk, muv, muz, w0, wup, a0, aup, gup, k_k, k_a):
    r = pr + (qr - pr) * mur
    k = pk + (qk - pk) * muk
    v = pv + (qv - pv) * muv
    z = pz + (qz - pz) * muz
    w_raw = w0 + _mmb(jnp.tanh(z), wup)
    lw = -jnp.exp(-_softplus(-w_raw) - 0.5)
    a = _sigmoid(a0 + _mmb(z, aup))
    g = _mmb(_sigmoid(z), gup)
    kk = k * k_k
    kap = kk * lax.rsqrt(jnp.maximum(_segsum(kk * kk), 1e-24))
    k2 = k * (1.0 + (a - 1.0) * k_a)
    return r, lw, k2, v, kap, a, g


def _f_rwpost(y, r, k2, v, g, lnx_w, lnx_b, r_k):
    inv = 1.0 / HEAD_DIM
    yc = y - _segsum(y) * inv
    var = _segsum(yc * yc) * inv
    yn = yc * lax.rsqrt(var + GN_EPS) * lnx_w + lnx_b
    bonus = _segsum(r * k2 * r_k) * v
    return ((yn + bonus) * g,)


RW_GROUPS = (0, 512, 1024, 1536, RW_COLS)


def _column_groups(p):
    return [p[:, a:b] for a, b in zip(RW_GROUPS[:-1], RW_GROUPS[1:])]


def _previous_tokens(p, halo, first_of_sequence):
    rows = lax.broadcasted_iota(jnp.int32, (p.shape[0], 1), 0)
    before = jnp.where(first_of_sequence, 0.0, halo[SUBLANES - 1:SUBLANES, :])
    return jnp.where(rows == 0, before, pltpu.roll(p, 1, axis=0))


def _halo_spec(tile, order):
    per = tile // SUBLANES
    return pl.BlockSpec((SUBLANES, RW_COLS), lambda i: (jnp.maximum(order(i) * per - 1, 0), 0))


def _rw_pre(p_rw, params, S, tile=128):
    T = p_rw.shape[0]
    tile = min(tile, T)
    assert S % tile == 0
    n_p = len(params)

    def body(*refs):
        p_ref, halo_ref = refs[0], refs[1]
        par = [x[...].astype(F32) for x in refs[2:2 + n_p]]
        p = p_ref[...]
        first = lax.rem(pl.program_id(0) * tile, S) == 0
        prev = _previous_tokens(p, halo_ref[...], first)
        for o_ref, val in zip(refs[2 + n_p:], _f_rwpre(*_column_groups(p), *_column_groups(prev), *par)):
            o_ref[...] = val

    out_spec = pl.BlockSpec((tile, RW_WIDTH), lambda i: (i, 0))
    return pl.pallas_call(
        body, name="rw_pre",
        grid=(T // tile,),
        in_specs=[pl.BlockSpec((tile, RW_COLS), lambda i: (i, 0)), _halo_spec(tile, lambda i: i)]
                 + [_full_spec(q.shape) for q in params],
        out_specs=[out_spec] * 7,
        out_shape=[jax.ShapeDtypeStruct((T, RW_WIDTH), F32)] * 7,
        compiler_params=_params(("parallel",)),
    )(p_rw, p_rw, *params)


def _rw_pre_bwd(p_rw, params, cts, S, tile=128):
    T = p_rw.shape[0]
    tile = min(tile, T)
    assert S % tile == 0
    nt = T // tile
    n_p = len(params)
    ct_flat = [c for group in cts for c in group]
    ct_sizes = [len(group) for group in cts]
    n_ct = len(ct_flat)

    def body(*refs):
        p_ref, halo_ref = refs[0], refs[1]
        par = [x[...].astype(F32) for x in refs[2:2 + n_p]]
        ct_vals = [x[...] for x in refs[2 + n_p:2 + n_p + n_ct]]
        dp_ref = refs[2 + n_p + n_ct]
        dpar_refs = refs[3 + n_p + n_ct:3 + 2 * n_p + n_ct]
        carry = refs[-1]
        step = pl.program_id(0)

        @pl.when(step == 0)
        def _():
            carry[...] = jnp.zeros_like(carry)
            for ref in dpar_refs:
                ref[...] = jnp.zeros_like(ref)

        ct_in, q = [], 0
        for n in ct_sizes:
            ct_in.append(functools.reduce(lambda u, v: u + v, ct_vals[q:q + n]))
            q += n
        p = p_ref[...]
        first = lax.rem((nt - 1 - step) * tile, S) == 0
        prev = _previous_tokens(p, halo_ref[...], first)
        _, vjp = jax.vjp(_f_rwpre, *_column_groups(p), *_column_groups(prev), *par)
        grads = vjp(tuple(ct_in))
        d_here = jnp.concatenate(grads[0:4], axis=1)
        d_prev = jnp.concatenate(grads[4:8], axis=1)
        rows = lax.broadcasted_iota(jnp.int32, (tile, 1), 0)
        from_next = jnp.where(rows == tile - 1, carry[0:1, :], pltpu.roll(d_prev, tile - 1, axis=0))
        dp_ref[...] = (d_here + from_next).astype(dp_ref.dtype)
        carry[...] = jnp.broadcast_to(jnp.where(first, 0.0, d_prev[0:1, :]), carry.shape)
        for ref, g in zip(dpar_refs, grads[8:]):
            ref[...] += g

    back = lambda i: nt - 1 - i
    row = lambda w: pl.BlockSpec((tile, w), lambda i: (back(i), 0))
    res = pl.pallas_call(
        body, name="rw_pre_bwd",
        grid=(nt,),
        in_specs=[row(RW_COLS), _halo_spec(tile, back)] + [_full_spec(q.shape) for q in params]
                 + [row(RW_WIDTH)] * n_ct,
        out_specs=[row(RW_COLS)] + [_full_spec(q.shape) for q in params],
        out_shape=[jax.ShapeDtypeStruct((T, RW_COLS), BF16)] + [jax.ShapeDtypeStruct(q.shape, F32) for q in params],
        scratch_shapes=[pltpu.VMEM((SUBLANES, RW_COLS), F32)],
        compiler_params=_params(("arbitrary",)),
    )(p_rw, p_rw, *params, *ct_flat)
    return res[0], res[1:]


def _loss_head(x1, f, target, g4, tile=256):
    T, D = x1.shape
    tile = min(tile, T)

    def body(x1_ref, f_ref, t_ref, g_ref, loss_ref, dx1_ref, df_ref, dg_ref):
        (y,), vjp = jax.vjp(_f_out, x1_ref[...], f_ref[...], g_ref[...])
        err = y - t_ref[...]
        dx1, df, dg = vjp((err * (1.0 / D),))
        dx1_ref[...] = dx1
        df_ref[...] = df.astype(df_ref.dtype)

        @pl.when(pl.program_id(0) == 0)
        def _():
            loss_ref[...] = jnp.zeros_like(loss_ref)
            dg_ref[...] = jnp.zeros_like(dg_ref)

        part = jnp.sum(jnp.sum(err * err, axis=1, keepdims=True), axis=0, keepdims=True) * (0.5 / D)
        loss_ref[...] += jnp.broadcast_to(part, loss_ref.shape)
        dg_ref[...] += dg

    row = pl.BlockSpec((tile, D), lambda i: (i, 0))
    return pl.pallas_call(
        body, name="loss_head",
        grid=(T // tile,),
        in_specs=[row, row, row, _full_spec(g4.shape)],
        out_specs=[_full_spec((SUBLANES, LANES)), row, row, _full_spec(g4.shape)],
        out_shape=[jax.ShapeDtypeStruct((SUBLANES, LANES), F32), jax.ShapeDtypeStruct((T, D), F32),
                   jax.ShapeDtypeStruct((T, D), BF16), jax.ShapeDtypeStruct(g4.shape, F32)],
        compiler_params=_params(("arbitrary",)),
    )(x1, f, target, g4)


def _nn(a, b):
    return _dot(a, b, ((1,), (0,)))


def _nt(a, b):
    return _dot(a, b, ((1,), (1,)))


def _tn(a, b):
    return _dot(a, b, ((0,), (0,)))


def _split_dot(x, u2):
    hi = x.astype(BF16)
    lo = (x - hi.astype(F32)).astype(BF16)
    return _nn(jnp.concatenate([hi, lo], axis=1), u2)


def _by_head(x, masks):
    return jnp.concatenate([(x * m).astype(BF16) for m in masks], axis=0)


def _fold_heads(x2, masks):
    R = x2.shape[0] // len(masks)
    return functools.reduce(lambda u, v: u + v, [x2[h * R:(h + 1) * R] * m for h, m in enumerate(masks)])


def _head_masks():
    lane = lax.broadcasted_iota(jnp.int32, (1, LANES), 1)
    return [((lane >= h * HEAD_DIM) & (lane < (h + 1) * HEAD_DIM)).astype(F32) for h in range(LANES // HEAD_DIM)]


def _key_tri(op):
    row = lax.broadcasted_iota(jnp.int32, (ATTN_KEYS, ATTN_KEYS), 0)
    col = lax.broadcasted_iota(jnp.int32, (ATTN_KEYS, ATTN_KEYS), 1)
    u = op(row, col).astype(BF16)
    return jnp.concatenate([u, u], axis=0)


def _causal(rows):
    row = lax.broadcasted_iota(jnp.int32, (rows, ATTN_KEYS), 0)
    col = lax.broadcasted_iota(jnp.int32, (rows, ATTN_KEYS), 1)
    return col < row


def _from_row(tree, r):
    return jax.tree.map(lambda x: x[r:], tree)


def _onto_rows(old, new, r):
    return jax.tree.map(lambda o, n: jnp.concatenate([o[:r], n], axis=0) if r else n, old, new)


def _sb_weights(qb16, kbh, c_fails, u_gt, strict):
    z_all = _nt(qb16, kbh)
    zs = [z_all[:, h * ATTN_KEYS:(h + 1) * ATTN_KEYS] for h in range(len(c_fails))]
    Ls = [jnp.minimum(-z, 0.0) - jnp.log(1.0 + jnp.exp(-jnp.abs(z))) for z in zs]
    Lms = Ls if strict is None else [jnp.where(strict, L, 0.0) for L in Ls]
    cums = [_split_dot(Lm, u_gt) for Lm in Lms]
    As = [jnp.exp(z + L + c + cum) for z, L, c, cum in zip(zs, Ls, c_fails, cums)]
    if strict is not None:
        As = [jnp.where(strict, A, 0.0) for A in As]
    return zs, Ls, Lms, As


def _attn_specs(S, qb):
    nq = S // qb
    q_spec = pl.BlockSpec((qb, LANES), lambda b, p, i: (b * nq + i, p))
    k_spec = pl.BlockSpec((S, LANES), lambda b, p, i: (b, SB_WIDTH // LANES + p))
    v_spec = pl.BlockSpec((S, LANES), lambda b, p, i: (b, 2 * SB_WIDTH // LANES + p))
    seq = pl.BlockSpec((S, LANES), lambda b, p, i: (b, p))
    return q_spec, k_spec, v_spec, q_spec, seq


def _key_walk(i, qb, block, carry, fails):
    per = qb // ATTN_KEYS
    for sub in reversed(range(per)):
        carry = block(i * per + sub, carry, sub * ATTN_KEYS)
    n = i * per

    def alive(c):
        return jnp.max(functools.reduce(jnp.maximum, fails(c))) > ATTN_DEAD

    def cond(state):
        return jnp.logical_and(state[0] < n, state[1])

    def body(state):
        c = block(n - 1 - state[0], state[2], None)
        return state[0] + 1, alive(c), c

    return lax.while_loop(cond, body, (jnp.int32(0), alive(carry), carry))[2]


def _attn_fwd(proj, B, S):
    qb = min(ATTN_QUERIES, S)
    scale = HEAD_DIM ** -0.5

    def body(q_ref, k_ref, v_ref, o_ref, q16):
        i = pl.program_id(2)
        masks = _head_masks()
        u_gt = _key_tri(lambda r, c: r > c)
        q16[...] = (q_ref[...] * scale).astype(BF16)

        def block(J, carry, row0):
            r0 = pl.multiple_of(J * ATTN_KEYS, ATTN_KEYS)
            kbh = _by_head(k_ref[pl.ds(r0, ATTN_KEYS), :], masks)
            vbh = _by_head(v_ref[pl.ds(r0, ATTN_KEYS), :], masks)
            lo = row0 or 0
            strict = None if row0 is None else _causal(qb - lo)
            acc, cs = _from_row(carry, lo)
            _, _, Lms, As = _sb_weights(q16[lo:], kbh, cs, u_gt, strict)
            acc = acc + _nn(jnp.concatenate([A.astype(BF16) for A in As], axis=1), vbh)
            cs = tuple(c + jnp.sum(Lm, axis=1, keepdims=True) for c, Lm in zip(cs, Lms))
            return _onto_rows(carry, (acc, cs), lo)

        zero_c = tuple(jnp.zeros((qb, 1), F32) for _ in masks)
        carry = _key_walk(i, qb, block, (jnp.zeros((qb, LANES), F32), zero_c), lambda c: c[1])
        o_ref[...] = carry[0]

    q_spec, k_spec, v_spec, blk, _ = _attn_specs(S, qb)
    return pl.pallas_call(
        body, name="sb_attn_fwd",
        grid=(B, SB_WIDTH // LANES, S // qb),
        in_specs=[q_spec, k_spec, v_spec],
        out_specs=blk,
        out_shape=jax.ShapeDtypeStruct((B * S, SB_WIDTH), F32),
        scratch_shapes=[pltpu.VMEM((qb, LANES), BF16)],
        compiler_params=_params(("parallel", "parallel", "arbitrary")),
    )(proj, proj, proj)


def _attn_bwd(proj, o, do, B, S):
    qb = min(ATTN_QUERIES, S)
    nq = S // qb
    scale = HEAD_DIM ** -0.5

    def body(q_ref, k_ref, v_ref, o_ref, do_ref, dq_ref, dk_out, dv_out, dk_ref, dv_ref, q16, do16):
        i = pl.program_id(2)

        @pl.when(i == 0)
        def _():
            dk_ref[...] = jnp.zeros_like(dk_ref)
            dv_ref[...] = jnp.zeros_like(dv_ref)

        masks = _head_masks()
        u_gt = _key_tri(lambda r, c: r > c)
        u_ge = _key_tri(lambda r, c: r >= c)
        heads = range(len(masks))
        q16[...] = (q_ref[...] * scale).astype(BF16)
        do16[...] = do_ref[...].astype(BF16)
        od = o_ref[...] * do16[...].astype(F32)
        totals = tuple(jnp.sum(od * m, axis=1, keepdims=True) for m in masks)

        def block(J, carry, row0):
            r0 = pl.multiple_of(J * ATTN_KEYS, ATTN_KEYS)
            kbh = _by_head(k_ref[pl.ds(r0, ATTN_KEYS), :], masks)
            vbh = _by_head(v_ref[pl.ds(r0, ATTN_KEYS), :], masks)
            lo = row0 or 0
            strict = None if row0 is None else _causal(qb - lo)
            dq, c_fail, c_p = _from_row(carry, lo)
            tot = _from_row(totals, lo)
            zs, Ls, Lms, As = _sb_weights(q16[lo:], kbh, c_fail, u_gt, strict)
            Abs = [A.astype(BF16) for A in As]
            dA_all = _nt(do16[lo:], vbh)
            Ps = [Abs[h].astype(F32) * dA_all[:, h * ATTN_KEYS:(h + 1) * ATTN_KEYS] for h in heads]
            afters = [c_p[h] + _split_dot(Ps[h], u_ge) for h in heads]
            sigs = [jnp.exp(zs[h] + Ls[h]) for h in heads]
            dzs = [Ps[h] * (1.0 - sigs[h]) - sigs[h] * (tot[h] - afters[h]) for h in heads]
            if strict is not None:
                dzs = [jnp.where(strict, dz, 0.0) for dz in dzs]
            dz_all = jnp.concatenate([dz.astype(BF16) for dz in dzs], axis=1)
            dv_ref[pl.ds(r0, ATTN_KEYS), :] += _fold_heads(_tn(jnp.concatenate(Abs, axis=1), do16[lo:]), masks)
            dk_ref[pl.ds(r0, ATTN_KEYS), :] += _fold_heads(_tn(dz_all, q16[lo:]), masks)
            dq = dq + _nn(dz_all, kbh)
            c_fail = tuple(c_fail[h] + jnp.sum(Lms[h], axis=1, keepdims=True) for h in heads)
            c_p = tuple(c_p[h] + jnp.sum(Ps[h], axis=1, keepdims=True) for h in heads)
            return _onto_rows(carry, (dq, c_fail, c_p), lo)

        zc = tuple(jnp.zeros((qb, 1), F32) for _ in masks)
        carry = _key_walk(i, qb, block, (jnp.zeros((qb, LANES), F32), zc, zc), lambda c: c[1])
        dq_ref[...] = (carry[0] * scale).astype(dq_ref.dtype)

        @pl.when(i == nq - 1)
        def _():
            dk_out[...] = dk_ref[...].astype(dk_out.dtype)
            dv_out[...] = dv_ref[...].astype(dv_out.dtype)

    q_spec, k_spec, v_spec, blk, seq = _attn_specs(S, qb)
    return pl.pallas_call(
        body, name="sb_attn_bwd",
        grid=(B, SB_WIDTH // LANES, nq),
        in_specs=[q_spec, k_spec, v_spec, blk, blk],
        out_specs=[blk, seq, seq],
        out_shape=[jax.ShapeDtypeStruct((B * S, SB_WIDTH), BF16)] * 3,
        scratch_shapes=[pltpu.VMEM((S, LANES), F32), pltpu.VMEM((S, LANES), F32),
                        pltpu.VMEM((qb, LANES), BF16), pltpu.VMEM((qb, LANES), BF16)],
        compiler_params=_params(("parallel", "parallel", "arbitrary")),
    )(proj, proj, proj, o, do)


_BATCHED = {"nn": "gmk,gkn->gmn", "nt": "gmk,gnk->gmn", "tn": "gkm,gkn->gmn"}


def _bdot_raw(a, b, kind, passes):
    e = functools.partial(jnp.einsum, _BATCHED[kind], preferred_element_type=F32)
    ah, bh = a.astype(BF16), b.astype(BF16)
    if passes == 1:
        return e(ah, bh)
    al, bl = (a - ah.astype(F32)).astype(BF16), (b - bh.astype(F32)).astype(BF16)
    return e(ah, bh) + e(ah, bl) + e(al, bh)


def _cumsum_rows(x, kind):
    G, C, _ = x.shape
    row = lax.broadcasted_iota(jnp.int32, (C, C), 0)
    col = lax.broadcasted_iota(jnp.int32, (C, C), 1)
    tri = jnp.broadcast_to((col <= row).astype(BF16), (G, C, C))
    e = functools.partial(jnp.einsum, _BATCHED[kind], preferred_element_type=F32)
    hi = x.astype(BF16)
    return e(tri, hi) + e(tri, (x - hi.astype(F32)).astype(BF16))


@jax.custom_vjp
def _running_sum(x):
    return _cumsum_rows(x, "nn")


_running_sum.defvjp(lambda x: (_cumsum_rows(x, "nn"), None), lambda _, g: (_cumsum_rows(g, "tn"),))


@functools.partial(jax.custom_vjp, nondiff_argnums=(2, 3))
def _bdot(a, b, kind, passes):
    return _bdot_raw(a, b, kind, passes)


def _bdot_fwd(a, b, kind, passes):
    return _bdot_raw(a, b, kind, passes), (a, b)


def _bdot_bwd(kind, passes, res, g):
    a, b = res
    if kind == "nn":
        return _bdot_raw(g, b, "nt", passes), _bdot_raw(a, g, "tn", passes)
    if kind == "nt":
        return _bdot_raw(g, b, "nn", passes), _bdot_raw(g, a, "tn", passes)
    return _bdot_raw(b, g, "nt", passes), _bdot_raw(a, g, "nn", passes)


_bdot.defvjp(_bdot_fwd, _bdot_bwd)


def _solve_powers(m):
    powers = [m]
    for _ in range(max(1, (m.shape[1] - 1).bit_length()) - 1):
        powers.append(_bdot_raw(powers[-1], powers[-1], "nn", 1))
    return powers


def _solve_fwd(m, rhs):
    powers = _solve_powers(m)
    x = rhs
    for p in powers:
        x = x + _bdot_raw(p, x, "nn", 1)
    return x, (powers, x)


def _solve_bwd(res, g):
    powers, x = res
    for p in powers:
        g = g + _bdot_raw(p, g, "tn", 1)
    return _bdot_raw(g, x, "nt", 1), g


@jax.custom_vjp
def _unit_lower_solve(m, rhs):
    return _solve_fwd(m, rhs)[0]


_unit_lower_solve.defvjp(_solve_fwd, _solve_bwd)


def _wkv_chunk(S0, r, lw, k, v, kap, a):
    G, C, N = r.shape
    row = lax.broadcasted_iota(jnp.int32, (C, C), 0)
    col = lax.broadcasted_iota(jnp.int32, (C, C), 1)
    incl = (col <= row).astype(F32)
    strict = (col < row).astype(F32)
    cum = _running_sum(lw)
    e_pos = jnp.exp(cum)
    e_neg = jnp.exp(-cum)
    al = -kap * jnp.exp(cum - lw)
    be = kap * a * e_neg
    kt = k * e_neg
    rt = r * e_pos
    bk = jnp.concatenate([be, kt], axis=1)
    mask = jnp.concatenate([jnp.concatenate([strict, strict], axis=1), jnp.concatenate([incl, incl], axis=1)], axis=0)
    m_all = _bdot(jnp.concatenate([al, rt], axis=1), bk, "nt", 3) * mask
    m_ab, m_ak = m_all[:, :C, :C], m_all[:, :C, C:]
    m_rb, m_rk = m_all[:, C:, :C], m_all[:, C:, C:]
    S0t = jnp.swapaxes(S0, 1, 2)
    sa = _unit_lower_solve(m_ab, _bdot(jnp.concatenate([al, m_ak], axis=2), jnp.concatenate([S0t, v], axis=1), "nn", 3))
    y =_bdot(jnp.concatenate([rt, m_rb, m_rk], axis=2), jnp.concatenate([S0t, sa, v], axis=1), "nn", 3)
    S1 = (S0 + _bdot(jnp.concatenate([sa, v], axis=1), bk, "tn", 1)) * e_pos[:, C - 1:C, :]
    return y, S1


def _split_heads(x):
    return jnp.stack([x[:, h * HEAD_DIM:(h + 1) * HEAD_DIM] for h in range(x.shape[1] // HEAD_DIM)], axis=0)


def _merge_heads(x):
    return jnp.concatenate([x[h] for h in range(x.shape[0])], axis=1)


def _seq_heads(ref):
    return jnp.concatenate([_split_heads(ref[s]) for s in range(ref.shape[0])], axis=0)


def _store_seq_heads(ref, x):
    heads = x.shape[0] // ref.shape[0]
    for s in range(ref.shape[0]):
        ref[s] = _merge_heads(x[s * heads:(s + 1) * heads])


def _wkv_fwd(r, lw, k, v, kap, a, B, S):
    C, H, N = WKV_CHUNK, RW_WIDTH // HEAD_DIM, HEAD_DIM
    nc = S // C
    Q = min(WKV_SEQS, B)

    def body(r_ref, lw_ref, k_ref, v_ref, kap_ref, a_ref, y_ref, st_ref, s_scr):
        @pl.when(pl.program_id(1) == 0)
        def _():
            s_scr[...] = jnp.zeros_like(s_scr)

        S0 = s_scr[...]
        for s in range(Q):
            st_ref[s, 0] = S0[s * H:(s + 1) * H]
        args = [_seq_heads(ref) for ref in (r_ref, lw_ref, k_ref, v_ref, kap_ref, a_ref)]
        y, S1 = _wkv_chunk(S0, *args)
        s_scr[...] = S1
        _store_seq_heads(y_ref, y)

    row_spec = pl.BlockSpec((Q, C, RW_WIDTH), lambda b, c: (b, c, 0))
    seqs = lambda t: t.reshape(B, S, RW_WIDTH)
    y, states = pl.pallas_call(
        body, name="wkv_fwd",
        grid=(B // Q, nc),
        in_specs=[row_spec] * 6,
        out_specs=[row_spec, pl.BlockSpec((Q, 1, H, N, N), lambda b, c: (b, c, 0, 0, 0))],
        out_shape=[jax.ShapeDtypeStruct((B, S, RW_WIDTH), F32), jax.ShapeDtypeStruct((B, nc, H, N, N), F32)],
        scratch_shapes=[pltpu.VMEM((Q * H, N, N), F32)],
        compiler_params=_params(("arbitrary", "arbitrary")),
    )(*map(seqs, (r, lw, k, v, kap, a)))
    return y.reshape(B * S, RW_WIDTH), states


def _wkv_bwd(r, lw, k, v, kap, a, states, dy, B, S):
    C, H, N = WKV_CHUNK, RW_WIDTH // HEAD_DIM, HEAD_DIM
    nc = S // C
    Q = min(WKV_SEQS, B)

    def body(r_ref, lw_ref, k_ref, v_ref, kap_ref, a_ref, st_ref, dy_ref,
             dr_ref, dlw_ref, dk_ref, dv_ref, dkap_ref, da_ref, ds_scr):
        @pl.when(pl.program_id(1) == 0)
        def _():
            ds_scr[...] = jnp.zeros_like(ds_scr)

        args = [_seq_heads(ref) for ref in (r_ref, lw_ref, k_ref, v_ref, kap_ref, a_ref)]
        S0 = jnp.concatenate([st_ref[s, 0] for s in range(Q)], axis=0)
        _, vjp = jax.vjp(_wkv_chunk, S0, *args)
        g = vjp((_seq_heads(dy_ref), ds_scr[...]))
        ds_scr[...] = g[0]
        for ref, gv in zip((dr_ref, dlw_ref, dk_ref, dv_ref, dkap_ref, da_ref), g[1:]):
            _store_seq_heads(ref, gv)

    row_spec = pl.BlockSpec((Q, C, RW_WIDTH), lambda b, c: (b, nc - 1 - c, 0))
    st_spec = pl.BlockSpec((Q, 1, H, N, N), lambda b, c: (b, nc - 1 - c, 0, 0, 0))
    seqs = lambda t: t.reshape(B, S, RW_WIDTH)
    res = pl.pallas_call(
        body, name="wkv_bwd",
        grid=(B // Q, nc),
        in_specs=[row_spec] * 6 + [st_spec, row_spec],
        out_specs=[row_spec] * 6,
        out_shape=[jax.ShapeDtypeStruct((B, S, RW_WIDTH), F32)] * 6,
        scratch_shapes=[pltpu.VMEM((Q * H, N, N), F32)],
        compiler_params=_params(("arbitrary", "arbitrary"), vmem=WKV_BWD_VMEM),
    )(*map(seqs, (r, lw, k, v, kap, a)), states, seqs(dy))
    return [t.reshape(B * S, RW_WIDTH) for t in res]


HBM = pl.BlockSpec(memory_space=pl.ANY)


def _place():
    return lax.axis_index("x"), lax.axis_index("y"), lax.axis_index("c")


def _other_chips(x, y):
    return [(1 - x, y), (x, 1 - y), (1 - x, 1 - y)]


def _all_gather_chips(shards):
    n = len(shards)

    def body(*refs):
        ins, outs = refs[:n], refs[n:2 * n]
        ici_send, ici_recv, d2d_send, d2d_recv, local = refs[2 * n:]
        x, y, c = _place()
        me = 2 * x + y
        sib = (x, y, 1 - c)
        chips = _other_chips(x, y)
        started, copies = [], []
        for w in range(n):
            cp = pltpu.make_async_copy(ins[w].at[c], outs[w].at[me, c], local.at[w])
            cp.start()
            copies.append(cp)
            for j, (px, py) in enumerate(chips):
                rd = pltpu.make_async_remote_copy(
                    src_ref=ins[w].at[c], dst_ref=outs[w].at[me, c], send_sem=ici_send.at[3 * w + j],
                    recv_sem=ici_recv.at[3 * w + j], device_id=(px, py, c), device_id_type=MESH)
                rd.start()
                started.append(rd)
            rd = pltpu.make_async_remote_copy(
                src_ref=ins[w].at[c], dst_ref=outs[w].at[me, c], send_sem=d2d_send.at[4 * w + 3],
                recv_sem=d2d_recv.at[4 * w + 3], device_id=sib, device_id_type=MESH)
            rd.start()
            started.append(rd)
        for w in range(n):
            for j, (px, py) in enumerate(chips):
                src = 2 * px + py
                pltpu.make_async_remote_copy(
                    src_ref=ins[w].at[c], dst_ref=outs[w].at[src, c], send_sem=ici_send.at[3 * w + j],
                    recv_sem=ici_recv.at[3 * w + j], device_id=(px, py, c), device_id_type=MESH).wait_recv()
                rd = pltpu.make_async_remote_copy(
                    src_ref=outs[w].at[src, c], dst_ref=outs[w].at[src, c], send_sem=d2d_send.at[4 * w + j],
                    recv_sem=d2d_recv.at[4 * w + j], device_id=sib, device_id_type=MESH)
                rd.start()
                started.append(rd)
        for w in range(n):
            for j, (px, py) in enumerate(chips):
                pltpu.make_async_remote_copy(
                    src_ref=ins[w].at[c], dst_ref=outs[w].at[2 * px + py, 1 - c], send_sem=d2d_send.at[4 * w + j],
                    recv_sem=d2d_recv.at[4 * w + j], device_id=sib, device_id_type=MESH).wait_recv()
            pltpu.make_async_remote_copy(
                src_ref=ins[w].at[c], dst_ref=outs[w].at[me, 1 - c], send_sem=d2d_send.at[4 * w + 3],
                recv_sem=d2d_recv.at[4 * w + 3], device_id=sib, device_id_type=MESH).wait_recv()
        for rd in started:
            rd.wait_send()
        for cp in copies:
            cp.wait()

    return pl.pallas_call(
        body, name="gather_weights",
        in_specs=[HBM] * n, out_specs=[HBM] * n,
        out_shape=[jax.ShapeDtypeStruct((N_CHIPS,) + s.shape, s.dtype) for s in shards],
        scratch_shapes=[pltpu.SemaphoreType.DMA((3 * n,)), pltpu.SemaphoreType.DMA((3 * n,)),
                        pltpu.SemaphoreType.DMA((4 * n,)), pltpu.SemaphoreType.DMA((4 * n,)),
                        pltpu.SemaphoreType.DMA((n,))],
        compiler_params=pltpu.CompilerParams(has_side_effects=True),
    )(*shards)


def _pair_split(grads):
    n = len(grads)

    def body(*refs):
        ins, theirs = refs[:n], refs[n:2 * n]
        send, recv = refs[2 * n:]
        x, y, c = _place()
        sib = (x, y, 1 - c)
        rds = []
        for w in range(n):
            rd = pltpu.make_async_remote_copy(
                src_ref=ins[w].at[:, 1 - c], dst_ref=theirs[w], send_sem=send.at[w], recv_sem=recv.at[w],
                device_id=sib, device_id_type=MESH)
            rd.start()
            rds.append(rd)
        for rd in rds:
            rd.wait_recv()
        for rd in rds:
            rd.wait_send()

    return pl.pallas_call(
        body, name="grad_pair_split",
        in_specs=[HBM] * n, out_specs=[HBM] * n,
        out_shape=[jax.ShapeDtypeStruct((g.shape[0],) + g.shape[2:], g.dtype) for g in grads],
        scratch_shapes=[pltpu.SemaphoreType.DMA((n,)), pltpu.SemaphoreType.DMA((n,))],
        compiler_params=pltpu.CompilerParams(has_side_effects=True),
    )(*grads)


def _chip_scatter(parts):
    n = len(parts)

    def body(*refs):
        ins, outs = refs[:n], refs[n:2 * n]
        send, recv = refs[2 * n:]
        x, y, c = _place()
        me = 2 * x + y
        rds = []
        for w in range(n):
            for j, (px, py) in enumerate(_other_chips(x, y)):
                s = 3 * w + j
                rd = pltpu.make_async_remote_copy(
                    src_ref=ins[w].at[2 * px + py], dst_ref=outs[w].at[j], send_sem=send.at[s], recv_sem=recv.at[s],
                    device_id=(px, py, c), device_id_type=MESH)
                rd.start()
                rds.append(rd)
        for w in range(n):
            for j, (px, py) in enumerate(_other_chips(x, y)):
                s = 3 * w + j
                pltpu.make_async_remote_copy(
                    src_ref=ins[w].at[me], dst_ref=outs[w].at[j], send_sem=send.at[s], recv_sem=recv.at[s],
                    device_id=(px, py, c), device_id_type=MESH).wait_recv()
        for rd in rds:
            rd.wait_send()

    return pl.pallas_call(
        body, name="grad_chip_scatter",
        in_specs=[HBM] * n, out_specs=[HBM] * n,
        out_shape=[jax.ShapeDtypeStruct((N_CHIPS - 1,) + p.shape[1:], p.dtype) for p in parts],
        scratch_shapes=[pltpu.SemaphoreType.DMA((3 * n,)), pltpu.SemaphoreType.DMA((3 * n,))],
        compiler_params=pltpu.CompilerParams(has_side_effects=True),
    )(*parts)


def _pair_join(bufs):
    n = len(bufs)

    def body(*refs):
        ins, outs = refs[:n], refs[n:2 * n]
        send, recv = refs[2 * n:]
        x, y, c = _place()
        sib = (x, y, 1 - c)
        rds = []
        for w in range(n):
            rd = pltpu.make_async_remote_copy(
                src_ref=ins[w].at[c], dst_ref=outs[w].at[c], send_sem=send.at[w], recv_sem=recv.at[w],
                device_id=sib, device_id_type=MESH)
            rd.start()
            rds.append(rd)
        for w in range(n):
            pltpu.make_async_remote_copy(
                src_ref=ins[w].at[c], dst_ref=outs[w].at[1 - c], send_sem=send.at[w], recv_sem=recv.at[w],
                device_id=sib, device_id_type=MESH).wait_recv()
        for rd in rds:
            rd.wait_send()

    return pl.pallas_call(
        body, name="grad_pair_join",
        in_specs=[HBM] * n, out_specs=[HBM] * n,
        out_shape=[jax.ShapeDtypeStruct(b.shape, b.dtype) for b in bufs],
        input_output_aliases={w: w for w in range(n)},
        scratch_shapes=[pltpu.SemaphoreType.DMA((n,)), pltpu.SemaphoreType.DMA((n,))],
        compiler_params=pltpu.CompilerParams(has_side_effects=True),
    )(*bufs)


def _all_reduce_small(packed):
    R = packed.shape[0]

    def body(x_ref, o_ref, buf, send, recv):
        x, y, c = _place()
        me = 4 * x + 2 * y + c
        buf[me] = x_ref[...]
        rds = []
        for rel in range(1, N_DEV):
            fx, fy, fc = (rel >> 2) & 1, (rel >> 1) & 1, rel & 1
            peer = (1 - x if fx else x, 1 - y if fy else y, 1 - c if fc else c)
            rd = pltpu.make_async_remote_copy(
                src_ref=x_ref, dst_ref=buf.at[me], send_sem=send.at[rel - 1], recv_sem=recv.at[rel - 1],
                device_id=peer, device_id_type=MESH)
            rd.start()
            rds.append((rd, peer))
        for rel in range(1, N_DEV):
            rd, (px, py, pc) = rds[rel - 1]
            pltpu.make_async_remote_copy(
                src_ref=x_ref, dst_ref=buf.at[4 * px + 2 * py + pc], send_sem=send.at[rel - 1], recv_sem=recv.at[rel - 1],
                device_id=(px, py, pc), device_id_type=MESH).wait_recv()
        for rd, _ in rds:
            rd.wait_send()
        total = buf[0]
        for d in range(1, N_DEV):
            total = total + buf[d]
        o_ref[...] = total

    return pl.pallas_call(
        body, name="all_reduce_small",
        in_specs=[pl.BlockSpec(memory_space=pltpu.VMEM)],
        out_specs=pl.BlockSpec(memory_space=pltpu.VMEM),
        out_shape=jax.ShapeDtypeStruct(packed.shape, F32),
        scratch_shapes=[pltpu.VMEM((N_DEV, R, LANES), F32), pltpu.SemaphoreType.DMA((N_DEV - 1,)),
                        pltpu.SemaphoreType.DMA((N_DEV - 1,))],
        compiler_params=pltpu.CompilerParams(has_side_effects=True),
    )(packed)


def _pair_sum(name, split, theirs, core):
    n_chip, _, Rh, C = split.shape
    tile = _div_tile(Rh, 256, 2 * SUBLANES)
    nt = Rh // tile

    def body(core_ref, a_ref, b_ref, o_ref):
        o_ref[...] = (a_ref[...] + b_ref[...]).astype(o_ref.dtype)

    return pl.pallas_call(
        body, name=name,
        grid_spec=pltpu.PrefetchScalarGridSpec(
            num_scalar_prefetch=1,
            grid=(n_chip, nt),
            in_specs=[pl.BlockSpec((None, None, tile, C), lambda j, i, core_ref: (j, core_ref[0], i, 0)),
                      pl.BlockSpec((None, tile, C), lambda j, i, core_ref: (j, i, 0))],
            out_specs=pl.BlockSpec((None, tile, C), lambda j, i, core_ref: (j, i, 0)),
        ),
        out_shape=jax.ShapeDtypeStruct((n_chip, Rh, C), BF16),
        compiler_params=_params(("parallel", "parallel")),
    )(core, split, theirs)


def _chip_sum(name, own, landed, core):
    n_in, Rh, C = landed.shape
    tile = _div_tile(Rh, 256, 2 * SUBLANES)

    def body(core_ref, *refs):
        total = refs[0][...].astype(F32)
        for ref in refs[1:n_in + 1]:
            total = total + ref[...].astype(F32)
        refs[n_in + 1][...] = total

    slot = lambda j: pl.BlockSpec((None, tile, C), lambda i, core_ref: (j, i, 0))
    return pl.pallas_call(
        body, name=name,
        grid_spec=pltpu.PrefetchScalarGridSpec(
            num_scalar_prefetch=1,
            grid=(Rh // tile,),
            in_specs=[pl.BlockSpec((None, tile, C), lambda i, core_ref: (core_ref[1], i, 0))]
                     + [slot(j) for j in range(n_in)],
            out_specs=pl.BlockSpec((None, tile, C), lambda i, core_ref: (core_ref[0], i, 0)),
        ),
        out_shape=jax.ShapeDtypeStruct((2, Rh, C), F32),
        compiler_params=_params(("parallel",)),
    )(core, own, *([landed] * n_in))


def _adamw(name, w, g, m, v):
    R, C = w.shape
    tile = _div_tile(R, 256, SUBLANES)
    c1 = 1.0 / (1.0 - ADAM_B1 ** ADAM_STEP)
    c2 = 1.0 / (1.0 - ADAM_B2 ** ADAM_STEP)

    def body(w_ref, g_ref, m_ref, v_ref, d_ref, nm_ref, nv_ref):
        g_ = g_ref[...]
        nm = ADAM_B1 * m_ref[...] + (1.0 - ADAM_B1) * g_
        nv = ADAM_B2 * v_ref[...] + (1.0 - ADAM_B2) * (g_ * g_)
        d_ref[...] = -ADAM_LR * ((nm * c1) / (jnp.sqrt(nv * c2) + ADAM_EPS) + ADAM_WD * w_ref[...])
        nm_ref[...] = nm
        nv_ref[...] = nv

    spec = pl.BlockSpec((tile, C), lambda i: (i, 0))
    return pl.pallas_call(
        body, name=name,
        grid=(R // tile,),
        in_specs=[spec] * 4, out_specs=[spec] * 3,
        out_shape=[jax.ShapeDtypeStruct((R, C), F32)] * 3,
        compiler_params=_params(("parallel",)),
    )(w, g, m, v)


SMALL =["norm_mix_pre", "b_gate", "mu_rw", "w0", "a0", "k_k", "k_a", "r_k", "lnx_w", "lnx_b",
         "norm_mix_post", "norm_ffn_pre", "norm_ffn_post"]
BIG = ["w_in", "w_up", "a_up", "g_up", "w_sb_out", "w_rw_out", "w_o", "w_ffn_gate", "w_ffn_up", "w_ffn_down"]
ROW_SHARDED = ("w_o", "w_ffn_down")
ORDER = ["norm_mix_pre", "w_in", "b_gate", "mu_rw", "w0", "w_up", "a0", "a_up", "g_up", "k_k", "k_a", "r_k",
         "lnx_w", "lnx_b", "w_sb_out", "w_rw_out", "w_o", "norm_mix_post", "norm_ffn_pre", "w_ffn_gate",
         "w_ffn_up", "w_ffn_down", "norm_ffn_post"]


def _pack_small(vals, extra_rows=0):
    rows = jnp.concatenate([vals[n].reshape(-1, LANES) for n in SMALL], axis=0)
    pad = (-(rows.shape[0] + extra_rows)) % SUBLANES + extra_rows
    return jnp.pad(rows, ((0, pad), (0, 0)))


def _unpack_small(packed, shapes):
    out, r = {}, 0
    for n in SMALL:
        size = 1
        for s in shapes[n]:
            size *= s
        out[n] = packed[r:r + size // LANES].reshape(shapes[n])
        r += size // LANES
    return out


def kernel(x, norm_mix_pre, w_in, b_gate, mu_rw, w0, w_up, a0, a_up, g_up, k_k, k_a, r_k, lnx_w, lnx_b, w_sb_out, w_rw_out, w_o, norm_mix_post, norm_ffn_pre, w_ffn_gate, w_ffn_up, w_ffn_down, norm_ffn_post, loss_target, m_norm_mix_pre, m_w_in, m_b_gate, m_mu_rw, m_w0, m_w_up, m_a0, m_a_up, m_g_up, m_k_k, m_k_a, m_r_k, m_lnx_w, m_lnx_b, m_w_sb_out, m_w_rw_out, m_w_o, m_norm_mix_post, m_norm_ffn_pre, m_w_ffn_gate, m_w_ffn_up, m_w_ffn_down, m_norm_ffn_post, v_norm_mix_pre, v_w_in, v_b_gate, v_mu_rw, v_w0, v_w_up, v_a0, v_a_up, v_g_up, v_k_k, v_k_a, v_r_k, v_lnx_w, v_lnx_b, v_w_sb_out, v_w_rw_out, v_w_o, v_norm_mix_post, v_norm_ffn_pre, v_w_ffn_gate, v_w_ffn_up, v_w_ffn_down, v_norm_ffn_post):
    W = dict(norm_mix_pre=norm_mix_pre, w_in=w_in, b_gate=b_gate, mu_rw=mu_rw, w0=w0, w_up=w_up, a0=a0, a_up=a_up,
             g_up=g_up, k_k=k_k, k_a=k_a, r_k=r_k, lnx_w=lnx_w, lnx_b=lnx_b, w_sb_out=w_sb_out, w_rw_out=w_rw_out,
             w_o=w_o, norm_mix_post=norm_mix_post, norm_ffn_pre=norm_ffn_pre, w_ffn_gate=w_ffn_gate,
             w_ffn_up=w_ffn_up, w_ffn_down=w_ffn_down, norm_ffn_post=norm_ffn_post)
    Mo = dict(norm_mix_pre=m_norm_mix_pre, w_in=m_w_in, b_gate=m_b_gate, mu_rw=m_mu_rw, w0=m_w0, w_up=m_w_up, a0=m_a0,
              a_up=m_a_up, g_up=m_g_up, k_k=m_k_k, k_a=m_k_a, r_k=m_r_k, lnx_w=m_lnx_w, lnx_b=m_lnx_b,
              w_sb_out=m_w_sb_out, w_rw_out=m_w_rw_out, w_o=m_w_o, norm_mix_post=m_norm_mix_post,
              norm_ffn_pre=m_norm_ffn_pre, w_ffn_gate=m_w_ffn_gate, w_ffn_up=m_w_ffn_up, w_ffn_down=m_w_ffn_down,
              norm_ffn_post=m_norm_ffn_post)
    Vo = dict(norm_mix_pre=v_norm_mix_pre, w_in=v_w_in, b_gate=v_b_gate, mu_rw=v_mu_rw, w0=v_w0, w_up=v_w_up, a0=v_a0,
              a_up=v_a_up, g_up=v_g_up, k_k=v_k_k, k_a=v_k_a, r_k=v_r_k, lnx_w=v_lnx_w, lnx_b=v_lnx_b,
              w_sb_out=v_w_sb_out, w_rw_out=v_w_rw_out, w_o=v_w_o, norm_mix_post=v_norm_mix_post,
              norm_ffn_pre=v_norm_ffn_pre, w_ffn_gate=v_w_ffn_gate, w_ffn_up=v_w_ffn_up, w_ffn_down=v_w_ffn_down,
              norm_ffn_post=v_norm_ffn_post)
    shapes = {n: W[n].shape for n in ORDER}
    B, S, D = x.shape
    T = B * S
    x2 = x.reshape(T, D)
    tgt = loss_target.reshape(T, D)
    vec = {n: W[n].reshape(1, -1) for n in SMALL}

    work = lambda t, n: t[0] if n in ROW_SHARDED else jnp.swapaxes(t[0], 0, 1)
    halved = [work(W[n], n).astype(BF16) for n in BIG]
    halved = [h.reshape(2, h.shape[0] // 2, h.shape[1]) for h in halved]
    full = {n: gth.reshape(-1, gth.shape[3]) for n, gth in zip(BIG, _all_gather_chips(halved))}
    w_in_t = full["w_in"]
    w_sb_t, w_rw_t, w_gt_t = w_in_t[:SB_COLS], w_in_t[SB_COLS:SB_COLS + RW_COLS], w_in_t[SB_COLS + RW_COLS:]
    lora_rows = {"w_up": 0, "a_up": 64, "g_up": 128}
    lora = {n: jnp.pad(full[n].T, ((r0, LORA_COLS - r0 - full[n].shape[1]), (0, 0))) for n, r0 in lora_rows.items()}
    mu = vec["mu_rw"]
    mu_parts = [mu[:, :512], mu[:, 512:1024], mu[:, 1024:1536], mu[:, 1536:]]
    b1, b2 = vec["b_gate"][:, :D], vec["b_gate"][:, D:]

    (h1,) = _rowwise("norm_mix_pre", _f_norm, [(x2, D, 0)], [vec["norm_mix_pre"]], [(D, BF16)], tile=512)
    p_sb = _mm("proj_sb", h1, w_sb_t, tb=True, out_dtype=BF16)
    p_rw = _mm("proj_rw", h1, w_rw_t, tb=True)
    p_gt = _mm("proj_gate", h1, w_gt_t, tb=True, out_dtype=BF16)
    o_sb = _attn_fwd(p_sb, B, S)
    pre_params = mu_parts + [vec["w0"], lora["w_up"], vec["a0"], lora["a_up"], lora["g_up"], vec["k_k"], vec["k_a"]]
    r_, lw_, k2_, v_, kap_, a_, g_ = _rw_pre(p_rw, pre_params, S, tile=256)
    y_wkv, states = _wkv_fwd(r_, lw_, k2_, v_, kap_, a_, B, S)
    post_rows = [(y_wkv, 512, 0), (r_, 512, 0), (k2_, 512, 0), (v_, 512, 0), (g_, 512, 0)]
    post_params = [vec["lnx_w"], vec["lnx_b"], vec["r_k"]]
    (o_rw,) = _rowwise("rw_post", _f_rwpost, post_rows, post_params, [(512, BF16)], tile=512)
    m1 = _mm("mix_sb_out", o_sb, full["w_sb_out"], tb=True, out_dtype=BF16)
    m2 = _mm("mix_rw_out", o_rw, full["w_rw_out"], tb=True, out_dtype=BF16)
    merge_rows = [(p_gt, D, 0), (p_gt, D, 1), (m1, D, 0), (m2, D, 0)]
    (merged,) = _rowwise("merge", _f_merge, merge_rows, [b1, b2], [(D, BF16)], tile=512)
    u = _mm("mix_out", merged, full["w_o"])
    post1_params = [vec["norm_mix_post"], vec["norm_ffn_pre"]]
    x1, h2 = _rowwise("post_mix", _f_post1, [(x2, D, 0), (u, D, 0)], post1_params, [(D, F32), (D, BF16)], tile=512)
    ag, au, sw = _mm_fused("ffn_in", [h2, h2], [full["w_ffn_gate"], full["w_ffn_up"]], [BF16] * 3, tb=True,
                           epilogue=lambda gu, _: (gu[0], gu[1], _f_swiglu(*gu)[0]))
    f = _mm("ffn_down", sw, full["w_ffn_down"])
    loss_part, dx1, df, dg4 = _loss_head(x1, f, tgt, vec["norm_ffn_post"], tile=512)

    gbig, gsmall = {}, {"norm_ffn_post": dg4}
    gbig["w_ffn_down"] = _mm("g_ffn_down", sw, df, ta=True)

    def swiglu_back(dsw, gu):
        return jax.vjp(_f_swiglu, *gu)[1]((dsw[0],))

    dag, dau = _mm_fused("ffn_back", [df], [full["w_ffn_down"]], [BF16] * 2, tb=True, extras=[ag, au],
                         epilogue=swiglu_back)
    (dh2,) = _mm_fused("d_h2", [dag, dau], [full["w_ffn_gate"], full["w_ffn_up"]], [F32], add=True)
    gbig["w_ffn_gate"] = _mm("g_ffn_gate", dag, h2, ta=True)
    gbig["w_ffn_up"] = _mm("g_ffn_up", dau, h2, ta=True)
    (dx_res, du), (dg2, dg3) = _rowwise_vjp("post_mix_bwd", _f_post1, [(x2, D, 0), (u, D, 0)], post1_params,
                                            [[dx1], [dh2]], [True, True], [True, True], bf16_rows=(1,))
    gsmall["norm_mix_post"], gsmall["norm_ffn_pre"] = dg2, dg3
    dmerged = _mm("d_merged", du, full["w_o"], tb=True, out_dtype=BF16)
    gbig["w_o"] = _mm("g_w_o", merged, du, ta=True)
    (dpg1, dpg2, dm1, dm2), (db1, db2) = _rowwise_vjp("merge_bwd", _f_merge, merge_rows, [b1, b2], [[dmerged]],
                                                      [True] * 4, [True, True], bf16_rows=(0, 1, 2, 3))
    gsmall["b_gate"] = jnp.concatenate([db1, db2], axis=1)
    do_sb = _mm("d_o_sb", dm1, full["w_sb_out"])
    do_rw = _mm("d_o_rw", dm2, full["w_rw_out"])
    gbig["w_sb_out"] = _mm("g_sb_out", dm1, o_sb, ta=True)
    gbig["w_rw_out"] = _mm("g_rw_out", dm2, o_rw, ta=True)
    (dy_wkv, dr_a, dk2_a, dv_a, dg_), (dlnx_w, dlnx_b, dr_k) = _rowwise_vjp(
        "rw_post_bwd", _f_rwpost, post_rows, post_params, [[do_rw]], [True] * 5, [True] * 3)
    gsmall["lnx_w"], gsmall["lnx_b"], gsmall["r_k"] = dlnx_w, dlnx_b, dr_k
    dr_b, dlw, dk2_b, dv_b, dkap, da = _wkv_bwd(r_, lw_, k2_, v_, kap_, a_, states, dy_wkv, B, S)
    pre_cts = [[dr_a, dr_b], [dlw], [dk2_a, dk2_b], [dv_a, dv_b], [dkap], [da], [dg_]]
    dp_rw, dpre_params = _rw_pre_bwd(p_rw, pre_params, pre_cts, S)
    gsmall["mu_rw"] = jnp.concatenate(dpre_params[:4], axis=1)
    gsmall["w0"], gsmall["a0"], gsmall["k_k"], gsmall["k_a"] = dpre_params[4], dpre_params[6], dpre_params[9], dpre_params[10]
    glora = {"w_up": dpre_params[5][0:64].T, "a_up": dpre_params[7][64:128].T, "g_up": dpre_params[8][128:256].T}
    dq, dk, dv = _attn_bwd(p_sb, o_sb, do_sb, B, S)
    (dh1,) = _mm_fused("d_h1_sb", [dq, dk, dv], [w_sb_t[:512], w_sb_t[512:1024], w_sb_t[1024:]], [F32], add=True)
    dh1 = _mm("d_h1_rw", dp_rw, w_rw_t, acc=dh1)
    (dh1,) = _mm_fused("d_h1_gate", [dpg1, dpg2], [w_gt_t[:D], w_gt_t[D:]], [F32], add=True,
                       extras=[dh1], epilogue=lambda p, e: (p[0] + e[0],))
    gbig["w_in"] = jnp.concatenate(
        [_mm("g_in_" + tag, d, h1, ta=True)
         for tag, d in (("q", dq), ("k", dk), ("v", dv), ("rw", dp_rw), ("g1", dpg1), ("g2", dpg2))], axis=0)
    (grad_x2,), (dg1,) = _rowwise_vjp("norm_mix_pre_bwd", _f_norm, [(x2, D, 0)], [vec["norm_mix_pre"]], [[dh1]],
                                      [True], [True], add_to={0: dx_res}, tile=512)
    gsmall["norm_mix_pre"] = dg1
    gbig.update(glora)

    split = [gbig[n].reshape(N_CHIPS, 2, gbig[n].shape[0] // (2 * N_CHIPS), gbig[n].shape[1]) for n in BIG]
    core = jnp.stack([lax.axis_index("c"), 2 * lax.axis_index("x") + lax.axis_index("y")]).astype(jnp.int32)
    theirs = _pair_split(split)
    chip_sums = [_pair_sum("pair_sum_" + n, a, b, core) for n, a, b in zip(BIG, split, theirs)]
    landed = _chip_scatter(chip_sums)
    joined = _pair_join([_chip_sum("chip_sum_" + n, own, got, core) for n, own, got in zip(BIG, chip_sums, landed)])
    grads = {n: j.reshape(-1, j.shape[2]) for n, j in zip(BIG, joined)}

    small_local = _pack_small({n: gsmall[n] for n in SMALL}, extra_rows=1)
    loss_row = small_local.shape[0] - 1
    small_local = small_local.at[loss_row].set(loss_part[0])
    small_sum = _all_reduce_small(small_local)
    loss = small_sum[loss_row, 0]

    delta, new_m, new_v = {}, {}, {}
    unwork = lambda t, n: (t if n in ROW_SHARDED else jnp.swapaxes(t, 0, 1))[None]
    for n in BIG:
        d_, m_, v2_ = _adamw("adamw_" + n, work(W[n], n), grads[n], work(Mo[n], n), work(Vo[n], n))
        delta[n], new_m[n], new_v[n], grads[n] = (unwork(t, n) for t in (d_, m_, v2_, grads[n]))
    pk = lambda src: _pack_small({n: src[n] for n in SMALL}, extra_rows=1)
    d_s, m_s, v_s = _adamw("adamw_small", pk(W), small_sum.at[loss_row].set(0.0), pk(Mo), pk(Vo))
    for dst, packed in ((grads, small_sum), (delta, d_s), (new_m, m_s), (new_v, v_s)):
        dst.update(_unpack_small(packed, shapes))

    return (loss, grad_x2.reshape(B, S, D), *[grads[n] for n in ORDER], *[delta[n] for n in ORDER],
            *[new_m[n] for n in ORDER], *[new_v[n] for n in ORDER])
```
